```python
import jax, jax.numpy as jnp
from jax import lax
import numpy as np

D_MODEL = 1024
BATCH = 8
SEQ = 16384
DEPTH = 1

PLE_DIM = 256
EPS = 1e-6
RET_HEADS = 4
RET_QK_DIM = 256
RET_V_DIM = 512
RET_CHUNK = 128
RET_ROPE_BASE = 10000.0
DIL_GROUPS = ((128, 1), (512, 4), (2048, 16))
DIL_SLOTS = 8
DIL_HEAD_DIM = 64
DIL_BLOCK = 128
ROPE_THETA = 500000.0
ROPE_DIM = DIL_HEAD_DIM // 4
D_FF = 4 * D_MODEL

RET_QK_W = RET_HEADS * RET_QK_DIM
RET_V_W = RET_HEADS * RET_V_DIM
DIL_HEADS = len(DIL_GROUPS) * DIL_SLOTS
DIL_W = DIL_HEADS * DIL_HEAD_DIM
DIL_OUT_W = DIL_SLOTS * DIL_HEAD_DIM
SPLITS = (RET_QK_W, RET_QK_W, RET_V_W, RET_V_W, DIL_W, DIL_W, DIL_W, D_MODEL, D_MODEL)
IN_W = RET_QK_W * 2 + RET_V_W * 2 + DIL_W * 3 + D_MODEL * 2

kernel_name = "hybrid_retention_dilated_attn_block"


def rmsnorm(x, g=None):
    xf = x.astype(jnp.float32)
    y = xf * lax.rsqrt(jnp.mean(xf * xf, axis=-1, keepdims=True) + EPS)
    if g is not None:
        y = y * g.astype(jnp.float32)
    return y.astype(x.dtype)


def rope(x, cos, sin, rot_dim):
    half = rot_dim // 2
    x1 = x[..., :half]
    x2 = x[..., half:rot_dim]
    return jnp.concatenate([x1 * cos - x2 * sin, x2 * cos + x1 * sin, x[..., rot_dim:]], axis=-1)


def retention(q, k, v, pos):
    B, S, H, dk = q.shape
    dv = v.shape[-1]
    f32 = jnp.float32
    half = dk // 2
    inv_freq = 1.0 / (RET_ROPE_BASE ** jnp.linspace(0.0, 1.0, half, dtype=f32))
    ang = pos.astype(f32)[:, :, None, None] * inv_freq
    cos, sin = jnp.cos(ang), jnp.sin(ang)
    q = rope(q.astype(f32), cos, sin, dk)
    k = rope(k.astype(f32), cos, sin, dk) * (dk ** -0.5)
    v = v.astype(f32)
    C = RET_CHUNK
    Sp = -(-S // C) * C
    padw = ((0, 0), (0, Sp - S), (0, 0), (0, 0))
    nc = Sp // C
    qc = jnp.pad(q, padw).reshape(B, nc, C, H, dk)
    kc = jnp.pad(k, padw).reshape(B, nc, C, H, dk)
    vc = jnp.pad(v, padw).reshape(B, nc, C, H, dv)
    log_gamma = jnp.log1p(-(2.0 ** (-5.0 - jnp.arange(H, dtype=f32))))
    idx = jnp.arange(C, dtype=f32)
    diff = idx[:, None] - idx[None, :]
    decay = jnp.where(diff[None] >= 0, jnp.exp(jnp.maximum(diff, 0.0)[None] * log_gamma[:, None, None]), 0.0)
    scores = jnp.einsum('bnihd,bnjhd->bnhij', qc, kc) * decay[None, None]
    inner = jnp.einsum('bnhij,bnjhe->bnihe', scores, vc)
    q_dec = jnp.exp((idx + 1.0)[:, None] * log_gamma[None, :])
    k_dec = jnp.exp((C - 1.0 - idx)[:, None] * log_gamma[None, :])
    chunk_dec = jnp.exp(C * log_gamma)

    def step(R, inp):
        qn, kn, vn = inp
        cross = jnp.einsum('bihd,bhde->bihe', qn * q_dec[None, :, :, None], R)
        R = R * chunk_dec[None, :, None, None] + jnp.einsum('bjhd,bjhe->bhde', kn * k_dec[None, :, :, None], vn)
        return R, cross

    R0 = jnp.zeros((B, H, dk, dv), f32)
    _, cross = lax.scan(step, R0, (jnp.moveaxis(qc, 1, 0), jnp.moveaxis(kc, 1, 0), jnp.moveaxis(vc, 1, 0)))
    y = inner + jnp.moveaxis(cross, 0, 1)
    return y.reshape(B, Sp, H, dv)[:, :S]


def dilated_group(q, k, v, window, dilation):
    B, S, Hg, hd = q.shape
    f32 = jnp.float32
    band = window // dilation
    QB = DIL_BLOCK
    seg = dilation * QB
    Sp = -(-S // seg) * seg
    L = Sp // dilation
    nb = L // QB
    padw = ((0, 0), (0, Sp - S), (0, 0), (0, 0))

    def to_streams(t):
        t = jnp.pad(t.astype(f32), padw).reshape(B, L, dilation, Hg, hd)
        return t.transpose(0, 2, 1, 3, 4).reshape(B, dilation, nb, QB, Hg, hd)

    def with_prev(t):
        prev = jnp.pad(t[:, :, :-1], ((0, 0), (0, 0), (1, 0), (0, 0), (0, 0), (0, 0)))
        return jnp.concatenate([prev, t], axis=3)

    qs = to_streams(q)
    kk = with_prev(to_streams(k))
    vv = with_prev(to_streams(v))
    s = jnp.einsum('bcnqhe,bcnkhe->bcnhqk', qs, kk) * (hd ** -0.5)
    qi = jnp.arange(QB)[:, None]
    kj = jnp.arange(2 * QB)[None, :] - QB
    dist = qi - kj
    in_band = (dist >= 0) & (dist <= band)
    key_exists = (jnp.arange(nb)[:, None, None] * QB + kj[None]) >= 0
    mask = in_band[None] & key_exists
    s = jnp.where(mask[None, None, :, None], s, -1e30)
    lse = jax.nn.logsumexp(s, axis=-1)
    pr = jnp.exp(s - lse[..., None])
    o = jnp.einsum('bcnhqk,bcnkhe->bcnqhe', pr, vv)
    o = o.reshape(B, dilation, L, Hg, hd).transpose(0, 2, 1, 3, 4).reshape(B, Sp, Hg, hd)[:, :S]
    lse = lse.transpose(0, 1, 2, 4, 3).reshape(B, dilation, L, Hg).transpose(0, 2, 1, 3).reshape(B, Sp, Hg)[:, :S]
    return o, lse


def dilated_attention(q, k, v, pos):
    f32 = jnp.float32
    freqs = ROPE_THETA ** (-jnp.arange(0, ROPE_DIM, 2, dtype=f32) / ROPE_DIM)
    ang = pos.astype(f32)[:, :, None, None] * freqs
    cos, sin = jnp.cos(ang), jnp.sin(ang)
    q = rope(q.astype(f32), cos, sin, ROPE_DIM)
    k = rope(k.astype(f32), cos, sin, ROPE_DIM)
    outs, lses = [], []
    for g, (window, dilation) in enumerate(DIL_GROUPS):
        sl = slice(g * DIL_SLOTS, (g + 1) * DIL_SLOTS)
        o, l = dilated_group(q[:, :, sl], k[:, :, sl], v[:, :, sl], window, dilation)
        outs.append(o)
        lses.append(l)
    w = jax.nn.softmax(jnp.stack(lses, axis=0), axis=0)
    return jnp.sum(w[..., None] * jnp.stack(outs, axis=0), axis=0)


def _fwd_setup_inputs(seed: int = 0) -> dict:
    key = jax.random.key(seed)
    ks = jax.random.split(key, 24)
    f32 = jnp.float32
    nrm = lambda k, shape, fan_in: jax.random.normal(k, shape, f32) * (fan_in ** -0.5)
    gain = lambda k: 1.0 + 0.02 * jax.random.normal(k, (DEPTH, D_MODEL), f32)
    x = jax.random.normal(ks[0], (BATCH, SEQ, D_MODEL), f32)
    p = jax.random.normal(ks[1], (DEPTH, BATCH, SEQ, PLE_DIM), f32)
    positions = jnp.broadcast_to(jnp.arange(SEQ, dtype=jnp.int32)[None, :], (BATCH, SEQ))
    return {
        "x": x,
        "p": p,
        "positions": positions,
        "w_in": nrm(ks[2], (DEPTH, D_MODEL, IN_W), D_MODEL),
        "b_gate": 0.01 * jax.random.normal(ks[3], (DEPTH, 2, D_MODEL), f32),
        "w_ret_out": nrm(ks[4], (DEPTH, RET_V_W, D_MODEL), RET_V_W),
        "w_dil_out": nrm(ks[5], (DEPTH, DIL_OUT_W, D_MODEL), DIL_OUT_W),
        "w_o": nrm(ks[6], (DEPTH, D_MODEL, D_MODEL), D_MODEL),
        "g_pre_mix": gain(ks[7]),
        "g_post_mix": gain(ks[8]),
        "g_pre_mlp": gain(ks[9]),
        "g_post_mlp": gain(ks[10]),
        "w_up": nrm(ks[11], (DEPTH, D_MODEL, D_FF), D_MODEL),
        "w_down": nrm(ks[12], (DEPTH, D_FF, D_MODEL), D_FF),
        "g_pre_ple": gain(ks[13]),
        "w_ple_gate": nrm(ks[14], (DEPTH, D_MODEL, D_MODEL), D_MODEL),
        "b_ple_gate": 0.01 * jax.random.normal(ks[15], (DEPTH, D_MODEL), f32),
        "w_ple_in": nrm(ks[16], (DEPTH, PLE_DIM, D_MODEL), PLE_DIM),
        "g_post_ple": gain(ks[17]),
    }


def _fwd_reference(x, p, positions, w_in, b_gate, w_ret_out, w_dil_out, w_o, g_pre_mix, g_post_mix,
              g_pre_mlp, g_post_mlp, w_up, w_down, g_pre_ple, w_ple_gate, b_ple_gate, w_ple_in,
              g_post_ple):
    B, S, _ = x.shape
    split_points = [int(s) for s in np.cumsum(SPLITS)[:-1]]
    h = x
    for i in range(DEPTH):
        u = rmsnorm(h, g_pre_mix[i])
        proj = u @ w_in[i]
        rq, rk, rv, rg, aq, ak, av, gr, ga = jnp.split(proj, split_points, axis=-1)
        yr = retention(rq.reshape(B, S, RET_HEADS, RET_QK_DIM), rk.reshape(B, S, RET_HEADS, RET_QK_DIM),
                       rv.reshape(B, S, RET_HEADS, RET_V_DIM), positions)
        yr = rmsnorm(yr) * jax.nn.silu(rg.reshape(B, S, RET_HEADS, RET_V_DIM).astype(jnp.float32))
        ya_branch = yr.reshape(B, S, RET_V_W).astype(h.dtype) @ w_ret_out[i]
        ya = dilated_attention(aq.reshape(B, S, DIL_HEADS, DIL_HEAD_DIM), ak.reshape(B, S, DIL_HEADS, DIL_HEAD_DIM),
                               av.reshape(B, S, DIL_HEADS, DIL_HEAD_DIM), positions)
        yb_branch = ya.reshape(B, S, DIL_OUT_W).astype(h.dtype) @ w_dil_out[i]
        mixed = jax.nn.sigmoid(gr + b_gate[i, 0]) * ya_branch + jax.nn.sigmoid(ga + b_gate[i, 1]) * yb_branch
        h = h + rmsnorm(mixed @ w_o[i], g_post_mix[i])
        v2 = rmsnorm(h, g_pre_mlp[i])
        f = jnp.square(jax.nn.relu(v2 @ w_up[i])) @ w_down[i]
        h = h + rmsnorm(f, g_post_mlp[i])
        gate = jax.nn.sigmoid(rmsnorm(h, g_pre_ple[i]) @ w_ple_gate[i] + b_ple_gate[i])
        e = p[i].astype(h.dtype) @ w_ple_in[i]
        h = h + rmsnorm(gate * e, g_post_ple[i])
    return h


import jax as _jax
import jax.numpy as _jnp

TWIN_FORMAT = 'train_step'
FWD_PARAMS = ['x', 'p', 'positions', 'w_in', 'b_gate', 'w_ret_out', 'w_dil_out', 'w_o', 'g_pre_mix', 'g_post_mix', 'g_pre_mlp', 'g_post_mlp', 'w_up', 'w_down', 'g_pre_ple', 'w_ple_gate', 'b_ple_gate', 'w_ple_in', 'g_post_ple']
TWIN_WEIGHTS = ['w_in', 'b_gate', 'w_ret_out', 'w_dil_out', 'w_o', 'g_pre_mix', 'g_post_mix', 'g_pre_mlp', 'g_post_mlp', 'w_up', 'w_down', 'g_pre_ple', 'w_ple_gate', 'b_ple_gate', 'w_ple_in', 'g_post_ple']
TWIN_DIFF_INPUT = 'x'
TWIN_INPUTS = ['x', 'p', 'positions', 'w_in', 'b_gate', 'w_ret_out', 'w_dil_out', 'w_o', 'g_pre_mix', 'g_post_mix', 'g_pre_mlp', 'g_post_mlp', 'w_up', 'w_down', 'g_pre_ple', 'w_ple_gate', 'b_ple_gate', 'w_ple_in', 'g_post_ple', 'loss_target', 'm_w_in', 'm_b_gate', 'm_w_ret_out', 'm_w_dil_out', 'm_w_o', 'm_g_pre_mix', 'm_g_post_mix', 'm_g_pre_mlp', 'm_g_post_mlp', 'm_w_up', 'm_w_down', 'm_g_pre_ple', 'm_w_ple_gate', 'm_b_ple_gate', 'm_w_ple_in', 'm_g_post_ple', 'v_w_in', 'v_b_gate', 'v_w_ret_out', 'v_w_dil_out', 'v_w_o', 'v_g_pre_mix', 'v_g_post_mix', 'v_g_pre_mlp', 'v_g_post_mlp', 'v_w_up', 'v_w_down', 'v_g_pre_ple', 'v_w_ple_gate', 'v_b_ple_gate', 'v_w_ple_in', 'v_g_post_ple']
TWIN_OUTPUTS = ['loss', 'grad_x', 'grad_w_in', 'grad_b_gate', 'grad_w_ret_out', 'grad_w_dil_out', 'grad_w_o', 'grad_g_pre_mix', 'grad_g_post_mix', 'grad_g_pre_mlp', 'grad_g_post_mlp', 'grad_w_up', 'grad_w_down', 'grad_g_pre_ple', 'grad_w_ple_gate', 'grad_b_ple_gate', 'grad_w_ple_in', 'grad_g_post_ple', 'delta_w_in', 'delta_b_gate', 'delta_w_ret_out', 'delta_w_dil_out', 'delta_w_o', 'delta_g_pre_mix', 'delta_g_post_mix', 'delta_g_pre_mlp', 'delta_g_post_mlp', 'delta_w_up', 'delta_w_down', 'delta_g_pre_ple', 'delta_w_ple_gate', 'delta_b_ple_gate', 'delta_w_ple_in', 'delta_g_post_ple', 'new_m_w_in', 'new_m_b_gate', 'new_m_w_ret_out', 'new_m_w_dil_out', 'new_m_w_o', 'new_m_g_pre_mix', 'new_m_g_post_mix', 'new_m_g_pre_mlp', 'new_m_g_post_mlp', 'new_m_w_up', 'new_m_w_down', 'new_m_g_pre_ple', 'new_m_w_ple_gate', 'new_m_b_ple_gate', 'new_m_w_ple_in', 'new_m_g_post_ple', 'new_v_w_in', 'new_v_b_gate', 'new_v_w_ret_out', 'new_v_w_dil_out', 'new_v_w_o', 'new_v_g_pre_mix', 'new_v_g_post_mix', 'new_v_g_pre_mlp', 'new_v_g_post_mlp', 'new_v_w_up', 'new_v_w_down', 'new_v_g_pre_ple', 'new_v_w_ple_gate', 'new_v_b_ple_gate', 'new_v_w_ple_in', 'new_v_g_post_ple']
TWIN_LEAF_KINDS = {'loss': 'loss', 'grad_x': 'grad_x', 'grad_w_in': 'grad_w', 'grad_b_gate': 'grad_w', 'grad_w_ret_out': 'grad_w', 'grad_w_dil_out': 'grad_w', 'grad_w_o': 'grad_w', 'grad_g_pre_mix': 'grad_w', 'grad_g_post_mix': 'grad_w', 'grad_g_pre_mlp': 'grad_w', 'grad_g_post_mlp': 'grad_w', 'grad_w_up': 'grad_w', 'grad_w_down': 'grad_w', 'grad_g_pre_ple': 'grad_w', 'grad_w_ple_gate': 'grad_w', 'grad_b_ple_gate': 'grad_w', 'grad_w_ple_in': 'grad_w', 'grad_g_post_ple': 'grad_w', 'delta_w_in': 'delta_w', 'delta_b_gate': 'delta_w', 'delta_w_ret_out': 'delta_w', 'delta_w_dil_out': 'delta_w', 'delta_w_o': 'delta_w', 'delta_g_pre_mix': 'delta_w', 'delta_g_post_mix': 'delta_w', 'delta_g_pre_mlp': 'delta_w', 'delta_g_post_mlp': 'delta_w', 'delta_w_up': 'delta_w', 'delta_w_down': 'delta_w', 'delta_g_pre_ple': 'delta_w', 'delta_w_ple_gate': 'delta_w', 'delta_b_ple_gate': 'delta_w', 'delta_w_ple_in': 'delta_w', 'delta_g_post_ple': 'delta_w', 'new_m_w_in': 'new_m', 'new_m_b_gate': 'new_m', 'new_m_w_ret_out': 'new_m', 'new_m_w_dil_out': 'new_m', 'new_m_w_o': 'new_m', 'new_m_g_pre_mix': 'new_m', 'new_m_g_post_mix': 'new_m', 'new_m_g_pre_mlp': 'new_m', 'new_m_g_post_mlp': 'new_m', 'new_m_w_up': 'new_m', 'new_m_w_down': 'new_m', 'new_m_g_pre_ple': 'new_m', 'new_m_w_ple_gate': 'new_m', 'new_m_b_ple_gate': 'new_m', 'new_m_w_ple_in': 'new_m', 'new_m_g_post_ple': 'new_m', 'new_v_w_in': 'new_v', 'new_v_b_gate': 'new_v', 'new_v_w_ret_out': 'new_v', 'new_v_w_dil_out': 'new_v', 'new_v_w_o': 'new_v', 'new_v_g_pre_mix': 'new_v', 'new_v_g_post_mix': 'new_v', 'new_v_g_pre_mlp': 'new_v', 'new_v_g_post_mlp': 'new_v', 'new_v_w_up': 'new_v', 'new_v_w_down': 'new_v', 'new_v_g_pre_ple': 'new_v', 'new_v_w_ple_gate': 'new_v', 'new_v_b_ple_gate': 'new_v', 'new_v_w_ple_in': 'new_v', 'new_v_g_post_ple': 'new_v'}


def _forward(args):
    return _fwd_reference(*[args[k] for k in FWD_PARAMS])


def _output_shape():
    def fwd():
        inp = _fwd_setup_inputs(0)
        return _fwd_reference(*[inp[k] for k in FWD_PARAMS])
    out = _jax.eval_shape(fwd)
    return out.shape, out.dtype

N_MICROBATCH = 1
ADAM_LR = 0.001
ADAM_B1 = 0.9
ADAM_B2 = 0.999
ADAM_EPS = 1e-08
ADAM_WD = 0.01
ADAM_STEP = 10
PER_EXAMPLE_BATCH_AXIS = {'x': 0, 'p': 1, 'positions': 0, 'loss_target': 0}
SHARED_INPUTS = []
_WEIGHT_DTYPES = {'w_in': _jnp.float32, 'b_gate': _jnp.float32, 'w_ret_out': _jnp.float32, 'w_dil_out': _jnp.float32, 'w_o': _jnp.float32, 'g_pre_mix': _jnp.float32, 'g_post_mix': _jnp.float32, 'g_pre_mlp': _jnp.float32, 'g_post_mlp': _jnp.float32, 'w_up': _jnp.float32, 'w_down': _jnp.float32, 'g_pre_ple': _jnp.float32, 'w_ple_gate': _jnp.float32, 'b_ple_gate': _jnp.float32, 'w_ple_in': _jnp.float32, 'g_post_ple': _jnp.float32}
MOMENT_SCALE = {'w_in': 5.639527e-01, 'b_gate': 5.248933e-01, 'w_ret_out': 1.195989e+00, 'w_dil_out': 3.497466e-01, 'w_o': 1.701047e+00, 'g_pre_mix': 1.996736e+00, 'g_post_mix': 1.281462e+02, 'g_pre_mlp': 2.514245e+00, 'g_post_mlp': 1.308602e+02, 'w_up': 1.268519e+00, 'w_down': 2.660955e+00, 'g_pre_ple': 3.372380e-01, 'w_ple_gate': 3.250904e-01, 'b_ple_gate': 1.122461e+00, 'w_ple_in': 7.246152e-01, 'g_post_ple': 1.303256e+02}


def _to_microbatches(a, axis):
    t = _jnp.moveaxis(a, axis, 0)
    t = t.reshape((N_MICROBATCH, t.shape[0] // N_MICROBATCH) + t.shape[1:])
    return _jnp.moveaxis(t, 1, axis + 1)


def setup_inputs(seed: int = 0) -> dict:
    inp = _fwd_setup_inputs(seed)
    key = _jax.random.fold_in(_jax.random.key(seed), 7919)
    shape, _ = _output_shape()
    out = dict(inp)
    out["loss_target"] = _jax.random.normal(_jax.random.fold_in(key, 0), shape, _jnp.float32)
    for i, name in enumerate(TWIN_WEIGHTS):
        w = inp[name].astype(_jnp.float32)
        if MOMENT_SCALE is None:
            s = _jnp.sqrt(_jnp.mean(_jnp.square(w)) + 1e-30)
        else:
            s = MOMENT_SCALE[name]
        km, kv = _jax.random.split(_jax.random.fold_in(key, i + 1))
        out[name] = w
        out["m_" + name] = s * _jax.random.normal(km, w.shape, _jnp.float32)
        out["v_" + name] = (s * s) * _jax.random.uniform(kv, w.shape, _jnp.float32, 0.5, 1.5)
    if N_MICROBATCH > 1:
        for name, axis in PER_EXAMPLE_BATCH_AXIS.items():
            out[name] = _to_microbatches(out[name], axis)
    return {'x': out['x'], 'p': out['p'], 'positions': out['positions'], 'w_in': out['w_in'], 'b_gate': out['b_gate'], 'w_ret_out': out['w_ret_out'], 'w_dil_out': out['w_dil_out'], 'w_o': out['w_o'], 'g_pre_mix': out['g_pre_mix'], 'g_post_mix': out['g_post_mix'], 'g_pre_mlp': out['g_pre_mlp'], 'g_post_mlp': out['g_post_mlp'], 'w_up': out['w_up'], 'w_down': out['w_down'], 'g_pre_ple': out['g_pre_ple'], 'w_ple_gate': out['w_ple_gate'], 'b_ple_gate': out['b_ple_gate'], 'w_ple_in': out['w_ple_in'], 'g_post_ple': out['g_post_ple'], 'loss_target': out['loss_target'], 'm_w_in': out['m_w_in'], 'm_b_gate': out['m_b_gate'], 'm_w_ret_out': out['m_w_ret_out'], 'm_w_dil_out': out['m_w_dil_out'], 'm_w_o': out['m_w_o'], 'm_g_pre_mix': out['m_g_pre_mix'], 'm_g_post_mix': out['m_g_post_mix'], 'm_g_pre_mlp': out['m_g_pre_mlp'], 'm_g_post_mlp': out['m_g_post_mlp'], 'm_w_up': out['m_w_up'], 'm_w_down': out['m_w_down'], 'm_g_pre_ple': out['m_g_pre_ple'], 'm_w_ple_gate': out['m_w_ple_gate'], 'm_b_ple_gate': out['m_b_ple_gate'], 'm_w_ple_in': out['m_w_ple_in'], 'm_g_post_ple': out['m_g_post_ple'], 'v_w_in': out['v_w_in'], 'v_b_gate': out['v_b_gate'], 'v_w_ret_out': out['v_w_ret_out'], 'v_w_dil_out': out['v_w_dil_out'], 'v_w_o': out['v_w_o'], 'v_g_pre_mix': out['v_g_pre_mix'], 'v_g_post_mix': out['v_g_post_mix'], 'v_g_pre_mlp': out['v_g_pre_mlp'], 'v_g_post_mlp': out['v_g_post_mlp'], 'v_w_up': out['v_w_up'], 'v_w_down': out['v_w_down'], 'v_g_pre_ple': out['v_g_pre_ple'], 'v_w_ple_gate': out['v_w_ple_gate'], 'v_b_ple_gate': out['v_b_ple_gate'], 'v_w_ple_in': out['v_w_ple_in'], 'v_g_post_ple': out['v_g_post_ple']}


def _loss(weights, diff, rest, loss_target):
    with _jax.named_scope("forward"):
        args = {**rest, TWIN_DIFF_INPUT: diff, **{k: w.astype(_WEIGHT_DTYPES[k]) for k, w in weights.items()}}
        y = _forward(args)
    with _jax.named_scope("loss_head"):
        err = _jnp.square(y.astype(_jnp.float32) - loss_target)
        return 0.5 * _jnp.sum(_jnp.mean(err, axis=-1)) if err.ndim else 0.5 * err


def _adamw(w, g, m, v):
    m = ADAM_B1 * m + (1.0 - ADAM_B1) * g
    v = ADAM_B2 * v + (1.0 - ADAM_B2) * _jnp.square(g)
    m_hat = m / (1.0 - ADAM_B1 ** ADAM_STEP)
    v_hat = v / (1.0 - ADAM_B2 ** ADAM_STEP)
    delta = -ADAM_LR * (m_hat / (_jnp.sqrt(v_hat) + ADAM_EPS) + ADAM_WD * w)
    return delta, m, v


def reference(x, p, positions, w_in, b_gate, w_ret_out, w_dil_out, w_o, g_pre_mix, g_post_mix, g_pre_mlp, g_post_mlp, w_up, w_down, g_pre_ple, w_ple_gate, b_ple_gate, w_ple_in, g_post_ple, loss_target, m_w_in, m_b_gate, m_w_ret_out, m_w_dil_out, m_w_o, m_g_pre_mix, m_g_post_mix, m_g_pre_mlp, m_g_post_mlp, m_w_up, m_w_down, m_g_pre_ple, m_w_ple_gate, m_b_ple_gate, m_w_ple_in, m_g_post_ple, v_w_in, v_b_gate, v_w_ret_out, v_w_dil_out, v_w_o, v_g_pre_mix, v_g_post_mix, v_g_pre_mlp, v_g_post_mlp, v_w_up, v_w_down, v_g_pre_ple, v_w_ple_gate, v_b_ple_gate, v_w_ple_in, v_g_post_ple):
    given = dict(x=x, p=p, positions=positions, w_in=w_in, b_gate=b_gate, w_ret_out=w_ret_out, w_dil_out=w_dil_out, w_o=w_o, g_pre_mix=g_pre_mix, g_post_mix=g_post_mix, g_pre_mlp=g_pre_mlp, g_post_mlp=g_post_mlp, w_up=w_up, w_down=w_down, g_pre_ple=g_pre_ple, w_ple_gate=w_ple_gate, b_ple_gate=b_ple_gate, w_ple_in=w_ple_in, g_post_ple=g_post_ple, loss_target=loss_target, m_w_in=m_w_in, m_b_gate=m_b_gate, m_w_ret_out=m_w_ret_out, m_w_dil_out=m_w_dil_out, m_w_o=m_w_o, m_g_pre_mix=m_g_pre_mix, m_g_post_mix=m_g_post_mix, m_g_pre_mlp=m_g_pre_mlp, m_g_post_mlp=m_g_post_mlp, m_w_up=m_w_up, m_w_down=m_w_down, m_g_pre_ple=m_g_pre_ple, m_w_ple_gate=m_w_ple_gate, m_b_ple_gate=m_b_ple_gate, m_w_ple_in=m_w_ple_in, m_g_post_ple=m_g_post_ple, v_w_in=v_w_in, v_b_gate=v_b_gate, v_w_ret_out=v_w_ret_out, v_w_dil_out=v_w_dil_out, v_w_o=v_w_o, v_g_pre_mix=v_g_pre_mix, v_g_post_mix=v_g_post_mix, v_g_pre_mlp=v_g_pre_mlp, v_g_post_mlp=v_g_post_mlp, v_w_up=v_w_up, v_w_down=v_w_down, v_g_pre_ple=v_g_pre_ple, v_w_ple_gate=v_w_ple_gate, v_b_ple_gate=v_b_ple_gate, v_w_ple_in=v_w_ple_in, v_g_post_ple=v_g_post_ple)
    weights = {n: given[n] for n in TWIN_WEIGHTS}
    shared = {n: given[n] for n in SHARED_INPUTS}
    per_example = {n: given[n] for n in ['x', 'p', 'positions']}
    grad_fn = _jax.value_and_grad(_loss, argnums=(0, 1))

    def one_microbatch(ex, loss_target):
        ex = dict(ex)
        diff = ex.pop(TWIN_DIFF_INPUT)
        return grad_fn(weights, diff, {**shared, **ex}, loss_target)

    if N_MICROBATCH == 1:
        loss, (grad_w, grad_x) = one_microbatch(per_example, given["loss_target"])
    else:
        def body(carry, xs):
            loss_sum, grad_sum = carry
            l_k, (gw_k, gx_k) = one_microbatch(xs[0], xs[1])
            with _jax.named_scope("update"):
                return (loss_sum + l_k, _jax.tree.map(_jnp.add, grad_sum, gw_k)), gx_k

        init = (_jnp.zeros((), _jnp.float32), _jax.tree.map(_jnp.zeros_like, weights))
        (loss, grad_w), grad_x = _jax.lax.scan(body, init, (per_example, given["loss_target"]))
    with _jax.named_scope("update"):
        delta_w, new_m, new_v = {}, {}, {}
        for n in TWIN_WEIGHTS:
            delta_w[n], new_m[n], new_v[n] = _adamw(weights[n], grad_w[n], given["m_" + n], given["v_" + n])
    return (loss, grad_x, *[grad_w[n] for n in TWIN_WEIGHTS], *[delta_w[n] for n in TWIN_WEIGHTS],
            *[new_m[n] for n in TWIN_WEIGHTS], *[new_v[n] for n in TWIN_WEIGHTS])
```

```python
import functools
import math

import numpy as np
import jax
import jax.numpy as jnp
from jax import lax
from jax.experimental import pallas as pl
from jax.experimental.pallas import tpu as pltpu

F32, BF16 = jnp.float32, jnp.bfloat16
D_MODEL = 1024
EPS = 1e-6
N_DEV = 8
RET_HEADS, RET_QK, RET_V, RET_CHUNK = 4, 256, 512, 128
DIL_GROUPS = (1, 4, 16)
DIL_W = 512
QB = 128
NEG = -1e30
ADAM_LR, ADAM_B1, ADAM_B2, ADAM_EPS, ADAM_WD, ADAM_STEP = 0.001, 0.9, 0.999, 1e-08, 0.01, 10
VMEM_LIMIT_BYTES = 56 * 1024 * 1024
MESH = pl.DeviceIdType.MESH

NN = ((1,), (0,))
NT = ((1,), (1,))
TN = ((0,), (0,))


def _dot(a, b, dn):
    return lax.dot_general(a, b, (dn, ((), ())), preferred_element_type=F32)


def _cparams(sem):
    return pltpu.CompilerParams(dimension_semantics=sem, vmem_limit_bytes=VMEM_LIMIT_BYTES)


def _rms(x):
    return x * lax.rsqrt(jnp.mean(x * x, axis=-1, keepdims=True) + EPS)


def _rms_bwd(x, g, dy):
    r = lax.rsqrt(jnp.mean(x * x, axis=-1, keepdims=True) + EPS)
    xh = x * r
    t = dy * g
    dx = r * (t - xh * jnp.mean(t * xh, axis=-1, keepdims=True))
    return dx, dy * xh


def _colsum(v):
    return jnp.sum(v, axis=0, keepdims=True)


def _sigmoid(v):
    return 1.0 / (1.0 + jnp.exp(-v))


def _matmul(a, b, *, mode, m, n, k, tm, tn, tk, out_dtype, name, a_map=None, b_map=None,
            o_map=None, out_shape=None, a_fn=None, epi=None, epi_fn=None, alias=None):
    nk = k // tk
    grid = (m // tm, n // tn, nk)
    if mode == "nn":
        a_blk, a_im, b_blk, b_im, dn = (tm, tk), (lambda i, j, kk: (i, kk)), (tk, tn), (lambda i, j, kk: (kk, j)), NN
    elif mode == "nt":
        a_blk, a_im, b_blk, b_im, dn = (tm, tk), (lambda i, j, kk: (i, kk)), (tn, tk), (lambda i, j, kk: (j, kk)), NT
    else:
        a_blk, a_im, b_blk, b_im, dn = (tk, tm), (lambda i, j, kk: (kk, i)), (tk, tn), (lambda i, j, kk: (kk, j)), TN
    a_im = a_map or a_im
    b_im = b_map or b_im
    o_im = o_map or (lambda i, j, kk: (i, j))
    n_in = 2 + (epi is not None) + (alias is not None)

    def body(*refs):
        a_ref, b_ref = refs[0], refs[1]
        e_ref = refs[2] if epi is not None else None
        o_ref = refs[n_in]
        acc_ref = refs[n_in + 1] if nk > 1 else None

        def finish(acc):
            if e_ref is not None:
                acc = epi_fn(acc, e_ref[...])
            o_ref[...] = acc.astype(o_ref.dtype)

        av = a_ref[...]
        if a_fn is not None:
            av = a_fn(av)
        part = _dot(av, b_ref[...], dn)
        if nk == 1:
            finish(part)
        else:
            kk = pl.program_id(2)

            @pl.when(kk == 0)
            def _():
                acc_ref[...] = part

            @pl.when(kk > 0)
            def _():
                acc_ref[...] += part

            @pl.when(kk == nk - 1)
            def _():
                finish(acc_ref[...])

    in_specs = [pl.BlockSpec(a_blk, a_im), pl.BlockSpec(b_blk, b_im)]
    args = [a, b]
    if epi is not None:
        in_specs.append(pl.BlockSpec((tm, tn), o_im))
        args.append(epi)
    io_alias = {}
    if alias is not None:
        in_specs.append(pl.BlockSpec(memory_space=pl.ANY))
        args.append(alias)
        io_alias = {len(args) - 1: 0}
    return pl.pallas_call(
        body, name=name, grid=grid, in_specs=in_specs,
        out_specs=pl.BlockSpec((tm, tn), o_im),
        out_shape=jax.ShapeDtypeStruct(out_shape or (m, n), out_dtype),
        scratch_shapes=[pltpu.VMEM((tm, tn), F32)] if nk > 1 else [],
        input_output_aliases=io_alias,
        compiler_params=_cparams(("parallel", "parallel", "arbitrary")),
    )(*args)


def _relu_sq(v):
    r = jnp.maximum(v.astype(F32), 0.0)
    return (r * r).astype(BF16)


def _rowwise(name, fn, s, tr, rows, vecs, outs, accs=()):
    n_r, n_v, n_o, n_a = len(rows), len(vecs), len(outs), len(accs)

    def body(*refs):
        vals = [refs[i][...] for i in range(n_r + n_v)]
        o_refs = refs[n_r + n_v:n_r + n_v + n_o]
        a_refs = refs[n_r + n_v + n_o:]
        o_vals, a_vals = fn(*vals)
        for ref, val in zip(o_refs, o_vals):
            ref[...] = val.astype(ref.dtype)
        if n_a:
            @pl.when(pl.program_id(0) == 0)
            def _():
                for ref in a_refs:
                    ref[...] = jnp.zeros_like(ref)

            for ref, val in zip(a_refs, a_vals):
                ref[...] += val

    in_specs = [pl.BlockSpec((tr, w), functools.partial(lambda i, cb: (i, cb), cb=cb)) for _, w, cb in rows]
    in_specs += [pl.BlockSpec(v.shape, lambda i: (0, 0)) for v in vecs]
    out_specs = [pl.BlockSpec((tr, w), lambda i: (i, 0)) for w, _ in outs]
    out_specs += [pl.BlockSpec((1, w), lambda i: (0, 0)) for w in accs]
    out_shape = [jax.ShapeDtypeStruct((s, w), dt) for w, dt in outs]
    out_shape += [jax.ShapeDtypeStruct((1, w), F32) for w in accs]
    res = pl.pallas_call(
        body, name=name, grid=(s // tr,), in_specs=in_specs, out_specs=out_specs, out_shape=out_shape,
        compiler_params=_cparams(("arbitrary",)),
    )(*[r[0] for r in rows], *vecs)
    return res[:n_o], res[n_o:]


def _to_streams(a, dil):
    if dil == 1:
        return a
    s, w = a.shape
    return a.reshape(s // dil, dil, w).transpose(1, 0, 2).reshape(s, w)


def _from_streams(a, dil):
    if dil == 1:
        return a
    s, w = a.shape
    return a.reshape(dil, s // dil, w).transpose(1, 0, 2).reshape(s, w)


def _ret_tables():
    h = np.arange(RET_HEADS, dtype=np.float32)
    lg = np.log1p(-(np.float32(2.0) ** (-5.0 - h))).astype(np.float32)
    idx = np.arange(RET_CHUNK, dtype=np.float32)
    diff = idx[:, None] - idx[None, :]
    dm = np.where(diff[None] >= 0, np.exp(np.maximum(diff, 0.0)[None] * lg[:, None, None]), 0.0)
    qd = np.exp((idx + 1.0)[None, :, None] * lg[:, None, None])
    kd = np.exp((RET_CHUNK - 1.0 - idx)[None, :, None] * lg[:, None, None])
    cd = np.exp(RET_CHUNK * lg)[:, None, None]
    return [jnp.asarray(t, F32) for t in (dm, qd, kd, cd)]


def _rope_half(v, cos, sin):
    v1, v2 = v[:, :128], v[:, 128:]
    return jnp.concatenate([v1 * cos - v2 * sin, v2 * cos + v1 * sin], axis=1)


def _unrope_half(d, cos, sin):
    d1, d2 = d[:, :128], d[:, 128:]
    return jnp.concatenate([d1 * cos + d2 * sin, d2 * cos - d1 * sin], axis=1)


def _ret_specs(rb, rev_n):
    def rowmap(w_blk):
        return lambda h, n: (rev_n(n), w_blk(h))
    tab = [pl.BlockSpec((1, RET_CHUNK, RET_CHUNK), lambda h, n: (h, 0, 0)),
           pl.BlockSpec((1, RET_CHUNK, 1), lambda h, n: (h, 0, 0)),
           pl.BlockSpec((1, RET_CHUNK, 1), lambda h, n: (h, 0, 0)),
           pl.BlockSpec((1, 1, 1), lambda h, n: (h, 0, 0))]
    proj = pl.BlockSpec((rb, 1536), rowmap(lambda h: h))
    cs = pl.BlockSpec((rb, 128), rowmap(lambda h: 0))
    hv = pl.BlockSpec((rb, RET_V), rowmap(lambda h: h))
    return proj, cs, hv, tab


def _ret_fwd(proj_ret, cos, sin, s):
    rb = min(512, s)
    ch = rb // RET_CHUNK
    nb = s // rb
    proj_spec, cs_spec, hv_spec, tab_specs = _ret_specs(rb, lambda n: n)

    def body(p_ref, cos_ref, sin_ref, dm_ref, qd_ref, kd_ref, cd_ref, yr_ref, y_ref, rs_ref, r_acc):
        @pl.when(pl.program_id(1) == 0)
        def _():
            r_acc[...] = jnp.zeros_like(r_acc)

        dm, qd, kd, cd = dm_ref[0], qd_ref[0], kd_ref[0], cd_ref[0]
        for c in range(ch):
            rows = slice(c * RET_CHUNK, (c + 1) * RET_CHUNK)
            cosv, sinv = cos_ref[rows, :], sin_ref[rows, :]
            q = _rope_half(p_ref[rows, 0:256].astype(F32), cosv, sinv)
            kk = _rope_half(p_ref[rows, 256:512].astype(F32), cosv, sinv) * (RET_QK ** -0.5)
            v = p_ref[rows, 512:1024]
            g = p_ref[rows, 1024:1536].astype(F32)
            rb16 = r_acc[...].astype(BF16)
            rs_ref[0, c] = rb16
            sc = _dot(q.astype(BF16), kk.astype(BF16), NT) * dm
            y = _dot(sc.astype(BF16), v, NN) + _dot((q * qd).astype(BF16), rb16, NN)
            r_acc[...] = r_acc[...] * cd + _dot((kk * kd).astype(BF16), v, TN)
            y_ref[rows, :] = y.astype(BF16)
            yr_ref[rows, :] = (_rms(y) * (g * _sigmoid(g))).astype(BF16)

    return pl.pallas_call(
        body, name="ret_fwd", grid=(RET_HEADS, nb),
        in_specs=[proj_spec, cs_spec, cs_spec] + tab_specs,
        out_specs=[hv_spec, hv_spec, pl.BlockSpec((1, ch, RET_QK, RET_V), lambda h, n: (h, n, 0, 0))],
        out_shape=[jax.ShapeDtypeStruct((s, RET_HEADS * RET_V), BF16), jax.ShapeDtypeStruct((s, RET_HEADS * RET_V), BF16),
                   jax.ShapeDtypeStruct((RET_HEADS, s // RET_CHUNK, RET_QK, RET_V), BF16)],
        scratch_shapes=[pltpu.VMEM((RET_QK, RET_V), F32)],
        compiler_params=_cparams(("parallel", "arbitrary")),
    )(proj_ret, cos, sin, *_ret_tables())


def _ret_bwd(proj_ret, cos, sin, y, d_yr, rs, s):
    rb = min(512, s)
    ch = rb // RET_CHUNK
    nb = s // rb
    proj_spec, cs_spec, hv_spec, tab_specs = _ret_specs(rb, lambda n: nb - 1 - n)

    def body(p_ref, cos_ref, sin_ref, y_ref, dyr_ref, rs_ref, dm_ref, qd_ref, kd_ref, cd_ref, o_ref, dr_acc):
        @pl.when(pl.program_id(1) == 0)
        def _():
            dr_acc[...] = jnp.zeros_like(dr_acc)

        dm, qd, kd, cd = dm_ref[0], qd_ref[0], kd_ref[0], cd_ref[0]
        for c in reversed(range(ch)):
            rows = slice(c * RET_CHUNK, (c + 1) * RET_CHUNK)
            cosv, sinv = cos_ref[rows, :], sin_ref[rows, :]
            q = _rope_half(p_ref[rows, 0:256].astype(F32), cosv, sinv)
            kk = _rope_half(p_ref[rows, 256:512].astype(F32), cosv, sinv) * (RET_QK ** -0.5)
            v = p_ref[rows, 512:1024]
            g = p_ref[rows, 1024:1536].astype(F32)
            yv = y_ref[rows, :].astype(F32)
            dyr = dyr_ref[rows, :].astype(F32)
            sg = _sigmoid(g)
            r = lax.rsqrt(jnp.mean(yv * yv, axis=-1, keepdims=True) + EPS)
            yn = yv * r
            dg = dyr * yn * (sg * (1.0 + g * (1.0 - sg)))
            dyn = dyr * (g * sg)
            dy = (r * (dyn - yn * jnp.mean(dyn * yn, axis=-1, keepdims=True))).astype(BF16)
            qb, kb = q.astype(BF16), kk.astype(BF16)
            rb16 = rs_ref[0, c]
            drb = dr_acc[...].astype(BF16)
            sd = _dot(qb, kb, NT) * dm
            ds = (_dot(dy, v, NT) * dm).astype(BF16)
            dq = _dot(ds, kb, NN) + qd * _dot(dy, rb16, NT)
            dk = _dot(ds, qb, TN) + kd * _dot(v, drb, NT)
            dv = _dot(sd.astype(BF16), dy, TN) + _dot((kk * kd).astype(BF16), drb, NN)
            dr_acc[...] = dr_acc[...] * cd + _dot((q * qd).astype(BF16), dy, TN)
            o_ref[rows, 0:256] = _unrope_half(dq, cosv, sinv).astype(BF16)
            o_ref[rows, 256:512] = (_unrope_half(dk, cosv, sinv) * (RET_QK ** -0.5)).astype(BF16)
            o_ref[rows, 512:1024] = dv.astype(BF16)
            o_ref[rows, 1024:1536] = dg.astype(BF16)

    return pl.pallas_call(
        body, name="ret_bwd", grid=(RET_HEADS, nb),
        in_specs=[proj_spec, cs_spec, cs_spec, hv_spec, hv_spec,
                  pl.BlockSpec((1, ch, RET_QK, RET_V), lambda h, n: (h, nb - 1 - n, 0, 0))] + tab_specs,
        out_specs=proj_spec,
        out_shape=jax.ShapeDtypeStruct((s, RET_HEADS * 1536), BF16),
        scratch_shapes=[pltpu.VMEM((RET_QK, RET_V), F32)],
        compiler_params=_cparams(("parallel", "arbitrary")),
    )(proj_ret, cos, sin, y, d_yr, rs, *_ret_tables())


def _rope_qk(qkv, tc, ts1, ts2, s, name):
    def fn(q, k, c, s1, s2):
        outs = []
        for v in (q, k):
            for cc in range(4):
                vv = v[:, cc * 128:(cc + 1) * 128].astype(F32)
                outs.append(vv * c + pltpu.roll(vv, 120, 1) * s1 + pltpu.roll(vv, 8, 1) * s2)
        return [jnp.concatenate(outs, axis=1)], []

    (out,), _ = _rowwise(name, fn, s, min(512, s), [(qkv, 512, 0), (qkv, 512, 1), (tc, 128, 0), (ts1, 128, 0), (ts2, 128, 0)],
                         [], [(1024, BF16)])
    return out


def _head_masks():
    ri = lax.broadcasted_iota(jnp.int32, (QB, QB), 0)
    ci = lax.broadcasted_iota(jnp.int32, (QB, QB), 1)
    return ri - ci, ci < 64


def _dil_fwd(qkr, qkv, dil, s, name):
    length = s // dil
    rb = min(512, length)
    nsub = rb // QB
    nbs = length // rb
    sub_per = rb // QB

    def body(q_ref, k_ref, v_ref, kp_ref, vp_ref, o_ref, l_ref, kf, vf):
        first = (pl.program_id(0) % nbs) == 0
        kf[0:QB, :] = kp_ref[...]
        kf[QB:, :] = k_ref[...]
        vf[0:QB, :] = vp_ref[...]
        vf[QB:, :] = v_ref[...]
        diff, lane_lo = _head_masks()
        lower = diff >= 0

        def step(t, carry):
            j = t // nsub
            i = t % nsub
            lo = pl.multiple_of(j * 128, 128)
            r0 = pl.multiple_of(i * QB, QB)
            r1 = pl.multiple_of(i * QB + QB, QB)
            q = q_ref[pl.ds(r0, QB), pl.ds(lo, 128)]
            kp, kc = kf[pl.ds(r0, QB), pl.ds(lo, 128)], kf[pl.ds(r1, QB), pl.ds(lo, 128)]
            vp, vc = vf[pl.ds(r0, QB), pl.ds(lo, 128)], vf[pl.ds(r1, QB), pl.ds(lo, 128)]
            pmask = (0 - diff) >= jnp.where(jnp.logical_and(first, i == 0), QB, 0)
            o = jnp.zeros((QB, 128), F32)
            lse = jnp.zeros((QB, 128), F32)
            for hh in range(2):
                hm = lane_lo if hh == 0 else jnp.logical_not(lane_lo)
                qm = jnp.where(hm, q, jnp.zeros_like(q))
                sc = jnp.where(lower, _dot(qm, kc, NT) * 0.125, NEG)
                sp = jnp.where(pmask, _dot(qm, kp, NT) * 0.125, NEG)
                m = jnp.maximum(jnp.max(sc, axis=1, keepdims=True), jnp.max(sp, axis=1, keepdims=True))
                pc, pp = jnp.exp(sc - m), jnp.exp(sp - m)
                den = jnp.sum(pc, axis=1, keepdims=True) + jnp.sum(pp, axis=1, keepdims=True)
                vcm = jnp.where(hm, vc, jnp.zeros_like(vc))
                vpm = jnp.where(hm, vp, jnp.zeros_like(vp))
                o = o + (_dot(pc.astype(BF16), vcm, NN) + _dot(pp.astype(BF16), vpm, NN)) / den
                lse = lse + jnp.where(hm, m + jnp.log(den), 0.0)
            o_ref[pl.ds(r0, QB), pl.ds(lo, 128)] = o
            l_ref[pl.ds(r0, QB), pl.ds(lo, 128)] = lse
            return carry

        lax.fori_loop(0, 4 * nsub, step, 0)

    prev = lambda n: jnp.maximum(n * sub_per - 1, 0)
    cur = lambda cb: (lambda n: (n, cb))
    return pl.pallas_call(
        body, name=name, grid=(s // rb,),
        in_specs=[pl.BlockSpec((rb, DIL_W), cur(0)), pl.BlockSpec((rb, DIL_W), cur(1)), pl.BlockSpec((rb, DIL_W), cur(2)),
                  pl.BlockSpec((QB, DIL_W), lambda n: (prev(n), 1)), pl.BlockSpec((QB, DIL_W), lambda n: (prev(n), 2))],
        out_specs=[pl.BlockSpec((rb, DIL_W), cur(0)), pl.BlockSpec((rb, DIL_W), cur(0))],
        out_shape=[jax.ShapeDtypeStruct((s, DIL_W), F32), jax.ShapeDtypeStruct((s, DIL_W), F32)],
        scratch_shapes=[pltpu.VMEM((QB + rb, DIL_W), BF16), pltpu.VMEM((QB + rb, DIL_W), BF16)],
        compiler_params=_cparams(("parallel",)),
    )(qkr, qkr, qkv, qkr, qkv)


def _dil_bwd(qkr, qkv, dya, lse, dlt, tc, ts1, ts2, dil, s, name):
    length = s // dil
    rb = min(512, length)
    nsub = rb // QB
    nbs = length // rb
    npairs = 2 * nsub + 1
    last_blk = s // QB - 1

    def body(q_ref, k_ref, v_ref, kp_ref, vp_ref, qn_ref, dy_ref, dyn_ref, l_ref, ln_ref, d_ref, dn_ref,
             c_ref, s1_ref, s2_ref, o_ref, kf, vf, qf, dyf, lf, df, dqa, dka, dva):
        nl = pl.program_id(0) % nbs
        first, last = nl == 0, nl == nbs - 1
        kf[0:QB, :] = kp_ref[...]
        kf[QB:, :] = k_ref[...]
        vf[0:QB, :] = vp_ref[...]
        vf[QB:, :] = v_ref[...]
        qf[0:rb, :] = q_ref[...]
        qf[rb:, :] = qn_ref[...]
        dyf[0:rb, :] = dy_ref[...]
        dyf[rb:, :] = dyn_ref[...]
        lf[0:rb, :] = l_ref[...]
        lf[rb:, :] = ln_ref[...]
        df[0:rb, :] = d_ref[...]
        df[rb:, :] = dn_ref[...]
        dqa[...] = jnp.zeros_like(dqa)
        dka[...] = jnp.zeros_like(dka)
        dva[...] = jnp.zeros_like(dva)
        diff, lane_lo = _head_masks()

        def step(t, carry):
            j = t // npairs
            pidx = t % npairs
            qi = pidx // 2
            tb = pidx % 2
            lo = pl.multiple_of(j * 128, 128)
            qr = pl.multiple_of(qi * QB, QB)
            kr = pl.multiple_of((qi + tb) * QB, QB)
            q = qf[pl.ds(qr, QB), pl.ds(lo, 128)]
            do = dyf[pl.ds(qr, QB), pl.ds(lo, 128)]
            lv = lf[pl.ds(qr, QB), pl.ds(lo, 128)]
            dl = df[pl.ds(qr, QB), pl.ds(lo, 128)]
            k = kf[pl.ds(kr, QB), pl.ds(lo, 128)]
            v = vf[pl.ds(kr, QB), pl.ds(lo, 128)]
            bad = jnp.logical_or(jnp.logical_and(first, qi == 0), jnp.logical_and(last, qi == nsub))
            skip = jnp.logical_and(tb == 0, bad)
            mask = diff * (2 * tb - 1) >= jnp.where(skip, QB, 0)
            dq = jnp.zeros((QB, 128), F32)
            dk = jnp.zeros((QB, 128), F32)
            dv = jnp.zeros((QB, 128), F32)
            for hh in range(2):
                hm = lane_lo if hh == 0 else jnp.logical_not(lane_lo)
                qm = jnp.where(hm, q, jnp.zeros_like(q))
                dom = jnp.where(hm, do, jnp.zeros_like(do))
                km = jnp.where(hm, k, jnp.zeros_like(k))
                lh = lv[:, 64 * hh:64 * hh + 1]
                dh = dl[:, 64 * hh:64 * hh + 1]
                sc = _dot(qm, k, NT) * 0.125
                p = jnp.where(mask, jnp.exp(jnp.minimum(sc - lh, 0.0)), 0.0)
                ds = (p * (_dot(dom, v, NT) - dh) * 0.125).astype(BF16)
                dq = dq + _dot(ds, km, NN)
                dk = dk + _dot(ds, qm, TN)
                dv = dv + _dot(p.astype(BF16), dom, TN)
            dqa[pl.ds(qr, QB), pl.ds(lo, 128)] += dq
            dka[pl.ds(kr, QB), pl.ds(lo, 128)] += dk
            dva[pl.ds(kr, QB), pl.ds(lo, 128)] += dv
            return carry

        lax.fori_loop(0, 4 * npairs, step, 0)
        cv, s1v, s2v = c_ref[...], s1_ref[...], s2_ref[...]

        def unrope(d):
            return d * cv + pltpu.roll(d * s1v, 8, 1) + pltpu.roll(d * s2v, 120, 1)

        for cc in range(4):
            lanes = slice(cc * 128, (cc + 1) * 128)
            o_ref[:, cc * 128:(cc + 1) * 128] = unrope(dqa[0:rb, lanes]).astype(BF16)
            o_ref[:, 512 + cc * 128:512 + (cc + 1) * 128] = unrope(dka[QB:, lanes]).astype(BF16)
            o_ref[:, 1024 + cc * 128:1024 + (cc + 1) * 128] = dva[QB:, lanes].astype(BF16)

    prev = lambda n: jnp.maximum(n * nsub - 1, 0)
    nxt = lambda n: jnp.minimum(n * nsub + nsub, last_blk)
    cur = lambda cb: (lambda n: (n, cb))
    big = lambda cb: pl.BlockSpec((rb, DIL_W), cur(cb))
    small = lambda im: pl.BlockSpec((QB, DIL_W), im)
    tab = pl.BlockSpec((rb, 128), cur(0))
    return pl.pallas_call(
        body, name=name, grid=(s // rb,),
        in_specs=[big(0), big(1), big(2), small(lambda n: (prev(n), 1)), small(lambda n: (prev(n), 2)),
                  small(lambda n: (nxt(n), 0)), big(0), small(lambda n: (nxt(n), 0)), big(0), small(lambda n: (nxt(n), 0)),
                  big(0), small(lambda n: (nxt(n), 0)), tab, tab, tab],
        out_specs=pl.BlockSpec((rb, 3 * DIL_W), cur(0)),
        out_shape=jax.ShapeDtypeStruct((s, 3 * DIL_W), BF16),
        scratch_shapes=[pltpu.VMEM((QB + rb, DIL_W), BF16), pltpu.VMEM((QB + rb, DIL_W), BF16),
                        pltpu.VMEM((rb + QB, DIL_W), BF16), pltpu.VMEM((rb + QB, DIL_W), BF16),
                        pltpu.VMEM((rb + QB, DIL_W), F32), pltpu.VMEM((rb + QB, DIL_W), F32),
                        pltpu.VMEM((rb + QB, DIL_W), F32), pltpu.VMEM((QB + rb, DIL_W), F32), pltpu.VMEM((QB + rb, DIL_W), F32)],
        compiler_params=_cparams(("parallel",)),
    )(qkr, qkr, qkv, qkr, qkv, qkr, dya, dya, lse, lse, dlt, dlt, tc, ts1, ts2)


def _dil_merge(o_g, l_g, s):
    def fn(o0, l0, o1, l1, o2, l2):
        m = jnp.maximum(jnp.maximum(l0, l1), l2)
        e0, e1, e2 = jnp.exp(l0 - m), jnp.exp(l1 - m), jnp.exp(l2 - m)
        den = e0 + e1 + e2
        return [(e0 * o0 + e1 * o1 + e2 * o2) / den, m + jnp.log(den)], []

    rows = [(a, DIL_W, 0) for pair in zip(o_g, l_g) for a in pair]
    (ya, lse), _ = _rowwise("dil_merge", fn, s, min(512, s), rows, [], [(DIL_W, BF16), (DIL_W, F32)])
    return ya, lse


def _dil_bwd_prep(d_ya, ya, s):
    def fn(dya, yav):
        lane_lo = lax.broadcasted_iota(jnp.int32, (dya.shape[0], 128), 1) < 64
        parts = []
        for cc in range(4):
            prod = dya[:, cc * 128:(cc + 1) * 128] * yav[:, cc * 128:(cc + 1) * 128].astype(F32)
            lo = jnp.where(lane_lo, prod, 0.0)
            s_lo = jnp.sum(lo, axis=1, keepdims=True)
            s_hi = jnp.sum(prod - lo, axis=1, keepdims=True)
            parts.append(jnp.where(lane_lo, s_lo, s_hi))
        return [dya, jnp.concatenate(parts, axis=1)], []

    (dyb, dlt), _ = _rowwise("dil_bwd_prep", fn, s, min(512, s), [(d_ya, DIL_W, 0), (ya, DIL_W, 0)], [],
                             [(DIL_W, BF16), (DIL_W, F32)])
    return dyb, dlt


def _wb_ret(t):
    h, r = t // 6, t % 6
    return jnp.where(r == 0, h, jnp.where(r == 1, 4 + h, jnp.where(r < 4, 6 + 2 * h + r, 12 + 2 * h + r)))


def _wb_gate(t):
    return 21 + t


def _wb_dil(g):
    return lambda t: 12 + g + 3 * t


def _local_step(xs, pb, tgt, tabs, wts, vec, s):
    tm = min(1024, s)
    tr = min(256, s)
    mm = functools.partial(_matmul, tm=tm)
    win = wts["w_in"]

    (u_nat,), _ = _rowwise("prenorm", lambda xv, g: ([_rms(xv) * g], []), s, tr, [(xs, 1024, 0)], [vec["g_pre_mix"]], [(1024, BF16)])
    u = [_to_streams(u_nat, dil) for dil in DIL_GROUPS]
    proj_ret = mm(u[0], win, mode="nt", m=s, n=6144, k=1024, tn=256, tk=1024, out_dtype=BF16, name="inproj_ret",
                  b_map=lambda i, j, kk: (_wb_ret(j), 0))
    proj_gate = mm(u[0], win, mode="nt", m=s, n=2048, k=1024, tn=512, tk=1024, out_dtype=BF16, name="inproj_gate",
                   b_map=lambda i, j, kk: (_wb_gate(j), 0))
    qkv = [mm(u[g], win, mode="nt", m=s, n=1536, k=1024, tn=512, tk=1024, out_dtype=BF16, name="inproj_dil%d" % g,
              b_map=functools.partial(lambda i, j, kk, f: (f(j), 0), f=_wb_dil(g))) for g in range(3)]

    yr, y_ret, rstate = _ret_fwd(proj_ret, tabs["cos_r"], tabs["sin_r"], s)
    a_br = mm(yr, wts["w_ret_out"], mode="nn", m=s, n=1024, k=2048, tn=1024, tk=512, out_dtype=BF16, name="ret_out")

    qkr, o_g, l_g = [], [], []
    for g, dil in enumerate(DIL_GROUPS):
        qkr.append(_rope_qk(qkv[g], *tabs["dil"][g], s, "rope_qk%d" % g))
        o, l = _dil_fwd(qkr[g], qkv[g], dil, s, "dil_fwd%d" % g)
        o_g.append(_from_streams(o, dil))
        l_g.append(_from_streams(l, dil))
    ya, lse = _dil_merge(o_g, l_g, s)
    b_br = mm(ya, wts["w_dil_out"], mode="nt", m=s, n=1024, k=512, tn=1024, tk=512, out_dtype=BF16, name="dil_out")

    def gate_mix(a, b, gr, ga, b0, b1):
        return [_sigmoid(gr.astype(F32) + b0) * a.astype(F32) + _sigmoid(ga.astype(F32) + b1) * b.astype(F32)], []

    (mixed,), _ = _rowwise("gate_mix", gate_mix, s, tr, [(a_br, 1024, 0), (b_br, 1024, 0), (proj_gate, 1024, 0), (proj_gate, 1024, 1)],
                           [vec["b0"], vec["b1"]], [(1024, BF16)])
    z = mm(mixed, wts["w_o"], mode="nn", m=s, n=1024, k=1024, tn=1024, tk=1024, out_dtype=F32, name="w_o")

    def post_norm(h, f, g_post, g_pre):
        hn = h + _rms(f) * g_post
        return [hn, _rms(hn) * g_pre], []

    (h1, v2), _ = _rowwise("post_mix", post_norm, s, tr, [(xs, 1024, 0), (z, 1024, 0)], [vec["g_post_mix"], vec["g_pre_mlp"]],
                           [(1024, F32), (1024, BF16)])
    a_up = mm(v2, wts["w_up"], mode="nt", m=s, n=4096, k=1024, tn=512, tk=1024, out_dtype=BF16, name="mlp_up")
    f_dn = mm(a_up, wts["w_down"], mode="nn", m=s, n=1024, k=4096, tn=1024, tk=512, out_dtype=F32, name="mlp_down", a_fn=_relu_sq)
    (h2, t_ple), _ = _rowwise("post_mlp", post_norm, s, tr, [(h1, 1024, 0), (f_dn, 1024, 0)], [vec["g_post_mlp"], vec["g_pre_ple"]],
                              [(1024, F32), (1024, BF16)])
    gl = mm(t_ple, wts["w_ple_gate"], mode="nn", m=s, n=1024, k=1024, tn=1024, tk=1024, out_dtype=F32, name="ple_gate")
    e_ple = mm(pb, wts["w_ple_in"], mode="nt", m=s, n=1024, k=256, tn=1024, tk=256, out_dtype=F32, name="ple_in")

    def ple_loss(h, glv, e, tg, b, g):
        gate = _sigmoid(glv + b)
        ge = gate * e
        diff = h + _rms(ge) * g - tg
        dy = diff * (1.0 / D_MODEL)
        d_ge, dg = _rms_bwd(ge, g, dy)
        d_gl = d_ge * e * gate * (1.0 - gate)
        loss = jnp.zeros((1, D_MODEL), F32) + 0.5 * jnp.sum(diff * diff) * (1.0 / D_MODEL)
        return [dy, d_gl, d_ge * gate], [_colsum(dg), _colsum(d_gl), loss]

    (dy, d_gl, d_e), (dg_post_ple, db_ple, loss) = _rowwise(
        "ple_loss", ple_loss, s, tr, [(h2, 1024, 0), (gl, 1024, 0), (e_ple, 1024, 0), (tgt, 1024, 0)],
        [vec["b_ple"], vec["g_post_ple"]], [(1024, F32), (1024, BF16), (1024, BF16)], [1024, 1024, 1024])

    ts = min(512, s)
    wg = functools.partial(_matmul, mode="tn", k=s, tk=ts, out_dtype=F32)
    grads = {}
    grads["w_ple_in"] = wg(d_e, pb, m=1024, n=256, tm=1024, tn=256, name="g_ple_in")
    grads["w_ple_gate"] = wg(t_ple, d_gl, m=1024, n=1024, tm=1024, tn=1024, name="g_ple_gate")
    d_t = mm(d_gl, wts["w_ple_gate"], mode="nt", m=s, n=1024, k=1024, tn=1024, tk=1024, out_dtype=F32, name="d_t")

    def bwd_ple_mlp(h, dt, dyv, f, g_pre, g_post):
        dx, dg1 = _rms_bwd(h, g_pre, dt)
        dh = dyv + dx
        df, dg2 = _rms_bwd(f, g_post, dh)
        return [dh, df], [_colsum(dg1), _colsum(dg2)]

    (d_h2, d_f), (dg_pre_ple, dg_post_mlp) = _rowwise(
        "bwd_ple_mlp", bwd_ple_mlp, s, tr, [(h2, 1024, 0), (d_t, 1024, 0), (dy, 1024, 0), (f_dn, 1024, 0)],
        [vec["g_pre_ple"], vec["g_post_mlp"]], [(1024, F32), (1024, BF16)], [1024, 1024])
    d_a = mm(d_f, wts["w_down"], mode="nt", m=s, n=4096, k=1024, tn=512, tk=1024, out_dtype=BF16, name="d_a",
             epi=a_up, epi_fn=lambda acc, av: acc * (2.0 * jnp.maximum(av.astype(F32), 0.0)))
    grads["w_down"] = wg(a_up, d_f, m=4096, n=1024, tm=1024, tn=1024, name="g_down", a_fn=_relu_sq)
    grads["w_up"] = wg(d_a, v2, m=4096, n=1024, tm=1024, tn=1024, name="g_up")
    d_v2 = mm(d_a, wts["w_up"], mode="nn", m=s, n=1024, k=4096, tn=1024, tk=512, out_dtype=F32, name="d_v2")

    (d_h1, d_z), (dg_pre_mlp, dg_post_mix) = _rowwise(
        "bwd_mlp_mix", bwd_ple_mlp, s, tr, [(h1, 1024, 0), (d_v2, 1024, 0), (d_h2, 1024, 0), (z, 1024, 0)],
        [vec["g_pre_mlp"], vec["g_post_mix"]], [(1024, F32), (1024, BF16)], [1024, 1024])
    d_mixed = mm(d_z, wts["w_o"], mode="nt", m=s, n=1024, k=1024, tn=1024, tk=1024, out_dtype=F32, name="d_mixed")
    grads["w_o"] = wg(mixed, d_z, m=1024, n=1024, tm=1024, tn=1024, name="g_o")

    def bwd_gate(dm, a, b, gr, ga, b0, b1):
        sa, sb = _sigmoid(gr.astype(F32) + b0), _sigmoid(ga.astype(F32) + b1)
        dgr = dm * a.astype(F32) * sa * (1.0 - sa)
        dga = dm * b.astype(F32) * sb * (1.0 - sb)
        return [dm * sa, dm * sb, jnp.concatenate([dgr, dga], axis=1)], [_colsum(dgr), _colsum(dga)]

    (d_abr, d_bbr, dproj_gate), (db0, db1) = _rowwise(
        "bwd_gate", bwd_gate, s, tr, [(d_mixed, 1024, 0), (a_br, 1024, 0), (b_br, 1024, 0), (proj_gate, 1024, 0), (proj_gate, 1024, 1)],
        [vec["b0"], vec["b1"]], [(1024, BF16), (1024, BF16), (2048, BF16)], [1024, 1024])
    grads["w_ret_out"] = wg(yr, d_abr, m=2048, n=1024, tm=1024, tn=1024, name="g_ret_out")
    d_yr = mm(d_abr, wts["w_ret_out"], mode="nt", m=s, n=2048, k=1024, tn=512, tk=1024, out_dtype=BF16, name="d_yr")
    grads["w_dil_out"] = wg(d_bbr, ya, m=1024, n=512, tm=1024, tn=512, name="g_dil_out")
    d_ya = mm(d_bbr, wts["w_dil_out"], mode="nn", m=s, n=512, k=1024, tn=512, tk=1024, out_dtype=F32, name="d_ya")

    dproj_ret = _ret_bwd(proj_ret, tabs["cos_r"], tabs["sin_r"], y_ret, d_yr, rstate, s)
    dyb, dlt = _dil_bwd_prep(d_ya, ya, s)
    dqkv = [_dil_bwd(qkr[g], qkv[g], _to_streams(dyb, dil), _to_streams(lse, dil), _to_streams(dlt, dil), *tabs["dil"][g],
                     dil, s, "dil_bwd%d" % g) for g, dil in enumerate(DIL_GROUPS)]

    g_in = wg(dproj_ret, u[0], m=6144, n=1024, tm=256, tn=1024, name="g_in_ret", out_shape=(12800, 1024),
              o_map=lambda i, j, kk: (_wb_ret(i), 0))
    g_in = wg(dproj_gate, u[0], m=2048, n=1024, tm=512, tn=1024, name="g_in_gate", out_shape=(12800, 1024),
              o_map=lambda i, j, kk: (_wb_gate(i), 0), alias=g_in)
    for g in range(3):
        g_in = wg(dqkv[g], u[g], m=1536, n=1024, tm=512, tn=1024, name="g_in_dil%d" % g, out_shape=(12800, 1024),
                  o_map=functools.partial(lambda i, j, kk, f: (f(i), 0), f=_wb_dil(g)), alias=g_in)
    grads["w_in"] = g_in

    du_ret = mm(dproj_ret, win, mode="nn", m=s, n=1024, k=6144, tn=1024, tk=256, out_dtype=F32, name="du_ret",
                b_map=lambda i, j, kk: (_wb_ret(kk), 0))
    du_gate = mm(dproj_gate, win, mode="nn", m=s, n=1024, k=2048, tn=1024, tk=512, out_dtype=F32, name="du_gate",
                 b_map=lambda i, j, kk: (_wb_gate(kk), 0))
    du_dil = [mm(dqkv[g], win, mode="nn", m=s, n=1024, k=1536, tn=1024, tk=512, out_dtype=F32, name="du_dil%d" % g,
                 b_map=functools.partial(lambda i, j, kk, f: (f(kk), 0), f=_wb_dil(g))) for g in range(3)]
    def grad_x_fn(xv, dh, d0, d1, d2, d3, d4, g):
        dx, dg = _rms_bwd(xv, g, d0 + d1 + d2 + d3 + d4)
        return [dh + dx], [_colsum(dg)]

    du_all = [du_ret, du_gate] + [_from_streams(du_dil[g], dil) for g, dil in enumerate(DIL_GROUPS)]
    (grad_x,), (dg_pre_mix,) = _rowwise("grad_x", grad_x_fn, s, tr, [(a, 1024, 0) for a in [xs, d_h1] + du_all],
                                        [vec["g_pre_mix"]], [(1024, F32)], [1024])

    zero = jnp.zeros((1, D_MODEL), F32)
    packet = jnp.concatenate([dg_pre_mix, dg_post_mix, dg_pre_mlp, dg_post_mlp, dg_pre_ple, db_ple, dg_post_ple, loss,
                              db0, db1] + [zero] * 6, axis=0)
    return grad_x, grads, packet


def _mesh_pos():
    return lax.axis_index("x"), lax.axis_index("y"), lax.axis_index("c")


def _all_gather(shards):
    nw = len(shards)

    def body(*refs):
        ins, outs = refs[:nw], refs[nw:2 * nw]
        send_sems, recv_sems, local_sems = refs[2 * nw:]
        x, y, c = _mesh_pos()
        me, sibling = (x, y, c), (x, y, 1 - c)
        chips = [(1 - x, y), (x, 1 - y), (1 - x, 1 - y)]

        def region(w, dev):
            return outs[w].at[4 * dev[0] + 2 * dev[1] + dev[2]]

        def copy(w, kk, block, to, src=None):
            return pltpu.make_async_remote_copy(
                src_ref=region(w, block) if src is None else src, dst_ref=region(w, block),
                send_sem=send_sems.at[w * 7 + kk], recv_sem=recv_sems.at[w * 7 + kk], device_id=to, device_id_type=MESH)

        mine = [pltpu.make_async_copy(ins[w], region(w, me), local_sems.at[w]) for w in range(nw)]
        for cp in mine:
            cp.start()
        first = []
        for w in range(nw):
            first.append(copy(w, 0, me, sibling, src=ins[w]))
            first += [copy(w, 1 + j, me, (*chip, c), src=ins[w]) for j, chip in enumerate(chips)]
        for cp in first:
            cp.start()
        passed = []
        for j, chip in enumerate(chips):
            for w in range(nw):
                copy(w, 1 + j, (*chip, c), me).wait_recv()
                cp = copy(w, 4 + j, (*chip, c), sibling)
                cp.start()
                passed.append(cp)
        for w in range(nw):
            copy(w, 0, sibling, me).wait_recv()
            for j, chip in enumerate(chips):
                copy(w, 4 + j, (*chip, 1 - c), me).wait_recv()
        for cp in first + passed:
            cp.wait_send()
        for cp in mine:
            cp.wait()

    hbm = pl.BlockSpec(memory_space=pl.ANY)
    return pl.pallas_call(
        body, name="gather_weights",
        in_specs=[hbm] * nw, out_specs=[hbm] * nw,
        out_shape=[jax.ShapeDtypeStruct((N_DEV,) + sh.shape, sh.dtype) for sh in shards],
        scratch_shapes=[pltpu.SemaphoreType.DMA((nw * 7,)), pltpu.SemaphoreType.DMA((nw * 7,)), pltpu.SemaphoreType.DMA((nw,))],
    )(*shards)


def _exchange_grads(full_grads, packet):
    nw = len(full_grads)
    ni = nw + 1

    def body(*refs):
        srcs, dsts = refs[:ni], refs[ni:2 * ni]
        send_sems, recv_sems, local_sems = refs[2 * ni:]
        x, y, c = _mesh_pos()
        my = 4 * x + 2 * y + c

        def src_of(w, idx):
            return srcs[w] if w == nw else srcs[w].at[idx]

        local = [pltpu.make_async_copy(src_of(w, my), dsts[w].at[my], local_sems.at[w]) for w in range(ni)]
        for cp in local:
            cp.start()
        sends, recvs = [], []
        for w in range(ni):
            for r in range(1, N_DEV):
                px = 1 - x if r & 4 else x
                py = 1 - y if r & 2 else y
                pc = 1 - c if r & 1 else c
                pidx = 4 * px + 2 * py + pc
                sem = w * 7 + r - 1
                sends.append(pltpu.make_async_remote_copy(
                    src_ref=src_of(w, pidx), dst_ref=dsts[w].at[my], send_sem=send_sems.at[sem], recv_sem=recv_sems.at[sem],
                    device_id=(px, py, pc), device_id_type=MESH))
                recvs.append(pltpu.make_async_remote_copy(
                    src_ref=src_of(w, pidx), dst_ref=dsts[w].at[pidx], send_sem=send_sems.at[sem], recv_sem=recv_sems.at[sem],
                    device_id=(px, py, pc), device_id_type=MESH))
        for cp in sends:
            cp.start()
        for cp in recvs:
            cp.wait_recv()
        for cp in sends:
            cp.wait_send()
        for cp in local:
            cp.wait()

    hbm = pl.BlockSpec(memory_space=pl.ANY)
    out_shape = [jax.ShapeDtypeStruct(g.shape, g.dtype) for g in full_grads]
    out_shape.append(jax.ShapeDtypeStruct((N_DEV,) + packet.shape, packet.dtype))
    res = pl.pallas_call(
        body, name="exchange_grads",
        in_specs=[hbm] * ni, out_specs=[hbm] * ni, out_shape=out_shape,
        scratch_shapes=[pltpu.SemaphoreType.DMA((ni * 7,)), pltpu.SemaphoreType.DMA((ni * 7,)), pltpu.SemaphoreType.DMA((ni,))],
    )(*full_grads, packet)
    return res[:nw], res[nw]


def _pick_rows(r, c, target_bytes):
    t = r
    while t % 2 == 0 and t // 2 >= 8 and (t // 2) % 8 == 0 and t * c * 4 > target_bytes:
        t //= 2
    return t


def _sum_slots(slots, name):
    ns, r, c = slots.shape
    tr = _pick_rows(r, c, 256 * 1024)

    def body(s_ref, o_ref):
        acc = s_ref[0]
        for kk in range(1, ns):
            acc = acc + s_ref[kk]
        o_ref[...] = acc

    return pl.pallas_call(
        body, name=name, grid=(r // tr,),
        in_specs=[pl.BlockSpec((ns, tr, c), lambda i: (0, i, 0))], out_specs=pl.BlockSpec((tr, c), lambda i: (i, 0)),
        out_shape=jax.ShapeDtypeStruct((r, c), F32), compiler_params=_cparams(("parallel",)),
    )(slots)


def _adamw(slots, w, m, v, name):
    ns, r, c = slots.shape
    tr = _pick_rows(r, c, 256 * 1024)

    def body(s_ref, w_ref, m_ref, v_ref, g_out, d_out, m_out, v_out):
        g = s_ref[0]
        for kk in range(1, ns):
            g = g + s_ref[kk]
        mn = ADAM_B1 * m_ref[...] + (1.0 - ADAM_B1) * g
        vn = ADAM_B2 * v_ref[...] + (1.0 - ADAM_B2) * (g * g)
        m_hat = mn / (1.0 - ADAM_B1 ** ADAM_STEP)
        v_hat = vn / (1.0 - ADAM_B2 ** ADAM_STEP)
        g_out[...] = g
        d_out[...] = -ADAM_LR * (m_hat / (jnp.sqrt(v_hat) + ADAM_EPS) + ADAM_WD * w_ref[...])
        m_out[...] = mn
        v_out[...] = vn

    blk = pl.BlockSpec((tr, c), lambda i: (i, 0))
    return pl.pallas_call(
        body, name=name, grid=(r // tr,),
        in_specs=[pl.BlockSpec((ns, tr, c), lambda i: (0, i, 0)), blk, blk, blk], out_specs=[blk] * 4,
        out_shape=[jax.ShapeDtypeStruct((r, c), F32)] * 4, compiler_params=_cparams(("parallel",)),
    )(slots, w, m, v)


def _rotary_tables(pos, s):
    posf = pos.astype(F32)
    inv_freq = 1.0 / (10000.0 ** jnp.linspace(0.0, 1.0, RET_QK // 2, dtype=F32))
    ang = posf[:, None] * inv_freq
    tabs = {"cos_r": jnp.cos(ang), "sin_r": jnp.sin(ang), "dil": []}
    freqs = 500000.0 ** (-jnp.arange(0, 16, 2, dtype=F32) / 16)
    ang = posf[:, None] * freqs
    cos, sin = jnp.cos(ang), jnp.sin(ang)
    one, zero = jnp.ones((s, 48), F32), jnp.zeros((s, 48), F32)
    z8 = jnp.zeros((s, 8), F32)
    tc = jnp.tile(jnp.concatenate([cos, cos, one], axis=1), (1, 2))
    ts1 = jnp.tile(jnp.concatenate([-sin, z8, zero], axis=1), (1, 2))
    ts2 = jnp.tile(jnp.concatenate([z8, sin, zero], axis=1), (1, 2))
    for dil in DIL_GROUPS:
        perm = lambda t: t.reshape(s // dil, dil, 128).transpose(1, 0, 2).reshape(s, 128)
        tabs["dil"].append(tuple(perm(t) for t in (tc, ts1, ts2)))
    return tabs


_TRANSPOSED = ("w_in", "w_dil_out", "w_up", "w_ple_in")
_MATS = ("w_in", "w_ret_out", "w_dil_out", "w_o", "w_up", "w_down", "w_ple_gate", "w_ple_in")
_VECS = ("g_pre_mix", "g_post_mix", "g_pre_mlp", "g_post_mlp", "g_pre_ple", "b_ple_gate", "g_post_ple")
_ORDER = ("w_in", "b_gate", "w_ret_out", "w_dil_out", "w_o", "g_pre_mix", "g_post_mix", "g_pre_mlp", "g_post_mlp", "w_up",
          "w_down", "g_pre_ple", "w_ple_gate", "b_ple_gate", "w_ple_in", "g_post_ple")


def kernel(x, p, positions, w_in, b_gate, w_ret_out, w_dil_out, w_o, g_pre_mix, g_post_mix, g_pre_mlp, g_post_mlp, w_up, w_down, g_pre_ple, w_ple_gate, b_ple_gate, w_ple_in, g_post_ple, loss_target, m_w_in, m_b_gate, m_w_ret_out, m_w_dil_out, m_w_o, m_g_pre_mix, m_g_post_mix, m_g_pre_mlp, m_g_post_mlp, m_w_up, m_w_down, m_g_pre_ple, m_w_ple_gate, m_b_ple_gate, m_w_ple_in, m_g_post_ple, v_w_in, v_b_gate, v_w_ret_out, v_w_dil_out, v_w_o, v_g_pre_mix, v_g_post_mix, v_g_pre_mlp, v_g_post_mlp, v_w_up, v_w_down, v_g_pre_ple, v_w_ple_gate, v_b_ple_gate, v_w_ple_in, v_g_post_ple):
    s = x.shape[1]
    wd = dict(w_in=w_in, b_gate=b_gate, w_ret_out=w_ret_out, w_dil_out=w_dil_out, w_o=w_o, g_pre_mix=g_pre_mix,
              g_post_mix=g_post_mix, g_pre_mlp=g_pre_mlp, g_post_mlp=g_post_mlp, w_up=w_up, w_down=w_down,
              g_pre_ple=g_pre_ple, w_ple_gate=w_ple_gate, b_ple_gate=b_ple_gate, w_ple_in=w_ple_in, g_post_ple=g_post_ple)
    md = dict(w_in=m_w_in, b_gate=m_b_gate, w_ret_out=m_w_ret_out, w_dil_out=m_w_dil_out, w_o=m_w_o, g_pre_mix=m_g_pre_mix,
              g_post_mix=m_g_post_mix, g_pre_mlp=m_g_pre_mlp, g_post_mlp=m_g_post_mlp, w_up=m_w_up, w_down=m_w_down,
              g_pre_ple=m_g_pre_ple, w_ple_gate=m_w_ple_gate, b_ple_gate=m_b_ple_gate, w_ple_in=m_w_ple_in, g_post_ple=m_g_post_ple)
    vd = dict(w_in=v_w_in, b_gate=v_b_gate, w_ret_out=v_w_ret_out, w_dil_out=v_w_dil_out, w_o=v_w_o, g_pre_mix=v_g_pre_mix,
              g_post_mix=v_g_post_mix, g_pre_mlp=v_g_pre_mlp, g_post_mlp=v_g_post_mlp, w_up=v_w_up, w_down=v_w_down,
              g_pre_ple=v_g_pre_ple, w_ple_gate=v_w_ple_gate, b_ple_gate=v_b_ple_gate, w_ple_in=v_w_ple_in, g_post_ple=v_g_post_ple)

    shards = [(wd[n][0].T if n in _TRANSPOSED else wd[n][0]).astype(BF16) for n in _MATS]
    gathered = _all_gather(shards + [b_gate[0]])
    wts = {n: g.reshape(N_DEV * g.shape[1], g.shape[2]) for n, g in zip(_MATS, gathered[:-1])}
    bg = gathered[-1].transpose(1, 0, 2).reshape(2, D_MODEL)
    vec = {n: wd[n] for n in _VECS}
    vec.update(b0=bg[0:1], b1=bg[1:2], b_ple=b_ple_gate)

    tabs = _rotary_tables(positions[0], s)
    grad_x, grads, packet = _local_step(x[0], p[0, 0].astype(BF16), loss_target[0], tabs, wts, vec, s)

    full = [grads[n].reshape(N_DEV, grads[n].shape[0] // N_DEV, grads[n].shape[1]) for n in _MATS]
    slots, packets = _exchange_grads(full, packet)
    out = {}
    for n, sl in zip(_MATS, slots):
        if n in _TRANSPOSED:
            sl = _sum_slots(sl, "sum_" + n).T[None]
        out[n] = _adamw(sl, wd[n][0], md[n][0], vd[n][0], "adamw_" + n)
    zero_rows = jnp.zeros((16 - len(_VECS), D_MODEL), F32)
    pack = lambda d: jnp.concatenate([d[n] for n in _VECS] + [zero_rows], axis=0)
    small = _adamw(packets, pack(wd), pack(md), pack(vd), "adamw_vectors")
    for i, n in enumerate(_VECS):
        out[n] = tuple(t[i:i + 1] for t in small)
    my = 4 * lax.axis_index("x") + 2 * lax.axis_index("y") + lax.axis_index("c")
    g_bias = lax.dynamic_slice(small[0], (8, my * 128), (2, 128))
    out["b_gate"] = _adamw(g_bias[None], b_gate[0], m_b_gate[0], v_b_gate[0], "adamw_b_gate")
    loss = small[0][7, 0]

    res = [loss, grad_x[None]]
    for kk in range(4):
        res += [out[n][kk][None] if out[n][kk].ndim == 2 and wd[n].ndim == 3 else out[n][kk] for n in _ORDER]
    return tuple(res)
```

```python
import functools
import math

import numpy as np
import jax
import jax.numpy as jnp
from jax import lax
from jax.experimental import pallas as pl
from jax.experimental.pallas import tpu as pltpu

F32, BF16 = jnp.float32, jnp.bfloat16
D_MODEL = 1024
EPS = 1e-6
N_DEV = 8
RET_HEADS, RET_QK, RET_V, RET_CHUNK = 4, 256, 512, 128
DIL_GROUPS = (1, 4, 16)
DIL_W = 512
QB = 128
NEG = -1e30
ADAM_LR, ADAM_B1, ADAM_B2, ADAM_EPS, ADAM_WD, ADAM_STEP = 0.001, 0.9, 0.999, 1e-08, 0.01, 10
VMEM_LIMIT_BYTES = 56 * 1024 * 1024
MESH = pl.DeviceIdType.MESH

NN = ((1,), (0,))
NT = ((1,), (1,))
TN = ((0,), (0,))


def _dot(a, b, dn):
    return lax.dot_general(a, b, (dn, ((), ())), preferred_element_type=F32)


def _cparams(sem):
    return pltpu.CompilerParams(dimension_semantics=sem, vmem_limit_bytes=VMEM_LIMIT_BYTES)


def _rms(x):
    return x * lax.rsqrt(jnp.mean(x * x, axis=-1, keepdims=True) + EPS)


def _rms_bwd(x, g, dy):
    r = lax.rsqrt(jnp.mean(x * x, axis=-1, keepdims=True) + EPS)
    xh = x * r
    t = dy * g
    dx = r * (t - xh * jnp.mean(t * xh, axis=-1, keepdims=True))
    return dx, dy * xh


def _colsum(v):
    return jnp.sum(v, axis=0, keepdims=True)


def _sigmoid(v):
    return 1.0 / (1.0 + jnp.exp(-v))


def _matmul(a, b, *, mode, m, n, k, tm, tn, tk, out_dtype, name, a_fn=None, epi=None, epi_fn=None):
    nk = k // tk
    grid = (m // tm, n // tn, nk)
    if mode == "nn":
        a_blk, a_im, b_blk, b_im, dn = (tm, tk), (lambda i, j, kk: (i, kk)), (tk, tn), (lambda i, j, kk: (kk, j)), NN
    elif mode == "nt":
        a_blk, a_im, b_blk, b_im, dn = (tm, tk), (lambda i, j, kk: (i, kk)), (tn, tk), (lambda i, j, kk: (j, kk)), NT
    else:
        a_blk, a_im, b_blk, b_im, dn = (tk, tm), (lambda i, j, kk: (kk, i)), (tk, tn), (lambda i, j, kk: (kk, j)), TN
    o_im = lambda i, j, kk: (i, j)
    n_in = 2 + (epi is not None)

    def body(*refs):
        a_ref, b_ref = refs[0], refs[1]
        e_ref = refs[2] if epi is not None else None
        o_ref = refs[n_in]
        acc_ref = refs[n_in + 1] if nk > 1 else None

        def finish(acc):
            if e_ref is not None:
                acc = epi_fn(acc, e_ref[...])
            o_ref[...] = acc.astype(o_ref.dtype)

        av = a_ref[...]
        if a_fn is not None:
            av = a_fn(av)
        part = _dot(av, b_ref[...], dn)
        if nk == 1:
            finish(part)
        else:
            kk = pl.program_id(2)

            @pl.when(kk == 0)
            def _():
                acc_ref[...] = part

            @pl.when(kk > 0)
            def _():
                acc_ref[...] += part

            @pl.when(kk == nk - 1)
            def _():
                finish(acc_ref[...])

    in_specs = [pl.BlockSpec(a_blk, a_im), pl.BlockSpec(b_blk, b_im)]
    args = [a, b]
    if epi is not None:
        in_specs.append(pl.BlockSpec((tm, tn), o_im))
        args.append(epi)
    return pl.pallas_call(
        body, name=name, grid=grid, in_specs=in_specs,
        out_specs=pl.BlockSpec((tm, tn), o_im),
        out_shape=jax.ShapeDtypeStruct((m, n), out_dtype),
        scratch_shapes=[pltpu.VMEM((tm, tn), F32)] if nk > 1 else [],
        compiler_params=_cparams(("parallel", "parallel", "arbitrary")),
    )(*args)


def _relu_sq(v):
    r = jnp.maximum(v.astype(F32), 0.0)
    return (r * r).astype(BF16)


def _rowwise(name, fn, s, tr, rows, vecs, outs, accs=()):
    n_r, n_v, n_o, n_a = len(rows), len(vecs), len(outs), len(accs)

    def body(*refs):
        vals = [refs[i][...] for i in range(n_r + n_v)]
        o_refs = refs[n_r + n_v:n_r + n_v + n_o]
        a_refs = refs[n_r + n_v + n_o:]
        o_vals, a_vals = fn(*vals)
        for ref, val in zip(o_refs, o_vals):
            ref[...] = val.astype(ref.dtype)
        if n_a:
            @pl.when(pl.program_id(0) == 0)
            def _():
                for ref in a_refs:
                    ref[...] = jnp.zeros_like(ref)

            for ref, val in zip(a_refs, a_vals):
                ref[...] += val

    in_specs = [pl.BlockSpec((tr, w), functools.partial(lambda i, cb: (i, cb), cb=cb)) for _, w, cb in rows]
    in_specs += [pl.BlockSpec(v.shape, lambda i: (0, 0)) for v in vecs]
    out_specs = [pl.BlockSpec((tr, w), lambda i: (i, 0)) for w, _ in outs]
    out_specs += [pl.BlockSpec((1, w), lambda i: (0, 0)) for w in accs]
    out_shape = [jax.ShapeDtypeStruct((s, w), dt) for w, dt in outs]
    out_shape += [jax.ShapeDtypeStruct((1, w), F32) for w in accs]
    res = pl.pallas_call(
        body, name=name, grid=(s // tr,), in_specs=in_specs, out_specs=out_specs, out_shape=out_shape,
        compiler_params=_cparams(("arbitrary",)),
    )(*[r[0] for r in rows], *vecs)
    return res[:n_o], res[n_o:]


def _to_streams(a, dil):
    if dil == 1:
        return a
    s, w = a.shape
    return a.reshape(s // dil, dil, w).transpose(1, 0, 2).reshape(s, w)


def _from_streams(a, dil):
    if dil == 1:
        return a
    s, w = a.shape
    return a.reshape(dil, s // dil, w).transpose(1, 0, 2).reshape(s, w)


def _ret_tables():
    h = np.arange(RET_HEADS, dtype=np.float32)
    lg = np.log1p(-(np.float32(2.0) ** (-5.0 - h))).astype(np.float32)
    idx = np.arange(RET_CHUNK, dtype=np.float32)
    diff = idx[:, None] - idx[None, :]
    dm = np.where(diff[None] >= 0, np.exp(np.maximum(diff, 0.0)[None] * lg[:, None, None]), 0.0)
    qd = np.exp((idx + 1.0)[None, :, None] * lg[:, None, None])
    kd = np.exp((RET_CHUNK - 1.0 - idx)[None, :, None] * lg[:, None, None])
    cd = np.exp(RET_CHUNK * lg)[:, None, None]
    return [jnp.asarray(t, F32) for t in (dm, qd, kd, cd)]


def _rope_half(v, cos, sin):
    v1, v2 = v[:, :128], v[:, 128:]
    return jnp.concatenate([v1 * cos - v2 * sin, v2 * cos + v1 * sin], axis=1)


def _unrope_half(d, cos, sin):
    d1, d2 = d[:, :128], d[:, 128:]
    return jnp.concatenate([d1 * cos + d2 * sin, d2 * cos - d1 * sin], axis=1)


def _ret_specs(rb, rev_n):
    def rowmap(w_blk):
        return lambda h, n: (rev_n(n), w_blk(h))
    tab = [pl.BlockSpec((1, RET_CHUNK, RET_CHUNK), lambda h, n: (h, 0, 0)),
           pl.BlockSpec((1, RET_CHUNK, 1), lambda h, n: (h, 0, 0)),
           pl.BlockSpec((1, RET_CHUNK, 1), lambda h, n: (h, 0, 0)),
           pl.BlockSpec((1, 1, 1), lambda h, n: (h, 0, 0))]
    proj = pl.BlockSpec((rb, 1536), rowmap(lambda h: h))
    cs = pl.BlockSpec((rb, 128), rowmap(lambda h: 0))
    hv = pl.BlockSpec((rb, RET_V), rowmap(lambda h: h))
    return proj, cs, hv, tab


def _ret_fwd(proj_ret, cos, sin, s):
    rb = min(512, s)
    ch = rb // RET_CHUNK
    nb = s // rb
    proj_spec, cs_spec, hv_spec, tab_specs = _ret_specs(rb, lambda n: n)

    def body(p_ref, cos_ref, sin_ref, dm_ref, qd_ref, kd_ref, cd_ref, yr_ref, y_ref, rs_ref, r_acc):
        @pl.when(pl.program_id(1) == 0)
        def _():
            r_acc[...] = jnp.zeros_like(r_acc)

        dm, qd, kd, cd = dm_ref[0], qd_ref[0], kd_ref[0], cd_ref[0]
        for c in range(ch):
            rows = slice(c * RET_CHUNK, (c + 1) * RET_CHUNK)
            cosv, sinv = cos_ref[rows, :], sin_ref[rows, :]
            q = _rope_half(p_ref[rows, 0:256].astype(F32), cosv, sinv)
            kk = _rope_half(p_ref[rows, 256:512].astype(F32), cosv, sinv) * (RET_QK ** -0.5)
            v = p_ref[rows, 512:1024]
            g = p_ref[rows, 1024:1536].astype(F32)
            rb16 = r_acc[...].astype(BF16)
            rs_ref[0, c] = rb16
            sc = _dot(q.astype(BF16), kk.astype(BF16), NT) * dm
            y = _dot(sc.astype(BF16), v, NN) + _dot((q * qd).astype(BF16), rb16, NN)
            r_acc[...] = r_acc[...] * cd + _dot((kk * kd).astype(BF16), v, TN)
            y_ref[rows, :] = y.astype(BF16)
            yr_ref[rows, :] = (_rms(y) * (g * _sigmoid(g))).astype(BF16)

    return pl.pallas_call(
        body, name="ret_fwd", grid=(RET_HEADS, nb),
        in_specs=[proj_spec, cs_spec, cs_spec] + tab_specs,
        out_specs=[hv_spec, hv_spec, pl.BlockSpec((1, ch, RET_QK, RET_V), lambda h, n: (h, n, 0, 0))],
        out_shape=[jax.ShapeDtypeStruct((s, RET_HEADS * RET_V), BF16), jax.ShapeDtypeStruct((s, RET_HEADS * RET_V), BF16),
                   jax.ShapeDtypeStruct((RET_HEADS, s // RET_CHUNK, RET_QK, RET_V), BF16)],
        scratch_shapes=[pltpu.VMEM((RET_QK, RET_V), F32)],
        compiler_params=_cparams(("parallel", "arbitrary")),
    )(proj_ret, cos, sin, *_ret_tables())


def _ret_bwd(proj_ret, cos, sin, y, d_yr, rs, s):
    rb = min(512, s)
    ch = rb // RET_CHUNK
    nb = s // rb
    proj_spec, cs_spec, hv_spec, tab_specs = _ret_specs(rb, lambda n: nb - 1 - n)

    def body(p_ref, cos_ref, sin_ref, y_ref, dyr_ref, rs_ref, dm_ref, qd_ref, kd_ref, cd_ref, o_ref, dr_acc):
        @pl.when(pl.program_id(1) == 0)
        def _():
            dr_acc[...] = jnp.zeros_like(dr_acc)

        dm, qd, kd, cd = dm_ref[0], qd_ref[0], kd_ref[0], cd_ref[0]
        for c in reversed(range(ch)):
            rows = slice(c * RET_CHUNK, (c + 1) * RET_CHUNK)
            cosv, sinv = cos_ref[rows, :], sin_ref[rows, :]
            q = _rope_half(p_ref[rows, 0:256].astype(F32), cosv, sinv)
            kk = _rope_half(p_ref[rows, 256:512].astype(F32), cosv, sinv) * (RET_QK ** -0.5)
            v = p_ref[rows, 512:1024]
            g = p_ref[rows, 1024:1536].astype(F32)
            yv = y_ref[rows, :].astype(F32)
            dyr = dyr_ref[rows, :].astype(F32)
            sg = _sigmoid(g)
            r = lax.rsqrt(jnp.mean(yv * yv, axis=-1, keepdims=True) + EPS)
            yn = yv * r
            dg = dyr * yn * (sg * (1.0 + g * (1.0 - sg)))
            dyn = dyr * (g * sg)
            dy = (r * (dyn - yn * jnp.mean(dyn * yn, axis=-1, keepdims=True))).astype(BF16)
            qb, kb = q.astype(BF16), kk.astype(BF16)
            rb16 = rs_ref[0, c]
            drb = dr_acc[...].astype(BF16)
            sd = _dot(qb, kb, NT) * dm
            ds = (_dot(dy, v, NT) * dm).astype(BF16)
            dq = _dot(ds, kb, NN) + qd * _dot(dy, rb16, NT)
            dk = _dot(ds, qb, TN) + kd * _dot(v, drb, NT)
            dv = _dot(sd.astype(BF16), dy, TN) + _dot((kk * kd).astype(BF16), drb, NN)
            dr_acc[...] = dr_acc[...] * cd + _dot((q * qd).astype(BF16), dy, TN)
            o_ref[rows, 0:256] = _unrope_half(dq, cosv, sinv).astype(BF16)
            o_ref[rows, 256:512] = (_unrope_half(dk, cosv, sinv) * (RET_QK ** -0.5)).astype(BF16)
            o_ref[rows, 512:1024] = dv.astype(BF16)
            o_ref[rows, 1024:1536] = dg.astype(BF16)

    return pl.pallas_call(
        body, name="ret_bwd", grid=(RET_HEADS, nb),
        in_specs=[proj_spec, cs_spec, cs_spec, hv_spec, hv_spec,
                  pl.BlockSpec((1, ch, RET_QK, RET_V), lambda h, n: (h, nb - 1 - n, 0, 0))] + tab_specs,
        out_specs=proj_spec,
        out_shape=jax.ShapeDtypeStruct((s, RET_HEADS * 1536), BF16),
        scratch_shapes=[pltpu.VMEM((RET_QK, RET_V), F32)],
        compiler_params=_cparams(("parallel", "arbitrary")),
    )(proj_ret, cos, sin, y, d_yr, rs, *_ret_tables())


def _rope_qk(qkv, tc, ts1, ts2, s, name):
    def fn(q, k, c, s1, s2):
        outs = []
        for v in (q, k):
            for cc in range(4):
                vv = v[:, cc * 128:(cc + 1) * 128].astype(F32)
                outs.append(vv * c + pltpu.roll(vv, 120, 1) * s1 + pltpu.roll(vv, 8, 1) * s2)
        return [jnp.concatenate(outs, axis=1)], []

    (out,), _ = _rowwise(name, fn, s, min(512, s), [(qkv, 512, 0), (qkv, 512, 1), (tc, 128, 0), (ts1, 128, 0), (ts2, 128, 0)],
                         [], [(1024, BF16)])
    return out


def _pair_masks():
    ri = lax.broadcasted_iota(jnp.int32, (2 * QB, 2 * QB), 0)
    ci = lax.broadcasted_iota(jnp.int32, (2 * QB, 2 * QB), 1)
    e = ci - (ri & (QB - 1))
    lane_lo = lax.broadcasted_iota(jnp.int32, (2 * QB, 128), 1) < 64
    return ci, jnp.logical_and(e >= 0, e <= QB), lane_lo


def _stack_heads(v, lane_lo):
    z = jnp.zeros_like(v)
    return jnp.concatenate([jnp.where(lane_lo, v, z), jnp.where(lane_lo, z, v)], axis=0)


def _dil_fwd(qkr, qkv, dil, s, name):
    length = s // dil
    rb = min(512, length)
    nsub = rb // QB
    nbs = length // rb
    sub_per = rb // QB

    def body(q_ref, k_ref, v_ref, kp_ref, vp_ref, o_ref, l_ref, kf, vf):
        first = (pl.program_id(0) % nbs) == 0
        kf[0:QB, :] = kp_ref[...]
        kf[QB:, :] = k_ref[...]
        vf[0:QB, :] = vp_ref[...]
        vf[QB:, :] = v_ref[...]
        ci, band, lane_lo = _pair_masks()
        lo1 = lane_lo[0:QB]

        def step(t, carry):
            j = t // nsub
            i = t % nsub
            lo = pl.multiple_of(j * 128, 128)
            r0 = pl.multiple_of(i * QB, QB)
            q2 = _stack_heads(q_ref[pl.ds(r0, QB), pl.ds(lo, 128)], lo1)
            k2 = kf[pl.ds(r0, 2 * QB), pl.ds(lo, 128)]
            v2 = _stack_heads(vf[pl.ds(r0, 2 * QB), pl.ds(lo, 128)], lane_lo)
            mask = jnp.logical_and(band, ci >= jnp.where(jnp.logical_and(first, i == 0), QB, 0))
            sc = jnp.where(mask, _dot(q2, k2, NT) * 0.125, NEG)
            m = jnp.max(sc, axis=1, keepdims=True)
            p = jnp.exp(sc - m)
            den = jnp.sum(p, axis=1, keepdims=True)
            pb = p.astype(BF16)
            o = _dot(jnp.concatenate([pb[0:QB], pb[QB:]], axis=1), v2, NN)
            inv = 1.0 / den
            lse = m + jnp.log(den)
            o_ref[pl.ds(r0, QB), pl.ds(lo, 128)] = o * jnp.where(lo1, inv[0:QB], inv[QB:])
            l_ref[pl.ds(r0, QB), pl.ds(lo, 128)] = jnp.where(lo1, lse[0:QB], lse[QB:])
            return carry

        lax.fori_loop(0, 4 * nsub, step, 0)

    prev = lambda n: jnp.maximum(n * sub_per - 1, 0)
    cur = lambda cb: (lambda n: (n, cb))
    return pl.pallas_call(
        body, name=name, grid=(s // rb,),
        in_specs=[pl.BlockSpec((rb, DIL_W), cur(0)), pl.BlockSpec((rb, DIL_W), cur(1)), pl.BlockSpec((rb, DIL_W), cur(2)),
                  pl.BlockSpec((QB, DIL_W), lambda n: (prev(n), 1)), pl.BlockSpec((QB, DIL_W), lambda n: (prev(n), 2))],
        out_specs=[pl.BlockSpec((rb, DIL_W), cur(0)), pl.BlockSpec((rb, DIL_W), cur(0))],
        out_shape=[jax.ShapeDtypeStruct((s, DIL_W), F32), jax.ShapeDtypeStruct((s, DIL_W), F32)],
        scratch_shapes=[pltpu.VMEM((QB + rb, DIL_W), BF16), pltpu.VMEM((QB + rb, DIL_W), BF16)],
        compiler_params=_cparams(("parallel",)),
    )(qkr, qkr, qkv, qkr, qkv)


def _dil_bwd(qkr, qkv, dya, lse, dlt, tc, ts1, ts2, dil, s, name):
    length = s // dil
    rb = min(512, length)
    nsub = rb // QB
    nbs = length // rb
    last_blk = s // QB - 1

    def body(q_ref, k_ref, v_ref, kp_ref, vp_ref, qn_ref, dy_ref, dyn_ref, l_ref, ln_ref, d_ref, dn_ref,
             c_ref, s1_ref, s2_ref, o_ref, kf, vf, qf, dyf, lf, df, dqa, dka, dva):
        nl = pl.program_id(0) % nbs
        first, last = nl == 0, nl == nbs - 1
        kf[0:QB, :] = kp_ref[...]
        kf[QB:QB + rb, :] = k_ref[...]
        kf[QB + rb:, :] = jnp.zeros((QB, DIL_W), BF16)
        vf[0:QB, :] = vp_ref[...]
        vf[QB:QB + rb, :] = v_ref[...]
        vf[QB + rb:, :] = jnp.zeros((QB, DIL_W), BF16)
        qf[0:rb, :] = q_ref[...]
        qf[rb:, :] = qn_ref[...]
        dyf[0:rb, :] = dy_ref[...]
        dyf[rb:, :] = dyn_ref[...]
        lf[0:rb, :] = l_ref[...]
        lf[rb:, :] = ln_ref[...]
        df[0:rb, :] = d_ref[...]
        df[rb:, :] = dn_ref[...]
        dka[...] = jnp.zeros_like(dka)
        dva[...] = jnp.zeros_like(dva)
        ci, band, lane_lo = _pair_masks()
        lo1 = lane_lo[0:QB]

        def step(t, carry):
            j = t // (nsub + 1)
            qi = t % (nsub + 1)
            lo = pl.multiple_of(j * 128, 128)
            qr = pl.multiple_of(qi * QB, QB)
            q2 = _stack_heads(qf[pl.ds(qr, QB), pl.ds(lo, 128)], lo1)
            do2 = _stack_heads(dyf[pl.ds(qr, QB), pl.ds(lo, 128)], lo1)
            lv = lf[pl.ds(qr, QB), pl.ds(lo, 128)]
            dl = df[pl.ds(qr, QB), pl.ds(lo, 128)]
            lse2 = jnp.concatenate([lv[:, 0:1], lv[:, 64:65]], axis=0)
            dl2 = jnp.concatenate([dl[:, 0:1], dl[:, 64:65]], axis=0)
            k2 = kf[pl.ds(qr, 2 * QB), pl.ds(lo, 128)]
            v2 = vf[pl.ds(qr, 2 * QB), pl.ds(lo, 128)]
            is_next = qi == nsub
            cmin = jnp.where(jnp.logical_and(first, qi == 0), QB, 0)
            cmax = jnp.where(is_next, jnp.where(last, -1, QB - 1), 2 * QB - 1)
            mask = jnp.logical_and(band, jnp.logical_and(ci >= cmin, ci <= cmax))
            sc = _dot(q2, k2, NT) * 0.125
            p = jnp.where(mask, jnp.exp(jnp.minimum(sc - lse2, 0.0)), 0.0)
            ds = (p * (_dot(do2, v2, NT) - dl2) * 0.125).astype(BF16)
            dq = _dot(jnp.concatenate([ds[0:QB], ds[QB:]], axis=1), _stack_heads(k2, lane_lo), NN)
            dqa[pl.ds(qr, QB), pl.ds(lo, 128)] = dq
            dka[pl.ds(qr, 2 * QB), pl.ds(lo, 128)] += _dot(ds, q2, TN)
            dva[pl.ds(qr, 2 * QB), pl.ds(lo, 128)] += _dot(p.astype(BF16), do2, TN)
            return carry

        lax.fori_loop(0, 4 * (nsub + 1), step, 0)
        cv, s1v, s2v = c_ref[...], s1_ref[...], s2_ref[...]

        def unrope(d):
            return d * cv + pltpu.roll(d * s1v, 8, 1) + pltpu.roll(d * s2v, 120, 1)

        for cc in range(4):
            lanes = slice(cc * 128, (cc + 1) * 128)
            o_ref[:, cc * 128:(cc + 1) * 128] = unrope(dqa[0:rb, lanes]).astype(BF16)
            o_ref[:, 512 + cc * 128:512 + (cc + 1) * 128] = unrope(dka[QB:QB + rb, lanes]).astype(BF16)
            o_ref[:, 1024 + cc * 128:1024 + (cc + 1) * 128] = dva[QB:QB + rb, lanes].astype(BF16)

    prev = lambda n: jnp.maximum(n * nsub - 1, 0)
    nxt = lambda n: jnp.minimum(n * nsub + nsub, last_blk)
    cur = lambda cb: (lambda n: (n, cb))
    big = lambda cb: pl.BlockSpec((rb, DIL_W), cur(cb))
    small = lambda im: pl.BlockSpec((QB, DIL_W), im)
    tab = pl.BlockSpec((rb, 128), cur(0))
    return pl.pallas_call(
        body, name=name, grid=(s // rb,),
        in_specs=[big(0), big(1), big(2), small(lambda n: (prev(n), 1)), small(lambda n: (prev(n), 2)),
                  small(lambda n: (nxt(n), 0)), big(0), small(lambda n: (nxt(n), 0)), big(0), small(lambda n: (nxt(n), 0)),
                  big(0), small(lambda n: (nxt(n), 0)), tab, tab, tab],
        out_specs=pl.BlockSpec((rb, 3 * DIL_W), cur(0)),
        out_shape=jax.ShapeDtypeStruct((s, 3 * DIL_W), BF16),
        scratch_shapes=[pltpu.VMEM((rb + 2 * QB, DIL_W), BF16), pltpu.VMEM((rb + 2 * QB, DIL_W), BF16),
                        pltpu.VMEM((rb + QB, DIL_W), BF16), pltpu.VMEM((rb + QB, DIL_W), BF16),
                        pltpu.VMEM((rb + QB, DIL_W), F32), pltpu.VMEM((rb + QB, DIL_W), F32),
                        pltpu.VMEM((rb + QB, DIL_W), F32), pltpu.VMEM((rb + 2 * QB, DIL_W), F32),
                        pltpu.VMEM((rb + 2 * QB, DIL_W), F32)],
        compiler_params=_cparams(("parallel",)),
    )(qkr, qkr, qkv, qkr, qkv, qkr, dya, dya, lse, lse, dlt, dlt, tc, ts1, ts2)


def _dil_merge(o_g, l_g, s):
    def fn(o0, l0, o1, l1, o2, l2):
        m = jnp.maximum(jnp.maximum(l0, l1), l2)
        e0, e1, e2 = jnp.exp(l0 - m), jnp.exp(l1 - m), jnp.exp(l2 - m)
        den = e0 + e1 + e2
        return [(e0 * o0 + e1 * o1 + e2 * o2) / den, m + jnp.log(den)], []

    rows = [(a, DIL_W, 0) for pair in zip(o_g, l_g) for a in pair]
    (ya, lse), _ = _rowwise("dil_merge", fn, s, min(512, s), rows, [], [(DIL_W, BF16), (DIL_W, F32)])
    return ya, lse


def _dil_bwd_prep(d_ya, ya, s):
    def fn(dya, yav):
        lane_lo = lax.broadcasted_iota(jnp.int32, (dya.shape[0], 128), 1) < 64
        parts = []
        for cc in range(4):
            prod = dya[:, cc * 128:(cc + 1) * 128] * yav[:, cc * 128:(cc + 1) * 128].astype(F32)
            lo = jnp.where(lane_lo, prod, 0.0)
            s_lo = jnp.sum(lo, axis=1, keepdims=True)
            s_hi = jnp.sum(prod - lo, axis=1, keepdims=True)
            parts.append(jnp.where(lane_lo, s_lo, s_hi))
        return [dya, jnp.concatenate(parts, axis=1)], []

    (dyb, dlt), _ = _rowwise("dil_bwd_prep", fn, s, min(512, s), [(d_ya, DIL_W, 0), (ya, DIL_W, 0)], [],
                             [(DIL_W, BF16), (DIL_W, F32)])
    return dyb, dlt


_RET_ROWS = [(256 * h, 256) for h in range(4)], [(1024 + 256 * h, 256) for h in range(4)], \
            [(2048 + 512 * h, 512) for h in range(4)], [(4096 + 512 * h, 512) for h in range(4)]


def _split_w_in(win):
    rows = lambda a, n: win[a:a + n]
    w_ret = jnp.concatenate([rows(*seg[h]) for h in range(RET_HEADS) for seg in _RET_ROWS], axis=0)
    w_dil = [jnp.concatenate([rows(base + 512 * g, 512) for base in (6144, 7680, 9216)], axis=0) for g in range(3)]
    return w_ret, win[10752:12800], w_dil


def _join_w_in(g_ret, g_gate, g_dil):
    parts = []
    for i in range(4):
        off = (0, 256, 512, 1024)[i]
        parts += [g_ret[1536 * h + off:1536 * h + off + _RET_ROWS[i][h][1]] for h in range(RET_HEADS)]
    for i in range(3):
        parts += [g_dil[g][512 * i:512 * (i + 1)] for g in range(3)]
    return jnp.concatenate(parts + [g_gate], axis=0)


def _local_step(xs, pb, tgt, tabs, wts, vec, s):
    tm = min(1024, s)
    tr = min(256, s)
    mm = functools.partial(_matmul, tm=tm)
    w_ret, w_gate, w_dil = _split_w_in(wts["w_in"])

    (u_nat,), _ = _rowwise("prenorm", lambda xv, g: ([_rms(xv) * g], []), s, tr, [(xs, 1024, 0)], [vec["g_pre_mix"]], [(1024, BF16)])
    u = [_to_streams(u_nat, dil) for dil in DIL_GROUPS]
    proj_ret = mm(u[0], w_ret, mode="nt", m=s, n=6144, k=1024, tn=1024, tk=1024, out_dtype=BF16, name="inproj_ret")
    proj_gate = mm(u[0], w_gate, mode="nt", m=s, n=2048, k=1024, tn=1024, tk=1024, out_dtype=BF16, name="inproj_gate")
    qkv = [mm(u[g], w_dil[g], mode="nt", m=s, n=1536, k=1024, tn=512, tk=1024, out_dtype=BF16, name="inproj_dil%d" % g)
           for g in range(3)]

    yr, y_ret, rstate = _ret_fwd(proj_ret, tabs["cos_r"], tabs["sin_r"], s)
    a_br = mm(yr, wts["w_ret_out"], mode="nn", m=s, n=1024, k=2048, tn=1024, tk=1024, out_dtype=BF16, name="ret_out")

    qkr, o_g, l_g = [], [], []
    for g, dil in enumerate(DIL_GROUPS):
        qkr.append(_rope_qk(qkv[g], *tabs["dil"][g], s, "rope_qk%d" % g))
        o, l = _dil_fwd(qkr[g], qkv[g], dil, s, "dil_fwd%d" % g)
        o_g.append(_from_streams(o, dil))
        l_g.append(_from_streams(l, dil))
    ya, lse = _dil_merge(o_g, l_g, s)
    b_br = mm(ya, wts["w_dil_out"], mode="nt", m=s, n=1024, k=512, tn=1024, tk=512, out_dtype=BF16, name="dil_out")

    def gate_mix(a, b, gr, ga, b0, b1):
        return [_sigmoid(gr.astype(F32) + b0) * a.astype(F32) + _sigmoid(ga.astype(F32) + b1) * b.astype(F32)], []

    (mixed,), _ = _rowwise("gate_mix", gate_mix, s, tr, [(a_br, 1024, 0), (b_br, 1024, 0), (proj_gate, 1024, 0), (proj_gate, 1024, 1)],
                           [vec["b0"], vec["b1"]], [(1024, BF16)])
    z = mm(mixed, wts["w_o"], mode="nn", m=s, n=1024, k=1024, tn=1024, tk=1024, out_dtype=F32, name="w_o")

    def post_norm(h, f, g_post, g_pre):
        hn = h + _rms(f) * g_post
        return [hn, _rms(hn) * g_pre], []

    (h1, v2), _ = _rowwise("post_mix", post_norm, s, tr, [(xs, 1024, 0), (z, 1024, 0)], [vec["g_post_mix"], vec["g_pre_mlp"]],
                           [(1024, F32), (1024, BF16)])
    a_up = mm(v2, wts["w_up"], mode="nt", m=s, n=4096, k=1024, tn=512, tk=1024, out_dtype=BF16, name="mlp_up")
    f_dn = mm(a_up, wts["w_down"], mode="nn", m=s, n=1024, k=4096, tn=1024, tk=1024, out_dtype=F32, name="mlp_down", a_fn=_relu_sq)
    (h2, t_ple), _ = _rowwise("post_mlp", post_norm, s, tr, [(h1, 1024, 0), (f_dn, 1024, 0)], [vec["g_post_mlp"], vec["g_pre_ple"]],
                              [(1024, F32), (1024, BF16)])
    gl = mm(t_ple, wts["w_ple_gate"], mode="nn", m=s, n=1024, k=1024, tn=1024, tk=1024, out_dtype=F32, name="ple_gate")
    e_ple = mm(pb, wts["w_ple_in"], mode="nt", m=s, n=1024, k=256, tn=1024, tk=256, out_dtype=F32, name="ple_in")

    def ple_loss(h, glv, e, tg, b, g):
        gate = _sigmoid(glv + b)
        ge = gate * e
        diff = h + _rms(ge) * g - tg
        dy = diff * (1.0 / D_MODEL)
        d_ge, dg = _rms_bwd(ge, g, dy)
        d_gl = d_ge * e * gate * (1.0 - gate)
        loss = jnp.zeros((1, D_MODEL), F32) + 0.5 * jnp.sum(diff * diff) * (1.0 / D_MODEL)
        return [dy, d_gl, d_ge * gate], [_colsum(dg), _colsum(d_gl), loss]

    (dy, d_gl, d_e), (dg_post_ple, db_ple, loss) = _rowwise(
        "ple_loss", ple_loss, s, tr, [(h2, 1024, 0), (gl, 1024, 0), (e_ple, 1024, 0), (tgt, 1024, 0)],
        [vec["b_ple"], vec["g_post_ple"]], [(1024, F32), (1024, BF16), (1024, BF16)], [1024, 1024, 1024])

    ts = min(1024, s)
    wg = functools.partial(_matmul, mode="tn", k=s, tk=ts, out_dtype=F32)
    grads = {}
    grads["w_ple_in"] = wg(d_e, pb, m=1024, n=256, tm=1024, tn=256, name="g_ple_in")
    grads["w_ple_gate"] = wg(t_ple, d_gl, m=1024, n=1024, tm=1024, tn=1024, name="g_ple_gate")
    d_t = mm(d_gl, wts["w_ple_gate"], mode="nt", m=s, n=1024, k=1024, tn=1024, tk=1024, out_dtype=F32, name="d_t")

    def bwd_ple_mlp(h, dt, dyv, f, g_pre, g_post):
        dx, dg1 = _rms_bwd(h, g_pre, dt)
        dh = dyv + dx
        df, dg2 = _rms_bwd(f, g_post, dh)
        return [dh, df], [_colsum(dg1), _colsum(dg2)]

    (d_h2, d_f), (dg_pre_ple, dg_post_mlp) = _rowwise(
        "bwd_ple_mlp", bwd_ple_mlp, s, tr, [(h2, 1024, 0), (d_t, 1024, 0), (dy, 1024, 0), (f_dn, 1024, 0)],
        [vec["g_pre_ple"], vec["g_post_mlp"]], [(1024, F32), (1024, BF16)], [1024, 1024])
    d_a = mm(d_f, wts["w_down"], mode="nt", m=s, n=4096, k=1024, tn=512, tk=1024, out_dtype=BF16, name="d_a",
             epi=a_up, epi_fn=lambda acc, av: acc * (2.0 * jnp.maximum(av.astype(F32), 0.0)))
    grads["w_down"] = wg(a_up, d_f, m=4096, n=1024, tm=1024, tn=1024, name="g_down", a_fn=_relu_sq)
    grads["w_up"] = wg(d_a, v2, m=4096, n=1024, tm=1024, tn=1024, name="g_up")
    d_v2 = mm(d_a, wts["w_up"], mode="nn", m=s, n=1024, k=4096, tn=1024, tk=1024, out_dtype=F32, name="d_v2")

    (d_h1, d_z), (dg_pre_mlp, dg_post_mix) = _rowwise(
        "bwd_mlp_mix", bwd_ple_mlp, s, tr, [(h1, 1024, 0), (d_v2, 1024, 0), (d_h2, 1024, 0), (z, 1024, 0)],
        [vec["g_pre_mlp"], vec["g_post_mix"]], [(1024, F32), (1024, BF16)], [1024, 1024])
    d_mixed = mm(d_z, wts["w_o"], mode="nt", m=s, n=1024, k=1024, tn=1024, tk=1024, out_dtype=F32, name="d_mixed")
    grads["w_o"] = wg(mixed, d_z, m=1024, n=1024, tm=1024, tn=1024, name="g_o")

    def bwd_gate(dm, a, b, gr, ga, b0, b1):
        sa, sb = _sigmoid(gr.astype(F32) + b0), _sigmoid(ga.astype(F32) + b1)
        dgr = dm * a.astype(F32) * sa * (1.0 - sa)
        dga = dm * b.astype(F32) * sb * (1.0 - sb)
        return [dm * sa, dm * sb, jnp.concatenate([dgr, dga], axis=1)], [_colsum(dgr), _colsum(dga)]

    (d_abr, d_bbr, dproj_gate), (db0, db1) = _rowwise(
        "bwd_gate", bwd_gate, s, tr, [(d_mixed, 1024, 0), (a_br, 1024, 0), (b_br, 1024, 0), (proj_gate, 1024, 0), (proj_gate, 1024, 1)],
        [vec["b0"], vec["b1"]], [(1024, BF16), (1024, BF16), (2048, BF16)], [1024, 1024])
    grads["w_ret_out"] = wg(yr, d_abr, m=2048, n=1024, tm=1024, tn=1024, name="g_ret_out")
    d_yr = mm(d_abr, wts["w_ret_out"], mode="nt", m=s, n=2048, k=1024, tn=512, tk=1024, out_dtype=BF16, name="d_yr")
    grads["w_dil_out"] = wg(d_bbr, ya, m=1024, n=512, tm=1024, tn=512, name="g_dil_out")
    d_ya = mm(d_bbr, wts["w_dil_out"], mode="nn", m=s, n=512, k=1024, tn=512, tk=1024, out_dtype=F32, name="d_ya")

    dproj_ret = _ret_bwd(proj_ret, tabs["cos_r"], tabs["sin_r"], y_ret, d_yr, rstate, s)
    dyb, dlt = _dil_bwd_prep(d_ya, ya, s)
    dqkv = [_dil_bwd(qkr[g], qkv[g], _to_streams(dyb, dil), _to_streams(lse, dil), _to_streams(dlt, dil), *tabs["dil"][g],
                     dil, s, "dil_bwd%d" % g) for g, dil in enumerate(DIL_GROUPS)]

    g_ret = wg(dproj_ret, u[0], m=6144, n=1024, tm=2048, tn=1024, name="g_in_ret")
    g_gate = wg(dproj_gate, u[0], m=2048, n=1024, tm=2048, tn=1024, name="g_in_gate")
    g_dil = [wg(dqkv[g], u[g], m=1536, n=1024, tm=1536, tn=1024, name="g_in_dil%d" % g) for g in range(3)]
    grads["w_in"] = _join_w_in(g_ret, g_gate, g_dil)

    du_ret = mm(dproj_ret, w_ret, mode="nn", m=s, n=1024, k=6144, tn=1024, tk=1024, out_dtype=F32, name="du_ret")
    du_gate = mm(dproj_gate, w_gate, mode="nn", m=s, n=1024, k=2048, tn=1024, tk=1024, out_dtype=F32, name="du_gate")
    du_dil = [mm(dqkv[g], w_dil[g], mode="nn", m=s, n=1024, k=1536, tn=1024, tk=512, out_dtype=F32, name="du_dil%d" % g)
              for g in range(3)]

    def grad_x_fn(xv, dh, d0, d1, d2, d3, d4, g):
        dx, dg = _rms_bwd(xv, g, d0 + d1 + d2 + d3 + d4)
        return [dh + dx], [_colsum(dg)]

    du_all = [du_ret, du_gate] + [_from_streams(du_dil[g], dil) for g, dil in enumerate(DIL_GROUPS)]
    (grad_x,), (dg_pre_mix,) = _rowwise("grad_x", grad_x_fn, s, tr, [(a, 1024, 0) for a in [xs, d_h1] + du_all],
                                        [vec["g_pre_mix"]], [(1024, F32)], [1024])

    zero = jnp.zeros((1, D_MODEL), F32)
    packet = jnp.concatenate([dg_pre_mix, dg_post_mix, dg_pre_mlp, dg_post_mlp, dg_pre_ple, db_ple, dg_post_ple, loss,
                              db0, db1] + [zero] * 6, axis=0)
    return grad_x, grads, packet


def _mesh_pos():
    return lax.axis_index("x"), lax.axis_index("y"), lax.axis_index("c")


def _all_gather(shards):
    nw = len(shards)

    def body(*refs):
        ins, outs = refs[:nw], refs[nw:2 * nw]
        send_sems, recv_sems, local_sems = refs[2 * nw:]
        x, y, c = _mesh_pos()
        me, sibling = (x, y, c), (x, y, 1 - c)
        chips = [(1 - x, y), (x, 1 - y), (1 - x, 1 - y)]

        def region(w, dev):
            return outs[w].at[4 * dev[0] + 2 * dev[1] + dev[2]]

        def copy(w, kk, block, to, src=None):
            return pltpu.make_async_remote_copy(
                src_ref=region(w, block) if src is None else src, dst_ref=region(w, block),
                send_sem=send_sems.at[w * 7 + kk], recv_sem=recv_sems.at[w * 7 + kk], device_id=to, device_id_type=MESH)

        mine = [pltpu.make_async_copy(ins[w], region(w, me), local_sems.at[w]) for w in range(nw)]
        for cp in mine:
            cp.start()
        first = []
        for w in range(nw):
            first.append(copy(w, 0, me, sibling, src=ins[w]))
            first += [copy(w, 1 + j, me, (*chip, c), src=ins[w]) for j, chip in enumerate(chips)]
        for cp in first:
            cp.start()
        passed = []
        for j, chip in enumerate(chips):
            for w in range(nw):
                copy(w, 1 + j, (*chip, c), me).wait_recv()
                cp = copy(w, 4 + j, (*chip, c), sibling)
                cp.start()
                passed.append(cp)
        for w in range(nw):
            copy(w, 0, sibling, me).wait_recv()
            for j, chip in enumerate(chips):
                copy(w, 4 + j, (*chip, 1 - c), me).wait_recv()
        for cp in first + passed:
            cp.wait_send()
        for cp in mine:
            cp.wait()

    hbm = pl.BlockSpec(memory_space=pl.ANY)
    return pl.pallas_call(
        body, name="gather_weights",
        in_specs=[hbm] * nw, out_specs=[hbm] * nw,
        out_shape=[jax.ShapeDtypeStruct((N_DEV,) + sh.shape, sh.dtype) for sh in shards],
        scratch_shapes=[pltpu.SemaphoreType.DMA((nw * 7,)), pltpu.SemaphoreType.DMA((nw * 7,)), pltpu.SemaphoreType.DMA((nw,))],
    )(*shards)


def _exchange_grads(full_grads, packet):
    nw = len(full_grads)
    ni = nw + 1

    def body(*refs):
        srcs, dsts = refs[:ni], refs[ni:2 * ni]
        send_sems, recv_sems, local_sems = refs[2 * ni:]
        x, y, c = _mesh_pos()
        my = 4 * x + 2 * y + c

        def src_of(w, idx):
            return srcs[w] if w == nw else srcs[w].at[idx]

        local = [pltpu.make_async_copy(src_of(w, my), dsts[w].at[my], local_sems.at[w]) for w in range(ni)]
        for cp in local:
            cp.start()
        sends, recvs = [], []
        for w in range(ni):
            for r in range(1, N_DEV):
                px = 1 - x if r & 4 else x
                py = 1 - y if r & 2 else y
                pc = 1 - c if r & 1 else c
                pidx = 4 * px + 2 * py + pc
                sem = w * 7 + r - 1
                sends.append(pltpu.make_async_remote_copy(
                    src_ref=src_of(w, pidx), dst_ref=dsts[w].at[my], send_sem=send_sems.at[sem], recv_sem=recv_sems.at[sem],
                    device_id=(px, py, pc), device_id_type=MESH))
                recvs.append(pltpu.make_async_remote_copy(
                    src_ref=src_of(w, pidx), dst_ref=dsts[w].at[pidx], send_sem=send_sems.at[sem], recv_sem=recv_sems.at[sem],
                    device_id=(px, py, pc), device_id_type=MESH))
        for cp in sends:
            cp.start()
        for cp in recvs:
            cp.wait_recv()
        for cp in sends:
            cp.wait_send()
        for cp in local:
            cp.wait()

    hbm = pl.BlockSpec(memory_space=pl.ANY)
    out_shape = [jax.ShapeDtypeStruct(g.shape, g.dtype) for g in full_grads]
    out_shape.append(jax.ShapeDtypeStruct((N_DEV,) + packet.shape, packet.dtype))
    res = pl.pallas_call(
        body, name="exchange_grads",
        in_specs=[hbm] * ni, out_specs=[hbm] * ni, out_shape=out_shape,
        scratch_shapes=[pltpu.SemaphoreType.DMA((ni * 7,)), pltpu.SemaphoreType.DMA((ni * 7,)), pltpu.SemaphoreType.DMA((ni,))],
    )(*full_grads, packet)
    return res[:nw], res[nw]


def _pick_rows(r, c, target_bytes):
    t = r
    while t % 2 == 0 and t // 2 >= 8 and (t // 2) % 8 == 0 and t * c * 4 > target_bytes:
        t //= 2
    return t


def _sum_slots(slots, name):
    ns, r, c = slots.shape
    tr = _pick_rows(r, c, 256 * 1024)

    def body(s_ref, o_ref):
        acc = s_ref[0]
        for kk in range(1, ns):
            acc = acc + s_ref[kk]
        o_ref[...] = acc

    return pl.pallas_call(
        body, name=name, grid=(r // tr,),
        in_specs=[pl.BlockSpec((ns, tr, c), lambda i: (0, i, 0))], out_specs=pl.BlockSpec((tr, c), lambda i: (i, 0)),
        out_shape=jax.ShapeDtypeStruct((r, c), F32), compiler_params=_cparams(("parallel",)),
    )(slots)


def _adamw(slots, w, m, v, name):
    ns, r, c = slots.shape
    tr = _pick_rows(r, c, 256 * 1024)

    def body(s_ref, w_ref, m_ref, v_ref, g_out, d_out, m_out, v_out):
        g = s_ref[0]
        for kk in range(1, ns):
            g = g + s_ref[kk]
        mn = ADAM_B1 * m_ref[...] + (1.0 - ADAM_B1) * g
        vn = ADAM_B2 * v_ref[...] + (1.0 - ADAM_B2) * (g * g)
        m_hat = mn / (1.0 - ADAM_B1 ** ADAM_STEP)
        v_hat = vn / (1.0 - ADAM_B2 ** ADAM_STEP)
        g_out[...] = g
        d_out[...] = -ADAM_LR * (m_hat / (jnp.sqrt(v_hat) + ADAM_EPS) + ADAM_WD * w_ref[...])
        m_out[...] = mn
        v_out[...] = vn

    blk = pl.BlockSpec((tr, c), lambda i: (i, 0))
    return pl.pallas_call(
        body, name=name, grid=(r // tr,),
        in_specs=[pl.BlockSpec((ns, tr, c), lambda i: (0, i, 0)), blk, blk, blk], out_specs=[blk] * 4,
        out_shape=[jax.ShapeDtypeStruct((r, c), F32)] * 4, compiler_params=_cparams(("parallel",)),
    )(slots, w, m, v)


def _rotary_tables(pos, s):
    posf = pos.astype(F32)
    inv_freq = 1.0 / (10000.0 ** jnp.linspace(0.0, 1.0, RET_QK // 2, dtype=F32))
    ang = posf[:, None] * inv_freq
    tabs = {"cos_r": jnp.cos(ang), "sin_r": jnp.sin(ang), "dil": []}
    freqs = 500000.0 ** (-jnp.arange(0, 16, 2, dtype=F32) / 16)
    z8, z48 = jnp.zeros((8,), F32), jnp.zeros((48,), F32)
    f_c = jnp.tile(jnp.concatenate([freqs, freqs, z48]), 2)
    f_1 = jnp.tile(jnp.concatenate([freqs, z8, z48]), 2)
    f_2 = jnp.tile(jnp.concatenate([z8, freqs, z48]), 2)
    for dil in DIL_GROUPS:
        pp = posf.reshape(s // dil, dil).T.reshape(s, 1)
        tabs["dil"].append((jnp.cos(pp * f_c), -jnp.sin(pp * f_1), jnp.sin(pp * f_2)))
    return tabs


_TRANSPOSED = ("w_in", "w_dil_out", "w_up", "w_ple_in")
_MATS = ("w_in", "w_ret_out", "w_dil_out", "w_o", "w_up", "w_down", "w_ple_gate", "w_ple_in")
_VECS = ("g_pre_mix", "g_post_mix", "g_pre_mlp", "g_post_mlp", "g_pre_ple", "b_ple_gate", "g_post_ple")
_ORDER = ("w_in", "b_gate", "w_ret_out", "w_dil_out", "w_o", "g_pre_mix", "g_post_mix", "g_pre_mlp", "g_post_mlp", "w_up",
          "w_down", "g_pre_ple", "w_ple_gate", "b_ple_gate", "w_ple_in", "g_post_ple")


def kernel(x, p, positions, w_in, b_gate, w_ret_out, w_dil_out, w_o, g_pre_mix, g_post_mix, g_pre_mlp, g_post_mlp, w_up, w_down, g_pre_ple, w_ple_gate, b_ple_gate, w_ple_in, g_post_ple, loss_target, m_w_in, m_b_gate, m_w_ret_out, m_w_dil_out, m_w_o, m_g_pre_mix, m_g_post_mix, m_g_pre_mlp, m_g_post_mlp, m_w_up, m_w_down, m_g_pre_ple, m_w_ple_gate, m_b_ple_gate, m_w_ple_in, m_g_post_ple, v_w_in, v_b_gate, v_w_ret_out, v_w_dil_out, v_w_o, v_g_pre_mix, v_g_post_mix, v_g_pre_mlp, v_g_post_mlp, v_w_up, v_w_down, v_g_pre_ple, v_w_ple_gate, v_b_ple_gate, v_w_ple_in, v_g_post_ple):
    s = x.shape[1]
    wd = dict(w_in=w_in, b_gate=b_gate, w_ret_out=w_ret_out, w_dil_out=w_dil_out, w_o=w_o, g_pre_mix=g_pre_mix,
              g_post_mix=g_post_mix, g_pre_mlp=g_pre_mlp, g_post_mlp=g_post_mlp, w_up=w_up, w_down=w_down,
              g_pre_ple=g_pre_ple, w_ple_gate=w_ple_gate, b_ple_gate=b_ple_gate, w_ple_in=w_ple_in, g_post_ple=g_post_ple)
    md = dict(w_in=m_w_in, b_gate=m_b_gate, w_ret_out=m_w_ret_out, w_dil_out=m_w_dil_out, w_o=m_w_o, g_pre_mix=m_g_pre_mix,
              g_post_mix=m_g_post_mix, g_pre_mlp=m_g_pre_mlp, g_post_mlp=m_g_post_mlp, w_up=m_w_up, w_down=m_w_down,
              g_pre_ple=m_g_pre_ple, w_ple_gate=m_w_ple_gate, b_ple_gate=m_b_ple_gate, w_ple_in=m_w_ple_in, g_post_ple=m_g_post_ple)
    vd = dict(w_in=v_w_in, b_gate=v_b_gate, w_ret_out=v_w_ret_out, w_dil_out=v_w_dil_out, w_o=v_w_o, g_pre_mix=v_g_pre_mix,
              g_post_mix=v_g_post_mix, g_pre_mlp=v_g_pre_mlp, g_post_mlp=v_g_post_mlp, w_up=v_w_up, w_down=v_w_down,
              g_pre_ple=v_g_pre_ple, w_ple_gate=v_w_ple_gate, b_ple_gate=v_b_ple_gate, w_ple_in=v_w_ple_in, g_post_ple=v_g_post_ple)

    shards = [(wd[n][0].T if n in _TRANSPOSED else wd[n][0]).astype(BF16) for n in _MATS]
    gathered = _all_gather(shards + [b_gate[0]])
    wts = {n: g.reshape(N_DEV * g.shape[1], g.shape[2]) for n, g in zip(_MATS, gathered[:-1])}
    bg = gathered[-1].transpose(1, 0, 2).reshape(2, D_MODEL)
    vec = {n: wd[n] for n in _VECS}
    vec.update(b0=bg[0:1], b1=bg[1:2], b_ple=b_ple_gate)

    tabs = _rotary_tables(positions[0], s)
    grad_x, grads, packet = _local_step(x[0], p[0, 0].astype(BF16), loss_target[0], tabs, wts, vec, s)

    full = [grads[n].reshape(N_DEV, grads[n].shape[0] // N_DEV, grads[n].shape[1]) for n in _MATS]
    slots, packets = _exchange_grads(full, packet)
    out = {}
    for n, sl in zip(_MATS, slots):
        if n in _TRANSPOSED:
            sl = _sum_slots(sl, "sum_" + n).T[None]
        out[n] = _adamw(sl, wd[n][0], md[n][0], vd[n][0], "adamw_" + n)
    zero_rows = jnp.zeros((16 - len(_VECS), D_MODEL), F32)
    pack = lambda d: jnp.concatenate([d[n] for n in _VECS] + [zero_rows], axis=0)
    small = _adamw(packets, pack(wd), pack(md), pack(vd), "adamw_vectors")
    for i, n in enumerate(_VECS):
        out[n] = tuple(t[i:i + 1] for t in small)
    my = 4 * lax.axis_index("x") + 2 * lax.axis_index("y") + lax.axis_index("c")
    g_bias = lax.dynamic_slice(small[0], (8, my * 128), (2, 128))
    out["b_gate"] = _adamw(g_bias[None], b_gate[0], m_b_gate[0], v_b_gate[0], "adamw_b_gate")
    loss = small[0][7, 0]

    res = [loss, grad_x[None]]
    for kk in range(4):
        res += [out[n][kk][None] if out[n][kk].ndim == 2 and wd[n].ndim == 3 else out[n][kk] for n in _ORDER]
    return tuple(res)
```

```python
import functools
import math

import numpy as np
import jax
import jax.numpy as jnp
from jax import lax
from jax.experimental import pallas as pl
from jax.experimental.pallas import tpu as pltpu

F32, BF16 = jnp.float32, jnp.bfloat16
D_MODEL = 1024
EPS = 1e-6
N_DEV = 8
RET_HEADS, RET_QK, RET_V, RET_CHUNK = 4, 256, 512, 128
DIL_GROUPS = (1, 4, 16)
DIL_W = 512
QB = 128
NEG = -1e30
ADAM_LR, ADAM_B1, ADAM_B2, ADAM_EPS, ADAM_WD, ADAM_STEP = 0.001, 0.9, 0.999, 1e-08, 0.01, 10
VMEM_LIMIT_BYTES = 56 * 1024 * 1024
MESH = pl.DeviceIdType.MESH

NN = ((1,), (0,))
NT = ((1,), (1,))
TN = ((0,), (0,))


def _dot(a, b, dn):
    return lax.dot_general(a, b, (dn, ((), ())), preferred_element_type=F32)


def _cparams(sem):
    return pltpu.CompilerParams(dimension_semantics=sem, vmem_limit_bytes=VMEM_LIMIT_BYTES)


def _rms(x):
    return x * lax.rsqrt(jnp.mean(x * x, axis=-1, keepdims=True) + EPS)


def _rms_bwd(x, g, dy):
    r = lax.rsqrt(jnp.mean(x * x, axis=-1, keepdims=True) + EPS)
    xh = x * r
    t = dy * g
    dx = r * (t - xh * jnp.mean(t * xh, axis=-1, keepdims=True))
    return dx, dy * xh


def _colsum(v):
    return jnp.sum(v, axis=0, keepdims=True)


def _sigmoid(v):
    return 1.0 / (1.0 + jnp.exp(-v))


def _matmul(a, b, *, mode, m, n, k, tm, tn, tk, out_dtype, name, a_fn=None, epi=None, epi_fn=None, carry=None):
    nk = k // tk
    grid = (m // tm, n // tn, nk)
    nc = carry.n if carry is not None else 0
    if mode == "nn":
        a_blk, a_im, b_blk, b_im, dn = (tm, tk), (lambda i, j, kk: (i, kk)), (tk, tn), (lambda i, j, kk: (kk, j)), NN
    elif mode == "nt":
        a_blk, a_im, b_blk, b_im, dn = (tm, tk), (lambda i, j, kk: (i, kk)), (tn, tk), (lambda i, j, kk: (j, kk)), NT
    else:
        a_blk, a_im, b_blk, b_im, dn = (tk, tm), (lambda i, j, kk: (kk, i)), (tk, tn), (lambda i, j, kk: (kk, j)), TN
    o_im = lambda i, j, kk: (i, j)
    n_in = 2 + (epi is not None)

    def body(*refs):
        a_ref, b_ref = refs[0], refs[1]
        e_ref = refs[2] if epi is not None else None
        o_ref = refs[n_in + nc]
        acc_ref = refs[n_in + 2 * nc + 1] if nk > 1 else None
        if carry is not None:
            step = (pl.program_id(0) * grid[1] + pl.program_id(1)) * nk + pl.program_id(2)
            pl.when(step == 0)(lambda: carry.start(*carry.split(refs, n_in, 1)))

        def finish(acc):
            if e_ref is not None:
                acc = epi_fn(acc, e_ref[...])
            o_ref[...] = acc.astype(o_ref.dtype)

        av = a_ref[...]
        if a_fn is not None:
            av = a_fn(av)
        part = _dot(av, b_ref[...], dn)
        if nk == 1:
            finish(part)
        else:
            kk = pl.program_id(2)

            @pl.when(kk == 0)
            def _():
                acc_ref[...] = part

            @pl.when(kk > 0)
            def _():
                acc_ref[...] += part

            @pl.when(kk == nk - 1)
            def _():
                finish(acc_ref[...])

        if carry is not None:
            pl.when(step == grid[0] * grid[1] * nk - 1)(lambda: carry.wait(*carry.split(refs, n_in, 1)))

    in_specs = [pl.BlockSpec(a_blk, a_im), pl.BlockSpec(b_blk, b_im)]
    args = [a, b]
    if epi is not None:
        in_specs.append(pl.BlockSpec((tm, tn), o_im))
        args.append(epi)
    out_specs = pl.BlockSpec((tm, tn), o_im)
    out_shape = jax.ShapeDtypeStruct((m, n), out_dtype)
    scratch = [pltpu.VMEM((tm, tn), F32)] if nk > 1 else []
    if carry is None:
        return pl.pallas_call(
            body, name=name, grid=grid, in_specs=in_specs, out_specs=out_specs, out_shape=out_shape,
            scratch_shapes=scratch, compiler_params=_cparams(("parallel", "parallel", "arbitrary")),
        )(*args)
    res = pl.pallas_call(
        body, name=name, grid=grid, in_specs=in_specs + carry.specs, out_specs=[out_specs] + carry.specs,
        out_shape=[out_shape] + carry.out_shape, scratch_shapes=scratch + carry.scratch,
        compiler_params=_cparams(("arbitrary", "arbitrary", "arbitrary")),
    )(*args, *carry.arrays)
    return res[0], res[1:]


def _relu_sq(v):
    r = jnp.maximum(v.astype(F32), 0.0)
    return (r * r).astype(BF16)


def _rowwise(name, fn, s, tr, rows, vecs, outs, accs=()):
    n_r, n_v, n_o, n_a = len(rows), len(vecs), len(outs), len(accs)

    def body(*refs):
        vals = [refs[i][...] for i in range(n_r + n_v)]
        o_refs = refs[n_r + n_v:n_r + n_v + n_o]
        a_refs = refs[n_r + n_v + n_o:]
        o_vals, a_vals = fn(*vals)
        for ref, val in zip(o_refs, o_vals):
            ref[...] = val.astype(ref.dtype)
        if n_a:
            @pl.when(pl.program_id(0) == 0)
            def _():
                for ref in a_refs:
                    ref[...] = jnp.zeros_like(ref)

            for ref, val in zip(a_refs, a_vals):
                ref[...] += val

    in_specs = [pl.BlockSpec((tr, w), functools.partial(lambda i, cb: (i, cb), cb=cb)) for _, w, cb in rows]
    in_specs += [pl.BlockSpec(v.shape, lambda i: (0, 0)) for v in vecs]
    out_specs = [pl.BlockSpec((tr, w), lambda i: (i, 0)) for w, _ in outs]
    out_specs += [pl.BlockSpec((1, w), lambda i: (0, 0)) for w in accs]
    out_shape = [jax.ShapeDtypeStruct((s, w), dt) for w, dt in outs]
    out_shape += [jax.ShapeDtypeStruct((1, w), F32) for w in accs]
    res = pl.pallas_call(
        body, name=name, grid=(s // tr,), in_specs=in_specs, out_specs=out_specs, out_shape=out_shape,
        compiler_params=_cparams(("arbitrary",)),
    )(*[r[0] for r in rows], *vecs)
    return res[:n_o], res[n_o:]


def _to_streams(a, dil):
    if dil == 1:
        return a
    s, w = a.shape
    return a.reshape(s // dil, dil, w).transpose(1, 0, 2).reshape(s, w)


def _from_streams(a, dil):
    if dil == 1:
        return a
    s, w = a.shape
    return a.reshape(dil, s // dil, w).transpose(1, 0, 2).reshape(s, w)


def _ret_tables():
    h = np.arange(RET_HEADS, dtype=np.float32)
    lg = np.log1p(-(np.float32(2.0) ** (-5.0 - h))).astype(np.float32)
    idx = np.arange(RET_CHUNK, dtype=np.float32)
    diff = idx[:, None] - idx[None, :]
    dm = np.where(diff[None] >= 0, np.exp(np.maximum(diff, 0.0)[None] * lg[:, None, None]), 0.0)
    qd = np.exp((idx + 1.0)[None, :, None] * lg[:, None, None])
    kd = np.exp((RET_CHUNK - 1.0 - idx)[None, :, None] * lg[:, None, None])
    cd = np.exp(RET_CHUNK * lg)[:, None, None]
    return [jnp.asarray(t, F32) for t in (dm, qd, kd, cd)]


def _rope_half(v, cos, sin):
    v1, v2 = v[:, :128], v[:, 128:]
    return jnp.concatenate([v1 * cos - v2 * sin, v2 * cos + v1 * sin], axis=1)


def _unrope_half(d, cos, sin):
    d1, d2 = d[:, :128], d[:, 128:]
    return jnp.concatenate([d1 * cos + d2 * sin, d2 * cos - d1 * sin], axis=1)


def _ret_specs(rb, rev_n):
    def rowmap(w_blk):
        return lambda h, n: (rev_n(n), w_blk(h))
    tab = [pl.BlockSpec((1, RET_CHUNK, RET_CHUNK), lambda h, n: (h, 0, 0)),
           pl.BlockSpec((1, RET_CHUNK, 1), lambda h, n: (h, 0, 0)),
           pl.BlockSpec((1, RET_CHUNK, 1), lambda h, n: (h, 0, 0)),
           pl.BlockSpec((1, 1, 1), lambda h, n: (h, 0, 0))]
    proj = pl.BlockSpec((rb, 1536), rowmap(lambda h: h))
    cs = pl.BlockSpec((rb, 128), rowmap(lambda h: 0))
    hv = pl.BlockSpec((rb, RET_V), rowmap(lambda h: h))
    return proj, cs, hv, tab


def _ret_fwd(proj_ret, cos, sin, s):
    rb = min(512, s)
    ch = rb // RET_CHUNK
    nb = s // rb
    proj_spec, cs_spec, hv_spec, tab_specs = _ret_specs(rb, lambda n: n)

    def body(p_ref, cos_ref, sin_ref, dm_ref, qd_ref, kd_ref, cd_ref, yr_ref, y_ref, rs_ref, r_acc):
        @pl.when(pl.program_id(1) == 0)
        def _():
            r_acc[...] = jnp.zeros_like(r_acc)

        dm, qd, kd, cd = dm_ref[0], qd_ref[0], kd_ref[0], cd_ref[0]
        for c in range(ch):
            rows = slice(c * RET_CHUNK, (c + 1) * RET_CHUNK)
            cosv, sinv = cos_ref[rows, :], sin_ref[rows, :]
            q = _rope_half(p_ref[rows, 0:256].astype(F32), cosv, sinv)
            kk = _rope_half(p_ref[rows, 256:512].astype(F32), cosv, sinv) * (RET_QK ** -0.5)
            v = p_ref[rows, 512:1024]
            g = p_ref[rows, 1024:1536].astype(F32)
            rb16 = r_acc[...].astype(BF16)
            rs_ref[0, c] = rb16
            sc = _dot(q.astype(BF16), kk.astype(BF16), NT) * dm
            y = _dot(sc.astype(BF16), v, NN) + _dot((q * qd).astype(BF16), rb16, NN)
            r_acc[...] = r_acc[...] * cd + _dot((kk * kd).astype(BF16), v, TN)
            y_ref[rows, :] = y.astype(BF16)
            yr_ref[rows, :] = (_rms(y) * (g * _sigmoid(g))).astype(BF16)

    return pl.pallas_call(
        body, name="ret_fwd", grid=(RET_HEADS, nb),
        in_specs=[proj_spec, cs_spec, cs_spec] + tab_specs,
        out_specs=[hv_spec, hv_spec, pl.BlockSpec((1, ch, RET_QK, RET_V), lambda h, n: (h, n, 0, 0))],
        out_shape=[jax.ShapeDtypeStruct((s, RET_HEADS * RET_V), BF16), jax.ShapeDtypeStruct((s, RET_HEADS * RET_V), BF16),
                   jax.ShapeDtypeStruct((RET_HEADS, s // RET_CHUNK, RET_QK, RET_V), BF16)],
        scratch_shapes=[pltpu.VMEM((RET_QK, RET_V), F32)],
        compiler_params=_cparams(("parallel", "arbitrary")),
    )(proj_ret, cos, sin, *_ret_tables())


def _ret_bwd(proj_ret, cos, sin, y, d_yr, rs, s, carry=None):
    rb = min(512, s)
    ch = rb // RET_CHUNK
    nb = s // rb
    proj_spec, cs_spec, hv_spec, tab_specs = _ret_specs(rb, lambda n: nb - 1 - n)
    nc = carry.n if carry is not None else 0
    n_in = 10

    def body(*refs):
        step = pl.program_id(0) * nb + pl.program_id(1)
        if carry is not None:
            pl.when(step == 0)(lambda: carry.start(*carry.split(refs, n_in, 1)))
        compute(*refs[:n_in], refs[n_in + nc], refs[n_in + 2 * nc + 1])
        if carry is not None:
            pl.when(step == RET_HEADS * nb - 1)(lambda: carry.wait(*carry.split(refs, n_in, 1)))

    def compute(p_ref, cos_ref, sin_ref, y_ref, dyr_ref, rs_ref, dm_ref, qd_ref, kd_ref, cd_ref, o_ref, dr_acc):
        @pl.when(pl.program_id(1) == 0)
        def _():
            dr_acc[...] = jnp.zeros_like(dr_acc)

        dm, qd, kd, cd = dm_ref[0], qd_ref[0], kd_ref[0], cd_ref[0]
        for c in reversed(range(ch)):
            rows = slice(c * RET_CHUNK, (c + 1) * RET_CHUNK)
            cosv, sinv = cos_ref[rows, :], sin_ref[rows, :]
            q = _rope_half(p_ref[rows, 0:256].astype(F32), cosv, sinv)
            kk = _rope_half(p_ref[rows, 256:512].astype(F32), cosv, sinv) * (RET_QK ** -0.5)
            v = p_ref[rows, 512:1024]
            g = p_ref[rows, 1024:1536].astype(F32)
            yv = y_ref[rows, :].astype(F32)
            dyr = dyr_ref[rows, :].astype(F32)
            sg = _sigmoid(g)
            r = lax.rsqrt(jnp.mean(yv * yv, axis=-1, keepdims=True) + EPS)
            yn = yv * r
            dg = dyr * yn * (sg * (1.0 + g * (1.0 - sg)))
            dyn = dyr * (g * sg)
            dy = (r * (dyn - yn * jnp.mean(dyn * yn, axis=-1, keepdims=True))).astype(BF16)
            qb, kb = q.astype(BF16), kk.astype(BF16)
            rb16 = rs_ref[0, c]
            drb = dr_acc[...].astype(BF16)
            sd = _dot(qb, kb, NT) * dm
            ds = (_dot(dy, v, NT) * dm).astype(BF16)
            dq = _dot(ds, kb, NN) + qd * _dot(dy, rb16, NT)
            dk = _dot(ds, qb, TN) + kd * _dot(v, drb, NT)
            dv = _dot(sd.astype(BF16), dy, TN) + _dot((kk * kd).astype(BF16), drb, NN)
            dr_acc[...] = dr_acc[...] * cd + _dot((q * qd).astype(BF16), dy, TN)
            o_ref[rows, 0:256] = _unrope_half(dq, cosv, sinv).astype(BF16)
            o_ref[rows, 256:512] = (_unrope_half(dk, cosv, sinv) * (RET_QK ** -0.5)).astype(BF16)
            o_ref[rows, 512:1024] = dv.astype(BF16)
            o_ref[rows, 1024:1536] = dg.astype(BF16)

    in_specs = [proj_spec, cs_spec, cs_spec, hv_spec, hv_spec,
                pl.BlockSpec((1, ch, RET_QK, RET_V), lambda h, n: (h, nb - 1 - n, 0, 0))] + tab_specs
    out_shape = jax.ShapeDtypeStruct((s, RET_HEADS * 1536), BF16)
    scratch = [pltpu.VMEM((RET_QK, RET_V), F32)]
    args = (proj_ret, cos, sin, y, d_yr, rs, *_ret_tables())
    if carry is None:
        return pl.pallas_call(body, name="ret_bwd", grid=(RET_HEADS, nb), in_specs=in_specs, out_specs=proj_spec,
                              out_shape=out_shape, scratch_shapes=scratch, compiler_params=_cparams(("parallel", "arbitrary")))(*args)
    res = pl.pallas_call(
        body, name="ret_bwd", grid=(RET_HEADS, nb), in_specs=in_specs + carry.specs, out_specs=[proj_spec] + carry.specs,
        out_shape=[out_shape] + carry.out_shape, scratch_shapes=scratch + carry.scratch,
        compiler_params=_cparams(("arbitrary", "arbitrary")),
    )(*args, *carry.arrays)
    return res[0], res[1:]


def _rope_qk(qkv, tc, ts1, ts2, s, name):
    def fn(q, k, c, s1, s2):
        outs = []
        for v in (q, k):
            for cc in range(4):
                vv = v[:, cc * 128:(cc + 1) * 128].astype(F32)
                outs.append(vv * c + pltpu.roll(vv, 120, 1) * s1 + pltpu.roll(vv, 8, 1) * s2)
        return [jnp.concatenate(outs, axis=1)], []

    (out,), _ = _rowwise(name, fn, s, min(512, s), [(qkv, 512, 0), (qkv, 512, 1), (tc, 128, 0), (ts1, 128, 0), (ts2, 128, 0)],
                         [], [(1024, BF16)])
    return out


def _pair_masks():
    ri = lax.broadcasted_iota(jnp.int32, (2 * QB, 2 * QB), 0)
    ci = lax.broadcasted_iota(jnp.int32, (2 * QB, 2 * QB), 1)
    e = ci - (ri & (QB - 1))
    lane_lo = lax.broadcasted_iota(jnp.int32, (2 * QB, 128), 1) < 64
    return ci, jnp.logical_and(e >= 0, e <= QB), lane_lo


def _stack_heads(v, lane_lo):
    z = jnp.zeros_like(v)
    return jnp.concatenate([jnp.where(lane_lo, v, z), jnp.where(lane_lo, z, v)], axis=0)


def _dil_fwd(qkr, qkv, dil, s, name):
    length = s // dil
    rb = min(512, length)
    nsub = rb // QB
    nbs = length // rb
    sub_per = rb // QB

    def body(q_ref, k_ref, v_ref, kp_ref, vp_ref, o_ref, l_ref, kf, vf):
        first = (pl.program_id(0) % nbs) == 0
        kf[0:QB, :] = kp_ref[...]
        kf[QB:, :] = k_ref[...]
        vf[0:QB, :] = vp_ref[...]
        vf[QB:, :] = v_ref[...]
        ci, band, lane_lo = _pair_masks()
        lo1 = lane_lo[0:QB]

        def step(i, carry):
            r0 = pl.multiple_of(i * QB, QB)
            mask = jnp.logical_and(band, ci >= jnp.where(jnp.logical_and(first, i == 0), QB, 0))
            for j in range(4):
                lanes = slice(j * 128, (j + 1) * 128)
                q2 = _stack_heads(q_ref[pl.ds(r0, QB), lanes], lo1)
                k2 = kf[pl.ds(r0, 2 * QB), lanes]
                v2 = _stack_heads(vf[pl.ds(r0, 2 * QB), lanes], lane_lo)
                sc = jnp.where(mask, _dot(q2, k2, NT) * 0.125, NEG)
                m = jnp.max(sc, axis=1, keepdims=True)
                p = jnp.exp(sc - m)
                den = jnp.sum(p, axis=1, keepdims=True)
                pb = p.astype(BF16)
                o = _dot(jnp.concatenate([pb[0:QB], pb[QB:]], axis=1), v2, NN)
                inv = 1.0 / den
                lse = m + jnp.log(den)
                o_ref[pl.ds(r0, QB), lanes] = o * jnp.where(lo1, inv[0:QB], inv[QB:])
                l_ref[pl.ds(r0, QB), lanes] = jnp.where(lo1, lse[0:QB], lse[QB:])
            return carry

        lax.fori_loop(0, nsub, step, 0)

    prev = lambda n: jnp.maximum(n * sub_per - 1, 0)
    cur = lambda cb: (lambda n: (n, cb))
    return pl.pallas_call(
        body, name=name, grid=(s // rb,),
        in_specs=[pl.BlockSpec((rb, DIL_W), cur(0)), pl.BlockSpec((rb, DIL_W), cur(1)), pl.BlockSpec((rb, DIL_W), cur(2)),
                  pl.BlockSpec((QB, DIL_W), lambda n: (prev(n), 1)), pl.BlockSpec((QB, DIL_W), lambda n: (prev(n), 2))],
        out_specs=[pl.BlockSpec((rb, DIL_W), cur(0)), pl.BlockSpec((rb, DIL_W), cur(0))],
        out_shape=[jax.ShapeDtypeStruct((s, DIL_W), F32), jax.ShapeDtypeStruct((s, DIL_W), F32)],
        scratch_shapes=[pltpu.VMEM((QB + rb, DIL_W), BF16), pltpu.VMEM((QB + rb, DIL_W), BF16)],
        compiler_params=_cparams(("parallel",)),
    )(qkr, qkr, qkv, qkr, qkv)


def _dil_bwd(qkr, qkv, dya, lse, dlt, tc, ts1, ts2, dil, s, name):
    length = s // dil
    rb = min(512, length)
    nsub = rb // QB
    nbs = length // rb
    last_blk = s // QB - 1

    def body(q_ref, k_ref, v_ref, kp_ref, vp_ref, qn_ref, dy_ref, dyn_ref, l_ref, ln_ref, d_ref, dn_ref,
             c_ref, s1_ref, s2_ref, o_ref, kf, vf, qf, dyf, lf, df, dqa, dka, dva):
        nl = pl.program_id(0) % nbs
        first, last = nl == 0, nl == nbs - 1
        kf[0:QB, :] = kp_ref[...]
        kf[QB:QB + rb, :] = k_ref[...]
        kf[QB + rb:, :] = jnp.zeros((QB, DIL_W), BF16)
        vf[0:QB, :] = vp_ref[...]
        vf[QB:QB + rb, :] = v_ref[...]
        vf[QB + rb:, :] = jnp.zeros((QB, DIL_W), BF16)
        qf[0:rb, :] = q_ref[...]
        qf[rb:, :] = qn_ref[...]
        dyf[0:rb, :] = dy_ref[...]
        dyf[rb:, :] = dyn_ref[...]
        lf[0:rb, :] = l_ref[...]
        lf[rb:, :] = ln_ref[...]
        df[0:rb, :] = d_ref[...]
        df[rb:, :] = dn_ref[...]
        dka[...] = jnp.zeros_like(dka)
        dva[...] = jnp.zeros_like(dva)
        ci, band, lane_lo = _pair_masks()
        lo1 = lane_lo[0:QB]

        def step(qi, carry):
            qr = pl.multiple_of(qi * QB, QB)
            is_next = qi == nsub
            cmin = jnp.where(jnp.logical_and(first, qi == 0), QB, 0)
            cmax = jnp.where(is_next, jnp.where(last, -1, QB - 1), 2 * QB - 1)
            mask = jnp.logical_and(band, jnp.logical_and(ci >= cmin, ci <= cmax))
            for j in range(4):
                lanes = slice(j * 128, (j + 1) * 128)
                q2 = _stack_heads(qf[pl.ds(qr, QB), lanes], lo1)
                do2 = _stack_heads(dyf[pl.ds(qr, QB), lanes], lo1)
                lv = lf[pl.ds(qr, QB), lanes]
                dl = df[pl.ds(qr, QB), lanes]
                lse2 = jnp.concatenate([lv[:, 0:1], lv[:, 64:65]], axis=0)
                dl2 = jnp.concatenate([dl[:, 0:1], dl[:, 64:65]], axis=0)
                k2 = kf[pl.ds(qr, 2 * QB), lanes]
                v2 = vf[pl.ds(qr, 2 * QB), lanes]
                sc = _dot(q2, k2, NT) * 0.125
                p = jnp.where(mask, jnp.exp(jnp.minimum(sc - lse2, 0.0)), 0.0)
                ds = (p * (_dot(do2, v2, NT) - dl2) * 0.125).astype(BF16)
                dq = _dot(jnp.concatenate([ds[0:QB], ds[QB:]], axis=1), _stack_heads(k2, lane_lo), NN)
                dqa[pl.ds(qr, QB), lanes] = dq
                dka[pl.ds(qr, 2 * QB), lanes] += _dot(ds, q2, TN)
                dva[pl.ds(qr, 2 * QB), lanes] += _dot(p.astype(BF16), do2, TN)
            return carry

        lax.fori_loop(0, nsub + 1, step, 0)
        cv, s1v, s2v = c_ref[...], s1_ref[...], s2_ref[...]

        def unrope(d):
            return d * cv + pltpu.roll(d * s1v, 8, 1) + pltpu.roll(d * s2v, 120, 1)

        for cc in range(4):
            lanes = slice(cc * 128, (cc + 1) * 128)
            o_ref[:, cc * 128:(cc + 1) * 128] = unrope(dqa[0:rb, lanes]).astype(BF16)
            o_ref[:, 512 + cc * 128:512 + (cc + 1) * 128] = unrope(dka[QB:QB + rb, lanes]).astype(BF16)
            o_ref[:, 1024 + cc * 128:1024 + (cc + 1) * 128] = dva[QB:QB + rb, lanes].astype(BF16)

    prev = lambda n: jnp.maximum(n * nsub - 1, 0)
    nxt = lambda n: jnp.minimum(n * nsub + nsub, last_blk)
    cur = lambda cb: (lambda n: (n, cb))
    big = lambda cb: pl.BlockSpec((rb, DIL_W), cur(cb))
    small = lambda im: pl.BlockSpec((QB, DIL_W), im)
    tab = pl.BlockSpec((rb, 128), cur(0))
    return pl.pallas_call(
        body, name=name, grid=(s // rb,),
        in_specs=[big(0), big(1), big(2), small(lambda n: (prev(n), 1)), small(lambda n: (prev(n), 2)),
                  small(lambda n: (nxt(n), 0)), big(0), small(lambda n: (nxt(n), 0)), big(0), small(lambda n: (nxt(n), 0)),
                  big(0), small(lambda n: (nxt(n), 0)), tab, tab, tab],
        out_specs=pl.BlockSpec((rb, 3 * DIL_W), cur(0)),
        out_shape=jax.ShapeDtypeStruct((s, 3 * DIL_W), BF16),
        scratch_shapes=[pltpu.VMEM((rb + 2 * QB, DIL_W), BF16), pltpu.VMEM((rb + 2 * QB, DIL_W), BF16),
                        pltpu.VMEM((rb + QB, DIL_W), BF16), pltpu.VMEM((rb + QB, DIL_W), BF16),
                        pltpu.VMEM((rb + QB, DIL_W), F32), pltpu.VMEM((rb + QB, DIL_W), F32),
                        pltpu.VMEM((rb + QB, DIL_W), F32), pltpu.VMEM((rb + 2 * QB, DIL_W), F32),
                        pltpu.VMEM((rb + 2 * QB, DIL_W), F32)],
        compiler_params=_cparams(("parallel",)),
    )(qkr, qkr, qkv, qkr, qkv, qkr, dya, dya, lse, lse, dlt, dlt, tc, ts1, ts2)


def _dil_merge(o_g, l_g, s):
    def fn(o0, l0, o1, l1, o2, l2):
        m = jnp.maximum(jnp.maximum(l0, l1), l2)
        e0, e1, e2 = jnp.exp(l0 - m), jnp.exp(l1 - m), jnp.exp(l2 - m)
        den = e0 + e1 + e2
        return [(e0 * o0 + e1 * o1 + e2 * o2) / den, m + jnp.log(den)], []

    rows = [(a, DIL_W, 0) for pair in zip(o_g, l_g) for a in pair]
    (ya, lse), _ = _rowwise("dil_merge", fn, s, min(512, s), rows, [], [(DIL_W, BF16), (DIL_W, F32)])
    return ya, lse


def _dil_bwd_prep(d_ya, ya, s):
    def fn(dya, yav):
        lane_lo = lax.broadcasted_iota(jnp.int32, (dya.shape[0], 128), 1) < 64
        parts = []
        for cc in range(4):
            prod = dya[:, cc * 128:(cc + 1) * 128] * yav[:, cc * 128:(cc + 1) * 128].astype(F32)
            lo = jnp.where(lane_lo, prod, 0.0)
            s_lo = jnp.sum(lo, axis=1, keepdims=True)
            s_hi = jnp.sum(prod - lo, axis=1, keepdims=True)
            parts.append(jnp.where(lane_lo, s_lo, s_hi))
        return [dya, jnp.concatenate(parts, axis=1)], []

    (dyb, dlt), _ = _rowwise("dil_bwd_prep", fn, s, min(512, s), [(d_ya, DIL_W, 0), (ya, DIL_W, 0)], [],
                             [(DIL_W, BF16), (DIL_W, F32)])
    return dyb, dlt


_RET_ROWS = [(256 * h, 256) for h in range(4)], [(1024 + 256 * h, 256) for h in range(4)], \
            [(2048 + 512 * h, 512) for h in range(4)], [(4096 + 512 * h, 512) for h in range(4)]


def _split_w_in(win):
    rows = lambda a, n: win[a:a + n]
    w_ret = jnp.concatenate([rows(*seg[h]) for h in range(RET_HEADS) for seg in _RET_ROWS], axis=0)
    w_dil = [jnp.concatenate([rows(base + 512 * g, 512) for base in (6144, 7680, 9216)], axis=0) for g in range(3)]
    return w_ret, win[10752:12800], w_dil


def _join_w_in(g_ret, g_gate, g_dil):
    parts = []
    for i in range(4):
        off = (0, 256, 512, 1024)[i]
        parts += [g_ret[1536 * h + off:1536 * h + off + _RET_ROWS[i][h][1]] for h in range(RET_HEADS)]
    for i in range(3):
        parts += [g_dil[g][512 * i:512 * (i + 1)] for g in range(3)]
    return jnp.concatenate(parts + [g_gate], axis=0)


def _local_step(xs, pb, tgt, tabs, wts, vec, s, late_shards=None):
    tm = min(1024, s)
    tr = min(256, s)
    mm = functools.partial(_matmul, tm=tm)
    on_mesh = late_shards is not None
    wts = dict(wts)
    w_ret, w_gate, w_dil = _split_w_in(wts["w_in"])
    blocks = lambda g: g.reshape(N_DEV, g.shape[0] // N_DEV, g.shape[1])

    (u_nat,), _ = _rowwise("prenorm", lambda xv, g: ([_rms(xv) * g], []), s, tr, [(xs, 1024, 0)], [vec["g_pre_mix"]], [(1024, BF16)])
    u = [_to_streams(u_nat, dil) for dil in DIL_GROUPS]
    inproj_ret = functools.partial(mm, u[0], w_ret, mode="nt", m=s, n=6144, k=1024, tn=1024, tk=1024, out_dtype=BF16, name="inproj_ret")
    if on_mesh:
        names = list(late_shards)
        proj_ret, gathered = inproj_ret(carry=_Exchange([late_shards[n] for n in names], [False] * len(names)))
        wts.update({n: g.reshape(N_DEV * g.shape[1], g.shape[2]) for n, g in zip(names, gathered)})
    else:
        proj_ret = inproj_ret()
    proj_gate = mm(u[0], w_gate, mode="nt", m=s, n=2048, k=1024, tn=1024, tk=1024, out_dtype=BF16, name="inproj_gate")
    qkv = [mm(u[g], w_dil[g], mode="nt", m=s, n=1536, k=1024, tn=512, tk=1024, out_dtype=BF16, name="inproj_dil%d" % g)
           for g in range(3)]

    yr, y_ret, rstate = _ret_fwd(proj_ret, tabs["cos_r"], tabs["sin_r"], s)
    a_br = mm(yr, wts["w_ret_out"], mode="nn", m=s, n=1024, k=2048, tn=1024, tk=1024, out_dtype=BF16, name="ret_out")

    qkr, o_g, l_g = [], [], []
    for g, dil in enumerate(DIL_GROUPS):
        qkr.append(_rope_qk(qkv[g], *tabs["dil"][g], s, "rope_qk%d" % g))
        o, l = _dil_fwd(qkr[g], qkv[g], dil, s, "dil_fwd%d" % g)
        o_g.append(_from_streams(o, dil))
        l_g.append(_from_streams(l, dil))
    ya, lse = _dil_merge(o_g, l_g, s)
    b_br = mm(ya, wts["w_dil_out"], mode="nt", m=s, n=1024, k=512, tn=1024, tk=512, out_dtype=BF16, name="dil_out")

    def gate_mix(a, b, gr, ga, b0, b1):
        return [_sigmoid(gr.astype(F32) + b0) * a.astype(F32) + _sigmoid(ga.astype(F32) + b1) * b.astype(F32)], []

    (mixed,), _ = _rowwise("gate_mix", gate_mix, s, tr, [(a_br, 1024, 0), (b_br, 1024, 0), (proj_gate, 1024, 0), (proj_gate, 1024, 1)],
                           [vec["b0"], vec["b1"]], [(1024, BF16)])
    z = mm(mixed, wts["w_o"], mode="nn", m=s, n=1024, k=1024, tn=1024, tk=1024, out_dtype=F32, name="w_o")

    def post_norm(h, f, g_post, g_pre):
        hn = h + _rms(f) * g_post
        return [hn, _rms(hn) * g_pre], []

    (h1, v2), _ = _rowwise("post_mix", post_norm, s, tr, [(xs, 1024, 0), (z, 1024, 0)], [vec["g_post_mix"], vec["g_pre_mlp"]],
                           [(1024, F32), (1024, BF16)])
    a_up = mm(v2, wts["w_up"], mode="nt", m=s, n=4096, k=1024, tn=512, tk=1024, out_dtype=BF16, name="mlp_up")
    f_dn = mm(a_up, wts["w_down"], mode="nn", m=s, n=1024, k=4096, tn=1024, tk=1024, out_dtype=F32, name="mlp_down", a_fn=_relu_sq)
    (h2, t_ple), _ = _rowwise("post_mlp", post_norm, s, tr, [(h1, 1024, 0), (f_dn, 1024, 0)], [vec["g_post_mlp"], vec["g_pre_ple"]],
                              [(1024, F32), (1024, BF16)])
    gl = mm(t_ple, wts["w_ple_gate"], mode="nn", m=s, n=1024, k=1024, tn=1024, tk=1024, out_dtype=F32, name="ple_gate")
    e_ple = mm(pb, wts["w_ple_in"], mode="nt", m=s, n=1024, k=256, tn=1024, tk=256, out_dtype=F32, name="ple_in")

    def ple_loss(h, glv, e, tg, b, g):
        gate = _sigmoid(glv + b)
        ge = gate * e
        diff = h + _rms(ge) * g - tg
        dy = diff * (1.0 / D_MODEL)
        d_ge, dg = _rms_bwd(ge, g, dy)
        d_gl = d_ge * e * gate * (1.0 - gate)
        loss = jnp.zeros((1, D_MODEL), F32) + 0.5 * jnp.sum(diff * diff) * (1.0 / D_MODEL)
        return [dy, d_gl, d_ge * gate], [_colsum(dg), _colsum(d_gl), loss]

    (dy, d_gl, d_e), (dg_post_ple, db_ple, loss) = _rowwise(
        "ple_loss", ple_loss, s, tr, [(h2, 1024, 0), (gl, 1024, 0), (e_ple, 1024, 0), (tgt, 1024, 0)],
        [vec["b_ple"], vec["g_post_ple"]], [(1024, F32), (1024, BF16), (1024, BF16)], [1024, 1024, 1024])

    ts = min(1024, s)
    wg = functools.partial(_matmul, mode="tn", k=s, tk=ts, out_dtype=BF16)
    grads = {}
    grads["w_ple_in"] = wg(d_e, pb, m=1024, n=256, tm=1024, tn=256, name="g_ple_in")
    grads["w_ple_gate"] = wg(t_ple, d_gl, m=1024, n=1024, tm=1024, tn=1024, name="g_ple_gate")
    d_t = mm(d_gl, wts["w_ple_gate"], mode="nt", m=s, n=1024, k=1024, tn=1024, tk=1024, out_dtype=F32, name="d_t")

    def bwd_ple_mlp(h, dt, dyv, f, g_pre, g_post):
        dx, dg1 = _rms_bwd(h, g_pre, dt)
        dh = dyv + dx
        df, dg2 = _rms_bwd(f, g_post, dh)
        return [dh, df], [_colsum(dg1), _colsum(dg2)]

    (d_h2, d_f), (dg_pre_ple, dg_post_mlp) = _rowwise(
        "bwd_ple_mlp", bwd_ple_mlp, s, tr, [(h2, 1024, 0), (d_t, 1024, 0), (dy, 1024, 0), (f_dn, 1024, 0)],
        [vec["g_pre_ple"], vec["g_post_mlp"]], [(1024, F32), (1024, BF16)], [1024, 1024])
    d_a = mm(d_f, wts["w_down"], mode="nt", m=s, n=4096, k=1024, tn=512, tk=1024, out_dtype=BF16, name="d_a",
             epi=a_up, epi_fn=lambda acc, av: acc * (2.0 * jnp.maximum(av.astype(F32), 0.0)))
    grads["w_down"] = wg(a_up, d_f, m=4096, n=1024, tm=1024, tn=1024, name="g_down", a_fn=_relu_sq)
    grads["w_up"] = wg(d_a, v2, m=4096, n=1024, tm=1024, tn=1024, name="g_up")
    d_v2 = mm(d_a, wts["w_up"], mode="nn", m=s, n=1024, k=4096, tn=1024, tk=1024, out_dtype=F32, name="d_v2")

    (d_h1, d_z), (dg_pre_mlp, dg_post_mix) = _rowwise(
        "bwd_mlp_mix", bwd_ple_mlp, s, tr, [(h1, 1024, 0), (d_v2, 1024, 0), (d_h2, 1024, 0), (z, 1024, 0)],
        [vec["g_pre_mlp"], vec["g_post_mix"]], [(1024, F32), (1024, BF16)], [1024, 1024])
    d_mixed = mm(d_z, wts["w_o"], mode="nt", m=s, n=1024, k=1024, tn=1024, tk=1024, out_dtype=F32, name="d_mixed")
    grads["w_o"] = wg(mixed, d_z, m=1024, n=1024, tm=1024, tn=1024, name="g_o")

    def bwd_gate(dm, a, b, gr, ga, b0, b1):
        sa, sb = _sigmoid(gr.astype(F32) + b0), _sigmoid(ga.astype(F32) + b1)
        dgr = dm * a.astype(F32) * sa * (1.0 - sa)
        dga = dm * b.astype(F32) * sb * (1.0 - sb)
        return [dm * sa, dm * sb, jnp.concatenate([dgr, dga], axis=1)], [_colsum(dgr), _colsum(dga)]

    (d_abr, d_bbr, dproj_gate), (db0, db1) = _rowwise(
        "bwd_gate", bwd_gate, s, tr, [(d_mixed, 1024, 0), (a_br, 1024, 0), (b_br, 1024, 0), (proj_gate, 1024, 0), (proj_gate, 1024, 1)],
        [vec["b0"], vec["b1"]], [(1024, BF16), (1024, BF16), (2048, BF16)], [1024, 1024])
    grads["w_ret_out"] = wg(yr, d_abr, m=2048, n=1024, tm=1024, tn=1024, name="g_ret_out")
    d_yr = mm(d_abr, wts["w_ret_out"], mode="nt", m=s, n=2048, k=1024, tn=512, tk=1024, out_dtype=BF16, name="d_yr")
    grads["w_dil_out"] = wg(d_bbr, ya, m=1024, n=512, tm=1024, tn=512, name="g_dil_out")
    d_ya = mm(d_bbr, wts["w_dil_out"], mode="nn", m=s, n=512, k=1024, tn=512, tk=1024, out_dtype=F32, name="d_ya")

    slots = {}
    if on_mesh:
        names = list(grads)
        dproj_ret, got = _ret_bwd(proj_ret, tabs["cos_r"], tabs["sin_r"], y_ret, d_yr, rstate, s,
                                  carry=_Exchange([blocks(grads[n]) for n in names], [True] * len(names)))
        slots.update(zip(names, got))
    else:
        dproj_ret = _ret_bwd(proj_ret, tabs["cos_r"], tabs["sin_r"], y_ret, d_yr, rstate, s)
    dyb, dlt = _dil_bwd_prep(d_ya, ya, s)
    dqkv = [_dil_bwd(qkr[g], qkv[g], _to_streams(dyb, dil), _to_streams(lse, dil), _to_streams(dlt, dil), *tabs["dil"][g],
                     dil, s, "dil_bwd%d" % g) for g, dil in enumerate(DIL_GROUPS)]

    g_ret = wg(dproj_ret, u[0], m=6144, n=1024, tm=2048, tn=1024, name="g_in_ret")
    g_gate = wg(dproj_gate, u[0], m=2048, n=1024, tm=2048, tn=1024, name="g_in_gate")
    g_dil = [wg(dqkv[g], u[g], m=1536, n=1024, tm=1536, tn=1024, name="g_in_dil%d" % g) for g in range(3)]
    grads["w_in"] = _join_w_in(g_ret, g_gate, g_dil)

    du_ret = functools.partial(mm, dproj_ret, w_ret, mode="nn", m=s, n=1024, k=6144, tn=1024, tk=1024, out_dtype=F32, name="du_ret")
    if on_mesh:
        du_ret, (slots["w_in"],) = du_ret(carry=_Exchange([blocks(grads["w_in"])], [True]))
    else:
        du_ret = du_ret()
    du_gate = mm(dproj_gate, w_gate, mode="nn", m=s, n=1024, k=2048, tn=1024, tk=1024, out_dtype=F32, name="du_gate")
    du_dil = [mm(dqkv[g], w_dil[g], mode="nn", m=s, n=1024, k=1536, tn=1024, tk=512, out_dtype=F32, name="du_dil%d" % g)
              for g in range(3)]

    def grad_x_fn(xv, dh, d0, d1, d2, d3, d4, g):
        dx, dg = _rms_bwd(xv, g, d0 + d1 + d2 + d3 + d4)
        return [dh + dx], [_colsum(dg)]

    du_all = [du_ret, du_gate] + [_from_streams(du_dil[g], dil) for g, dil in enumerate(DIL_GROUPS)]
    (grad_x,), (dg_pre_mix,) = _rowwise("grad_x", grad_x_fn, s, tr, [(a, 1024, 0) for a in [xs, d_h1] + du_all],
                                        [vec["g_pre_mix"]], [(1024, F32)], [1024])

    zero = jnp.zeros((1, D_MODEL), F32)
    packet = jnp.concatenate([dg_pre_mix, dg_post_mix, dg_pre_mlp, dg_post_mlp, dg_pre_ple, db_ple, dg_post_ple, loss,
                              db0, db1] + [zero] * 6, axis=0)
    return grad_x, (slots if on_mesh else grads), packet


def _mesh_pos():
    return lax.axis_index("x"), lax.axis_index("y"), lax.axis_index("c")


def _all_gather(shards):
    nw = len(shards)

    def body(*refs):
        ins, outs = refs[:nw], refs[nw:2 * nw]
        send_sems, recv_sems, local_sems = refs[2 * nw:]
        x, y, c = _mesh_pos()
        me, sibling = (x, y, c), (x, y, 1 - c)
        chips = [(1 - x, y), (x, 1 - y), (1 - x, 1 - y)]

        def region(w, dev):
            return outs[w].at[4 * dev[0] + 2 * dev[1] + dev[2]]

        def copy(w, kk, block, to, src=None):
            return pltpu.make_async_remote_copy(
                src_ref=region(w, block) if src is None else src, dst_ref=region(w, block),
                send_sem=send_sems.at[w * 7 + kk], recv_sem=recv_sems.at[w * 7 + kk], device_id=to, device_id_type=MESH)

        mine = [pltpu.make_async_copy(ins[w], region(w, me), local_sems.at[w]) for w in range(nw)]
        for cp in mine:
            cp.start()
        first = []
        for w in range(nw):
            first.append(copy(w, 0, me, sibling, src=ins[w]))
            first += [copy(w, 1 + j, me, (*chip, c), src=ins[w]) for j, chip in enumerate(chips)]
        for cp in first:
            cp.start()
        passed = []
        for j, chip in enumerate(chips):
            for w in range(nw):
                copy(w, 1 + j, (*chip, c), me).wait_recv()
                cp = copy(w, 4 + j, (*chip, c), sibling)
                cp.start()
                passed.append(cp)
        for w in range(nw):
            copy(w, 0, sibling, me).wait_recv()
            for j, chip in enumerate(chips):
                copy(w, 4 + j, (*chip, 1 - c), me).wait_recv()
        for cp in first + passed:
            cp.wait_send()
        for cp in mine:
            cp.wait()

    hbm = pl.BlockSpec(memory_space=pl.ANY)
    return pl.pallas_call(
        body, name="gather_weights",
        in_specs=[hbm] * nw, out_specs=[hbm] * nw,
        out_shape=[jax.ShapeDtypeStruct((N_DEV,) + sh.shape, sh.dtype) for sh in shards],
        scratch_shapes=[pltpu.SemaphoreType.DMA((nw * 7,)), pltpu.SemaphoreType.DMA((nw * 7,)), pltpu.SemaphoreType.DMA((nw,))],
    )(*shards)


class _Exchange:
    def __init__(self, arrays, scatter):
        self.arrays, self.scatter, self.n = list(arrays), list(scatter), len(arrays)
        self.out_shape = [jax.ShapeDtypeStruct(a.shape if sc else (N_DEV,) + a.shape, a.dtype)
                          for a, sc in zip(self.arrays, self.scatter)]
        self.scratch = [pltpu.SemaphoreType.DMA((self.n * 7,)), pltpu.SemaphoreType.DMA((self.n * 7,)),
                        pltpu.SemaphoreType.DMA((self.n,))]
        self.specs = [pl.BlockSpec(memory_space=pl.ANY)] * self.n

    def _copies(self, srcs, dsts, sems):
        send_sems, recv_sems, local_sems = sems
        x, y, c = _mesh_pos()
        my = 4 * x + 2 * y + c
        src_of = lambda w, idx: srcs[w].at[idx] if self.scatter[w] else srcs[w]
        local = [pltpu.make_async_copy(src_of(w, my), dsts[w].at[my], local_sems.at[w]) for w in range(self.n)]
        sends, recvs = [], []
        for w in range(self.n):
            for r in range(1, N_DEV):
                px = 1 - x if r & 4 else x
                py = 1 - y if r & 2 else y
                pc = 1 - c if r & 1 else c
                pidx = 4 * px + 2 * py + pc
                kw = dict(send_sem=send_sems.at[w * 7 + r - 1], recv_sem=recv_sems.at[w * 7 + r - 1],
                          device_id=(px, py, pc), device_id_type=MESH)
                sends.append(pltpu.make_async_remote_copy(src_ref=src_of(w, pidx), dst_ref=dsts[w].at[my], **kw))
                recvs.append(pltpu.make_async_remote_copy(src_ref=src_of(w, pidx), dst_ref=dsts[w].at[pidx], **kw))
        return local, sends, recvs

    def start(self, srcs, dsts, sems):
        local, sends, _ = self._copies(srcs, dsts, sems)
        for cp in local + sends:
            cp.start()

    def wait(self, srcs, dsts, sems):
        local, sends, recvs = self._copies(srcs, dsts, sems)
        for cp in recvs:
            cp.wait_recv()
        for cp in sends:
            cp.wait_send()
        for cp in local:
            cp.wait()

    def split(self, refs, n_in, n_out):
        srcs = refs[n_in:n_in + self.n]
        dsts = refs[n_in + self.n + n_out:n_in + 2 * self.n + n_out]
        return srcs, dsts, refs[len(refs) - 3:]


def _run_exchange(ex, name):
    def body(*refs):
        parts = ex.split(refs, 0, 0)
        ex.start(*parts)
        ex.wait(*parts)

    return pl.pallas_call(body, name=name, in_specs=ex.specs, out_specs=ex.specs, out_shape=ex.out_shape,
                          scratch_shapes=ex.scratch)(*ex.arrays)


def _pick_rows(r, c, target_bytes):
    t = r
    while (t // 2) % 16 == 0 and t // 2 >= 16 and t * c * 4 > target_bytes:
        t //= 2
    return t


def _sum_slots(slots, name):
    ns, r, c = slots.shape
    tr = _pick_rows(r, c, 256 * 1024)

    def body(s_ref, o_ref):
        acc = s_ref[0].astype(F32)
        for kk in range(1, ns):
            acc = acc + s_ref[kk].astype(F32)
        o_ref[...] = acc

    return pl.pallas_call(
        body, name=name, grid=(r // tr,),
        in_specs=[pl.BlockSpec((ns, tr, c), lambda i: (0, i, 0))], out_specs=pl.BlockSpec((tr, c), lambda i: (i, 0)),
        out_shape=jax.ShapeDtypeStruct((r, c), F32), compiler_params=_cparams(("parallel",)),
    )(slots)


def _adamw(slots, w, m, v, name):
    ns, r, c = slots.shape
    tr = _pick_rows(r, c, 256 * 1024)

    def body(s_ref, w_ref, m_ref, v_ref, g_out, d_out, m_out, v_out):
        g = s_ref[0].astype(F32)
        for kk in range(1, ns):
            g = g + s_ref[kk].astype(F32)
        mn = ADAM_B1 * m_ref[...] + (1.0 - ADAM_B1) * g
        vn = ADAM_B2 * v_ref[...] + (1.0 - ADAM_B2) * (g * g)
        m_hat = mn / (1.0 - ADAM_B1 ** ADAM_STEP)
        v_hat = vn / (1.0 - ADAM_B2 ** ADAM_STEP)
        g_out[...] = g
        d_out[...] = -ADAM_LR * (m_hat / (jnp.sqrt(v_hat) + ADAM_EPS) + ADAM_WD * w_ref[...])
        m_out[...] = mn
        v_out[...] = vn

    blk = pl.BlockSpec((tr, c), lambda i: (i, 0))
    return pl.pallas_call(
        body, name=name, grid=(r // tr,),
        in_specs=[pl.BlockSpec((ns, tr, c), lambda i: (0, i, 0)), blk, blk, blk], out_specs=[blk] * 4,
        out_shape=[jax.ShapeDtypeStruct((r, c), F32)] * 4, compiler_params=_cparams(("parallel",)),
    )(slots, w, m, v)


def _rotary_tables(pos, s):
    posf = pos.astype(F32)
    inv_freq = 1.0 / (10000.0 ** jnp.linspace(0.0, 1.0, RET_QK // 2, dtype=F32))
    ang = posf[:, None] * inv_freq
    tabs = {"cos_r": jnp.cos(ang), "sin_r": jnp.sin(ang), "dil": []}
    freqs = 500000.0 ** (-jnp.arange(0, 16, 2, dtype=F32) / 16)
    spread = np.zeros((16, 384), np.float32)
    bias = np.zeros((1, 384), np.float32)
    for head in range(2):
        for i in range(8):
            spread[i, 64 * head + i] = spread[i, 64 * head + 8 + i] = 1.0
            spread[8 + i, 128 + 64 * head + i] = -1.0
            spread[8 + i, 256 + 64 * head + 8 + i] = 1.0
        bias[0, 64 * head + 16:64 * head + 64] = 1.0

    def expand(t, e, b):
        hi = t.astype(BF16)
        lo = (t - hi.astype(F32)).astype(BF16)
        out = _dot(hi, e, NN) + _dot(lo, e, NN) + b
        return [out[:, 0:128], out[:, 128:256], out[:, 256:384]], []

    for g, dil in enumerate(DIL_GROUPS):
        ang = posf.reshape(s // dil, dil).T.reshape(s, 1) * freqs
        cs = jnp.concatenate([jnp.cos(ang), jnp.sin(ang)], axis=1)
        t3, _ = _rowwise("rot_tables%d" % g, expand, s, min(1024, s), [(cs, 16, 0)],
                         [jnp.asarray(spread, BF16), jnp.asarray(bias)], [(128, F32)] * 3)
        tabs["dil"].append(tuple(t3))
    return tabs


_TRANSPOSED = ("w_in", "w_dil_out", "w_up", "w_ple_in")
_MATS = ("w_in", "w_ret_out", "w_dil_out", "w_o", "w_up", "w_down", "w_ple_gate", "w_ple_in")
_VECS = ("g_pre_mix", "g_post_mix", "g_pre_mlp", "g_post_mlp", "g_pre_ple", "b_ple_gate", "g_post_ple")
_ORDER = ("w_in", "b_gate", "w_ret_out", "w_dil_out", "w_o", "g_pre_mix", "g_post_mix", "g_pre_mlp", "g_post_mlp", "w_up",
          "w_down", "g_pre_ple", "w_ple_gate", "b_ple_gate", "w_ple_in", "g_post_ple")


def kernel(x, p, positions, w_in, b_gate, w_ret_out, w_dil_out, w_o, g_pre_mix, g_post_mix, g_pre_mlp, g_post_mlp, w_up, w_down, g_pre_ple, w_ple_gate, b_ple_gate, w_ple_in, g_post_ple, loss_target, m_w_in, m_b_gate, m_w_ret_out, m_w_dil_out, m_w_o, m_g_pre_mix, m_g_post_mix, m_g_pre_mlp, m_g_post_mlp, m_w_up, m_w_down, m_g_pre_ple, m_w_ple_gate, m_b_ple_gate, m_w_ple_in, m_g_post_ple, v_w_in, v_b_gate, v_w_ret_out, v_w_dil_out, v_w_o, v_g_pre_mix, v_g_post_mix, v_g_pre_mlp, v_g_post_mlp, v_w_up, v_w_down, v_g_pre_ple, v_w_ple_gate, v_b_ple_gate, v_w_ple_in, v_g_post_ple):
    s = x.shape[1]
    wd = dict(w_in=w_in, b_gate=b_gate, w_ret_out=w_ret_out, w_dil_out=w_dil_out, w_o=w_o, g_pre_mix=g_pre_mix,
              g_post_mix=g_post_mix, g_pre_mlp=g_pre_mlp, g_post_mlp=g_post_mlp, w_up=w_up, w_down=w_down,
              g_pre_ple=g_pre_ple, w_ple_gate=w_ple_gate, b_ple_gate=b_ple_gate, w_ple_in=w_ple_in, g_post_ple=g_post_ple)
    md = dict(w_in=m_w_in, b_gate=m_b_gate, w_ret_out=m_w_ret_out, w_dil_out=m_w_dil_out, w_o=m_w_o, g_pre_mix=m_g_pre_mix,
              g_post_mix=m_g_post_mix, g_pre_mlp=m_g_pre_mlp, g_post_mlp=m_g_post_mlp, w_up=m_w_up, w_down=m_w_down,
              g_pre_ple=m_g_pre_ple, w_ple_gate=m_w_ple_gate, b_ple_gate=m_b_ple_gate, w_ple_in=m_w_ple_in, g_post_ple=m_g_post_ple)
    vd = dict(w_in=v_w_in, b_gate=v_b_gate, w_ret_out=v_w_ret_out, w_dil_out=v_w_dil_out, w_o=v_w_o, g_pre_mix=v_g_pre_mix,
              g_post_mix=v_g_post_mix, g_pre_mlp=v_g_pre_mlp, g_post_mlp=v_g_post_mlp, w_up=v_w_up, w_down=v_w_down,
              g_pre_ple=v_g_pre_ple, w_ple_gate=v_w_ple_gate, b_ple_gate=v_b_ple_gate, w_ple_in=v_w_ple_in, g_post_ple=v_g_post_ple)

    shards = {n: (wd[n][0].T if n in _TRANSPOSED else wd[n][0]).astype(BF16) for n in _MATS}
    w_in_all, bg_all = _all_gather([shards.pop("w_in"), b_gate[0]])
    wts = {"w_in": w_in_all.reshape(N_DEV * w_in_all.shape[1], D_MODEL)}
    bg = bg_all.transpose(1, 0, 2).reshape(2, D_MODEL)
    vec = {n: wd[n] for n in _VECS}
    vec.update(b0=bg[0:1], b1=bg[1:2], b_ple=b_ple_gate)

    tabs = _rotary_tables(positions[0], s)
    grad_x, slots, packet = _local_step(x[0], p[0, 0].astype(BF16), loss_target[0], tabs, wts, vec, s, late_shards=shards)

    (packets,) = _run_exchange(_Exchange([packet], [False]), "exchange_vectors")
    out = {}
    for n in _MATS:
        sl = slots[n]
        if n in _TRANSPOSED:
            sl = _sum_slots(sl, "sum_" + n).T[None]
        out[n] = _adamw(sl, wd[n][0], md[n][0], vd[n][0], "adamw_" + n)
    zero_rows = jnp.zeros((16 - len(_VECS), D_MODEL), F32)
    pack = lambda d: jnp.concatenate([d[n] for n in _VECS] + [zero_rows], axis=0)
    small = _adamw(packets, pack(wd), pack(md), pack(vd), "adamw_vectors")
    for i, n in enumerate(_VECS):
        out[n] = tuple(t[i:i + 1] for t in small)
    my = 4 * lax.axis_index("x") + 2 * lax.axis_index("y") + lax.axis_index("c")
    g_bias = lax.dynamic_slice(small[0], (8, my * 128), (2, 128))
    out["b_gate"] = _adamw(g_bias[None], b_gate[0], m_b_gate[0], v_b_gate[0], "adamw_b_gate")
    loss = small[0][7, 0]

    res = [loss, grad_x[None]]
    for kk in range(4):
        res += [out[n][kk][None] if out[n][kk].ndim == 2 and wd[n].ndim == 3 else out[n][kk] for n in _ORDER]
    return tuple(res)
```

```python
import functools
import math

import numpy as np
import jax
import jax.numpy as jnp
from jax import lax
from jax.experimental import pallas as pl
from jax.experimental.pallas import tpu as pltpu

F32, BF16 = jnp.float32, jnp.bfloat16
D_MODEL = 1024
EPS = 1e-6
N_DEV = 8
RET_HEADS, RET_QK, RET_V, RET_CHUNK = 4, 256, 512, 128
DIL_GROUPS = (1, 4, 16)
DIL_W = 512
QB = 128
NEG = -1e30
ADAM_LR, ADAM_B1, ADAM_B2, ADAM_EPS, ADAM_WD, ADAM_STEP = 0.001, 0.9, 0.999, 1e-08, 0.01, 10
VMEM_LIMIT_BYTES = 56 * 1024 * 1024
MESH = pl.DeviceIdType.MESH

NN = ((1,), (0,))
NT = ((1,), (1,))
TN = ((0,), (0,))


def _dot(a, b, dn):
    return lax.dot_general(a, b, (dn, ((), ())), preferred_element_type=F32)


def _cparams(sem):
    return pltpu.CompilerParams(dimension_semantics=sem, vmem_limit_bytes=VMEM_LIMIT_BYTES)


def _rms(x):
    return x * lax.rsqrt(jnp.mean(x * x, axis=-1, keepdims=True) + EPS)


def _rms_bwd(x, g, dy):
    r = lax.rsqrt(jnp.mean(x * x, axis=-1, keepdims=True) + EPS)
    xh = x * r
    t = dy * g
    dx = r * (t - xh * jnp.mean(t * xh, axis=-1, keepdims=True))
    return dx, dy * xh


def _colsum(v):
    return jnp.sum(v, axis=0, keepdims=True)


def _sigmoid(v):
    return 1.0 / (1.0 + jnp.exp(-v))


def _matmul(a, b, *, mode, m, n, k, tm, tn, tk, out_dtype, name, a_fn=None, epi=None, epi_fn=None, carry=None):
    nk = k // tk
    grid = (m // tm, n // tn, nk)
    nc = carry.n if carry is not None else 0
    if mode == "nn":
        a_blk, a_im, b_blk, b_im, dn = (tm, tk), (lambda i, j, kk: (i, kk)), (tk, tn), (lambda i, j, kk: (kk, j)), NN
    elif mode == "nt":
        a_blk, a_im, b_blk, b_im, dn = (tm, tk), (lambda i, j, kk: (i, kk)), (tn, tk), (lambda i, j, kk: (j, kk)), NT
    else:
        a_blk, a_im, b_blk, b_im, dn = (tk, tm), (lambda i, j, kk: (kk, i)), (tk, tn), (lambda i, j, kk: (kk, j)), TN
    o_im = lambda i, j, kk: (i, j)
    n_in = 2 + (epi is not None)

    def body(*refs):
        a_ref, b_ref = refs[0], refs[1]
        e_ref = refs[2] if epi is not None else None
        o_ref = refs[n_in + nc]
        acc_ref = refs[n_in + 2 * nc + 1] if nk > 1 else None
        if carry is not None:
            step = (pl.program_id(0) * grid[1] + pl.program_id(1)) * nk + pl.program_id(2)
            pl.when(step == 0)(lambda: carry.start(*carry.split(refs, n_in, 1)))

        def finish(acc):
            if e_ref is not None:
                acc = epi_fn(acc, e_ref[...])
            o_ref[...] = acc.astype(o_ref.dtype)

        av = a_ref[...]
        if a_fn is not None:
            av = a_fn(av)
        part = _dot(av, b_ref[...], dn)
        if nk == 1:
            finish(part)
        else:
            kk = pl.program_id(2)

            @pl.when(kk == 0)
            def _():
                acc_ref[...] = part

            @pl.when(kk > 0)
            def _():
                acc_ref[...] += part

            @pl.when(kk == nk - 1)
            def _():
                finish(acc_ref[...])

        if carry is not None:
            pl.when(step == grid[0] * grid[1] * nk - 1)(lambda: carry.wait(*carry.split(refs, n_in, 1)))

    in_specs = [pl.BlockSpec(a_blk, a_im), pl.BlockSpec(b_blk, b_im)]
    args = [a, b]
    if epi is not None:
        in_specs.append(pl.BlockSpec((tm, tn), o_im))
        args.append(epi)
    out_specs = pl.BlockSpec((tm, tn), o_im)
    out_shape = jax.ShapeDtypeStruct((m, n), out_dtype)
    scratch = [pltpu.VMEM((tm, tn), F32)] if nk > 1 else []
    if carry is None:
        return pl.pallas_call(
            body, name=name, grid=grid, in_specs=in_specs, out_specs=out_specs, out_shape=out_shape,
            scratch_shapes=scratch, compiler_params=_cparams(("parallel", "parallel", "arbitrary")),
        )(*args)
    res = pl.pallas_call(
        body, name=name, grid=grid, in_specs=in_specs + carry.specs, out_specs=[out_specs] + carry.specs,
        out_shape=[out_shape] + carry.out_shape, scratch_shapes=scratch + carry.scratch,
        compiler_params=_cparams(("arbitrary", "arbitrary", "arbitrary")),
    )(*args, *carry.arrays)
    return res[0], res[1:]


def _relu_sq(v):
    r = jnp.maximum(v.astype(F32), 0.0)
    return (r * r).astype(BF16)


def _rowwise(name, fn, s, tr, rows, vecs, outs, accs=()):
    n_r, n_v, n_o, n_a = len(rows), len(vecs), len(outs), len(accs)

    def body(*refs):
        vals = [refs[i][...] for i in range(n_r + n_v)]
        o_refs = refs[n_r + n_v:n_r + n_v + n_o]
        a_refs = refs[n_r + n_v + n_o:]
        o_vals, a_vals = fn(*vals)
        for ref, val in zip(o_refs, o_vals):
            ref[...] = val.astype(ref.dtype)
        if n_a:
            @pl.when(pl.program_id(0) == 0)
            def _():
                for ref in a_refs:
                    ref[...] = jnp.zeros_like(ref)

            for ref, val in zip(a_refs, a_vals):
                ref[...] += val

    in_specs = [pl.BlockSpec((tr, w), functools.partial(lambda i, cb: (i, cb), cb=cb)) for _, w, cb in rows]
    in_specs += [pl.BlockSpec(v.shape, lambda i: (0, 0)) for v in vecs]
    out_specs = [pl.BlockSpec((tr, w), lambda i: (i, 0)) for w, _ in outs]
    out_specs += [pl.BlockSpec((1, w), lambda i: (0, 0)) for w in accs]
    out_shape = [jax.ShapeDtypeStruct((s, w), dt) for w, dt in outs]
    out_shape += [jax.ShapeDtypeStruct((1, w), F32) for w in accs]
    res = pl.pallas_call(
        body, name=name, grid=(s // tr,), in_specs=in_specs, out_specs=out_specs, out_shape=out_shape,
        compiler_params=_cparams(("arbitrary",)),
    )(*[r[0] for r in rows], *vecs)
    return res[:n_o], res[n_o:]


_ROW_TILE = 256
_STREAM_SPECS = [pl.BlockSpec((dil, _ROW_TILE // dil, D_MODEL), lambda i: (0, i, 0)) for dil in DIL_GROUPS[1:]]
_NAT_SPEC = pl.BlockSpec((_ROW_TILE, D_MODEL), lambda i: (i, 0))
_VEC_SPEC = pl.BlockSpec((1, D_MODEL), lambda i: (0, 0))
_COL_BLOCKS = pltpu.VMEM((D_MODEL // 128, _ROW_TILE, 128), F32)


def _prenorm(xs, g, s):
    tr = _ROW_TILE

    def body(x_ref, g_ref, u_ref, u4_ref, u16_ref, buf):
        xn = _rms(x_ref[...]) * g_ref[...]
        u_ref[...] = xn.astype(BF16)
        for cb in range(8):
            buf[cb] = xn[:, cb * 128:(cb + 1) * 128]
        for dil, out in ((4, u4_ref), (16, u16_ref)):
            for c in range(dil):
                rows = pl.ds(c, tr // dil, stride=dil)
                out[c] = jnp.concatenate([buf.at[cb][rows, :] for cb in range(8)], axis=1).astype(BF16)

    res = pl.pallas_call(
        body, name="prenorm", grid=(s // tr,), in_specs=[_NAT_SPEC, _VEC_SPEC], out_specs=[_NAT_SPEC] + _STREAM_SPECS,
        out_shape=[jax.ShapeDtypeStruct((s, D_MODEL), BF16)]
        + [jax.ShapeDtypeStruct((dil, s // dil, D_MODEL), BF16) for dil in DIL_GROUPS[1:]],
        scratch_shapes=[_COL_BLOCKS], compiler_params=_cparams(("parallel",)),
    )(xs, g)
    return [r.reshape(s, D_MODEL) for r in res]


def _grad_x(xs, d_h1, du_nat, du4, du16, g, s):
    tr = _ROW_TILE

    def body(x_ref, dh_ref, a_ref, b_ref, c_ref, u4_ref, u16_ref, g_ref, dx_ref, dg_ref, buf):
        du = a_ref[...] + b_ref[...] + c_ref[...]
        for dil, src in ((4, u4_ref), (16, u16_ref)):
            for c in range(dil):
                part = src[c]
                for cb in range(8):
                    buf.at[cb][pl.ds(c, tr // dil, stride=dil), :] = part[:, cb * 128:(cb + 1) * 128]
            du = du + jnp.concatenate([buf[cb] for cb in range(8)], axis=1)
        dx, dgr = _rms_bwd(x_ref[...], g_ref[...], du)
        dx_ref[...] = dh_ref[...] + dx

        @pl.when(pl.program_id(0) == 0)
        def _():
            dg_ref[...] = jnp.zeros_like(dg_ref)

        dg_ref[...] += _colsum(dgr)

    return pl.pallas_call(
        body, name="grad_x", grid=(s // tr,), in_specs=[_NAT_SPEC] * 5 + _STREAM_SPECS + [_VEC_SPEC],
        out_specs=[_NAT_SPEC, _VEC_SPEC],
        out_shape=[jax.ShapeDtypeStruct((s, D_MODEL), F32), jax.ShapeDtypeStruct((1, D_MODEL), F32)],
        scratch_shapes=[_COL_BLOCKS], compiler_params=_cparams(("arbitrary",)),
    )(xs, d_h1, *du_nat, du4.reshape(4, s // 4, D_MODEL), du16.reshape(16, s // 16, D_MODEL), g)


def _ret_tables():
    h = np.arange(RET_HEADS, dtype=np.float32)
    lg = np.log1p(-(np.float32(2.0) ** (-5.0 - h))).astype(np.float32)
    idx = np.arange(RET_CHUNK, dtype=np.float32)
    diff = idx[:, None] - idx[None, :]
    dm = np.where(diff[None] >= 0, np.exp(np.maximum(diff, 0.0)[None] * lg[:, None, None]), 0.0)
    qd = np.exp((idx + 1.0)[None, :, None] * lg[:, None, None])
    kd = np.exp((RET_CHUNK - 1.0 - idx)[None, :, None] * lg[:, None, None])
    cd = np.exp(RET_CHUNK * lg)[:, None, None]
    return [jnp.asarray(t, F32) for t in (dm, qd, kd, cd)]


def _rope_half(v, cos, sin):
    v1, v2 = v[:, :128], v[:, 128:]
    return jnp.concatenate([v1 * cos - v2 * sin, v2 * cos + v1 * sin], axis=1)


def _unrope_half(d, cos, sin):
    d1, d2 = d[:, :128], d[:, 128:]
    return jnp.concatenate([d1 * cos + d2 * sin, d2 * cos - d1 * sin], axis=1)


def _ret_specs(rb, rev_n):
    def rowmap(w_blk):
        return lambda h, n: (rev_n(n), w_blk(h))
    tab = [pl.BlockSpec((1, RET_CHUNK, RET_CHUNK), lambda h, n: (h, 0, 0)),
           pl.BlockSpec((1, RET_CHUNK, 1), lambda h, n: (h, 0, 0)),
           pl.BlockSpec((1, RET_CHUNK, 1), lambda h, n: (h, 0, 0)),
           pl.BlockSpec((1, 1, 1), lambda h, n: (h, 0, 0))]
    proj = pl.BlockSpec((rb, 1536), rowmap(lambda h: h))
    cs = pl.BlockSpec((rb, 128), rowmap(lambda h: 0))
    hv = pl.BlockSpec((rb, RET_V), rowmap(lambda h: h))
    return proj, cs, hv, tab


def _ret_fwd(proj_ret, cos, sin, s):
    rb = min(512, s)
    ch = rb // RET_CHUNK
    nb = s // rb
    proj_spec, cs_spec, hv_spec, tab_specs = _ret_specs(rb, lambda n: n)

    def body(p_ref, cos_ref, sin_ref, dm_ref, qd_ref, kd_ref, cd_ref, yr_ref, y_ref, rs_ref, r_acc):
        @pl.when(pl.program_id(1) == 0)
        def _():
            r_acc[...] = jnp.zeros_like(r_acc)

        dm, qd, kd, cd = dm_ref[0], qd_ref[0], kd_ref[0], cd_ref[0]
        for c in range(ch):
            rows = slice(c * RET_CHUNK, (c + 1) * RET_CHUNK)
            cosv, sinv = cos_ref[rows, :], sin_ref[rows, :]
            q = _rope_half(p_ref[rows, 0:256].astype(F32), cosv, sinv)
            kk = _rope_half(p_ref[rows, 256:512].astype(F32), cosv, sinv) * (RET_QK ** -0.5)
            v = p_ref[rows, 512:1024]
            g = p_ref[rows, 1024:1536].astype(F32)
            rb16 = r_acc[...].astype(BF16)
            rs_ref[0, c] = rb16
            sc = _dot(q.astype(BF16), kk.astype(BF16), NT) * dm
            y = _dot(sc.astype(BF16), v, NN) + _dot((q * qd).astype(BF16), rb16, NN)
            r_acc[...] = r_acc[...] * cd + _dot((kk * kd).astype(BF16), v, TN)
            y_ref[rows, :] = y.astype(BF16)
            yr_ref[rows, :] = (_rms(y) * (g * _sigmoid(g))).astype(BF16)

    return pl.pallas_call(
        body, name="ret_fwd", grid=(RET_HEADS, nb),
        in_specs=[proj_spec, cs_spec, cs_spec] + tab_specs,
        out_specs=[hv_spec, hv_spec, pl.BlockSpec((1, ch, RET_QK, RET_V), lambda h, n: (h, n, 0, 0))],
        out_shape=[jax.ShapeDtypeStruct((s, RET_HEADS * RET_V), BF16), jax.ShapeDtypeStruct((s, RET_HEADS * RET_V), BF16),
                   jax.ShapeDtypeStruct((RET_HEADS, s // RET_CHUNK, RET_QK, RET_V), BF16)],
        scratch_shapes=[pltpu.VMEM((RET_QK, RET_V), F32)],
        compiler_params=_cparams(("parallel", "arbitrary")),
    )(proj_ret, cos, sin, *_ret_tables())


def _ret_bwd(proj_ret, cos, sin, y, d_yr, rs, s, carry=None):
    rb = min(512, s)
    ch = rb // RET_CHUNK
    nb = s // rb
    proj_spec, cs_spec, hv_spec, tab_specs = _ret_specs(rb, lambda n: nb - 1 - n)
    nc = carry.n if carry is not None else 0
    n_in = 10

    def body(*refs):
        step = pl.program_id(0) * nb + pl.program_id(1)
        if carry is not None:
            pl.when(step == 0)(lambda: carry.start(*carry.split(refs, n_in, 1)))
        compute(*refs[:n_in], refs[n_in + nc], refs[n_in + 2 * nc + 1])
        if carry is not None:
            pl.when(step == RET_HEADS * nb - 1)(lambda: carry.wait(*carry.split(refs, n_in, 1)))

    def compute(p_ref, cos_ref, sin_ref, y_ref, dyr_ref, rs_ref, dm_ref, qd_ref, kd_ref, cd_ref, o_ref, dr_acc):
        @pl.when(pl.program_id(1) == 0)
        def _():
            dr_acc[...] = jnp.zeros_like(dr_acc)

        dm, qd, kd, cd = dm_ref[0], qd_ref[0], kd_ref[0], cd_ref[0]
        for c in reversed(range(ch)):
            rows = slice(c * RET_CHUNK, (c + 1) * RET_CHUNK)
            cosv, sinv = cos_ref[rows, :], sin_ref[rows, :]
            q = _rope_half(p_ref[rows, 0:256].astype(F32), cosv, sinv)
            kk = _rope_half(p_ref[rows, 256:512].astype(F32), cosv, sinv) * (RET_QK ** -0.5)
            v = p_ref[rows, 512:1024]
            g = p_ref[rows, 1024:1536].astype(F32)
            yv = y_ref[rows, :].astype(F32)
            dyr = dyr_ref[rows, :].astype(F32)
            sg = _sigmoid(g)
            r = lax.rsqrt(jnp.mean(yv * yv, axis=-1, keepdims=True) + EPS)
            yn = yv * r
            dg = dyr * yn * (sg * (1.0 + g * (1.0 - sg)))
            dyn = dyr * (g * sg)
            dy = (r * (dyn - yn * jnp.mean(dyn * yn, axis=-1, keepdims=True))).astype(BF16)
            qb, kb = q.astype(BF16), kk.astype(BF16)
            rb16 = rs_ref[0, c]
            drb = dr_acc[...].astype(BF16)
            sd = _dot(qb, kb, NT) * dm
            ds = (_dot(dy, v, NT) * dm).astype(BF16)
            dq = _dot(ds, kb, NN) + qd * _dot(dy, rb16, NT)
            dk = _dot(ds, qb, TN) + kd * _dot(v, drb, NT)
            dv = _dot(sd.astype(BF16), dy, TN) + _dot((kk * kd).astype(BF16), drb, NN)
            dr_acc[...] = dr_acc[...] * cd + _dot((q * qd).astype(BF16), dy, TN)
            o_ref[rows, 0:256] = _unrope_half(dq, cosv, sinv).astype(BF16)
            o_ref[rows, 256:512] = (_unrope_half(dk, cosv, sinv) * (RET_QK ** -0.5)).astype(BF16)
            o_ref[rows, 512:1024] = dv.astype(BF16)
            o_ref[rows, 1024:1536] = dg.astype(BF16)

    in_specs = [proj_spec, cs_spec, cs_spec, hv_spec, hv_spec,
                pl.BlockSpec((1, ch, RET_QK, RET_V), lambda h, n: (h, nb - 1 - n, 0, 0))] + tab_specs
    out_shape = jax.ShapeDtypeStruct((s, RET_HEADS * 1536), BF16)
    scratch = [pltpu.VMEM((RET_QK, RET_V), F32)]
    args = (proj_ret, cos, sin, y, d_yr, rs, *_ret_tables())
    if carry is None:
        return pl.pallas_call(body, name="ret_bwd", grid=(RET_HEADS, nb), in_specs=in_specs, out_specs=proj_spec,
                              out_shape=out_shape, scratch_shapes=scratch, compiler_params=_cparams(("parallel", "arbitrary")))(*args)
    res = pl.pallas_call(
        body, name="ret_bwd", grid=(RET_HEADS, nb), in_specs=in_specs + carry.specs, out_specs=[proj_spec] + carry.specs,
        out_shape=[out_shape] + carry.out_shape, scratch_shapes=scratch + carry.scratch,
        compiler_params=_cparams(("arbitrary", "arbitrary")),
    )(*args, *carry.arrays)
    return res[0], res[1:]


def _rope_qk(qkv, tc, ts1, ts2, s, name):
    def fn(q, k, c, s1, s2):
        outs = []
        for v in (q, k):
            for cc in range(4):
                vv = v[:, cc * 128:(cc + 1) * 128].astype(F32)
                outs.append(vv * c + pltpu.roll(vv, 120, 1) * s1 + pltpu.roll(vv, 8, 1) * s2)
        return [jnp.concatenate(outs, axis=1)], []

    (out,), _ = _rowwise(name, fn, s, min(512, s), [(qkv, 512, 0), (qkv, 512, 1), (tc, 128, 0), (ts1, 128, 0), (ts2, 128, 0)],
                         [], [(1024, BF16)])
    return out


def _pair_masks():
    ri = lax.broadcasted_iota(jnp.int32, (2 * QB, 2 * QB), 0)
    ci = lax.broadcasted_iota(jnp.int32, (2 * QB, 2 * QB), 1)
    e = ci - (ri & (QB - 1))
    lane_lo = lax.broadcasted_iota(jnp.int32, (2 * QB, 128), 1) < 64
    return ci, jnp.logical_and(e >= 0, e <= QB), lane_lo


def _stack_heads(v, lane_lo):
    z = jnp.zeros_like(v)
    return jnp.concatenate([jnp.where(lane_lo, v, z), jnp.where(lane_lo, z, v)], axis=0)


def _dil_fwd(qkr, qkv, dil, s, name):
    length = s // dil
    rb = min(512, length)
    nsub = rb // QB
    nbs = length // rb
    sub_per = rb // QB

    def body(q_ref, k_ref, v_ref, kp_ref, vp_ref, o_ref, l_ref, kf, vf):
        first = (pl.program_id(0) % nbs) == 0
        kf[0:QB, :] = kp_ref[...]
        kf[QB:, :] = k_ref[...]
        vf[0:QB, :] = vp_ref[...]
        vf[QB:, :] = v_ref[...]
        ci, band, lane_lo = _pair_masks()
        lo1 = lane_lo[0:QB]

        def step(i, carry):
            r0 = pl.multiple_of(i * QB, QB)
            mask = jnp.logical_and(band, ci >= jnp.where(jnp.logical_and(first, i == 0), QB, 0))
            for j in range(4):
                lanes = slice(j * 128, (j + 1) * 128)
                q2 = _stack_heads(q_ref[pl.ds(r0, QB), lanes], lo1)
                k2 = kf[pl.ds(r0, 2 * QB), lanes]
                v2 = _stack_heads(vf[pl.ds(r0, 2 * QB), lanes], lane_lo)
                sc = jnp.where(mask, _dot(q2, k2, NT) * 0.125, NEG)
                m = jnp.max(sc, axis=1, keepdims=True)
                p = jnp.exp(sc - m)
                den = jnp.sum(p, axis=1, keepdims=True)
                pb = p.astype(BF16)
                o = _dot(jnp.concatenate([pb[0:QB], pb[QB:]], axis=1), v2, NN)
                inv = 1.0 / den
                lse = m + jnp.log(den)
                o_ref[pl.ds(r0, QB), lanes] = o * jnp.where(lo1, inv[0:QB], inv[QB:])
                l_ref[pl.ds(r0, QB), lanes] = jnp.where(lo1, lse[0:QB], lse[QB:])
            return carry

        lax.fori_loop(0, nsub, step, 0)

    prev = lambda n: jnp.maximum(n * sub_per - 1, 0)
    cur = lambda cb: (lambda n: (n, cb))
    return pl.pallas_call(
        body, name=name, grid=(s // rb,),
        in_specs=[pl.BlockSpec((rb, DIL_W), cur(0)), pl.BlockSpec((rb, DIL_W), cur(1)), pl.BlockSpec((rb, DIL_W), cur(2)),
                  pl.BlockSpec((QB, DIL_W), lambda n: (prev(n), 1)), pl.BlockSpec((QB, DIL_W), lambda n: (prev(n), 2))],
        out_specs=[pl.BlockSpec((rb, DIL_W), cur(0)), pl.BlockSpec((rb, DIL_W), cur(0))],
        out_shape=[jax.ShapeDtypeStruct((s, DIL_W), F32), jax.ShapeDtypeStruct((s, DIL_W), F32)],
        scratch_shapes=[pltpu.VMEM((QB + rb, DIL_W), BF16), pltpu.VMEM((QB + rb, DIL_W), BF16)],
        compiler_params=_cparams(("parallel",)),
    )(qkr, qkr, qkv, qkr, qkv)


def _dil_bwd(qkr, qkv, dya, lse, dlt, tc, ts1, ts2, dil, s, name):
    length = s // dil
    rb = min(512, length)
    nsub = rb // QB
    nbs = length // rb
    last_blk = s // QB - 1

    def body(q_ref, k_ref, v_ref, kp_ref, vp_ref, qn_ref, dy_ref, dyn_ref, l_ref, ln_ref, d_ref, dn_ref,
             c_ref, s1_ref, s2_ref, o_ref, kf, vf, qf, dyf, lf, df, dqa, dka, dva):
        nl = pl.program_id(0) % nbs
        first, last = nl == 0, nl == nbs - 1
        kf[0:QB, :] = kp_ref[...]
        kf[QB:QB + rb, :] = k_ref[...]
        kf[QB + rb:, :] = jnp.zeros((QB, DIL_W), BF16)
        vf[0:QB, :] = vp_ref[...]
        vf[QB:QB + rb, :] = v_ref[...]
        vf[QB + rb:, :] = jnp.zeros((QB, DIL_W), BF16)
        qf[0:rb, :] = q_ref[...]
        qf[rb:, :] = qn_ref[...]
        dyf[0:rb, :] = dy_ref[...]
        dyf[rb:, :] = dyn_ref[...]
        lf[0:rb, :] = l_ref[...]
        lf[rb:, :] = ln_ref[...]
        df[0:rb, :] = d_ref[...]
        df[rb:, :] = dn_ref[...]
        dka[...] = jnp.zeros_like(dka)
        dva[...] = jnp.zeros_like(dva)
        ci, band, lane_lo = _pair_masks()
        lo1 = lane_lo[0:QB]

        def step(qi, carry):
            qr = pl.multiple_of(qi * QB, QB)
            is_next = qi == nsub
            cmin = jnp.where(jnp.logical_and(first, qi == 0), QB, 0)
            cmax = jnp.where(is_next, jnp.where(last, -1, QB - 1), 2 * QB - 1)
            mask = jnp.logical_and(band, jnp.logical_and(ci >= cmin, ci <= cmax))
            for j in range(4):
                lanes = slice(j * 128, (j + 1) * 128)
                q2 = _stack_heads(qf[pl.ds(qr, QB), lanes], lo1)
                do2 = _stack_heads(dyf[pl.ds(qr, QB), lanes], lo1)
                lv = lf[pl.ds(qr, QB), lanes]
                dl = df[pl.ds(qr, QB), lanes]
                lse2 = jnp.concatenate([lv[:, 0:1], lv[:, 64:65]], axis=0)
                dl2 = jnp.concatenate([dl[:, 0:1], dl[:, 64:65]], axis=0)
                k2 = kf[pl.ds(qr, 2 * QB), lanes]
                v2 = vf[pl.ds(qr, 2 * QB), lanes]
                sc = _dot(q2, k2, NT) * 0.125
                p = jnp.where(mask, jnp.exp(jnp.minimum(sc - lse2, 0.0)), 0.0)
                ds = (p * (_dot(do2, v2, NT) - dl2) * 0.125).astype(BF16)
                dq = _dot(jnp.concatenate([ds[0:QB], ds[QB:]], axis=1), _stack_heads(k2, lane_lo), NN)
                dqa[pl.ds(qr, QB), lanes] = dq
                dka[pl.ds(qr, 2 * QB), lanes] += _dot(ds, q2, TN)
                dva[pl.ds(qr, 2 * QB), lanes] += _dot(p.astype(BF16), do2, TN)
            return carry

        lax.fori_loop(0, nsub + 1, step, 0)
        cv, s1v, s2v = c_ref[...], s1_ref[...], s2_ref[...]

        def unrope(d):
            return d * cv + pltpu.roll(d * s1v, 8, 1) + pltpu.roll(d * s2v, 120, 1)

        for cc in range(4):
            lanes = slice(cc * 128, (cc + 1) * 128)
            o_ref[:, cc * 128:(cc + 1) * 128] = unrope(dqa[0:rb, lanes]).astype(BF16)
            o_ref[:, 512 + cc * 128:512 + (cc + 1) * 128] = unrope(dka[QB:QB + rb, lanes]).astype(BF16)
            o_ref[:, 1024 + cc * 128:1024 + (cc + 1) * 128] = dva[QB:QB + rb, lanes].astype(BF16)

    prev = lambda n: jnp.maximum(n * nsub - 1, 0)
    nxt = lambda n: jnp.minimum(n * nsub + nsub, last_blk)
    cur = lambda cb: (lambda n: (n, cb))
    big = lambda cb: pl.BlockSpec((rb, DIL_W), cur(cb))
    small = lambda im: pl.BlockSpec((QB, DIL_W), im)
    tab = pl.BlockSpec((rb, 128), cur(0))
    return pl.pallas_call(
        body, name=name, grid=(s // rb,),
        in_specs=[big(0), big(1), big(2), small(lambda n: (prev(n), 1)), small(lambda n: (prev(n), 2)),
                  small(lambda n: (nxt(n), 0)), big(0), small(lambda n: (nxt(n), 0)), big(0), small(lambda n: (nxt(n), 0)),
                  big(0), small(lambda n: (nxt(n), 0)), tab, tab, tab],
        out_specs=pl.BlockSpec((rb, 3 * DIL_W), cur(0)),
        out_shape=jax.ShapeDtypeStruct((s, 3 * DIL_W), BF16),
        scratch_shapes=[pltpu.VMEM((rb + 2 * QB, DIL_W), BF16), pltpu.VMEM((rb + 2 * QB, DIL_W), BF16),
                        pltpu.VMEM((rb + QB, DIL_W), BF16), pltpu.VMEM((rb + QB, DIL_W), BF16),
                        pltpu.VMEM((rb + QB, DIL_W), F32), pltpu.VMEM((rb + QB, DIL_W), F32),
                        pltpu.VMEM((rb + QB, DIL_W), F32), pltpu.VMEM((rb + 2 * QB, DIL_W), F32),
                        pltpu.VMEM((rb + 2 * QB, DIL_W), F32)],
        compiler_params=_cparams(("parallel",)),
    )(qkr, qkr, qkv, qkr, qkv, qkr, dya, dya, lse, lse, dlt, dlt, tc, ts1, ts2)


def _stream_specs(tr):
    nat = pl.BlockSpec((tr, 128), lambda i, j: (i, j))
    return [nat] + [pl.BlockSpec((dil, tr // dil, 128), lambda i, j: (0, i, j)) for dil in DIL_GROUPS[1:]]


def _dil_merge(o_g, l_g, s):
    tr = min(512, s)
    nat, sp4, sp16 = _stream_specs(tr)

    def body(o0_ref, l0_ref, o1_ref, l1_ref, o2_ref, l2_ref, ya_ref, lse_ref, o1n, l1n, o2n, l2n):
        for src, dst, dil in ((o1_ref, o1n, 4), (l1_ref, l1n, 4), (o2_ref, o2n, 16), (l2_ref, l2n, 16)):
            for c in range(dil):
                dst[pl.ds(c, tr // dil, stride=dil), :] = src[c]
        l0, l1, l2 = l0_ref[...], l1n[...], l2n[...]
        m = jnp.maximum(jnp.maximum(l0, l1), l2)
        e0, e1, e2 = jnp.exp(l0 - m), jnp.exp(l1 - m), jnp.exp(l2 - m)
        den = e0 + e1 + e2
        ya_ref[...] = ((e0 * o0_ref[...] + e1 * o1n[...] + e2 * o2n[...]) / den).astype(BF16)
        lse_ref[...] = m + jnp.log(den)

    v3 = lambda a, dil: a.reshape(dil, s // dil, DIL_W)
    return pl.pallas_call(
        body, name="dil_merge", grid=(s // tr, 4),
        in_specs=[nat, nat, sp4, sp4, sp16, sp16], out_specs=[nat, nat],
        out_shape=[jax.ShapeDtypeStruct((s, DIL_W), BF16), jax.ShapeDtypeStruct((s, DIL_W), F32)],
        scratch_shapes=[pltpu.VMEM((tr, 128), F32)] * 4,
        compiler_params=_cparams(("parallel", "parallel")),
    )(o_g[0], l_g[0], v3(o_g[1], 4), v3(l_g[1], 4), v3(o_g[2], 16), v3(l_g[2], 16))


def _dil_bwd_prep(d_ya, ya, lse, s):
    tr = min(512, s)
    nat, sp4, sp16 = _stream_specs(tr)

    def body(dya_ref, ya_ref, lse_ref, dy0, dl0, dy1, ls1, dl1, dy2, ls2, dl2, dlt):
        lane_lo = lax.broadcasted_iota(jnp.int32, (tr, 128), 1) < 64
        prod = dya_ref[...] * ya_ref[...].astype(F32)
        lo = jnp.where(lane_lo, prod, 0.0)
        dlt[...] = jnp.where(lane_lo, jnp.sum(lo, axis=1, keepdims=True), jnp.sum(prod - lo, axis=1, keepdims=True))
        dy0[...] = dya_ref[...].astype(BF16)
        dl0[...] = dlt[...]
        for dil, dy, ls, dl in ((4, dy1, ls1, dl1), (16, dy2, ls2, dl2)):
            for c in range(dil):
                rows = pl.ds(c, tr // dil, stride=dil)
                dy[c] = dya_ref[rows, :].astype(BF16)
                ls[c] = lse_ref[rows, :]
                dl[c] = dlt[rows, :]

    sh = lambda dil, dt: jax.ShapeDtypeStruct((dil, s // dil, DIL_W), dt)
    res = pl.pallas_call(
        body, name="dil_bwd_prep", grid=(s // tr, 4),
        in_specs=[nat, nat, nat], out_specs=[nat, nat, sp4, sp4, sp4, sp16, sp16, sp16],
        out_shape=[jax.ShapeDtypeStruct((s, DIL_W), BF16), jax.ShapeDtypeStruct((s, DIL_W), F32),
                   sh(4, BF16), sh(4, F32), sh(4, F32), sh(16, BF16), sh(16, F32), sh(16, F32)],
        scratch_shapes=[pltpu.VMEM((tr, 128), F32)],
        compiler_params=_cparams(("parallel", "parallel")),
    )(d_ya, ya, lse)
    dy0, dl0, dy1, ls1, dl1, dy2, ls2, dl2 = [r.reshape(s, DIL_W) for r in res]
    return [(dy0, lse, dl0), (dy1, ls1, dl1), (dy2, ls2, dl2)]


_RET_ROWS = [(256 * h, 256) for h in range(4)], [(1024 + 256 * h, 256) for h in range(4)], \
            [(2048 + 512 * h, 512) for h in range(4)], [(4096 + 512 * h, 512) for h in range(4)]


def _split_w_in(win):
    rows = lambda a, n: win[a:a + n]
    w_ret = jnp.concatenate([rows(*seg[h]) for h in range(RET_HEADS) for seg in _RET_ROWS], axis=0)
    w_dil = [jnp.concatenate([rows(base + 512 * g, 512) for base in (6144, 7680, 9216)], axis=0) for g in range(3)]
    return w_ret, win[10752:12800], w_dil


def _join_w_in(g_ret, g_gate, g_dil):
    parts = []
    for i in range(4):
        off = (0, 256, 512, 1024)[i]
        parts += [g_ret[1536 * h + off:1536 * h + off + _RET_ROWS[i][h][1]] for h in range(RET_HEADS)]
    for i in range(3):
        parts += [g_dil[g][512 * i:512 * (i + 1)] for g in range(3)]
    return jnp.concatenate(parts + [g_gate], axis=0)


def _local_step(xs, pb, tgt, tabs, wts, vec, s, late_shards=None):
    tm = min(2048, s)
    tr = min(256, s)
    mm = functools.partial(_matmul, tm=tm)
    on_mesh = late_shards is not None
    wts = dict(wts)
    w_ret, w_gate, w_dil = _split_w_in(wts["w_in"])
    blocks = lambda g: g.reshape(N_DEV, g.shape[0] // N_DEV, g.shape[1])

    u = _prenorm(xs, vec["g_pre_mix"], s)
    inproj_ret = functools.partial(mm, u[0], w_ret, mode="nt", m=s, n=6144, k=1024, tn=1024, tk=1024, out_dtype=BF16, name="inproj_ret")
    if on_mesh:
        names = list(late_shards)
        proj_ret, gathered = inproj_ret(carry=_Exchange([late_shards[n] for n in names], [False] * len(names)))
        wts.update({n: g.reshape(N_DEV * g.shape[1], g.shape[2]) for n, g in zip(names, gathered)})
    else:
        proj_ret = inproj_ret()
    proj_gate = mm(u[0], w_gate, mode="nt", m=s, n=2048, k=1024, tn=1024, tk=1024, out_dtype=BF16, name="inproj_gate")
    qkv = [mm(u[g], w_dil[g], mode="nt", m=s, n=1536, k=1024, tn=1536, tk=1024, out_dtype=BF16, name="inproj_dil%d" % g)
           for g in range(3)]

    yr, y_ret, rstate = _ret_fwd(proj_ret, tabs["cos_r"], tabs["sin_r"], s)
    a_br = mm(yr, wts["w_ret_out"], mode="nn", m=s, n=1024, k=2048, tn=1024, tk=1024, out_dtype=BF16, name="ret_out")

    qkr, o_g, l_g = [], [], []
    for g, dil in enumerate(DIL_GROUPS):
        qkr.append(_rope_qk(qkv[g], *tabs["dil"][g], s, "rope_qk%d" % g))
        o, l = _dil_fwd(qkr[g], qkv[g], dil, s, "dil_fwd%d" % g)
        o_g.append(o)
        l_g.append(l)
    ya, lse = _dil_merge(o_g, l_g, s)
    b_br = mm(ya, wts["w_dil_out"], mode="nt", m=s, n=1024, k=512, tn=1024, tk=512, out_dtype=BF16, name="dil_out")

    def gate_mix(a, b, gr, ga, b0, b1):
        return [_sigmoid(gr.astype(F32) + b0) * a.astype(F32) + _sigmoid(ga.astype(F32) + b1) * b.astype(F32)], []

    (mixed,), _ = _rowwise("gate_mix", gate_mix, s, tr, [(a_br, 1024, 0), (b_br, 1024, 0), (proj_gate, 1024, 0), (proj_gate, 1024, 1)],
                           [vec["b0"], vec["b1"]], [(1024, BF16)])
    z = mm(mixed, wts["w_o"], mode="nn", m=s, n=1024, k=1024, tn=1024, tk=1024, out_dtype=F32, name="w_o")

    def post_norm(h, f, g_post, g_pre):
        hn = h + _rms(f) * g_post
        return [hn, _rms(hn) * g_pre], []

    (h1, v2), _ = _rowwise("post_mix", post_norm, s, tr, [(xs, 1024, 0), (z, 1024, 0)], [vec["g_post_mix"], vec["g_pre_mlp"]],
                           [(1024, F32), (1024, BF16)])
    a_up = mm(v2, wts["w_up"], mode="nt", m=s, n=4096, k=1024, tn=1024, tk=1024, out_dtype=BF16, name="mlp_up")
    f_dn = mm(a_up, wts["w_down"], mode="nn", m=s, n=1024, k=4096, tn=1024, tk=1024, out_dtype=F32, name="mlp_down", a_fn=_relu_sq)
    (h2, t_ple), _ = _rowwise("post_mlp", post_norm, s, tr, [(h1, 1024, 0), (f_dn, 1024, 0)], [vec["g_post_mlp"], vec["g_pre_ple"]],
                              [(1024, F32), (1024, BF16)])
    gl = mm(t_ple, wts["w_ple_gate"], mode="nn", m=s, n=1024, k=1024, tn=1024, tk=1024, out_dtype=F32, name="ple_gate")
    e_ple = mm(pb, wts["w_ple_in"], mode="nt", m=s, n=1024, k=256, tn=1024, tk=256, out_dtype=F32, name="ple_in")

    def ple_loss(h, glv, e, tg, b, g):
        gate = _sigmoid(glv + b)
        ge = gate * e
        diff = h + _rms(ge) * g - tg
        dy = diff * (1.0 / D_MODEL)
        d_ge, dg = _rms_bwd(ge, g, dy)
        d_gl = d_ge * e * gate * (1.0 - gate)
        loss = jnp.zeros((1, D_MODEL), F32) + 0.5 * jnp.sum(diff * diff) * (1.0 / D_MODEL)
        return [dy, d_gl, d_ge * gate], [_colsum(dg), _colsum(d_gl), loss]

    (dy, d_gl, d_e), (dg_post_ple, db_ple, loss) = _rowwise(
        "ple_loss", ple_loss, s, tr, [(h2, 1024, 0), (gl, 1024, 0), (e_ple, 1024, 0), (tgt, 1024, 0)],
        [vec["b_ple"], vec["g_post_ple"]], [(1024, F32), (1024, BF16), (1024, BF16)], [1024, 1024, 1024])

    ts = min(1024, s)
    wg = functools.partial(_matmul, mode="tn", k=s, tk=ts, out_dtype=BF16)
    grads = {}
    grads["w_ple_in"] = wg(d_e, pb, m=1024, n=256, tm=1024, tn=256, name="g_ple_in")
    grads["w_ple_gate"] = wg(t_ple, d_gl, m=1024, n=1024, tm=1024, tn=1024, name="g_ple_gate")
    d_t = mm(d_gl, wts["w_ple_gate"], mode="nt", m=s, n=1024, k=1024, tn=1024, tk=1024, out_dtype=F32, name="d_t")

    def bwd_ple_mlp(h, dt, dyv, f, g_pre, g_post):
        dx, dg1 = _rms_bwd(h, g_pre, dt)
        dh = dyv + dx
        df, dg2 = _rms_bwd(f, g_post, dh)
        return [dh, df], [_colsum(dg1), _colsum(dg2)]

    (d_h2, d_f), (dg_pre_ple, dg_post_mlp) = _rowwise(
        "bwd_ple_mlp", bwd_ple_mlp, s, tr, [(h2, 1024, 0), (d_t, 1024, 0), (dy, 1024, 0), (f_dn, 1024, 0)],
        [vec["g_pre_ple"], vec["g_post_mlp"]], [(1024, F32), (1024, BF16)], [1024, 1024])
    d_a = mm(d_f, wts["w_down"], mode="nt", m=s, n=4096, k=1024, tn=1024, tk=1024, out_dtype=BF16, name="d_a",
             epi=a_up, epi_fn=lambda acc, av: acc * (2.0 * jnp.maximum(av.astype(F32), 0.0)))
    grads["w_down"] = wg(a_up, d_f, m=4096, n=1024, tm=2048, tn=1024, name="g_down", a_fn=_relu_sq)
    grads["w_up"] = wg(d_a, v2, m=4096, n=1024, tm=2048, tn=1024, name="g_up")
    d_v2 = mm(d_a, wts["w_up"], mode="nn", m=s, n=1024, k=4096, tn=1024, tk=1024, out_dtype=F32, name="d_v2")

    (d_h1, d_z), (dg_pre_mlp, dg_post_mix) = _rowwise(
        "bwd_mlp_mix", bwd_ple_mlp, s, tr, [(h1, 1024, 0), (d_v2, 1024, 0), (d_h2, 1024, 0), (z, 1024, 0)],
        [vec["g_pre_mlp"], vec["g_post_mix"]], [(1024, F32), (1024, BF16)], [1024, 1024])
    d_mixed = mm(d_z, wts["w_o"], mode="nt", m=s, n=1024, k=1024, tn=1024, tk=1024, out_dtype=F32, name="d_mixed")
    grads["w_o"] = wg(mixed, d_z, m=1024, n=1024, tm=1024, tn=1024, name="g_o")

    def bwd_gate(dm, a, b, gr, ga, b0, b1):
        sa, sb = _sigmoid(gr.astype(F32) + b0), _sigmoid(ga.astype(F32) + b1)
        dgr = dm * a.astype(F32) * sa * (1.0 - sa)
        dga = dm * b.astype(F32) * sb * (1.0 - sb)
        return [dm * sa, dm * sb, jnp.concatenate([dgr, dga], axis=1)], [_colsum(dgr), _colsum(dga)]

    (d_abr, d_bbr, dproj_gate), (db0, db1) = _rowwise(
        "bwd_gate", bwd_gate, s, tr, [(d_mixed, 1024, 0), (a_br, 1024, 0), (b_br, 1024, 0), (proj_gate, 1024, 0), (proj_gate, 1024, 1)],
        [vec["b0"], vec["b1"]], [(1024, BF16), (1024, BF16), (2048, BF16)], [1024, 1024])
    grads["w_ret_out"] = wg(yr, d_abr, m=2048, n=1024, tm=2048, tn=1024, name="g_ret_out")
    d_yr = mm(d_abr, wts["w_ret_out"], mode="nt", m=s, n=2048, k=1024, tn=1024, tk=1024, out_dtype=BF16, name="d_yr")
    grads["w_dil_out"] = wg(d_bbr, ya, m=1024, n=512, tm=1024, tn=512, name="g_dil_out")
    d_ya = mm(d_bbr, wts["w_dil_out"], mode="nn", m=s, n=512, k=1024, tn=512, tk=1024, out_dtype=F32, name="d_ya")

    slots = {}
    if on_mesh:
        names = list(grads)
        dproj_ret, got = _ret_bwd(proj_ret, tabs["cos_r"], tabs["sin_r"], y_ret, d_yr, rstate, s,
                                  carry=_Exchange([blocks(grads[n]) for n in names], [True] * len(names)))
        slots.update(zip(names, got))
    else:
        dproj_ret = _ret_bwd(proj_ret, tabs["cos_r"], tabs["sin_r"], y_ret, d_yr, rstate, s)
    upstream = _dil_bwd_prep(d_ya, ya, lse, s)
    dqkv = [_dil_bwd(qkr[g], qkv[g], *upstream[g], *tabs["dil"][g], dil, s, "dil_bwd%d" % g)
            for g, dil in enumerate(DIL_GROUPS)]

    g_ret = wg(dproj_ret, u[0], m=6144, n=1024, tm=2048, tn=1024, name="g_in_ret")
    g_gate = wg(dproj_gate, u[0], m=2048, n=1024, tm=2048, tn=1024, name="g_in_gate")
    g_dil = [wg(dqkv[g], u[g], m=1536, n=1024, tm=1536, tn=1024, name="g_in_dil%d" % g) for g in range(3)]
    grads["w_in"] = _join_w_in(g_ret, g_gate, g_dil)

    du_ret = functools.partial(mm, dproj_ret, w_ret, mode="nn", m=s, n=1024, k=6144, tn=1024, tk=1024, out_dtype=F32, name="du_ret")
    if on_mesh:
        du_ret, (slots["w_in"],) = du_ret(carry=_Exchange([blocks(grads["w_in"])], [True]))
    else:
        du_ret = du_ret()
    du_gate = mm(dproj_gate, w_gate, mode="nn", m=s, n=1024, k=2048, tn=1024, tk=1024, out_dtype=F32, name="du_gate")
    du_dil = [mm(dqkv[g], w_dil[g], mode="nn", m=s, n=1024, k=1536, tn=1024, tk=1536, out_dtype=F32, name="du_dil%d" % g)
              for g in range(3)]

    grad_x, dg_pre_mix = _grad_x(xs, d_h1, (du_ret, du_gate, du_dil[0]), du_dil[1], du_dil[2], vec["g_pre_mix"], s)

    zero = jnp.zeros((1, D_MODEL), F32)
    packet = jnp.concatenate([dg_pre_mix, dg_post_mix, dg_pre_mlp, dg_post_mlp, dg_pre_ple, db_ple, dg_post_ple, loss,
                              db0, db1] + [zero] * 6, axis=0)
    return grad_x, (slots if on_mesh else grads), packet


def _mesh_pos():
    return lax.axis_index("x"), lax.axis_index("y"), lax.axis_index("c")


def _all_gather(shards):
    nw = len(shards)

    def body(*refs):
        ins, outs = refs[:nw], refs[nw:2 * nw]
        send_sems, recv_sems, local_sems = refs[2 * nw:]
        x, y, c = _mesh_pos()
        me, sibling = (x, y, c), (x, y, 1 - c)
        chips = [(1 - x, y), (x, 1 - y), (1 - x, 1 - y)]

        def region(w, dev):
            return outs[w].at[4 * dev[0] + 2 * dev[1] + dev[2]]

        def copy(w, kk, block, to, src=None):
            return pltpu.make_async_remote_copy(
                src_ref=region(w, block) if src is None else src, dst_ref=region(w, block),
                send_sem=send_sems.at[w * 7 + kk], recv_sem=recv_sems.at[w * 7 + kk], device_id=to, device_id_type=MESH)

        mine = [pltpu.make_async_copy(ins[w], region(w, me), local_sems.at[w]) for w in range(nw)]
        for cp in mine:
            cp.start()
        first = []
        for w in range(nw):
            first.append(copy(w, 0, me, sibling, src=ins[w]))
            first += [copy(w, 1 + j, me, (*chip, c), src=ins[w]) for j, chip in enumerate(chips)]
        for cp in first:
            cp.start()
        passed = []
        for j, chip in enumerate(chips):
            for w in range(nw):
                copy(w, 1 + j, (*chip, c), me).wait_recv()
                cp = copy(w, 4 + j, (*chip, c), sibling)
                cp.start()
                passed.append(cp)
        for w in range(nw):
            copy(w, 0, sibling, me).wait_recv()
            for j, chip in enumerate(chips):
                copy(w, 4 + j, (*chip, 1 - c), me).wait_recv()
        for cp in first + passed:
            cp.wait_send()
        for cp in mine:
            cp.wait()

    hbm = pl.BlockSpec(memory_space=pl.ANY)
    return pl.pallas_call(
        body, name="gather_weights",
        in_specs=[hbm] * nw, out_specs=[hbm] * nw,
        out_shape=[jax.ShapeDtypeStruct((N_DEV,) + sh.shape, sh.dtype) for sh in shards],
        scratch_shapes=[pltpu.SemaphoreType.DMA((nw * 7,)), pltpu.SemaphoreType.DMA((nw * 7,)), pltpu.SemaphoreType.DMA((nw,))],
    )(*shards)


class _Exchange:
    def __init__(self, arrays, scatter):
        self.arrays, self.scatter, self.n = list(arrays), list(scatter), len(arrays)
        self.out_shape = [jax.ShapeDtypeStruct(a.shape if sc else (N_DEV,) + a.shape, a.dtype)
                          for a, sc in zip(self.arrays, self.scatter)]
        self.scratch = [pltpu.SemaphoreType.DMA((self.n * 7,)), pltpu.SemaphoreType.DMA((self.n * 7,)),
                        pltpu.SemaphoreType.DMA((self.n,))]
        self.specs = [pl.BlockSpec(memory_space=pl.ANY)] * self.n

    def _copies(self, srcs, dsts, sems):
        send_sems, recv_sems, local_sems = sems
        x, y, c = _mesh_pos()
        my = 4 * x + 2 * y + c
        src_of = lambda w, idx: srcs[w].at[idx] if self.scatter[w] else srcs[w]
        local = [pltpu.make_async_copy(src_of(w, my), dsts[w].at[my], local_sems.at[w]) for w in range(self.n)]
        sends, recvs = [], []
        for w in range(self.n):
            for r in range(1, N_DEV):
                px = 1 - x if r & 4 else x
                py = 1 - y if r & 2 else y
                pc = 1 - c if r & 1 else c
                pidx = 4 * px + 2 * py + pc
                kw = dict(send_sem=send_sems.at[w * 7 + r - 1], recv_sem=recv_sems.at[w * 7 + r - 1],
                          device_id=(px, py, pc), device_id_type=MESH)
                sends.append(pltpu.make_async_remote_copy(src_ref=src_of(w, pidx), dst_ref=dsts[w].at[my], **kw))
                recvs.append(pltpu.make_async_remote_copy(src_ref=src_of(w, pidx), dst_ref=dsts[w].at[pidx], **kw))
        return local, sends, recvs

    def start(self, srcs, dsts, sems):
        local, sends, _ = self._copies(srcs, dsts, sems)
        for cp in local + sends:
            cp.start()

    def wait(self, srcs, dsts, sems):
        local, sends, recvs = self._copies(srcs, dsts, sems)
        for cp in recvs:
            cp.wait_recv()
        for cp in sends:
            cp.wait_send()
        for cp in local:
            cp.wait()

    def split(self, refs, n_in, n_out):
        srcs = refs[n_in:n_in + self.n]
        dsts = refs[n_in + self.n + n_out:n_in + 2 * self.n + n_out]
        return srcs, dsts, refs[len(refs) - 3:]


def _run_exchange(ex, name):
    def body(*refs):
        parts = ex.split(refs, 0, 0)
        ex.start(*parts)
        ex.wait(*parts)

    return pl.pallas_call(body, name=name, in_specs=ex.specs, out_specs=ex.specs, out_shape=ex.out_shape,
                          scratch_shapes=ex.scratch)(*ex.arrays)


def _pick_rows(r, c, target_bytes):
    t = r
    while (t // 2) % 16 == 0 and t // 2 >= 16 and t * c * 4 > target_bytes:
        t //= 2
    return t


def _sum_slots(slots, name):
    ns, r, c = slots.shape
    tr = _pick_rows(r, c, 256 * 1024)

    def body(s_ref, o_ref):
        acc = s_ref[0].astype(F32)
        for kk in range(1, ns):
            acc = acc + s_ref[kk].astype(F32)
        o_ref[...] = acc

    return pl.pallas_call(
        body, name=name, grid=(r // tr,),
        in_specs=[pl.BlockSpec((ns, tr, c), lambda i: (0, i, 0))], out_specs=pl.BlockSpec((tr, c), lambda i: (i, 0)),
        out_shape=jax.ShapeDtypeStruct((r, c), F32), compiler_params=_cparams(("parallel",)),
    )(slots)


def _adamw(slots, w, m, v, name):
    ns, r, c = slots.shape
    tr = _pick_rows(r, c, 256 * 1024)

    def body(s_ref, w_ref, m_ref, v_ref, g_out, d_out, m_out, v_out):
        g = s_ref[0].astype(F32)
        for kk in range(1, ns):
            g = g + s_ref[kk].astype(F32)
        mn = ADAM_B1 * m_ref[...] + (1.0 - ADAM_B1) * g
        vn = ADAM_B2 * v_ref[...] + (1.0 - ADAM_B2) * (g * g)
        m_hat = mn / (1.0 - ADAM_B1 ** ADAM_STEP)
        v_hat = vn / (1.0 - ADAM_B2 ** ADAM_STEP)
        g_out[...] = g
        d_out[...] = -ADAM_LR * (m_hat / (jnp.sqrt(v_hat) + ADAM_EPS) + ADAM_WD * w_ref[...])
        m_out[...] = mn
        v_out[...] = vn

    blk = pl.BlockSpec((tr, c), lambda i: (i, 0))
    return pl.pallas_call(
        body, name=name, grid=(r // tr,),
        in_specs=[pl.BlockSpec((ns, tr, c), lambda i: (0, i, 0)), blk, blk, blk], out_specs=[blk] * 4,
        out_shape=[jax.ShapeDtypeStruct((r, c), F32)] * 4, compiler_params=_cparams(("parallel",)),
    )(slots, w, m, v)


def _rotary_tables(pos, s):
    posf = pos.astype(F32)
    inv_freq = 1.0 / (10000.0 ** jnp.linspace(0.0, 1.0, RET_QK // 2, dtype=F32))
    ang = posf[:, None] * inv_freq
    tabs = {"cos_r": jnp.cos(ang), "sin_r": jnp.sin(ang), "dil": []}
    freqs = 500000.0 ** (-jnp.arange(0, 16, 2, dtype=F32) / 16)
    spread = np.zeros((16, 384), np.float32)
    bias = np.zeros((1, 384), np.float32)
    for head in range(2):
        for i in range(8):
            spread[i, 64 * head + i] = spread[i, 64 * head + 8 + i] = 1.0
            spread[8 + i, 128 + 64 * head + i] = -1.0
            spread[8 + i, 256 + 64 * head + 8 + i] = 1.0
        bias[0, 64 * head + 16:64 * head + 64] = 1.0

    def expand(t, e, b):
        hi = t.astype(BF16)
        lo = (t - hi.astype(F32)).astype(BF16)
        out = _dot(hi, e, NN) + _dot(lo, e, NN) + b
        return [out[:, 0:128], out[:, 128:256], out[:, 256:384]], []

    for g, dil in enumerate(DIL_GROUPS):
        ang = posf.reshape(s // dil, dil).T.reshape(s, 1) * freqs
        cs = jnp.concatenate([jnp.cos(ang), jnp.sin(ang)], axis=1)
        t3, _ = _rowwise("rot_tables%d" % g, expand, s, min(1024, s), [(cs, 16, 0)],
                         [jnp.asarray(spread, BF16), jnp.asarray(bias)], [(128, F32)] * 3)
        tabs["dil"].append(tuple(t3))
    return tabs


_TRANSPOSED = ("w_in", "w_dil_out", "w_up", "w_ple_in")
_MATS = ("w_in", "w_ret_out", "w_dil_out", "w_o", "w_up", "w_down", "w_ple_gate", "w_ple_in")
_VECS = ("g_pre_mix", "g_post_mix", "g_pre_mlp", "g_post_mlp", "g_pre_ple", "b_ple_gate", "g_post_ple")
_ORDER = ("w_in", "b_gate", "w_ret_out", "w_dil_out", "w_o", "g_pre_mix", "g_post_mix", "g_pre_mlp", "g_post_mlp", "w_up",
          "w_down", "g_pre_ple", "w_ple_gate", "b_ple_gate", "w_ple_in", "g_post_ple")


def kernel(x, p, positions, w_in, b_gate, w_ret_out, w_dil_out, w_o, g_pre_mix, g_post_mix, g_pre_mlp, g_post_mlp, w_up, w_down, g_pre_ple, w_ple_gate, b_ple_gate, w_ple_in, g_post_ple, loss_target, m_w_in, m_b_gate, m_w_ret_out, m_w_dil_out, m_w_o, m_g_pre_mix, m_g_post_mix, m_g_pre_mlp, m_g_post_mlp, m_w_up, m_w_down, m_g_pre_ple, m_w_ple_gate, m_b_ple_gate, m_w_ple_in, m_g_post_ple, v_w_in, v_b_gate, v_w_ret_out, v_w_dil_out, v_w_o, v_g_pre_mix, v_g_post_mix, v_g_pre_mlp, v_g_post_mlp, v_w_up, v_w_down, v_g_pre_ple, v_w_ple_gate, v_b_ple_gate, v_w_ple_in, v_g_post_ple):
    s = x.shape[1]
    wd = dict(w_in=w_in, b_gate=b_gate, w_ret_out=w_ret_out, w_dil_out=w_dil_out, w_o=w_o, g_pre_mix=g_pre_mix,
              g_post_mix=g_post_mix, g_pre_mlp=g_pre_mlp, g_post_mlp=g_post_mlp, w_up=w_up, w_down=w_down,
              g_pre_ple=g_pre_ple, w_ple_gate=w_ple_gate, b_ple_gate=b_ple_gate, w_ple_in=w_ple_in, g_post_ple=g_post_ple)
    md = dict(w_in=m_w_in, b_gate=m_b_gate, w_ret_out=m_w_ret_out, w_dil_out=m_w_dil_out, w_o=m_w_o, g_pre_mix=m_g_pre_mix,
              g_post_mix=m_g_post_mix, g_pre_mlp=m_g_pre_mlp, g_post_mlp=m_g_post_mlp, w_up=m_w_up, w_down=m_w_down,
              g_pre_ple=m_g_pre_ple, w_ple_gate=m_w_ple_gate, b_ple_gate=m_b_ple_gate, w_ple_in=m_w_ple_in, g_post_ple=m_g_post_ple)
    vd = dict(w_in=v_w_in, b_gate=v_b_gate, w_ret_out=v_w_ret_out, w_dil_out=v_w_dil_out, w_o=v_w_o, g_pre_mix=v_g_pre_mix,
              g_post_mix=v_g_post_mix, g_pre_mlp=v_g_pre_mlp, g_post_mlp=v_g_post_mlp, w_up=v_w_up, w_down=v_w_down,
              g_pre_ple=v_g_pre_ple, w_ple_gate=v_w_ple_gate, b_ple_gate=v_b_ple_gate, w_ple_in=v_w_ple_in, g_post_ple=v_g_post_ple)

    shards = {n: (wd[n][0].T if n in _TRANSPOSED else wd[n][0]).astype(BF16) for n in _MATS}
    w_in_all, bg_all = _all_gather([shards.pop("w_in"), b_gate[0]])
    wts = {"w_in": w_in_all.reshape(N_DEV * w_in_all.shape[1], D_MODEL)}
    bg = bg_all.transpose(1, 0, 2).reshape(2, D_MODEL)
    vec = {n: wd[n] for n in _VECS}
    vec.update(b0=bg[0:1], b1=bg[1:2], b_ple=b_ple_gate)

    tabs = _rotary_tables(positions[0], s)
    grad_x, slots, packet = _local_step(x[0], p[0, 0].astype(BF16), loss_target[0], tabs, wts, vec, s, late_shards=shards)

    (packets,) = _run_exchange(_Exchange([packet], [False]), "exchange_vectors")
    out = {}
    for n in _MATS:
        sl = slots[n]
        if n in _TRANSPOSED:
            sl = _sum_slots(sl, "sum_" + n).T[None]
        out[n] = _adamw(sl, wd[n][0], md[n][0], vd[n][0], "adamw_" + n)
    zero_rows = jnp.zeros((16 - len(_VECS), D_MODEL), F32)
    pack = lambda d: jnp.concatenate([d[n] for n in _VECS] + [zero_rows], axis=0)
    small = _adamw(packets, pack(wd), pack(md), pack(vd), "adamw_vectors")
    for i, n in enumerate(_VECS):
        out[n] = tuple(t[i:i + 1] for t in small)
    my = 4 * lax.axis_index("x") + 2 * lax.axis_index("y") + lax.axis_index("c")
    g_bias = lax.dynamic_slice(small[0], (8, my * 128), (2, 128))
    out["b_gate"] = _adamw(g_bias[None], b_gate[0], m_b_gate[0], v_b_gate[0], "adamw_b_gate")
    loss = small[0][7, 0]

    res = [loss, grad_x[None]]
    for kk in range(4):
        res += [out[n][kk][None] if out[n][kk].ndim == 2 and wd[n].ndim == 3 else out[n][kk] for n in _ORDER]
    return tuple(res)
```

```python
import functools
import math

import numpy as np
import jax
import jax.numpy as jnp
from jax import lax
from jax.experimental import pallas as pl
from jax.experimental.pallas import tpu as pltpu

F32, BF16 = jnp.float32, jnp.bfloat16
D_MODEL = 1024
EPS = 1e-6
N_DEV = 8
RET_HEADS, RET_QK, RET_V, RET_CHUNK = 4, 256, 512, 128
DIL_GROUPS = (1, 4, 16)
DIL_W = 512
QB = 128
NEG = -1e30
ADAM_LR, ADAM_B1, ADAM_B2, ADAM_EPS, ADAM_WD, ADAM_STEP = 0.001, 0.9, 0.999, 1e-08, 0.01, 10
VMEM_LIMIT_BYTES = 56 * 1024 * 1024
MESH = pl.DeviceIdType.MESH

NN = ((1,), (0,))
NT = ((1,), (1,))
TN = ((0,), (0,))


def _dot(a, b, dn):
    return lax.dot_general(a, b, (dn, ((), ())), preferred_element_type=F32)


def _cparams(sem):
    return pltpu.CompilerParams(dimension_semantics=sem, vmem_limit_bytes=VMEM_LIMIT_BYTES)


def _rms(x):
    return x * lax.rsqrt(jnp.mean(x * x, axis=-1, keepdims=True) + EPS)


def _rms_bwd(x, g, dy):
    r = lax.rsqrt(jnp.mean(x * x, axis=-1, keepdims=True) + EPS)
    xh = x * r
    t = dy * g
    dx = r * (t - xh * jnp.mean(t * xh, axis=-1, keepdims=True))
    return dx, dy * xh


def _colsum(v):
    return jnp.sum(v, axis=0, keepdims=True)


def _sigmoid(v):
    return 1.0 / (1.0 + jnp.exp(-v))


def _pallas(compute, *, name, grid, in_specs, out_specs, out_shape, scratch, semantics, args, carry=None):
    n_in, n_out = len(in_specs), len(out_specs)
    if carry is None:
        res = pl.pallas_call(compute, name=name, grid=grid, in_specs=in_specs, out_specs=out_specs, out_shape=out_shape,
                             scratch_shapes=scratch, compiler_params=_cparams(semantics))(*args)
        return res, []
    n_steps = math.prod(grid)

    def body(*refs):
        step = 0
        for axis, size in enumerate(grid):
            step = step * size + pl.program_id(axis)
        parts = carry.split(refs, n_in, n_out)
        pl.when(step == 0)(lambda: carry.start(*parts))
        compute(*refs[:n_in], *refs[n_in + carry.n:n_in + carry.n + n_out], *refs[n_in + 2 * carry.n + n_out:len(refs) - 3])
        pl.when(step == n_steps - 1)(lambda: carry.wait(*parts))

    res = pl.pallas_call(
        body, name=name, grid=grid, in_specs=list(in_specs) + carry.specs, out_specs=list(out_specs) + carry.specs,
        out_shape=list(out_shape) + carry.out_shape, scratch_shapes=list(scratch) + carry.scratch,
        compiler_params=_cparams(("arbitrary",) * len(grid)))(*args, *carry.arrays)
    return res[:n_out], res[n_out:]


def _matmul(a, b, *, mode, m, n, k, tm, tn, tk, out_dtype, name, a_fn=None, epi=None, epi_fn=None, carry=None):
    nk = k // tk
    grid = (m // tm, n // tn, nk)
    if mode == "nn":
        a_blk, a_im, b_blk, b_im, dn = (tm, tk), (lambda i, j, kk: (i, kk)), (tk, tn), (lambda i, j, kk: (kk, j)), NN
    elif mode == "nt":
        a_blk, a_im, b_blk, b_im, dn = (tm, tk), (lambda i, j, kk: (i, kk)), (tn, tk), (lambda i, j, kk: (j, kk)), NT
    else:
        a_blk, a_im, b_blk, b_im, dn = (tk, tm), (lambda i, j, kk: (kk, i)), (tk, tn), (lambda i, j, kk: (kk, j)), TN
    o_im = lambda i, j, kk: (i, j)
    n_in = 2 + (epi is not None)

    def body(*refs):
        a_ref, b_ref = refs[0], refs[1]
        e_ref = refs[2] if epi is not None else None
        o_ref = refs[n_in]
        acc_ref = refs[n_in + 1] if nk > 1 else None

        def finish(acc):
            if e_ref is not None:
                acc = epi_fn(acc, e_ref[...])
            o_ref[...] = acc.astype(o_ref.dtype)

        av = a_ref[...]
        if a_fn is not None:
            av = a_fn(av)
        part = _dot(av, b_ref[...], dn)
        if nk == 1:
            finish(part)
        else:
            kk = pl.program_id(2)

            @pl.when(kk == 0)
            def _():
                acc_ref[...] = part

            @pl.when(kk > 0)
            def _():
                acc_ref[...] += part

            @pl.when(kk == nk - 1)
            def _():
                finish(acc_ref[...])

    in_specs = [pl.BlockSpec(a_blk, a_im), pl.BlockSpec(b_blk, b_im)]
    args = [a, b]
    if epi is not None:
        in_specs.append(pl.BlockSpec((tm, tn), o_im))
        args.append(epi)
    (out,), got = _pallas(
        body, name=name, grid=grid, in_specs=in_specs, out_specs=[pl.BlockSpec((tm, tn), o_im)],
        out_shape=[jax.ShapeDtypeStruct((m, n), out_dtype)], scratch=[pltpu.VMEM((tm, tn), F32)] if nk > 1 else [],
        semantics=("parallel", "parallel", "arbitrary"), args=args, carry=carry)
    return out if carry is None else (out, got)


def _relu_sq(v):
    r = jnp.maximum(v.astype(F32), 0.0)
    return (r * r).astype(BF16)


def _rowwise(name, fn, s, tr, rows, vecs, outs, accs=()):
    n_r, n_v, n_o, n_a = len(rows), len(vecs), len(outs), len(accs)

    def body(*refs):
        vals = [refs[i][...].astype(F32) for i in range(n_r)] + [refs[n_r + i][...] for i in range(n_v)]
        o_refs = refs[n_r + n_v:n_r + n_v + n_o]
        a_refs = refs[n_r + n_v + n_o:]
        o_vals, a_vals = fn(*vals)
        for ref, val in zip(o_refs, o_vals):
            ref[...] = val.astype(ref.dtype)
        if n_a:
            @pl.when(pl.program_id(0) == 0)
            def _():
                for ref in a_refs:
                    ref[...] = jnp.zeros_like(ref)

            for ref, val in zip(a_refs, a_vals):
                ref[...] += val

    in_specs = [pl.BlockSpec((tr, w), functools.partial(lambda i, cb: (i, cb), cb=cb)) for _, w, cb in rows]
    in_specs += [pl.BlockSpec(v.shape, lambda i: (0, 0)) for v in vecs]
    out_specs = [pl.BlockSpec((tr, w), lambda i: (i, 0)) for w, _ in outs]
    out_specs += [pl.BlockSpec((1, w), lambda i: (0, 0)) for w in accs]
    out_shape = [jax.ShapeDtypeStruct((s, w), dt) for w, dt in outs]
    out_shape += [jax.ShapeDtypeStruct((1, w), F32) for w in accs]
    res = pl.pallas_call(
        body, name=name, grid=(s // tr,), in_specs=in_specs, out_specs=out_specs, out_shape=out_shape,
        compiler_params=_cparams(("arbitrary",)),
    )(*[r[0] for r in rows], *vecs)
    return res[:n_o], res[n_o:]


_ROW_TILE = 256
_STREAM_SPECS = [pl.BlockSpec((dil, _ROW_TILE // dil, D_MODEL), lambda i: (0, i, 0)) for dil in DIL_GROUPS[1:]]
_NAT_SPEC = pl.BlockSpec((_ROW_TILE, D_MODEL), lambda i: (i, 0))
_VEC_SPEC = pl.BlockSpec((1, D_MODEL), lambda i: (0, 0))
_COL_BLOCKS = pltpu.VMEM((D_MODEL // 128, _ROW_TILE, 128), F32)


def _prenorm(xs, g, s):
    tr = _ROW_TILE

    def body(x_ref, g_ref, u_ref, u4_ref, u16_ref, buf):
        xn = _rms(x_ref[...]) * g_ref[...]
        u_ref[...] = xn.astype(BF16)
        for cb in range(8):
            buf[cb] = xn[:, cb * 128:(cb + 1) * 128]
        for dil, out in ((4, u4_ref), (16, u16_ref)):
            for c in range(dil):
                rows = pl.ds(c, tr // dil, stride=dil)
                out[c] = jnp.concatenate([buf.at[cb][rows, :] for cb in range(8)], axis=1).astype(BF16)

    res = pl.pallas_call(
        body, name="prenorm", grid=(s // tr,), in_specs=[_NAT_SPEC, _VEC_SPEC], out_specs=[_NAT_SPEC] + _STREAM_SPECS,
        out_shape=[jax.ShapeDtypeStruct((s, D_MODEL), BF16)]
        + [jax.ShapeDtypeStruct((dil, s // dil, D_MODEL), BF16) for dil in DIL_GROUPS[1:]],
        scratch_shapes=[_COL_BLOCKS], compiler_params=_cparams(("parallel",)),
    )(xs, g)
    return [r.reshape(s, D_MODEL) for r in res]


def _grad_x(xs, d_h1, du_nat, du4, du16, g, s):
    tr = _ROW_TILE

    def body(x_ref, dh_ref, a_ref, b_ref, c_ref, u4_ref, u16_ref, g_ref, dx_ref, dg_ref, buf):
        du = a_ref[...].astype(F32) + b_ref[...].astype(F32) + c_ref[...].astype(F32)
        for dil, src in ((4, u4_ref), (16, u16_ref)):
            for c in range(dil):
                part = src[c].astype(F32)
                for cb in range(8):
                    buf.at[cb][pl.ds(c, tr // dil, stride=dil), :] = part[:, cb * 128:(cb + 1) * 128]
            du = du + jnp.concatenate([buf[cb] for cb in range(8)], axis=1)
        dx, dgr = _rms_bwd(x_ref[...], g_ref[...], du)
        dx_ref[...] = dh_ref[...] + dx

        @pl.when(pl.program_id(0) == 0)
        def _():
            dg_ref[...] = jnp.zeros_like(dg_ref)

        dg_ref[...] += _colsum(dgr)

    return pl.pallas_call(
        body, name="grad_x", grid=(s // tr,), in_specs=[_NAT_SPEC] * 5 + _STREAM_SPECS + [_VEC_SPEC],
        out_specs=[_NAT_SPEC, _VEC_SPEC],
        out_shape=[jax.ShapeDtypeStruct((s, D_MODEL), F32), jax.ShapeDtypeStruct((1, D_MODEL), F32)],
        scratch_shapes=[_COL_BLOCKS], compiler_params=_cparams(("arbitrary",)),
    )(xs, d_h1, *du_nat, du4.reshape(4, s // 4, D_MODEL), du16.reshape(16, s // 16, D_MODEL), g)


def _ret_tables():
    h = np.arange(RET_HEADS, dtype=np.float32)
    lg = np.log1p(-(np.float32(2.0) ** (-5.0 - h))).astype(np.float32)
    idx = np.arange(RET_CHUNK, dtype=np.float32)
    diff = idx[:, None] - idx[None, :]
    dm = np.where(diff[None] >= 0, np.exp(np.maximum(diff, 0.0)[None] * lg[:, None, None]), 0.0)
    qd = np.exp((idx + 1.0)[None, :, None] * lg[:, None, None])
    kd = np.exp((RET_CHUNK - 1.0 - idx)[None, :, None] * lg[:, None, None])
    cd = np.exp(RET_CHUNK * lg)[:, None, None]
    return [jnp.asarray(t, F32) for t in (dm, qd, kd, cd)]


def _rope_half(v, cos, sin):
    v1, v2 = v[:, :128], v[:, 128:]
    return jnp.concatenate([v1 * cos - v2 * sin, v2 * cos + v1 * sin], axis=1)


def _unrope_half(d, cos, sin):
    d1, d2 = d[:, :128], d[:, 128:]
    return jnp.concatenate([d1 * cos + d2 * sin, d2 * cos - d1 * sin], axis=1)


def _ret_specs(rb, rev_n):
    def rowmap(w_blk):
        return lambda h, n: (rev_n(n), w_blk(h))
    tab = [pl.BlockSpec((1, RET_CHUNK, RET_CHUNK), lambda h, n: (h, 0, 0)),
           pl.BlockSpec((1, RET_CHUNK, 1), lambda h, n: (h, 0, 0)),
           pl.BlockSpec((1, RET_CHUNK, 1), lambda h, n: (h, 0, 0)),
           pl.BlockSpec((1, 1, 1), lambda h, n: (h, 0, 0))]
    proj = pl.BlockSpec((rb, 1536), rowmap(lambda h: h))
    cs = pl.BlockSpec((rb, 128), rowmap(lambda h: 0))
    hv = pl.BlockSpec((rb, RET_V), rowmap(lambda h: h))
    return proj, cs, hv, tab


def _ret_fwd(proj_ret, cos, sin, s, carry=None):
    rb = min(512, s)
    ch = rb // RET_CHUNK
    nb = s // rb
    proj_spec, cs_spec, hv_spec, tab_specs = _ret_specs(rb, lambda n: n)

    def body(p_ref, cos_ref, sin_ref, dm_ref, qd_ref, kd_ref, cd_ref, yr_ref, y_ref, rs_ref, r_acc):
        @pl.when(pl.program_id(1) == 0)
        def _():
            r_acc[...] = jnp.zeros_like(r_acc)

        dm, qd, kd, cd = dm_ref[0], qd_ref[0], kd_ref[0], cd_ref[0]
        for c in range(ch):
            rows = slice(c * RET_CHUNK, (c + 1) * RET_CHUNK)
            cosv, sinv = cos_ref[rows, :], sin_ref[rows, :]
            q = _rope_half(p_ref[rows, 0:256].astype(F32), cosv, sinv)
            kk = _rope_half(p_ref[rows, 256:512].astype(F32), cosv, sinv) * (RET_QK ** -0.5)
            v = p_ref[rows, 512:1024]
            g = p_ref[rows, 1024:1536].astype(F32)
            rb16 = r_acc[...].astype(BF16)
            rs_ref[0, c] = rb16
            sc = _dot(q.astype(BF16), kk.astype(BF16), NT) * dm
            y = _dot(sc.astype(BF16), v, NN) + _dot((q * qd).astype(BF16), rb16, NN)
            r_acc[...] = r_acc[...] * cd + _dot((kk * kd).astype(BF16), v, TN)
            y_ref[rows, :] = y.astype(BF16)
            yr_ref[rows, :] = (_rms(y) * (g * _sigmoid(g))).astype(BF16)

    return _pallas(
        body, name="ret_fwd", grid=(RET_HEADS, nb),
        in_specs=[proj_spec, cs_spec, cs_spec] + tab_specs,
        out_specs=[hv_spec, hv_spec, pl.BlockSpec((1, ch, RET_QK, RET_V), lambda h, n: (h, n, 0, 0))],
        out_shape=[jax.ShapeDtypeStruct((s, RET_HEADS * RET_V), BF16), jax.ShapeDtypeStruct((s, RET_HEADS * RET_V), BF16),
                   jax.ShapeDtypeStruct((RET_HEADS, s // RET_CHUNK, RET_QK, RET_V), BF16)],
        scratch=[pltpu.VMEM((RET_QK, RET_V), F32)], semantics=("parallel", "arbitrary"),
        args=(proj_ret, cos, sin, *_ret_tables()), carry=carry)


def _ret_bwd(proj_ret, cos, sin, y, d_yr, rs, s, carry=None):
    rb = min(512, s)
    ch = rb // RET_CHUNK
    nb = s // rb
    proj_spec, cs_spec, hv_spec, tab_specs = _ret_specs(rb, lambda n: nb - 1 - n)

    def body(p_ref, cos_ref, sin_ref, y_ref, dyr_ref, rs_ref, dm_ref, qd_ref, kd_ref, cd_ref, o_ref, dr_acc):
        @pl.when(pl.program_id(1) == 0)
        def _():
            dr_acc[...] = jnp.zeros_like(dr_acc)

        dm, qd, kd, cd = dm_ref[0], qd_ref[0], kd_ref[0], cd_ref[0]
        for c in reversed(range(ch)):
            rows = slice(c * RET_CHUNK, (c + 1) * RET_CHUNK)
            cosv, sinv = cos_ref[rows, :], sin_ref[rows, :]
            q = _rope_half(p_ref[rows, 0:256].astype(F32), cosv, sinv)
            kk = _rope_half(p_ref[rows, 256:512].astype(F32), cosv, sinv) * (RET_QK ** -0.5)
            v = p_ref[rows, 512:1024]
            g = p_ref[rows, 1024:1536].astype(F32)
            yv = y_ref[rows, :].astype(F32)
            dyr = dyr_ref[rows, :].astype(F32)
            sg = _sigmoid(g)
            r = lax.rsqrt(jnp.mean(yv * yv, axis=-1, keepdims=True) + EPS)
            yn = yv * r
            dg = dyr * yn * (sg * (1.0 + g * (1.0 - sg)))
            dyn = dyr * (g * sg)
            dy = (r * (dyn - yn * jnp.mean(dyn * yn, axis=-1, keepdims=True))).astype(BF16)
            qb, kb = q.astype(BF16), kk.astype(BF16)
            rb16 = rs_ref[0, c]
            drb = dr_acc[...].astype(BF16)
            sd = _dot(qb, kb, NT) * dm
            ds = (_dot(dy, v, NT) * dm).astype(BF16)
            dq = _dot(ds, kb, NN) + qd * _dot(dy, rb16, NT)
            dk = _dot(ds, qb, TN) + kd * _dot(v, drb, NT)
            dv = _dot(sd.astype(BF16), dy, TN) + _dot((kk * kd).astype(BF16), drb, NN)
            dr_acc[...] = dr_acc[...] * cd + _dot((q * qd).astype(BF16), dy, TN)
            o_ref[rows, 0:256] = _unrope_half(dq, cosv, sinv).astype(BF16)
            o_ref[rows, 256:512] = (_unrope_half(dk, cosv, sinv) * (RET_QK ** -0.5)).astype(BF16)
            o_ref[rows, 512:1024] = dv.astype(BF16)
            o_ref[rows, 1024:1536] = dg.astype(BF16)

    in_specs = [proj_spec, cs_spec, cs_spec, hv_spec, hv_spec,
                pl.BlockSpec((1, ch, RET_QK, RET_V), lambda h, n: (h, nb - 1 - n, 0, 0))] + tab_specs
    return _pallas(
        body, name="ret_bwd", grid=(RET_HEADS, nb), in_specs=in_specs, out_specs=[proj_spec],
        out_shape=[jax.ShapeDtypeStruct((s, RET_HEADS * 1536), BF16)], scratch=[pltpu.VMEM((RET_QK, RET_V), F32)],
        semantics=("parallel", "arbitrary"), args=(proj_ret, cos, sin, y, d_yr, rs, *_ret_tables()), carry=carry)


def _rope_qk(qkv, tc, ts1, ts2, s, name):
    def fn(q, k, c, s1, s2):
        outs = []
        for v in (q, k):
            for cc in range(4):
                vv = v[:, cc * 128:(cc + 1) * 128].astype(F32)
                outs.append(vv * c + pltpu.roll(vv, 120, 1) * s1 + pltpu.roll(vv, 8, 1) * s2)
        return [jnp.concatenate(outs, axis=1)], []

    (out,), _ = _rowwise(name, fn, s, min(512, s), [(qkv, 512, 0), (qkv, 512, 1), (tc, 128, 0), (ts1, 128, 0), (ts2, 128, 0)],
                         [], [(1024, BF16)])
    return out


def _pair_masks():
    ri = lax.broadcasted_iota(jnp.int32, (2 * QB, 2 * QB), 0)
    ci = lax.broadcasted_iota(jnp.int32, (2 * QB, 2 * QB), 1)
    e = ci - (ri & (QB - 1))
    lane_lo = lax.broadcasted_iota(jnp.int32, (2 * QB, 128), 1) < 64
    return ci, jnp.logical_and(e >= 0, e <= QB), lane_lo


def _stack_heads(v, lane_lo):
    z = jnp.zeros_like(v)
    return jnp.concatenate([jnp.where(lane_lo, v, z), jnp.where(lane_lo, z, v)], axis=0)


def _dil_fwd(qkr, qkv, dil, s, name):
    length = s // dil
    rb = min(512, length)
    nsub = rb // QB
    nbs = length // rb
    sub_per = rb // QB

    def body(q_ref, k_ref, v_ref, kp_ref, vp_ref, o_ref, l_ref, kf, vf):
        first = (pl.program_id(0) % nbs) == 0
        kf[0:QB, :] = kp_ref[...]
        kf[QB:, :] = k_ref[...]
        vf[0:QB, :] = vp_ref[...]
        vf[QB:, :] = v_ref[...]
        ci, band, lane_lo = _pair_masks()
        lo1 = lane_lo[0:QB]

        def step(i, carry):
            r0 = pl.multiple_of(i * QB, QB)
            mask = jnp.logical_and(band, ci >= jnp.where(jnp.logical_and(first, i == 0), QB, 0))
            for j in range(4):
                lanes = slice(j * 128, (j + 1) * 128)
                q2 = _stack_heads(q_ref[pl.ds(r0, QB), lanes], lo1)
                k2 = kf[pl.ds(r0, 2 * QB), lanes]
                v2 = _stack_heads(vf[pl.ds(r0, 2 * QB), lanes], lane_lo)
                sc = jnp.where(mask, _dot(q2, k2, NT) * 0.125, NEG)
                m = jnp.max(sc, axis=1, keepdims=True)
                p = jnp.exp(sc - m)
                den = jnp.sum(p, axis=1, keepdims=True)
                pb = p.astype(BF16)
                o = _dot(jnp.concatenate([pb[0:QB], pb[QB:]], axis=1), v2, NN)
                inv = 1.0 / den
                lse = m + jnp.log(den)
                o_ref[pl.ds(r0, QB), lanes] = o * jnp.where(lo1, inv[0:QB], inv[QB:])
                l_ref[pl.ds(r0, QB), lanes] = jnp.where(lo1, lse[0:QB], lse[QB:])
            return carry

        lax.fori_loop(0, nsub, step, 0)

    prev = lambda n: jnp.maximum(n * sub_per - 1, 0)
    cur = lambda cb: (lambda n: (n, cb))
    return pl.pallas_call(
        body, name=name, grid=(s // rb,),
        in_specs=[pl.BlockSpec((rb, DIL_W), cur(0)), pl.BlockSpec((rb, DIL_W), cur(1)), pl.BlockSpec((rb, DIL_W), cur(2)),
                  pl.BlockSpec((QB, DIL_W), lambda n: (prev(n), 1)), pl.BlockSpec((QB, DIL_W), lambda n: (prev(n), 2))],
        out_specs=[pl.BlockSpec((rb, DIL_W), cur(0)), pl.BlockSpec((rb, DIL_W), cur(0))],
        out_shape=[jax.ShapeDtypeStruct((s, DIL_W), F32), jax.ShapeDtypeStruct((s, DIL_W), F32)],
        scratch_shapes=[pltpu.VMEM((QB + rb, DIL_W), BF16), pltpu.VMEM((QB + rb, DIL_W), BF16)],
        compiler_params=_cparams(("parallel",)),
    )(qkr, qkr, qkv, qkr, qkv)


def _dil_bwd(qkr, qkv, dya, lse, dlt, tc, ts1, ts2, dil, s, name):
    length = s // dil
    rb = min(512, length)
    nsub = rb // QB
    nbs = length // rb
    last_blk = s // QB - 1

    def body(q_ref, k_ref, v_ref, kp_ref, vp_ref, qn_ref, dy_ref, dyn_ref, l_ref, ln_ref, d_ref, dn_ref,
             c_ref, s1_ref, s2_ref, o_ref, kf, vf, qf, dyf, lf, df, dqa, dka, dva):
        nl = pl.program_id(0) % nbs
        first, last = nl == 0, nl == nbs - 1
        kf[0:QB, :] = kp_ref[...]
        kf[QB:QB + rb, :] = k_ref[...]
        kf[QB + rb:, :] = jnp.zeros((QB, DIL_W), BF16)
        vf[0:QB, :] = vp_ref[...]
        vf[QB:QB + rb, :] = v_ref[...]
        vf[QB + rb:, :] = jnp.zeros((QB, DIL_W), BF16)
        qf[0:rb, :] = q_ref[...]
        qf[rb:, :] = qn_ref[...]
        dyf[0:rb, :] = dy_ref[...]
        dyf[rb:, :] = dyn_ref[...]
        lf[0:rb, :] = l_ref[...]
        lf[rb:, :] = ln_ref[...]
        df[0:rb, :] = d_ref[...]
        df[rb:, :] = dn_ref[...]
        dka[...] = jnp.zeros_like(dka)
        dva[...] = jnp.zeros_like(dva)
        ci, band, lane_lo = _pair_masks()
        lo1 = lane_lo[0:QB]

        def step(qi, carry):
            qr = pl.multiple_of(qi * QB, QB)
            is_next = qi == nsub
            cmin = jnp.where(jnp.logical_and(first, qi == 0), QB, 0)
            cmax = jnp.where(is_next, jnp.where(last, -1, QB - 1), 2 * QB - 1)
            mask = jnp.logical_and(band, jnp.logical_and(ci >= cmin, ci <= cmax))
            for j in range(4):
                lanes = slice(j * 128, (j + 1) * 128)
                q2 = _stack_heads(qf[pl.ds(qr, QB), lanes], lo1)
                do2 = _stack_heads(dyf[pl.ds(qr, QB), lanes], lo1)
                lv = lf[pl.ds(qr, QB), lanes]
                dl = df[pl.ds(qr, QB), lanes]
                lse2 = jnp.concatenate([lv[:, 0:1], lv[:, 64:65]], axis=0)
                dl2 = jnp.concatenate([dl[:, 0:1], dl[:, 64:65]], axis=0)
                k2 = kf[pl.ds(qr, 2 * QB), lanes]
                v2 = vf[pl.ds(qr, 2 * QB), lanes]
                sc = _dot(q2, k2, NT) * 0.125
                p = jnp.where(mask, jnp.exp(jnp.minimum(sc - lse2, 0.0)), 0.0)
                ds = (p * (_dot(do2, v2, NT) - dl2) * 0.125).astype(BF16)
                dq = _dot(jnp.concatenate([ds[0:QB], ds[QB:]], axis=1), _stack_heads(k2, lane_lo), NN)
                dqa[pl.ds(qr, QB), lanes] = dq
                dka[pl.ds(qr, 2 * QB), lanes] += _dot(ds, q2, TN)
                dva[pl.ds(qr, 2 * QB), lanes] += _dot(p.astype(BF16), do2, TN)
            return carry

        lax.fori_loop(0, nsub + 1, step, 0)
        cv, s1v, s2v = c_ref[...], s1_ref[...], s2_ref[...]

        def unrope(d):
            return d * cv + pltpu.roll(d * s1v, 8, 1) + pltpu.roll(d * s2v, 120, 1)

        for cc in range(4):
            lanes = slice(cc * 128, (cc + 1) * 128)
            o_ref[:, cc * 128:(cc + 1) * 128] = unrope(dqa[0:rb, lanes]).astype(BF16)
            o_ref[:, 512 + cc * 128:512 + (cc + 1) * 128] = unrope(dka[QB:QB + rb, lanes]).astype(BF16)
            o_ref[:, 1024 + cc * 128:1024 + (cc + 1) * 128] = dva[QB:QB + rb, lanes].astype(BF16)

    prev = lambda n: jnp.maximum(n * nsub - 1, 0)
    nxt = lambda n: jnp.minimum(n * nsub + nsub, last_blk)
    cur = lambda cb: (lambda n: (n, cb))
    big = lambda cb: pl.BlockSpec((rb, DIL_W), cur(cb))
    small = lambda im: pl.BlockSpec((QB, DIL_W), im)
    tab = pl.BlockSpec((rb, 128), cur(0))
    return pl.pallas_call(
        body, name=name, grid=(s // rb,),
        in_specs=[big(0), big(1), big(2), small(lambda n: (prev(n), 1)), small(lambda n: (prev(n), 2)),
                  small(lambda n: (nxt(n), 0)), big(0), small(lambda n: (nxt(n), 0)), big(0), small(lambda n: (nxt(n), 0)),
                  big(0), small(lambda n: (nxt(n), 0)), tab, tab, tab],
        out_specs=pl.BlockSpec((rb, 3 * DIL_W), cur(0)),
        out_shape=jax.ShapeDtypeStruct((s, 3 * DIL_W), BF16),
        scratch_shapes=[pltpu.VMEM((rb + 2 * QB, DIL_W), BF16), pltpu.VMEM((rb + 2 * QB, DIL_W), BF16),
                        pltpu.VMEM((rb + QB, DIL_W), BF16), pltpu.VMEM((rb + QB, DIL_W), BF16),
                        pltpu.VMEM((rb + QB, DIL_W), F32), pltpu.VMEM((rb + QB, DIL_W), F32),
                        pltpu.VMEM((rb + QB, DIL_W), F32), pltpu.VMEM((rb + 2 * QB, DIL_W), F32),
                        pltpu.VMEM((rb + 2 * QB, DIL_W), F32)],
        compiler_params=_cparams(("parallel",)),
    )(qkr, qkr, qkv, qkr, qkv, qkr, dya, dya, lse, lse, dlt, dlt, tc, ts1, ts2)


def _stream_specs(tr):
    nat = pl.BlockSpec((tr, 128), lambda i, j: (i, j))
    return [nat] + [pl.BlockSpec((dil, tr // dil, 128), lambda i, j: (0, i, j)) for dil in DIL_GROUPS[1:]]


def _dil_merge(o_g, l_g, s):
    tr = min(2048, s)
    nat, sp4, sp16 = _stream_specs(tr)

    def body(o0_ref, l0_ref, o1_ref, l1_ref, o2_ref, l2_ref, ya_ref, lse_ref, o1n, l1n, o2n, l2n):
        for src, dst, dil in ((o1_ref, o1n, 4), (l1_ref, l1n, 4), (o2_ref, o2n, 16), (l2_ref, l2n, 16)):
            for c in range(dil):
                dst[pl.ds(c, tr // dil, stride=dil), :] = src[c]
        l0, l1, l2 = l0_ref[...], l1n[...], l2n[...]
        m = jnp.maximum(jnp.maximum(l0, l1), l2)
        e0, e1, e2 = jnp.exp(l0 - m), jnp.exp(l1 - m), jnp.exp(l2 - m)
        den = e0 + e1 + e2
        ya_ref[...] = ((e0 * o0_ref[...] + e1 * o1n[...] + e2 * o2n[...]) / den).astype(BF16)
        lse_ref[...] = m + jnp.log(den)

    v3 = lambda a, dil: a.reshape(dil, s // dil, DIL_W)
    return pl.pallas_call(
        body, name="dil_merge", grid=(s // tr, 4),
        in_specs=[nat, nat, sp4, sp4, sp16, sp16], out_specs=[nat, nat],
        out_shape=[jax.ShapeDtypeStruct((s, DIL_W), BF16), jax.ShapeDtypeStruct((s, DIL_W), F32)],
        scratch_shapes=[pltpu.VMEM((tr, 128), F32)] * 4,
        compiler_params=_cparams(("parallel", "parallel")),
    )(o_g[0], l_g[0], v3(o_g[1], 4), v3(l_g[1], 4), v3(o_g[2], 16), v3(l_g[2], 16))


def _dil_bwd_prep(d_ya, ya, lse, s):
    tr = min(2048, s)
    nat, sp4, sp16 = _stream_specs(tr)

    def body(dya_ref, ya_ref, lse_ref, dy0, dl0, dy1, ls1, dl1, dy2, ls2, dl2, dlt):
        lane_lo = lax.broadcasted_iota(jnp.int32, (tr, 128), 1) < 64
        prod = dya_ref[...] * ya_ref[...].astype(F32)
        lo = jnp.where(lane_lo, prod, 0.0)
        dlt[...] = jnp.where(lane_lo, jnp.sum(lo, axis=1, keepdims=True), jnp.sum(prod - lo, axis=1, keepdims=True))
        dy0[...] = dya_ref[...].astype(BF16)
        dl0[...] = dlt[...]
        for dil, dy, ls, dl in ((4, dy1, ls1, dl1), (16, dy2, ls2, dl2)):
            for c in range(dil):
                rows = pl.ds(c, tr // dil, stride=dil)
                dy[c] = dya_ref[rows, :].astype(BF16)
                ls[c] = lse_ref[rows, :]
                dl[c] = dlt[rows, :]

    sh = lambda dil, dt: jax.ShapeDtypeStruct((dil, s // dil, DIL_W), dt)
    res = pl.pallas_call(
        body, name="dil_bwd_prep", grid=(s // tr, 4),
        in_specs=[nat, nat, nat], out_specs=[nat, nat, sp4, sp4, sp4, sp16, sp16, sp16],
        out_shape=[jax.ShapeDtypeStruct((s, DIL_W), BF16), jax.ShapeDtypeStruct((s, DIL_W), F32),
                   sh(4, BF16), sh(4, F32), sh(4, F32), sh(16, BF16), sh(16, F32), sh(16, F32)],
        scratch_shapes=[pltpu.VMEM((tr, 128), F32)],
        compiler_params=_cparams(("parallel", "parallel")),
    )(d_ya, ya, lse)
    dy0, dl0, dy1, ls1, dl1, dy2, ls2, dl2 = [r.reshape(s, DIL_W) for r in res]
    return [(dy0, lse, dl0), (dy1, ls1, dl1), (dy2, ls2, dl2)]


_RET_ROWS = [(256 * h, 256) for h in range(4)], [(1024 + 256 * h, 256) for h in range(4)], \
            [(2048 + 512 * h, 512) for h in range(4)], [(4096 + 512 * h, 512) for h in range(4)]


def _split_w_in(win):
    rows = lambda a, n: win[a:a + n]
    w_ret = jnp.concatenate([rows(*seg[h]) for h in range(RET_HEADS) for seg in _RET_ROWS], axis=0)
    w_dil = [jnp.concatenate([rows(base + 512 * g, 512) for base in (6144, 7680, 9216)], axis=0) for g in range(3)]
    return w_ret, win[10752:12800], w_dil


def _join_w_in(g_ret, g_gate, g_dil):
    parts = []
    for i in range(4):
        off = (0, 256, 512, 1024)[i]
        parts += [g_ret[1536 * h + off:1536 * h + off + _RET_ROWS[i][h][1]] for h in range(RET_HEADS)]
    for i in range(3):
        parts += [g_dil[g][512 * i:512 * (i + 1)] for g in range(3)]
    return jnp.concatenate(parts + [g_gate], axis=0)


def _local_step(xs, pb, tgt, tabs, wts, vec, s, late_shards=None):
    tm = min(2048, s)
    tr = min(256, s)
    mm = functools.partial(_matmul, tm=tm)
    on_mesh = late_shards is not None
    wts = dict(wts)
    w_ret, w_gate, w_dil = _split_w_in(wts["w_in"])
    blocks = lambda g: g.reshape(N_DEV, g.shape[0] // N_DEV, g.shape[1])

    u = _prenorm(xs, vec["g_pre_mix"], s)
    proj_ret = mm(u[0], w_ret, mode="nt", m=s, n=6144, k=1024, tn=1024, tk=1024, out_dtype=BF16, name="inproj_ret")
    proj_gate = mm(u[0], w_gate, mode="nt", m=s, n=2048, k=1024, tn=1024, tk=1024, out_dtype=BF16, name="inproj_gate")
    qkv = [mm(u[g], w_dil[g], mode="nt", m=s, n=1536, k=1024, tn=1536, tk=1024, out_dtype=BF16, name="inproj_dil%d" % g)
           for g in range(3)]

    names = list(late_shards) if on_mesh else []
    gather = _Exchange([late_shards[n] for n in names], [False] * len(names)) if on_mesh else None
    (yr, y_ret, rstate), gathered = _ret_fwd(proj_ret, tabs["cos_r"], tabs["sin_r"], s, carry=gather)
    wts.update({n: g.reshape(N_DEV * g.shape[1], g.shape[2]) for n, g in zip(names, gathered)})
    a_br = mm(yr, wts["w_ret_out"], mode="nn", m=s, n=1024, k=2048, tn=1024, tk=1024, out_dtype=BF16, name="ret_out")

    qkr, o_g, l_g = [], [], []
    for g, dil in enumerate(DIL_GROUPS):
        qkr.append(_rope_qk(qkv[g], *tabs["dil"][g], s, "rope_qk%d" % g))
        o, l = _dil_fwd(qkr[g], qkv[g], dil, s, "dil_fwd%d" % g)
        o_g.append(o)
        l_g.append(l)
    ya, lse = _dil_merge(o_g, l_g, s)
    b_br = mm(ya, wts["w_dil_out"], mode="nt", m=s, n=1024, k=512, tn=1024, tk=512, out_dtype=BF16, name="dil_out")

    def gate_mix(a, b, gr, ga, b0, b1):
        return [_sigmoid(gr.astype(F32) + b0) * a.astype(F32) + _sigmoid(ga.astype(F32) + b1) * b.astype(F32)], []

    (mixed,), _ = _rowwise("gate_mix", gate_mix, s, tr, [(a_br, 1024, 0), (b_br, 1024, 0), (proj_gate, 1024, 0), (proj_gate, 1024, 1)],
                           [vec["b0"], vec["b1"]], [(1024, BF16)])
    z = mm(mixed, wts["w_o"], mode="nn", m=s, n=1024, k=1024, tn=1024, tk=1024, out_dtype=BF16, name="w_o")

    def post_norm(h, f, g_post, g_pre):
        hn = h + _rms(f) * g_post
        return [hn, _rms(hn) * g_pre], []

    (h1, v2), _ = _rowwise("post_mix", post_norm, s, tr, [(xs, 1024, 0), (z, 1024, 0)], [vec["g_post_mix"], vec["g_pre_mlp"]],
                           [(1024, F32), (1024, BF16)])
    a_up = mm(v2, wts["w_up"], mode="nt", m=s, n=4096, k=1024, tn=1024, tk=1024, out_dtype=BF16, name="mlp_up")
    f_dn = mm(a_up, wts["w_down"], mode="nn", m=s, n=1024, k=4096, tn=1024, tk=1024, out_dtype=BF16, name="mlp_down", a_fn=_relu_sq)
    (h2, t_ple), _ = _rowwise("post_mlp", post_norm, s, tr, [(h1, 1024, 0), (f_dn, 1024, 0)], [vec["g_post_mlp"], vec["g_pre_ple"]],
                              [(1024, F32), (1024, BF16)])
    gl = mm(t_ple, wts["w_ple_gate"], mode="nn", m=s, n=1024, k=1024, tn=1024, tk=1024, out_dtype=BF16, name="ple_gate")
    e_ple = mm(pb, wts["w_ple_in"], mode="nt", m=s, n=1024, k=256, tn=1024, tk=256, out_dtype=BF16, name="ple_in")

    def ple_loss(h, glv, e, tg, b, g):
        gate = _sigmoid(glv + b)
        ge = gate * e
        diff = h + _rms(ge) * g - tg
        dy = diff * (1.0 / D_MODEL)
        d_ge, dg = _rms_bwd(ge, g, dy)
        d_gl = d_ge * e * gate * (1.0 - gate)
        loss = jnp.zeros((1, D_MODEL), F32) + 0.5 * jnp.sum(diff * diff) * (1.0 / D_MODEL)
        return [dy, d_gl, d_ge * gate], [_colsum(dg), _colsum(d_gl), loss]

    (dy, d_gl, d_e), (dg_post_ple, db_ple, loss) = _rowwise(
        "ple_loss", ple_loss, s, tr, [(h2, 1024, 0), (gl, 1024, 0), (e_ple, 1024, 0), (tgt, 1024, 0)],
        [vec["b_ple"], vec["g_post_ple"]], [(1024, F32), (1024, BF16), (1024, BF16)], [1024, 1024, 1024])

    ts = min(1024, s)
    wg = functools.partial(_matmul, mode="tn", k=s, tk=ts, out_dtype=BF16)
    grads = {}
    grads["w_ple_in"] = wg(d_e, pb, m=1024, n=256, tm=1024, tn=256, name="g_ple_in")
    grads["w_ple_gate"] = wg(t_ple, d_gl, m=1024, n=1024, tm=1024, tn=1024, name="g_ple_gate")
    d_t = mm(d_gl, wts["w_ple_gate"], mode="nt", m=s, n=1024, k=1024, tn=1024, tk=1024, out_dtype=BF16, name="d_t")

    def bwd_ple_mlp(h, dt, dyv, f, g_pre, g_post):
        dx, dg1 = _rms_bwd(h, g_pre, dt)
        dh = dyv + dx
        df, dg2 = _rms_bwd(f, g_post, dh)
        return [dh, df], [_colsum(dg1), _colsum(dg2)]

    (d_h2, d_f), (dg_pre_ple, dg_post_mlp) = _rowwise(
        "bwd_ple_mlp", bwd_ple_mlp, s, tr, [(h2, 1024, 0), (d_t, 1024, 0), (dy, 1024, 0), (f_dn, 1024, 0)],
        [vec["g_pre_ple"], vec["g_post_mlp"]], [(1024, F32), (1024, BF16)], [1024, 1024])
    d_a = mm(d_f, wts["w_down"], mode="nt", m=s, n=4096, k=1024, tn=1024, tk=1024, out_dtype=BF16, name="d_a",
             epi=a_up, epi_fn=lambda acc, av: acc * (2.0 * jnp.maximum(av.astype(F32), 0.0)))
    grads["w_down"] = wg(a_up, d_f, m=4096, n=1024, tm=2048, tn=1024, name="g_down", a_fn=_relu_sq)
    grads["w_up"] = wg(d_a, v2, m=4096, n=1024, tm=2048, tn=1024, name="g_up")
    d_v2 = mm(d_a, wts["w_up"], mode="nn", m=s, n=1024, k=4096, tn=1024, tk=1024, out_dtype=BF16, name="d_v2")

    (d_h1, d_z), (dg_pre_mlp, dg_post_mix) = _rowwise(
        "bwd_mlp_mix", bwd_ple_mlp, s, tr, [(h1, 1024, 0), (d_v2, 1024, 0), (d_h2, 1024, 0), (z, 1024, 0)],
        [vec["g_pre_mlp"], vec["g_post_mix"]], [(1024, F32), (1024, BF16)], [1024, 1024])
    d_mixed = mm(d_z, wts["w_o"], mode="nt", m=s, n=1024, k=1024, tn=1024, tk=1024, out_dtype=BF16, name="d_mixed")
    grads["w_o"] = wg(mixed, d_z, m=1024, n=1024, tm=1024, tn=1024, name="g_o")

    def bwd_gate(dm, a, b, gr, ga, b0, b1):
        sa, sb = _sigmoid(gr.astype(F32) + b0), _sigmoid(ga.astype(F32) + b1)
        dgr = dm * a.astype(F32) * sa * (1.0 - sa)
        dga = dm * b.astype(F32) * sb * (1.0 - sb)
        return [dm * sa, dm * sb, jnp.concatenate([dgr, dga], axis=1)], [_colsum(dgr), _colsum(dga)]

    (d_abr, d_bbr, dproj_gate), (db0, db1) = _rowwise(
        "bwd_gate", bwd_gate, s, tr, [(d_mixed, 1024, 0), (a_br, 1024, 0), (b_br, 1024, 0), (proj_gate, 1024, 0), (proj_gate, 1024, 1)],
        [vec["b0"], vec["b1"]], [(1024, BF16), (1024, BF16), (2048, BF16)], [1024, 1024])
    grads["w_ret_out"] = wg(yr, d_abr, m=2048, n=1024, tm=2048, tn=1024, name="g_ret_out")
    d_yr = mm(d_abr, wts["w_ret_out"], mode="nt", m=s, n=2048, k=1024, tn=1024, tk=1024, out_dtype=BF16, name="d_yr")
    grads["w_dil_out"] = wg(d_bbr, ya, m=1024, n=512, tm=1024, tn=512, name="g_dil_out")
    d_ya = mm(d_bbr, wts["w_dil_out"], mode="nn", m=s, n=512, k=1024, tn=512, tk=1024, out_dtype=F32, name="d_ya")

    slots = {}
    names = list(grads) if on_mesh else []
    shares = _Exchange([blocks(grads[n]) for n in names], [True] * len(names)) if on_mesh else None
    (dproj_ret,), got = _ret_bwd(proj_ret, tabs["cos_r"], tabs["sin_r"], y_ret, d_yr, rstate, s, carry=shares)
    slots.update(zip(names, got))
    upstream = _dil_bwd_prep(d_ya, ya, lse, s)
    dqkv = [_dil_bwd(qkr[g], qkv[g], *upstream[g], *tabs["dil"][g], dil, s, "dil_bwd%d" % g)
            for g, dil in enumerate(DIL_GROUPS)]

    g_ret = wg(dproj_ret, u[0], m=6144, n=1024, tm=2048, tn=1024, name="g_in_ret")
    g_gate = wg(dproj_gate, u[0], m=2048, n=1024, tm=2048, tn=1024, name="g_in_gate")
    g_dil = [wg(dqkv[g], u[g], m=1536, n=1024, tm=1536, tn=1024, name="g_in_dil%d" % g) for g in range(3)]
    grads["w_in"] = _join_w_in(g_ret, g_gate, g_dil)

    du_ret = functools.partial(mm, dproj_ret, w_ret, mode="nn", m=s, n=1024, k=6144, tn=1024, tk=1024, out_dtype=BF16, name="du_ret")
    if on_mesh:
        du_ret, (slots["w_in"],) = du_ret(carry=_Exchange([blocks(grads["w_in"])], [True]))
    else:
        du_ret = du_ret()
    du_gate = mm(dproj_gate, w_gate, mode="nn", m=s, n=1024, k=2048, tn=1024, tk=1024, out_dtype=BF16, name="du_gate")
    du_dil = [mm(dqkv[g], w_dil[g], mode="nn", m=s, n=1024, k=1536, tn=1024, tk=1536, out_dtype=BF16, name="du_dil%d" % g)
              for g in range(3)]

    grad_x, dg_pre_mix = _grad_x(xs, d_h1, (du_ret, du_gate, du_dil[0]), du_dil[1], du_dil[2], vec["g_pre_mix"], s)

    zero = jnp.zeros((1, D_MODEL), F32)
    packet = jnp.concatenate([dg_pre_mix, dg_post_mix, dg_pre_mlp, dg_post_mlp, dg_pre_ple, db_ple, dg_post_ple, loss,
                              db0, db1] + [zero] * 6, axis=0)
    return grad_x, (slots if on_mesh else grads), packet


def _mesh_pos():
    return lax.axis_index("x"), lax.axis_index("y"), lax.axis_index("c")


def _all_gather(shards):
    nw = len(shards)

    def body(*refs):
        ins, outs = refs[:nw], refs[nw:2 * nw]
        send_sems, recv_sems, local_sems = refs[2 * nw:]
        x, y, c = _mesh_pos()
        me, sibling = (x, y, c), (x, y, 1 - c)
        chips = [(1 - x, y), (x, 1 - y), (1 - x, 1 - y)]

        def region(w, dev):
            return outs[w].at[4 * dev[0] + 2 * dev[1] + dev[2]]

        def copy(w, kk, block, to, src=None):
            return pltpu.make_async_remote_copy(
                src_ref=region(w, block) if src is None else src, dst_ref=region(w, block),
                send_sem=send_sems.at[w * 7 + kk], recv_sem=recv_sems.at[w * 7 + kk], device_id=to, device_id_type=MESH)

        mine = [pltpu.make_async_copy(ins[w], region(w, me), local_sems.at[w]) for w in range(nw)]
        for cp in mine:
            cp.start()
        first = []
        for w in range(nw):
            first.append(copy(w, 0, me, sibling, src=ins[w]))
            first += [copy(w, 1 + j, me, (*chip, c), src=ins[w]) for j, chip in enumerate(chips)]
        for cp in first:
            cp.start()
        passed = []
        for j, chip in enumerate(chips):
            for w in range(nw):
                copy(w, 1 + j, (*chip, c), me).wait_recv()
                cp = copy(w, 4 + j, (*chip, c), sibling)
                cp.start()
                passed.append(cp)
        for w in range(nw):
            copy(w, 0, sibling, me).wait_recv()
            for j, chip in enumerate(chips):
                copy(w, 4 + j, (*chip, 1 - c), me).wait_recv()
        for cp in first + passed:
            cp.wait_send()
        for cp in mine:
            cp.wait()

    hbm = pl.BlockSpec(memory_space=pl.ANY)
    return pl.pallas_call(
        body, name="gather_weights",
        in_specs=[hbm] * nw, out_specs=[hbm] * nw,
        out_shape=[jax.ShapeDtypeStruct((N_DEV,) + sh.shape, sh.dtype) for sh in shards],
        scratch_shapes=[pltpu.SemaphoreType.DMA((nw * 7,)), pltpu.SemaphoreType.DMA((nw * 7,)), pltpu.SemaphoreType.DMA((nw,))],
    )(*shards)


class _Exchange:
    def __init__(self, arrays, scatter):
        self.arrays, self.scatter, self.n = list(arrays), list(scatter), len(arrays)
        self.out_shape = [jax.ShapeDtypeStruct(a.shape if sc else (N_DEV,) + a.shape, a.dtype)
                          for a, sc in zip(self.arrays, self.scatter)]
        self.scratch = [pltpu.SemaphoreType.DMA((self.n * 7,)), pltpu.SemaphoreType.DMA((self.n * 7,)),
                        pltpu.SemaphoreType.DMA((self.n,))]
        self.specs = [pl.BlockSpec(memory_space=pl.ANY)] * self.n

    def _copies(self, srcs, dsts, sems):
        send_sems, recv_sems, local_sems = sems
        x, y, c = _mesh_pos()
        my = 4 * x + 2 * y + c
        src_of = lambda w, idx: srcs[w].at[idx] if self.scatter[w] else srcs[w]
        local = [pltpu.make_async_copy(src_of(w, my), dsts[w].at[my], local_sems.at[w]) for w in range(self.n)]
        sends, recvs = [], []
        for w in range(self.n):
            for r in range(1, N_DEV):
                px = 1 - x if r & 4 else x
                py = 1 - y if r & 2 else y
                pc = 1 - c if r & 1 else c
                pidx = 4 * px + 2 * py + pc
                kw = dict(send_sem=send_sems.at[w * 7 + r - 1], recv_sem=recv_sems.at[w * 7 + r - 1],
                          device_id=(px, py, pc), device_id_type=MESH)
                sends.append(pltpu.make_async_remote_copy(src_ref=src_of(w, pidx), dst_ref=dsts[w].at[my], **kw))
                recvs.append(pltpu.make_async_remote_copy(src_ref=src_of(w, pidx), dst_ref=dsts[w].at[pidx], **kw))
        return local, sends, recvs

    def start(self, srcs, dsts, sems):
        local, sends, _ = self._copies(srcs, dsts, sems)
        for cp in local + sends:
            cp.start()

    def wait(self, srcs, dsts, sems):
        local, sends, recvs = self._copies(srcs, dsts, sems)
        for cp in recvs:
            cp.wait_recv()
        for cp in sends:
            cp.wait_send()
        for cp in local:
            cp.wait()

    def split(self, refs, n_in, n_out):
        srcs = refs[n_in:n_in + self.n]
        dsts = refs[n_in + self.n + n_out:n_in + 2 * self.n + n_out]
        return srcs, dsts, refs[len(refs) - 3:]


def _run_exchange(ex, name):
    def body(*refs):
        parts = ex.split(refs, 0, 0)
        ex.start(*parts)
        ex.wait(*parts)

    return pl.pallas_call(body, name=name, in_specs=ex.specs, out_specs=ex.specs, out_shape=ex.out_shape,
                          scratch_shapes=ex.scratch)(*ex.arrays)


def _pick_rows(r, c, target_bytes):
    t = r
    while (t // 2) % 16 == 0 and t // 2 >= 16 and t * c * 4 > target_bytes:
        t //= 2
    return t


def _sum_slots(slots, name):
    ns, r, c = slots.shape
    tr = _pick_rows(r, c, 256 * 1024)

    def body(s_ref, o_ref):
        acc = s_ref[0].astype(F32)
        for kk in range(1, ns):
            acc = acc + s_ref[kk].astype(F32)
        o_ref[...] = acc

    return pl.pallas_call(
        body, name=name, grid=(r // tr,),
        in_specs=[pl.BlockSpec((ns, tr, c), lambda i: (0, i, 0))], out_specs=pl.BlockSpec((tr, c), lambda i: (i, 0)),
        out_shape=jax.ShapeDtypeStruct((r, c), F32), compiler_params=_cparams(("parallel",)),
    )(slots)


def _adamw(slots, w, m, v, name):
    ns, r, c = slots.shape
    tr = _pick_rows(r, c, 256 * 1024)

    def body(s_ref, w_ref, m_ref, v_ref, g_out, d_out, m_out, v_out):
        g = s_ref[0].astype(F32)
        for kk in range(1, ns):
            g = g + s_ref[kk].astype(F32)
        mn = ADAM_B1 * m_ref[...] + (1.0 - ADAM_B1) * g
        vn = ADAM_B2 * v_ref[...] + (1.0 - ADAM_B2) * (g * g)
        m_hat = mn / (1.0 - ADAM_B1 ** ADAM_STEP)
        v_hat = vn / (1.0 - ADAM_B2 ** ADAM_STEP)
        g_out[...] = g
        d_out[...] = -ADAM_LR * (m_hat / (jnp.sqrt(v_hat) + ADAM_EPS) + ADAM_WD * w_ref[...])
        m_out[...] = mn
        v_out[...] = vn

    blk = pl.BlockSpec((tr, c), lambda i: (i, 0))
    return pl.pallas_call(
        body, name=name, grid=(r // tr,),
        in_specs=[pl.BlockSpec((ns, tr, c), lambda i: (0, i, 0)), blk, blk, blk], out_specs=[blk] * 4,
        out_shape=[jax.ShapeDtypeStruct((r, c), F32)] * 4, compiler_params=_cparams(("parallel",)),
    )(slots, w, m, v)


def _rotary_tables(pos, s):
    posf = pos.astype(F32)
    inv_freq = 1.0 / (10000.0 ** jnp.linspace(0.0, 1.0, RET_QK // 2, dtype=F32))
    ang = posf[:, None] * inv_freq
    tabs = {"cos_r": jnp.cos(ang), "sin_r": jnp.sin(ang), "dil": []}
    freqs = 500000.0 ** (-jnp.arange(0, 16, 2, dtype=F32) / 16)
    spread = np.zeros((16, 384), np.float32)
    bias = np.zeros((1, 384), np.float32)
    for head in range(2):
        for i in range(8):
            spread[i, 64 * head + i] = spread[i, 64 * head + 8 + i] = 1.0
            spread[8 + i, 128 + 64 * head + i] = -1.0
            spread[8 + i, 256 + 64 * head + 8 + i] = 1.0
        bias[0, 64 * head + 16:64 * head + 64] = 1.0

    def expand(t, e, b):
        hi = t.astype(BF16)
        lo = (t - hi.astype(F32)).astype(BF16)
        out = _dot(hi, e, NN) + _dot(lo, e, NN) + b
        return [out[:, 0:128], out[:, 128:256], out[:, 256:384]], []

    for g, dil in enumerate(DIL_GROUPS):
        ang = posf.reshape(s // dil, dil).T.reshape(s, 1) * freqs
        cs = jnp.concatenate([jnp.cos(ang), jnp.sin(ang)], axis=1)
        t3, _ = _rowwise("rot_tables%d" % g, expand, s, min(1024, s), [(cs, 16, 0)],
                         [jnp.asarray(spread, BF16), jnp.asarray(bias)], [(128, F32)] * 3)
        tabs["dil"].append(tuple(t3))
    return tabs


_TRANSPOSED = ("w_in", "w_dil_out", "w_up", "w_ple_in")
_MATS = ("w_in", "w_ret_out", "w_dil_out", "w_o", "w_up", "w_down", "w_ple_gate", "w_ple_in")
_VECS = ("g_pre_mix", "g_post_mix", "g_pre_mlp", "g_post_mlp", "g_pre_ple", "b_ple_gate", "g_post_ple")
_ORDER = ("w_in", "b_gate", "w_ret_out", "w_dil_out", "w_o", "g_pre_mix", "g_post_mix", "g_pre_mlp", "g_post_mlp", "w_up",
          "w_down", "g_pre_ple", "w_ple_gate", "b_ple_gate", "w_ple_in", "g_post_ple")


def kernel(x, p, positions, w_in, b_gate, w_ret_out, w_dil_out, w_o, g_pre_mix, g_post_mix, g_pre_mlp, g_post_mlp, w_up, w_down, g_pre_ple, w_ple_gate, b_ple_gate, w_ple_in, g_post_ple, loss_target, m_w_in, m_b_gate, m_w_ret_out, m_w_dil_out, m_w_o, m_g_pre_mix, m_g_post_mix, m_g_pre_mlp, m_g_post_mlp, m_w_up, m_w_down, m_g_pre_ple, m_w_ple_gate, m_b_ple_gate, m_w_ple_in, m_g_post_ple, v_w_in, v_b_gate, v_w_ret_out, v_w_dil_out, v_w_o, v_g_pre_mix, v_g_post_mix, v_g_pre_mlp, v_g_post_mlp, v_w_up, v_w_down, v_g_pre_ple, v_w_ple_gate, v_b_ple_gate, v_w_ple_in, v_g_post_ple):
    s = x.shape[1]
    wd = dict(w_in=w_in, b_gate=b_gate, w_ret_out=w_ret_out, w_dil_out=w_dil_out, w_o=w_o, g_pre_mix=g_pre_mix,
              g_post_mix=g_post_mix, g_pre_mlp=g_pre_mlp, g_post_mlp=g_post_mlp, w_up=w_up, w_down=w_down,
              g_pre_ple=g_pre_ple, w_ple_gate=w_ple_gate, b_ple_gate=b_ple_gate, w_ple_in=w_ple_in, g_post_ple=g_post_ple)
    md = dict(w_in=m_w_in, b_gate=m_b_gate, w_ret_out=m_w_ret_out, w_dil_out=m_w_dil_out, w_o=m_w_o, g_pre_mix=m_g_pre_mix,
              g_post_mix=m_g_post_mix, g_pre_mlp=m_g_pre_mlp, g_post_mlp=m_g_post_mlp, w_up=m_w_up, w_down=m_w_down,
              g_pre_ple=m_g_pre_ple, w_ple_gate=m_w_ple_gate, b_ple_gate=m_b_ple_gate, w_ple_in=m_w_ple_in, g_post_ple=m_g_post_ple)
    vd = dict(w_in=v_w_in, b_gate=v_b_gate, w_ret_out=v_w_ret_out, w_dil_out=v_w_dil_out, w_o=v_w_o, g_pre_mix=v_g_pre_mix,
              g_post_mix=v_g_post_mix, g_pre_mlp=v_g_pre_mlp, g_post_mlp=v_g_post_mlp, w_up=v_w_up, w_down=v_w_down,
              g_pre_ple=v_g_pre_ple, w_ple_gate=v_w_ple_gate, b_ple_gate=v_b_ple_gate, w_ple_in=v_w_ple_in, g_post_ple=v_g_post_ple)

    shards = {n: (wd[n][0].T if n in _TRANSPOSED else wd[n][0]).astype(BF16) for n in _MATS}
    w_in_all, bg_all = _all_gather([shards.pop("w_in"), b_gate[0]])
    wts = {"w_in": w_in_all.reshape(N_DEV * w_in_all.shape[1], D_MODEL)}
    bg = bg_all.transpose(1, 0, 2).reshape(2, D_MODEL)
    vec = {n: wd[n] for n in _VECS}
    vec.update(b0=bg[0:1], b1=bg[1:2], b_ple=b_ple_gate)

    tabs = _rotary_tables(positions[0], s)
    grad_x, slots, packet = _local_step(x[0], p[0, 0].astype(BF16), loss_target[0], tabs, wts, vec, s, late_shards=shards)

    (packets,) = _run_exchange(_Exchange([packet], [False]), "exchange_vectors")
    out = {}
    for n in _MATS:
        sl = slots[n]
        if n in _TRANSPOSED:
            sl = _sum_slots(sl, "sum_" + n).T[None]
        out[n] = _adamw(sl, wd[n][0], md[n][0], vd[n][0], "adamw_" + n)
    zero_rows = jnp.zeros((16 - len(_VECS), D_MODEL), F32)
    pack = lambda d: jnp.concatenate([d[n] for n in _VECS] + [zero_rows], axis=0)
    small = _adamw(packets, pack(wd), pack(md), pack(vd), "adamw_vectors")
    for i, n in enumerate(_VECS):
        out[n] = tuple(t[i:i + 1] for t in small)
    my = 4 * lax.axis_index("x") + 2 * lax.axis_index("y") + lax.axis_index("c")
    g_bias = lax.dynamic_slice(small[0], (8, my * 128), (2, 128))
    out["b_gate"] = _adamw(g_bias[None], b_gate[0], m_b_gate[0], v_b_gate[0], "adamw_b_gate")
    loss = small[0][7, 0]

    res = [loss, grad_x[None]]
    for kk in range(4):
        res += [out[n][kk][None] if out[n][kk].ndim == 2 and wd[n].ndim == 3 else out[n][kk] for n in _ORDER]
    return tuple(res)
```

```python
import functools
import math

import numpy as np
import jax
import jax.numpy as jnp
from jax import lax
from jax.experimental import pallas as pl
from jax.experimental.pallas import tpu as pltpu

F32, BF16 = jnp.float32, jnp.bfloat16
D_MODEL = 1024
EPS = 1e-6
N_DEV = 8
RET_HEADS, RET_QK, RET_V, RET_CHUNK = 4, 256, 512, 128
DIL_GROUPS = (1, 4, 16)
DIL_W = 512
QB = 128
NEG = -1e30
ADAM_LR, ADAM_B1, ADAM_B2, ADAM_EPS, ADAM_WD, ADAM_STEP = 0.001, 0.9, 0.999, 1e-08, 0.01, 10
VMEM_LIMIT_BYTES = 56 * 1024 * 1024
MESH = pl.DeviceIdType.MESH

NN = ((1,), (0,))
NT = ((1,), (1,))
TN = ((0,), (0,))


def _dot(a, b, dn):
    return lax.dot_general(a, b, (dn, ((), ())), preferred_element_type=F32)


def _cparams(sem):
    return pltpu.CompilerParams(dimension_semantics=sem, vmem_limit_bytes=VMEM_LIMIT_BYTES)


def _rms(x):
    return x * lax.rsqrt(jnp.mean(x * x, axis=-1, keepdims=True) + EPS)


def _rms_bwd(x, g, dy):
    r = lax.rsqrt(jnp.mean(x * x, axis=-1, keepdims=True) + EPS)
    xh = x * r
    t = dy * g
    dx = r * (t - xh * jnp.mean(t * xh, axis=-1, keepdims=True))
    return dx, dy * xh


def _colsum(v):
    return jnp.sum(v, axis=0, keepdims=True)


def _sigmoid(v):
    return 1.0 / (1.0 + jnp.exp(-v))


def _pallas(compute, *, name, grid, in_specs, out_specs, out_shape, scratch, semantics, args, carry=None):
    n_in, n_out = len(in_specs), len(out_specs)
    if carry is None:
        res = pl.pallas_call(compute, name=name, grid=grid, in_specs=in_specs, out_specs=out_specs, out_shape=out_shape,
                             scratch_shapes=scratch, compiler_params=_cparams(semantics))(*args)
        return res, []
    n_steps = math.prod(grid)

    def body(*refs):
        step = 0
        for axis, size in enumerate(grid):
            step = step * size + pl.program_id(axis)
        parts = carry.split(refs, n_in, n_out)
        pl.when(step == 0)(lambda: carry.start(*parts))
        compute(*refs[:n_in], *refs[n_in + carry.n:n_in + carry.n + n_out], *refs[n_in + 2 * carry.n + n_out:len(refs) - 3])
        pl.when(step == n_steps - 1)(lambda: carry.wait(*parts))

    res = pl.pallas_call(
        body, name=name, grid=grid, in_specs=list(in_specs) + carry.specs, out_specs=list(out_specs) + carry.specs,
        out_shape=list(out_shape) + carry.out_shape, scratch_shapes=list(scratch) + carry.scratch,
        compiler_params=_cparams(("arbitrary",) * len(grid)))(*args, *carry.arrays)
    return res[:n_out], res[n_out:]


def _matmul(a, b, *, mode, m, n, k, tm, tn, tk, out_dtype, name, a_fn=None, epi=(), epi_width=None, epi_fn=None, carry=None):
    nk = k // tk
    grid = (m // tm, n // tn, nk)
    if mode == "nn":
        a_blk, a_im, b_blk, b_im, dn = (tm, tk), (lambda i, j, kk: (i, kk)), (tk, tn), (lambda i, j, kk: (kk, j)), NN
    elif mode == "nt":
        a_blk, a_im, b_blk, b_im, dn = (tm, tk), (lambda i, j, kk: (i, kk)), (tn, tk), (lambda i, j, kk: (j, kk)), NT
    else:
        a_blk, a_im, b_blk, b_im, dn = (tk, tm), (lambda i, j, kk: (kk, i)), (tk, tn), (lambda i, j, kk: (kk, j)), TN
    o_im = lambda i, j, kk: (i, j)
    n_in = 2 + len(epi)

    def body(*refs):
        a_ref, b_ref = refs[0], refs[1]
        o_ref = refs[n_in]
        acc_ref = refs[n_in + 1] if nk > 1 else None

        def finish(acc):
            if epi:
                acc = epi_fn(acc, *[r[...] for r in refs[2:n_in]])
            o_ref[...] = acc.astype(o_ref.dtype)

        av = a_ref[...]
        if a_fn is not None:
            av = a_fn(av)
        part = _dot(av, b_ref[...], dn)
        if nk == 1:
            finish(part)
        else:
            kk = pl.program_id(2)

            @pl.when(kk == 0)
            def _():
                acc_ref[...] = part

            @pl.when(kk > 0)
            def _():
                acc_ref[...] += part

            @pl.when(kk == nk - 1)
            def _():
                finish(acc_ref[...])

    epi_spec = pl.BlockSpec((tm, tn), o_im) if epi_width is None else pl.BlockSpec((tm, epi_width), lambda i, j, kk: (i, 0))
    in_specs = [pl.BlockSpec(a_blk, a_im), pl.BlockSpec(b_blk, b_im)] + [epi_spec] * len(epi)
    args = [a, b, *epi]
    (out,), got = _pallas(
        body, name=name, grid=grid, in_specs=in_specs, out_specs=[pl.BlockSpec((tm, tn), o_im)],
        out_shape=[jax.ShapeDtypeStruct((m, n), out_dtype)], scratch=[pltpu.VMEM((tm, tn), F32)] if nk > 1 else [],
        semantics=("parallel", "parallel", "arbitrary"), args=args, carry=carry)
    return out if carry is None else (out, got)


def _relu_sq(v):
    r = jnp.maximum(v.astype(F32), 0.0)
    return (r * r).astype(BF16)


def _rowwise(name, fn, s, tr, rows, vecs, outs, accs=()):
    n_r, n_v, n_o, n_a = len(rows), len(vecs), len(outs), len(accs)

    def body(*refs):
        vals = [refs[i][...].astype(F32) for i in range(n_r)] + [refs[n_r + i][...] for i in range(n_v)]
        o_refs = refs[n_r + n_v:n_r + n_v + n_o]
        a_refs = refs[n_r + n_v + n_o:]
        o_vals, a_vals = fn(*vals)
        for ref, val in zip(o_refs, o_vals):
            ref[...] = val.astype(ref.dtype)
        if n_a:
            @pl.when(pl.program_id(0) == 0)
            def _():
                for ref in a_refs:
                    ref[...] = jnp.zeros_like(ref)

            for ref, val in zip(a_refs, a_vals):
                ref[...] += val

    in_specs = [pl.BlockSpec((tr, w), functools.partial(lambda i, cb: (i, cb), cb=cb)) for _, w, cb in rows]
    in_specs += [pl.BlockSpec(v.shape, lambda i: (0, 0)) for v in vecs]
    out_specs = [pl.BlockSpec((tr, w), lambda i: (i, 0)) for w, _ in outs]
    out_specs += [pl.BlockSpec((1, w), lambda i: (0, 0)) for w in accs]
    out_shape = [jax.ShapeDtypeStruct((s, w), dt) for w, dt in outs]
    out_shape += [jax.ShapeDtypeStruct((1, w), F32) for w in accs]
    res = pl.pallas_call(
        body, name=name, grid=(s // tr,), in_specs=in_specs, out_specs=out_specs, out_shape=out_shape,
        compiler_params=_cparams(("arbitrary",)),
    )(*[r[0] for r in rows], *vecs)
    return res[:n_o], res[n_o:]


_ROW_TILE = 256
_STREAM_SPECS = [pl.BlockSpec((dil, _ROW_TILE // dil, D_MODEL), lambda i: (0, i, 0)) for dil in DIL_GROUPS[1:]]
_NAT_SPEC = pl.BlockSpec((_ROW_TILE, D_MODEL), lambda i: (i, 0))
_VEC_SPEC = pl.BlockSpec((1, D_MODEL), lambda i: (0, 0))
_COL_BLOCKS = pltpu.VMEM((D_MODEL // 128, _ROW_TILE, 128), F32)


def _prenorm(xs, g, s):
    tr = _ROW_TILE

    def body(x_ref, g_ref, u_ref, u4_ref, u16_ref, buf):
        xn = _rms(x_ref[...]) * g_ref[...]
        u_ref[...] = xn.astype(BF16)
        for cb in range(8):
            buf[cb] = xn[:, cb * 128:(cb + 1) * 128]
        for dil, out in ((4, u4_ref), (16, u16_ref)):
            for c in range(dil):
                rows = pl.ds(c, tr // dil, stride=dil)
                out[c] = jnp.concatenate([buf.at[cb][rows, :] for cb in range(8)], axis=1).astype(BF16)

    res = pl.pallas_call(
        body, name="prenorm", grid=(s // tr,), in_specs=[_NAT_SPEC, _VEC_SPEC], out_specs=[_NAT_SPEC] + _STREAM_SPECS,
        out_shape=[jax.ShapeDtypeStruct((s, D_MODEL), BF16)]
        + [jax.ShapeDtypeStruct((dil, s // dil, D_MODEL), BF16) for dil in DIL_GROUPS[1:]],
        scratch_shapes=[_COL_BLOCKS], compiler_params=_cparams(("parallel",)),
    )(xs, g)
    return [r.reshape(s, D_MODEL) for r in res]


def _grad_x(xs, d_h1, du_nat, du4, du16, g, s):
    tr = _ROW_TILE

    def body(x_ref, dh_ref, a_ref, b_ref, c_ref, u4_ref, u16_ref, g_ref, dx_ref, dg_ref, buf):
        du = a_ref[...].astype(F32) + b_ref[...].astype(F32) + c_ref[...].astype(F32)
        for dil, src in ((4, u4_ref), (16, u16_ref)):
            for c in range(dil):
                part = src[c].astype(F32)
                for cb in range(8):
                    buf.at[cb][pl.ds(c, tr // dil, stride=dil), :] = part[:, cb * 128:(cb + 1) * 128]
            du = du + jnp.concatenate([buf[cb] for cb in range(8)], axis=1)
        dx, dgr = _rms_bwd(x_ref[...], g_ref[...], du)
        dx_ref[...] = dh_ref[...] + dx

        @pl.when(pl.program_id(0) == 0)
        def _():
            dg_ref[...] = jnp.zeros_like(dg_ref)

        dg_ref[...] += _colsum(dgr)

    return pl.pallas_call(
        body, name="grad_x", grid=(s // tr,), in_specs=[_NAT_SPEC] * 5 + _STREAM_SPECS + [_VEC_SPEC],
        out_specs=[_NAT_SPEC, _VEC_SPEC],
        out_shape=[jax.ShapeDtypeStruct((s, D_MODEL), F32), jax.ShapeDtypeStruct((1, D_MODEL), F32)],
        scratch_shapes=[_COL_BLOCKS], compiler_params=_cparams(("arbitrary",)),
    )(xs, d_h1, *du_nat, du4.reshape(4, s // 4, D_MODEL), du16.reshape(16, s // 16, D_MODEL), g)


def _ret_tables():
    h = np.arange(RET_HEADS, dtype=np.float32)
    lg = np.log1p(-(np.float32(2.0) ** (-5.0 - h))).astype(np.float32)
    idx = np.arange(RET_CHUNK, dtype=np.float32)
    diff = idx[:, None] - idx[None, :]
    dm = np.where(diff[None] >= 0, np.exp(np.maximum(diff, 0.0)[None] * lg[:, None, None]), 0.0)
    qd = np.exp((idx + 1.0)[None, :, None] * lg[:, None, None])
    kd = np.exp((RET_CHUNK - 1.0 - idx)[None, :, None] * lg[:, None, None])
    cd = np.exp(RET_CHUNK * lg)[:, None, None]
    return [jnp.asarray(t, F32) for t in (dm, qd, kd, cd)]


def _rope_half(v, cos, sin):
    v1, v2 = v[:, :128], v[:, 128:]
    return jnp.concatenate([v1 * cos - v2 * sin, v2 * cos + v1 * sin], axis=1)


def _unrope_half(d, cos, sin):
    d1, d2 = d[:, :128], d[:, 128:]
    return jnp.concatenate([d1 * cos + d2 * sin, d2 * cos - d1 * sin], axis=1)


def _ret_specs(rb, rev_n):
    def rowmap(w_blk):
        return lambda h, n: (rev_n(n), w_blk(h))
    tab = [pl.BlockSpec((1, RET_CHUNK, RET_CHUNK), lambda h, n: (h, 0, 0)),
           pl.BlockSpec((1, RET_CHUNK, 1), lambda h, n: (h, 0, 0)),
           pl.BlockSpec((1, RET_CHUNK, 1), lambda h, n: (h, 0, 0)),
           pl.BlockSpec((1, 1, 1), lambda h, n: (h, 0, 0))]
    proj = pl.BlockSpec((rb, 1536), rowmap(lambda h: h))
    cs = pl.BlockSpec((rb, 128), rowmap(lambda h: 0))
    hv = pl.BlockSpec((rb, RET_V), rowmap(lambda h: h))
    return proj, cs, hv, tab


def _ret_fwd(proj_ret, cos, sin, s, carry=None):
    rb = min(512, s)
    ch = rb // RET_CHUNK
    nb = s // rb
    proj_spec, cs_spec, hv_spec, tab_specs = _ret_specs(rb, lambda n: n)

    def body(p_ref, cos_ref, sin_ref, dm_ref, qd_ref, kd_ref, cd_ref, yr_ref, y_ref, rs_ref, r_acc):
        @pl.when(pl.program_id(1) == 0)
        def _():
            r_acc[...] = jnp.zeros_like(r_acc)

        dm, qd, kd, cd = dm_ref[0], qd_ref[0], kd_ref[0], cd_ref[0]
        for c in range(ch):
            rows = slice(c * RET_CHUNK, (c + 1) * RET_CHUNK)
            cosv, sinv = cos_ref[rows, :], sin_ref[rows, :]
            q = _rope_half(p_ref[rows, 0:256].astype(F32), cosv, sinv)
            kk = _rope_half(p_ref[rows, 256:512].astype(F32), cosv, sinv) * (RET_QK ** -0.5)
            v = p_ref[rows, 512:1024]
            g = p_ref[rows, 1024:1536].astype(F32)
            rb16 = r_acc[...].astype(BF16)
            rs_ref[0, c] = rb16
            sc = _dot(q.astype(BF16), kk.astype(BF16), NT) * dm
            y = _dot(sc.astype(BF16), v, NN) + _dot((q * qd).astype(BF16), rb16, NN)
            r_acc[...] = r_acc[...] * cd + _dot((kk * kd).astype(BF16), v, TN)
            y_ref[rows, :] = y.astype(BF16)
            yr_ref[rows, :] = (_rms(y) * (g * _sigmoid(g))).astype(BF16)

    return _pallas(
        body, name="ret_fwd", grid=(RET_HEADS, nb),
        in_specs=[proj_spec, cs_spec, cs_spec] + tab_specs,
        out_specs=[hv_spec, hv_spec, pl.BlockSpec((1, ch, RET_QK, RET_V), lambda h, n: (h, n, 0, 0))],
        out_shape=[jax.ShapeDtypeStruct((s, RET_HEADS * RET_V), BF16), jax.ShapeDtypeStruct((s, RET_HEADS * RET_V), BF16),
                   jax.ShapeDtypeStruct((RET_HEADS, s // RET_CHUNK, RET_QK, RET_V), BF16)],
        scratch=[pltpu.VMEM((RET_QK, RET_V), F32)], semantics=("parallel", "arbitrary"),
        args=(proj_ret, cos, sin, *_ret_tables()), carry=carry)


def _ret_bwd(proj_ret, cos, sin, y, d_yr, rs, s, carry=None):
    rb = min(512, s)
    ch = rb // RET_CHUNK
    nb = s // rb
    proj_spec, cs_spec, hv_spec, tab_specs = _ret_specs(rb, lambda n: nb - 1 - n)

    def body(p_ref, cos_ref, sin_ref, y_ref, dyr_ref, rs_ref, dm_ref, qd_ref, kd_ref, cd_ref, o_ref, dr_acc):
        @pl.when(pl.program_id(1) == 0)
        def _():
            dr_acc[...] = jnp.zeros_like(dr_acc)

        dm, qd, kd, cd = dm_ref[0], qd_ref[0], kd_ref[0], cd_ref[0]
        for c in reversed(range(ch)):
            rows = slice(c * RET_CHUNK, (c + 1) * RET_CHUNK)
            cosv, sinv = cos_ref[rows, :], sin_ref[rows, :]
            q = _rope_half(p_ref[rows, 0:256].astype(F32), cosv, sinv)
            kk = _rope_half(p_ref[rows, 256:512].astype(F32), cosv, sinv) * (RET_QK ** -0.5)
            v = p_ref[rows, 512:1024]
            g = p_ref[rows, 1024:1536].astype(F32)
            yv = y_ref[rows, :].astype(F32)
            dyr = dyr_ref[rows, :].astype(F32)
            sg = _sigmoid(g)
            r = lax.rsqrt(jnp.mean(yv * yv, axis=-1, keepdims=True) + EPS)
            yn = yv * r
            dg = dyr * yn * (sg * (1.0 + g * (1.0 - sg)))
            dyn = dyr * (g * sg)
            dy = (r * (dyn - yn * jnp.mean(dyn * yn, axis=-1, keepdims=True))).astype(BF16)
            qb, kb = q.astype(BF16), kk.astype(BF16)
            rb16 = rs_ref[0, c]
            drb = dr_acc[...].astype(BF16)
            sd = _dot(qb, kb, NT) * dm
            ds = (_dot(dy, v, NT) * dm).astype(BF16)
            dq = _dot(ds, kb, NN) + qd * _dot(dy, rb16, NT)
            dk = _dot(ds, qb, TN) + kd * _dot(v, drb, NT)
            dv = _dot(sd.astype(BF16), dy, TN) + _dot((kk * kd).astype(BF16), drb, NN)
            dr_acc[...] = dr_acc[...] * cd + _dot((q * qd).astype(BF16), dy, TN)
            o_ref[rows, 0:256] = _unrope_half(dq, cosv, sinv).astype(BF16)
            o_ref[rows, 256:512] = (_unrope_half(dk, cosv, sinv) * (RET_QK ** -0.5)).astype(BF16)
            o_ref[rows, 512:1024] = dv.astype(BF16)
            o_ref[rows, 1024:1536] = dg.astype(BF16)

    in_specs = [proj_spec, cs_spec, cs_spec, hv_spec, hv_spec,
                pl.BlockSpec((1, ch, RET_QK, RET_V), lambda h, n: (h, nb - 1 - n, 0, 0))] + tab_specs
    return _pallas(
        body, name="ret_bwd", grid=(RET_HEADS, nb), in_specs=in_specs, out_specs=[proj_spec],
        out_shape=[jax.ShapeDtypeStruct((s, RET_HEADS * 1536), BF16)], scratch=[pltpu.VMEM((RET_QK, RET_V), F32)],
        semantics=("parallel", "arbitrary"), args=(proj_ret, cos, sin, y, d_yr, rs, *_ret_tables()), carry=carry)


def _rope_qk(acc, c, s1, s2):
    outs = []
    for cc in range(8):
        vv = acc[:, cc * 128:(cc + 1) * 128]
        outs.append(vv * c + pltpu.roll(vv, 120, 1) * s1 + pltpu.roll(vv, 8, 1) * s2)
    return jnp.concatenate(outs + [acc[:, 2 * DIL_W:]], axis=1)


def _pair_masks():
    ri = lax.broadcasted_iota(jnp.int32, (2 * QB, 2 * QB), 0)
    ci = lax.broadcasted_iota(jnp.int32, (2 * QB, 2 * QB), 1)
    e = ci - (ri & (QB - 1))
    lane_lo = lax.broadcasted_iota(jnp.int32, (2 * QB, 128), 1) < 64
    return ci, jnp.logical_and(e >= 0, e <= QB), lane_lo


def _stack_heads(v, lane_lo):
    z = jnp.zeros_like(v)
    return jnp.concatenate([jnp.where(lane_lo, v, z), jnp.where(lane_lo, z, v)], axis=0)


def _dil_fwd(qkv, dil, s, name):
    length = s // dil
    rb = min(512, length)
    nsub = rb // QB
    nbs = length // rb
    sub_per = rb // QB

    def body(q_ref, k_ref, v_ref, kp_ref, vp_ref, o_ref, l_ref, kf, vf):
        first = (pl.program_id(0) % nbs) == 0
        kf[0:QB, :] = kp_ref[...]
        kf[QB:, :] = k_ref[...]
        vf[0:QB, :] = vp_ref[...]
        vf[QB:, :] = v_ref[...]
        ci, band, lane_lo = _pair_masks()
        lo1 = lane_lo[0:QB]

        def step(i, carry):
            r0 = pl.multiple_of(i * QB, QB)
            mask = jnp.logical_and(band, ci >= jnp.where(jnp.logical_and(first, i == 0), QB, 0))
            for j in range(4):
                lanes = slice(j * 128, (j + 1) * 128)
                q2 = _stack_heads(q_ref[pl.ds(r0, QB), lanes], lo1)
                k2 = kf[pl.ds(r0, 2 * QB), lanes]
                v2 = _stack_heads(vf[pl.ds(r0, 2 * QB), lanes], lane_lo)
                sc = jnp.where(mask, _dot(q2, k2, NT) * 0.125, NEG)
                m = jnp.max(sc, axis=1, keepdims=True)
                p = jnp.exp(sc - m)
                den = jnp.sum(p, axis=1, keepdims=True)
                pb = p.astype(BF16)
                o = _dot(jnp.concatenate([pb[0:QB], pb[QB:]], axis=1), v2, NN)
                inv = 1.0 / den
                lse = m + jnp.log(den)
                o_ref[pl.ds(r0, QB), lanes] = o * jnp.where(lo1, inv[0:QB], inv[QB:])
                l_ref[pl.ds(r0, QB), lanes] = jnp.where(lo1, lse[0:QB], lse[QB:])
            return carry

        lax.fori_loop(0, nsub, step, 0)

    prev = lambda n: jnp.maximum(n * sub_per - 1, 0)
    cur = lambda cb: (lambda n: (n, cb))
    return pl.pallas_call(
        body, name=name, grid=(s // rb,),
        in_specs=[pl.BlockSpec((rb, DIL_W), cur(0)), pl.BlockSpec((rb, DIL_W), cur(1)), pl.BlockSpec((rb, DIL_W), cur(2)),
                  pl.BlockSpec((QB, DIL_W), lambda n: (prev(n), 1)), pl.BlockSpec((QB, DIL_W), lambda n: (prev(n), 2))],
        out_specs=[pl.BlockSpec((rb, DIL_W), cur(0)), pl.BlockSpec((rb, DIL_W), cur(0))],
        out_shape=[jax.ShapeDtypeStruct((s, DIL_W), F32), jax.ShapeDtypeStruct((s, DIL_W), F32)],
        scratch_shapes=[pltpu.VMEM((QB + rb, DIL_W), BF16), pltpu.VMEM((QB + rb, DIL_W), BF16)],
        compiler_params=_cparams(("parallel",)),
    )(qkv, qkv, qkv, qkv, qkv)


def _dil_bwd(qkv, dya, lse, dlt, tc, ts1, ts2, dil, s, name):
    length = s // dil
    rb = min(512, length)
    nsub = rb // QB
    nbs = length // rb
    last_blk = s // QB - 1

    def body(q_ref, k_ref, v_ref, kp_ref, vp_ref, qn_ref, dy_ref, dyn_ref, l_ref, ln_ref, d_ref, dn_ref,
             c_ref, s1_ref, s2_ref, o_ref, dka, dva):
        nl = pl.program_id(0) % nbs
        first, last = nl == 0, nl == nbs - 1
        ci, band, lane_lo = _pair_masks()
        lo1 = lane_lo[0:QB]

        def unrope(d, rows):
            return d * c_ref[rows, :] + pltpu.roll(d * s1_ref[rows, :], 8, 1) + pltpu.roll(d * s2_ref[rows, :], 120, 1)

        for qi in range(nsub + 1):
            nxt = qi == nsub
            rows = slice((nsub - 1) * QB, nsub * QB) if nxt else slice(qi * QB, (qi + 1) * QB)
            prev_rows = slice((qi - 1) * QB, qi * QB)
            if qi == 0:
                mask = jnp.logical_and(band, ci >= jnp.where(first, QB, 0))
            elif nxt:
                mask = jnp.logical_and(band, ci <= jnp.where(last, -1, QB - 1))[:, 0:QB]
            else:
                mask = band
            for j in range(4):
                lanes = slice(j * 128, (j + 1) * 128)
                if nxt:
                    q, do, lv, dl = qn_ref[:, lanes], dyn_ref[:, lanes], ln_ref[:, lanes], dn_ref[:, lanes]
                    k2, v2 = k_ref[prev_rows, lanes], v_ref[prev_rows, lanes]
                else:
                    q, do, lv, dl = q_ref[rows, lanes], dy_ref[rows, lanes], l_ref[rows, lanes], d_ref[rows, lanes]
                    if qi == 0:
                        k2 = jnp.concatenate([kp_ref[:, lanes], k_ref[rows, lanes]], axis=0)
                        v2 = jnp.concatenate([vp_ref[:, lanes], v_ref[rows, lanes]], axis=0)
                    else:
                        k2, v2 = k_ref[(qi - 1) * QB:(qi + 1) * QB, lanes], v_ref[(qi - 1) * QB:(qi + 1) * QB, lanes]
                q2, do2 = _stack_heads(q, lo1), _stack_heads(do, lo1)
                lse2 = jnp.concatenate([lv[:, 0:1], lv[:, 64:65]], axis=0)
                dl2 = jnp.concatenate([dl[:, 0:1], dl[:, 64:65]], axis=0)
                sc = _dot(q2, k2, NT) * 0.125
                p = jnp.where(mask, jnp.exp(jnp.minimum(sc - lse2, 0.0)), 0.0)
                ds = (p * (_dot(do2, v2, NT) - dl2) * 0.125).astype(BF16)
                dk2 = _dot(ds, q2, TN)
                dv2 = _dot(p.astype(BF16), do2, TN)
                if qi >= 1:
                    dka[prev_rows, lanes] += dk2[0:QB]
                    dva[prev_rows, lanes] += dv2[0:QB]
                if not nxt:
                    dka[rows, lanes] = dk2[QB:]
                    dva[rows, lanes] = dv2[QB:]
                    dq = _dot(jnp.concatenate([ds[0:QB], ds[QB:]], axis=1), _stack_heads(k2, lane_lo), NN)
                    o_ref[rows, lanes] = unrope(dq, rows).astype(BF16)

        for cc in range(4):
            lanes = slice(cc * 128, (cc + 1) * 128)
            o_ref[:, 512 + cc * 128:512 + (cc + 1) * 128] = unrope(dka[:, lanes], slice(None)).astype(BF16)
            o_ref[:, 1024 + cc * 128:1024 + (cc + 1) * 128] = dva[:, lanes].astype(BF16)

    prev = lambda n: jnp.maximum(n * nsub - 1, 0)
    nxt = lambda n: jnp.minimum(n * nsub + nsub, last_blk)
    cur = lambda cb: (lambda n: (n, cb))
    big = lambda cb: pl.BlockSpec((rb, DIL_W), cur(cb))
    small = lambda im: pl.BlockSpec((QB, DIL_W), im)
    tab = pl.BlockSpec((rb, 128), cur(0))
    return pl.pallas_call(
        body, name=name, grid=(s // rb,),
        in_specs=[big(0), big(1), big(2), small(lambda n: (prev(n), 1)), small(lambda n: (prev(n), 2)),
                  small(lambda n: (nxt(n), 0)), big(0), small(lambda n: (nxt(n), 0)), big(0), small(lambda n: (nxt(n), 0)),
                  big(0), small(lambda n: (nxt(n), 0)), tab, tab, tab],
        out_specs=pl.BlockSpec((rb, 3 * DIL_W), cur(0)),
        out_shape=jax.ShapeDtypeStruct((s, 3 * DIL_W), BF16),
        scratch_shapes=[pltpu.VMEM((rb, DIL_W), F32), pltpu.VMEM((rb, DIL_W), F32)],
        compiler_params=_cparams(("parallel",)),
    )(qkv, qkv, qkv, qkv, qkv, qkv, dya, dya, lse, lse, dlt, dlt, tc, ts1, ts2)


def _stream_specs(tr):
    nat = pl.BlockSpec((tr, 128), lambda i, j: (i, j))
    return [nat] + [pl.BlockSpec((dil, tr // dil, 128), lambda i, j: (0, i, j)) for dil in DIL_GROUPS[1:]]


def _dil_merge(o_g, l_g, s):
    tr = min(2048, s)
    nat, sp4, sp16 = _stream_specs(tr)

    def body(o0_ref, l0_ref, o1_ref, l1_ref, o2_ref, l2_ref, ya_ref, lse_ref, o1n, l1n, o2n, l2n):
        for src, dst, dil in ((o1_ref, o1n, 4), (l1_ref, l1n, 4), (o2_ref, o2n, 16), (l2_ref, l2n, 16)):
            for c in range(dil):
                dst[pl.ds(c, tr // dil, stride=dil), :] = src[c]
        l0, l1, l2 = l0_ref[...], l1n[...], l2n[...]
        m = jnp.maximum(jnp.maximum(l0, l1), l2)
        e0, e1, e2 = jnp.exp(l0 - m), jnp.exp(l1 - m), jnp.exp(l2 - m)
        den = e0 + e1 + e2
        ya_ref[...] = ((e0 * o0_ref[...] + e1 * o1n[...] + e2 * o2n[...]) / den).astype(BF16)
        lse_ref[...] = m + jnp.log(den)

    v3 = lambda a, dil: a.reshape(dil, s // dil, DIL_W)
    return pl.pallas_call(
        body, name="dil_merge", grid=(s // tr, 4),
        in_specs=[nat, nat, sp4, sp4, sp16, sp16], out_specs=[nat, nat],
        out_shape=[jax.ShapeDtypeStruct((s, DIL_W), BF16), jax.ShapeDtypeStruct((s, DIL_W), F32)],
        scratch_shapes=[pltpu.VMEM((tr, 128), F32)] * 4,
        compiler_params=_cparams(("parallel", "parallel")),
    )(o_g[0], l_g[0], v3(o_g[1], 4), v3(l_g[1], 4), v3(o_g[2], 16), v3(l_g[2], 16))


def _dil_bwd_prep(d_ya, ya, lse, s):
    tr = min(2048, s)
    nat, sp4, sp16 = _stream_specs(tr)

    def body(dya_ref, ya_ref, lse_ref, dy0, dl0, dy1, ls1, dl1, dy2, ls2, dl2, dlt):
        lane_lo = lax.broadcasted_iota(jnp.int32, (tr, 128), 1) < 64
        prod = dya_ref[...] * ya_ref[...].astype(F32)
        lo = jnp.where(lane_lo, prod, 0.0)
        dlt[...] = jnp.where(lane_lo, jnp.sum(lo, axis=1, keepdims=True), jnp.sum(prod - lo, axis=1, keepdims=True))
        dy0[...] = dya_ref[...].astype(BF16)
        dl0[...] = dlt[...]
        for dil, dy, ls, dl in ((4, dy1, ls1, dl1), (16, dy2, ls2, dl2)):
            for c in range(dil):
                rows = pl.ds(c, tr // dil, stride=dil)
                dy[c] = dya_ref[rows, :].astype(BF16)
                ls[c] = lse_ref[rows, :]
                dl[c] = dlt[rows, :]

    sh = lambda dil, dt: jax.ShapeDtypeStruct((dil, s // dil, DIL_W), dt)
    res = pl.pallas_call(
        body, name="dil_bwd_prep", grid=(s // tr, 4),
        in_specs=[nat, nat, nat], out_specs=[nat, nat, sp4, sp4, sp4, sp16, sp16, sp16],
        out_shape=[jax.ShapeDtypeStruct((s, DIL_W), BF16), jax.ShapeDtypeStruct((s, DIL_W), F32),
                   sh(4, BF16), sh(4, F32), sh(4, F32), sh(16, BF16), sh(16, F32), sh(16, F32)],
        scratch_shapes=[pltpu.VMEM((tr, 128), F32)],
        compiler_params=_cparams(("parallel", "parallel")),
    )(d_ya, ya, lse)
    dy0, dl0, dy1, ls1, dl1, dy2, ls2, dl2 = [r.reshape(s, DIL_W) for r in res]
    return [(dy0, lse, dl0), (dy1, ls1, dl1), (dy2, ls2, dl2)]


_RET_SEGS = ((0, 256), (1024, 256), (2048, 512), (4096, 512))


def _split_w_in(win):
    per_head = [win[a:a + RET_HEADS * n].reshape(RET_HEADS, n, D_MODEL) for a, n in _RET_SEGS]
    w_ret = jnp.concatenate(per_head, axis=1).reshape(RET_HEADS * 1536, D_MODEL)
    w_dil = win[6144:10752].reshape(3, 3, DIL_W, D_MODEL).transpose(1, 0, 2, 3).reshape(3, 3 * DIL_W, D_MODEL)
    return w_ret, win[10752:12800], [w_dil[g] for g in range(3)]


def _join_w_in(g_ret, g_gate, g_dil):
    g_ret = g_ret.reshape(RET_HEADS, 1536, D_MODEL)
    off = (0, 256, 512, 1024, 1536)
    parts = [g_ret[:, off[i]:off[i + 1]].reshape(-1, D_MODEL) for i in range(4)]
    dil = jnp.stack(g_dil).reshape(3, 3, DIL_W, D_MODEL).transpose(1, 0, 2, 3).reshape(9 * DIL_W, D_MODEL)
    return jnp.concatenate(parts + [dil, g_gate], axis=0)


def _local_step(xs, pb, tgt, tabs, wts, vec, s, late_shards=None):
    tm = min(2048, s)
    tr = min(256, s)
    mm = functools.partial(_matmul, tm=tm)
    on_mesh = late_shards is not None
    wts = dict(wts)
    w_ret, w_gate, w_dil = _split_w_in(wts["w_in"])
    blocks = lambda g: g.reshape(N_DEV, g.shape[0] // N_DEV, g.shape[1])

    u = _prenorm(xs, vec["g_pre_mix"], s)
    proj_ret = mm(u[0], w_ret, mode="nt", m=s, n=6144, k=1024, tn=1024, tk=1024, out_dtype=BF16, name="inproj_ret")
    proj_gate = mm(u[0], w_gate, mode="nt", m=s, n=2048, k=1024, tn=1024, tk=1024, out_dtype=BF16, name="inproj_gate")
    qkv = [_matmul(u[g], w_dil[g], mode="nt", m=s, n=1536, k=1024, tm=min(1024, s), tn=1536, tk=1024, out_dtype=BF16,
                   name="inproj_dil%d" % g, epi=tabs["dil"][g], epi_width=128, epi_fn=_rope_qk) for g in range(3)]

    names = list(late_shards) if on_mesh else []
    gather = _Exchange([late_shards[n] for n in names], [False] * len(names)) if on_mesh else None
    (yr, y_ret, rstate), gathered = _ret_fwd(proj_ret, tabs["cos_r"], tabs["sin_r"], s, carry=gather)
    wts.update({n: g.reshape(N_DEV * g.shape[1], g.shape[2]) for n, g in zip(names, gathered)})
    a_br = mm(yr, wts["w_ret_out"], mode="nn", m=s, n=1024, k=2048, tn=1024, tk=1024, out_dtype=BF16, name="ret_out")

    o_g, l_g = [], []
    for g, dil in enumerate(DIL_GROUPS):
        o, l = _dil_fwd(qkv[g], dil, s, "dil_fwd%d" % g)
        o_g.append(o)
        l_g.append(l)
    ya, lse = _dil_merge(o_g, l_g, s)
    b_br = mm(ya, wts["w_dil_out"], mode="nt", m=s, n=1024, k=512, tn=1024, tk=512, out_dtype=BF16, name="dil_out")

    def gate_mix(a, b, gr, ga, b0, b1):
        return [_sigmoid(gr.astype(F32) + b0) * a.astype(F32) + _sigmoid(ga.astype(F32) + b1) * b.astype(F32)], []

    (mixed,), _ = _rowwise("gate_mix", gate_mix, s, tr, [(a_br, 1024, 0), (b_br, 1024, 0), (proj_gate, 1024, 0), (proj_gate, 1024, 1)],
                           [vec["b0"], vec["b1"]], [(1024, BF16)])
    z = mm(mixed, wts["w_o"], mode="nn", m=s, n=1024, k=1024, tn=1024, tk=1024, out_dtype=BF16, name="w_o")

    def post_norm(h, f, g_post, g_pre):
        hn = h + _rms(f) * g_post
        return [hn, _rms(hn) * g_pre], []

    (h1, v2), _ = _rowwise("post_mix", post_norm, s, tr, [(xs, 1024, 0), (z, 1024, 0)], [vec["g_post_mix"], vec["g_pre_mlp"]],
                           [(1024, F32), (1024, BF16)])
    a_up = mm(v2, wts["w_up"], mode="nt", m=s, n=4096, k=1024, tn=1024, tk=1024, out_dtype=BF16, name="mlp_up")
    f_dn = mm(a_up, wts["w_down"], mode="nn", m=s, n=1024, k=4096, tn=1024, tk=1024, out_dtype=BF16, name="mlp_down", a_fn=_relu_sq)
    (h2, t_ple), _ = _rowwise("post_mlp", post_norm, s, tr, [(h1, 1024, 0), (f_dn, 1024, 0)], [vec["g_post_mlp"], vec["g_pre_ple"]],
                              [(1024, F32), (1024, BF16)])
    gl = mm(t_ple, wts["w_ple_gate"], mode="nn", m=s, n=1024, k=1024, tn=1024, tk=1024, out_dtype=BF16, name="ple_gate")
    e_ple = mm(pb, wts["w_ple_in"], mode="nt", m=s, n=1024, k=256, tn=1024, tk=256, out_dtype=BF16, name="ple_in")

    def ple_loss(h, glv, e, tg, b, g):
        gate = _sigmoid(glv + b)
        ge = gate * e
        diff = h + _rms(ge) * g - tg
        dy = diff * (1.0 / D_MODEL)
        d_ge, dg = _rms_bwd(ge, g, dy)
        d_gl = d_ge * e * gate * (1.0 - gate)
        loss = jnp.zeros((1, D_MODEL), F32) + 0.5 * jnp.sum(diff * diff) * (1.0 / D_MODEL)
        return [dy, d_gl, d_ge * gate], [_colsum(dg), _colsum(d_gl), loss]

    (dy, d_gl, d_e), (dg_post_ple, db_ple, loss) = _rowwise(
        "ple_loss", ple_loss, s, tr, [(h2, 1024, 0), (gl, 1024, 0), (e_ple, 1024, 0), (tgt, 1024, 0)],
        [vec["b_ple"], vec["g_post_ple"]], [(1024, F32), (1024, BF16), (1024, BF16)], [1024, 1024, 1024])

    ts = min(1024, s)
    wg = functools.partial(_matmul, mode="tn", k=s, tk=ts, out_dtype=BF16)
    grads = {}
    grads["w_ple_in"] = wg(d_e, pb, m=1024, n=256, tm=1024, tn=256, name="g_ple_in")
    grads["w_ple_gate"] = wg(t_ple, d_gl, m=1024, n=1024, tm=1024, tn=1024, name="g_ple_gate")
    d_t = mm(d_gl, wts["w_ple_gate"], mode="nt", m=s, n=1024, k=1024, tn=1024, tk=1024, out_dtype=BF16, name="d_t")

    def bwd_ple_mlp(h, dt, dyv, f, g_pre, g_post):
        dx, dg1 = _rms_bwd(h, g_pre, dt)
        dh = dyv + dx
        df, dg2 = _rms_bwd(f, g_post, dh)
        return [dh, df], [_colsum(dg1), _colsum(dg2)]

    (d_h2, d_f), (dg_pre_ple, dg_post_mlp) = _rowwise(
        "bwd_ple_mlp", bwd_ple_mlp, s, tr, [(h2, 1024, 0), (d_t, 1024, 0), (dy, 1024, 0), (f_dn, 1024, 0)],
        [vec["g_pre_ple"], vec["g_post_mlp"]], [(1024, F32), (1024, BF16)], [1024, 1024])
    d_a = mm(d_f, wts["w_down"], mode="nt", m=s, n=4096, k=1024, tn=1024, tk=1024, out_dtype=BF16, name="d_a",
             epi=(a_up,), epi_fn=lambda acc, av: acc * (2.0 * jnp.maximum(av.astype(F32), 0.0)))
    grads["w_down"] = wg(a_up, d_f, m=4096, n=1024, tm=2048, tn=1024, name="g_down", a_fn=_relu_sq)
    grads["w_up"] = wg(d_a, v2, m=4096, n=1024, tm=2048, tn=1024, name="g_up")
    d_v2 = mm(d_a, wts["w_up"], mode="nn", m=s, n=1024, k=4096, tn=1024, tk=1024, out_dtype=BF16, name="d_v2")

    (d_h1, d_z), (dg_pre_mlp, dg_post_mix) = _rowwise(
        "bwd_mlp_mix", bwd_ple_mlp, s, tr, [(h1, 1024, 0), (d_v2, 1024, 0), (d_h2, 1024, 0), (z, 1024, 0)],
        [vec["g_pre_mlp"], vec["g_post_mix"]], [(1024, F32), (1024, BF16)], [1024, 1024])
    d_mixed = mm(d_z, wts["w_o"], mode="nt", m=s, n=1024, k=1024, tn=1024, tk=1024, out_dtype=BF16, name="d_mixed")
    grads["w_o"] = wg(mixed, d_z, m=1024, n=1024, tm=1024, tn=1024, name="g_o")

    def bwd_gate(dm, a, b, gr, ga, b0, b1):
        sa, sb = _sigmoid(gr.astype(F32) + b0), _sigmoid(ga.astype(F32) + b1)
        dgr = dm * a.astype(F32) * sa * (1.0 - sa)
        dga = dm * b.astype(F32) * sb * (1.0 - sb)
        return [dm * sa, dm * sb, jnp.concatenate([dgr, dga], axis=1)], [_colsum(dgr), _colsum(dga)]

    (d_abr, d_bbr, dproj_gate), (db0, db1) = _rowwise(
        "bwd_gate", bwd_gate, s, tr, [(d_mixed, 1024, 0), (a_br, 1024, 0), (b_br, 1024, 0), (proj_gate, 1024, 0), (proj_gate, 1024, 1)],
        [vec["b0"], vec["b1"]], [(1024, BF16), (1024, BF16), (2048, BF16)], [1024, 1024])
    grads["w_ret_out"] = wg(yr, d_abr, m=2048, n=1024, tm=2048, tn=1024, name="g_ret_out")
    d_yr = mm(d_abr, wts["w_ret_out"], mode="nt", m=s, n=2048, k=1024, tn=1024, tk=1024, out_dtype=BF16, name="d_yr")
    grads["w_dil_out"] = wg(d_bbr, ya, m=1024, n=512, tm=1024, tn=512, name="g_dil_out")
    d_ya = mm(d_bbr, wts["w_dil_out"], mode="nn", m=s, n=512, k=1024, tn=512, tk=1024, out_dtype=F32, name="d_ya")

    slots = {}
    names = list(grads) if on_mesh else []
    shares = _Exchange([blocks(grads[n]) for n in names], [True] * len(names)) if on_mesh else None
    (dproj_ret,), got = _ret_bwd(proj_ret, tabs["cos_r"], tabs["sin_r"], y_ret, d_yr, rstate, s, carry=shares)
    slots.update(zip(names, got))
    upstream = _dil_bwd_prep(d_ya, ya, lse, s)
    dqkv = [_dil_bwd(qkv[g], *upstream[g], *tabs["dil"][g], dil, s, "dil_bwd%d" % g)
            for g, dil in enumerate(DIL_GROUPS)]

    g_ret = wg(dproj_ret, u[0], m=6144, n=1024, tm=2048, tn=1024, name="g_in_ret")
    g_gate = wg(dproj_gate, u[0], m=2048, n=1024, tm=2048, tn=1024, name="g_in_gate")
    g_dil = [wg(dqkv[g], u[g], m=1536, n=1024, tm=1536, tn=1024, name="g_in_dil%d" % g) for g in range(3)]
    grads["w_in"] = _join_w_in(g_ret, g_gate, g_dil)

    du_ret = functools.partial(mm, dproj_ret, w_ret, mode="nn", m=s, n=1024, k=6144, tn=1024, tk=1024, out_dtype=BF16, name="du_ret")
    if on_mesh:
        du_ret, (slots["w_in"],) = du_ret(carry=_Exchange([blocks(grads["w_in"])], [True]))
    else:
        du_ret = du_ret()
    du_gate = mm(dproj_gate, w_gate, mode="nn", m=s, n=1024, k=2048, tn=1024, tk=1024, out_dtype=BF16, name="du_gate")
    du_dil = [mm(dqkv[g], w_dil[g], mode="nn", m=s, n=1024, k=1536, tn=1024, tk=1536, out_dtype=BF16, name="du_dil%d" % g)
              for g in range(3)]

    grad_x, dg_pre_mix = _grad_x(xs, d_h1, (du_ret, du_gate, du_dil[0]), du_dil[1], du_dil[2], vec["g_pre_mix"], s)

    zero = jnp.zeros((1, D_MODEL), F32)
    packet = jnp.concatenate([dg_pre_mix, dg_post_mix, dg_pre_mlp, dg_post_mlp, dg_pre_ple, db_ple, dg_post_ple, loss,
                              db0, db1] + [zero] * 6, axis=0)
    return grad_x, (slots if on_mesh else grads), packet


def _mesh_pos():
    return lax.axis_index("x"), lax.axis_index("y"), lax.axis_index("c")


def _all_gather(shards):
    nw = len(shards)

    def body(*refs):
        ins, outs = refs[:nw], refs[nw:2 * nw]
        send_sems, recv_sems, local_sems = refs[2 * nw:]
        x, y, c = _mesh_pos()
        me, sibling = (x, y, c), (x, y, 1 - c)
        chips = [(1 - x, y), (x, 1 - y), (1 - x, 1 - y)]

        def region(w, dev):
            return outs[w].at[4 * dev[0] + 2 * dev[1] + dev[2]]

        def copy(w, kk, block, to, src=None):
            return pltpu.make_async_remote_copy(
                src_ref=region(w, block) if src is None else src, dst_ref=region(w, block),
                send_sem=send_sems.at[w * 7 + kk], recv_sem=recv_sems.at[w * 7 + kk], device_id=to, device_id_type=MESH)

        mine = [pltpu.make_async_copy(ins[w], region(w, me), local_sems.at[w]) for w in range(nw)]
        for cp in mine:
            cp.start()
        first = []
        for w in range(nw):
            first.append(copy(w, 0, me, sibling, src=ins[w]))
            first += [copy(w, 1 + j, me, (*chip, c), src=ins[w]) for j, chip in enumerate(chips)]
        for cp in first:
            cp.start()
        passed = []
        for j, chip in enumerate(chips):
            for w in range(nw):
                copy(w, 1 + j, (*chip, c), me).wait_recv()
                cp = copy(w, 4 + j, (*chip, c), sibling)
                cp.start()
                passed.append(cp)
        for w in range(nw):
            copy(w, 0, sibling, me).wait_recv()
            for j, chip in enumerate(chips):
                copy(w, 4 + j, (*chip, 1 - c), me).wait_recv()
        for cp in first + passed:
            cp.wait_send()
        for cp in mine:
            cp.wait()

    hbm = pl.BlockSpec(memory_space=pl.ANY)
    return pl.pallas_call(
        body, name="gather_weights",
        in_specs=[hbm] * nw, out_specs=[hbm] * nw,
        out_shape=[jax.ShapeDtypeStruct((N_DEV,) + sh.shape, sh.dtype) for sh in shards],
        scratch_shapes=[pltpu.SemaphoreType.DMA((nw * 7,)), pltpu.SemaphoreType.DMA((nw * 7,)), pltpu.SemaphoreType.DMA((nw,))],
    )(*shards)


class _Exchange:
    def __init__(self, arrays, scatter):
        self.arrays, self.scatter, self.n = list(arrays), list(scatter), len(arrays)
        self.out_shape = [jax.ShapeDtypeStruct(a.shape if sc else (N_DEV,) + a.shape, a.dtype)
                          for a, sc in zip(self.arrays, self.scatter)]
        self.scratch = [pltpu.SemaphoreType.DMA((self.n * 7,)), pltpu.SemaphoreType.DMA((self.n * 7,)),
                        pltpu.SemaphoreType.DMA((self.n,))]
        self.specs = [pl.BlockSpec(memory_space=pl.ANY)] * self.n

    def _copies(self, srcs, dsts, sems):
        send_sems, recv_sems, local_sems = sems
        x, y, c = _mesh_pos()
        my = 4 * x + 2 * y + c
        src_of = lambda w, idx: srcs[w].at[idx] if self.scatter[w] else srcs[w]
        local = [pltpu.make_async_copy(src_of(w, my), dsts[w].at[my], local_sems.at[w]) for w in range(self.n)]
        sends, recvs = [], []
        for w in range(self.n):
            for r in range(1, N_DEV):
                px = 1 - x if r & 4 else x
                py = 1 - y if r & 2 else y
                pc = 1 - c if r & 1 else c
                pidx = 4 * px + 2 * py + pc
                kw = dict(send_sem=send_sems.at[w * 7 + r - 1], recv_sem=recv_sems.at[w * 7 + r - 1],
                          device_id=(px, py, pc), device_id_type=MESH)
                sends.append(pltpu.make_async_remote_copy(src_ref=src_of(w, pidx), dst_ref=dsts[w].at[my], **kw))
                recvs.append(pltpu.make_async_remote_copy(src_ref=src_of(w, pidx), dst_ref=dsts[w].at[pidx], **kw))
        return local, sends, recvs

    def start(self, srcs, dsts, sems):
        local, sends, _ = self._copies(srcs, dsts, sems)
        for cp in local + sends:
            cp.start()

    def wait(self, srcs, dsts, sems):
        local, sends, recvs = self._copies(srcs, dsts, sems)
        for cp in recvs:
            cp.wait_recv()
        for cp in sends:
            cp.wait_send()
        for cp in local:
            cp.wait()

    def split(self, refs, n_in, n_out):
        srcs = refs[n_in:n_in + self.n]
        dsts = refs[n_in + self.n + n_out:n_in + 2 * self.n + n_out]
        return srcs, dsts, refs[len(refs) - 3:]


def _run_exchange(ex, name):
    def body(*refs):
        parts = ex.split(refs, 0, 0)
        ex.start(*parts)
        ex.wait(*parts)

    return pl.pallas_call(body, name=name, in_specs=ex.specs, out_specs=ex.specs, out_shape=ex.out_shape,
                          scratch_shapes=ex.scratch)(*ex.arrays)


def _pick_rows(r, c, target_bytes):
    t = r
    while (t // 2) % 16 == 0 and t // 2 >= 16 and t * c * 4 > target_bytes:
        t //= 2
    return t


def _sum_slots(slots, name):
    ns, r, c = slots.shape
    tr = _pick_rows(r, c, 256 * 1024)

    def body(s_ref, o_ref):
        acc = s_ref[0].astype(F32)
        for kk in range(1, ns):
            acc = acc + s_ref[kk].astype(F32)
        o_ref[...] = acc

    return pl.pallas_call(
        body, name=name, grid=(r // tr,),
        in_specs=[pl.BlockSpec((ns, tr, c), lambda i: (0, i, 0))], out_specs=pl.BlockSpec((tr, c), lambda i: (i, 0)),
        out_shape=jax.ShapeDtypeStruct((r, c), F32), compiler_params=_cparams(("parallel",)),
    )(slots)


def _adamw(slots, w, m, v, name):
    ns, r, c = slots.shape
    tr = _pick_rows(r, c, 256 * 1024)

    def body(s_ref, w_ref, m_ref, v_ref, g_out, d_out, m_out, v_out):
        g = s_ref[0].astype(F32)
        for kk in range(1, ns):
            g = g + s_ref[kk].astype(F32)
        mn = ADAM_B1 * m_ref[...] + (1.0 - ADAM_B1) * g
        vn = ADAM_B2 * v_ref[...] + (1.0 - ADAM_B2) * (g * g)
        m_hat = mn / (1.0 - ADAM_B1 ** ADAM_STEP)
        v_hat = vn / (1.0 - ADAM_B2 ** ADAM_STEP)
        g_out[...] = g
        d_out[...] = -ADAM_LR * (m_hat / (jnp.sqrt(v_hat) + ADAM_EPS) + ADAM_WD * w_ref[...])
        m_out[...] = mn
        v_out[...] = vn

    blk = pl.BlockSpec((tr, c), lambda i: (i, 0))
    return pl.pallas_call(
        body, name=name, grid=(r // tr,),
        in_specs=[pl.BlockSpec((ns, tr, c), lambda i: (0, i, 0)), blk, blk, blk], out_specs=[blk] * 4,
        out_shape=[jax.ShapeDtypeStruct((r, c), F32)] * 4, compiler_params=_cparams(("parallel",)),
    )(slots, w, m, v)


def _rotary_tables(pos, s):
    posf = pos.astype(F32)
    inv_freq = 1.0 / (10000.0 ** jnp.linspace(0.0, 1.0, RET_QK // 2, dtype=F32))
    ang = posf[:, None] * inv_freq
    tabs = {"cos_r": jnp.cos(ang), "sin_r": jnp.sin(ang), "dil": []}
    freqs = 500000.0 ** (-jnp.arange(0, 16, 2, dtype=F32) / 16)
    spread = np.zeros((16, 384), np.float32)
    bias = np.zeros((1, 384), np.float32)
    for head in range(2):
        for i in range(8):
            spread[i, 64 * head + i] = spread[i, 64 * head + 8 + i] = 1.0
            spread[8 + i, 128 + 64 * head + i] = -1.0
            spread[8 + i, 256 + 64 * head + 8 + i] = 1.0
        bias[0, 64 * head + 16:64 * head + 64] = 1.0

    def expand(t, e, b):
        hi = t.astype(BF16)
        lo = (t - hi.astype(F32)).astype(BF16)
        out = _dot(hi, e, NN) + _dot(lo, e, NN) + b
        return [out[:, 0:128], out[:, 128:256], out[:, 256:384]], []

    for g, dil in enumerate(DIL_GROUPS):
        ang = posf.reshape(s // dil, dil).T.reshape(s, 1) * freqs
        cs = jnp.concatenate([jnp.cos(ang), jnp.sin(ang)], axis=1)
        t3, _ = _rowwise("rot_tables%d" % g, expand, s, min(1024, s), [(cs, 16, 0)],
                         [jnp.asarray(spread, BF16), jnp.asarray(bias)], [(128, F32)] * 3)
        tabs["dil"].append(tuple(t3))
    return tabs


_TRANSPOSED = ("w_in", "w_dil_out", "w_up", "w_ple_in")
_MATS = ("w_in", "w_ret_out", "w_dil_out", "w_o", "w_up", "w_down", "w_ple_gate", "w_ple_in")
_VECS = ("g_pre_mix", "g_post_mix", "g_pre_mlp", "g_post_mlp", "g_pre_ple", "b_ple_gate", "g_post_ple")
_ORDER = ("w_in", "b_gate", "w_ret_out", "w_dil_out", "w_o", "g_pre_mix", "g_post_mix", "g_pre_mlp", "g_post_mlp", "w_up",
          "w_down", "g_pre_ple", "w_ple_gate", "b_ple_gate", "w_ple_in", "g_post_ple")


def kernel(x, p, positions, w_in, b_gate, w_ret_out, w_dil_out, w_o, g_pre_mix, g_post_mix, g_pre_mlp, g_post_mlp, w_up, w_down, g_pre_ple, w_ple_gate, b_ple_gate, w_ple_in, g_post_ple, loss_target, m_w_in, m_b_gate, m_w_ret_out, m_w_dil_out, m_w_o, m_g_pre_mix, m_g_post_mix, m_g_pre_mlp, m_g_post_mlp, m_w_up, m_w_down, m_g_pre_ple, m_w_ple_gate, m_b_ple_gate, m_w_ple_in, m_g_post_ple, v_w_in, v_b_gate, v_w_ret_out, v_w_dil_out, v_w_o, v_g_pre_mix, v_g_post_mix, v_g_pre_mlp, v_g_post_mlp, v_w_up, v_w_down, v_g_pre_ple, v_w_ple_gate, v_b_ple_gate, v_w_ple_in, v_g_post_ple):
    s = x.shape[1]
    wd = dict(w_in=w_in, b_gate=b_gate, w_ret_out=w_ret_out, w_dil_out=w_dil_out, w_o=w_o, g_pre_mix=g_pre_mix,
              g_post_mix=g_post_mix, g_pre_mlp=g_pre_mlp, g_post_mlp=g_post_mlp, w_up=w_up, w_down=w_down,
              g_pre_ple=g_pre_ple, w_ple_gate=w_ple_gate, b_ple_gate=b_ple_gate, w_ple_in=w_ple_in, g_post_ple=g_post_ple)
    md = dict(w_in=m_w_in, b_gate=m_b_gate, w_ret_out=m_w_ret_out, w_dil_out=m_w_dil_out, w_o=m_w_o, g_pre_mix=m_g_pre_mix,
              g_post_mix=m_g_post_mix, g_pre_mlp=m_g_pre_mlp, g_post_mlp=m_g_post_mlp, w_up=m_w_up, w_down=m_w_down,
              g_pre_ple=m_g_pre_ple, w_ple_gate=m_w_ple_gate, b_ple_gate=m_b_ple_gate, w_ple_in=m_w_ple_in, g_post_ple=m_g_post_ple)
    vd = dict(w_in=v_w_in, b_gate=v_b_gate, w_ret_out=v_w_ret_out, w_dil_out=v_w_dil_out, w_o=v_w_o, g_pre_mix=v_g_pre_mix,
              g_post_mix=v_g_post_mix, g_pre_mlp=v_g_pre_mlp, g_post_mlp=v_g_post_mlp, w_up=v_w_up, w_down=v_w_down,
              g_pre_ple=v_g_pre_ple, w_ple_gate=v_w_ple_gate, b_ple_gate=v_b_ple_gate, w_ple_in=v_w_ple_in, g_post_ple=v_g_post_ple)

    shards = {n: (wd[n][0].T if n in _TRANSPOSED else wd[n][0]).astype(BF16) for n in _MATS}
    w_in_all, bg_all = _all_gather([shards.pop("w_in"), b_gate[0]])
    wts = {"w_in": w_in_all.reshape(N_DEV * w_in_all.shape[1], D_MODEL)}
    bg = bg_all.transpose(1, 0, 2).reshape(2, D_MODEL)
    vec = {n: wd[n] for n in _VECS}
    vec.update(b0=bg[0:1], b1=bg[1:2], b_ple=b_ple_gate)

    tabs = _rotary_tables(positions[0], s)
    grad_x, slots, packet = _local_step(x[0], p[0, 0].astype(BF16), loss_target[0], tabs, wts, vec, s, late_shards=shards)

    (packets,) = _run_exchange(_Exchange([packet], [False]), "exchange_vectors")
    out = {}
    for n in _MATS:
        sl = slots[n]
        if n in _TRANSPOSED:
            sl = _sum_slots(sl, "sum_" + n).T[None]
        out[n] = _adamw(sl, wd[n][0], md[n][0], vd[n][0], "adamw_" + n)
    zero_rows = jnp.zeros((16 - len(_VECS), D_MODEL), F32)
    pack = lambda d: jnp.concatenate([d[n] for n in _VECS] + [zero_rows], axis=0)
    small = _adamw(packets, pack(wd), pack(md), pack(vd), "adamw_vectors")
    for i, n in enumerate(_VECS):
        out[n] = tuple(t[i:i + 1] for t in small)
    my = 4 * lax.axis_index("x") + 2 * lax.axis_index("y") + lax.axis_index("c")
    g_bias = lax.dynamic_slice(small[0], (8, my * 128), (2, 128))
    out["b_gate"] = _adamw(g_bias[None], b_gate[0], m_b_gate[0], v_b_gate[0], "adamw_b_gate")
    loss = small[0][7, 0]

    res = [loss, grad_x[None]]
    for kk in range(4):
        res += [out[n][kk][None] if out[n][kk].ndim == 2 and wd[n].ndim == 3 else out[n][kk] for n in _ORDER]
    return tuple(res)
```

```python
import functools
import math

import numpy as np
import jax
import jax.numpy as jnp
from jax import lax
from jax.experimental import pallas as pl
from jax.experimental.pallas import tpu as pltpu

F32, BF16 = jnp.float32, jnp.bfloat16
D_MODEL = 1024
EPS = 1e-6
N_DEV = 8
RET_HEADS, RET_QK, RET_V, RET_CHUNK = 4, 256, 512, 128
DIL_GROUPS = (1, 4, 16)
DIL_W = 512
QB = 128
NEG = -1e30
ADAM_LR, ADAM_B1, ADAM_B2, ADAM_EPS, ADAM_WD, ADAM_STEP = 0.001, 0.9, 0.999, 1e-08, 0.01, 10
VMEM_LIMIT_BYTES = 56 * 1024 * 1024
MESH = pl.DeviceIdType.MESH

NN = ((1,), (0,))
NT = ((1,), (1,))
TN = ((0,), (0,))


def _dot(a, b, dn):
    return lax.dot_general(a, b, (dn, ((), ())), preferred_element_type=F32)


def _cparams(sem):
    return pltpu.CompilerParams(dimension_semantics=sem, vmem_limit_bytes=VMEM_LIMIT_BYTES)


def _rms(x):
    return x * lax.rsqrt(jnp.mean(x * x, axis=-1, keepdims=True) + EPS)


def _rms_bwd(x, g, dy):
    r = lax.rsqrt(jnp.mean(x * x, axis=-1, keepdims=True) + EPS)
    xh = x * r
    t = dy * g
    dx = r * (t - xh * jnp.mean(t * xh, axis=-1, keepdims=True))
    return dx, dy * xh


def _colsum(v):
    return jnp.sum(v, axis=0, keepdims=True)


def _sigmoid(v):
    return 1.0 / (1.0 + jnp.exp(-v))


def _pallas(compute, *, name, grid, in_specs, out_specs, out_shape, scratch, semantics, args, carry=None):
    n_in, n_out = len(in_specs), len(out_specs)
    if carry is None:
        res = pl.pallas_call(compute, name=name, grid=grid, in_specs=in_specs, out_specs=out_specs, out_shape=out_shape,
                             scratch_shapes=scratch, compiler_params=_cparams(semantics))(*args)
        return res, []
    n_steps = math.prod(grid)

    def body(*refs):
        step = 0
        for axis, size in enumerate(grid):
            step = step * size + pl.program_id(axis)
        parts = carry.split(refs, n_in, n_out)
        pl.when(step == 0)(lambda: carry.start(*parts))
        compute(*refs[:n_in], *refs[n_in + carry.n:n_in + carry.n + n_out], *refs[n_in + 2 * carry.n + n_out:len(refs) - 3])
        pl.when(step == n_steps - 1)(lambda: carry.wait(*parts))

    res = pl.pallas_call(
        body, name=name, grid=grid, in_specs=list(in_specs) + carry.specs, out_specs=list(out_specs) + carry.specs,
        out_shape=list(out_shape) + carry.out_shape, scratch_shapes=list(scratch) + carry.scratch,
        compiler_params=_cparams(("arbitrary",) * len(grid)))(*args, *carry.arrays)
    return res[:n_out], res[n_out:]


def _matmul(a, b, *, mode, m, n, k, tm, tn, tk, out_dtype, name, a_fn=None, epi=(), epi_width=None, epi_fn=None, carry=None):
    nk = k // tk
    grid = (m // tm, n // tn, nk)
    if mode == "nn":
        a_blk, a_im, b_blk, b_im, dn = (tm, tk), (lambda i, j, kk: (i, kk)), (tk, tn), (lambda i, j, kk: (kk, j)), NN
    elif mode == "nt":
        a_blk, a_im, b_blk, b_im, dn = (tm, tk), (lambda i, j, kk: (i, kk)), (tn, tk), (lambda i, j, kk: (j, kk)), NT
    else:
        a_blk, a_im, b_blk, b_im, dn = (tk, tm), (lambda i, j, kk: (kk, i)), (tk, tn), (lambda i, j, kk: (kk, j)), TN
    o_im = lambda i, j, kk: (i, j)
    n_in = 2 + len(epi)

    def body(*refs):
        a_ref, b_ref = refs[0], refs[1]
        o_ref = refs[n_in]
        acc_ref = refs[n_in + 1] if nk > 1 else None

        def finish(acc):
            if epi:
                acc = epi_fn(acc, *[r[...] for r in refs[2:n_in]])
            o_ref[...] = acc.astype(o_ref.dtype)

        av = a_ref[...]
        if a_fn is not None:
            av = a_fn(av)
        part = _dot(av, b_ref[...], dn)
        if nk == 1:
            finish(part)
        else:
            kk = pl.program_id(2)

            @pl.when(kk == 0)
            def _():
                acc_ref[...] = part

            @pl.when(kk > 0)
            def _():
                acc_ref[...] += part

            @pl.when(kk == nk - 1)
            def _():
                finish(acc_ref[...])

    epi_spec = pl.BlockSpec((tm, tn), o_im) if epi_width is None else pl.BlockSpec((tm, epi_width), lambda i, j, kk: (i, 0))
    in_specs = [pl.BlockSpec(a_blk, a_im), pl.BlockSpec(b_blk, b_im)] + [epi_spec] * len(epi)
    args = [a, b, *epi]
    (out,), got = _pallas(
        body, name=name, grid=grid, in_specs=in_specs, out_specs=[pl.BlockSpec((tm, tn), o_im)],
        out_shape=[jax.ShapeDtypeStruct((m, n), out_dtype)], scratch=[pltpu.VMEM((tm, tn), F32)] if nk > 1 else [],
        semantics=("parallel", "parallel", "arbitrary"), args=args, carry=carry)
    return out if carry is None else (out, got)


def _relu_sq(v):
    r = jnp.maximum(v.astype(F32), 0.0)
    return (r * r).astype(BF16)


def _rowwise(name, fn, s, tr, rows, vecs, outs, accs=()):
    n_r, n_v, n_o, n_a = len(rows), len(vecs), len(outs), len(accs)

    def body(*refs):
        vals = [refs[i][...].astype(F32) for i in range(n_r)] + [refs[n_r + i][...] for i in range(n_v)]
        o_refs = refs[n_r + n_v:n_r + n_v + n_o]
        a_refs = refs[n_r + n_v + n_o:]
        o_vals, a_vals = fn(*vals)
        for ref, val in zip(o_refs, o_vals):
            ref[...] = val.astype(ref.dtype)
        if n_a:
            @pl.when(pl.program_id(0) == 0)
            def _():
                for ref in a_refs:
                    ref[...] = jnp.zeros_like(ref)

            for ref, val in zip(a_refs, a_vals):
                ref[...] += val

    in_specs = [pl.BlockSpec((tr, w), functools.partial(lambda i, cb: (i, cb), cb=cb)) for _, w, cb in rows]
    in_specs += [pl.BlockSpec(v.shape, lambda i: (0, 0)) for v in vecs]
    out_specs = [pl.BlockSpec((tr, w), lambda i: (i, 0)) for w, _ in outs]
    out_specs += [pl.BlockSpec((1, w), lambda i: (0, 0)) for w in accs]
    out_shape = [jax.ShapeDtypeStruct((s, w), dt) for w, dt in outs]
    out_shape += [jax.ShapeDtypeStruct((1, w), F32) for w in accs]
    res = pl.pallas_call(
        body, name=name, grid=(s // tr,), in_specs=in_specs, out_specs=out_specs, out_shape=out_shape,
        compiler_params=_cparams(("arbitrary",)),
    )(*[r[0] for r in rows], *vecs)
    return res[:n_o], res[n_o:]


_ROW_TILE = 256
_STREAM_SPECS = [pl.BlockSpec((dil, _ROW_TILE // dil, D_MODEL), lambda i: (0, i, 0)) for dil in DIL_GROUPS[1:]]
_NAT_SPEC = pl.BlockSpec((_ROW_TILE, D_MODEL), lambda i: (i, 0))
_VEC_SPEC = pl.BlockSpec((1, D_MODEL), lambda i: (0, 0))
_COL_BLOCKS = pltpu.VMEM((D_MODEL // 128, _ROW_TILE, 128), F32)


def _prenorm(xs, g, s):
    tr = _ROW_TILE

    def body(x_ref, g_ref, u_ref, u4_ref, u16_ref, buf):
        xn = _rms(x_ref[...]) * g_ref[...]
        u_ref[...] = xn.astype(BF16)
        for cb in range(8):
            buf[cb] = xn[:, cb * 128:(cb + 1) * 128]
        for dil, out in ((4, u4_ref), (16, u16_ref)):
            for c in range(dil):
                rows = pl.ds(c, tr // dil, stride=dil)
                out[c] = jnp.concatenate([buf.at[cb][rows, :] for cb in range(8)], axis=1).astype(BF16)

    res = pl.pallas_call(
        body, name="prenorm", grid=(s // tr,), in_specs=[_NAT_SPEC, _VEC_SPEC], out_specs=[_NAT_SPEC] + _STREAM_SPECS,
        out_shape=[jax.ShapeDtypeStruct((s, D_MODEL), BF16)]
        + [jax.ShapeDtypeStruct((dil, s // dil, D_MODEL), BF16) for dil in DIL_GROUPS[1:]],
        scratch_shapes=[_COL_BLOCKS], compiler_params=_cparams(("parallel",)),
    )(xs, g)
    return [r.reshape(s, D_MODEL) for r in res]


def _grad_x(xs, d_h1, du_nat, du4, du16, g, s):
    tr = _ROW_TILE

    def body(x_ref, dh_ref, a_ref, b_ref, c_ref, u4_ref, u16_ref, g_ref, dx_ref, dg_ref, buf):
        du = a_ref[...].astype(F32) + b_ref[...].astype(F32) + c_ref[...].astype(F32)
        for dil, src in ((4, u4_ref), (16, u16_ref)):
            for c in range(dil):
                part = src[c].astype(F32)
                for cb in range(8):
                    buf.at[cb][pl.ds(c, tr // dil, stride=dil), :] = part[:, cb * 128:(cb + 1) * 128]
            du = du + jnp.concatenate([buf[cb] for cb in range(8)], axis=1)
        dx, dgr = _rms_bwd(x_ref[...], g_ref[...], du)
        dx_ref[...] = dh_ref[...] + dx

        @pl.when(pl.program_id(0) == 0)
        def _():
            dg_ref[...] = jnp.zeros_like(dg_ref)

        dg_ref[...] += _colsum(dgr)

    return pl.pallas_call(
        body, name="grad_x", grid=(s // tr,), in_specs=[_NAT_SPEC] * 5 + _STREAM_SPECS + [_VEC_SPEC],
        out_specs=[_NAT_SPEC, _VEC_SPEC],
        out_shape=[jax.ShapeDtypeStruct((s, D_MODEL), F32), jax.ShapeDtypeStruct((1, D_MODEL), F32)],
        scratch_shapes=[_COL_BLOCKS], compiler_params=_cparams(("arbitrary",)),
    )(xs, d_h1, *du_nat, du4.reshape(4, s // 4, D_MODEL), du16.reshape(16, s // 16, D_MODEL), g)


def _ret_tables():
    h = np.arange(RET_HEADS, dtype=np.float32)
    lg = np.log1p(-(np.float32(2.0) ** (-5.0 - h))).astype(np.float32)
    idx = np.arange(RET_CHUNK, dtype=np.float32)
    diff = idx[:, None] - idx[None, :]
    dm = np.where(diff[None] >= 0, np.exp(np.maximum(diff, 0.0)[None] * lg[:, None, None]), 0.0)
    qd = np.exp((idx + 1.0)[None, :, None] * lg[:, None, None])
    kd = np.exp((RET_CHUNK - 1.0 - idx)[None, :, None] * lg[:, None, None])
    cd = np.exp(RET_CHUNK * lg)[:, None, None]
    return [jnp.asarray(t, F32) for t in (dm, qd, kd, cd)]


def _rope_half(v, cos, sin):
    v1, v2 = v[:, :128], v[:, 128:]
    return jnp.concatenate([v1 * cos - v2 * sin, v2 * cos + v1 * sin], axis=1)


def _unrope_half(d, cos, sin):
    d1, d2 = d[:, :128], d[:, 128:]
    return jnp.concatenate([d1 * cos + d2 * sin, d2 * cos - d1 * sin], axis=1)


_RET_HEADS_FWD, _RET_HEADS_BWD = 1, 2


def _ret_specs(rb, rev_n, hp):
    def rowmap(w_blk):
        return lambda h, n: (rev_n(n), w_blk(h))
    tab = [pl.BlockSpec((hp, RET_CHUNK, RET_CHUNK), lambda h, n: (h, 0, 0)),
           pl.BlockSpec((hp, RET_CHUNK, 1), lambda h, n: (h, 0, 0)),
           pl.BlockSpec((hp, RET_CHUNK, 1), lambda h, n: (h, 0, 0)),
           pl.BlockSpec((hp, 1, 1), lambda h, n: (h, 0, 0))]
    proj = pl.BlockSpec((rb, hp * 1536), rowmap(lambda h: h))
    cs = pl.BlockSpec((rb, 128), rowmap(lambda h: 0))
    hv = pl.BlockSpec((rb, hp * RET_V), rowmap(lambda h: h))
    return proj, cs, hv, tab


def _ret_fwd(proj_ret, cos, sin, s, carry=None):
    rb = min(512, s)
    ch = rb // RET_CHUNK
    nb = s // rb
    hp = _RET_HEADS_FWD
    proj_spec, cs_spec, hv_spec, tab_specs = _ret_specs(rb, lambda n: n, hp)

    def body(p_ref, cos_ref, sin_ref, dm_ref, qd_ref, kd_ref, cd_ref, yr_ref, y_ref, rs_ref, r_acc):
        @pl.when(pl.program_id(1) == 0)
        def _():
            r_acc[...] = jnp.zeros_like(r_acc)

        for c, hh in [(c, hh) for c in range(ch) for hh in range(hp)]:
            rows = slice(c * RET_CHUNK, (c + 1) * RET_CHUNK)
            pc, hc = hh * 1536, hh * RET_V
            dm, qd, kd, cd = dm_ref[hh], qd_ref[hh], kd_ref[hh], cd_ref[hh]
            cosv, sinv = cos_ref[rows, :], sin_ref[rows, :]
            q = _rope_half(p_ref[rows, pc:pc + 256].astype(F32), cosv, sinv)
            kk = _rope_half(p_ref[rows, pc + 256:pc + 512].astype(F32), cosv, sinv) * (RET_QK ** -0.5)
            v = p_ref[rows, pc + 512:pc + 1024]
            g = p_ref[rows, pc + 1024:pc + 1536].astype(F32)
            rb16 = r_acc[hh].astype(BF16)
            rs_ref[hh, c] = rb16
            sc = _dot(q.astype(BF16), kk.astype(BF16), NT) * dm
            y = _dot(sc.astype(BF16), v, NN) + _dot((q * qd).astype(BF16), rb16, NN)
            r_acc[hh] = r_acc[hh] * cd + _dot((kk * kd).astype(BF16), v, TN)
            y_ref[rows, hc:hc + RET_V] = y.astype(BF16)
            yr_ref[rows, hc:hc + RET_V] = (_rms(y) * (g * _sigmoid(g))).astype(BF16)

    return _pallas(
        body, name="ret_fwd", grid=(RET_HEADS // hp, nb),
        in_specs=[proj_spec, cs_spec, cs_spec] + tab_specs,
        out_specs=[hv_spec, hv_spec, pl.BlockSpec((hp, ch, RET_QK, RET_V), lambda h, n: (h, n, 0, 0))],
        out_shape=[jax.ShapeDtypeStruct((s, RET_HEADS * RET_V), BF16), jax.ShapeDtypeStruct((s, RET_HEADS * RET_V), BF16),
                   jax.ShapeDtypeStruct((RET_HEADS, s // RET_CHUNK, RET_QK, RET_V), BF16)],
        scratch=[pltpu.VMEM((hp, RET_QK, RET_V), F32)], semantics=("parallel", "arbitrary"),
        args=(proj_ret, cos, sin, *_ret_tables()), carry=carry)


def _ret_bwd(proj_ret, cos, sin, y, d_yr, rs, s, carry=None):
    rb = min(512, s)
    ch = rb // RET_CHUNK
    nb = s // rb
    hp = _RET_HEADS_BWD
    proj_spec, cs_spec, hv_spec, tab_specs = _ret_specs(rb, lambda n: nb - 1 - n, hp)

    def body(p_ref, cos_ref, sin_ref, y_ref, dyr_ref, rs_ref, dm_ref, qd_ref, kd_ref, cd_ref, o_ref, dr_acc):
        @pl.when(pl.program_id(1) == 0)
        def _():
            dr_acc[...] = jnp.zeros_like(dr_acc)

        for c, hh in [(c, hh) for c in reversed(range(ch)) for hh in range(hp)]:
            rows = slice(c * RET_CHUNK, (c + 1) * RET_CHUNK)
            pc, hc = hh * 1536, hh * RET_V
            dm, qd, kd, cd = dm_ref[hh], qd_ref[hh], kd_ref[hh], cd_ref[hh]
            cosv, sinv = cos_ref[rows, :], sin_ref[rows, :]
            q = _rope_half(p_ref[rows, pc:pc + 256].astype(F32), cosv, sinv)
            kk = _rope_half(p_ref[rows, pc + 256:pc + 512].astype(F32), cosv, sinv) * (RET_QK ** -0.5)
            v = p_ref[rows, pc + 512:pc + 1024]
            g = p_ref[rows, pc + 1024:pc + 1536].astype(F32)
            yv = y_ref[rows, hc:hc + RET_V].astype(F32)
            dyr = dyr_ref[rows, hc:hc + RET_V].astype(F32)
            sg = _sigmoid(g)
            r = lax.rsqrt(jnp.mean(yv * yv, axis=-1, keepdims=True) + EPS)
            yn = yv * r
            dg = dyr * yn * (sg * (1.0 + g * (1.0 - sg)))
            dyn = dyr * (g * sg)
            dy = (r * (dyn - yn * jnp.mean(dyn * yn, axis=-1, keepdims=True))).astype(BF16)
            qb, kb = q.astype(BF16), kk.astype(BF16)
            rb16 = rs_ref[hh, c]
            drb = dr_acc[hh].astype(BF16)
            sd = _dot(qb, kb, NT) * dm
            ds = (_dot(dy, v, NT) * dm).astype(BF16)
            dq = _dot(ds, kb, NN) + qd * _dot(dy, rb16, NT)
            dk = _dot(ds, qb, TN) + kd * _dot(v, drb, NT)
            dv = _dot(sd.astype(BF16), dy, TN) + _dot((kk * kd).astype(BF16), drb, NN)
            dr_acc[hh] = dr_acc[hh] * cd + _dot((q * qd).astype(BF16), dy, TN)
            o_ref[rows, pc:pc + 256] = _unrope_half(dq, cosv, sinv).astype(BF16)
            o_ref[rows, pc + 256:pc + 512] = (_unrope_half(dk, cosv, sinv) * (RET_QK ** -0.5)).astype(BF16)
            o_ref[rows, pc + 512:pc + 1024] = dv.astype(BF16)
            o_ref[rows, pc + 1024:pc + 1536] = dg.astype(BF16)

    in_specs = [proj_spec, cs_spec, cs_spec, hv_spec, hv_spec,
                pl.BlockSpec((hp, ch, RET_QK, RET_V), lambda h, n: (h, nb - 1 - n, 0, 0))] + tab_specs
    return _pallas(
        body, name="ret_bwd", grid=(RET_HEADS // hp, nb), in_specs=in_specs, out_specs=[proj_spec],
        out_shape=[jax.ShapeDtypeStruct((s, RET_HEADS * 1536), BF16)], scratch=[pltpu.VMEM((hp, RET_QK, RET_V), F32)],
        semantics=("parallel", "arbitrary"), args=(proj_ret, cos, sin, y, d_yr, rs, *_ret_tables()), carry=carry)


def _rope_qk(acc, c, s1, s2):
    outs = []
    for cc in range(8):
        vv = acc[:, cc * 128:(cc + 1) * 128]
        outs.append(vv * c + pltpu.roll(vv, 120, 1) * s1 + pltpu.roll(vv, 8, 1) * s2)
    return jnp.concatenate(outs + [acc[:, 2 * DIL_W:]], axis=1)


def _pair_masks():
    ri = lax.broadcasted_iota(jnp.int32, (2 * QB, 2 * QB), 0)
    ci = lax.broadcasted_iota(jnp.int32, (2 * QB, 2 * QB), 1)
    e = ci - (ri & (QB - 1))
    lane_lo = lax.broadcasted_iota(jnp.int32, (2 * QB, 128), 1) < 64
    return ci, jnp.logical_and(e >= 0, e <= QB), lane_lo


def _stack_heads(v, lane_lo):
    z = jnp.zeros_like(v)
    return jnp.concatenate([jnp.where(lane_lo, v, z), jnp.where(lane_lo, z, v)], axis=0)


def _dil_fwd(qkv, dil, s, name):
    length = s // dil
    rb = min(512, length)
    nsub = rb // QB
    nbs = length // rb
    sub_per = rb // QB

    def body(q_ref, k_ref, v_ref, kp_ref, vp_ref, o_ref, l_ref):
        first = (pl.program_id(0) % nbs) == 0
        ci, band, lane_lo = _pair_masks()
        lo1 = lane_lo[0:QB]

        for i in range(nsub):
            rows = slice(i * QB, (i + 1) * QB)
            mask = jnp.logical_and(band, ci >= jnp.where(first, QB, 0)) if i == 0 else band
            for j in range(4):
                lanes = slice(j * 128, (j + 1) * 128)
                q2 = _stack_heads(q_ref[rows, lanes], lo1)
                if i == 0:
                    k2 = jnp.concatenate([kp_ref[:, lanes], k_ref[rows, lanes]], axis=0)
                    v2 = jnp.concatenate([vp_ref[:, lanes], v_ref[rows, lanes]], axis=0)
                else:
                    k2, v2 = k_ref[(i - 1) * QB:(i + 1) * QB, lanes], v_ref[(i - 1) * QB:(i + 1) * QB, lanes]
                v2 = _stack_heads(v2, lane_lo)
                sc = jnp.where(mask, _dot(q2, k2, NT) * 0.125, NEG)
                m = jnp.max(sc, axis=1, keepdims=True)
                p = jnp.exp(sc - m)
                den = jnp.sum(p, axis=1, keepdims=True)
                pb = p.astype(BF16)
                o = _dot(jnp.concatenate([pb[0:QB], pb[QB:]], axis=1), v2, NN)
                inv = 1.0 / den
                lse = m + jnp.log(den)
                o_ref[rows, lanes] = o * jnp.where(lo1, inv[0:QB], inv[QB:])
                l_ref[rows, lanes] = jnp.where(lo1, lse[0:QB], lse[QB:])

    prev = lambda n: jnp.maximum(n * sub_per - 1, 0)
    cur = lambda cb: (lambda n: (n, cb))
    return pl.pallas_call(
        body, name=name, grid=(s // rb,),
        in_specs=[pl.BlockSpec((rb, DIL_W), cur(0)), pl.BlockSpec((rb, DIL_W), cur(1)), pl.BlockSpec((rb, DIL_W), cur(2)),
                  pl.BlockSpec((QB, DIL_W), lambda n: (prev(n), 1)), pl.BlockSpec((QB, DIL_W), lambda n: (prev(n), 2))],
        out_specs=[pl.BlockSpec((rb, DIL_W), cur(0)), pl.BlockSpec((rb, DIL_W), cur(0))],
        out_shape=[jax.ShapeDtypeStruct((s, DIL_W), F32), jax.ShapeDtypeStruct((s, DIL_W), F32)],
        compiler_params=_cparams(("parallel",)),
    )(qkv, qkv, qkv, qkv, qkv)


def _dil_bwd(qkv, dya, lse, dlt, tc, ts1, ts2, dil, s, name):
    length = s // dil
    rb = min(512, length)
    nsub = rb // QB
    nbs = length // rb
    last_blk = s // QB - 1

    def body(q_ref, k_ref, v_ref, kp_ref, vp_ref, qn_ref, dy_ref, dyn_ref, l_ref, ln_ref, d_ref, dn_ref,
             c_ref, s1_ref, s2_ref, o_ref, dka, dva):
        nl = pl.program_id(0) % nbs
        first, last = nl == 0, nl == nbs - 1
        ci, band, lane_lo = _pair_masks()
        lo1 = lane_lo[0:QB]

        def unrope(d, rows):
            return d * c_ref[rows, :] + pltpu.roll(d * s1_ref[rows, :], 8, 1) + pltpu.roll(d * s2_ref[rows, :], 120, 1)

        for qi in range(nsub + 1):
            nxt = qi == nsub
            rows = slice((nsub - 1) * QB, nsub * QB) if nxt else slice(qi * QB, (qi + 1) * QB)
            prev_rows = slice((qi - 1) * QB, qi * QB)
            if qi == 0:
                mask = jnp.logical_and(band, ci >= jnp.where(first, QB, 0))
            elif nxt:
                mask = jnp.logical_and(band, ci <= jnp.where(last, -1, QB - 1))[:, 0:QB]
            else:
                mask = band
            for j in range(4):
                lanes = slice(j * 128, (j + 1) * 128)
                if nxt:
                    q, do, lv, dl = qn_ref[:, lanes], dyn_ref[:, lanes], ln_ref[:, lanes], dn_ref[:, lanes]
                    k2, v2 = k_ref[prev_rows, lanes], v_ref[prev_rows, lanes]
                else:
                    q, do, lv, dl = q_ref[rows, lanes], dy_ref[rows, lanes], l_ref[rows, lanes], d_ref[rows, lanes]
                    if qi == 0:
                        k2 = jnp.concatenate([kp_ref[:, lanes], k_ref[rows, lanes]], axis=0)
                        v2 = jnp.concatenate([vp_ref[:, lanes], v_ref[rows, lanes]], axis=0)
                    else:
                        k2, v2 = k_ref[(qi - 1) * QB:(qi + 1) * QB, lanes], v_ref[(qi - 1) * QB:(qi + 1) * QB, lanes]
                q2, do2 = _stack_heads(q, lo1), _stack_heads(do, lo1)
                lse2 = jnp.concatenate([lv[:, 0:1], lv[:, 64:65]], axis=0)
                dl2 = jnp.concatenate([dl[:, 0:1], dl[:, 64:65]], axis=0)
                sc = _dot(q2, k2, NT) * 0.125
                p = jnp.where(mask, jnp.exp(jnp.minimum(sc - lse2, 0.0)), 0.0)
                ds = (p * (_dot(do2, v2, NT) - dl2) * 0.125).astype(BF16)
                dk2 = _dot(ds, q2, TN)
                dv2 = _dot(p.astype(BF16), do2, TN)
                if qi >= 1:
                    dka[prev_rows, lanes] += dk2[0:QB]
                    dva[prev_rows, lanes] += dv2[0:QB]
                if not nxt:
                    dka[rows, lanes] = dk2[QB:]
                    dva[rows, lanes] = dv2[QB:]
                    dq = _dot(jnp.concatenate([ds[0:QB], ds[QB:]], axis=1), _stack_heads(k2, lane_lo), NN)
                    o_ref[rows, lanes] = unrope(dq, rows).astype(BF16)

        for cc in range(4):
            lanes = slice(cc * 128, (cc + 1) * 128)
            o_ref[:, 512 + cc * 128:512 + (cc + 1) * 128] = unrope(dka[:, lanes], slice(None)).astype(BF16)
            o_ref[:, 1024 + cc * 128:1024 + (cc + 1) * 128] = dva[:, lanes].astype(BF16)

    prev = lambda n: jnp.maximum(n * nsub - 1, 0)
    nxt = lambda n: jnp.minimum(n * nsub + nsub, last_blk)
    cur = lambda cb: (lambda n: (n, cb))
    big = lambda cb: pl.BlockSpec((rb, DIL_W), cur(cb))
    small = lambda im: pl.BlockSpec((QB, DIL_W), im)
    tab = pl.BlockSpec((rb, 128), cur(0))
    return pl.pallas_call(
        body, name=name, grid=(s // rb,),
        in_specs=[big(0), big(1), big(2), small(lambda n: (prev(n), 1)), small(lambda n: (prev(n), 2)),
                  small(lambda n: (nxt(n), 0)), big(0), small(lambda n: (nxt(n), 0)), big(0), small(lambda n: (nxt(n), 0)),
                  big(0), small(lambda n: (nxt(n), 0)), tab, tab, tab],
        out_specs=pl.BlockSpec((rb, 3 * DIL_W), cur(0)),
        out_shape=jax.ShapeDtypeStruct((s, 3 * DIL_W), BF16),
        scratch_shapes=[pltpu.VMEM((rb, DIL_W), F32), pltpu.VMEM((rb, DIL_W), F32)],
        compiler_params=_cparams(("parallel",)),
    )(qkv, qkv, qkv, qkv, qkv, qkv, dya, dya, lse, lse, dlt, dlt, tc, ts1, ts2)


def _stream_specs(tr):
    nat = pl.BlockSpec((tr, 128), lambda i, j: (i, j))
    return [nat] + [pl.BlockSpec((dil, tr // dil, 128), lambda i, j: (0, i, j)) for dil in DIL_GROUPS[1:]]


def _dil_merge(o_g, l_g, s):
    tr = min(2048, s)
    nat, sp4, sp16 = _stream_specs(tr)

    def body(o0_ref, l0_ref, o1_ref, l1_ref, o2_ref, l2_ref, ya_ref, lse_ref, o1n, l1n, o2n, l2n):
        for src, dst, dil in ((o1_ref, o1n, 4), (l1_ref, l1n, 4), (o2_ref, o2n, 16), (l2_ref, l2n, 16)):
            for c in range(dil):
                dst[pl.ds(c, tr // dil, stride=dil), :] = src[c]
        l0, l1, l2 = l0_ref[...], l1n[...], l2n[...]
        m = jnp.maximum(jnp.maximum(l0, l1), l2)
        e0, e1, e2 = jnp.exp(l0 - m), jnp.exp(l1 - m), jnp.exp(l2 - m)
        den = e0 + e1 + e2
        ya_ref[...] = ((e0 * o0_ref[...] + e1 * o1n[...] + e2 * o2n[...]) / den).astype(BF16)
        lse_ref[...] = m + jnp.log(den)

    v3 = lambda a, dil: a.reshape(dil, s // dil, DIL_W)
    return pl.pallas_call(
        body, name="dil_merge", grid=(s // tr, 4),
        in_specs=[nat, nat, sp4, sp4, sp16, sp16], out_specs=[nat, nat],
        out_shape=[jax.ShapeDtypeStruct((s, DIL_W), BF16), jax.ShapeDtypeStruct((s, DIL_W), F32)],
        scratch_shapes=[pltpu.VMEM((tr, 128), F32)] * 4,
        compiler_params=_cparams(("parallel", "parallel")),
    )(o_g[0], l_g[0], v3(o_g[1], 4), v3(l_g[1], 4), v3(o_g[2], 16), v3(l_g[2], 16))


def _dil_bwd_prep(d_ya, ya, lse, s):
    tr = min(2048, s)
    nat, sp4, sp16 = _stream_specs(tr)

    def body(dya_ref, ya_ref, lse_ref, dy0, dl0, dy1, ls1, dl1, dy2, ls2, dl2, dlt):
        lane_lo = lax.broadcasted_iota(jnp.int32, (tr, 128), 1) < 64
        prod = dya_ref[...] * ya_ref[...].astype(F32)
        lo = jnp.where(lane_lo, prod, 0.0)
        dlt[...] = jnp.where(lane_lo, jnp.sum(lo, axis=1, keepdims=True), jnp.sum(prod - lo, axis=1, keepdims=True))
        dy0[...] = dya_ref[...].astype(BF16)
        dl0[...] = dlt[...]
        for dil, dy, ls, dl in ((4, dy1, ls1, dl1), (16, dy2, ls2, dl2)):
            for c in range(dil):
                rows = pl.ds(c, tr // dil, stride=dil)
                dy[c] = dya_ref[rows, :].astype(BF16)
                ls[c] = lse_ref[rows, :]
                dl[c] = dlt[rows, :]

    sh = lambda dil, dt: jax.ShapeDtypeStruct((dil, s // dil, DIL_W), dt)
    res = pl.pallas_call(
        body, name="dil_bwd_prep", grid=(s // tr, 4),
        in_specs=[nat, nat, nat], out_specs=[nat, nat, sp4, sp4, sp4, sp16, sp16, sp16],
        out_shape=[jax.ShapeDtypeStruct((s, DIL_W), BF16), jax.ShapeDtypeStruct((s, DIL_W), F32),
                   sh(4, BF16), sh(4, F32), sh(4, F32), sh(16, BF16), sh(16, F32), sh(16, F32)],
        scratch_shapes=[pltpu.VMEM((tr, 128), F32)],
        compiler_params=_cparams(("parallel", "parallel")),
    )(d_ya, ya, lse)
    dy0, dl0, dy1, ls1, dl1, dy2, ls2, dl2 = [r.reshape(s, DIL_W) for r in res]
    return [(dy0, lse, dl0), (dy1, ls1, dl1), (dy2, ls2, dl2)]


_RET_SEGS = ((0, 256), (1024, 256), (2048, 512), (4096, 512))


def _split_w_in(win):
    per_head = [win[a:a + RET_HEADS * n].reshape(RET_HEADS, n, D_MODEL) for a, n in _RET_SEGS]
    w_ret = jnp.concatenate(per_head, axis=1).reshape(RET_HEADS * 1536, D_MODEL)
    w_dil = [jnp.concatenate([win[a + DIL_W * g:a + DIL_W * (g + 1)] for a in (6144, 7680, 9216)], axis=0) for g in range(3)]
    return w_ret, win[10752:12800], w_dil


def _join_w_in(g_ret, g_gate, g_dil):
    g_ret = g_ret.reshape(RET_HEADS, 1536, D_MODEL)
    off = (0, 256, 512, 1024, 1536)
    parts = [g_ret[:, off[i]:off[i + 1]].reshape(-1, D_MODEL) for i in range(4)]
    dil = [g_dil[g][DIL_W * i:DIL_W * (i + 1)] for i in range(3) for g in range(3)]
    return jnp.concatenate(parts + dil + [g_gate], axis=0)


def _local_step(xs, pb, tgt, tabs, wts, vec, s, late_shards=None):
    tm = min(2048, s)
    tr = min(256, s)
    mm = functools.partial(_matmul, tm=tm)
    on_mesh = late_shards is not None
    wts = dict(wts)
    w_ret, w_gate, w_dil = _split_w_in(wts["w_in"])
    blocks = lambda g: g.reshape(N_DEV, g.shape[0] // N_DEV, g.shape[1])

    u = _prenorm(xs, vec["g_pre_mix"], s)
    proj_ret = mm(u[0], w_ret, mode="nt", m=s, n=6144, k=1024, tn=1024, tk=1024, out_dtype=BF16, name="inproj_ret")
    proj_gate = mm(u[0], w_gate, mode="nt", m=s, n=2048, k=1024, tn=1024, tk=1024, out_dtype=BF16, name="inproj_gate")
    qkv = [_matmul(u[g], w_dil[g], mode="nt", m=s, n=1536, k=1024, tm=min(1024, s), tn=1536, tk=1024, out_dtype=BF16,
                   name="inproj_dil%d" % g, epi=tabs["dil"][g], epi_width=128, epi_fn=_rope_qk) for g in range(3)]

    names = list(late_shards) if on_mesh else []
    gather = _Exchange([late_shards[n] for n in names], [False] * len(names)) if on_mesh else None
    (yr, y_ret, rstate), gathered = _ret_fwd(proj_ret, tabs["cos_r"], tabs["sin_r"], s, carry=gather)
    wts.update({n: g.reshape(N_DEV * g.shape[1], g.shape[2]) for n, g in zip(names, gathered)})
    a_br = mm(yr, wts["w_ret_out"], mode="nn", m=s, n=1024, k=2048, tn=1024, tk=1024, out_dtype=BF16, name="ret_out")

    o_g, l_g = [], []
    for g, dil in enumerate(DIL_GROUPS):
        o, l = _dil_fwd(qkv[g], dil, s, "dil_fwd%d" % g)
        o_g.append(o)
        l_g.append(l)
    ya, lse = _dil_merge(o_g, l_g, s)
    b_br = mm(ya, wts["w_dil_out"], mode="nt", m=s, n=1024, k=512, tn=1024, tk=512, out_dtype=BF16, name="dil_out")

    def gate_mix(a, b, gr, ga, b0, b1):
        return [_sigmoid(gr.astype(F32) + b0) * a.astype(F32) + _sigmoid(ga.astype(F32) + b1) * b.astype(F32)], []

    (mixed,), _ = _rowwise("gate_mix", gate_mix, s, tr, [(a_br, 1024, 0), (b_br, 1024, 0), (proj_gate, 1024, 0), (proj_gate, 1024, 1)],
                           [vec["b0"], vec["b1"]], [(1024, BF16)])
    z = mm(mixed, wts["w_o"], mode="nn", m=s, n=1024, k=1024, tn=1024, tk=1024, out_dtype=BF16, name="w_o")

    def post_norm(h, f, g_post, g_pre):
        hn = h + _rms(f) * g_post
        return [hn, _rms(hn) * g_pre], []

    (h1, v2), _ = _rowwise("post_mix", post_norm, s, tr, [(xs, 1024, 0), (z, 1024, 0)], [vec["g_post_mix"], vec["g_pre_mlp"]],
                           [(1024, F32), (1024, BF16)])
    a_up = mm(v2, wts["w_up"], mode="nt", m=s, n=4096, k=1024, tn=1024, tk=1024, out_dtype=BF16, name="mlp_up")
    f_dn = mm(a_up, wts["w_down"], mode="nn", m=s, n=1024, k=4096, tn=1024, tk=1024, out_dtype=BF16, name="mlp_down", a_fn=_relu_sq)
    (h2, t_ple), _ = _rowwise("post_mlp", post_norm, s, tr, [(h1, 1024, 0), (f_dn, 1024, 0)], [vec["g_post_mlp"], vec["g_pre_ple"]],
                              [(1024, F32), (1024, BF16)])
    gl = mm(t_ple, wts["w_ple_gate"], mode="nn", m=s, n=1024, k=1024, tn=1024, tk=1024, out_dtype=BF16, name="ple_gate")
    e_ple = mm(pb, wts["w_ple_in"], mode="nt", m=s, n=1024, k=256, tn=1024, tk=256, out_dtype=BF16, name="ple_in")

    def ple_loss(h, glv, e, tg, b, g):
        gate = _sigmoid(glv + b)
        ge = gate * e
        diff = h + _rms(ge) * g - tg
        dy = diff * (1.0 / D_MODEL)
        d_ge, dg = _rms_bwd(ge, g, dy)
        d_gl = d_ge * e * gate * (1.0 - gate)
        loss = jnp.zeros((1, D_MODEL), F32) + 0.5 * jnp.sum(diff * diff) * (1.0 / D_MODEL)
        return [dy, d_gl, d_ge * gate], [_colsum(dg), _colsum(d_gl), loss]

    (dy, d_gl, d_e), (dg_post_ple, db_ple, loss) = _rowwise(
        "ple_loss", ple_loss, s, tr, [(h2, 1024, 0), (gl, 1024, 0), (e_ple, 1024, 0), (tgt, 1024, 0)],
        [vec["b_ple"], vec["g_post_ple"]], [(1024, F32), (1024, BF16), (1024, BF16)], [1024, 1024, 1024])

    ts = min(1024, s)
    wg = functools.partial(_matmul, mode="tn", k=s, tk=ts, out_dtype=BF16)
    grads = {}
    grads["w_ple_in"] = wg(d_e, pb, m=1024, n=256, tm=1024, tn=256, name="g_ple_in")
    grads["w_ple_gate"] = wg(t_ple, d_gl, m=1024, n=1024, tm=1024, tn=1024, name="g_ple_gate")
    d_t = mm(d_gl, wts["w_ple_gate"], mode="nt", m=s, n=1024, k=1024, tn=1024, tk=1024, out_dtype=BF16, name="d_t")

    def bwd_ple_mlp(h, dt, dyv, f, g_pre, g_post):
        dx, dg1 = _rms_bwd(h, g_pre, dt)
        dh = dyv + dx
        df, dg2 = _rms_bwd(f, g_post, dh)
        return [dh, df], [_colsum(dg1), _colsum(dg2)]

    (d_h2, d_f), (dg_pre_ple, dg_post_mlp) = _rowwise(
        "bwd_ple_mlp", bwd_ple_mlp, s, tr, [(h2, 1024, 0), (d_t, 1024, 0), (dy, 1024, 0), (f_dn, 1024, 0)],
        [vec["g_pre_ple"], vec["g_post_mlp"]], [(1024, F32), (1024, BF16)], [1024, 1024])
    d_a = mm(d_f, wts["w_down"], mode="nt", m=s, n=4096, k=1024, tn=1024, tk=1024, out_dtype=BF16, name="d_a",
             epi=(a_up,), epi_fn=lambda acc, av: acc * (2.0 * jnp.maximum(av.astype(F32), 0.0)))
    grads["w_down"] = wg(a_up, d_f, m=4096, n=1024, tm=2048, tn=1024, name="g_down", a_fn=_relu_sq)
    grads["w_up"] = wg(d_a, v2, m=4096, n=1024, tm=2048, tn=1024, name="g_up")
    d_v2 = mm(d_a, wts["w_up"], mode="nn", m=s, n=1024, k=4096, tn=1024, tk=1024, out_dtype=BF16, name="d_v2")

    (d_h1, d_z), (dg_pre_mlp, dg_post_mix) = _rowwise(
        "bwd_mlp_mix", bwd_ple_mlp, s, tr, [(h1, 1024, 0), (d_v2, 1024, 0), (d_h2, 1024, 0), (z, 1024, 0)],
        [vec["g_pre_mlp"], vec["g_post_mix"]], [(1024, F32), (1024, BF16)], [1024, 1024])
    d_mixed = mm(d_z, wts["w_o"], mode="nt", m=s, n=1024, k=1024, tn=1024, tk=1024, out_dtype=BF16, name="d_mixed")
    grads["w_o"] = wg(mixed, d_z, m=1024, n=1024, tm=1024, tn=1024, name="g_o")

    def bwd_gate(dm, a, b, gr, ga, b0, b1):
        sa, sb = _sigmoid(gr.astype(F32) + b0), _sigmoid(ga.astype(F32) + b1)
        dgr = dm * a.astype(F32) * sa * (1.0 - sa)
        dga = dm * b.astype(F32) * sb * (1.0 - sb)
        return [dm * sa, dm * sb, jnp.concatenate([dgr, dga], axis=1)], [_colsum(dgr), _colsum(dga)]

    (d_abr, d_bbr, dproj_gate), (db0, db1) = _rowwise(
        "bwd_gate", bwd_gate, s, tr, [(d_mixed, 1024, 0), (a_br, 1024, 0), (b_br, 1024, 0), (proj_gate, 1024, 0), (proj_gate, 1024, 1)],
        [vec["b0"], vec["b1"]], [(1024, BF16), (1024, BF16), (2048, BF16)], [1024, 1024])
    grads["w_ret_out"] = wg(yr, d_abr, m=2048, n=1024, tm=2048, tn=1024, name="g_ret_out")
    d_yr = mm(d_abr, wts["w_ret_out"], mode="nt", m=s, n=2048, k=1024, tn=1024, tk=1024, out_dtype=BF16, name="d_yr")
    grads["w_dil_out"] = wg(d_bbr, ya, m=1024, n=512, tm=1024, tn=512, name="g_dil_out")
    d_ya = mm(d_bbr, wts["w_dil_out"], mode="nn", m=s, n=512, k=1024, tn=512, tk=1024, out_dtype=F32, name="d_ya")

    slots = {}
    names = list(grads) if on_mesh else []
    shares = _Exchange([blocks(grads[n]) for n in names], [True] * len(names)) if on_mesh else None
    (dproj_ret,), got = _ret_bwd(proj_ret, tabs["cos_r"], tabs["sin_r"], y_ret, d_yr, rstate, s, carry=shares)
    slots.update(zip(names, got))
    upstream = _dil_bwd_prep(d_ya, ya, lse, s)
    dqkv = [_dil_bwd(qkv[g], *upstream[g], *tabs["dil"][g], dil, s, "dil_bwd%d" % g)
            for g, dil in enumerate(DIL_GROUPS)]

    g_ret = wg(dproj_ret, u[0], m=6144, n=1024, tm=2048, tn=1024, name="g_in_ret")
    g_gate = wg(dproj_gate, u[0], m=2048, n=1024, tm=2048, tn=1024, name="g_in_gate")
    g_dil = [wg(dqkv[g], u[g], m=1536, n=1024, tm=1536, tn=1024, name="g_in_dil%d" % g) for g in range(3)]
    grads["w_in"] = _join_w_in(g_ret, g_gate, g_dil)

    du_ret = functools.partial(mm, dproj_ret, w_ret, mode="nn", m=s, n=1024, k=6144, tn=1024, tk=1024, out_dtype=BF16, name="du_ret")
    if on_mesh:
        du_ret, (slots["w_in"],) = du_ret(carry=_Exchange([blocks(grads["w_in"])], [True]))
    else:
        du_ret = du_ret()
    du_gate = mm(dproj_gate, w_gate, mode="nn", m=s, n=1024, k=2048, tn=1024, tk=1024, out_dtype=BF16, name="du_gate")
    du_dil = [mm(dqkv[g], w_dil[g], mode="nn", m=s, n=1024, k=1536, tn=1024, tk=1536, out_dtype=BF16, name="du_dil%d" % g)
              for g in range(3)]

    grad_x, dg_pre_mix = _grad_x(xs, d_h1, (du_ret, du_gate, du_dil[0]), du_dil[1], du_dil[2], vec["g_pre_mix"], s)

    zero = jnp.zeros((1, D_MODEL), F32)
    packet = jnp.concatenate([dg_pre_mix, dg_post_mix, dg_pre_mlp, dg_post_mlp, dg_pre_ple, db_ple, dg_post_ple, loss,
                              db0, db1] + [zero] * 6, axis=0)
    return grad_x, (slots if on_mesh else grads), packet


def _mesh_pos():
    return lax.axis_index("x"), lax.axis_index("y"), lax.axis_index("c")


def _all_gather(shards):
    nw = len(shards)

    def body(*refs):
        ins, outs = refs[:nw], refs[nw:2 * nw]
        send_sems, recv_sems, local_sems = refs[2 * nw:]
        x, y, c = _mesh_pos()
        me, sibling = (x, y, c), (x, y, 1 - c)
        chips = [(1 - x, y), (x, 1 - y), (1 - x, 1 - y)]

        def region(w, dev):
            return outs[w].at[4 * dev[0] + 2 * dev[1] + dev[2]]

        def copy(w, kk, block, to, src=None):
            return pltpu.make_async_remote_copy(
                src_ref=region(w, block) if src is None else src, dst_ref=region(w, block),
                send_sem=send_sems.at[w * 7 + kk], recv_sem=recv_sems.at[w * 7 + kk], device_id=to, device_id_type=MESH)

        mine = [pltpu.make_async_copy(ins[w], region(w, me), local_sems.at[w]) for w in range(nw)]
        for cp in mine:
            cp.start()
        first = []
        for w in range(nw):
            first.append(copy(w, 0, me, sibling, src=ins[w]))
            first += [copy(w, 1 + j, me, (*chip, c), src=ins[w]) for j, chip in enumerate(chips)]
        for cp in first:
            cp.start()
        passed = []
        for j, chip in enumerate(chips):
            for w in range(nw):
                copy(w, 1 + j, (*chip, c), me).wait_recv()
                cp = copy(w, 4 + j, (*chip, c), sibling)
                cp.start()
                passed.append(cp)
        for w in range(nw):
            copy(w, 0, sibling, me).wait_recv()
            for j, chip in enumerate(chips):
                copy(w, 4 + j, (*chip, 1 - c), me).wait_recv()
        for cp in first + passed:
            cp.wait_send()
        for cp in mine:
            cp.wait()

    hbm = pl.BlockSpec(memory_space=pl.ANY)
    return pl.pallas_call(
        body, name="gather_weights",
        in_specs=[hbm] * nw, out_specs=[hbm] * nw,
        out_shape=[jax.ShapeDtypeStruct((N_DEV,) + sh.shape, sh.dtype) for sh in shards],
        scratch_shapes=[pltpu.SemaphoreType.DMA((nw * 7,)), pltpu.SemaphoreType.DMA((nw * 7,)), pltpu.SemaphoreType.DMA((nw,))],
    )(*shards)


class _Exchange:
    def __init__(self, arrays, scatter):
        self.arrays, self.scatter, self.n = list(arrays), list(scatter), len(arrays)
        self.out_shape = [jax.ShapeDtypeStruct(a.shape if sc else (N_DEV,) + a.shape, a.dtype)
                          for a, sc in zip(self.arrays, self.scatter)]
        self.scratch = [pltpu.SemaphoreType.DMA((self.n * 7,)), pltpu.SemaphoreType.DMA((self.n * 7,)),
                        pltpu.SemaphoreType.DMA((self.n,))]
        self.specs = [pl.BlockSpec(memory_space=pl.ANY)] * self.n

    def _copies(self, srcs, dsts, sems):
        send_sems, recv_sems, local_sems = sems
        x, y, c = _mesh_pos()
        my = 4 * x + 2 * y + c
        src_of = lambda w, idx: srcs[w].at[idx] if self.scatter[w] else srcs[w]
        local = [pltpu.make_async_copy(src_of(w, my), dsts[w].at[my], local_sems.at[w]) for w in range(self.n)]
        sends, recvs = [], []
        for w in range(self.n):
            for r in range(1, N_DEV):
                px = 1 - x if r & 4 else x
                py = 1 - y if r & 2 else y
                pc = 1 - c if r & 1 else c
                pidx = 4 * px + 2 * py + pc
                kw = dict(send_sem=send_sems.at[w * 7 + r - 1], recv_sem=recv_sems.at[w * 7 + r - 1],
                          device_id=(px, py, pc), device_id_type=MESH)
                sends.append(pltpu.make_async_remote_copy(src_ref=src_of(w, pidx), dst_ref=dsts[w].at[my], **kw))
                recvs.append(pltpu.make_async_remote_copy(src_ref=src_of(w, pidx), dst_ref=dsts[w].at[pidx], **kw))
        return local, sends, recvs

    def start(self, srcs, dsts, sems):
        local, sends, _ = self._copies(srcs, dsts, sems)
        for cp in local + sends:
            cp.start()

    def wait(self, srcs, dsts, sems):
        local, sends, recvs = self._copies(srcs, dsts, sems)
        for cp in recvs:
            cp.wait_recv()
        for cp in sends:
            cp.wait_send()
        for cp in local:
            cp.wait()

    def split(self, refs, n_in, n_out):
        srcs = refs[n_in:n_in + self.n]
        dsts = refs[n_in + self.n + n_out:n_in + 2 * self.n + n_out]
        return srcs, dsts, refs[len(refs) - 3:]


def _run_exchange(ex, name):
    def body(*refs):
        parts = ex.split(refs, 0, 0)
        ex.start(*parts)
        ex.wait(*parts)

    return pl.pallas_call(body, name=name, in_specs=ex.specs, out_specs=ex.specs, out_shape=ex.out_shape,
                          scratch_shapes=ex.scratch)(*ex.arrays)


def _pick_rows(r, c, target_bytes):
    t = r
    while (t // 2) % 16 == 0 and t // 2 >= 16 and t * c * 4 > target_bytes:
        t //= 2
    return t


def _sum_slots(slots, name):
    ns, r, c = slots.shape
    tr = _pick_rows(r, c, 256 * 1024)

    def body(s_ref, o_ref):
        acc = s_ref[0].astype(F32)
        for kk in range(1, ns):
            acc = acc + s_ref[kk].astype(F32)
        o_ref[...] = acc

    return pl.pallas_call(
        body, name=name, grid=(r // tr,),
        in_specs=[pl.BlockSpec((ns, tr, c), lambda i: (0, i, 0))], out_specs=pl.BlockSpec((tr, c), lambda i: (i, 0)),
        out_shape=jax.ShapeDtypeStruct((r, c), F32), compiler_params=_cparams(("parallel",)),
    )(slots)


def _adamw(slots, w, m, v, name):
    ns, r, c = slots.shape
    tr = _pick_rows(r, c, 256 * 1024)

    def body(s_ref, w_ref, m_ref, v_ref, g_out, d_out, m_out, v_out):
        g = s_ref[0].astype(F32)
        for kk in range(1, ns):
            g = g + s_ref[kk].astype(F32)
        mn = ADAM_B1 * m_ref[...] + (1.0 - ADAM_B1) * g
        vn = ADAM_B2 * v_ref[...] + (1.0 - ADAM_B2) * (g * g)
        m_hat = mn / (1.0 - ADAM_B1 ** ADAM_STEP)
        v_hat = vn / (1.0 - ADAM_B2 ** ADAM_STEP)
        g_out[...] = g
        d_out[...] = -ADAM_LR * (m_hat / (jnp.sqrt(v_hat) + ADAM_EPS) + ADAM_WD * w_ref[...])
        m_out[...] = mn
        v_out[...] = vn

    blk = pl.BlockSpec((tr, c), lambda i: (i, 0))
    return pl.pallas_call(
        body, name=name, grid=(r // tr,),
        in_specs=[pl.BlockSpec((ns, tr, c), lambda i: (0, i, 0)), blk, blk, blk], out_specs=[blk] * 4,
        out_shape=[jax.ShapeDtypeStruct((r, c), F32)] * 4, compiler_params=_cparams(("parallel",)),
    )(slots, w, m, v)


def _rotary_tables(pos, s):
    posf = pos.astype(F32)
    inv_freq = 1.0 / (10000.0 ** jnp.linspace(0.0, 1.0, RET_QK // 2, dtype=F32))
    ang = posf[:, None] * inv_freq
    tabs = {"cos_r": jnp.cos(ang), "sin_r": jnp.sin(ang), "dil": []}
    freqs = 500000.0 ** (-jnp.arange(0, 16, 2, dtype=F32) / 16)
    spread = np.zeros((16, 384), np.float32)
    bias = np.zeros((1, 384), np.float32)
    for head in range(2):
        for i in range(8):
            spread[i, 64 * head + i] = spread[i, 64 * head + 8 + i] = 1.0
            spread[8 + i, 128 + 64 * head + i] = -1.0
            spread[8 + i, 256 + 64 * head + 8 + i] = 1.0
        bias[0, 64 * head + 16:64 * head + 64] = 1.0

    def expand(t, e, b):
        hi = t.astype(BF16)
        lo = (t - hi.astype(F32)).astype(BF16)
        out = _dot(hi, e, NN) + _dot(lo, e, NN) + b
        return [out[:, 0:128], out[:, 128:256], out[:, 256:384]], []

    for g, dil in enumerate(DIL_GROUPS):
        ang = posf.reshape(s // dil, dil).T.reshape(s, 1) * freqs
        cs = jnp.concatenate([jnp.cos(ang), jnp.sin(ang)], axis=1)
        t3, _ = _rowwise("rot_tables%d" % g, expand, s, min(1024, s), [(cs, 16, 0)],
                         [jnp.asarray(spread, BF16), jnp.asarray(bias)], [(128, F32)] * 3)
        tabs["dil"].append(tuple(t3))
    return tabs


_TRANSPOSED = ("w_in", "w_dil_out", "w_up", "w_ple_in")
_MATS = ("w_in", "w_ret_out", "w_dil_out", "w_o", "w_up", "w_down", "w_ple_gate", "w_ple_in")
_VECS = ("g_pre_mix", "g_post_mix", "g_pre_mlp", "g_post_mlp", "g_pre_ple", "b_ple_gate", "g_post_ple")
_ORDER = ("w_in", "b_gate", "w_ret_out", "w_dil_out", "w_o", "g_pre_mix", "g_post_mix", "g_pre_mlp", "g_post_mlp", "w_up",
          "w_down", "g_pre_ple", "w_ple_gate", "b_ple_gate", "w_ple_in", "g_post_ple")


def kernel(x, p, positions, w_in, b_gate, w_ret_out, w_dil_out, w_o, g_pre_mix, g_post_mix, g_pre_mlp, g_post_mlp, w_up, w_down, g_pre_ple, w_ple_gate, b_ple_gate, w_ple_in, g_post_ple, loss_target, m_w_in, m_b_gate, m_w_ret_out, m_w_dil_out, m_w_o, m_g_pre_mix, m_g_post_mix, m_g_pre_mlp, m_g_post_mlp, m_w_up, m_w_down, m_g_pre_ple, m_w_ple_gate, m_b_ple_gate, m_w_ple_in, m_g_post_ple, v_w_in, v_b_gate, v_w_ret_out, v_w_dil_out, v_w_o, v_g_pre_mix, v_g_post_mix, v_g_pre_mlp, v_g_post_mlp, v_w_up, v_w_down, v_g_pre_ple, v_w_ple_gate, v_b_ple_gate, v_w_ple_in, v_g_post_ple):
    s = x.shape[1]
    wd = dict(w_in=w_in, b_gate=b_gate, w_ret_out=w_ret_out, w_dil_out=w_dil_out, w_o=w_o, g_pre_mix=g_pre_mix,
              g_post_mix=g_post_mix, g_pre_mlp=g_pre_mlp, g_post_mlp=g_post_mlp, w_up=w_up, w_down=w_down,
              g_pre_ple=g_pre_ple, w_ple_gate=w_ple_gate, b_ple_gate=b_ple_gate, w_ple_in=w_ple_in, g_post_ple=g_post_ple)
    md = dict(w_in=m_w_in, b_gate=m_b_gate, w_ret_out=m_w_ret_out, w_dil_out=m_w_dil_out, w_o=m_w_o, g_pre_mix=m_g_pre_mix,
              g_post_mix=m_g_post_mix, g_pre_mlp=m_g_pre_mlp, g_post_mlp=m_g_post_mlp, w_up=m_w_up, w_down=m_w_down,
              g_pre_ple=m_g_pre_ple, w_ple_gate=m_w_ple_gate, b_ple_gate=m_b_ple_gate, w_ple_in=m_w_ple_in, g_post_ple=m_g_post_ple)
    vd = dict(w_in=v_w_in, b_gate=v_b_gate, w_ret_out=v_w_ret_out, w_dil_out=v_w_dil_out, w_o=v_w_o, g_pre_mix=v_g_pre_mix,
              g_post_mix=v_g_post_mix, g_pre_mlp=v_g_pre_mlp, g_post_mlp=v_g_post_mlp, w_up=v_w_up, w_down=v_w_down,
              g_pre_ple=v_g_pre_ple, w_ple_gate=v_w_ple_gate, b_ple_gate=v_b_ple_gate, w_ple_in=v_w_ple_in, g_post_ple=v_g_post_ple)

    shards = {n: (wd[n][0].T if n in _TRANSPOSED else wd[n][0]).astype(BF16) for n in _MATS}
    w_in_all, bg_all = _all_gather([shards.pop("w_in"), b_gate[0]])
    wts = {"w_in": w_in_all.reshape(N_DEV * w_in_all.shape[1], D_MODEL)}
    bg = bg_all.transpose(1, 0, 2).reshape(2, D_MODEL)
    vec = {n: wd[n] for n in _VECS}
    vec.update(b0=bg[0:1], b1=bg[1:2], b_ple=b_ple_gate)

    tabs = _rotary_tables(positions[0], s)
    grad_x, slots, packet = _local_step(x[0], p[0, 0].astype(BF16), loss_target[0], tabs, wts, vec, s, late_shards=shards)

    (packets,) = _run_exchange(_Exchange([packet], [False]), "exchange_vectors")
    out = {}
    for n in _MATS:
        sl = slots[n]
        if n in _TRANSPOSED:
            sl = _sum_slots(sl, "sum_" + n).T[None]
        out[n] = _adamw(sl, wd[n][0], md[n][0], vd[n][0], "adamw_" + n)
    zero_rows = jnp.zeros((16 - len(_VECS), D_MODEL), F32)
    pack = lambda d: jnp.concatenate([d[n] for n in _VECS] + [zero_rows], axis=0)
    small = _adamw(packets, pack(wd), pack(md), pack(vd), "adamw_vectors")
    for i, n in enumerate(_VECS):
        out[n] = tuple(t[i:i + 1] for t in small)
    my = 4 * lax.axis_index("x") + 2 * lax.axis_index("y") + lax.axis_index("c")
    g_bias = lax.dynamic_slice(small[0], (8, my * 128), (2, 128))
    out["b_gate"] = _adamw(g_bias[None], b_gate[0], m_b_gate[0], v_b_gate[0], "adamw_b_gate")
    loss = small[0][7, 0]

    res = [loss, grad_x[None]]
    for kk in range(4):
        res += [out[n][kk][None] if out[n][kk].ndim == 2 and wd[n].ndim == 3 else out[n][kk] for n in _ORDER]
    return tuple(res)
```

```python
import functools
import math

import numpy as np
import jax
import jax.numpy as jnp
from jax import lax
from jax.experimental import pallas as pl
from jax.experimental.pallas import tpu as pltpu

F32, BF16 = jnp.float32, jnp.bfloat16
D_MODEL = 1024
EPS = 1e-6
N_DEV = 8
RET_HEADS, RET_QK, RET_V, RET_CHUNK = 4, 256, 512, 128
DIL_GROUPS = (1, 4, 16)
DIL_W = 512
QB = 128
NEG = -1e30
ADAM_LR, ADAM_B1, ADAM_B2, ADAM_EPS, ADAM_WD, ADAM_STEP = 0.001, 0.9, 0.999, 1e-08, 0.01, 10
VMEM_LIMIT_BYTES = 56 * 1024 * 1024
MESH = pl.DeviceIdType.MESH

NN = ((1,), (0,))
NT = ((1,), (1,))
TN = ((0,), (0,))


def _dot(a, b, dn):
    return lax.dot_general(a, b, (dn, ((), ())), preferred_element_type=F32)


def _cparams(sem):
    return pltpu.CompilerParams(dimension_semantics=sem, vmem_limit_bytes=VMEM_LIMIT_BYTES)


def _rms(x):
    return x * lax.rsqrt(jnp.mean(x * x, axis=-1, keepdims=True) + EPS)


def _rms_bwd(x, g, dy):
    r = lax.rsqrt(jnp.mean(x * x, axis=-1, keepdims=True) + EPS)
    xh = x * r
    t = dy * g
    dx = r * (t - xh * jnp.mean(t * xh, axis=-1, keepdims=True))
    return dx, dy * xh


def _colsum(v):
    return jnp.sum(v, axis=0, keepdims=True)


def _sigmoid(v):
    return 1.0 / (1.0 + jnp.exp(-v))


def _pallas(compute, *, name, grid, in_specs, out_specs, out_shape, scratch, semantics, args, carry=None):
    n_in, n_out = len(in_specs), len(out_specs)
    if carry is None:
        res = pl.pallas_call(compute, name=name, grid=grid, in_specs=in_specs, out_specs=out_specs, out_shape=out_shape,
                             scratch_shapes=scratch, compiler_params=_cparams(semantics))(*args)
        return res, []
    n_steps = math.prod(grid)

    def body(*refs):
        step = 0
        for axis, size in enumerate(grid):
            step = step * size + pl.program_id(axis)
        parts = carry.split(refs, n_in, n_out)
        pl.when(step == 0)(lambda: carry.start(*parts))
        compute(*refs[:n_in], *refs[n_in + carry.n:n_in + carry.n + n_out], *refs[n_in + 2 * carry.n + n_out:len(refs) - 3])
        pl.when(step == n_steps - 1)(lambda: carry.wait(*parts))

    res = pl.pallas_call(
        body, name=name, grid=grid, in_specs=list(in_specs) + carry.specs, out_specs=list(out_specs) + carry.specs,
        out_shape=list(out_shape) + carry.out_shape, scratch_shapes=list(scratch) + carry.scratch,
        compiler_params=_cparams(("arbitrary",) * len(grid)))(*args, *carry.arrays)
    return res[:n_out], res[n_out:]


def _matmul(a, b, *, mode, m, n, k, tm, tn, tk, out_dtype, name, a_fn=None, epi=(), epi_width=None, epi_fn=None, carry=None):
    nk = k // tk
    grid = (m // tm, n // tn, nk)
    if mode == "nn":
        a_blk, a_im, b_blk, b_im, dn = (tm, tk), (lambda i, j, kk: (i, kk)), (tk, tn), (lambda i, j, kk: (kk, j)), NN
    elif mode == "nt":
        a_blk, a_im, b_blk, b_im, dn = (tm, tk), (lambda i, j, kk: (i, kk)), (tn, tk), (lambda i, j, kk: (j, kk)), NT
    else:
        a_blk, a_im, b_blk, b_im, dn = (tk, tm), (lambda i, j, kk: (kk, i)), (tk, tn), (lambda i, j, kk: (kk, j)), TN
    o_im = lambda i, j, kk: (i, j)
    n_in = 2 + len(epi)

    def body(*refs):
        a_ref, b_ref = refs[0], refs[1]
        o_ref = refs[n_in]
        acc_ref = refs[n_in + 1] if nk > 1 else None

        def finish(acc):
            if epi:
                acc = epi_fn(acc, *[r[...] for r in refs[2:n_in]])
            o_ref[...] = acc.astype(o_ref.dtype)

        av = a_ref[...]
        if a_fn is not None:
            av = a_fn(av)
        part = _dot(av, b_ref[...], dn)
        if nk == 1:
            finish(part)
        else:
            kk = pl.program_id(2)

            @pl.when(kk == 0)
            def _():
                acc_ref[...] = part

            @pl.when(kk > 0)
            def _():
                acc_ref[...] += part

            @pl.when(kk == nk - 1)
            def _():
                finish(acc_ref[...])

    epi_spec = pl.BlockSpec((tm, tn), o_im) if epi_width is None else pl.BlockSpec((tm, epi_width), lambda i, j, kk: (i, 0))
    in_specs = [pl.BlockSpec(a_blk, a_im), pl.BlockSpec(b_blk, b_im)] + [epi_spec] * len(epi)
    args = [a, b, *epi]
    (out,), got = _pallas(
        body, name=name, grid=grid, in_specs=in_specs, out_specs=[pl.BlockSpec((tm, tn), o_im)],
        out_shape=[jax.ShapeDtypeStruct((m, n), out_dtype)], scratch=[pltpu.VMEM((tm, tn), F32)] if nk > 1 else [],
        semantics=("parallel", "parallel", "arbitrary"), args=args, carry=carry)
    return out if carry is None else (out, got)


def _relu_sq(v):
    r = jnp.maximum(v.astype(F32), 0.0)
    return (r * r).astype(BF16)


def _rowwise(name, fn, s, tr, rows, vecs, outs, accs=()):
    n_r, n_v, n_o, n_a = len(rows), len(vecs), len(outs), len(accs)

    def body(*refs):
        vals = [refs[i][...].astype(F32) for i in range(n_r)] + [refs[n_r + i][...] for i in range(n_v)]
        o_refs = refs[n_r + n_v:n_r + n_v + n_o]
        a_refs = refs[n_r + n_v + n_o:]
        o_vals, a_vals = fn(*vals)
        for ref, val in zip(o_refs, o_vals):
            ref[...] = val.astype(ref.dtype)
        if n_a:
            @pl.when(pl.program_id(0) == 0)
            def _():
                for ref in a_refs:
                    ref[...] = jnp.zeros_like(ref)

            for ref, val in zip(a_refs, a_vals):
                ref[...] += val

    in_specs = [pl.BlockSpec((tr, w), functools.partial(lambda i, cb: (i, cb), cb=cb)) for _, w, cb in rows]
    in_specs += [pl.BlockSpec(v.shape, lambda i: (0, 0)) for v in vecs]
    out_specs = [pl.BlockSpec((tr, w), lambda i: (i, 0)) for w, _ in outs]
    out_specs += [pl.BlockSpec((1, w), lambda i: (0, 0)) for w in accs]
    out_shape = [jax.ShapeDtypeStruct((s, w), dt) for w, dt in outs]
    out_shape += [jax.ShapeDtypeStruct((1, w), F32) for w in accs]
    res = pl.pallas_call(
        body, name=name, grid=(s // tr,), in_specs=in_specs, out_specs=out_specs, out_shape=out_shape,
        compiler_params=_cparams(("arbitrary",)),
    )(*[r[0] for r in rows], *vecs)
    return res[:n_o], res[n_o:]


_ROW_TILE = 256
_STREAM_SPECS = [pl.BlockSpec((dil, _ROW_TILE // dil, D_MODEL), lambda i: (0, i, 0)) for dil in DIL_GROUPS[1:]]
_NAT_SPEC = pl.BlockSpec((_ROW_TILE, D_MODEL), lambda i: (i, 0))
_VEC_SPEC = pl.BlockSpec((1, D_MODEL), lambda i: (0, 0))
_COL_BLOCKS = pltpu.VMEM((D_MODEL // 128, _ROW_TILE, 128), F32)


def _prenorm(xs, g, s):
    tr = _ROW_TILE

    def body(x_ref, g_ref, u_ref, u4_ref, u16_ref, buf):
        xn = _rms(x_ref[...]) * g_ref[...]
        u_ref[...] = xn.astype(BF16)
        for cb in range(8):
            buf[cb] = xn[:, cb * 128:(cb + 1) * 128]
        for dil, out in ((4, u4_ref), (16, u16_ref)):
            for c in range(dil):
                rows = pl.ds(c, tr // dil, stride=dil)
                out[c] = jnp.concatenate([buf.at[cb][rows, :] for cb in range(8)], axis=1).astype(BF16)

    res = pl.pallas_call(
        body, name="prenorm", grid=(s // tr,), in_specs=[_NAT_SPEC, _VEC_SPEC], out_specs=[_NAT_SPEC] + _STREAM_SPECS,
        out_shape=[jax.ShapeDtypeStruct((s, D_MODEL), BF16)]
        + [jax.ShapeDtypeStruct((dil, s // dil, D_MODEL), BF16) for dil in DIL_GROUPS[1:]],
        scratch_shapes=[_COL_BLOCKS], compiler_params=_cparams(("parallel",)),
    )(xs, g)
    return [r.reshape(s, D_MODEL) for r in res]


def _grad_x(xs, d_h1, du_nat, du4, du16, g, s):
    tr = _ROW_TILE

    def body(x_ref, dh_ref, a_ref, b_ref, c_ref, u4_ref, u16_ref, g_ref, dx_ref, dg_ref, buf):
        du = a_ref[...].astype(F32) + b_ref[...].astype(F32) + c_ref[...].astype(F32)
        for dil, src in ((4, u4_ref), (16, u16_ref)):
            for c in range(dil):
                part = src[c].astype(F32)
                for cb in range(8):
                    buf.at[cb][pl.ds(c, tr // dil, stride=dil), :] = part[:, cb * 128:(cb + 1) * 128]
            du = du + jnp.concatenate([buf[cb] for cb in range(8)], axis=1)
        dx, dgr = _rms_bwd(x_ref[...], g_ref[...], du)
        dx_ref[...] = dh_ref[...] + dx

        @pl.when(pl.program_id(0) == 0)
        def _():
            dg_ref[...] = jnp.zeros_like(dg_ref)

        dg_ref[...] += _colsum(dgr)

    return pl.pallas_call(
        body, name="grad_x", grid=(s // tr,), in_specs=[_NAT_SPEC] * 5 + _STREAM_SPECS + [_VEC_SPEC],
        out_specs=[_NAT_SPEC, _VEC_SPEC],
        out_shape=[jax.ShapeDtypeStruct((s, D_MODEL), F32), jax.ShapeDtypeStruct((1, D_MODEL), F32)],
        scratch_shapes=[_COL_BLOCKS], compiler_params=_cparams(("arbitrary",)),
    )(xs, d_h1, *du_nat, du4.reshape(4, s // 4, D_MODEL), du16.reshape(16, s // 16, D_MODEL), g)


def _ret_tables():
    h = np.arange(RET_HEADS, dtype=np.float32)
    lg = np.log1p(-(np.float32(2.0) ** (-5.0 - h))).astype(np.float32)
    idx = np.arange(RET_CHUNK, dtype=np.float32)
    diff = idx[:, None] - idx[None, :]
    dm = np.where(diff[None] >= 0, np.exp(np.maximum(diff, 0.0)[None] * lg[:, None, None]), 0.0)
    qd = np.exp((idx + 1.0)[None, :, None] * lg[:, None, None])
    kd = np.exp((RET_CHUNK - 1.0 - idx)[None, :, None] * lg[:, None, None])
    cd = np.exp(RET_CHUNK * lg)[:, None, None]
    return [jnp.asarray(t, F32) for t in (dm, qd, kd, cd)]


def _rope_half(v, cos, sin):
    v1, v2 = v[:, :128], v[:, 128:]
    return jnp.concatenate([v1 * cos - v2 * sin, v2 * cos + v1 * sin], axis=1)


def _unrope_half(d, cos, sin):
    d1, d2 = d[:, :128], d[:, 128:]
    return jnp.concatenate([d1 * cos + d2 * sin, d2 * cos - d1 * sin], axis=1)


_RET_HEADS_FWD, _RET_HEADS_BWD = 1, 2


def _ret_specs(rb, rev_n, hp):
    def rowmap(w_blk):
        return lambda h, n: (rev_n(n), w_blk(h))
    tab = [pl.BlockSpec((hp, RET_CHUNK, RET_CHUNK), lambda h, n: (h, 0, 0)),
           pl.BlockSpec((hp, RET_CHUNK, 1), lambda h, n: (h, 0, 0)),
           pl.BlockSpec((hp, RET_CHUNK, 1), lambda h, n: (h, 0, 0)),
           pl.BlockSpec((hp, 1, 1), lambda h, n: (h, 0, 0))]
    proj = pl.BlockSpec((rb, hp * 1536), rowmap(lambda h: h))
    cs = pl.BlockSpec((rb, 128), rowmap(lambda h: 0))
    hv = pl.BlockSpec((rb, hp * RET_V), rowmap(lambda h: h))
    return proj, cs, hv, tab


def _ret_fwd(proj_ret, cos, sin, s, carry=None):
    rb = min(512, s)
    ch = rb // RET_CHUNK
    nb = s // rb
    hp = _RET_HEADS_FWD
    proj_spec, cs_spec, hv_spec, tab_specs = _ret_specs(rb, lambda n: n, hp)

    def body(p_ref, cos_ref, sin_ref, dm_ref, qd_ref, kd_ref, cd_ref, yr_ref, y_ref, rs_ref, r_acc):
        @pl.when(pl.program_id(1) == 0)
        def _():
            r_acc[...] = jnp.zeros_like(r_acc)

        for c, hh in [(c, hh) for c in range(ch) for hh in range(hp)]:
            rows = slice(c * RET_CHUNK, (c + 1) * RET_CHUNK)
            pc, hc = hh * 1536, hh * RET_V
            dm, qd, kd, cd = dm_ref[hh], qd_ref[hh], kd_ref[hh], cd_ref[hh]
            cosv, sinv = cos_ref[rows, :], sin_ref[rows, :]
            q = _rope_half(p_ref[rows, pc:pc + 256].astype(F32), cosv, sinv)
            kk = _rope_half(p_ref[rows, pc + 256:pc + 512].astype(F32), cosv, sinv) * (RET_QK ** -0.5)
            v = p_ref[rows, pc + 512:pc + 1024]
            g = p_ref[rows, pc + 1024:pc + 1536].astype(F32)
            rb16 = r_acc[hh].astype(BF16)
            rs_ref[hh, c] = rb16
            sc = _dot(q.astype(BF16), kk.astype(BF16), NT) * dm
            y = _dot(sc.astype(BF16), v, NN) + _dot((q * qd).astype(BF16), rb16, NN)
            r_acc[hh] = r_acc[hh] * cd + _dot((kk * kd).astype(BF16), v, TN)
            y_ref[rows, hc:hc + RET_V] = y.astype(BF16)
            yr_ref[rows, hc:hc + RET_V] = (_rms(y) * (g * _sigmoid(g))).astype(BF16)

    return _pallas(
        body, name="ret_fwd", grid=(RET_HEADS // hp, nb),
        in_specs=[proj_spec, cs_spec, cs_spec] + tab_specs,
        out_specs=[hv_spec, hv_spec, pl.BlockSpec((hp, ch, RET_QK, RET_V), lambda h, n: (h, n, 0, 0))],
        out_shape=[jax.ShapeDtypeStruct((s, RET_HEADS * RET_V), BF16), jax.ShapeDtypeStruct((s, RET_HEADS * RET_V), BF16),
                   jax.ShapeDtypeStruct((RET_HEADS, s // RET_CHUNK, RET_QK, RET_V), BF16)],
        scratch=[pltpu.VMEM((hp, RET_QK, RET_V), F32)], semantics=("parallel", "arbitrary"),
        args=(proj_ret, cos, sin, *_ret_tables()), carry=carry)


def _ret_bwd(proj_ret, cos, sin, y, d_yr, rs, s, carry=None):
    rb = min(512, s)
    ch = rb // RET_CHUNK
    nb = s // rb
    hp = _RET_HEADS_BWD
    proj_spec, cs_spec, hv_spec, tab_specs = _ret_specs(rb, lambda n: nb - 1 - n, hp)

    def body(p_ref, cos_ref, sin_ref, y_ref, dyr_ref, rs_ref, dm_ref, qd_ref, kd_ref, cd_ref, o_ref, dr_acc):
        @pl.when(pl.program_id(1) == 0)
        def _():
            dr_acc[...] = jnp.zeros_like(dr_acc)

        for c, hh in [(c, hh) for c in reversed(range(ch)) for hh in range(hp)]:
            rows = slice(c * RET_CHUNK, (c + 1) * RET_CHUNK)
            pc, hc = hh * 1536, hh * RET_V
            dm, qd, kd, cd = dm_ref[hh], qd_ref[hh], kd_ref[hh], cd_ref[hh]
            cosv, sinv = cos_ref[rows, :], sin_ref[rows, :]
            q = _rope_half(p_ref[rows, pc:pc + 256].astype(F32), cosv, sinv)
            kk = _rope_half(p_ref[rows, pc + 256:pc + 512].astype(F32), cosv, sinv) * (RET_QK ** -0.5)
            v = p_ref[rows, pc + 512:pc + 1024]
            g = p_ref[rows, pc + 1024:pc + 1536].astype(F32)
            yv = y_ref[rows, hc:hc + RET_V].astype(F32)
            dyr = dyr_ref[rows, hc:hc + RET_V].astype(F32)
            sg = _sigmoid(g)
            r = lax.rsqrt(jnp.mean(yv * yv, axis=-1, keepdims=True) + EPS)
            yn = yv * r
            dg = dyr * yn * (sg * (1.0 + g * (1.0 - sg)))
            dyn = dyr * (g * sg)
            dy = (r * (dyn - yn * jnp.mean(dyn * yn, axis=-1, keepdims=True))).astype(BF16)
            qb, kb = q.astype(BF16), kk.astype(BF16)
            rb16 = rs_ref[hh, c]
            drb = dr_acc[hh].astype(BF16)
            sd = _dot(qb, kb, NT) * dm
            ds = (_dot(dy, v, NT) * dm).astype(BF16)
            dq = _dot(ds, kb, NN) + qd * _dot(dy, rb16, NT)
            dk = _dot(ds, qb, TN) + kd * _dot(v, drb, NT)
            dv = _dot(sd.astype(BF16), dy, TN) + _dot((kk * kd).astype(BF16), drb, NN)
            dr_acc[hh] = dr_acc[hh] * cd + _dot((q * qd).astype(BF16), dy, TN)
            o_ref[rows, pc:pc + 256] = _unrope_half(dq, cosv, sinv).astype(BF16)
            o_ref[rows, pc + 256:pc + 512] = (_unrope_half(dk, cosv, sinv) * (RET_QK ** -0.5)).astype(BF16)
            o_ref[rows, pc + 512:pc + 1024] = dv.astype(BF16)
            o_ref[rows, pc + 1024:pc + 1536] = dg.astype(BF16)

    in_specs = [proj_spec, cs_spec, cs_spec, hv_spec, hv_spec,
                pl.BlockSpec((hp, ch, RET_QK, RET_V), lambda h, n: (h, nb - 1 - n, 0, 0))] + tab_specs
    return _pallas(
        body, name="ret_bwd", grid=(RET_HEADS // hp, nb), in_specs=in_specs, out_specs=[proj_spec],
        out_shape=[jax.ShapeDtypeStruct((s, RET_HEADS * 1536), BF16)], scratch=[pltpu.VMEM((hp, RET_QK, RET_V), F32)],
        semantics=("parallel", "arbitrary"), args=(proj_ret, cos, sin, y, d_yr, rs, *_ret_tables()), carry=carry)


def _rope_qk(acc, c, s1, s2):
    outs = []
    for cc in range(8):
        vv = acc[:, cc * 128:(cc + 1) * 128]
        outs.append(vv * c + pltpu.roll(vv, 120, 1) * s1 + pltpu.roll(vv, 8, 1) * s2)
    return jnp.concatenate(outs + [acc[:, 2 * DIL_W:]], axis=1)


def _pair_masks(keys_on_rows=False):
    ri = lax.broadcasted_iota(jnp.int32, (2 * QB, 2 * QB), 1 if keys_on_rows else 0)
    ci = lax.broadcasted_iota(jnp.int32, (2 * QB, 2 * QB), 0 if keys_on_rows else 1)
    e = ci - (ri & (QB - 1))
    lane_lo = lax.broadcasted_iota(jnp.int32, (2 * QB, 128), 1) < 64
    return ci, jnp.logical_and(e >= 0, e <= QB), lane_lo


def _stack_heads(v, lane_lo):
    z = jnp.zeros_like(v)
    return jnp.concatenate([jnp.where(lane_lo, v, z), jnp.where(lane_lo, z, v)], axis=0)


def _dil_fwd(qkv, dil, s, name):
    length = s // dil
    rb = min(512, length)
    nsub = rb // QB
    nbs = length // rb
    sub_per = rb // QB

    def body(q_ref, k_ref, v_ref, kp_ref, vp_ref, o_ref, l_ref):
        first = (pl.program_id(0) % nbs) == 0
        ci, band, lane_lo = _pair_masks()
        lo1 = lane_lo[0:QB]

        for i in range(nsub):
            rows = slice(i * QB, (i + 1) * QB)
            mask = jnp.logical_and(band, ci >= jnp.where(first, QB, 0)) if i == 0 else band
            for j in range(4):
                lanes = slice(j * 128, (j + 1) * 128)
                q2 = _stack_heads(q_ref[rows, lanes], lo1)
                if i == 0:
                    k2 = jnp.concatenate([kp_ref[:, lanes], k_ref[rows, lanes]], axis=0)
                    v2 = jnp.concatenate([vp_ref[:, lanes], v_ref[rows, lanes]], axis=0)
                else:
                    k2, v2 = k_ref[(i - 1) * QB:(i + 1) * QB, lanes], v_ref[(i - 1) * QB:(i + 1) * QB, lanes]
                v2 = _stack_heads(v2, lane_lo)
                sc = jnp.where(mask, _dot(q2, k2, NT) * 0.125, NEG)
                m = jnp.max(sc, axis=1, keepdims=True)
                p = jnp.exp(sc - m)
                den = jnp.sum(p, axis=1, keepdims=True)
                pb = p.astype(BF16)
                o = _dot(jnp.concatenate([pb[0:QB], pb[QB:]], axis=1), v2, NN)
                inv = 1.0 / den
                lse = m + jnp.log(den)
                o_ref[rows, lanes] = o * jnp.where(lo1, inv[0:QB], inv[QB:])
                l_ref[rows, lanes] = jnp.where(lo1, lse[0:QB], lse[QB:])

    prev = lambda n: jnp.maximum(n * sub_per - 1, 0)
    cur = lambda cb: (lambda n: (n, cb))
    return pl.pallas_call(
        body, name=name, grid=(s // rb,),
        in_specs=[pl.BlockSpec((rb, DIL_W), cur(0)), pl.BlockSpec((rb, DIL_W), cur(1)), pl.BlockSpec((rb, DIL_W), cur(2)),
                  pl.BlockSpec((QB, DIL_W), lambda n: (prev(n), 1)), pl.BlockSpec((QB, DIL_W), lambda n: (prev(n), 2))],
        out_specs=[pl.BlockSpec((rb, DIL_W), cur(0)), pl.BlockSpec((rb, DIL_W), cur(0))],
        out_shape=[jax.ShapeDtypeStruct((s, DIL_W), F32), jax.ShapeDtypeStruct((s, DIL_W), F32)],
        compiler_params=_cparams(("parallel",)),
    )(qkv, qkv, qkv, qkv, qkv)


def _dil_bwd(qkv, dya, lse, dlt, tc, ts1, ts2, dil, s, name):
    length = s // dil
    rb = min(512, length)
    nsub = rb // QB
    nbs = length // rb
    last_blk = s // QB - 1

    def body(q_ref, k_ref, v_ref, kp_ref, vp_ref, qn_ref, dy_ref, dyn_ref, l_ref, ln_ref, d_ref, dn_ref,
             c_ref, s1_ref, s2_ref, o_ref, dka, dva):
        nl = pl.program_id(0) % nbs
        first, last = nl == 0, nl == nbs - 1
        ci, band, lane_lo = _pair_masks(keys_on_rows=True)
        lo1 = lane_lo[0:QB]

        def unrope(d, rows):
            return d * c_ref[rows, :] + pltpu.roll(d * s1_ref[rows, :], 8, 1) + pltpu.roll(d * s2_ref[rows, :], 120, 1)

        for qi in range(nsub + 1):
            nxt = qi == nsub
            rows = slice((nsub - 1) * QB, nsub * QB) if nxt else slice(qi * QB, (qi + 1) * QB)
            prev_rows = slice((qi - 1) * QB, qi * QB)
            if qi == 0:
                mask = jnp.logical_and(band, ci >= jnp.where(first, QB, 0))
            elif nxt:
                mask = jnp.logical_and(band, ci <= jnp.where(last, -1, QB - 1))[0:QB, :]
            else:
                mask = band
            for j in range(4):
                lanes = slice(j * 128, (j + 1) * 128)
                if nxt:
                    q, do, lv, dl = qn_ref[:, lanes], dyn_ref[:, lanes], ln_ref[:, lanes], dn_ref[:, lanes]
                    k2, v2 = k_ref[prev_rows, lanes], v_ref[prev_rows, lanes]
                else:
                    q, do, lv, dl = q_ref[rows, lanes], dy_ref[rows, lanes], l_ref[rows, lanes], d_ref[rows, lanes]
                    if qi == 0:
                        k2 = jnp.concatenate([kp_ref[:, lanes], k_ref[rows, lanes]], axis=0)
                        v2 = jnp.concatenate([vp_ref[:, lanes], v_ref[rows, lanes]], axis=0)
                    else:
                        k2, v2 = k_ref[(qi - 1) * QB:(qi + 1) * QB, lanes], v_ref[(qi - 1) * QB:(qi + 1) * QB, lanes]
                q2, do2 = _stack_heads(q, lo1), _stack_heads(do, lo1)
                lt, dt = lv.T, dl.T
                lse2 = jnp.concatenate([lt[0:1], lt[64:65]], axis=1)
                dl2 = jnp.concatenate([dt[0:1], dt[64:65]], axis=1)
                sc = _dot(k2, q2, NT) * 0.125
                p = jnp.where(mask, jnp.exp(jnp.minimum(sc - lse2, 0.0)), 0.0)
                ds = (p * (_dot(v2, do2, NT) - dl2) * 0.125).astype(BF16)
                dk2 = _dot(ds, q2, NN)
                dv2 = _dot(p.astype(BF16), do2, NN)
                if qi >= 1:
                    dka[prev_rows, lanes] += dk2[0:QB]
                    dva[prev_rows, lanes] += dv2[0:QB]
                if not nxt:
                    dka[rows, lanes] = dk2[QB:]
                    dva[rows, lanes] = dv2[QB:]
                    dq = _dot(jnp.concatenate([ds[:, 0:QB], ds[:, QB:]], axis=0), _stack_heads(k2, lane_lo), TN)
                    o_ref[rows, lanes] = unrope(dq, rows).astype(BF16)

        for cc in range(4):
            lanes = slice(cc * 128, (cc + 1) * 128)
            o_ref[:, 512 + cc * 128:512 + (cc + 1) * 128] = unrope(dka[:, lanes], slice(None)).astype(BF16)
            o_ref[:, 1024 + cc * 128:1024 + (cc + 1) * 128] = dva[:, lanes].astype(BF16)

    prev = lambda n: jnp.maximum(n * nsub - 1, 0)
    nxt = lambda n: jnp.minimum(n * nsub + nsub, last_blk)
    cur = lambda cb: (lambda n: (n, cb))
    big = lambda cb: pl.BlockSpec((rb, DIL_W), cur(cb))
    small = lambda im: pl.BlockSpec((QB, DIL_W), im)
    tab = pl.BlockSpec((rb, 128), cur(0))
    return pl.pallas_call(
        body, name=name, grid=(s // rb,),
        in_specs=[big(0), big(1), big(2), small(lambda n: (prev(n), 1)), small(lambda n: (prev(n), 2)),
                  small(lambda n: (nxt(n), 0)), big(0), small(lambda n: (nxt(n), 0)), big(0), small(lambda n: (nxt(n), 0)),
                  big(0), small(lambda n: (nxt(n), 0)), tab, tab, tab],
        out_specs=pl.BlockSpec((rb, 3 * DIL_W), cur(0)),
        out_shape=jax.ShapeDtypeStruct((s, 3 * DIL_W), BF16),
        scratch_shapes=[pltpu.VMEM((rb, DIL_W), F32), pltpu.VMEM((rb, DIL_W), F32)],
        compiler_params=_cparams(("parallel",)),
    )(qkv, qkv, qkv, qkv, qkv, qkv, dya, dya, lse, lse, dlt, dlt, tc, ts1, ts2)


def _stream_specs(tr):
    nat = pl.BlockSpec((tr, 128), lambda i, j: (i, j))
    return [nat] + [pl.BlockSpec((dil, tr // dil, 128), lambda i, j: (0, i, j)) for dil in DIL_GROUPS[1:]]


def _dil_merge(o_g, l_g, s):
    tr = min(2048, s)
    nat, sp4, sp16 = _stream_specs(tr)

    def body(o0_ref, l0_ref, o1_ref, l1_ref, o2_ref, l2_ref, ya_ref, lse_ref, o1n, l1n, o2n, l2n):
        for src, dst, dil in ((o1_ref, o1n, 4), (l1_ref, l1n, 4), (o2_ref, o2n, 16), (l2_ref, l2n, 16)):
            for c in range(dil):
                dst[pl.ds(c, tr // dil, stride=dil), :] = src[c]
        l0, l1, l2 = l0_ref[...], l1n[...], l2n[...]
        m = jnp.maximum(jnp.maximum(l0, l1), l2)
        e0, e1, e2 = jnp.exp(l0 - m), jnp.exp(l1 - m), jnp.exp(l2 - m)
        den = e0 + e1 + e2
        ya_ref[...] = ((e0 * o0_ref[...] + e1 * o1n[...] + e2 * o2n[...]) / den).astype(BF16)
        lse_ref[...] = m + jnp.log(den)

    v3 = lambda a, dil: a.reshape(dil, s // dil, DIL_W)
    return pl.pallas_call(
        body, name="dil_merge", grid=(s // tr, 4),
        in_specs=[nat, nat, sp4, sp4, sp16, sp16], out_specs=[nat, nat],
        out_shape=[jax.ShapeDtypeStruct((s, DIL_W), BF16), jax.ShapeDtypeStruct((s, DIL_W), F32)],
        scratch_shapes=[pltpu.VMEM((tr, 128), F32)] * 4,
        compiler_params=_cparams(("parallel", "parallel")),
    )(o_g[0], l_g[0], v3(o_g[1], 4), v3(l_g[1], 4), v3(o_g[2], 16), v3(l_g[2], 16))


def _dil_bwd_prep(d_ya, ya, lse, s):
    tr = min(2048, s)
    nat, sp4, sp16 = _stream_specs(tr)

    def body(dya_ref, ya_ref, lse_ref, dy0, dl0, dy1, ls1, dl1, dy2, ls2, dl2, dlt):
        lane_lo = lax.broadcasted_iota(jnp.int32, (tr, 128), 1) < 64
        prod = dya_ref[...] * ya_ref[...].astype(F32)
        lo = jnp.where(lane_lo, prod, 0.0)
        dlt[...] = jnp.where(lane_lo, jnp.sum(lo, axis=1, keepdims=True), jnp.sum(prod - lo, axis=1, keepdims=True))
        dy0[...] = dya_ref[...].astype(BF16)
        dl0[...] = dlt[...]
        for dil, dy, ls, dl in ((4, dy1, ls1, dl1), (16, dy2, ls2, dl2)):
            for c in range(dil):
                rows = pl.ds(c, tr // dil, stride=dil)
                dy[c] = dya_ref[rows, :].astype(BF16)
                ls[c] = lse_ref[rows, :]
                dl[c] = dlt[rows, :]

    sh = lambda dil, dt: jax.ShapeDtypeStruct((dil, s // dil, DIL_W), dt)
    res = pl.pallas_call(
        body, name="dil_bwd_prep", grid=(s // tr, 4),
        in_specs=[nat, nat, nat], out_specs=[nat, nat, sp4, sp4, sp4, sp16, sp16, sp16],
        out_shape=[jax.ShapeDtypeStruct((s, DIL_W), BF16), jax.ShapeDtypeStruct((s, DIL_W), F32),
                   sh(4, BF16), sh(4, F32), sh(4, F32), sh(16, BF16), sh(16, F32), sh(16, F32)],
        scratch_shapes=[pltpu.VMEM((tr, 128), F32)],
        compiler_params=_cparams(("parallel", "parallel")),
    )(d_ya, ya, lse)
    dy0, dl0, dy1, ls1, dl1, dy2, ls2, dl2 = [r.reshape(s, DIL_W) for r in res]
    return [(dy0, lse, dl0), (dy1, ls1, dl1), (dy2, ls2, dl2)]


_RET_SEGS = ((0, 256), (1024, 256), (2048, 512), (4096, 512))


def _split_w_in(win):
    per_head = [win[a:a + RET_HEADS * n].reshape(RET_HEADS, n, D_MODEL) for a, n in _RET_SEGS]
    w_ret = jnp.concatenate(per_head, axis=1).reshape(RET_HEADS * 1536, D_MODEL)
    w_dil = [jnp.concatenate([win[a + DIL_W * g:a + DIL_W * (g + 1)] for a in (6144, 7680, 9216)], axis=0) for g in range(3)]
    return w_ret, win[10752:12800], w_dil


def _join_w_in(g_ret, g_gate, g_dil):
    g_ret = g_ret.reshape(RET_HEADS, 1536, D_MODEL)
    off = (0, 256, 512, 1024, 1536)
    parts = [g_ret[:, off[i]:off[i + 1]].reshape(-1, D_MODEL) for i in range(4)]
    dil = [g_dil[g][DIL_W * i:DIL_W * (i + 1)] for i in range(3) for g in range(3)]
    return jnp.concatenate(parts + dil + [g_gate], axis=0)


def _local_step(xs, pb, tgt, tabs, wts, vec, s, late_shards=None):
    tm = min(2048, s)
    tr = min(256, s)
    mm = functools.partial(_matmul, tm=tm)
    on_mesh = late_shards is not None
    wts = dict(wts)
    w_ret, w_gate, w_dil = _split_w_in(wts["w_in"])
    blocks = lambda g: g.reshape(N_DEV, g.shape[0] // N_DEV, g.shape[1])

    u = _prenorm(xs, vec["g_pre_mix"], s)
    proj_ret = mm(u[0], w_ret, mode="nt", m=s, n=6144, k=1024, tn=1024, tk=1024, out_dtype=BF16, name="inproj_ret")
    proj_gate = mm(u[0], w_gate, mode="nt", m=s, n=2048, k=1024, tn=1024, tk=1024, out_dtype=BF16, name="inproj_gate")
    qkv = [_matmul(u[g], w_dil[g], mode="nt", m=s, n=1536, k=1024, tm=min(1024, s), tn=1536, tk=1024, out_dtype=BF16,
                   name="inproj_dil%d" % g, epi=tabs["dil"][g], epi_width=128, epi_fn=_rope_qk) for g in range(3)]

    names = list(late_shards) if on_mesh else []
    gather = _Exchange([late_shards[n] for n in names], [False] * len(names)) if on_mesh else None
    (yr, y_ret, rstate), gathered = _ret_fwd(proj_ret, tabs["cos_r"], tabs["sin_r"], s, carry=gather)
    wts.update({n: g.reshape(N_DEV * g.shape[1], g.shape[2]) for n, g in zip(names, gathered)})
    a_br = mm(yr, wts["w_ret_out"], mode="nn", m=s, n=1024, k=2048, tn=1024, tk=1024, out_dtype=BF16, name="ret_out")

    o_g, l_g = [], []
    for g, dil in enumerate(DIL_GROUPS):
        o, l = _dil_fwd(qkv[g], dil, s, "dil_fwd%d" % g)
        o_g.append(o)
        l_g.append(l)
    ya, lse = _dil_merge(o_g, l_g, s)
    b_br = mm(ya, wts["w_dil_out"], mode="nt", m=s, n=1024, k=512, tn=1024, tk=512, out_dtype=BF16, name="dil_out")

    def gate_mix(a, b, gr, ga, b0, b1):
        return [_sigmoid(gr.astype(F32) + b0) * a.astype(F32) + _sigmoid(ga.astype(F32) + b1) * b.astype(F32)], []

    (mixed,), _ = _rowwise("gate_mix", gate_mix, s, tr, [(a_br, 1024, 0), (b_br, 1024, 0), (proj_gate, 1024, 0), (proj_gate, 1024, 1)],
                           [vec["b0"], vec["b1"]], [(1024, BF16)])
    z = mm(mixed, wts["w_o"], mode="nn", m=s, n=1024, k=1024, tn=1024, tk=1024, out_dtype=BF16, name="w_o")

    def post_norm(h, f, g_post, g_pre):
        hn = h + _rms(f) * g_post
        return [hn, _rms(hn) * g_pre], []

    (h1, v2), _ = _rowwise("post_mix", post_norm, s, tr, [(xs, 1024, 0), (z, 1024, 0)], [vec["g_post_mix"], vec["g_pre_mlp"]],
                           [(1024, F32), (1024, BF16)])
    a_up = mm(v2, wts["w_up"], mode="nt", m=s, n=4096, k=1024, tn=1024, tk=1024, out_dtype=BF16, name="mlp_up")
    f_dn = mm(a_up, wts["w_down"], mode="nn", m=s, n=1024, k=4096, tn=1024, tk=1024, out_dtype=BF16, name="mlp_down", a_fn=_relu_sq)
    (h2, t_ple), _ = _rowwise("post_mlp", post_norm, s, tr, [(h1, 1024, 0), (f_dn, 1024, 0)], [vec["g_post_mlp"], vec["g_pre_ple"]],
                              [(1024, F32), (1024, BF16)])
    gl = mm(t_ple, wts["w_ple_gate"], mode="nn", m=s, n=1024, k=1024, tn=1024, tk=1024, out_dtype=BF16, name="ple_gate")
    e_ple = mm(pb, wts["w_ple_in"], mode="nt", m=s, n=1024, k=256, tn=1024, tk=256, out_dtype=BF16, name="ple_in")

    def ple_loss(h, glv, e, tg, b, g):
        gate = _sigmoid(glv + b)
        ge = gate * e
        diff = h + _rms(ge) * g - tg
        dy = diff * (1.0 / D_MODEL)
        d_ge, dg = _rms_bwd(ge, g, dy)
        d_gl = d_ge * e * gate * (1.0 - gate)
        loss = jnp.zeros((1, D_MODEL), F32) + 0.5 * jnp.sum(diff * diff) * (1.0 / D_MODEL)
        return [dy, d_gl, d_ge * gate], [_colsum(dg), _colsum(d_gl), loss]

    (dy, d_gl, d_e), (dg_post_ple, db_ple, loss) = _rowwise(
        "ple_loss", ple_loss, s, tr, [(h2, 1024, 0), (gl, 1024, 0), (e_ple, 1024, 0), (tgt, 1024, 0)],
        [vec["b_ple"], vec["g_post_ple"]], [(1024, F32), (1024, BF16), (1024, BF16)], [1024, 1024, 1024])

    ts = min(1024, s)
    wg = functools.partial(_matmul, mode="tn", k=s, tk=ts, out_dtype=BF16)
    grads = {}
    grads["w_ple_in"] = wg(d_e, pb, m=1024, n=256, tm=1024, tn=256, name="g_ple_in")
    grads["w_ple_gate"] = wg(t_ple, d_gl, m=1024, n=1024, tm=1024, tn=1024, name="g_ple_gate")
    d_t = mm(d_gl, wts["w_ple_gate"], mode="nt", m=s, n=1024, k=1024, tn=1024, tk=1024, out_dtype=BF16, name="d_t")

    def bwd_ple_mlp(h, dt, dyv, f, g_pre, g_post):
        dx, dg1 = _rms_bwd(h, g_pre, dt)
        dh = dyv + dx
        df, dg2 = _rms_bwd(f, g_post, dh)
        return [dh, df], [_colsum(dg1), _colsum(dg2)]

    (d_h2, d_f), (dg_pre_ple, dg_post_mlp) = _rowwise(
        "bwd_ple_mlp", bwd_ple_mlp, s, tr, [(h2, 1024, 0), (d_t, 1024, 0), (dy, 1024, 0), (f_dn, 1024, 0)],
        [vec["g_pre_ple"], vec["g_post_mlp"]], [(1024, F32), (1024, BF16)], [1024, 1024])
    d_a = mm(d_f, wts["w_down"], mode="nt", m=s, n=4096, k=1024, tn=1024, tk=1024, out_dtype=BF16, name="d_a",
             epi=(a_up,), epi_fn=lambda acc, av: acc * (2.0 * jnp.maximum(av.astype(F32), 0.0)))
    grads["w_down"] = wg(a_up, d_f, m=4096, n=1024, tm=2048, tn=1024, name="g_down", a_fn=_relu_sq)
    grads["w_up"] = wg(d_a, v2, m=4096, n=1024, tm=2048, tn=1024, name="g_up")
    d_v2 = mm(d_a, wts["w_up"], mode="nn", m=s, n=1024, k=4096, tn=1024, tk=1024, out_dtype=BF16, name="d_v2")

    (d_h1, d_z), (dg_pre_mlp, dg_post_mix) = _rowwise(
        "bwd_mlp_mix", bwd_ple_mlp, s, tr, [(h1, 1024, 0), (d_v2, 1024, 0), (d_h2, 1024, 0), (z, 1024, 0)],
        [vec["g_pre_mlp"], vec["g_post_mix"]], [(1024, F32), (1024, BF16)], [1024, 1024])
    d_mixed = mm(d_z, wts["w_o"], mode="nt", m=s, n=1024, k=1024, tn=1024, tk=1024, out_dtype=BF16, name="d_mixed")
    grads["w_o"] = wg(mixed, d_z, m=1024, n=1024, tm=1024, tn=1024, name="g_o")

    def bwd_gate(dm, a, b, gr, ga, b0, b1):
        sa, sb = _sigmoid(gr.astype(F32) + b0), _sigmoid(ga.astype(F32) + b1)
        dgr = dm * a.astype(F32) * sa * (1.0 - sa)
        dga = dm * b.astype(F32) * sb * (1.0 - sb)
        return [dm * sa, dm * sb, jnp.concatenate([dgr, dga], axis=1)], [_colsum(dgr), _colsum(dga)]

    (d_abr, d_bbr, dproj_gate), (db0, db1) = _rowwise(
        "bwd_gate", bwd_gate, s, tr, [(d_mixed, 1024, 0), (a_br, 1024, 0), (b_br, 1024, 0), (proj_gate, 1024, 0), (proj_gate, 1024, 1)],
        [vec["b0"], vec["b1"]], [(1024, BF16), (1024, BF16), (2048, BF16)], [1024, 1024])
    grads["w_ret_out"] = wg(yr, d_abr, m=2048, n=1024, tm=2048, tn=1024, name="g_ret_out")
    d_yr = mm(d_abr, wts["w_ret_out"], mode="nt", m=s, n=2048, k=1024, tn=1024, tk=1024, out_dtype=BF16, name="d_yr")
    grads["w_dil_out"] = wg(d_bbr, ya, m=1024, n=512, tm=1024, tn=512, name="g_dil_out")
    d_ya = mm(d_bbr, wts["w_dil_out"], mode="nn", m=s, n=512, k=1024, tn=512, tk=1024, out_dtype=F32, name="d_ya")

    slots = {}
    names = list(grads) if on_mesh else []
    shares = _Exchange([blocks(grads[n]) for n in names], [True] * len(names)) if on_mesh else None
    (dproj_ret,), got = _ret_bwd(proj_ret, tabs["cos_r"], tabs["sin_r"], y_ret, d_yr, rstate, s, carry=shares)
    slots.update(zip(names, got))
    upstream = _dil_bwd_prep(d_ya, ya, lse, s)
    dqkv = [_dil_bwd(qkv[g], *upstream[g], *tabs["dil"][g], dil, s, "dil_bwd%d" % g)
            for g, dil in enumerate(DIL_GROUPS)]

    g_ret = wg(dproj_ret, u[0], m=6144, n=1024, tm=2048, tn=1024, name="g_in_ret")
    g_gate = wg(dproj_gate, u[0], m=2048, n=1024, tm=2048, tn=1024, name="g_in_gate")
    g_dil = [wg(dqkv[g], u[g], m=1536, n=1024, tm=1536, tn=1024, name="g_in_dil%d" % g) for g in range(3)]
    grads["w_in"] = _join_w_in(g_ret, g_gate, g_dil)

    du_ret = functools.partial(mm, dproj_ret, w_ret, mode="nn", m=s, n=1024, k=6144, tn=1024, tk=1024, out_dtype=BF16, name="du_ret")
    if on_mesh:
        du_ret, (slots["w_in"],) = du_ret(carry=_Exchange([blocks(grads["w_in"])], [True]))
    else:
        du_ret = du_ret()
    du_gate = mm(dproj_gate, w_gate, mode="nn", m=s, n=1024, k=2048, tn=1024, tk=1024, out_dtype=BF16, name="du_gate")
    du_dil = [mm(dqkv[g], w_dil[g], mode="nn", m=s, n=1024, k=1536, tn=1024, tk=1536, out_dtype=BF16, name="du_dil%d" % g)
              for g in range(3)]

    grad_x, dg_pre_mix = _grad_x(xs, d_h1, (du_ret, du_gate, du_dil[0]), du_dil[1], du_dil[2], vec["g_pre_mix"], s)

    zero = jnp.zeros((1, D_MODEL), F32)
    packet = jnp.concatenate([dg_pre_mix, dg_post_mix, dg_pre_mlp, dg_post_mlp, dg_pre_ple, db_ple, dg_post_ple, loss,
                              db0, db1] + [zero] * 6, axis=0)
    return grad_x, (slots if on_mesh else grads), packet


def _mesh_pos():
    return lax.axis_index("x"), lax.axis_index("y"), lax.axis_index("c")


def _all_gather(shards):
    nw = len(shards)

    def body(*refs):
        ins, outs = refs[:nw], refs[nw:2 * nw]
        send_sems, recv_sems, local_sems = refs[2 * nw:]
        x, y, c = _mesh_pos()
        me, sibling = (x, y, c), (x, y, 1 - c)
        chips = [(1 - x, y), (x, 1 - y), (1 - x, 1 - y)]

        def region(w, dev):
            return outs[w].at[4 * dev[0] + 2 * dev[1] + dev[2]]

        def copy(w, kk, block, to, src=None):
            return pltpu.make_async_remote_copy(
                src_ref=region(w, block) if src is None else src, dst_ref=region(w, block),
                send_sem=send_sems.at[w * 7 + kk], recv_sem=recv_sems.at[w * 7 + kk], device_id=to, device_id_type=MESH)

        mine = [pltpu.make_async_copy(ins[w], region(w, me), local_sems.at[w]) for w in range(nw)]
        for cp in mine:
            cp.start()
        first = []
        for w in range(nw):
            first.append(copy(w, 0, me, sibling, src=ins[w]))
            first += [copy(w, 1 + j, me, (*chip, c), src=ins[w]) for j, chip in enumerate(chips)]
        for cp in first:
            cp.start()
        passed = []
        for j, chip in enumerate(chips):
            for w in range(nw):
                copy(w, 1 + j, (*chip, c), me).wait_recv()
                cp = copy(w, 4 + j, (*chip, c), sibling)
                cp.start()
                passed.append(cp)
        for w in range(nw):
            copy(w, 0, sibling, me).wait_recv()
            for j, chip in enumerate(chips):
                copy(w, 4 + j, (*chip, 1 - c), me).wait_recv()
        for cp in first + passed:
            cp.wait_send()
        for cp in mine:
            cp.wait()

    hbm = pl.BlockSpec(memory_space=pl.ANY)
    return pl.pallas_call(
        body, name="gather_weights",
        in_specs=[hbm] * nw, out_specs=[hbm] * nw,
        out_shape=[jax.ShapeDtypeStruct((N_DEV,) + sh.shape, sh.dtype) for sh in shards],
        scratch_shapes=[pltpu.SemaphoreType.DMA((nw * 7,)), pltpu.SemaphoreType.DMA((nw * 7,)), pltpu.SemaphoreType.DMA((nw,))],
    )(*shards)


class _Exchange:
    def __init__(self, arrays, scatter):
        self.arrays, self.scatter, self.n = list(arrays), list(scatter), len(arrays)
        self.out_shape = [jax.ShapeDtypeStruct(a.shape if sc else (N_DEV,) + a.shape, a.dtype)
                          for a, sc in zip(self.arrays, self.scatter)]
        self.scratch = [pltpu.SemaphoreType.DMA((self.n * 7,)), pltpu.SemaphoreType.DMA((self.n * 7,)),
                        pltpu.SemaphoreType.DMA((self.n,))]
        self.specs = [pl.BlockSpec(memory_space=pl.ANY)] * self.n

    def _copies(self, srcs, dsts, sems):
        send_sems, recv_sems, local_sems = sems
        x, y, c = _mesh_pos()
        my = 4 * x + 2 * y + c
        src_of = lambda w, idx: srcs[w].at[idx] if self.scatter[w] else srcs[w]
        local = [pltpu.make_async_copy(src_of(w, my), dsts[w].at[my], local_sems.at[w]) for w in range(self.n)]
        sends, recvs = [], []
        for w in range(self.n):
            for r in range(1, N_DEV):
                px = 1 - x if r & 4 else x
                py = 1 - y if r & 2 else y
                pc = 1 - c if r & 1 else c
                pidx = 4 * px + 2 * py + pc
                kw = dict(send_sem=send_sems.at[w * 7 + r - 1], recv_sem=recv_sems.at[w * 7 + r - 1],
                          device_id=(px, py, pc), device_id_type=MESH)
                sends.append(pltpu.make_async_remote_copy(src_ref=src_of(w, pidx), dst_ref=dsts[w].at[my], **kw))
                recvs.append(pltpu.make_async_remote_copy(src_ref=src_of(w, pidx), dst_ref=dsts[w].at[pidx], **kw))
        return local, sends, recvs

    def start(self, srcs, dsts, sems):
        local, sends, _ = self._copies(srcs, dsts, sems)
        for cp in local + sends:
            cp.start()

    def wait(self, srcs, dsts, sems):
        local, sends, recvs = self._copies(srcs, dsts, sems)
        for cp in recvs:
            cp.wait_recv()
        for cp in sends:
            cp.wait_send()
        for cp in local:
            cp.wait()

    def split(self, refs, n_in, n_out):
        srcs = refs[n_in:n_in + self.n]
        dsts = refs[n_in + self.n + n_out:n_in + 2 * self.n + n_out]
        return srcs, dsts, refs[len(refs) - 3:]


def _run_exchange(ex, name):
    def body(*refs):
        parts = ex.split(refs, 0, 0)
        ex.start(*parts)
        ex.wait(*parts)

    return pl.pallas_call(body, name=name, in_specs=ex.specs, out_specs=ex.specs, out_shape=ex.out_shape,
                          scratch_shapes=ex.scratch)(*ex.arrays)


def _pick_rows(r, c, target_bytes):
    t = r
    while (t // 2) % 16 == 0 and t // 2 >= 16 and t * c * 4 > target_bytes:
        t //= 2
    return t


def _sum_slots(slots, name):
    ns, r, c = slots.shape
    tr = _pick_rows(r, c, 256 * 1024)

    def body(s_ref, o_ref):
        acc = s_ref[0].astype(F32)
        for kk in range(1, ns):
            acc = acc + s_ref[kk].astype(F32)
        o_ref[...] = acc

    return pl.pallas_call(
        body, name=name, grid=(r // tr,),
        in_specs=[pl.BlockSpec((ns, tr, c), lambda i: (0, i, 0))], out_specs=pl.BlockSpec((tr, c), lambda i: (i, 0)),
        out_shape=jax.ShapeDtypeStruct((r, c), F32), compiler_params=_cparams(("parallel",)),
    )(slots)


def _adamw(slots, w, m, v, name):
    ns, r, c = slots.shape
    tr = _pick_rows(r, c, 256 * 1024)

    def body(s_ref, w_ref, m_ref, v_ref, g_out, d_out, m_out, v_out):
        g = s_ref[0].astype(F32)
        for kk in range(1, ns):
            g = g + s_ref[kk].astype(F32)
        mn = ADAM_B1 * m_ref[...] + (1.0 - ADAM_B1) * g
        vn = ADAM_B2 * v_ref[...] + (1.0 - ADAM_B2) * (g * g)
        m_hat = mn / (1.0 - ADAM_B1 ** ADAM_STEP)
        v_hat = vn / (1.0 - ADAM_B2 ** ADAM_STEP)
        g_out[...] = g
        d_out[...] = -ADAM_LR * (m_hat / (jnp.sqrt(v_hat) + ADAM_EPS) + ADAM_WD * w_ref[...])
        m_out[...] = mn
        v_out[...] = vn

    blk = pl.BlockSpec((tr, c), lambda i: (i, 0))
    return pl.pallas_call(
        body, name=name, grid=(r // tr,),
        in_specs=[pl.BlockSpec((ns, tr, c), lambda i: (0, i, 0)), blk, blk, blk], out_specs=[blk] * 4,
        out_shape=[jax.ShapeDtypeStruct((r, c), F32)] * 4, compiler_params=_cparams(("parallel",)),
    )(slots, w, m, v)


def _rotary_tables(pos, s):
    posf = pos.astype(F32)
    inv_freq = 1.0 / (10000.0 ** jnp.linspace(0.0, 1.0, RET_QK // 2, dtype=F32))
    ang = posf[:, None] * inv_freq
    tabs = {"cos_r": jnp.cos(ang), "sin_r": jnp.sin(ang), "dil": []}
    freqs = 500000.0 ** (-jnp.arange(0, 16, 2, dtype=F32) / 16)
    spread = np.zeros((16, 384), np.float32)
    bias = np.zeros((1, 384), np.float32)
    for head in range(2):
        for i in range(8):
            spread[i, 64 * head + i] = spread[i, 64 * head + 8 + i] = 1.0
            spread[8 + i, 128 + 64 * head + i] = -1.0
            spread[8 + i, 256 + 64 * head + 8 + i] = 1.0
        bias[0, 64 * head + 16:64 * head + 64] = 1.0

    def expand(t, e, b):
        hi = t.astype(BF16)
        lo = (t - hi.astype(F32)).astype(BF16)
        out = _dot(hi, e, NN) + _dot(lo, e, NN) + b
        return [out[:, 0:128], out[:, 128:256], out[:, 256:384]], []

    for g, dil in enumerate(DIL_GROUPS):
        ang = posf.reshape(s // dil, dil).T.reshape(s, 1) * freqs
        cs = jnp.concatenate([jnp.cos(ang), jnp.sin(ang)], axis=1)
        t3, _ = _rowwise("rot_tables%d" % g, expand, s, min(1024, s), [(cs, 16, 0)],
                         [jnp.asarray(spread, BF16), jnp.asarray(bias)], [(128, F32)] * 3)
        tabs["dil"].append(tuple(t3))
    return tabs


_TRANSPOSED = ("w_in", "w_dil_out", "w_up", "w_ple_in")
_MATS = ("w_in", "w_ret_out", "w_dil_out", "w_o", "w_up", "w_down", "w_ple_gate", "w_ple_in")
_VECS = ("g_pre_mix", "g_post_mix", "g_pre_mlp", "g_post_mlp", "g_pre_ple", "b_ple_gate", "g_post_ple")
_ORDER = ("w_in", "b_gate", "w_ret_out", "w_dil_out", "w_o", "g_pre_mix", "g_post_mix", "g_pre_mlp", "g_post_mlp", "w_up",
          "w_down", "g_pre_ple", "w_ple_gate", "b_ple_gate", "w_ple_in", "g_post_ple")


def kernel(x, p, positions, w_in, b_gate, w_ret_out, w_dil_out, w_o, g_pre_mix, g_post_mix, g_pre_mlp, g_post_mlp, w_up, w_down, g_pre_ple, w_ple_gate, b_ple_gate, w_ple_in, g_post_ple, loss_target, m_w_in, m_b_gate, m_w_ret_out, m_w_dil_out, m_w_o, m_g_pre_mix, m_g_post_mix, m_g_pre_mlp, m_g_post_mlp, m_w_up, m_w_down, m_g_pre_ple, m_w_ple_gate, m_b_ple_gate, m_w_ple_in, m_g_post_ple, v_w_in, v_b_gate, v_w_ret_out, v_w_dil_out, v_w_o, v_g_pre_mix, v_g_post_mix, v_g_pre_mlp, v_g_post_mlp, v_w_up, v_w_down, v_g_pre_ple, v_w_ple_gate, v_b_ple_gate, v_w_ple_in, v_g_post_ple):
    s = x.shape[1]
    wd = dict(w_in=w_in, b_gate=b_gate, w_ret_out=w_ret_out, w_dil_out=w_dil_out, w_o=w_o, g_pre_mix=g_pre_mix,
              g_post_mix=g_post_mix, g_pre_mlp=g_pre_mlp, g_post_mlp=g_post_mlp, w_up=w_up, w_down=w_down,
              g_pre_ple=g_pre_ple, w_ple_gate=w_ple_gate, b_ple_gate=b_ple_gate, w_ple_in=w_ple_in, g_post_ple=g_post_ple)
    md = dict(w_in=m_w_in, b_gate=m_b_gate, w_ret_out=m_w_ret_out, w_dil_out=m_w_dil_out, w_o=m_w_o, g_pre_mix=m_g_pre_mix,
              g_post_mix=m_g_post_mix, g_pre_mlp=m_g_pre_mlp, g_post_mlp=m_g_post_mlp, w_up=m_w_up, w_down=m_w_down,
              g_pre_ple=m_g_pre_ple, w_ple_gate=m_w_ple_gate, b_ple_gate=m_b_ple_gate, w_ple_in=m_w_ple_in, g_post_ple=m_g_post_ple)
    vd = dict(w_in=v_w_in, b_gate=v_b_gate, w_ret_out=v_w_ret_out, w_dil_out=v_w_dil_out, w_o=v_w_o, g_pre_mix=v_g_pre_mix,
              g_post_mix=v_g_post_mix, g_pre_mlp=v_g_pre_mlp, g_post_mlp=v_g_post_mlp, w_up=v_w_up, w_down=v_w_down,
              g_pre_ple=v_g_pre_ple, w_ple_gate=v_w_ple_gate, b_ple_gate=v_b_ple_gate, w_ple_in=v_w_ple_in, g_post_ple=v_g_post_ple)

    shards = {n: (wd[n][0].T if n in _TRANSPOSED else wd[n][0]).astype(BF16) for n in _MATS}
    w_in_all, bg_all = _all_gather([shards.pop("w_in"), b_gate[0]])
    wts = {"w_in": w_in_all.reshape(N_DEV * w_in_all.shape[1], D_MODEL)}
    bg = bg_all.transpose(1, 0, 2).reshape(2, D_MODEL)
    vec = {n: wd[n] for n in _VECS}
    vec.update(b0=bg[0:1], b1=bg[1:2], b_ple=b_ple_gate)

    tabs = _rotary_tables(positions[0], s)
    grad_x, slots, packet = _local_step(x[0], p[0, 0].astype(BF16), loss_target[0], tabs, wts, vec, s, late_shards=shards)

    (packets,) = _run_exchange(_Exchange([packet], [False]), "exchange_vectors")
    out = {}
    for n in _MATS:
        sl = slots[n]
        if n in _TRANSPOSED:
            sl = _sum_slots(sl, "sum_" + n).T[None]
        out[n] = _adamw(sl, wd[n][0], md[n][0], vd[n][0], "adamw_" + n)
    zero_rows = jnp.zeros((16 - len(_VECS), D_MODEL), F32)
    pack = lambda d: jnp.concatenate([d[n] for n in _VECS] + [zero_rows], axis=0)
    small = _adamw(packets, pack(wd), pack(md), pack(vd), "adamw_vectors")
    for i, n in enumerate(_VECS):
        out[n] = tuple(t[i:i + 1] for t in small)
    my = 4 * lax.axis_index("x") + 2 * lax.axis_index("y") + lax.axis_index("c")
    g_bias = lax.dynamic_slice(small[0], (8, my * 128), (2, 128))
    out["b_gate"] = _adamw(g_bias[None], b_gate[0], m_b_gate[0], v_b_gate[0], "adamw_b_gate")
    loss = small[0][7, 0]

    res = [loss, grad_x[None]]
    for kk in range(4):
        res += [out[n][kk][None] if out[n][kk].ndim == 2 and wd[n].ndim == 3 else out[n][kk] for n in _ORDER]
    return tuple(res)
```

```python
import functools
import math

import numpy as np
import jax
import jax.numpy as jnp
from jax import lax
from jax.experimental import pallas as pl
from jax.experimental.pallas import tpu as pltpu

F32, BF16 = jnp.float32, jnp.bfloat16
D_MODEL = 1024
EPS = 1e-6
N_DEV = 8
RET_HEADS, RET_QK, RET_V, RET_CHUNK = 4, 256, 512, 128
DIL_GROUPS = (1, 4, 16)
DIL_W = 512
QB = 128
NEG = -1e30
ADAM_LR, ADAM_B1, ADAM_B2, ADAM_EPS, ADAM_WD, ADAM_STEP = 0.001, 0.9, 0.999, 1e-08, 0.01, 10
VMEM_LIMIT_BYTES = 56 * 1024 * 1024
MESH = pl.DeviceIdType.MESH

NN = ((1,), (0,))
NT = ((1,), (1,))
TN = ((0,), (0,))


def _dot(a, b, dn):
    return lax.dot_general(a, b, (dn, ((), ())), preferred_element_type=F32)


def _cparams(sem):
    return pltpu.CompilerParams(dimension_semantics=sem, vmem_limit_bytes=VMEM_LIMIT_BYTES)


def _rms(x):
    return x * lax.rsqrt(jnp.mean(x * x, axis=-1, keepdims=True) + EPS)


def _rms_bwd(x, g, dy):
    r = lax.rsqrt(jnp.mean(x * x, axis=-1, keepdims=True) + EPS)
    xh = x * r
    t = dy * g
    dx = r * (t - xh * jnp.mean(t * xh, axis=-1, keepdims=True))
    return dx, dy * xh


def _colsum(v):
    return jnp.sum(v, axis=0, keepdims=True)


def _sigmoid(v):
    return 1.0 / (1.0 + jnp.exp(-v))


def _pallas(compute, *, name, grid, in_specs, out_specs, out_shape, scratch, semantics, args, carry=None):
    n_in, n_out = len(in_specs), len(out_specs)
    if carry is None:
        res = pl.pallas_call(compute, name=name, grid=grid, in_specs=in_specs, out_specs=out_specs, out_shape=out_shape,
                             scratch_shapes=scratch, compiler_params=_cparams(semantics))(*args)
        return res, []
    n_steps = math.prod(grid)

    def body(*refs):
        step = 0
        for axis, size in enumerate(grid):
            step = step * size + pl.program_id(axis)
        parts = carry.split(refs, n_in, n_out)
        pl.when(step == 0)(lambda: carry.start(*parts))
        compute(*refs[:n_in], *refs[n_in + carry.n:n_in + carry.n + n_out], *refs[n_in + 2 * carry.n + n_out:len(refs) - 3])
        pl.when(step == n_steps - 1)(lambda: carry.wait(*parts))

    res = pl.pallas_call(
        body, name=name, grid=grid, in_specs=list(in_specs) + carry.specs, out_specs=list(out_specs) + carry.specs,
        out_shape=list(out_shape) + carry.out_shape, scratch_shapes=list(scratch) + carry.scratch,
        compiler_params=_cparams(("arbitrary",) * len(grid)))(*args, *carry.arrays)
    return res[:n_out], res[n_out:]


def _matmul(a, b, *, mode, m, n, k, tm, tn, tk, out_dtype, name, a_fn=None, epi=(), epi_width=None, epi_fn=None, carry=None):
    nk = k // tk
    grid = (m // tm, n // tn, nk)
    if mode == "nn":
        a_blk, a_im, b_blk, b_im, dn = (tm, tk), (lambda i, j, kk: (i, kk)), (tk, tn), (lambda i, j, kk: (kk, j)), NN
    elif mode == "nt":
        a_blk, a_im, b_blk, b_im, dn = (tm, tk), (lambda i, j, kk: (i, kk)), (tn, tk), (lambda i, j, kk: (j, kk)), NT
    else:
        a_blk, a_im, b_blk, b_im, dn = (tk, tm), (lambda i, j, kk: (kk, i)), (tk, tn), (lambda i, j, kk: (kk, j)), TN
    o_im = lambda i, j, kk: (i, j)
    n_in = 2 + len(epi)

    def body(*refs):
        a_ref, b_ref = refs[0], refs[1]
        o_ref = refs[n_in]
        acc_ref = refs[n_in + 1] if nk > 1 else None

        def finish(acc):
            if epi:
                acc = epi_fn(acc, *[r[...] for r in refs[2:n_in]])
            o_ref[...] = acc.astype(o_ref.dtype)

        av = a_ref[...]
        if a_fn is not None:
            av = a_fn(av)
        part = _dot(av, b_ref[...], dn)
        if nk == 1:
            finish(part)
        else:
            kk = pl.program_id(2)

            @pl.when(kk == 0)
            def _():
                acc_ref[...] = part

            @pl.when(kk > 0)
            def _():
                acc_ref[...] += part

            @pl.when(kk == nk - 1)
            def _():
                finish(acc_ref[...])

    epi_spec = pl.BlockSpec((tm, tn), o_im) if epi_width is None else pl.BlockSpec((tm, epi_width), lambda i, j, kk: (i, 0))
    in_specs = [pl.BlockSpec(a_blk, a_im), pl.BlockSpec(b_blk, b_im)] + [epi_spec] * len(epi)
    args = [a, b, *epi]
    (out,), got = _pallas(
        body, name=name, grid=grid, in_specs=in_specs, out_specs=[pl.BlockSpec((tm, tn), o_im)],
        out_shape=[jax.ShapeDtypeStruct((m, n), out_dtype)], scratch=[pltpu.VMEM((tm, tn), F32)] if nk > 1 else [],
        semantics=("parallel", "parallel", "arbitrary"), args=args, carry=carry)
    return out if carry is None else (out, got)


def _relu_sq(v):
    r = jnp.maximum(v.astype(F32), 0.0)
    return (r * r).astype(BF16)


def _rowwise(name, fn, s, tr, rows, vecs, outs, accs=()):
    n_r, n_v, n_o, n_a = len(rows), len(vecs), len(outs), len(accs)

    def body(*refs):
        vals = [refs[i][...].astype(F32) for i in range(n_r)] + [refs[n_r + i][...] for i in range(n_v)]
        o_refs = refs[n_r + n_v:n_r + n_v + n_o]
        a_refs = refs[n_r + n_v + n_o:]
        o_vals, a_vals = fn(*vals)
        for ref, val in zip(o_refs, o_vals):
            ref[...] = val.astype(ref.dtype)
        if n_a:
            @pl.when(pl.program_id(0) == 0)
            def _():
                for ref in a_refs:
                    ref[...] = jnp.zeros_like(ref)

            for ref, val in zip(a_refs, a_vals):
                ref[...] += val

    in_specs = [pl.BlockSpec((tr, w), functools.partial(lambda i, cb: (i, cb), cb=cb)) for _, w, cb in rows]
    in_specs += [pl.BlockSpec(v.shape, lambda i: (0, 0)) for v in vecs]
    out_specs = [pl.BlockSpec((tr, w), lambda i: (i, 0)) for w, _ in outs]
    out_specs += [pl.BlockSpec((1, w), lambda i: (0, 0)) for w in accs]
    out_shape = [jax.ShapeDtypeStruct((s, w), dt) for w, dt in outs]
    out_shape += [jax.ShapeDtypeStruct((1, w), F32) for w in accs]
    res = pl.pallas_call(
        body, name=name, grid=(s // tr,), in_specs=in_specs, out_specs=out_specs, out_shape=out_shape,
        compiler_params=_cparams(("arbitrary",)),
    )(*[r[0] for r in rows], *vecs)
    return res[:n_o], res[n_o:]


_ROW_TILE = 256
_STREAM_SPECS = [pl.BlockSpec((dil, _ROW_TILE // dil, D_MODEL), lambda i: (0, i, 0)) for dil in DIL_GROUPS[1:]]
_NAT_SPEC = pl.BlockSpec((_ROW_TILE, D_MODEL), lambda i: (i, 0))
_VEC_SPEC = pl.BlockSpec((1, D_MODEL), lambda i: (0, 0))
_COL_BLOCKS = pltpu.VMEM((D_MODEL // 128, _ROW_TILE, 128), F32)


def _prenorm(xs, g, s, carry=None):
    tr = _ROW_TILE

    def body(x_ref, g_ref, u_ref, u4_ref, u16_ref, buf):
        xn = _rms(x_ref[...]) * g_ref[...]
        u_ref[...] = xn.astype(BF16)
        for cb in range(8):
            buf[cb] = xn[:, cb * 128:(cb + 1) * 128]
        for dil, out in ((4, u4_ref), (16, u16_ref)):
            for c in range(dil):
                rows = pl.ds(c, tr // dil, stride=dil)
                out[c] = jnp.concatenate([buf.at[cb][rows, :] for cb in range(8)], axis=1).astype(BF16)

    res, got = _pallas(
        body, name="prenorm", grid=(s // tr,), in_specs=[_NAT_SPEC, _VEC_SPEC], out_specs=[_NAT_SPEC] + _STREAM_SPECS,
        out_shape=[jax.ShapeDtypeStruct((s, D_MODEL), BF16)]
        + [jax.ShapeDtypeStruct((dil, s // dil, D_MODEL), BF16) for dil in DIL_GROUPS[1:]],
        scratch=[_COL_BLOCKS], semantics=("parallel",), args=(xs, g), carry=carry)
    return [r.reshape(s, D_MODEL) for r in res], got


def _grad_x(xs, d_h1, du_nat, du4, du16, g, s):
    tr = _ROW_TILE

    def body(x_ref, dh_ref, a_ref, b_ref, c_ref, u4_ref, u16_ref, g_ref, dx_ref, dg_ref, buf):
        du = a_ref[...].astype(F32) + b_ref[...].astype(F32) + c_ref[...].astype(F32)
        for dil, src in ((4, u4_ref), (16, u16_ref)):
            for c in range(dil):
                part = src[c].astype(F32)
                for cb in range(8):
                    buf.at[cb][pl.ds(c, tr // dil, stride=dil), :] = part[:, cb * 128:(cb + 1) * 128]
            du = du + jnp.concatenate([buf[cb] for cb in range(8)], axis=1)
        dx, dgr = _rms_bwd(x_ref[...], g_ref[...], du)
        dx_ref[...] = dh_ref[...] + dx

        @pl.when(pl.program_id(0) == 0)
        def _():
            dg_ref[...] = jnp.zeros_like(dg_ref)

        dg_ref[...] += _colsum(dgr)

    return pl.pallas_call(
        body, name="grad_x", grid=(s // tr,), in_specs=[_NAT_SPEC] * 5 + _STREAM_SPECS + [_VEC_SPEC],
        out_specs=[_NAT_SPEC, _VEC_SPEC],
        out_shape=[jax.ShapeDtypeStruct((s, D_MODEL), F32), jax.ShapeDtypeStruct((1, D_MODEL), F32)],
        scratch_shapes=[_COL_BLOCKS], compiler_params=_cparams(("arbitrary",)),
    )(xs, d_h1, *du_nat, du4.reshape(4, s // 4, D_MODEL), du16.reshape(16, s // 16, D_MODEL), g)


def _ret_tables():
    h = np.arange(RET_HEADS, dtype=np.float32)
    lg = np.log1p(-(np.float32(2.0) ** (-5.0 - h))).astype(np.float32)
    idx = np.arange(RET_CHUNK, dtype=np.float32)
    diff = idx[:, None] - idx[None, :]
    dm = np.where(diff[None] >= 0, np.exp(np.maximum(diff, 0.0)[None] * lg[:, None, None]), 0.0)
    qd = np.exp((idx + 1.0)[None, :, None] * lg[:, None, None])
    kd = np.exp((RET_CHUNK - 1.0 - idx)[None, :, None] * lg[:, None, None])
    cd = np.exp(RET_CHUNK * lg)[:, None, None]
    return [jnp.asarray(t, F32) for t in (dm, qd, kd, cd)]


def _rope_half(v, cos, sin):
    v1, v2 = v[:, :128], v[:, 128:]
    return jnp.concatenate([v1 * cos - v2 * sin, v2 * cos + v1 * sin], axis=1)


def _unrope_half(d, cos, sin):
    d1, d2 = d[:, :128], d[:, 128:]
    return jnp.concatenate([d1 * cos + d2 * sin, d2 * cos - d1 * sin], axis=1)


_RET_HEADS_FWD, _RET_HEADS_BWD = 1, 2


def _ret_specs(rb, rev_n, hp):
    def rowmap(w_blk):
        return lambda h, n: (rev_n(n), w_blk(h))
    tab = [pl.BlockSpec((hp, RET_CHUNK, RET_CHUNK), lambda h, n: (h, 0, 0)),
           pl.BlockSpec((hp, RET_CHUNK, 1), lambda h, n: (h, 0, 0)),
           pl.BlockSpec((hp, RET_CHUNK, 1), lambda h, n: (h, 0, 0)),
           pl.BlockSpec((hp, 1, 1), lambda h, n: (h, 0, 0))]
    proj = pl.BlockSpec((rb, hp * 1536), rowmap(lambda h: h))
    cs = pl.BlockSpec((rb, 128), rowmap(lambda h: 0))
    hv = pl.BlockSpec((rb, hp * RET_V), rowmap(lambda h: h))
    return proj, cs, hv, tab


def _ret_fwd(proj_ret, cos, sin, s, carry=None):
    rb = min(512, s)
    ch = rb // RET_CHUNK
    nb = s // rb
    hp = _RET_HEADS_FWD
    proj_spec, cs_spec, hv_spec, tab_specs = _ret_specs(rb, lambda n: n, hp)

    def body(p_ref, cos_ref, sin_ref, dm_ref, qd_ref, kd_ref, cd_ref, yr_ref, y_ref, rs_ref, r_acc):
        @pl.when(pl.program_id(1) == 0)
        def _():
            r_acc[...] = jnp.zeros_like(r_acc)

        for c, hh in [(c, hh) for c in range(ch) for hh in range(hp)]:
            rows = slice(c * RET_CHUNK, (c + 1) * RET_CHUNK)
            pc, hc = hh * 1536, hh * RET_V
            dm, qd, kd, cd = dm_ref[hh], qd_ref[hh], kd_ref[hh], cd_ref[hh]
            cosv, sinv = cos_ref[rows, :], sin_ref[rows, :]
            q = _rope_half(p_ref[rows, pc:pc + 256].astype(F32), cosv, sinv)
            kk = _rope_half(p_ref[rows, pc + 256:pc + 512].astype(F32), cosv, sinv) * (RET_QK ** -0.5)
            v = p_ref[rows, pc + 512:pc + 1024]
            g = p_ref[rows, pc + 1024:pc + 1536].astype(F32)
            rb16 = r_acc[hh].astype(BF16)
            rs_ref[hh, c] = rb16
            sc = _dot(q.astype(BF16), kk.astype(BF16), NT) * dm
            y = _dot(sc.astype(BF16), v, NN) + _dot((q * qd).astype(BF16), rb16, NN)
            r_acc[hh] = r_acc[hh] * cd + _dot((kk * kd).astype(BF16), v, TN)
            y_ref[rows, hc:hc + RET_V] = y.astype(BF16)
            yr_ref[rows, hc:hc + RET_V] = (_rms(y) * (g * _sigmoid(g))).astype(BF16)

    return _pallas(
        body, name="ret_fwd", grid=(RET_HEADS // hp, nb),
        in_specs=[proj_spec, cs_spec, cs_spec] + tab_specs,
        out_specs=[hv_spec, hv_spec, pl.BlockSpec((hp, ch, RET_QK, RET_V), lambda h, n: (h, n, 0, 0))],
        out_shape=[jax.ShapeDtypeStruct((s, RET_HEADS * RET_V), BF16), jax.ShapeDtypeStruct((s, RET_HEADS * RET_V), BF16),
                   jax.ShapeDtypeStruct((RET_HEADS, s // RET_CHUNK, RET_QK, RET_V), BF16)],
        scratch=[pltpu.VMEM((hp, RET_QK, RET_V), F32)], semantics=("parallel", "arbitrary"),
        args=(proj_ret, cos, sin, *_ret_tables()), carry=carry)


def _ret_bwd(proj_ret, cos, sin, y, d_yr, rs, s, carry=None):
    rb = min(512, s)
    ch = rb // RET_CHUNK
    nb = s // rb
    hp = _RET_HEADS_BWD
    proj_spec, cs_spec, hv_spec, tab_specs = _ret_specs(rb, lambda n: nb - 1 - n, hp)

    def body(p_ref, cos_ref, sin_ref, y_ref, dyr_ref, rs_ref, dm_ref, qd_ref, kd_ref, cd_ref, o_ref, dr_acc):
        @pl.when(pl.program_id(1) == 0)
        def _():
            dr_acc[...] = jnp.zeros_like(dr_acc)

        for c, hh in [(c, hh) for c in reversed(range(ch)) for hh in range(hp)]:
            rows = slice(c * RET_CHUNK, (c + 1) * RET_CHUNK)
            pc, hc = hh * 1536, hh * RET_V
            dm, qd, kd, cd = dm_ref[hh], qd_ref[hh], kd_ref[hh], cd_ref[hh]
            cosv, sinv = cos_ref[rows, :], sin_ref[rows, :]
            q = _rope_half(p_ref[rows, pc:pc + 256].astype(F32), cosv, sinv)
            kk = _rope_half(p_ref[rows, pc + 256:pc + 512].astype(F32), cosv, sinv) * (RET_QK ** -0.5)
            v = p_ref[rows, pc + 512:pc + 1024]
            g = p_ref[rows, pc + 1024:pc + 1536].astype(F32)
            yv = y_ref[rows, hc:hc + RET_V].astype(F32)
            dyr = dyr_ref[rows, hc:hc + RET_V].astype(F32)
            sg = _sigmoid(g)
            r = lax.rsqrt(jnp.mean(yv * yv, axis=-1, keepdims=True) + EPS)
            yn = yv * r
            dg = dyr * yn * (sg * (1.0 + g * (1.0 - sg)))
            dyn = dyr * (g * sg)
            dy = (r * (dyn - yn * jnp.mean(dyn * yn, axis=-1, keepdims=True))).astype(BF16)
            qb, kb = q.astype(BF16), kk.astype(BF16)
            rb16 = rs_ref[hh, c]
            drb = dr_acc[hh].astype(BF16)
            sd = _dot(qb, kb, NT) * dm
            ds = (_dot(dy, v, NT) * dm).astype(BF16)
            dq = _dot(ds, kb, NN) + qd * _dot(dy, rb16, NT)
            dk = _dot(ds, qb, TN) + kd * _dot(v, drb, NT)
            dv = _dot(sd.astype(BF16), dy, TN) + _dot((kk * kd).astype(BF16), drb, NN)
            dr_acc[hh] = dr_acc[hh] * cd + _dot((q * qd).astype(BF16), dy, TN)
            o_ref[rows, pc:pc + 256] = _unrope_half(dq, cosv, sinv).astype(BF16)
            o_ref[rows, pc + 256:pc + 512] = (_unrope_half(dk, cosv, sinv) * (RET_QK ** -0.5)).astype(BF16)
            o_ref[rows, pc + 512:pc + 1024] = dv.astype(BF16)
            o_ref[rows, pc + 1024:pc + 1536] = dg.astype(BF16)

    in_specs = [proj_spec, cs_spec, cs_spec, hv_spec, hv_spec,
                pl.BlockSpec((hp, ch, RET_QK, RET_V), lambda h, n: (h, nb - 1 - n, 0, 0))] + tab_specs
    return _pallas(
        body, name="ret_bwd", grid=(RET_HEADS // hp, nb), in_specs=in_specs, out_specs=[proj_spec],
        out_shape=[jax.ShapeDtypeStruct((s, RET_HEADS * 1536), BF16)], scratch=[pltpu.VMEM((hp, RET_QK, RET_V), F32)],
        semantics=("parallel", "arbitrary"), args=(proj_ret, cos, sin, y, d_yr, rs, *_ret_tables()), carry=carry)


def _rope_qk(acc, c, s1, s2):
    outs = []
    for cc in range(8):
        vv = acc[:, cc * 128:(cc + 1) * 128]
        outs.append(vv * c + pltpu.roll(vv, 120, 1) * s1 + pltpu.roll(vv, 8, 1) * s2)
    return jnp.concatenate(outs + [acc[:, 2 * DIL_W:]], axis=1)


def _pair_masks(keys_on_rows=False):
    ri = lax.broadcasted_iota(jnp.int32, (2 * QB, 2 * QB), 1 if keys_on_rows else 0)
    ci = lax.broadcasted_iota(jnp.int32, (2 * QB, 2 * QB), 0 if keys_on_rows else 1)
    e = ci - (ri & (QB - 1))
    lane_lo = lax.broadcasted_iota(jnp.int32, (2 * QB, 128), 1) < 64
    return ci, jnp.logical_and(e >= 0, e <= QB), lane_lo


def _stack_heads(v, lane_lo):
    z = jnp.zeros_like(v)
    return jnp.concatenate([jnp.where(lane_lo, v, z), jnp.where(lane_lo, z, v)], axis=0)


def _dil_fwd(qkv, dil, s, name):
    length = s // dil
    rb = min(512, length)
    nsub = rb // QB
    nbs = length // rb
    sub_per = rb // QB

    def body(q_ref, k_ref, v_ref, kp_ref, vp_ref, o_ref, l_ref):
        first = (pl.program_id(0) % nbs) == 0
        ci, band, lane_lo = _pair_masks()
        lo1 = lane_lo[0:QB]

        for i in range(nsub):
            rows = slice(i * QB, (i + 1) * QB)
            mask = jnp.logical_and(band, ci >= jnp.where(first, QB, 0)) if i == 0 else band
            for j in range(4):
                lanes = slice(j * 128, (j + 1) * 128)
                q2 = _stack_heads(q_ref[rows, lanes], lo1)
                if i == 0:
                    k2 = jnp.concatenate([kp_ref[:, lanes], k_ref[rows, lanes]], axis=0)
                    v2 = jnp.concatenate([vp_ref[:, lanes], v_ref[rows, lanes]], axis=0)
                else:
                    k2, v2 = k_ref[(i - 1) * QB:(i + 1) * QB, lanes], v_ref[(i - 1) * QB:(i + 1) * QB, lanes]
                v2 = _stack_heads(v2, lane_lo)
                sc = jnp.where(mask, _dot(q2, k2, NT) * 0.125, NEG)
                m = jnp.max(sc, axis=1, keepdims=True)
                p = jnp.exp(sc - m)
                den = jnp.sum(p, axis=1, keepdims=True)
                pb = p.astype(BF16)
                o = _dot(jnp.concatenate([pb[0:QB], pb[QB:]], axis=1), v2, NN)
                inv = 1.0 / den
                lse = m + jnp.log(den)
                o_ref[rows, lanes] = o * jnp.where(lo1, inv[0:QB], inv[QB:])
                l_ref[rows, lanes] = jnp.where(lo1, lse[0:QB], lse[QB:])

    prev = lambda n: jnp.maximum(n * sub_per - 1, 0)
    cur = lambda cb: (lambda n: (n, cb))
    return pl.pallas_call(
        body, name=name, grid=(s // rb,),
        in_specs=[pl.BlockSpec((rb, DIL_W), cur(0)), pl.BlockSpec((rb, DIL_W), cur(1)), pl.BlockSpec((rb, DIL_W), cur(2)),
                  pl.BlockSpec((QB, DIL_W), lambda n: (prev(n), 1)), pl.BlockSpec((QB, DIL_W), lambda n: (prev(n), 2))],
        out_specs=[pl.BlockSpec((rb, DIL_W), cur(0)), pl.BlockSpec((rb, DIL_W), cur(0))],
        out_shape=[jax.ShapeDtypeStruct((s, DIL_W), F32), jax.ShapeDtypeStruct((s, DIL_W), F32)],
        compiler_params=_cparams(("parallel",)),
    )(qkv, qkv, qkv, qkv, qkv)


def _dil_bwd(qkv, dya, lse, dlt, tc, ts1, ts2, dil, s, name):
    length = s // dil
    rb = min(512, length)
    nsub = rb // QB
    nbs = length // rb
    last_blk = s // QB - 1

    def body(q_ref, k_ref, v_ref, kp_ref, vp_ref, qn_ref, dy_ref, dyn_ref, l_ref, ln_ref, d_ref, dn_ref,
             c_ref, s1_ref, s2_ref, o_ref, dka, dva):
        nl = pl.program_id(0) % nbs
        first, last = nl == 0, nl == nbs - 1
        ci, band, lane_lo = _pair_masks(keys_on_rows=True)
        lo1 = lane_lo[0:QB]

        def unrope(d, rows):
            return d * c_ref[rows, :] + pltpu.roll(d * s1_ref[rows, :], 8, 1) + pltpu.roll(d * s2_ref[rows, :], 120, 1)

        for qi in range(nsub + 1):
            nxt = qi == nsub
            rows = slice((nsub - 1) * QB, nsub * QB) if nxt else slice(qi * QB, (qi + 1) * QB)
            prev_rows = slice((qi - 1) * QB, qi * QB)
            if qi == 0:
                mask = jnp.logical_and(band, ci >= jnp.where(first, QB, 0))
            elif nxt:
                mask = jnp.logical_and(band, ci <= jnp.where(last, -1, QB - 1))[0:QB, :]
            else:
                mask = band
            for j in range(4):
                lanes = slice(j * 128, (j + 1) * 128)
                if nxt:
                    q, do, lv, dl = qn_ref[:, lanes], dyn_ref[:, lanes], ln_ref[:, lanes], dn_ref[:, lanes]
                    k2, v2 = k_ref[prev_rows, lanes], v_ref[prev_rows, lanes]
                else:
                    q, do, lv, dl = q_ref[rows, lanes], dy_ref[rows, lanes], l_ref[rows, lanes], d_ref[rows, lanes]
                    if qi == 0:
                        k2 = jnp.concatenate([kp_ref[:, lanes], k_ref[rows, lanes]], axis=0)
                        v2 = jnp.concatenate([vp_ref[:, lanes], v_ref[rows, lanes]], axis=0)
                    else:
                        k2, v2 = k_ref[(qi - 1) * QB:(qi + 1) * QB, lanes], v_ref[(qi - 1) * QB:(qi + 1) * QB, lanes]
                q2, do2 = _stack_heads(q, lo1), _stack_heads(do, lo1)
                lt, dt = lv.T, dl.T
                lse2 = jnp.concatenate([lt[0:1], lt[64:65]], axis=1)
                dl2 = jnp.concatenate([dt[0:1], dt[64:65]], axis=1)
                sc = _dot(k2, q2, NT) * 0.125
                p = jnp.where(mask, jnp.exp(jnp.minimum(sc - lse2, 0.0)), 0.0)
                ds = (p * (_dot(v2, do2, NT) - dl2) * 0.125).astype(BF16)
                dk2 = _dot(ds, q2, NN)
                dv2 = _dot(p.astype(BF16), do2, NN)
                if qi >= 1:
                    dka[prev_rows, lanes] += dk2[0:QB]
                    dva[prev_rows, lanes] += dv2[0:QB]
                if not nxt:
                    dka[rows, lanes] = dk2[QB:]
                    dva[rows, lanes] = dv2[QB:]
                    dq = _dot(jnp.concatenate([ds[:, 0:QB], ds[:, QB:]], axis=0), _stack_heads(k2, lane_lo), TN)
                    o_ref[rows, lanes] = unrope(dq, rows).astype(BF16)

        for cc in range(4):
            lanes = slice(cc * 128, (cc + 1) * 128)
            o_ref[:, 512 + cc * 128:512 + (cc + 1) * 128] = unrope(dka[:, lanes], slice(None)).astype(BF16)
            o_ref[:, 1024 + cc * 128:1024 + (cc + 1) * 128] = dva[:, lanes].astype(BF16)

    prev = lambda n: jnp.maximum(n * nsub - 1, 0)
    nxt = lambda n: jnp.minimum(n * nsub + nsub, last_blk)
    cur = lambda cb: (lambda n: (n, cb))
    big = lambda cb: pl.BlockSpec((rb, DIL_W), cur(cb))
    small = lambda im: pl.BlockSpec((QB, DIL_W), im)
    tab = pl.BlockSpec((rb, 128), cur(0))
    return pl.pallas_call(
        body, name=name, grid=(s // rb,),
        in_specs=[big(0), big(1), big(2), small(lambda n: (prev(n), 1)), small(lambda n: (prev(n), 2)),
                  small(lambda n: (nxt(n), 0)), big(0), small(lambda n: (nxt(n), 0)), big(0), small(lambda n: (nxt(n), 0)),
                  big(0), small(lambda n: (nxt(n), 0)), tab, tab, tab],
        out_specs=pl.BlockSpec((rb, 3 * DIL_W), cur(0)),
        out_shape=jax.ShapeDtypeStruct((s, 3 * DIL_W), BF16),
        scratch_shapes=[pltpu.VMEM((rb, DIL_W), F32), pltpu.VMEM((rb, DIL_W), F32)],
        compiler_params=_cparams(("parallel",)),
    )(qkv, qkv, qkv, qkv, qkv, qkv, dya, dya, lse, lse, dlt, dlt, tc, ts1, ts2)


def _stream_specs(tr):
    nat = pl.BlockSpec((tr, 128), lambda i, j: (i, j))
    return [nat] + [pl.BlockSpec((dil, tr // dil, 128), lambda i, j: (0, i, j)) for dil in DIL_GROUPS[1:]]


def _dil_merge(o_g, l_g, s):
    tr = min(2048, s)
    nat, sp4, sp16 = _stream_specs(tr)

    def body(o0_ref, l0_ref, o1_ref, l1_ref, o2_ref, l2_ref, ya_ref, lse_ref, o1n, l1n, o2n, l2n):
        for src, dst, dil in ((o1_ref, o1n, 4), (l1_ref, l1n, 4), (o2_ref, o2n, 16), (l2_ref, l2n, 16)):
            for c in range(dil):
                dst[pl.ds(c, tr // dil, stride=dil), :] = src[c]
        l0, l1, l2 = l0_ref[...], l1n[...], l2n[...]
        m = jnp.maximum(jnp.maximum(l0, l1), l2)
        e0, e1, e2 = jnp.exp(l0 - m), jnp.exp(l1 - m), jnp.exp(l2 - m)
        den = e0 + e1 + e2
        ya_ref[...] = ((e0 * o0_ref[...] + e1 * o1n[...] + e2 * o2n[...]) / den).astype(BF16)
        lse_ref[...] = m + jnp.log(den)

    v3 = lambda a, dil: a.reshape(dil, s // dil, DIL_W)
    return pl.pallas_call(
        body, name="dil_merge", grid=(s // tr, 4),
        in_specs=[nat, nat, sp4, sp4, sp16, sp16], out_specs=[nat, nat],
        out_shape=[jax.ShapeDtypeStruct((s, DIL_W), BF16), jax.ShapeDtypeStruct((s, DIL_W), F32)],
        scratch_shapes=[pltpu.VMEM((tr, 128), F32)] * 4,
        compiler_params=_cparams(("parallel", "parallel")),
    )(o_g[0], l_g[0], v3(o_g[1], 4), v3(l_g[1], 4), v3(o_g[2], 16), v3(l_g[2], 16))


def _dil_bwd_prep(d_ya, ya, lse, s):
    tr = min(2048, s)
    nat, sp4, sp16 = _stream_specs(tr)

    def body(dya_ref, ya_ref, lse_ref, dy0, dl0, dy1, ls1, dl1, dy2, ls2, dl2, dlt):
        lane_lo = lax.broadcasted_iota(jnp.int32, (tr, 128), 1) < 64
        prod = dya_ref[...] * ya_ref[...].astype(F32)
        lo = jnp.where(lane_lo, prod, 0.0)
        dlt[...] = jnp.where(lane_lo, jnp.sum(lo, axis=1, keepdims=True), jnp.sum(prod - lo, axis=1, keepdims=True))
        dy0[...] = dya_ref[...].astype(BF16)
        dl0[...] = dlt[...]
        for dil, dy, ls, dl in ((4, dy1, ls1, dl1), (16, dy2, ls2, dl2)):
            for c in range(dil):
                rows = pl.ds(c, tr // dil, stride=dil)
                dy[c] = dya_ref[rows, :].astype(BF16)
                ls[c] = lse_ref[rows, :]
                dl[c] = dlt[rows, :]

    sh = lambda dil, dt: jax.ShapeDtypeStruct((dil, s // dil, DIL_W), dt)
    res = pl.pallas_call(
        body, name="dil_bwd_prep", grid=(s // tr, 4),
        in_specs=[nat, nat, nat], out_specs=[nat, nat, sp4, sp4, sp4, sp16, sp16, sp16],
        out_shape=[jax.ShapeDtypeStruct((s, DIL_W), BF16), jax.ShapeDtypeStruct((s, DIL_W), F32),
                   sh(4, BF16), sh(4, F32), sh(4, F32), sh(16, BF16), sh(16, F32), sh(16, F32)],
        scratch_shapes=[pltpu.VMEM((tr, 128), F32)],
        compiler_params=_cparams(("parallel", "parallel")),
    )(d_ya, ya, lse)
    dy0, dl0, dy1, ls1, dl1, dy2, ls2, dl2 = [r.reshape(s, DIL_W) for r in res]
    return [(dy0, lse, dl0), (dy1, ls1, dl1), (dy2, ls2, dl2)]


_RET_SEGS = ((0, 256), (1024, 256), (2048, 512), (4096, 512))


def _split_w_in(win):
    per_head = [win[a:a + RET_HEADS * n].reshape(RET_HEADS, n, D_MODEL) for a, n in _RET_SEGS]
    w_ret = jnp.concatenate(per_head, axis=1).reshape(RET_HEADS * 1536, D_MODEL)
    w_dil = [jnp.concatenate([win[a + DIL_W * g:a + DIL_W * (g + 1)] for a in (6144, 7680, 9216)], axis=0) for g in range(3)]
    return w_ret, win[10752:12800], w_dil


def _join_w_in(g_ret, g_gate, g_dil):
    g_ret = g_ret.reshape(RET_HEADS, 1536, D_MODEL)
    off = (0, 256, 512, 1024, 1536)
    parts = [g_ret[:, off[i]:off[i + 1]].reshape(-1, D_MODEL) for i in range(4)]
    dil = [g_dil[g][DIL_W * i:DIL_W * (i + 1)] for i in range(3) for g in range(3)]
    return jnp.concatenate(parts + dil + [g_gate], axis=0)


def _local_step(xs, pb, tgt, tabs, wts, vec, s, shards=None):
    tm = min(2048, s)
    tr = min(256, s)
    mm = functools.partial(_matmul, tm=tm)
    on_mesh = shards is not None
    wts, vec = dict(wts), dict(vec)
    blocks = lambda g: g.reshape(N_DEV, g.shape[0] // N_DEV, g.shape[1])

    late_shards = dict(shards) if on_mesh else {}
    first = _TwoLevelGather([late_shards.pop("w_in"), late_shards.pop("b_gate")]) if on_mesh else None
    u, gathered = _prenorm(xs, vec["g_pre_mix"], s, carry=first)
    if on_mesh:
        wts["w_in"] = gathered[0].reshape(N_DEV * gathered[0].shape[1], D_MODEL)
        bias = gathered[1].transpose(1, 0, 2).reshape(2, D_MODEL)
        vec.update(b0=bias[0:1], b1=bias[1:2])
    w_ret, w_gate, w_dil = _split_w_in(wts["w_in"])
    proj_ret = mm(u[0], w_ret, mode="nt", m=s, n=6144, k=1024, tn=1024, tk=1024, out_dtype=BF16, name="inproj_ret")
    proj_gate = mm(u[0], w_gate, mode="nt", m=s, n=2048, k=1024, tn=1024, tk=1024, out_dtype=BF16, name="inproj_gate")
    qkv = [_matmul(u[g], w_dil[g], mode="nt", m=s, n=1536, k=1024, tm=min(1024, s), tn=1536, tk=1024, out_dtype=BF16,
                   name="inproj_dil%d" % g, epi=tabs["dil"][g], epi_width=128, epi_fn=_rope_qk) for g in range(3)]

    names = list(late_shards) if on_mesh else []
    gather = _Exchange([late_shards[n] for n in names], [False] * len(names)) if on_mesh else None
    (yr, y_ret, rstate), gathered = _ret_fwd(proj_ret, tabs["cos_r"], tabs["sin_r"], s, carry=gather)
    wts.update({n: g.reshape(N_DEV * g.shape[1], g.shape[2]) for n, g in zip(names, gathered)})
    a_br = mm(yr, wts["w_ret_out"], mode="nn", m=s, n=1024, k=2048, tn=1024, tk=1024, out_dtype=BF16, name="ret_out")

    o_g, l_g = [], []
    for g, dil in enumerate(DIL_GROUPS):
        o, l = _dil_fwd(qkv[g], dil, s, "dil_fwd%d" % g)
        o_g.append(o)
        l_g.append(l)
    ya, lse = _dil_merge(o_g, l_g, s)
    b_br = mm(ya, wts["w_dil_out"], mode="nt", m=s, n=1024, k=512, tn=1024, tk=512, out_dtype=BF16, name="dil_out")

    def gate_mix(a, b, gr, ga, b0, b1):
        return [_sigmoid(gr.astype(F32) + b0) * a.astype(F32) + _sigmoid(ga.astype(F32) + b1) * b.astype(F32)], []

    (mixed,), _ = _rowwise("gate_mix", gate_mix, s, tr, [(a_br, 1024, 0), (b_br, 1024, 0), (proj_gate, 1024, 0), (proj_gate, 1024, 1)],
                           [vec["b0"], vec["b1"]], [(1024, BF16)])
    z = mm(mixed, wts["w_o"], mode="nn", m=s, n=1024, k=1024, tn=1024, tk=1024, out_dtype=BF16, name="w_o")

    def post_norm(h, f, g_post, g_pre):
        hn = h + _rms(f) * g_post
        return [hn, _rms(hn) * g_pre], []

    (h1, v2), _ = _rowwise("post_mix", post_norm, s, tr, [(xs, 1024, 0), (z, 1024, 0)], [vec["g_post_mix"], vec["g_pre_mlp"]],
                           [(1024, F32), (1024, BF16)])
    a_up = mm(v2, wts["w_up"], mode="nt", m=s, n=4096, k=1024, tn=1024, tk=1024, out_dtype=BF16, name="mlp_up")
    f_dn = mm(a_up, wts["w_down"], mode="nn", m=s, n=1024, k=4096, tn=1024, tk=1024, out_dtype=BF16, name="mlp_down", a_fn=_relu_sq)
    (h2, t_ple), _ = _rowwise("post_mlp", post_norm, s, tr, [(h1, 1024, 0), (f_dn, 1024, 0)], [vec["g_post_mlp"], vec["g_pre_ple"]],
                              [(1024, F32), (1024, BF16)])
    gl = mm(t_ple, wts["w_ple_gate"], mode="nn", m=s, n=1024, k=1024, tn=1024, tk=1024, out_dtype=BF16, name="ple_gate")
    e_ple = mm(pb, wts["w_ple_in"], mode="nt", m=s, n=1024, k=256, tn=1024, tk=256, out_dtype=BF16, name="ple_in")

    def ple_loss(h, glv, e, tg, b, g):
        gate = _sigmoid(glv + b)
        ge = gate * e
        diff = h + _rms(ge) * g - tg
        dy = diff * (1.0 / D_MODEL)
        d_ge, dg = _rms_bwd(ge, g, dy)
        d_gl = d_ge * e * gate * (1.0 - gate)
        loss = jnp.zeros((1, D_MODEL), F32) + 0.5 * jnp.sum(diff * diff) * (1.0 / D_MODEL)
        return [dy, d_gl, d_ge * gate], [_colsum(dg), _colsum(d_gl), loss]

    (dy, d_gl, d_e), (dg_post_ple, db_ple, loss) = _rowwise(
        "ple_loss", ple_loss, s, tr, [(h2, 1024, 0), (gl, 1024, 0), (e_ple, 1024, 0), (tgt, 1024, 0)],
        [vec["b_ple"], vec["g_post_ple"]], [(1024, F32), (1024, BF16), (1024, BF16)], [1024, 1024, 1024])

    ts = min(1024, s)
    wg = functools.partial(_matmul, mode="tn", k=s, tk=ts, out_dtype=BF16)
    grads = {}
    grads["w_ple_in"] = wg(d_e, pb, m=1024, n=256, tm=1024, tn=256, name="g_ple_in")
    grads["w_ple_gate"] = wg(t_ple, d_gl, m=1024, n=1024, tm=1024, tn=1024, name="g_ple_gate")
    d_t = mm(d_gl, wts["w_ple_gate"], mode="nt", m=s, n=1024, k=1024, tn=1024, tk=1024, out_dtype=BF16, name="d_t")

    def bwd_ple_mlp(h, dt, dyv, f, g_pre, g_post):
        dx, dg1 = _rms_bwd(h, g_pre, dt)
        dh = dyv + dx
        df, dg2 = _rms_bwd(f, g_post, dh)
        return [dh, df], [_colsum(dg1), _colsum(dg2)]

    (d_h2, d_f), (dg_pre_ple, dg_post_mlp) = _rowwise(
        "bwd_ple_mlp", bwd_ple_mlp, s, tr, [(h2, 1024, 0), (d_t, 1024, 0), (dy, 1024, 0), (f_dn, 1024, 0)],
        [vec["g_pre_ple"], vec["g_post_mlp"]], [(1024, F32), (1024, BF16)], [1024, 1024])
    d_a = mm(d_f, wts["w_down"], mode="nt", m=s, n=4096, k=1024, tn=1024, tk=1024, out_dtype=BF16, name="d_a",
             epi=(a_up,), epi_fn=lambda acc, av: acc * (2.0 * jnp.maximum(av.astype(F32), 0.0)))
    grads["w_down"] = wg(a_up, d_f, m=4096, n=1024, tm=2048, tn=1024, name="g_down", a_fn=_relu_sq)
    grads["w_up"] = wg(d_a, v2, m=4096, n=1024, tm=2048, tn=1024, name="g_up")
    d_v2 = mm(d_a, wts["w_up"], mode="nn", m=s, n=1024, k=4096, tn=1024, tk=1024, out_dtype=BF16, name="d_v2")

    (d_h1, d_z), (dg_pre_mlp, dg_post_mix) = _rowwise(
        "bwd_mlp_mix", bwd_ple_mlp, s, tr, [(h1, 1024, 0), (d_v2, 1024, 0), (d_h2, 1024, 0), (z, 1024, 0)],
        [vec["g_pre_mlp"], vec["g_post_mix"]], [(1024, F32), (1024, BF16)], [1024, 1024])
    d_mixed = mm(d_z, wts["w_o"], mode="nt", m=s, n=1024, k=1024, tn=1024, tk=1024, out_dtype=BF16, name="d_mixed")
    grads["w_o"] = wg(mixed, d_z, m=1024, n=1024, tm=1024, tn=1024, name="g_o")

    def bwd_gate(dm, a, b, gr, ga, b0, b1):
        sa, sb = _sigmoid(gr.astype(F32) + b0), _sigmoid(ga.astype(F32) + b1)
        dgr = dm * a.astype(F32) * sa * (1.0 - sa)
        dga = dm * b.astype(F32) * sb * (1.0 - sb)
        return [dm * sa, dm * sb, jnp.concatenate([dgr, dga], axis=1)], [_colsum(dgr), _colsum(dga)]

    (d_abr, d_bbr, dproj_gate), (db0, db1) = _rowwise(
        "bwd_gate", bwd_gate, s, tr, [(d_mixed, 1024, 0), (a_br, 1024, 0), (b_br, 1024, 0), (proj_gate, 1024, 0), (proj_gate, 1024, 1)],
        [vec["b0"], vec["b1"]], [(1024, BF16), (1024, BF16), (2048, BF16)], [1024, 1024])
    grads["w_ret_out"] = wg(yr, d_abr, m=2048, n=1024, tm=2048, tn=1024, name="g_ret_out")
    d_yr = mm(d_abr, wts["w_ret_out"], mode="nt", m=s, n=2048, k=1024, tn=1024, tk=1024, out_dtype=BF16, name="d_yr")
    grads["w_dil_out"] = wg(d_bbr, ya, m=1024, n=512, tm=1024, tn=512, name="g_dil_out")
    d_ya = mm(d_bbr, wts["w_dil_out"], mode="nn", m=s, n=512, k=1024, tn=512, tk=1024, out_dtype=F32, name="d_ya")

    slots = {}
    names = list(grads) if on_mesh else []
    shares = _Exchange([blocks(grads[n]) for n in names], [True] * len(names)) if on_mesh else None
    (dproj_ret,), got = _ret_bwd(proj_ret, tabs["cos_r"], tabs["sin_r"], y_ret, d_yr, rstate, s, carry=shares)
    slots.update(zip(names, got))
    upstream = _dil_bwd_prep(d_ya, ya, lse, s)
    dqkv = [_dil_bwd(qkv[g], *upstream[g], *tabs["dil"][g], dil, s, "dil_bwd%d" % g)
            for g, dil in enumerate(DIL_GROUPS)]

    g_ret = wg(dproj_ret, u[0], m=6144, n=1024, tm=2048, tn=1024, name="g_in_ret")
    g_gate = wg(dproj_gate, u[0], m=2048, n=1024, tm=2048, tn=1024, name="g_in_gate")
    g_dil = [wg(dqkv[g], u[g], m=1536, n=1024, tm=1536, tn=1024, name="g_in_dil%d" % g) for g in range(3)]
    grads["w_in"] = _join_w_in(g_ret, g_gate, g_dil)

    du_ret = functools.partial(mm, dproj_ret, w_ret, mode="nn", m=s, n=1024, k=6144, tn=1024, tk=1024, out_dtype=BF16, name="du_ret")
    if on_mesh:
        du_ret, (slots["w_in"],) = du_ret(carry=_Exchange([blocks(grads["w_in"])], [True]))
    else:
        du_ret = du_ret()
    du_gate = mm(dproj_gate, w_gate, mode="nn", m=s, n=1024, k=2048, tn=1024, tk=1024, out_dtype=BF16, name="du_gate")
    du_dil = [mm(dqkv[g], w_dil[g], mode="nn", m=s, n=1024, k=1536, tn=1024, tk=1536, out_dtype=BF16, name="du_dil%d" % g)
              for g in range(3)]

    grad_x, dg_pre_mix = _grad_x(xs, d_h1, (du_ret, du_gate, du_dil[0]), du_dil[1], du_dil[2], vec["g_pre_mix"], s)

    zero = jnp.zeros((1, D_MODEL), F32)
    packet = jnp.concatenate([dg_pre_mix, dg_post_mix, dg_pre_mlp, dg_post_mlp, dg_pre_ple, db_ple, dg_post_ple, loss,
                              db0, db1] + [zero] * 6, axis=0)
    return grad_x, (slots if on_mesh else grads), packet


def _mesh_pos():
    return lax.axis_index("x"), lax.axis_index("y"), lax.axis_index("c")


class _Exchange:
    def __init__(self, arrays, scatter):
        self.arrays, self.scatter, self.n = list(arrays), list(scatter), len(arrays)
        self.out_shape = [jax.ShapeDtypeStruct(a.shape if sc else (N_DEV,) + a.shape, a.dtype)
                          for a, sc in zip(self.arrays, self.scatter)]
        self.scratch = [pltpu.SemaphoreType.DMA((self.n * 7,)), pltpu.SemaphoreType.DMA((self.n * 7,)),
                        pltpu.SemaphoreType.DMA((self.n,))]
        self.specs = [pl.BlockSpec(memory_space=pl.ANY)] * self.n

    def _copies(self, srcs, dsts, sems):
        send_sems, recv_sems, local_sems = sems
        x, y, c = _mesh_pos()
        my = 4 * x + 2 * y + c
        src_of = lambda w, idx: srcs[w].at[idx] if self.scatter[w] else srcs[w]
        local = [pltpu.make_async_copy(src_of(w, my), dsts[w].at[my], local_sems.at[w]) for w in range(self.n)]
        sends, recvs = [], []
        for w in range(self.n):
            for r in range(1, N_DEV):
                px = 1 - x if r & 4 else x
                py = 1 - y if r & 2 else y
                pc = 1 - c if r & 1 else c
                pidx = 4 * px + 2 * py + pc
                kw = dict(send_sem=send_sems.at[w * 7 + r - 1], recv_sem=recv_sems.at[w * 7 + r - 1],
                          device_id=(px, py, pc), device_id_type=MESH)
                sends.append(pltpu.make_async_remote_copy(src_ref=src_of(w, pidx), dst_ref=dsts[w].at[my], **kw))
                recvs.append(pltpu.make_async_remote_copy(src_ref=src_of(w, pidx), dst_ref=dsts[w].at[pidx], **kw))
        return local, sends, recvs

    def start(self, srcs, dsts, sems):
        local, sends, _ = self._copies(srcs, dsts, sems)
        for cp in local + sends:
            cp.start()

    def wait(self, srcs, dsts, sems):
        local, sends, recvs = self._copies(srcs, dsts, sems)
        for cp in recvs:
            cp.wait_recv()
        for cp in sends:
            cp.wait_send()
        for cp in local:
            cp.wait()

    def split(self, refs, n_in, n_out):
        srcs = refs[n_in:n_in + self.n]
        dsts = refs[n_in + self.n + n_out:n_in + 2 * self.n + n_out]
        return srcs, dsts, refs[len(refs) - 3:]


class _TwoLevelGather(_Exchange):
    def __init__(self, arrays):
        super().__init__(arrays, [False] * len(arrays))

    def _plan(self, srcs, dsts, sems):
        send_sems, recv_sems, local_sems = sems
        x, y, c = _mesh_pos()
        me, sibling = (x, y, c), (x, y, 1 - c)
        chips = [(1 - x, y), (x, 1 - y), (1 - x, 1 - y)]
        region = lambda w, dev: dsts[w].at[4 * dev[0] + 2 * dev[1] + dev[2]]

        def copy(w, kk, block, to, src=None):
            return pltpu.make_async_remote_copy(
                src_ref=region(w, block) if src is None else src, dst_ref=region(w, block),
                send_sem=send_sems.at[w * 7 + kk], recv_sem=recv_sems.at[w * 7 + kk], device_id=to, device_id_type=MESH)

        mine = [pltpu.make_async_copy(srcs[w], region(w, me), local_sems.at[w]) for w in range(self.n)]
        first = []
        for w in range(self.n):
            first.append(copy(w, 0, me, sibling, src=srcs[w]))
            first += [copy(w, 1 + j, me, (*chip, c), src=srcs[w]) for j, chip in enumerate(chips)]
        return me, sibling, chips, c, copy, mine, first

    def start(self, srcs, dsts, sems):
        *_, mine, first = self._plan(srcs, dsts, sems)
        for cp in mine + first:
            cp.start()

    def wait(self, srcs, dsts, sems):
        me, sibling, chips, c, copy, mine, first = self._plan(srcs, dsts, sems)
        passed = []
        for j, chip in enumerate(chips):
            for w in range(self.n):
                copy(w, 1 + j, (*chip, c), me).wait_recv()
                cp = copy(w, 4 + j, (*chip, c), sibling)
                cp.start()
                passed.append(cp)
        for w in range(self.n):
            copy(w, 0, sibling, me).wait_recv()
            for j, chip in enumerate(chips):
                copy(w, 4 + j, (*chip, 1 - c), me).wait_recv()
        for cp in first + passed:
            cp.wait_send()
        for cp in mine:
            cp.wait()


def _run_exchange(ex, name):
    def body(*refs):
        parts = ex.split(refs, 0, 0)
        ex.start(*parts)
        ex.wait(*parts)

    return pl.pallas_call(body, name=name, in_specs=ex.specs, out_specs=ex.specs, out_shape=ex.out_shape,
                          scratch_shapes=ex.scratch)(*ex.arrays)


def _pick_rows(r, c, target_bytes):
    t = r
    while (t // 2) % 16 == 0 and t // 2 >= 16 and t * c * 4 > target_bytes:
        t //= 2
    return t


def _sum_slots(slots, name):
    ns, r, c = slots.shape
    tr = _pick_rows(r, c, 256 * 1024)

    def body(s_ref, o_ref):
        acc = s_ref[0].astype(F32)
        for kk in range(1, ns):
            acc = acc + s_ref[kk].astype(F32)
        o_ref[...] = acc

    return pl.pallas_call(
        body, name=name, grid=(r // tr,),
        in_specs=[pl.BlockSpec((ns, tr, c), lambda i: (0, i, 0))], out_specs=pl.BlockSpec((tr, c), lambda i: (i, 0)),
        out_shape=jax.ShapeDtypeStruct((r, c), F32), compiler_params=_cparams(("parallel",)),
    )(slots)


def _adamw(slots, w, m, v, name):
    ns, r, c = slots.shape
    tr = _pick_rows(r, c, 256 * 1024)

    def body(s_ref, w_ref, m_ref, v_ref, g_out, d_out, m_out, v_out):
        g = s_ref[0].astype(F32)
        for kk in range(1, ns):
            g = g + s_ref[kk].astype(F32)
        mn = ADAM_B1 * m_ref[...] + (1.0 - ADAM_B1) * g
        vn = ADAM_B2 * v_ref[...] + (1.0 - ADAM_B2) * (g * g)
        m_hat = mn / (1.0 - ADAM_B1 ** ADAM_STEP)
        v_hat = vn / (1.0 - ADAM_B2 ** ADAM_STEP)
        g_out[...] = g
        d_out[...] = -ADAM_LR * (m_hat / (jnp.sqrt(v_hat) + ADAM_EPS) + ADAM_WD * w_ref[...])
        m_out[...] = mn
        v_out[...] = vn

    blk = pl.BlockSpec((tr, c), lambda i: (i, 0))
    return pl.pallas_call(
        body, name=name, grid=(r // tr,),
        in_specs=[pl.BlockSpec((ns, tr, c), lambda i: (0, i, 0)), blk, blk, blk], out_specs=[blk] * 4,
        out_shape=[jax.ShapeDtypeStruct((r, c), F32)] * 4, compiler_params=_cparams(("parallel",)),
    )(slots, w, m, v)


def _rotary_tables(pos, s):
    posf = pos.astype(F32)
    inv_freq = 1.0 / (10000.0 ** jnp.linspace(0.0, 1.0, RET_QK // 2, dtype=F32))
    ang = posf[:, None] * inv_freq
    tabs = {"cos_r": jnp.cos(ang), "sin_r": jnp.sin(ang), "dil": []}
    freqs = 500000.0 ** (-jnp.arange(0, 16, 2, dtype=F32) / 16)
    spread = np.zeros((16, 384), np.float32)
    bias = np.zeros((1, 384), np.float32)
    for head in range(2):
        for i in range(8):
            spread[i, 64 * head + i] = spread[i, 64 * head + 8 + i] = 1.0
            spread[8 + i, 128 + 64 * head + i] = -1.0
            spread[8 + i, 256 + 64 * head + 8 + i] = 1.0
        bias[0, 64 * head + 16:64 * head + 64] = 1.0

    def expand(t, e, b):
        hi = t.astype(BF16)
        lo = (t - hi.astype(F32)).astype(BF16)
        out = _dot(hi, e, NN) + _dot(lo, e, NN) + b
        return [out[:, 0:128], out[:, 128:256], out[:, 256:384]], []

    for g, dil in enumerate(DIL_GROUPS):
        ang = posf.reshape(s // dil, dil).T.reshape(s, 1) * freqs
        cs = jnp.concatenate([jnp.cos(ang), jnp.sin(ang)], axis=1)
        t3, _ = _rowwise("rot_tables%d" % g, expand, s, min(1024, s), [(cs, 16, 0)],
                         [jnp.asarray(spread, BF16), jnp.asarray(bias)], [(128, F32)] * 3)
        tabs["dil"].append(tuple(t3))
    return tabs


_TRANSPOSED = ("w_in", "w_dil_out", "w_up", "w_ple_in")
_MATS = ("w_in", "w_ret_out", "w_dil_out", "w_o", "w_up", "w_down", "w_ple_gate", "w_ple_in")
_VECS = ("g_pre_mix", "g_post_mix", "g_pre_mlp", "g_post_mlp", "g_pre_ple", "b_ple_gate", "g_post_ple")
_ORDER = ("w_in", "b_gate", "w_ret_out", "w_dil_out", "w_o", "g_pre_mix", "g_post_mix", "g_pre_mlp", "g_post_mlp", "w_up",
          "w_down", "g_pre_ple", "w_ple_gate", "b_ple_gate", "w_ple_in", "g_post_ple")


def kernel(x, p, positions, w_in, b_gate, w_ret_out, w_dil_out, w_o, g_pre_mix, g_post_mix, g_pre_mlp, g_post_mlp, w_up, w_down, g_pre_ple, w_ple_gate, b_ple_gate, w_ple_in, g_post_ple, loss_target, m_w_in, m_b_gate, m_w_ret_out, m_w_dil_out, m_w_o, m_g_pre_mix, m_g_post_mix, m_g_pre_mlp, m_g_post_mlp, m_w_up, m_w_down, m_g_pre_ple, m_w_ple_gate, m_b_ple_gate, m_w_ple_in, m_g_post_ple, v_w_in, v_b_gate, v_w_ret_out, v_w_dil_out, v_w_o, v_g_pre_mix, v_g_post_mix, v_g_pre_mlp, v_g_post_mlp, v_w_up, v_w_down, v_g_pre_ple, v_w_ple_gate, v_b_ple_gate, v_w_ple_in, v_g_post_ple):
    s = x.shape[1]
    wd = dict(w_in=w_in, b_gate=b_gate, w_ret_out=w_ret_out, w_dil_out=w_dil_out, w_o=w_o, g_pre_mix=g_pre_mix,
              g_post_mix=g_post_mix, g_pre_mlp=g_pre_mlp, g_post_mlp=g_post_mlp, w_up=w_up, w_down=w_down,
              g_pre_ple=g_pre_ple, w_ple_gate=w_ple_gate, b_ple_gate=b_ple_gate, w_ple_in=w_ple_in, g_post_ple=g_post_ple)
    md = dict(w_in=m_w_in, b_gate=m_b_gate, w_ret_out=m_w_ret_out, w_dil_out=m_w_dil_out, w_o=m_w_o, g_pre_mix=m_g_pre_mix,
              g_post_mix=m_g_post_mix, g_pre_mlp=m_g_pre_mlp, g_post_mlp=m_g_post_mlp, w_up=m_w_up, w_down=m_w_down,
              g_pre_ple=m_g_pre_ple, w_ple_gate=m_w_ple_gate, b_ple_gate=m_b_ple_gate, w_ple_in=m_w_ple_in, g_post_ple=m_g_post_ple)
    vd = dict(w_in=v_w_in, b_gate=v_b_gate, w_ret_out=v_w_ret_out, w_dil_out=v_w_dil_out, w_o=v_w_o, g_pre_mix=v_g_pre_mix,
              g_post_mix=v_g_post_mix, g_pre_mlp=v_g_pre_mlp, g_post_mlp=v_g_post_mlp, w_up=v_w_up, w_down=v_w_down,
              g_pre_ple=v_g_pre_ple, w_ple_gate=v_w_ple_gate, b_ple_gate=v_b_ple_gate, w_ple_in=v_w_ple_in, g_post_ple=v_g_post_ple)

    shards = {n: (wd[n][0].T if n in _TRANSPOSED else wd[n][0]).astype(BF16) for n in _MATS}
    shards["b_gate"] = b_gate[0]
    vec = {n: wd[n] for n in _VECS}
    vec["b_ple"] = b_ple_gate

    tabs = _rotary_tables(positions[0], s)
    grad_x, slots, packet = _local_step(x[0], p[0, 0].astype(BF16), loss_target[0], tabs, {}, vec, s, shards=shards)

    (packets,) = _run_exchange(_Exchange([packet], [False]), "exchange_vectors")
    out = {}
    for n in _MATS:
        sl = slots[n]
        if n in _TRANSPOSED:
            sl = _sum_slots(sl, "sum_" + n).T[None]
        out[n] = _adamw(sl, wd[n][0], md[n][0], vd[n][0], "adamw_" + n)
    zero_rows = jnp.zeros((16 - len(_VECS), D_MODEL), F32)
    pack = lambda d: jnp.concatenate([d[n] for n in _VECS] + [zero_rows], axis=0)
    small = _adamw(packets, pack(wd), pack(md), pack(vd), "adamw_vectors")
    for i, n in enumerate(_VECS):
        out[n] = tuple(t[i:i + 1] for t in small)
    my = 4 * lax.axis_index("x") + 2 * lax.axis_index("y") + lax.axis_index("c")
    g_bias = lax.dynamic_slice(small[0], (8, my * 128), (2, 128))
    out["b_gate"] = _adamw(g_bias[None], b_gate[0], m_b_gate[0], v_b_gate[0], "adamw_b_gate")
    loss = small[0][7, 0]

    res = [loss, grad_x[None]]
    for kk in range(4):
        res += [out[n][kk][None] if out[n][kk].ndim == 2 and wd[n].ndim == 3 else out[n][kk] for n in _ORDER]
    return tuple(res)
```

```python
import functools
import math

import numpy as np
import jax
import jax.numpy as jnp
from jax import lax
from jax.experimental import pallas as pl
from jax.experimental.pallas import tpu as pltpu

F32, BF16 = jnp.float32, jnp.bfloat16
D_MODEL = 1024
EPS = 1e-6
N_DEV = 8
RET_HEADS, RET_QK, RET_V, RET_CHUNK = 4, 256, 512, 128
DIL_GROUPS = (1, 4, 16)
DIL_W = 512
QB = 128
NEG = -1e30
ADAM_LR, ADAM_B1, ADAM_B2, ADAM_EPS, ADAM_WD, ADAM_STEP = 0.001, 0.9, 0.999, 1e-08, 0.01, 10
VMEM_LIMIT_BYTES = 56 * 1024 * 1024
MESH = pl.DeviceIdType.MESH

NN = ((1,), (0,))
NT = ((1,), (1,))
TN = ((0,), (0,))


def _dot(a, b, dn):
    return lax.dot_general(a, b, (dn, ((), ())), preferred_element_type=F32)


def _cparams(sem):
    return pltpu.CompilerParams(dimension_semantics=sem, vmem_limit_bytes=VMEM_LIMIT_BYTES)


def _rms(x):
    return x * lax.rsqrt(jnp.mean(x * x, axis=-1, keepdims=True) + EPS)


def _rms_bwd(x, g, dy):
    r = lax.rsqrt(jnp.mean(x * x, axis=-1, keepdims=True) + EPS)
    xh = x * r
    t = dy * g
    dx = r * (t - xh * jnp.mean(t * xh, axis=-1, keepdims=True))
    return dx, dy * xh


def _colsum(v):
    return jnp.sum(v, axis=0, keepdims=True)


def _sigmoid(v):
    return 1.0 / (1.0 + jnp.exp(-v))


def _pallas(compute, *, name, grid, in_specs, out_specs, out_shape, scratch, semantics, args, carry=None):
    n_in, n_out = len(in_specs), len(out_specs)
    if carry is None:
        res = pl.pallas_call(compute, name=name, grid=grid, in_specs=in_specs, out_specs=out_specs, out_shape=out_shape,
                             scratch_shapes=scratch, compiler_params=_cparams(semantics))(*args)
        return res, []
    n_steps = math.prod(grid)

    def body(*refs):
        step = 0
        for axis, size in enumerate(grid):
            step = step * size + pl.program_id(axis)
        parts = carry.split(refs, n_in, n_out)
        pl.when(step == 0)(lambda: carry.start(*parts))
        compute(*refs[:n_in], *refs[n_in + carry.n:n_in + carry.n + n_out], *refs[n_in + 2 * carry.n + n_out:len(refs) - 3])
        pl.when(step == n_steps - 1)(lambda: carry.wait(*parts))

    res = pl.pallas_call(
        body, name=name, grid=grid, in_specs=list(in_specs) + carry.specs, out_specs=list(out_specs) + carry.specs,
        out_shape=list(out_shape) + carry.out_shape, scratch_shapes=list(scratch) + carry.scratch,
        compiler_params=_cparams(("arbitrary",) * len(grid)))(*args, *carry.arrays)
    return res[:n_out], res[n_out:]


def _matmul(a, b, *, mode, m, n, k, tm, tn, tk, out_dtype, name, a_fn=None, epi=(), epi_width=None, epi_fn=None, carry=None):
    nk = k // tk
    grid = (m // tm, n // tn, nk)
    if mode == "nn":
        a_blk, a_im, b_blk, b_im, dn = (tm, tk), (lambda i, j, kk: (i, kk)), (tk, tn), (lambda i, j, kk: (kk, j)), NN
    elif mode == "nt":
        a_blk, a_im, b_blk, b_im, dn = (tm, tk), (lambda i, j, kk: (i, kk)), (tn, tk), (lambda i, j, kk: (j, kk)), NT
    else:
        a_blk, a_im, b_blk, b_im, dn = (tk, tm), (lambda i, j, kk: (kk, i)), (tk, tn), (lambda i, j, kk: (kk, j)), TN
    o_im = lambda i, j, kk: (i, j)
    n_in = 2 + len(epi)

    def body(*refs):
        a_ref, b_ref = refs[0], refs[1]
        o_ref = refs[n_in]
        acc_ref = refs[n_in + 1] if nk > 1 else None

        def finish(acc):
            if epi:
                acc = epi_fn(acc, *[r[...] for r in refs[2:n_in]])
            o_ref[...] = acc.astype(o_ref.dtype)

        av = a_ref[...]
        if a_fn is not None:
            av = a_fn(av)
        part = _dot(av, b_ref[...], dn)
        if nk == 1:
            finish(part)
        else:
            kk = pl.program_id(2)

            @pl.when(kk == 0)
            def _():
                acc_ref[...] = part

            @pl.when(kk > 0)
            def _():
                acc_ref[...] += part

            @pl.when(kk == nk - 1)
            def _():
                finish(acc_ref[...])

    epi_spec = pl.BlockSpec((tm, tn), o_im) if epi_width is None else pl.BlockSpec((tm, epi_width), lambda i, j, kk: (i, 0))
    in_specs = [pl.BlockSpec(a_blk, a_im), pl.BlockSpec(b_blk, b_im)] + [epi_spec] * len(epi)
    args = [a, b, *epi]
    (out,), got = _pallas(
        body, name=name, grid=grid, in_specs=in_specs, out_specs=[pl.BlockSpec((tm, tn), o_im)],
        out_shape=[jax.ShapeDtypeStruct((m, n), out_dtype)], scratch=[pltpu.VMEM((tm, tn), F32)] if nk > 1 else [],
        semantics=("parallel", "parallel", "arbitrary"), args=args, carry=carry)
    return out if carry is None else (out, got)


def _relu_sq(v):
    r = jnp.maximum(v.astype(F32), 0.0)
    return (r * r).astype(BF16)


def _rowwise(name, fn, s, tr, rows, vecs, outs, accs=()):
    n_r, n_v, n_o, n_a = len(rows), len(vecs), len(outs), len(accs)

    def body(*refs):
        vals = [refs[i][...].astype(F32) for i in range(n_r)] + [refs[n_r + i][...] for i in range(n_v)]
        o_refs = refs[n_r + n_v:n_r + n_v + n_o]
        a_refs = refs[n_r + n_v + n_o:]
        o_vals, a_vals = fn(*vals)
        for ref, val in zip(o_refs, o_vals):
            ref[...] = val.astype(ref.dtype)
        if n_a:
            @pl.when(pl.program_id(0) == 0)
            def _():
                for ref in a_refs:
                    ref[...] = jnp.zeros_like(ref)

            for ref, val in zip(a_refs, a_vals):
                ref[...] += val

    in_specs = [pl.BlockSpec((tr, w), functools.partial(lambda i, cb: (i, cb), cb=cb)) for _, w, cb in rows]
    in_specs += [pl.BlockSpec(v.shape, lambda i: (0, 0)) for v in vecs]
    out_specs = [pl.BlockSpec((tr, w), lambda i: (i, 0)) for w, _ in outs]
    out_specs += [pl.BlockSpec((1, w), lambda i: (0, 0)) for w in accs]
    out_shape = [jax.ShapeDtypeStruct((s, w), dt) for w, dt in outs]
    out_shape += [jax.ShapeDtypeStruct((1, w), F32) for w in accs]
    res = pl.pallas_call(
        body, name=name, grid=(s // tr,), in_specs=in_specs, out_specs=out_specs, out_shape=out_shape,
        compiler_params=_cparams(("arbitrary",)),
    )(*[r[0] for r in rows], *vecs)
    return res[:n_o], res[n_o:]


_ROW_TILE = 256
_STREAM_SPECS = [pl.BlockSpec((dil, _ROW_TILE // dil, D_MODEL), lambda i: (0, i, 0)) for dil in DIL_GROUPS[1:]]
_NAT_SPEC = pl.BlockSpec((_ROW_TILE, D_MODEL), lambda i: (i, 0))
_VEC_SPEC = pl.BlockSpec((1, D_MODEL), lambda i: (0, 0))
_COL_BLOCKS = pltpu.VMEM((D_MODEL // 128, _ROW_TILE, 128), F32)


def _prenorm(xs, g, s, carry=None):
    tr = _ROW_TILE

    def body(x_ref, g_ref, u_ref, u4_ref, u16_ref, buf):
        xn = _rms(x_ref[...]) * g_ref[...]
        u_ref[...] = xn.astype(BF16)
        for cb in range(8):
            buf[cb] = xn[:, cb * 128:(cb + 1) * 128]
        for dil, out in ((4, u4_ref), (16, u16_ref)):
            for c in range(dil):
                rows = pl.ds(c, tr // dil, stride=dil)
                out[c] = jnp.concatenate([buf.at[cb][rows, :] for cb in range(8)], axis=1).astype(BF16)

    res, got = _pallas(
        body, name="prenorm", grid=(s // tr,), in_specs=[_NAT_SPEC, _VEC_SPEC], out_specs=[_NAT_SPEC] + _STREAM_SPECS,
        out_shape=[jax.ShapeDtypeStruct((s, D_MODEL), BF16)]
        + [jax.ShapeDtypeStruct((dil, s // dil, D_MODEL), BF16) for dil in DIL_GROUPS[1:]],
        scratch=[_COL_BLOCKS], semantics=("parallel",), args=(xs, g), carry=carry)
    return [r.reshape(s, D_MODEL) for r in res], got


def _grad_x(xs, d_h1, du_nat, du4, du16, g, s):
    tr = _ROW_TILE

    def body(x_ref, dh_ref, a_ref, b_ref, c_ref, u4_ref, u16_ref, g_ref, dx_ref, dg_ref, buf):
        du = a_ref[...].astype(F32) + b_ref[...].astype(F32) + c_ref[...].astype(F32)
        for dil, src in ((4, u4_ref), (16, u16_ref)):
            for c in range(dil):
                part = src[c].astype(F32)
                for cb in range(8):
                    buf.at[cb][pl.ds(c, tr // dil, stride=dil), :] = part[:, cb * 128:(cb + 1) * 128]
            du = du + jnp.concatenate([buf[cb] for cb in range(8)], axis=1)
        dx, dgr = _rms_bwd(x_ref[...], g_ref[...], du)
        dx_ref[...] = dh_ref[...] + dx

        @pl.when(pl.program_id(0) == 0)
        def _():
            dg_ref[...] = jnp.zeros_like(dg_ref)

        dg_ref[...] += _colsum(dgr)

    return pl.pallas_call(
        body, name="grad_x", grid=(s // tr,), in_specs=[_NAT_SPEC] * 5 + _STREAM_SPECS + [_VEC_SPEC],
        out_specs=[_NAT_SPEC, _VEC_SPEC],
        out_shape=[jax.ShapeDtypeStruct((s, D_MODEL), F32), jax.ShapeDtypeStruct((1, D_MODEL), F32)],
        scratch_shapes=[_COL_BLOCKS], compiler_params=_cparams(("arbitrary",)),
    )(xs, d_h1, *du_nat, du4.reshape(4, s // 4, D_MODEL), du16.reshape(16, s // 16, D_MODEL), g)


def _ret_tables():
    h = np.arange(RET_HEADS, dtype=np.float32)
    lg = np.log1p(-(np.float32(2.0) ** (-5.0 - h))).astype(np.float32)
    idx = np.arange(RET_CHUNK, dtype=np.float32)
    diff = idx[:, None] - idx[None, :]
    dm = np.where(diff[None] >= 0, np.exp(np.maximum(diff, 0.0)[None] * lg[:, None, None]), 0.0)
    qd = np.exp((idx + 1.0)[None, :, None] * lg[:, None, None])
    kd = np.exp((RET_CHUNK - 1.0 - idx)[None, :, None] * lg[:, None, None])
    cd = np.exp(RET_CHUNK * lg)[:, None, None]
    return [jnp.asarray(t, F32) for t in (dm, qd, kd, cd)]


def _rope_half(v, cos, sin):
    v1, v2 = v[:, :128], v[:, 128:]
    return jnp.concatenate([v1 * cos - v2 * sin, v2 * cos + v1 * sin], axis=1)


def _unrope_half(d, cos, sin):
    d1, d2 = d[:, :128], d[:, 128:]
    return jnp.concatenate([d1 * cos + d2 * sin, d2 * cos - d1 * sin], axis=1)


_RET_HEADS_FWD, _RET_HEADS_BWD = 1, 2


def _ret_specs(rb, rev_n, hp):
    def rowmap(w_blk):
        return lambda h, n: (rev_n(n), w_blk(h))
    tab = [pl.BlockSpec((hp, RET_CHUNK, RET_CHUNK), lambda h, n: (h, 0, 0)),
           pl.BlockSpec((hp, RET_CHUNK, 1), lambda h, n: (h, 0, 0)),
           pl.BlockSpec((hp, RET_CHUNK, 1), lambda h, n: (h, 0, 0)),
           pl.BlockSpec((hp, 1, 1), lambda h, n: (h, 0, 0))]
    proj = pl.BlockSpec((rb, hp * 1536), rowmap(lambda h: h))
    cs = pl.BlockSpec((rb, 128), rowmap(lambda h: 0))
    hv = pl.BlockSpec((rb, hp * RET_V), rowmap(lambda h: h))
    return proj, cs, hv, tab


def _ret_fwd(proj_ret, cos, sin, s, carry=None):
    rb = min(512, s)
    ch = rb // RET_CHUNK
    nb = s // rb
    hp = _RET_HEADS_FWD
    proj_spec, cs_spec, hv_spec, tab_specs = _ret_specs(rb, lambda n: n, hp)

    def body(p_ref, cos_ref, sin_ref, dm_ref, qd_ref, kd_ref, cd_ref, yr_ref, y_ref, rs_ref, r_acc):
        @pl.when(pl.program_id(1) == 0)
        def _():
            r_acc[...] = jnp.zeros_like(r_acc)

        for c, hh in [(c, hh) for c in range(ch) for hh in range(hp)]:
            rows = slice(c * RET_CHUNK, (c + 1) * RET_CHUNK)
            pc, hc = hh * 1536, hh * RET_V
            dm, qd, kd, cd = dm_ref[hh], qd_ref[hh], kd_ref[hh], cd_ref[hh]
            cosv, sinv = cos_ref[rows, :], sin_ref[rows, :]
            q = _rope_half(p_ref[rows, pc:pc + 256].astype(F32), cosv, sinv)
            kk = _rope_half(p_ref[rows, pc + 256:pc + 512].astype(F32), cosv, sinv) * (RET_QK ** -0.5)
            v = p_ref[rows, pc + 512:pc + 1024]
            g = p_ref[rows, pc + 1024:pc + 1536].astype(F32)
            rb16 = r_acc[hh].astype(BF16)
            rs_ref[hh, c] = rb16
            sc = _dot(q.astype(BF16), kk.astype(BF16), NT) * dm
            y = _dot(sc.astype(BF16), v, NN) + _dot((q * qd).astype(BF16), rb16, NN)
            r_acc[hh] = r_acc[hh] * cd + _dot((kk * kd).astype(BF16), v, TN)
            y_ref[rows, hc:hc + RET_V] = y.astype(BF16)
            yr_ref[rows, hc:hc + RET_V] = (_rms(y) * (g * _sigmoid(g))).astype(BF16)

    return _pallas(
        body, name="ret_fwd", grid=(RET_HEADS // hp, nb),
        in_specs=[proj_spec, cs_spec, cs_spec] + tab_specs,
        out_specs=[hv_spec, hv_spec, pl.BlockSpec((hp, ch, RET_QK, RET_V), lambda h, n: (h, n, 0, 0))],
        out_shape=[jax.ShapeDtypeStruct((s, RET_HEADS * RET_V), BF16), jax.ShapeDtypeStruct((s, RET_HEADS * RET_V), BF16),
                   jax.ShapeDtypeStruct((RET_HEADS, s // RET_CHUNK, RET_QK, RET_V), BF16)],
        scratch=[pltpu.VMEM((hp, RET_QK, RET_V), F32)], semantics=("parallel", "arbitrary"),
        args=(proj_ret, cos, sin, *_ret_tables()), carry=carry)


def _ret_bwd(proj_ret, cos, sin, y, d_yr, rs, s, carry=None):
    rb = min(512, s)
    ch = rb // RET_CHUNK
    nb = s // rb
    hp = _RET_HEADS_BWD
    proj_spec, cs_spec, hv_spec, tab_specs = _ret_specs(rb, lambda n: nb - 1 - n, hp)

    def body(p_ref, cos_ref, sin_ref, y_ref, dyr_ref, rs_ref, dm_ref, qd_ref, kd_ref, cd_ref, o_ref, dr_acc):
        @pl.when(pl.program_id(1) == 0)
        def _():
            dr_acc[...] = jnp.zeros_like(dr_acc)

        for c, hh in [(c, hh) for c in reversed(range(ch)) for hh in range(hp)]:
            rows = slice(c * RET_CHUNK, (c + 1) * RET_CHUNK)
            pc, hc = hh * 1536, hh * RET_V
            dm, qd, kd, cd = dm_ref[hh], qd_ref[hh], kd_ref[hh], cd_ref[hh]
            cosv, sinv = cos_ref[rows, :], sin_ref[rows, :]
            q = _rope_half(p_ref[rows, pc:pc + 256].astype(F32), cosv, sinv)
            kk = _rope_half(p_ref[rows, pc + 256:pc + 512].astype(F32), cosv, sinv) * (RET_QK ** -0.5)
            v = p_ref[rows, pc + 512:pc + 1024]
            g = p_ref[rows, pc + 1024:pc + 1536].astype(F32)
            yv = y_ref[rows, hc:hc + RET_V].astype(F32)
            dyr = dyr_ref[rows, hc:hc + RET_V].astype(F32)
            sg = _sigmoid(g)
            r = lax.rsqrt(jnp.mean(yv * yv, axis=-1, keepdims=True) + EPS)
            yn = yv * r
            dg = dyr * yn * (sg * (1.0 + g * (1.0 - sg)))
            dyn = dyr * (g * sg)
            dy = (r * (dyn - yn * jnp.mean(dyn * yn, axis=-1, keepdims=True))).astype(BF16)
            qb, kb = q.astype(BF16), kk.astype(BF16)
            rb16 = rs_ref[hh, c]
            drb = dr_acc[hh].astype(BF16)
            sd = _dot(qb, kb, NT) * dm
            ds = (_dot(dy, v, NT) * dm).astype(BF16)
            dq = _dot(ds, kb, NN) + qd * _dot(dy, rb16, NT)
            dk = _dot(ds, qb, TN) + kd * _dot(v, drb, NT)
            dv = _dot(sd.astype(BF16), dy, TN) + _dot((kk * kd).astype(BF16), drb, NN)
            dr_acc[hh] = dr_acc[hh] * cd + _dot((q * qd).astype(BF16), dy, TN)
            o_ref[rows, pc:pc + 256] = _unrope_half(dq, cosv, sinv).astype(BF16)
            o_ref[rows, pc + 256:pc + 512] = (_unrope_half(dk, cosv, sinv) * (RET_QK ** -0.5)).astype(BF16)
            o_ref[rows, pc + 512:pc + 1024] = dv.astype(BF16)
            o_ref[rows, pc + 1024:pc + 1536] = dg.astype(BF16)

    in_specs = [proj_spec, cs_spec, cs_spec, hv_spec, hv_spec,
                pl.BlockSpec((hp, ch, RET_QK, RET_V), lambda h, n: (h, nb - 1 - n, 0, 0))] + tab_specs
    return _pallas(
        body, name="ret_bwd", grid=(RET_HEADS // hp, nb), in_specs=in_specs, out_specs=[proj_spec],
        out_shape=[jax.ShapeDtypeStruct((s, RET_HEADS * 1536), BF16)], scratch=[pltpu.VMEM((hp, RET_QK, RET_V), F32)],
        semantics=("parallel", "arbitrary"), args=(proj_ret, cos, sin, y, d_yr, rs, *_ret_tables()), carry=carry)


def _rope_qk(acc, c, s1, s2):
    outs = []
    for cc in range(8):
        vv = acc[:, cc * 128:(cc + 1) * 128]
        outs.append(vv * c + pltpu.roll(vv, 120, 1) * s1 + pltpu.roll(vv, 8, 1) * s2)
    return jnp.concatenate(outs + [acc[:, 2 * DIL_W:]], axis=1)


def _pair_masks(keys_on_rows=False):
    ri = lax.broadcasted_iota(jnp.int32, (2 * QB, 2 * QB), 1 if keys_on_rows else 0)
    ci = lax.broadcasted_iota(jnp.int32, (2 * QB, 2 * QB), 0 if keys_on_rows else 1)
    e = ci - (ri & (QB - 1))
    lane_lo = lax.broadcasted_iota(jnp.int32, (2 * QB, 128), 1) < 64
    return ci, jnp.logical_and(e >= 0, e <= QB), lane_lo


def _stack_heads(v, lane_lo):
    z = jnp.zeros_like(v)
    return jnp.concatenate([jnp.where(lane_lo, v, z), jnp.where(lane_lo, z, v)], axis=0)


def _dil_fwd(qkv, dil, s, name):
    length = s // dil
    rb = min(512, length)
    nsub = rb // QB
    nbs = length // rb
    sub_per = rb // QB

    def body(q_ref, k_ref, v_ref, kp_ref, vp_ref, o_ref, l_ref):
        first = (pl.program_id(0) % nbs) == 0
        ci, band, lane_lo = _pair_masks()
        lo1 = lane_lo[0:QB]

        for i in range(nsub):
            rows = slice(i * QB, (i + 1) * QB)
            mask = jnp.logical_and(band, ci >= jnp.where(first, QB, 0)) if i == 0 else band
            for j in range(4):
                lanes = slice(j * 128, (j + 1) * 128)
                q2 = _stack_heads(q_ref[rows, lanes], lo1)
                if i == 0:
                    k2 = jnp.concatenate([kp_ref[:, lanes], k_ref[rows, lanes]], axis=0)
                    v2 = jnp.concatenate([vp_ref[:, lanes], v_ref[rows, lanes]], axis=0)
                else:
                    k2, v2 = k_ref[(i - 1) * QB:(i + 1) * QB, lanes], v_ref[(i - 1) * QB:(i + 1) * QB, lanes]
                v2 = _stack_heads(v2, lane_lo)
                sc = jnp.where(mask, _dot(q2, k2, NT) * 0.125, NEG)
                m = jnp.max(sc, axis=1, keepdims=True)
                p = jnp.exp(sc - m)
                den = jnp.sum(p, axis=1, keepdims=True)
                pb = p.astype(BF16)
                o = _dot(jnp.concatenate([pb[0:QB], pb[QB:]], axis=1), v2, NN)
                inv = 1.0 / den
                lse = m + jnp.log(den)
                o_ref[rows, lanes] = o * jnp.where(lo1, inv[0:QB], inv[QB:])
                l_ref[rows, lanes] = jnp.where(lo1, lse[0:QB], lse[QB:])

    prev = lambda n: jnp.maximum(n * sub_per - 1, 0)
    cur = lambda cb: (lambda n: (n, cb))
    return pl.pallas_call(
        body, name=name, grid=(s // rb,),
        in_specs=[pl.BlockSpec((rb, DIL_W), cur(0)), pl.BlockSpec((rb, DIL_W), cur(1)), pl.BlockSpec((rb, DIL_W), cur(2)),
                  pl.BlockSpec((QB, DIL_W), lambda n: (prev(n), 1)), pl.BlockSpec((QB, DIL_W), lambda n: (prev(n), 2))],
        out_specs=[pl.BlockSpec((rb, DIL_W), cur(0)), pl.BlockSpec((rb, DIL_W), cur(0))],
        out_shape=[jax.ShapeDtypeStruct((s, DIL_W), F32), jax.ShapeDtypeStruct((s, DIL_W), F32)],
        compiler_params=_cparams(("parallel",)),
    )(qkv, qkv, qkv, qkv, qkv)


def _dil_bwd(qkv, dya, lse, dlt, tc, ts1, ts2, dil, s, name):
    length = s // dil
    rb = min(512, length)
    nsub = rb // QB
    nbs = length // rb
    last_blk = s // QB - 1

    def body(q_ref, k_ref, v_ref, kp_ref, vp_ref, qn_ref, dy_ref, dyn_ref, l_ref, ln_ref, d_ref, dn_ref,
             c_ref, s1_ref, s2_ref, o_ref, dka, dva):
        nl = pl.program_id(0) % nbs
        first, last = nl == 0, nl == nbs - 1
        ci, band, lane_lo = _pair_masks(keys_on_rows=True)
        lo1 = lane_lo[0:QB]

        def unrope(d, rows):
            return d * c_ref[rows, :] + pltpu.roll(d * s1_ref[rows, :], 8, 1) + pltpu.roll(d * s2_ref[rows, :], 120, 1)

        for qi in range(nsub + 1):
            nxt = qi == nsub
            rows = slice((nsub - 1) * QB, nsub * QB) if nxt else slice(qi * QB, (qi + 1) * QB)
            prev_rows = slice((qi - 1) * QB, qi * QB)
            if qi == 0:
                mask = jnp.logical_and(band, ci >= jnp.where(first, QB, 0))
            elif nxt:
                mask = jnp.logical_and(band, ci <= jnp.where(last, -1, QB - 1))[0:QB, :]
            else:
                mask = band
            for j in range(4):
                lanes = slice(j * 128, (j + 1) * 128)
                if nxt:
                    q, do, lv, dl = qn_ref[:, lanes], dyn_ref[:, lanes], ln_ref[:, lanes], dn_ref[:, lanes]
                    k2, v2 = k_ref[prev_rows, lanes], v_ref[prev_rows, lanes]
                else:
                    q, do, lv, dl = q_ref[rows, lanes], dy_ref[rows, lanes], l_ref[rows, lanes], d_ref[rows, lanes]
                    if qi == 0:
                        k2 = jnp.concatenate([kp_ref[:, lanes], k_ref[rows, lanes]], axis=0)
                        v2 = jnp.concatenate([vp_ref[:, lanes], v_ref[rows, lanes]], axis=0)
                    else:
                        k2, v2 = k_ref[(qi - 1) * QB:(qi + 1) * QB, lanes], v_ref[(qi - 1) * QB:(qi + 1) * QB, lanes]
                q2, do2 = _stack_heads(q, lo1), _stack_heads(do, lo1)
                lt, dt = lv.T, dl.T
                lse2 = jnp.concatenate([lt[0:1], lt[64:65]], axis=1)
                dl2 = jnp.concatenate([dt[0:1], dt[64:65]], axis=1)
                sc = _dot(k2, q2, NT) * 0.125
                p = jnp.where(mask, jnp.exp(jnp.minimum(sc - lse2, 0.0)), 0.0)
                ds = (p * (_dot(v2, do2, NT) - dl2) * 0.125).astype(BF16)
                dk2 = _dot(ds, q2, NN)
                dv2 = _dot(p.astype(BF16), do2, NN)
                if qi >= 1:
                    dka[prev_rows, lanes] += dk2[0:QB]
                    dva[prev_rows, lanes] += dv2[0:QB]
                if not nxt:
                    dka[rows, lanes] = dk2[QB:]
                    dva[rows, lanes] = dv2[QB:]
                    dq = _dot(jnp.concatenate([ds[:, 0:QB], ds[:, QB:]], axis=0), _stack_heads(k2, lane_lo), TN)
                    o_ref[rows, lanes] = unrope(dq, rows).astype(BF16)

        for cc in range(4):
            lanes = slice(cc * 128, (cc + 1) * 128)
            o_ref[:, 512 + cc * 128:512 + (cc + 1) * 128] = unrope(dka[:, lanes], slice(None)).astype(BF16)
            o_ref[:, 1024 + cc * 128:1024 + (cc + 1) * 128] = dva[:, lanes].astype(BF16)

    prev = lambda n: jnp.maximum(n * nsub - 1, 0)
    nxt = lambda n: jnp.minimum(n * nsub + nsub, last_blk)
    cur = lambda cb: (lambda n: (n, cb))
    big = lambda cb: pl.BlockSpec((rb, DIL_W), cur(cb))
    small = lambda im: pl.BlockSpec((QB, DIL_W), im)
    tab = pl.BlockSpec((rb, 128), cur(0))
    return pl.pallas_call(
        body, name=name, grid=(s // rb,),
        in_specs=[big(0), big(1), big(2), small(lambda n: (prev(n), 1)), small(lambda n: (prev(n), 2)),
                  small(lambda n: (nxt(n), 0)), big(0), small(lambda n: (nxt(n), 0)), big(0), small(lambda n: (nxt(n), 0)),
                  big(0), small(lambda n: (nxt(n), 0)), tab, tab, tab],
        out_specs=pl.BlockSpec((rb, 3 * DIL_W), cur(0)),
        out_shape=jax.ShapeDtypeStruct((s, 3 * DIL_W), BF16),
        scratch_shapes=[pltpu.VMEM((rb, DIL_W), F32), pltpu.VMEM((rb, DIL_W), F32)],
        compiler_params=_cparams(("parallel",)),
    )(qkv, qkv, qkv, qkv, qkv, qkv, dya, dya, lse, lse, dlt, dlt, tc, ts1, ts2)


def _stream_specs(tr):
    nat = pl.BlockSpec((tr, 128), lambda i, j: (i, j))
    return [nat] + [pl.BlockSpec((dil, tr // dil, 128), lambda i, j: (0, i, j)) for dil in DIL_GROUPS[1:]]


def _dil_merge(o_g, l_g, s):
    tr = min(2048, s)
    nat, sp4, sp16 = _stream_specs(tr)

    def body(o0_ref, l0_ref, o1_ref, l1_ref, o2_ref, l2_ref, ya_ref, lse_ref, o1n, l1n, o2n, l2n):
        for src, dst, dil in ((o1_ref, o1n, 4), (l1_ref, l1n, 4), (o2_ref, o2n, 16), (l2_ref, l2n, 16)):
            for c in range(dil):
                dst[pl.ds(c, tr // dil, stride=dil), :] = src[c]
        l0, l1, l2 = l0_ref[...], l1n[...], l2n[...]
        m = jnp.maximum(jnp.maximum(l0, l1), l2)
        e0, e1, e2 = jnp.exp(l0 - m), jnp.exp(l1 - m), jnp.exp(l2 - m)
        den = e0 + e1 + e2
        ya_ref[...] = ((e0 * o0_ref[...] + e1 * o1n[...] + e2 * o2n[...]) / den).astype(BF16)
        lse_ref[...] = m + jnp.log(den)

    v3 = lambda a, dil: a.reshape(dil, s // dil, DIL_W)
    return pl.pallas_call(
        body, name="dil_merge", grid=(s // tr, 4),
        in_specs=[nat, nat, sp4, sp4, sp16, sp16], out_specs=[nat, nat],
        out_shape=[jax.ShapeDtypeStruct((s, DIL_W), BF16), jax.ShapeDtypeStruct((s, DIL_W), F32)],
        scratch_shapes=[pltpu.VMEM((tr, 128), F32)] * 4,
        compiler_params=_cparams(("parallel", "parallel")),
    )(o_g[0], l_g[0], v3(o_g[1], 4), v3(l_g[1], 4), v3(o_g[2], 16), v3(l_g[2], 16))


def _dil_bwd_prep(d_ya, ya, lse, s):
    tr = min(2048, s)
    nat, sp4, sp16 = _stream_specs(tr)

    def body(dya_ref, ya_ref, lse_ref, dy0, dl0, dy1, ls1, dl1, dy2, ls2, dl2, dlt):
        lane_lo = lax.broadcasted_iota(jnp.int32, (tr, 128), 1) < 64
        prod = dya_ref[...] * ya_ref[...].astype(F32)
        lo = jnp.where(lane_lo, prod, 0.0)
        dlt[...] = jnp.where(lane_lo, jnp.sum(lo, axis=1, keepdims=True), jnp.sum(prod - lo, axis=1, keepdims=True))
        dy0[...] = dya_ref[...].astype(BF16)
        dl0[...] = dlt[...]
        for dil, dy, ls, dl in ((4, dy1, ls1, dl1), (16, dy2, ls2, dl2)):
            for c in range(dil):
                rows = pl.ds(c, tr // dil, stride=dil)
                dy[c] = dya_ref[rows, :].astype(BF16)
                ls[c] = lse_ref[rows, :]
                dl[c] = dlt[rows, :]

    sh = lambda dil, dt: jax.ShapeDtypeStruct((dil, s // dil, DIL_W), dt)
    res = pl.pallas_call(
        body, name="dil_bwd_prep", grid=(s // tr, 4),
        in_specs=[nat, nat, nat], out_specs=[nat, nat, sp4, sp4, sp4, sp16, sp16, sp16],
        out_shape=[jax.ShapeDtypeStruct((s, DIL_W), BF16), jax.ShapeDtypeStruct((s, DIL_W), F32),
                   sh(4, BF16), sh(4, F32), sh(4, F32), sh(16, BF16), sh(16, F32), sh(16, F32)],
        scratch_shapes=[pltpu.VMEM((tr, 128), F32)],
        compiler_params=_cparams(("parallel", "parallel")),
    )(d_ya, ya, lse)
    dy0, dl0, dy1, ls1, dl1, dy2, ls2, dl2 = [r.reshape(s, DIL_W) for r in res]
    return [(dy0, lse, dl0), (dy1, ls1, dl1), (dy2, ls2, dl2)]


_RET_SEGS = ((0, 256), (1024, 256), (2048, 512), (4096, 512))


def _split_w_in(win):
    per_head = [win[a:a + RET_HEADS * n].reshape(RET_HEADS, n, D_MODEL) for a, n in _RET_SEGS]
    w_ret = jnp.concatenate(per_head, axis=1).reshape(RET_HEADS * 1536, D_MODEL)
    w_dil = [jnp.concatenate([win[a + DIL_W * g:a + DIL_W * (g + 1)] for a in (6144, 7680, 9216)], axis=0) for g in range(3)]
    return w_ret, win[10752:12800], w_dil


def _join_w_in(g_ret, g_gate, g_dil):
    g_ret = g_ret.reshape(RET_HEADS, 1536, D_MODEL)
    off = (0, 256, 512, 1024, 1536)
    parts = [g_ret[:, off[i]:off[i + 1]].reshape(-1, D_MODEL) for i in range(4)]
    dil = [g_dil[g][DIL_W * i:DIL_W * (i + 1)] for i in range(3) for g in range(3)]
    return jnp.concatenate(parts + dil + [g_gate], axis=0)


def _local_step(xs, pb, tgt, tabs, wts, vec, s, shards=None):
    tm = min(2048, s)
    tr = min(512, s)
    mm = functools.partial(_matmul, tm=tm)
    on_mesh = shards is not None
    wts, vec = dict(wts), dict(vec)
    blocks = lambda g: g.reshape(N_DEV, g.shape[0] // N_DEV, g.shape[1])

    late_shards = dict(shards) if on_mesh else {}
    first = _TwoLevelGather([late_shards.pop("w_in"), late_shards.pop("b_gate")]) if on_mesh else None
    u, gathered = _prenorm(xs, vec["g_pre_mix"], s, carry=first)
    if on_mesh:
        wts["w_in"] = gathered[0].reshape(N_DEV * gathered[0].shape[1], D_MODEL)
        bias = gathered[1].transpose(1, 0, 2).reshape(2, D_MODEL)
        vec.update(b0=bias[0:1], b1=bias[1:2])
    w_ret, w_gate, w_dil = _split_w_in(wts["w_in"])
    proj_ret = mm(u[0], w_ret, mode="nt", m=s, n=6144, k=1024, tn=1024, tk=1024, out_dtype=BF16, name="inproj_ret")
    proj_gate = mm(u[0], w_gate, mode="nt", m=s, n=2048, k=1024, tn=1024, tk=1024, out_dtype=BF16, name="inproj_gate")
    qkv = [_matmul(u[g], w_dil[g], mode="nt", m=s, n=1536, k=1024, tm=min(1024, s), tn=1536, tk=1024, out_dtype=BF16,
                   name="inproj_dil%d" % g, epi=tabs["dil"][g], epi_width=128, epi_fn=_rope_qk) for g in range(3)]

    names = list(late_shards) if on_mesh else []
    gather = _Exchange([late_shards[n] for n in names], [False] * len(names)) if on_mesh else None
    (yr, y_ret, rstate), gathered = _ret_fwd(proj_ret, tabs["cos_r"], tabs["sin_r"], s, carry=gather)
    wts.update({n: g.reshape(N_DEV * g.shape[1], g.shape[2]) for n, g in zip(names, gathered)})
    a_br = mm(yr, wts["w_ret_out"], mode="nn", m=s, n=1024, k=2048, tn=1024, tk=1024, out_dtype=BF16, name="ret_out")

    o_g, l_g = [], []
    for g, dil in enumerate(DIL_GROUPS):
        o, l = _dil_fwd(qkv[g], dil, s, "dil_fwd%d" % g)
        o_g.append(o)
        l_g.append(l)
    ya, lse = _dil_merge(o_g, l_g, s)
    b_br = mm(ya, wts["w_dil_out"], mode="nt", m=s, n=1024, k=512, tn=1024, tk=512, out_dtype=BF16, name="dil_out")

    def gate_mix(a, b, gr, ga, b0, b1):
        return [_sigmoid(gr.astype(F32) + b0) * a.astype(F32) + _sigmoid(ga.astype(F32) + b1) * b.astype(F32)], []

    (mixed,), _ = _rowwise("gate_mix", gate_mix, s, tr, [(a_br, 1024, 0), (b_br, 1024, 0), (proj_gate, 1024, 0), (proj_gate, 1024, 1)],
                           [vec["b0"], vec["b1"]], [(1024, BF16)])
    z = mm(mixed, wts["w_o"], mode="nn", m=s, n=1024, k=1024, tn=1024, tk=1024, out_dtype=BF16, name="w_o")

    def post_norm(h, f, g_post, g_pre):
        hn = h + _rms(f) * g_post
        return [hn, _rms(hn) * g_pre], []

    (h1, v2), _ = _rowwise("post_mix", post_norm, s, tr, [(xs, 1024, 0), (z, 1024, 0)], [vec["g_post_mix"], vec["g_pre_mlp"]],
                           [(1024, F32), (1024, BF16)])
    a_up = mm(v2, wts["w_up"], mode="nt", m=s, n=4096, k=1024, tn=1024, tk=1024, out_dtype=BF16, name="mlp_up")
    f_dn = mm(a_up, wts["w_down"], mode="nn", m=s, n=1024, k=4096, tn=1024, tk=1024, out_dtype=BF16, name="mlp_down", a_fn=_relu_sq)
    (h2, t_ple), _ = _rowwise("post_mlp", post_norm, s, tr, [(h1, 1024, 0), (f_dn, 1024, 0)], [vec["g_post_mlp"], vec["g_pre_ple"]],
                              [(1024, F32), (1024, BF16)])
    gl = mm(t_ple, wts["w_ple_gate"], mode="nn", m=s, n=1024, k=1024, tn=1024, tk=1024, out_dtype=BF16, name="ple_gate")
    e_ple = mm(pb, wts["w_ple_in"], mode="nt", m=s, n=1024, k=256, tn=1024, tk=256, out_dtype=BF16, name="ple_in")

    def ple_loss(h, glv, e, tg, b, g):
        gate = _sigmoid(glv + b)
        ge = gate * e
        diff = h + _rms(ge) * g - tg
        dy = diff * (1.0 / D_MODEL)
        d_ge, dg = _rms_bwd(ge, g, dy)
        d_gl = d_ge * e * gate * (1.0 - gate)
        loss = jnp.zeros((1, D_MODEL), F32) + 0.5 * jnp.sum(diff * diff) * (1.0 / D_MODEL)
        return [dy, d_gl, d_ge * gate], [_colsum(dg), _colsum(d_gl), loss]

    (dy, d_gl, d_e), (dg_post_ple, db_ple, loss) = _rowwise(
        "ple_loss", ple_loss, s, tr, [(h2, 1024, 0), (gl, 1024, 0), (e_ple, 1024, 0), (tgt, 1024, 0)],
        [vec["b_ple"], vec["g_post_ple"]], [(1024, F32), (1024, BF16), (1024, BF16)], [1024, 1024, 1024])

    ts = min(1024, s)
    wg = functools.partial(_matmul, mode="tn", k=s, tk=ts, out_dtype=BF16)
    grads = {}
    grads["w_ple_in"] = wg(d_e, pb, m=1024, n=256, tm=1024, tn=256, name="g_ple_in")
    grads["w_ple_gate"] = wg(t_ple, d_gl, m=1024, n=1024, tm=1024, tn=1024, name="g_ple_gate")
    d_t = mm(d_gl, wts["w_ple_gate"], mode="nt", m=s, n=1024, k=1024, tn=1024, tk=1024, out_dtype=BF16, name="d_t")

    def bwd_ple_mlp(h, dt, dyv, f, g_pre, g_post):
        dx, dg1 = _rms_bwd(h, g_pre, dt)
        dh = dyv + dx
        df, dg2 = _rms_bwd(f, g_post, dh)
        return [dh, df], [_colsum(dg1), _colsum(dg2)]

    (d_h2, d_f), (dg_pre_ple, dg_post_mlp) = _rowwise(
        "bwd_ple_mlp", bwd_ple_mlp, s, tr, [(h2, 1024, 0), (d_t, 1024, 0), (dy, 1024, 0), (f_dn, 1024, 0)],
        [vec["g_pre_ple"], vec["g_post_mlp"]], [(1024, F32), (1024, BF16)], [1024, 1024])
    d_a = mm(d_f, wts["w_down"], mode="nt", m=s, n=4096, k=1024, tn=1024, tk=1024, out_dtype=BF16, name="d_a",
             epi=(a_up,), epi_fn=lambda acc, av: acc * (2.0 * jnp.maximum(av.astype(F32), 0.0)))
    grads["w_down"] = wg(a_up, d_f, m=4096, n=1024, tm=2048, tn=1024, name="g_down", a_fn=_relu_sq)
    grads["w_up"] = wg(d_a, v2, m=4096, n=1024, tm=2048, tn=1024, name="g_up")
    d_v2 = mm(d_a, wts["w_up"], mode="nn", m=s, n=1024, k=4096, tn=1024, tk=1024, out_dtype=BF16, name="d_v2")

    (d_h1, d_z), (dg_pre_mlp, dg_post_mix) = _rowwise(
        "bwd_mlp_mix", bwd_ple_mlp, s, tr, [(h1, 1024, 0), (d_v2, 1024, 0), (d_h2, 1024, 0), (z, 1024, 0)],
        [vec["g_pre_mlp"], vec["g_post_mix"]], [(1024, F32), (1024, BF16)], [1024, 1024])
    d_mixed = mm(d_z, wts["w_o"], mode="nt", m=s, n=1024, k=1024, tn=1024, tk=1024, out_dtype=BF16, name="d_mixed")
    grads["w_o"] = wg(mixed, d_z, m=1024, n=1024, tm=1024, tn=1024, name="g_o")

    def bwd_gate(dm, a, b, gr, ga, b0, b1):
        sa, sb = _sigmoid(gr.astype(F32) + b0), _sigmoid(ga.astype(F32) + b1)
        dgr = dm * a.astype(F32) * sa * (1.0 - sa)
        dga = dm * b.astype(F32) * sb * (1.0 - sb)
        return [dm * sa, dm * sb, jnp.concatenate([dgr, dga], axis=1)], [_colsum(dgr), _colsum(dga)]

    (d_abr, d_bbr, dproj_gate), (db0, db1) = _rowwise(
        "bwd_gate", bwd_gate, s, tr, [(d_mixed, 1024, 0), (a_br, 1024, 0), (b_br, 1024, 0), (proj_gate, 1024, 0), (proj_gate, 1024, 1)],
        [vec["b0"], vec["b1"]], [(1024, BF16), (1024, BF16), (2048, BF16)], [1024, 1024])
    grads["w_ret_out"] = wg(yr, d_abr, m=2048, n=1024, tm=2048, tn=1024, name="g_ret_out")
    d_yr = mm(d_abr, wts["w_ret_out"], mode="nt", m=s, n=2048, k=1024, tn=1024, tk=1024, out_dtype=BF16, name="d_yr")
    grads["w_dil_out"] = wg(d_bbr, ya, m=1024, n=512, tm=1024, tn=512, name="g_dil_out")
    d_ya = mm(d_bbr, wts["w_dil_out"], mode="nn", m=s, n=512, k=1024, tn=512, tk=1024, out_dtype=F32, name="d_ya")

    slots = {}
    names = list(grads) if on_mesh else []
    shares = _Exchange([blocks(grads[n]) for n in names], [True] * len(names)) if on_mesh else None
    (dproj_ret,), got = _ret_bwd(proj_ret, tabs["cos_r"], tabs["sin_r"], y_ret, d_yr, rstate, s, carry=shares)
    slots.update(zip(names, got))
    upstream = _dil_bwd_prep(d_ya, ya, lse, s)
    dqkv = [_dil_bwd(qkv[g], *upstream[g], *tabs["dil"][g], dil, s, "dil_bwd%d" % g)
            for g, dil in enumerate(DIL_GROUPS)]

    g_ret = wg(dproj_ret, u[0], m=6144, n=1024, tm=2048, tn=1024, name="g_in_ret")
    g_gate = wg(dproj_gate, u[0], m=2048, n=1024, tm=2048, tn=1024, name="g_in_gate")
    g_dil = [wg(dqkv[g], u[g], m=1536, n=1024, tm=1536, tn=1024, name="g_in_dil%d" % g) for g in range(3)]
    grads["w_in"] = _join_w_in(g_ret, g_gate, g_dil)

    du_ret = functools.partial(mm, dproj_ret, w_ret, mode="nn", m=s, n=1024, k=6144, tn=1024, tk=1024, out_dtype=BF16, name="du_ret")
    if on_mesh:
        du_ret, (slots["w_in"],) = du_ret(carry=_Exchange([blocks(grads["w_in"])], [True]))
    else:
        du_ret = du_ret()
    du_gate = mm(dproj_gate, w_gate, mode="nn", m=s, n=1024, k=2048, tn=1024, tk=1024, out_dtype=BF16, name="du_gate")
    du_dil = [mm(dqkv[g], w_dil[g], mode="nn", m=s, n=1024, k=1536, tn=1024, tk=1536, out_dtype=BF16, name="du_dil%d" % g)
              for g in range(3)]

    grad_x, dg_pre_mix = _grad_x(xs, d_h1, (du_ret, du_gate, du_dil[0]), du_dil[1], du_dil[2], vec["g_pre_mix"], s)

    zero = jnp.zeros((1, D_MODEL), F32)
    packet = jnp.concatenate([dg_pre_mix, dg_post_mix, dg_pre_mlp, dg_post_mlp, dg_pre_ple, db_ple, dg_post_ple, loss,
                              db0, db1] + [zero] * 6, axis=0)
    return grad_x, (slots if on_mesh else grads), packet


def _mesh_pos():
    return lax.axis_index("x"), lax.axis_index("y"), lax.axis_index("c")


class _Exchange:
    def __init__(self, arrays, scatter):
        self.arrays, self.scatter, self.n = list(arrays), list(scatter), len(arrays)
        self.out_shape = [jax.ShapeDtypeStruct(a.shape if sc else (N_DEV,) + a.shape, a.dtype)
                          for a, sc in zip(self.arrays, self.scatter)]
        self.scratch = [pltpu.SemaphoreType.DMA((self.n * 7,)), pltpu.SemaphoreType.DMA((self.n * 7,)),
                        pltpu.SemaphoreType.DMA((self.n,))]
        self.specs = [pl.BlockSpec(memory_space=pl.ANY)] * self.n

    def _copies(self, srcs, dsts, sems):
        send_sems, recv_sems, local_sems = sems
        x, y, c = _mesh_pos()
        my = 4 * x + 2 * y + c
        src_of = lambda w, idx: srcs[w].at[idx] if self.scatter[w] else srcs[w]
        local = [pltpu.make_async_copy(src_of(w, my), dsts[w].at[my], local_sems.at[w]) for w in range(self.n)]
        sends, recvs = [], []
        for w in range(self.n):
            for r in range(1, N_DEV):
                px = 1 - x if r & 4 else x
                py = 1 - y if r & 2 else y
                pc = 1 - c if r & 1 else c
                pidx = 4 * px + 2 * py + pc
                kw = dict(send_sem=send_sems.at[w * 7 + r - 1], recv_sem=recv_sems.at[w * 7 + r - 1],
                          device_id=(px, py, pc), device_id_type=MESH)
                sends.append(pltpu.make_async_remote_copy(src_ref=src_of(w, pidx), dst_ref=dsts[w].at[my], **kw))
                recvs.append(pltpu.make_async_remote_copy(src_ref=src_of(w, pidx), dst_ref=dsts[w].at[pidx], **kw))
        return local, sends, recvs

    def start(self, srcs, dsts, sems):
        local, sends, _ = self._copies(srcs, dsts, sems)
        for cp in local + sends:
            cp.start()

    def wait(self, srcs, dsts, sems):
        local, sends, recvs = self._copies(srcs, dsts, sems)
        for cp in recvs:
            cp.wait_recv()
        for cp in sends:
            cp.wait_send()
        for cp in local:
            cp.wait()

    def split(self, refs, n_in, n_out):
        srcs = refs[n_in:n_in + self.n]
        dsts = refs[n_in + self.n + n_out:n_in + 2 * self.n + n_out]
        return srcs, dsts, refs[len(refs) - 3:]


class _TwoLevelGather(_Exchange):
    def __init__(self, arrays):
        super().__init__(arrays, [False] * len(arrays))

    def _plan(self, srcs, dsts, sems):
        send_sems, recv_sems, local_sems = sems
        x, y, c = _mesh_pos()
        me, sibling = (x, y, c), (x, y, 1 - c)
        chips = [(1 - x, y), (x, 1 - y), (1 - x, 1 - y)]
        region = lambda w, dev: dsts[w].at[4 * dev[0] + 2 * dev[1] + dev[2]]

        def copy(w, kk, block, to, src=None):
            return pltpu.make_async_remote_copy(
                src_ref=region(w, block) if src is None else src, dst_ref=region(w, block),
                send_sem=send_sems.at[w * 7 + kk], recv_sem=recv_sems.at[w * 7 + kk], device_id=to, device_id_type=MESH)

        mine = [pltpu.make_async_copy(srcs[w], region(w, me), local_sems.at[w]) for w in range(self.n)]
        first = []
        for w in range(self.n):
            first.append(copy(w, 0, me, sibling, src=srcs[w]))
            first += [copy(w, 1 + j, me, (*chip, c), src=srcs[w]) for j, chip in enumerate(chips)]
        return me, sibling, chips, c, copy, mine, first

    def start(self, srcs, dsts, sems):
        *_, mine, first = self._plan(srcs, dsts, sems)
        for cp in mine + first:
            cp.start()

    def wait(self, srcs, dsts, sems):
        me, sibling, chips, c, copy, mine, first = self._plan(srcs, dsts, sems)
        passed = []
        for j, chip in enumerate(chips):
            for w in range(self.n):
                copy(w, 1 + j, (*chip, c), me).wait_recv()
                cp = copy(w, 4 + j, (*chip, c), sibling)
                cp.start()
                passed.append(cp)
        for w in range(self.n):
            copy(w, 0, sibling, me).wait_recv()
            for j, chip in enumerate(chips):
                copy(w, 4 + j, (*chip, 1 - c), me).wait_recv()
        for cp in first + passed:
            cp.wait_send()
        for cp in mine:
            cp.wait()


def _run_exchange(ex, name):
    def body(*refs):
        parts = ex.split(refs, 0, 0)
        ex.start(*parts)
        ex.wait(*parts)

    return pl.pallas_call(body, name=name, in_specs=ex.specs, out_specs=ex.specs, out_shape=ex.out_shape,
                          scratch_shapes=ex.scratch)(*ex.arrays)


def _pick_rows(r, c, target_bytes):
    t = r
    while (t // 2) % 16 == 0 and t // 2 >= 16 and t * c * 4 > target_bytes:
        t //= 2
    return t


def _sum_slots(slots, name):
    ns, r, c = slots.shape
    tr = _pick_rows(r, c, 256 * 1024)

    def body(s_ref, o_ref):
        acc = s_ref[0].astype(F32)
        for kk in range(1, ns):
            acc = acc + s_ref[kk].astype(F32)
        o_ref[...] = acc

    return pl.pallas_call(
        body, name=name, grid=(r // tr,),
        in_specs=[pl.BlockSpec((ns, tr, c), lambda i: (0, i, 0))], out_specs=pl.BlockSpec((tr, c), lambda i: (i, 0)),
        out_shape=jax.ShapeDtypeStruct((r, c), F32), compiler_params=_cparams(("parallel",)),
    )(slots)


def _adamw(slots, w, m, v, name):
    ns, r, c = slots.shape
    tr = _pick_rows(r, c, 256 * 1024)

    def body(s_ref, w_ref, m_ref, v_ref, g_out, d_out, m_out, v_out):
        g = s_ref[0].astype(F32)
        for kk in range(1, ns):
            g = g + s_ref[kk].astype(F32)
        mn = ADAM_B1 * m_ref[...] + (1.0 - ADAM_B1) * g
        vn = ADAM_B2 * v_ref[...] + (1.0 - ADAM_B2) * (g * g)
        m_hat = mn / (1.0 - ADAM_B1 ** ADAM_STEP)
        v_hat = vn / (1.0 - ADAM_B2 ** ADAM_STEP)
        g_out[...] = g
        d_out[...] = -ADAM_LR * (m_hat / (jnp.sqrt(v_hat) + ADAM_EPS) + ADAM_WD * w_ref[...])
        m_out[...] = mn
        v_out[...] = vn

    blk = pl.BlockSpec((tr, c), lambda i: (i, 0))
    return pl.pallas_call(
        body, name=name, grid=(r // tr,),
        in_specs=[pl.BlockSpec((ns, tr, c), lambda i: (0, i, 0)), blk, blk, blk], out_specs=[blk] * 4,
        out_shape=[jax.ShapeDtypeStruct((r, c), F32)] * 4, compiler_params=_cparams(("parallel",)),
    )(slots, w, m, v)


def _rotary_tables(pos, s):
    posf = pos.astype(F32)
    inv_freq = 1.0 / (10000.0 ** jnp.linspace(0.0, 1.0, RET_QK // 2, dtype=F32))
    ang = posf[:, None] * inv_freq
    tabs = {"cos_r": jnp.cos(ang), "sin_r": jnp.sin(ang), "dil": []}
    freqs = 500000.0 ** (-jnp.arange(0, 16, 2, dtype=F32) / 16)
    spread = np.zeros((16, 384), np.float32)
    bias = np.zeros((1, 384), np.float32)
    for head in range(2):
        for i in range(8):
            spread[i, 64 * head + i] = spread[i, 64 * head + 8 + i] = 1.0
            spread[8 + i, 128 + 64 * head + i] = -1.0
            spread[8 + i, 256 + 64 * head + 8 + i] = 1.0
        bias[0, 64 * head + 16:64 * head + 64] = 1.0

    def expand(t, e, b):
        hi = t.astype(BF16)
        lo = (t - hi.astype(F32)).astype(BF16)
        out = _dot(hi, e, NN) + _dot(lo, e, NN) + b
        return [out[:, 0:128], out[:, 128:256], out[:, 256:384]], []

    for g, dil in enumerate(DIL_GROUPS):
        ang = posf.reshape(s // dil, dil).T.reshape(s, 1) * freqs
        cs = jnp.concatenate([jnp.cos(ang), jnp.sin(ang)], axis=1)
        t3, _ = _rowwise("rot_tables%d" % g, expand, s, min(1024, s), [(cs, 16, 0)],
                         [jnp.asarray(spread, BF16), jnp.asarray(bias)], [(128, F32)] * 3)
        tabs["dil"].append(tuple(t3))
    return tabs


_TRANSPOSED = ("w_in", "w_dil_out", "w_up", "w_ple_in")
_MATS = ("w_in", "w_ret_out", "w_dil_out", "w_o", "w_up", "w_down", "w_ple_gate", "w_ple_in")
_VECS = ("g_pre_mix", "g_post_mix", "g_pre_mlp", "g_post_mlp", "g_pre_ple", "b_ple_gate", "g_post_ple")
_ORDER = ("w_in", "b_gate", "w_ret_out", "w_dil_out", "w_o", "g_pre_mix", "g_post_mix", "g_pre_mlp", "g_post_mlp", "w_up",
          "w_down", "g_pre_ple", "w_ple_gate", "b_ple_gate", "w_ple_in", "g_post_ple")


def kernel(x, p, positions, w_in, b_gate, w_ret_out, w_dil_out, w_o, g_pre_mix, g_post_mix, g_pre_mlp, g_post_mlp, w_up, w_down, g_pre_ple, w_ple_gate, b_ple_gate, w_ple_in, g_post_ple, loss_target, m_w_in, m_b_gate, m_w_ret_out, m_w_dil_out, m_w_o, m_g_pre_mix, m_g_post_mix, m_g_pre_mlp, m_g_post_mlp, m_w_up, m_w_down, m_g_pre_ple, m_w_ple_gate, m_b_ple_gate, m_w_ple_in, m_g_post_ple, v_w_in, v_b_gate, v_w_ret_out, v_w_dil_out, v_w_o, v_g_pre_mix, v_g_post_mix, v_g_pre_mlp, v_g_post_mlp, v_w_up, v_w_down, v_g_pre_ple, v_w_ple_gate, v_b_ple_gate, v_w_ple_in, v_g_post_ple):
    s = x.shape[1]
    wd = dict(w_in=w_in, b_gate=b_gate, w_ret_out=w_ret_out, w_dil_out=w_dil_out, w_o=w_o, g_pre_mix=g_pre_mix,
              g_post_mix=g_post_mix, g_pre_mlp=g_pre_mlp, g_post_mlp=g_post_mlp, w_up=w_up, w_down=w_down,
              g_pre_ple=g_pre_ple, w_ple_gate=w_ple_gate, b_ple_gate=b_ple_gate, w_ple_in=w_ple_in, g_post_ple=g_post_ple)
    md = dict(w_in=m_w_in, b_gate=m_b_gate, w_ret_out=m_w_ret_out, w_dil_out=m_w_dil_out, w_o=m_w_o, g_pre_mix=m_g_pre_mix,
              g_post_mix=m_g_post_mix, g_pre_mlp=m_g_pre_mlp, g_post_mlp=m_g_post_mlp, w_up=m_w_up, w_down=m_w_down,
              g_pre_ple=m_g_pre_ple, w_ple_gate=m_w_ple_gate, b_ple_gate=m_b_ple_gate, w_ple_in=m_w_ple_in, g_post_ple=m_g_post_ple)
    vd = dict(w_in=v_w_in, b_gate=v_b_gate, w_ret_out=v_w_ret_out, w_dil_out=v_w_dil_out, w_o=v_w_o, g_pre_mix=v_g_pre_mix,
              g_post_mix=v_g_post_mix, g_pre_mlp=v_g_pre_mlp, g_post_mlp=v_g_post_mlp, w_up=v_w_up, w_down=v_w_down,
              g_pre_ple=v_g_pre_ple, w_ple_gate=v_w_ple_gate, b_ple_gate=v_b_ple_gate, w_ple_in=v_w_ple_in, g_post_ple=v_g_post_ple)

    shards = {n: (wd[n][0].T if n in _TRANSPOSED else wd[n][0]).astype(BF16) for n in _MATS}
    shards["b_gate"] = b_gate[0]
    vec = {n: wd[n] for n in _VECS}
    vec["b_ple"] = b_ple_gate

    tabs = _rotary_tables(positions[0], s)
    grad_x, slots, packet = _local_step(x[0], p[0, 0].astype(BF16), loss_target[0], tabs, {}, vec, s, shards=shards)

    (packets,) = _run_exchange(_Exchange([packet], [False]), "exchange_vectors")
    out = {}
    for n in _MATS:
        sl = slots[n]
        if n in _TRANSPOSED:
            sl = _sum_slots(sl, "sum_" + n).T[None]
        out[n] = _adamw(sl, wd[n][0], md[n][0], vd[n][0], "adamw_" + n)
    zero_rows = jnp.zeros((16 - len(_VECS), D_MODEL), F32)
    pack = lambda d: jnp.concatenate([d[n] for n in _VECS] + [zero_rows], axis=0)
    small = _adamw(packets, pack(wd), pack(md), pack(vd), "adamw_vectors")
    for i, n in enumerate(_VECS):
        out[n] = tuple(t[i:i + 1] for t in small)
    my = 4 * lax.axis_index("x") + 2 * lax.axis_index("y") + lax.axis_index("c")
    g_bias = lax.dynamic_slice(small[0], (8, my * 128), (2, 128))
    out["b_gate"] = _adamw(g_bias[None], b_gate[0], m_b_gate[0], v_b_gate[0], "adamw_b_gate")
    loss = small[0][7, 0]

    res = [loss, grad_x[None]]
    for kk in range(4):
        res += [out[n][kk][None] if out[n][kk].ndim == 2 and wd[n].ndim == 3 else out[n][kk] for n in _ORDER]
    return tuple(res)
```

```python
import functools
import math

import numpy as np
import jax
import jax.numpy as jnp
from jax import lax
from jax.experimental import pallas as pl
from jax.experimental.pallas import tpu as pltpu

F32, BF16 = jnp.float32, jnp.bfloat16
D_MODEL = 1024
EPS = 1e-6
N_DEV = 8
RET_HEADS, RET_QK, RET_V, RET_CHUNK = 4, 256, 512, 128
DIL_GROUPS = (1, 4, 16)
DIL_W = 512
QB = 128
NEG = -1e30
ADAM_LR, ADAM_B1, ADAM_B2, ADAM_EPS, ADAM_WD, ADAM_STEP = 0.001, 0.9, 0.999, 1e-08, 0.01, 10
VMEM_LIMIT_BYTES = 56 * 1024 * 1024
MESH = pl.DeviceIdType.MESH

NN = ((1,), (0,))
NT = ((1,), (1,))
TN = ((0,), (0,))


def _dot(a, b, dn):
    return lax.dot_general(a, b, (dn, ((), ())), preferred_element_type=F32)


def _cparams(sem):
    return pltpu.CompilerParams(dimension_semantics=sem, vmem_limit_bytes=VMEM_LIMIT_BYTES)


def _rms(x):
    return x * lax.rsqrt(jnp.mean(x * x, axis=-1, keepdims=True) + EPS)


def _rms_bwd(x, g, dy):
    r = lax.rsqrt(jnp.mean(x * x, axis=-1, keepdims=True) + EPS)
    xh = x * r
    t = dy * g
    dx = r * (t - xh * jnp.mean(t * xh, axis=-1, keepdims=True))
    return dx, dy * xh


def _colsum(v):
    return jnp.sum(v, axis=0, keepdims=True)


def _sigmoid(v):
    return 1.0 / (1.0 + jnp.exp(-v))


def _pallas(compute, *, name, grid, in_specs, out_specs, out_shape, scratch, semantics, args, carry=None):
    n_in, n_out = len(in_specs), len(out_specs)
    if carry is None:
        res = pl.pallas_call(compute, name=name, grid=grid, in_specs=in_specs, out_specs=out_specs, out_shape=out_shape,
                             scratch_shapes=scratch, compiler_params=_cparams(semantics))(*args)
        return res, []
    n_steps = math.prod(grid)

    def body(*refs):
        step = 0
        for axis, size in enumerate(grid):
            step = step * size + pl.program_id(axis)
        parts = carry.split(refs, n_in, n_out)
        pl.when(step == 0)(lambda: carry.start(*parts))
        compute(*refs[:n_in], *refs[n_in + carry.n:n_in + carry.n + n_out], *refs[n_in + 2 * carry.n + n_out:len(refs) - 3])
        pl.when(step == n_steps - 1)(lambda: carry.wait(*parts))

    res = pl.pallas_call(
        body, name=name, grid=grid, in_specs=list(in_specs) + carry.specs, out_specs=list(out_specs) + carry.specs,
        out_shape=list(out_shape) + carry.out_shape, scratch_shapes=list(scratch) + carry.scratch,
        compiler_params=_cparams(("arbitrary",) * len(grid)))(*args, *carry.arrays)
    return res[:n_out], res[n_out:]


def _matmul(a, b, *, mode, m, n, k, tm, tn, tk, out_dtype, name, a_fn=None, epi=(), epi_width=None, epi_fn=None, carry=None):
    nk = k // tk
    grid = (m // tm, n // tn, nk)
    if mode == "nn":
        a_blk, a_im, b_blk, b_im, dn = (tm, tk), (lambda i, j, kk: (i, kk)), (tk, tn), (lambda i, j, kk: (kk, j)), NN
    elif mode == "nt":
        a_blk, a_im, b_blk, b_im, dn = (tm, tk), (lambda i, j, kk: (i, kk)), (tn, tk), (lambda i, j, kk: (j, kk)), NT
    else:
        a_blk, a_im, b_blk, b_im, dn = (tk, tm), (lambda i, j, kk: (kk, i)), (tk, tn), (lambda i, j, kk: (kk, j)), TN
    o_im = lambda i, j, kk: (i, j)
    n_in = 2 + len(epi)

    def body(*refs):
        a_ref, b_ref = refs[0], refs[1]
        o_ref = refs[n_in]
        acc_ref = refs[n_in + 1] if nk > 1 else None

        def finish(acc):
            if epi:
                acc = epi_fn(acc, *[r[...] for r in refs[2:n_in]])
            o_ref[...] = acc.astype(o_ref.dtype)

        av = a_ref[...]
        if a_fn is not None:
            av = a_fn(av)
        part = _dot(av, b_ref[...], dn)
        if nk == 1:
            finish(part)
        else:
            kk = pl.program_id(2)

            @pl.when(kk == 0)
            def _():
                acc_ref[...] = part

            @pl.when(kk > 0)
            def _():
                acc_ref[...] += part

            @pl.when(kk == nk - 1)
            def _():
                finish(acc_ref[...])

    epi_spec = pl.BlockSpec((tm, tn), o_im) if epi_width is None else pl.BlockSpec((tm, epi_width), lambda i, j, kk: (i, 0))
    in_specs = [pl.BlockSpec(a_blk, a_im), pl.BlockSpec(b_blk, b_im)] + [epi_spec] * len(epi)
    args = [a, b, *epi]
    (out,), got = _pallas(
        body, name=name, grid=grid, in_specs=in_specs, out_specs=[pl.BlockSpec((tm, tn), o_im)],
        out_shape=[jax.ShapeDtypeStruct((m, n), out_dtype)], scratch=[pltpu.VMEM((tm, tn), F32)] if nk > 1 else [],
        semantics=("parallel", "parallel", "arbitrary"), args=args, carry=carry)
    return out if carry is None else (out, got)


def _relu_sq(v):
    r = jnp.maximum(v.astype(F32), 0.0)
    return (r * r).astype(BF16)


def _rowwise(name, fn, s, tr, rows, vecs, outs, accs=()):
    n_r, n_v, n_o, n_a = len(rows), len(vecs), len(outs), len(accs)

    def body(*refs):
        vals = [refs[i][...].astype(F32) for i in range(n_r)] + [refs[n_r + i][...] for i in range(n_v)]
        o_refs = refs[n_r + n_v:n_r + n_v + n_o]
        a_refs = refs[n_r + n_v + n_o:]
        o_vals, a_vals = fn(*vals)
        for ref, val in zip(o_refs, o_vals):
            ref[...] = val.astype(ref.dtype)
        if n_a:
            @pl.when(pl.program_id(0) == 0)
            def _():
                for ref in a_refs:
                    ref[...] = jnp.zeros_like(ref)

            for ref, val in zip(a_refs, a_vals):
                ref[...] += val

    in_specs = [pl.BlockSpec((tr, w), functools.partial(lambda i, cb: (i, cb), cb=cb)) for _, w, cb in rows]
    in_specs += [pl.BlockSpec(v.shape, lambda i: (0, 0)) for v in vecs]
    out_specs = [pl.BlockSpec((tr, w), lambda i: (i, 0)) for w, _ in outs]
    out_specs += [pl.BlockSpec((1, w), lambda i: (0, 0)) for w in accs]
    out_shape = [jax.ShapeDtypeStruct((s, w), dt) for w, dt in outs]
    out_shape += [jax.ShapeDtypeStruct((1, w), F32) for w in accs]
    res = pl.pallas_call(
        body, name=name, grid=(s // tr,), in_specs=in_specs, out_specs=out_specs, out_shape=out_shape,
        compiler_params=_cparams(("arbitrary",)),
    )(*[r[0] for r in rows], *vecs)
    return res[:n_o], res[n_o:]


_ROW_TILE = 512
_STREAM_SPECS = [pl.BlockSpec((dil, _ROW_TILE // dil, D_MODEL), lambda i: (0, i, 0)) for dil in DIL_GROUPS[1:]]
_NAT_SPEC = pl.BlockSpec((_ROW_TILE, D_MODEL), lambda i: (i, 0))
_VEC_SPEC = pl.BlockSpec((1, D_MODEL), lambda i: (0, 0))
_COL_BLOCKS = pltpu.VMEM((D_MODEL // 128, _ROW_TILE, 128), F32)


def _prenorm(xs, g, s, carry=None):
    tr = _ROW_TILE

    def body(x_ref, g_ref, u_ref, u4_ref, u16_ref, buf):
        xn = _rms(x_ref[...]) * g_ref[...]
        u_ref[...] = xn.astype(BF16)
        for cb in range(8):
            buf[cb] = xn[:, cb * 128:(cb + 1) * 128]
        for dil, out in ((4, u4_ref), (16, u16_ref)):
            for c in range(dil):
                rows = pl.ds(c, tr // dil, stride=dil)
                out[c] = jnp.concatenate([buf.at[cb][rows, :] for cb in range(8)], axis=1).astype(BF16)

    res, got = _pallas(
        body, name="prenorm", grid=(s // tr,), in_specs=[_NAT_SPEC, _VEC_SPEC], out_specs=[_NAT_SPEC] + _STREAM_SPECS,
        out_shape=[jax.ShapeDtypeStruct((s, D_MODEL), BF16)]
        + [jax.ShapeDtypeStruct((dil, s // dil, D_MODEL), BF16) for dil in DIL_GROUPS[1:]],
        scratch=[_COL_BLOCKS], semantics=("parallel",), args=(xs, g), carry=carry)
    return [r.reshape(s, D_MODEL) for r in res], got


def _grad_x(xs, d_h1, du_nat, du4, du16, g, s):
    tr = _ROW_TILE

    def body(x_ref, dh_ref, a_ref, b_ref, c_ref, u4_ref, u16_ref, g_ref, dx_ref, dg_ref, buf):
        du = a_ref[...].astype(F32) + b_ref[...].astype(F32) + c_ref[...].astype(F32)
        for dil, src in ((4, u4_ref), (16, u16_ref)):
            for c in range(dil):
                part = src[c].astype(F32)
                for cb in range(8):
                    buf.at[cb][pl.ds(c, tr // dil, stride=dil), :] = part[:, cb * 128:(cb + 1) * 128]
            du = du + jnp.concatenate([buf[cb] for cb in range(8)], axis=1)
        dx, dgr = _rms_bwd(x_ref[...], g_ref[...], du)
        dx_ref[...] = dh_ref[...] + dx

        @pl.when(pl.program_id(0) == 0)
        def _():
            dg_ref[...] = jnp.zeros_like(dg_ref)

        dg_ref[...] += _colsum(dgr)

    return pl.pallas_call(
        body, name="grad_x", grid=(s // tr,), in_specs=[_NAT_SPEC] * 5 + _STREAM_SPECS + [_VEC_SPEC],
        out_specs=[_NAT_SPEC, _VEC_SPEC],
        out_shape=[jax.ShapeDtypeStruct((s, D_MODEL), F32), jax.ShapeDtypeStruct((1, D_MODEL), F32)],
        scratch_shapes=[_COL_BLOCKS], compiler_params=_cparams(("arbitrary",)),
    )(xs, d_h1, *du_nat, du4.reshape(4, s // 4, D_MODEL), du16.reshape(16, s // 16, D_MODEL), g)


def _ret_tables():
    h = np.arange(RET_HEADS, dtype=np.float32)
    lg = np.log1p(-(np.float32(2.0) ** (-5.0 - h))).astype(np.float32)
    idx = np.arange(RET_CHUNK, dtype=np.float32)
    diff = idx[:, None] - idx[None, :]
    dm = np.where(diff[None] >= 0, np.exp(np.maximum(diff, 0.0)[None] * lg[:, None, None]), 0.0)
    qd = np.exp((idx + 1.0)[None, :, None] * lg[:, None, None])
    kd = np.exp((RET_CHUNK - 1.0 - idx)[None, :, None] * lg[:, None, None])
    cd = np.exp(RET_CHUNK * lg)[:, None, None]
    return [jnp.asarray(t, F32) for t in (dm, qd, kd, cd)]


def _rope_half(v, cos, sin):
    v1, v2 = v[:, :128], v[:, 128:]
    return jnp.concatenate([v1 * cos - v2 * sin, v2 * cos + v1 * sin], axis=1)


def _unrope_half(d, cos, sin):
    d1, d2 = d[:, :128], d[:, 128:]
    return jnp.concatenate([d1 * cos + d2 * sin, d2 * cos - d1 * sin], axis=1)


_RET_HEADS_FWD, _RET_HEADS_BWD = 1, 2


def _ret_specs(rb, rev_n, hp):
    def rowmap(w_blk):
        return lambda h, n: (rev_n(n), w_blk(h))
    tab = [pl.BlockSpec((hp, RET_CHUNK, RET_CHUNK), lambda h, n: (h, 0, 0)),
           pl.BlockSpec((hp, RET_CHUNK, 1), lambda h, n: (h, 0, 0)),
           pl.BlockSpec((hp, RET_CHUNK, 1), lambda h, n: (h, 0, 0)),
           pl.BlockSpec((hp, 1, 1), lambda h, n: (h, 0, 0))]
    proj = pl.BlockSpec((rb, hp * 1536), rowmap(lambda h: h))
    cs = pl.BlockSpec((rb, 128), rowmap(lambda h: 0))
    hv = pl.BlockSpec((rb, hp * RET_V), rowmap(lambda h: h))
    return proj, cs, hv, tab


def _ret_fwd(proj_ret, cos, sin, s, carry=None):
    rb = min(1024, s)
    ch = rb // RET_CHUNK
    nb = s // rb
    hp = _RET_HEADS_FWD
    proj_spec, cs_spec, hv_spec, tab_specs = _ret_specs(rb, lambda n: n, hp)

    def body(p_ref, cos_ref, sin_ref, dm_ref, qd_ref, kd_ref, cd_ref, yr_ref, y_ref, rs_ref, r_acc):
        @pl.when(pl.program_id(1) == 0)
        def _():
            r_acc[...] = jnp.zeros_like(r_acc)

        for c, hh in [(c, hh) for c in range(ch) for hh in range(hp)]:
            rows = slice(c * RET_CHUNK, (c + 1) * RET_CHUNK)
            pc, hc = hh * 1536, hh * RET_V
            dm, qd, kd, cd = dm_ref[hh], qd_ref[hh], kd_ref[hh], cd_ref[hh]
            cosv, sinv = cos_ref[rows, :], sin_ref[rows, :]
            q = _rope_half(p_ref[rows, pc:pc + 256].astype(F32), cosv, sinv)
            kk = _rope_half(p_ref[rows, pc + 256:pc + 512].astype(F32), cosv, sinv) * (RET_QK ** -0.5)
            v = p_ref[rows, pc + 512:pc + 1024]
            g = p_ref[rows, pc + 1024:pc + 1536].astype(F32)
            rb16 = r_acc[hh].astype(BF16)
            rs_ref[hh, c] = rb16
            sc = _dot(q.astype(BF16), kk.astype(BF16), NT) * dm
            y = _dot(sc.astype(BF16), v, NN) + _dot((q * qd).astype(BF16), rb16, NN)
            r_acc[hh] = r_acc[hh] * cd + _dot((kk * kd).astype(BF16), v, TN)
            y_ref[rows, hc:hc + RET_V] = y.astype(BF16)
            yr_ref[rows, hc:hc + RET_V] = (_rms(y) * (g * _sigmoid(g))).astype(BF16)

    return _pallas(
        body, name="ret_fwd", grid=(RET_HEADS // hp, nb),
        in_specs=[proj_spec, cs_spec, cs_spec] + tab_specs,
        out_specs=[hv_spec, hv_spec, pl.BlockSpec((hp, ch, RET_QK, RET_V), lambda h, n: (h, n, 0, 0))],
        out_shape=[jax.ShapeDtypeStruct((s, RET_HEADS * RET_V), BF16), jax.ShapeDtypeStruct((s, RET_HEADS * RET_V), BF16),
                   jax.ShapeDtypeStruct((RET_HEADS, s // RET_CHUNK, RET_QK, RET_V), BF16)],
        scratch=[pltpu.VMEM((hp, RET_QK, RET_V), F32)], semantics=("parallel", "arbitrary"),
        args=(proj_ret, cos, sin, *_ret_tables()), carry=carry)


def _ret_bwd(proj_ret, cos, sin, y, d_yr, rs, s, carry=None):
    rb = min(512, s)
    ch = rb // RET_CHUNK
    nb = s // rb
    hp = _RET_HEADS_BWD
    proj_spec, cs_spec, hv_spec, tab_specs = _ret_specs(rb, lambda n: nb - 1 - n, hp)

    def body(p_ref, cos_ref, sin_ref, y_ref, dyr_ref, rs_ref, dm_ref, qd_ref, kd_ref, cd_ref, o_ref, dr_acc):
        @pl.when(pl.program_id(1) == 0)
        def _():
            dr_acc[...] = jnp.zeros_like(dr_acc)

        for c, hh in [(c, hh) for c in reversed(range(ch)) for hh in range(hp)]:
            rows = slice(c * RET_CHUNK, (c + 1) * RET_CHUNK)
            pc, hc = hh * 1536, hh * RET_V
            dm, qd, kd, cd = dm_ref[hh], qd_ref[hh], kd_ref[hh], cd_ref[hh]
            cosv, sinv = cos_ref[rows, :], sin_ref[rows, :]
            q = _rope_half(p_ref[rows, pc:pc + 256].astype(F32), cosv, sinv)
            kk = _rope_half(p_ref[rows, pc + 256:pc + 512].astype(F32), cosv, sinv) * (RET_QK ** -0.5)
            v = p_ref[rows, pc + 512:pc + 1024]
            g = p_ref[rows, pc + 1024:pc + 1536].astype(F32)
            yv = y_ref[rows, hc:hc + RET_V].astype(F32)
            dyr = dyr_ref[rows, hc:hc + RET_V].astype(F32)
            sg = _sigmoid(g)
            r = lax.rsqrt(jnp.mean(yv * yv, axis=-1, keepdims=True) + EPS)
            yn = yv * r
            dg = dyr * yn * (sg * (1.0 + g * (1.0 - sg)))
            dyn = dyr * (g * sg)
            dy = (r * (dyn - yn * jnp.mean(dyn * yn, axis=-1, keepdims=True))).astype(BF16)
            qb, kb = q.astype(BF16), kk.astype(BF16)
            rb16 = rs_ref[hh, c]
            drb = dr_acc[hh].astype(BF16)
            sd = _dot(qb, kb, NT) * dm
            ds = (_dot(dy, v, NT) * dm).astype(BF16)
            dq = _dot(ds, kb, NN) + qd * _dot(dy, rb16, NT)
            dk = _dot(ds, qb, TN) + kd * _dot(v, drb, NT)
            dv = _dot(sd.astype(BF16), dy, TN) + _dot((kk * kd).astype(BF16), drb, NN)
            dr_acc[hh] = dr_acc[hh] * cd + _dot((q * qd).astype(BF16), dy, TN)
            o_ref[rows, pc:pc + 256] = _unrope_half(dq, cosv, sinv).astype(BF16)
            o_ref[rows, pc + 256:pc + 512] = (_unrope_half(dk, cosv, sinv) * (RET_QK ** -0.5)).astype(BF16)
            o_ref[rows, pc + 512:pc + 1024] = dv.astype(BF16)
            o_ref[rows, pc + 1024:pc + 1536] = dg.astype(BF16)

    in_specs = [proj_spec, cs_spec, cs_spec, hv_spec, hv_spec,
                pl.BlockSpec((hp, ch, RET_QK, RET_V), lambda h, n: (h, nb - 1 - n, 0, 0))] + tab_specs
    return _pallas(
        body, name="ret_bwd", grid=(RET_HEADS // hp, nb), in_specs=in_specs, out_specs=[proj_spec],
        out_shape=[jax.ShapeDtypeStruct((s, RET_HEADS * 1536), BF16)], scratch=[pltpu.VMEM((hp, RET_QK, RET_V), F32)],
        semantics=("parallel", "arbitrary"), args=(proj_ret, cos, sin, y, d_yr, rs, *_ret_tables()), carry=carry)


def _rope_qk(acc, c, s1, s2):
    outs = []
    for cc in range(8):
        vv = acc[:, cc * 128:(cc + 1) * 128]
        outs.append(vv * c + pltpu.roll(vv, 120, 1) * s1 + pltpu.roll(vv, 8, 1) * s2)
    return jnp.concatenate(outs + [acc[:, 2 * DIL_W:]], axis=1)


def _pair_masks(keys_on_rows=False):
    ri = lax.broadcasted_iota(jnp.int32, (2 * QB, 2 * QB), 1 if keys_on_rows else 0)
    ci = lax.broadcasted_iota(jnp.int32, (2 * QB, 2 * QB), 0 if keys_on_rows else 1)
    e = ci - (ri & (QB - 1))
    lane_lo = lax.broadcasted_iota(jnp.int32, (2 * QB, 128), 1) < 64
    return ci, jnp.logical_and(e >= 0, e <= QB), lane_lo


def _stack_heads(v, lane_lo):
    z = jnp.zeros_like(v)
    return jnp.concatenate([jnp.where(lane_lo, v, z), jnp.where(lane_lo, z, v)], axis=0)


def _dil_fwd(qkv, dil, s, name):
    length = s // dil
    rb = min(512, length)
    nsub = rb // QB
    nbs = length // rb
    sub_per = rb // QB

    def body(q_ref, k_ref, v_ref, kp_ref, vp_ref, o_ref, l_ref):
        first = (pl.program_id(0) % nbs) == 0
        ci, band, lane_lo = _pair_masks()
        lo1 = lane_lo[0:QB]

        for i in range(nsub):
            rows = slice(i * QB, (i + 1) * QB)
            mask = jnp.logical_and(band, ci >= jnp.where(first, QB, 0)) if i == 0 else band
            for j in range(4):
                lanes = slice(j * 128, (j + 1) * 128)
                q2 = _stack_heads(q_ref[rows, lanes], lo1)
                if i == 0:
                    k2 = jnp.concatenate([kp_ref[:, lanes], k_ref[rows, lanes]], axis=0)
                    v2 = jnp.concatenate([vp_ref[:, lanes], v_ref[rows, lanes]], axis=0)
                else:
                    k2, v2 = k_ref[(i - 1) * QB:(i + 1) * QB, lanes], v_ref[(i - 1) * QB:(i + 1) * QB, lanes]
                v2 = _stack_heads(v2, lane_lo)
                sc = jnp.where(mask, _dot(q2, k2, NT) * 0.125, NEG)
                m = jnp.max(sc, axis=1, keepdims=True)
                p = jnp.exp(sc - m)
                den = jnp.sum(p, axis=1, keepdims=True)
                pb = p.astype(BF16)
                o = _dot(jnp.concatenate([pb[0:QB], pb[QB:]], axis=1), v2, NN)
                inv = 1.0 / den
                lse = m + jnp.log(den)
                o_ref[rows, lanes] = o * jnp.where(lo1, inv[0:QB], inv[QB:])
                l_ref[rows, lanes] = jnp.where(lo1, lse[0:QB], lse[QB:])

    prev = lambda n: jnp.maximum(n * sub_per - 1, 0)
    cur = lambda cb: (lambda n: (n, cb))
    return pl.pallas_call(
        body, name=name, grid=(s // rb,),
        in_specs=[pl.BlockSpec((rb, DIL_W), cur(0)), pl.BlockSpec((rb, DIL_W), cur(1)), pl.BlockSpec((rb, DIL_W), cur(2)),
                  pl.BlockSpec((QB, DIL_W), lambda n: (prev(n), 1)), pl.BlockSpec((QB, DIL_W), lambda n: (prev(n), 2))],
        out_specs=[pl.BlockSpec((rb, DIL_W), cur(0)), pl.BlockSpec((rb, DIL_W), cur(0))],
        out_shape=[jax.ShapeDtypeStruct((s, DIL_W), F32), jax.ShapeDtypeStruct((s, DIL_W), F32)],
        compiler_params=_cparams(("parallel",)),
    )(qkv, qkv, qkv, qkv, qkv)


def _dil_bwd(qkv, dya, lse, dlt, tc, ts1, ts2, dil, s, name):
    length = s // dil
    rb = min(512, length)
    nsub = rb // QB
    nbs = length // rb
    last_blk = s // QB - 1

    def body(q_ref, k_ref, v_ref, kp_ref, vp_ref, qn_ref, dy_ref, dyn_ref, l_ref, ln_ref, d_ref, dn_ref,
             c_ref, s1_ref, s2_ref, o_ref, dka, dva):
        nl = pl.program_id(0) % nbs
        first, last = nl == 0, nl == nbs - 1
        ci, band, lane_lo = _pair_masks(keys_on_rows=True)
        lo1 = lane_lo[0:QB]

        def unrope(d, rows):
            return d * c_ref[rows, :] + pltpu.roll(d * s1_ref[rows, :], 8, 1) + pltpu.roll(d * s2_ref[rows, :], 120, 1)

        for qi in range(nsub + 1):
            nxt = qi == nsub
            rows = slice((nsub - 1) * QB, nsub * QB) if nxt else slice(qi * QB, (qi + 1) * QB)
            prev_rows = slice((qi - 1) * QB, qi * QB)
            if qi == 0:
                mask = jnp.logical_and(band, ci >= jnp.where(first, QB, 0))
            elif nxt:
                mask = jnp.logical_and(band, ci <= jnp.where(last, -1, QB - 1))[0:QB, :]
            else:
                mask = band
            for j in range(4):
                lanes = slice(j * 128, (j + 1) * 128)
                if nxt:
                    q, do, lv, dl = qn_ref[:, lanes], dyn_ref[:, lanes], ln_ref[:, lanes], dn_ref[:, lanes]
                    k2, v2 = k_ref[prev_rows, lanes], v_ref[prev_rows, lanes]
                else:
                    q, do, lv, dl = q_ref[rows, lanes], dy_ref[rows, lanes], l_ref[rows, lanes], d_ref[rows, lanes]
                    if qi == 0:
                        k2 = jnp.concatenate([kp_ref[:, lanes], k_ref[rows, lanes]], axis=0)
                        v2 = jnp.concatenate([vp_ref[:, lanes], v_ref[rows, lanes]], axis=0)
                    else:
                        k2, v2 = k_ref[(qi - 1) * QB:(qi + 1) * QB, lanes], v_ref[(qi - 1) * QB:(qi + 1) * QB, lanes]
                q2, do2 = _stack_heads(q, lo1), _stack_heads(do, lo1)
                lt, dt = lv.T, dl.T
                lse2 = jnp.concatenate([lt[0:1], lt[64:65]], axis=1)
                dl2 = jnp.concatenate([dt[0:1], dt[64:65]], axis=1)
                sc = _dot(k2, q2, NT) * 0.125
                p = jnp.where(mask, jnp.exp(jnp.minimum(sc - lse2, 0.0)), 0.0)
                ds = (p * (_dot(v2, do2, NT) - dl2) * 0.125).astype(BF16)
                dk2 = _dot(ds, q2, NN)
                dv2 = _dot(p.astype(BF16), do2, NN)
                if qi >= 1:
                    dka[prev_rows, lanes] += dk2[0:QB]
                    dva[prev_rows, lanes] += dv2[0:QB]
                if not nxt:
                    dka[rows, lanes] = dk2[QB:]
                    dva[rows, lanes] = dv2[QB:]
                    dq = _dot(jnp.concatenate([ds[:, 0:QB], ds[:, QB:]], axis=0), _stack_heads(k2, lane_lo), TN)
                    o_ref[rows, lanes] = unrope(dq, rows).astype(BF16)

        for cc in range(4):
            lanes = slice(cc * 128, (cc + 1) * 128)
            o_ref[:, 512 + cc * 128:512 + (cc + 1) * 128] = unrope(dka[:, lanes], slice(None)).astype(BF16)
            o_ref[:, 1024 + cc * 128:1024 + (cc + 1) * 128] = dva[:, lanes].astype(BF16)

    prev = lambda n: jnp.maximum(n * nsub - 1, 0)
    nxt = lambda n: jnp.minimum(n * nsub + nsub, last_blk)
    cur = lambda cb: (lambda n: (n, cb))
    big = lambda cb: pl.BlockSpec((rb, DIL_W), cur(cb))
    small = lambda im: pl.BlockSpec((QB, DIL_W), im)
    tab = pl.BlockSpec((rb, 128), cur(0))
    return pl.pallas_call(
        body, name=name, grid=(s // rb,),
        in_specs=[big(0), big(1), big(2), small(lambda n: (prev(n), 1)), small(lambda n: (prev(n), 2)),
                  small(lambda n: (nxt(n), 0)), big(0), small(lambda n: (nxt(n), 0)), big(0), small(lambda n: (nxt(n), 0)),
                  big(0), small(lambda n: (nxt(n), 0)), tab, tab, tab],
        out_specs=pl.BlockSpec((rb, 3 * DIL_W), cur(0)),
        out_shape=jax.ShapeDtypeStruct((s, 3 * DIL_W), BF16),
        scratch_shapes=[pltpu.VMEM((rb, DIL_W), F32), pltpu.VMEM((rb, DIL_W), F32)],
        compiler_params=_cparams(("parallel",)),
    )(qkv, qkv, qkv, qkv, qkv, qkv, dya, dya, lse, lse, dlt, dlt, tc, ts1, ts2)


def _stream_specs(tr):
    nat = pl.BlockSpec((tr, 128), lambda i, j: (i, j))
    return [nat] + [pl.BlockSpec((dil, tr // dil, 128), lambda i, j: (0, i, j)) for dil in DIL_GROUPS[1:]]


def _dil_merge(o_g, l_g, s):
    tr = min(2048, s)
    nat, sp4, sp16 = _stream_specs(tr)

    def body(o0_ref, l0_ref, o1_ref, l1_ref, o2_ref, l2_ref, ya_ref, lse_ref, o1n, l1n, o2n, l2n):
        for src, dst, dil in ((o1_ref, o1n, 4), (l1_ref, l1n, 4), (o2_ref, o2n, 16), (l2_ref, l2n, 16)):
            for c in range(dil):
                dst[pl.ds(c, tr // dil, stride=dil), :] = src[c]
        l0, l1, l2 = l0_ref[...], l1n[...], l2n[...]
        m = jnp.maximum(jnp.maximum(l0, l1), l2)
        e0, e1, e2 = jnp.exp(l0 - m), jnp.exp(l1 - m), jnp.exp(l2 - m)
        den = e0 + e1 + e2
        ya_ref[...] = ((e0 * o0_ref[...] + e1 * o1n[...] + e2 * o2n[...]) / den).astype(BF16)
        lse_ref[...] = m + jnp.log(den)

    v3 = lambda a, dil: a.reshape(dil, s // dil, DIL_W)
    return pl.pallas_call(
        body, name="dil_merge", grid=(s // tr, 4),
        in_specs=[nat, nat, sp4, sp4, sp16, sp16], out_specs=[nat, nat],
        out_shape=[jax.ShapeDtypeStruct((s, DIL_W), BF16), jax.ShapeDtypeStruct((s, DIL_W), F32)],
        scratch_shapes=[pltpu.VMEM((tr, 128), F32)] * 4,
        compiler_params=_cparams(("parallel", "parallel")),
    )(o_g[0], l_g[0], v3(o_g[1], 4), v3(l_g[1], 4), v3(o_g[2], 16), v3(l_g[2], 16))


def _dil_bwd_prep(d_ya, ya, lse, s):
    tr = min(2048, s)
    nat, sp4, sp16 = _stream_specs(tr)

    def body(dya_ref, ya_ref, lse_ref, dy0, dl0, dy1, ls1, dl1, dy2, ls2, dl2, dlt):
        lane_lo = lax.broadcasted_iota(jnp.int32, (tr, 128), 1) < 64
        prod = dya_ref[...] * ya_ref[...].astype(F32)
        lo = jnp.where(lane_lo, prod, 0.0)
        dlt[...] = jnp.where(lane_lo, jnp.sum(lo, axis=1, keepdims=True), jnp.sum(prod - lo, axis=1, keepdims=True))
        dy0[...] = dya_ref[...].astype(BF16)
        dl0[...] = dlt[...]
        for dil, dy, ls, dl in ((4, dy1, ls1, dl1), (16, dy2, ls2, dl2)):
            for c in range(dil):
                rows = pl.ds(c, tr // dil, stride=dil)
                dy[c] = dya_ref[rows, :].astype(BF16)
                ls[c] = lse_ref[rows, :]
                dl[c] = dlt[rows, :]

    sh = lambda dil, dt: jax.ShapeDtypeStruct((dil, s // dil, DIL_W), dt)
    res = pl.pallas_call(
        body, name="dil_bwd_prep", grid=(s // tr, 4),
        in_specs=[nat, nat, nat], out_specs=[nat, nat, sp4, sp4, sp4, sp16, sp16, sp16],
        out_shape=[jax.ShapeDtypeStruct((s, DIL_W), BF16), jax.ShapeDtypeStruct((s, DIL_W), F32),
                   sh(4, BF16), sh(4, F32), sh(4, F32), sh(16, BF16), sh(16, F32), sh(16, F32)],
        scratch_shapes=[pltpu.VMEM((tr, 128), F32)],
        compiler_params=_cparams(("parallel", "parallel")),
    )(d_ya, ya, lse)
    dy0, dl0, dy1, ls1, dl1, dy2, ls2, dl2 = [r.reshape(s, DIL_W) for r in res]
    return [(dy0, lse, dl0), (dy1, ls1, dl1), (dy2, ls2, dl2)]


_RET_SEGS = ((0, 256), (1024, 256), (2048, 512), (4096, 512))


def _split_w_in(win):
    per_head = [win[a:a + RET_HEADS * n].reshape(RET_HEADS, n, D_MODEL) for a, n in _RET_SEGS]
    w_ret = jnp.concatenate(per_head, axis=1).reshape(RET_HEADS * 1536, D_MODEL)
    w_dil = [jnp.concatenate([win[a + DIL_W * g:a + DIL_W * (g + 1)] for a in (6144, 7680, 9216)], axis=0) for g in range(3)]
    return w_ret, win[10752:12800], w_dil


def _join_w_in(g_ret, g_gate, g_dil):
    g_ret = g_ret.reshape(RET_HEADS, 1536, D_MODEL)
    off = (0, 256, 512, 1024, 1536)
    parts = [g_ret[:, off[i]:off[i + 1]].reshape(-1, D_MODEL) for i in range(4)]
    dil = [g_dil[g][DIL_W * i:DIL_W * (i + 1)] for i in range(3) for g in range(3)]
    return jnp.concatenate(parts + dil + [g_gate], axis=0)


def _local_step(xs, pb, tgt, tabs, wts, vec, s, shards=None):
    tm = min(2048, s)
    tr = min(512, s)
    mm = functools.partial(_matmul, tm=tm)
    on_mesh = shards is not None
    wts, vec = dict(wts), dict(vec)
    blocks = lambda g: g.reshape(N_DEV, g.shape[0] // N_DEV, g.shape[1])

    late_shards = dict(shards) if on_mesh else {}
    first = _TwoLevelGather([late_shards.pop("w_in"), late_shards.pop("b_gate")]) if on_mesh else None
    u, gathered = _prenorm(xs, vec["g_pre_mix"], s, carry=first)
    if on_mesh:
        wts["w_in"] = gathered[0].reshape(N_DEV * gathered[0].shape[1], D_MODEL)
        bias = gathered[1].transpose(1, 0, 2).reshape(2, D_MODEL)
        vec.update(b0=bias[0:1], b1=bias[1:2])
    w_ret, w_gate, w_dil = _split_w_in(wts["w_in"])
    proj_ret = mm(u[0], w_ret, mode="nt", m=s, n=6144, k=1024, tn=1024, tk=1024, out_dtype=BF16, name="inproj_ret")
    proj_gate = mm(u[0], w_gate, mode="nt", m=s, n=2048, k=1024, tn=1024, tk=1024, out_dtype=BF16, name="inproj_gate")
    qkv = [_matmul(u[g], w_dil[g], mode="nt", m=s, n=1536, k=1024, tm=min(1024, s), tn=1536, tk=1024, out_dtype=BF16,
                   name="inproj_dil%d" % g, epi=tabs["dil"][g], epi_width=128, epi_fn=_rope_qk) for g in range(3)]

    names = list(late_shards) if on_mesh else []
    gather = _Exchange([late_shards[n] for n in names], [False] * len(names)) if on_mesh else None
    (yr, y_ret, rstate), gathered = _ret_fwd(proj_ret, tabs["cos_r"], tabs["sin_r"], s, carry=gather)
    wts.update({n: g.reshape(N_DEV * g.shape[1], g.shape[2]) for n, g in zip(names, gathered)})
    a_br = mm(yr, wts["w_ret_out"], mode="nn", m=s, n=1024, k=2048, tn=1024, tk=2048, out_dtype=BF16, name="ret_out")

    o_g, l_g = [], []
    for g, dil in enumerate(DIL_GROUPS):
        o, l = _dil_fwd(qkv[g], dil, s, "dil_fwd%d" % g)
        o_g.append(o)
        l_g.append(l)
    ya, lse = _dil_merge(o_g, l_g, s)
    b_br = mm(ya, wts["w_dil_out"], mode="nt", m=s, n=1024, k=512, tn=1024, tk=512, out_dtype=BF16, name="dil_out")

    def gate_mix(a, b, gr, ga, b0, b1):
        return [_sigmoid(gr.astype(F32) + b0) * a.astype(F32) + _sigmoid(ga.astype(F32) + b1) * b.astype(F32)], []

    (mixed,), _ = _rowwise("gate_mix", gate_mix, s, tr, [(a_br, 1024, 0), (b_br, 1024, 0), (proj_gate, 1024, 0), (proj_gate, 1024, 1)],
                           [vec["b0"], vec["b1"]], [(1024, BF16)])
    z = mm(mixed, wts["w_o"], mode="nn", m=s, n=1024, k=1024, tn=1024, tk=1024, out_dtype=BF16, name="w_o")

    def post_norm(h, f, g_post, g_pre):
        hn = h + _rms(f) * g_post
        return [hn, _rms(hn) * g_pre], []

    (h1, v2), _ = _rowwise("post_mix", post_norm, s, tr, [(xs, 1024, 0), (z, 1024, 0)], [vec["g_post_mix"], vec["g_pre_mlp"]],
                           [(1024, F32), (1024, BF16)])
    a_up = mm(v2, wts["w_up"], mode="nt", m=s, n=4096, k=1024, tn=1024, tk=1024, out_dtype=BF16, name="mlp_up")
    f_dn = mm(a_up, wts["w_down"], mode="nn", m=s, n=1024, k=4096, tn=1024, tk=1024, out_dtype=BF16, name="mlp_down", a_fn=_relu_sq)
    (h2, t_ple), _ = _rowwise("post_mlp", post_norm, s, tr, [(h1, 1024, 0), (f_dn, 1024, 0)], [vec["g_post_mlp"], vec["g_pre_ple"]],
                              [(1024, F32), (1024, BF16)])
    gl = mm(t_ple, wts["w_ple_gate"], mode="nn", m=s, n=1024, k=1024, tn=1024, tk=1024, out_dtype=BF16, name="ple_gate")
    e_ple = mm(pb, wts["w_ple_in"], mode="nt", m=s, n=1024, k=256, tn=1024, tk=256, out_dtype=BF16, name="ple_in")

    def ple_loss(h, glv, e, tg, b, g):
        gate = _sigmoid(glv + b)
        ge = gate * e
        diff = h + _rms(ge) * g - tg
        dy = diff * (1.0 / D_MODEL)
        d_ge, dg = _rms_bwd(ge, g, dy)
        d_gl = d_ge * e * gate * (1.0 - gate)
        loss = jnp.zeros((1, D_MODEL), F32) + 0.5 * jnp.sum(diff * diff) * (1.0 / D_MODEL)
        return [dy, d_gl, d_ge * gate], [_colsum(dg), _colsum(d_gl), loss]

    (dy, d_gl, d_e), (dg_post_ple, db_ple, loss) = _rowwise(
        "ple_loss", ple_loss, s, tr, [(h2, 1024, 0), (gl, 1024, 0), (e_ple, 1024, 0), (tgt, 1024, 0)],
        [vec["b_ple"], vec["g_post_ple"]], [(1024, F32), (1024, BF16), (1024, BF16)], [1024, 1024, 1024])

    ts, ts2 = min(1024, s), min(2048, s)
    wg = functools.partial(_matmul, mode="tn", k=s, tk=ts, out_dtype=BF16)
    grads = {}
    grads["w_ple_in"] = wg(d_e, pb, m=1024, n=256, tm=1024, tn=256, name="g_ple_in")
    grads["w_ple_gate"] = wg(t_ple, d_gl, m=1024, n=1024, tm=1024, tn=1024, name="g_ple_gate")
    d_t = mm(d_gl, wts["w_ple_gate"], mode="nt", m=s, n=1024, k=1024, tn=1024, tk=1024, out_dtype=BF16, name="d_t")

    def bwd_ple_mlp(h, dt, dyv, f, g_pre, g_post):
        dx, dg1 = _rms_bwd(h, g_pre, dt)
        dh = dyv + dx
        df, dg2 = _rms_bwd(f, g_post, dh)
        return [dh, df], [_colsum(dg1), _colsum(dg2)]

    (d_h2, d_f), (dg_pre_ple, dg_post_mlp) = _rowwise(
        "bwd_ple_mlp", bwd_ple_mlp, s, tr, [(h2, 1024, 0), (d_t, 1024, 0), (dy, 1024, 0), (f_dn, 1024, 0)],
        [vec["g_pre_ple"], vec["g_post_mlp"]], [(1024, F32), (1024, BF16)], [1024, 1024])
    d_a = mm(d_f, wts["w_down"], mode="nt", m=s, n=4096, k=1024, tn=1024, tk=1024, out_dtype=BF16, name="d_a",
             epi=(a_up,), epi_fn=lambda acc, av: acc * (2.0 * jnp.maximum(av.astype(F32), 0.0)))
    grads["w_down"] = wg(a_up, d_f, m=4096, n=1024, tm=2048, tn=1024, name="g_down", a_fn=_relu_sq)
    grads["w_up"] = wg(d_a, v2, m=4096, n=1024, tm=2048, tn=1024, tk=ts2, name="g_up")
    d_v2 = mm(d_a, wts["w_up"], mode="nn", m=s, n=1024, k=4096, tn=1024, tk=2048, out_dtype=BF16, name="d_v2")

    (d_h1, d_z), (dg_pre_mlp, dg_post_mix) = _rowwise(
        "bwd_mlp_mix", bwd_ple_mlp, s, tr, [(h1, 1024, 0), (d_v2, 1024, 0), (d_h2, 1024, 0), (z, 1024, 0)],
        [vec["g_pre_mlp"], vec["g_post_mix"]], [(1024, F32), (1024, BF16)], [1024, 1024])
    d_mixed = mm(d_z, wts["w_o"], mode="nt", m=s, n=1024, k=1024, tn=1024, tk=1024, out_dtype=BF16, name="d_mixed")
    grads["w_o"] = wg(mixed, d_z, m=1024, n=1024, tm=1024, tn=1024, name="g_o")

    def bwd_gate(dm, a, b, gr, ga, b0, b1):
        sa, sb = _sigmoid(gr.astype(F32) + b0), _sigmoid(ga.astype(F32) + b1)
        dgr = dm * a.astype(F32) * sa * (1.0 - sa)
        dga = dm * b.astype(F32) * sb * (1.0 - sb)
        return [dm * sa, dm * sb, jnp.concatenate([dgr, dga], axis=1)], [_colsum(dgr), _colsum(dga)]

    (d_abr, d_bbr, dproj_gate), (db0, db1) = _rowwise(
        "bwd_gate", bwd_gate, s, tr, [(d_mixed, 1024, 0), (a_br, 1024, 0), (b_br, 1024, 0), (proj_gate, 1024, 0), (proj_gate, 1024, 1)],
        [vec["b0"], vec["b1"]], [(1024, BF16), (1024, BF16), (2048, BF16)], [1024, 1024])
    grads["w_ret_out"] = wg(yr, d_abr, m=2048, n=1024, tm=2048, tn=1024, tk=ts2, name="g_ret_out")
    d_yr = mm(d_abr, wts["w_ret_out"], mode="nt", m=s, n=2048, k=1024, tn=1024, tk=1024, out_dtype=BF16, name="d_yr")
    grads["w_dil_out"] = wg(d_bbr, ya, m=1024, n=512, tm=1024, tn=512, name="g_dil_out")
    d_ya = mm(d_bbr, wts["w_dil_out"], mode="nn", m=s, n=512, k=1024, tn=512, tk=1024, out_dtype=F32, name="d_ya")

    slots = {}
    names = list(grads) if on_mesh else []
    shares = _Exchange([blocks(grads[n]) for n in names], [True] * len(names)) if on_mesh else None
    (dproj_ret,), got = _ret_bwd(proj_ret, tabs["cos_r"], tabs["sin_r"], y_ret, d_yr, rstate, s, carry=shares)
    slots.update(zip(names, got))
    upstream = _dil_bwd_prep(d_ya, ya, lse, s)
    dqkv = [_dil_bwd(qkv[g], *upstream[g], *tabs["dil"][g], dil, s, "dil_bwd%d" % g)
            for g, dil in enumerate(DIL_GROUPS)]

    g_ret = wg(dproj_ret, u[0], m=6144, n=1024, tm=2048, tn=1024, tk=ts2, name="g_in_ret")
    g_gate = wg(dproj_gate, u[0], m=2048, n=1024, tm=2048, tn=1024, tk=ts2, name="g_in_gate")
    g_dil = [wg(dqkv[g], u[g], m=1536, n=1024, tm=1536, tn=1024, name="g_in_dil%d" % g) for g in range(3)]
    grads["w_in"] = _join_w_in(g_ret, g_gate, g_dil)

    du_ret = functools.partial(mm, dproj_ret, w_ret, mode="nn", m=s, n=1024, k=6144, tn=1024, tk=2048, out_dtype=BF16, name="du_ret")
    if on_mesh:
        du_ret, (slots["w_in"],) = du_ret(carry=_Exchange([blocks(grads["w_in"])], [True]))
    else:
        du_ret = du_ret()
    du_gate = mm(dproj_gate, w_gate, mode="nn", m=s, n=1024, k=2048, tn=1024, tk=2048, out_dtype=BF16, name="du_gate")
    du_dil = [mm(dqkv[g], w_dil[g], mode="nn", m=s, n=1024, k=1536, tn=1024, tk=1536, out_dtype=BF16, name="du_dil%d" % g)
              for g in range(3)]

    grad_x, dg_pre_mix = _grad_x(xs, d_h1, (du_ret, du_gate, du_dil[0]), du_dil[1], du_dil[2], vec["g_pre_mix"], s)

    zero = jnp.zeros((1, D_MODEL), F32)
    packet = jnp.concatenate([dg_pre_mix, dg_post_mix, dg_pre_mlp, dg_post_mlp, dg_pre_ple, db_ple, dg_post_ple, loss,
                              db0, db1] + [zero] * 6, axis=0)
    return grad_x, (slots if on_mesh else grads), packet


def _mesh_pos():
    return lax.axis_index("x"), lax.axis_index("y"), lax.axis_index("c")


class _Exchange:
    def __init__(self, arrays, scatter):
        self.arrays, self.scatter, self.n = list(arrays), list(scatter), len(arrays)
        self.out_shape = [jax.ShapeDtypeStruct(a.shape if sc else (N_DEV,) + a.shape, a.dtype)
                          for a, sc in zip(self.arrays, self.scatter)]
        self.scratch = [pltpu.SemaphoreType.DMA((self.n * 7,)), pltpu.SemaphoreType.DMA((self.n * 7,)),
                        pltpu.SemaphoreType.DMA((self.n,))]
        self.specs = [pl.BlockSpec(memory_space=pl.ANY)] * self.n

    def _copies(self, srcs, dsts, sems):
        send_sems, recv_sems, local_sems = sems
        x, y, c = _mesh_pos()
        my = 4 * x + 2 * y + c
        src_of = lambda w, idx: srcs[w].at[idx] if self.scatter[w] else srcs[w]
        local = [pltpu.make_async_copy(src_of(w, my), dsts[w].at[my], local_sems.at[w]) for w in range(self.n)]
        sends, recvs = [], []
        for w in range(self.n):
            for r in range(1, N_DEV):
                px = 1 - x if r & 4 else x
                py = 1 - y if r & 2 else y
                pc = 1 - c if r & 1 else c
                pidx = 4 * px + 2 * py + pc
                kw = dict(send_sem=send_sems.at[w * 7 + r - 1], recv_sem=recv_sems.at[w * 7 + r - 1],
                          device_id=(px, py, pc), device_id_type=MESH)
                sends.append(pltpu.make_async_remote_copy(src_ref=src_of(w, pidx), dst_ref=dsts[w].at[my], **kw))
                recvs.append(pltpu.make_async_remote_copy(src_ref=src_of(w, pidx), dst_ref=dsts[w].at[pidx], **kw))
        return local, sends, recvs

    def start(self, srcs, dsts, sems):
        local, sends, _ = self._copies(srcs, dsts, sems)
        for cp in local + sends:
            cp.start()

    def wait(self, srcs, dsts, sems):
        local, sends, recvs = self._copies(srcs, dsts, sems)
        for cp in recvs:
            cp.wait_recv()
        for cp in sends:
            cp.wait_send()
        for cp in local:
            cp.wait()

    def split(self, refs, n_in, n_out):
        srcs = refs[n_in:n_in + self.n]
        dsts = refs[n_in + self.n + n_out:n_in + 2 * self.n + n_out]
        return srcs, dsts, refs[len(refs) - 3:]


class _TwoLevelGather(_Exchange):
    def __init__(self, arrays):
        super().__init__(arrays, [False] * len(arrays))

    def _plan(self, srcs, dsts, sems):
        send_sems, recv_sems, local_sems = sems
        x, y, c = _mesh_pos()
        me, sibling = (x, y, c), (x, y, 1 - c)
        chips = [(1 - x, y), (x, 1 - y), (1 - x, 1 - y)]
        region = lambda w, dev: dsts[w].at[4 * dev[0] + 2 * dev[1] + dev[2]]

        def copy(w, kk, block, to, src=None):
            return pltpu.make_async_remote_copy(
                src_ref=region(w, block) if src is None else src, dst_ref=region(w, block),
                send_sem=send_sems.at[w * 7 + kk], recv_sem=recv_sems.at[w * 7 + kk], device_id=to, device_id_type=MESH)

        mine = [pltpu.make_async_copy(srcs[w], region(w, me), local_sems.at[w]) for w in range(self.n)]
        first = []
        for w in range(self.n):
            first.append(copy(w, 0, me, sibling, src=srcs[w]))
            first += [copy(w, 1 + j, me, (*chip, c), src=srcs[w]) for j, chip in enumerate(chips)]
        return me, sibling, chips, c, copy, mine, first

    def start(self, srcs, dsts, sems):
        *_, mine, first = self._plan(srcs, dsts, sems)
        for cp in mine + first:
            cp.start()

    def wait(self, srcs, dsts, sems):
        me, sibling, chips, c, copy, mine, first = self._plan(srcs, dsts, sems)
        passed = []
        for j, chip in enumerate(chips):
            for w in range(self.n):
                copy(w, 1 + j, (*chip, c), me).wait_recv()
                cp = copy(w, 4 + j, (*chip, c), sibling)
                cp.start()
                passed.append(cp)
        for w in range(self.n):
            copy(w, 0, sibling, me).wait_recv()
            for j, chip in enumerate(chips):
                copy(w, 4 + j, (*chip, 1 - c), me).wait_recv()
        for cp in first + passed:
            cp.wait_send()
        for cp in mine:
            cp.wait()


def _run_exchange(ex, name):
    def body(*refs):
        parts = ex.split(refs, 0, 0)
        ex.start(*parts)
        ex.wait(*parts)

    return pl.pallas_call(body, name=name, in_specs=ex.specs, out_specs=ex.specs, out_shape=ex.out_shape,
                          scratch_shapes=ex.scratch)(*ex.arrays)


def _pick_rows(r, c, target_bytes):
    t = r
    while (t // 2) % 16 == 0 and t // 2 >= 16 and t * c * 4 > target_bytes:
        t //= 2
    return t


def _sum_slots(slots, name):
    ns, r, c = slots.shape
    tr = _pick_rows(r, c, 256 * 1024)

    def body(s_ref, o_ref):
        acc = s_ref[0].astype(F32)
        for kk in range(1, ns):
            acc = acc + s_ref[kk].astype(F32)
        o_ref[...] = acc

    return pl.pallas_call(
        body, name=name, grid=(r // tr,),
        in_specs=[pl.BlockSpec((ns, tr, c), lambda i: (0, i, 0))], out_specs=pl.BlockSpec((tr, c), lambda i: (i, 0)),
        out_shape=jax.ShapeDtypeStruct((r, c), F32), compiler_params=_cparams(("parallel",)),
    )(slots)


def _adamw(slots, w, m, v, name):
    ns, r, c = slots.shape
    tr = _pick_rows(r, c, 256 * 1024)

    def body(s_ref, w_ref, m_ref, v_ref, g_out, d_out, m_out, v_out):
        g = s_ref[0].astype(F32)
        for kk in range(1, ns):
            g = g + s_ref[kk].astype(F32)
        mn = ADAM_B1 * m_ref[...] + (1.0 - ADAM_B1) * g
        vn = ADAM_B2 * v_ref[...] + (1.0 - ADAM_B2) * (g * g)
        m_hat = mn / (1.0 - ADAM_B1 ** ADAM_STEP)
        v_hat = vn / (1.0 - ADAM_B2 ** ADAM_STEP)
        g_out[...] = g
        d_out[...] = -ADAM_LR * (m_hat / (jnp.sqrt(v_hat) + ADAM_EPS) + ADAM_WD * w_ref[...])
        m_out[...] = mn
        v_out[...] = vn

    blk = pl.BlockSpec((tr, c), lambda i: (i, 0))
    return pl.pallas_call(
        body, name=name, grid=(r // tr,),
        in_specs=[pl.BlockSpec((ns, tr, c), lambda i: (0, i, 0)), blk, blk, blk], out_specs=[blk] * 4,
        out_shape=[jax.ShapeDtypeStruct((r, c), F32)] * 4, compiler_params=_cparams(("parallel",)),
    )(slots, w, m, v)


def _rotary_tables(pos, s):
    posf = pos.astype(F32)
    inv_freq = 1.0 / (10000.0 ** jnp.linspace(0.0, 1.0, RET_QK // 2, dtype=F32))
    ang = posf[:, None] * inv_freq
    tabs = {"cos_r": jnp.cos(ang), "sin_r": jnp.sin(ang), "dil": []}
    freqs = 500000.0 ** (-jnp.arange(0, 16, 2, dtype=F32) / 16)
    spread = np.zeros((16, 384), np.float32)
    bias = np.zeros((1, 384), np.float32)
    for head in range(2):
        for i in range(8):
            spread[i, 64 * head + i] = spread[i, 64 * head + 8 + i] = 1.0
            spread[8 + i, 128 + 64 * head + i] = -1.0
            spread[8 + i, 256 + 64 * head + 8 + i] = 1.0
        bias[0, 64 * head + 16:64 * head + 64] = 1.0

    def expand(t, e, b):
        hi = t.astype(BF16)
        lo = (t - hi.astype(F32)).astype(BF16)
        out = _dot(hi, e, NN) + _dot(lo, e, NN) + b
        return [out[:, 0:128], out[:, 128:256], out[:, 256:384]], []

    for g, dil in enumerate(DIL_GROUPS):
        ang = posf.reshape(s // dil, dil).T.reshape(s, 1) * freqs
        cs = jnp.concatenate([jnp.cos(ang), jnp.sin(ang)], axis=1)
        t3, _ = _rowwise("rot_tables%d" % g, expand, s, min(1024, s), [(cs, 16, 0)],
                         [jnp.asarray(spread, BF16), jnp.asarray(bias)], [(128, F32)] * 3)
        tabs["dil"].append(tuple(t3))
    return tabs


_TRANSPOSED = ("w_in", "w_dil_out", "w_up", "w_ple_in")
_MATS = ("w_in", "w_ret_out", "w_dil_out", "w_o", "w_up", "w_down", "w_ple_gate", "w_ple_in")
_VECS = ("g_pre_mix", "g_post_mix", "g_pre_mlp", "g_post_mlp", "g_pre_ple", "b_ple_gate", "g_post_ple")
_ORDER = ("w_in", "b_gate", "w_ret_out", "w_dil_out", "w_o", "g_pre_mix", "g_post_mix", "g_pre_mlp", "g_post_mlp", "w_up",
          "w_down", "g_pre_ple", "w_ple_gate", "b_ple_gate", "w_ple_in", "g_post_ple")


def kernel(x, p, positions, w_in, b_gate, w_ret_out, w_dil_out, w_o, g_pre_mix, g_post_mix, g_pre_mlp, g_post_mlp, w_up, w_down, g_pre_ple, w_ple_gate, b_ple_gate, w_ple_in, g_post_ple, loss_target, m_w_in, m_b_gate, m_w_ret_out, m_w_dil_out, m_w_o, m_g_pre_mix, m_g_post_mix, m_g_pre_mlp, m_g_post_mlp, m_w_up, m_w_down, m_g_pre_ple, m_w_ple_gate, m_b_ple_gate, m_w_ple_in, m_g_post_ple, v_w_in, v_b_gate, v_w_ret_out, v_w_dil_out, v_w_o, v_g_pre_mix, v_g_post_mix, v_g_pre_mlp, v_g_post_mlp, v_w_up, v_w_down, v_g_pre_ple, v_w_ple_gate, v_b_ple_gate, v_w_ple_in, v_g_post_ple):
    s = x.shape[1]
    wd = dict(w_in=w_in, b_gate=b_gate, w_ret_out=w_ret_out, w_dil_out=w_dil_out, w_o=w_o, g_pre_mix=g_pre_mix,
              g_post_mix=g_post_mix, g_pre_mlp=g_pre_mlp, g_post_mlp=g_post_mlp, w_up=w_up, w_down=w_down,
              g_pre_ple=g_pre_ple, w_ple_gate=w_ple_gate, b_ple_gate=b_ple_gate, w_ple_in=w_ple_in, g_post_ple=g_post_ple)
    md = dict(w_in=m_w_in, b_gate=m_b_gate, w_ret_out=m_w_ret_out, w_dil_out=m_w_dil_out, w_o=m_w_o, g_pre_mix=m_g_pre_mix,
              g_post_mix=m_g_post_mix, g_pre_mlp=m_g_pre_mlp, g_post_mlp=m_g_post_mlp, w_up=m_w_up, w_down=m_w_down,
              g_pre_ple=m_g_pre_ple, w_ple_gate=m_w_ple_gate, b_ple_gate=m_b_ple_gate, w_ple_in=m_w_ple_in, g_post_ple=m_g_post_ple)
    vd = dict(w_in=v_w_in, b_gate=v_b_gate, w_ret_out=v_w_ret_out, w_dil_out=v_w_dil_out, w_o=v_w_o, g_pre_mix=v_g_pre_mix,
              g_post_mix=v_g_post_mix, g_pre_mlp=v_g_pre_mlp, g_post_mlp=v_g_post_mlp, w_up=v_w_up, w_down=v_w_down,
              g_pre_ple=v_g_pre_ple, w_ple_gate=v_w_ple_gate, b_ple_gate=v_b_ple_gate, w_ple_in=v_w_ple_in, g_post_ple=v_g_post_ple)

    shards = {n: (wd[n][0].T if n in _TRANSPOSED else wd[n][0]).astype(BF16) for n in _MATS}
    shards["b_gate"] = b_gate[0]
    vec = {n: wd[n] for n in _VECS}
    vec["b_ple"] = b_ple_gate

    tabs = _rotary_tables(positions[0], s)
    grad_x, slots, packet = _local_step(x[0], p[0, 0].astype(BF16), loss_target[0], tabs, {}, vec, s, shards=shards)

    (packets,) = _run_exchange(_Exchange([packet], [False]), "exchange_vectors")
    out = {}
    for n in _MATS:
        sl = slots[n]
        if n in _TRANSPOSED:
            sl = _sum_slots(sl, "sum_" + n).T[None]
        out[n] = _adamw(sl, wd[n][0], md[n][0], vd[n][0], "adamw_" + n)
    zero_rows = jnp.zeros((16 - len(_VECS), D_MODEL), F32)
    pack = lambda d: jnp.concatenate([d[n] for n in _VECS] + [zero_rows], axis=0)
    small = _adamw(packets, pack(wd), pack(md), pack(vd), "adamw_vectors")
    for i, n in enumerate(_VECS):
        out[n] = tuple(t[i:i + 1] for t in small)
    my = 4 * lax.axis_index("x") + 2 * lax.axis_index("y") + lax.axis_index("c")
    g_bias = lax.dynamic_slice(small[0], (8, my * 128), (2, 128))
    out["b_gate"] = _adamw(g_bias[None], b_gate[0], m_b_gate[0], v_b_gate[0], "adamw_b_gate")
    loss = small[0][7, 0]

    res = [loss, grad_x[None]]
    for kk in range(4):
        res += [out[n][kk][None] if out[n][kk].ndim == 2 and wd[n].ndim == 3 else out[n][kk] for n in _ORDER]
    return tuple(res)
```

```python
import functools
import math

import numpy as np
import jax
import jax.numpy as jnp
from jax import lax
from jax.experimental import pallas as pl
from jax.experimental.pallas import tpu as pltpu

F32, BF16 = jnp.float32, jnp.bfloat16
D_MODEL = 1024
EPS = 1e-6
N_DEV = 8
RET_HEADS, RET_QK, RET_V, RET_CHUNK = 4, 256, 512, 128
DIL_GROUPS = (1, 4, 16)
DIL_W = 512
QB = 128
NEG = -1e30
ADAM_LR, ADAM_B1, ADAM_B2, ADAM_EPS, ADAM_WD, ADAM_STEP = 0.001, 0.9, 0.999, 1e-08, 0.01, 10
VMEM_LIMIT_BYTES = 56 * 1024 * 1024
MESH = pl.DeviceIdType.MESH

NN = ((1,), (0,))
NT = ((1,), (1,))
TN = ((0,), (0,))


def _dot(a, b, dn):
    return lax.dot_general(a, b, (dn, ((), ())), preferred_element_type=F32)


def _cparams(sem):
    return pltpu.CompilerParams(dimension_semantics=sem, vmem_limit_bytes=VMEM_LIMIT_BYTES)


def _rms(x):
    return x * lax.rsqrt(jnp.mean(x * x, axis=-1, keepdims=True) + EPS)


def _rms_bwd(x, g, dy):
    r = lax.rsqrt(jnp.mean(x * x, axis=-1, keepdims=True) + EPS)
    xh = x * r
    t = dy * g
    dx = r * (t - xh * jnp.mean(t * xh, axis=-1, keepdims=True))
    return dx, dy * xh


def _colsum(v):
    return jnp.sum(v, axis=0, keepdims=True)


def _sigmoid(v):
    return 1.0 / (1.0 + jnp.exp(-v))


def _pallas(compute, *, name, grid, in_specs, out_specs, out_shape, scratch, semantics, args, carry=None):
    n_in, n_out = len(in_specs), len(out_specs)
    if carry is None:
        res = pl.pallas_call(compute, name=name, grid=grid, in_specs=in_specs, out_specs=out_specs, out_shape=out_shape,
                             scratch_shapes=scratch, compiler_params=_cparams(semantics))(*args)
        return res, []
    n_steps = math.prod(grid)

    def body(*refs):
        step = 0
        for axis, size in enumerate(grid):
            step = step * size + pl.program_id(axis)
        parts = carry.split(refs, n_in, n_out)
        pl.when(step == 0)(lambda: carry.start(*parts))
        compute(*refs[:n_in], *refs[n_in + carry.n:n_in + carry.n + n_out], *refs[n_in + 2 * carry.n + n_out:len(refs) - 3])
        pl.when(step == n_steps - 1)(lambda: carry.wait(*parts))

    res = pl.pallas_call(
        body, name=name, grid=grid, in_specs=list(in_specs) + carry.specs, out_specs=list(out_specs) + carry.specs,
        out_shape=list(out_shape) + carry.out_shape, scratch_shapes=list(scratch) + carry.scratch,
        compiler_params=_cparams(("arbitrary",) * len(grid)))(*args, *carry.arrays)
    return res[:n_out], res[n_out:]


def _matmul(a, b, *, mode, m, n, k, tm, tn, tk, out_dtype, name, a_fn=None, epi=(), epi_width=None, epi_fn=None, carry=None):
    nk = k // tk
    grid = (m // tm, n // tn, nk)
    if mode == "nn":
        a_blk, a_im, b_blk, b_im, dn = (tm, tk), (lambda i, j, kk: (i, kk)), (tk, tn), (lambda i, j, kk: (kk, j)), NN
    elif mode == "nt":
        a_blk, a_im, b_blk, b_im, dn = (tm, tk), (lambda i, j, kk: (i, kk)), (tn, tk), (lambda i, j, kk: (j, kk)), NT
    else:
        a_blk, a_im, b_blk, b_im, dn = (tk, tm), (lambda i, j, kk: (kk, i)), (tk, tn), (lambda i, j, kk: (kk, j)), TN
    o_im = lambda i, j, kk: (i, j)
    n_in = 2 + len(epi)

    def body(*refs):
        a_ref, b_ref = refs[0], refs[1]
        o_ref = refs[n_in]
        acc_ref = refs[n_in + 1] if nk > 1 else None

        def finish(acc):
            if epi:
                acc = epi_fn(acc, *[r[...] for r in refs[2:n_in]])
            o_ref[...] = acc.astype(o_ref.dtype)

        av = a_ref[...]
        if a_fn is not None:
            av = a_fn(av)
        part = _dot(av, b_ref[...], dn)
        if nk == 1:
            finish(part)
        else:
            kk = pl.program_id(2)

            @pl.when(kk == 0)
            def _():
                acc_ref[...] = part

            @pl.when(kk > 0)
            def _():
                acc_ref[...] += part

            @pl.when(kk == nk - 1)
            def _():
                finish(acc_ref[...])

    epi_spec = pl.BlockSpec((tm, tn), o_im) if epi_width is None else pl.BlockSpec((tm, epi_width), lambda i, j, kk: (i, 0))
    in_specs = [pl.BlockSpec(a_blk, a_im), pl.BlockSpec(b_blk, b_im)] + [epi_spec] * len(epi)
    args = [a, b, *epi]
    (out,), got = _pallas(
        body, name=name, grid=grid, in_specs=in_specs, out_specs=[pl.BlockSpec((tm, tn), o_im)],
        out_shape=[jax.ShapeDtypeStruct((m, n), out_dtype)], scratch=[pltpu.VMEM((tm, tn), F32)] if nk > 1 else [],
        semantics=("parallel", "parallel", "arbitrary"), args=args, carry=carry)
    return out if carry is None else (out, got)


def _relu_sq(v):
    r = jnp.maximum(v.astype(F32), 0.0)
    return (r * r).astype(BF16)


def _rowwise(name, fn, s, tr, rows, vecs, outs, accs=()):
    n_r, n_v, n_o, n_a = len(rows), len(vecs), len(outs), len(accs)

    def body(*refs):
        vals = [refs[i][...].astype(F32) for i in range(n_r)] + [refs[n_r + i][...] for i in range(n_v)]
        o_refs = refs[n_r + n_v:n_r + n_v + n_o]
        a_refs = refs[n_r + n_v + n_o:]
        o_vals, a_vals = fn(*vals)
        for ref, val in zip(o_refs, o_vals):
            ref[...] = val.astype(ref.dtype)
        if n_a:
            @pl.when(pl.program_id(0) == 0)
            def _():
                for ref in a_refs:
                    ref[...] = jnp.zeros_like(ref)

            for ref, val in zip(a_refs, a_vals):
                ref[...] += val

    in_specs = [pl.BlockSpec((tr, w), functools.partial(lambda i, cb: (i, cb), cb=cb)) for _, w, cb in rows]
    in_specs += [pl.BlockSpec(v.shape, lambda i: (0, 0)) for v in vecs]
    out_specs = [pl.BlockSpec((tr, w), lambda i: (i, 0)) for w, _ in outs]
    out_specs += [pl.BlockSpec((1, w), lambda i: (0, 0)) for w in accs]
    out_shape = [jax.ShapeDtypeStruct((s, w), dt) for w, dt in outs]
    out_shape += [jax.ShapeDtypeStruct((1, w), F32) for w in accs]
    res = pl.pallas_call(
        body, name=name, grid=(s // tr,), in_specs=in_specs, out_specs=out_specs, out_shape=out_shape,
        compiler_params=_cparams(("arbitrary",)),
    )(*[r[0] for r in rows], *vecs)
    return res[:n_o], res[n_o:]


_ROW_TILE = 512
_STREAM_SPECS = [pl.BlockSpec((dil, _ROW_TILE // dil, D_MODEL), lambda i: (0, i, 0)) for dil in DIL_GROUPS[1:]]
_NAT_SPEC = pl.BlockSpec((_ROW_TILE, D_MODEL), lambda i: (i, 0))
_VEC_SPEC = pl.BlockSpec((1, D_MODEL), lambda i: (0, 0))
_COL_BLOCKS = pltpu.VMEM((D_MODEL // 128, _ROW_TILE, 128), F32)


def _prenorm(xs, g, s, carry=None):
    tr = _ROW_TILE

    def body(x_ref, g_ref, u_ref, u4_ref, u16_ref, buf):
        xn = _rms(x_ref[...]) * g_ref[...]
        u_ref[...] = xn.astype(BF16)
        for cb in range(8):
            buf[cb] = xn[:, cb * 128:(cb + 1) * 128]
        for dil, out in ((4, u4_ref), (16, u16_ref)):
            for c in range(dil):
                rows = pl.ds(c, tr // dil, stride=dil)
                out[c] = jnp.concatenate([buf.at[cb][rows, :] for cb in range(8)], axis=1).astype(BF16)

    res, got = _pallas(
        body, name="prenorm", grid=(s // tr,), in_specs=[_NAT_SPEC, _VEC_SPEC], out_specs=[_NAT_SPEC] + _STREAM_SPECS,
        out_shape=[jax.ShapeDtypeStruct((s, D_MODEL), BF16)]
        + [jax.ShapeDtypeStruct((dil, s // dil, D_MODEL), BF16) for dil in DIL_GROUPS[1:]],
        scratch=[_COL_BLOCKS], semantics=("parallel",), args=(xs, g), carry=carry)
    return [r.reshape(s, D_MODEL) for r in res], got


def _grad_x(xs, d_h1, du_nat, du4, du16, g, s):
    tr = _ROW_TILE

    def body(x_ref, dh_ref, a_ref, b_ref, c_ref, u4_ref, u16_ref, g_ref, dx_ref, dg_ref, buf):
        du = a_ref[...].astype(F32) + b_ref[...].astype(F32) + c_ref[...].astype(F32)
        for dil, src in ((4, u4_ref), (16, u16_ref)):
            for c in range(dil):
                part = src[c].astype(F32)
                for cb in range(8):
                    buf.at[cb][pl.ds(c, tr // dil, stride=dil), :] = part[:, cb * 128:(cb + 1) * 128]
            du = du + jnp.concatenate([buf[cb] for cb in range(8)], axis=1)
        dx, dgr = _rms_bwd(x_ref[...], g_ref[...], du)
        dx_ref[...] = dh_ref[...] + dx

        @pl.when(pl.program_id(0) == 0)
        def _():
            dg_ref[...] = jnp.zeros_like(dg_ref)

        dg_ref[...] += _colsum(dgr)

    return pl.pallas_call(
        body, name="grad_x", grid=(s // tr,), in_specs=[_NAT_SPEC] * 5 + _STREAM_SPECS + [_VEC_SPEC],
        out_specs=[_NAT_SPEC, _VEC_SPEC],
        out_shape=[jax.ShapeDtypeStruct((s, D_MODEL), F32), jax.ShapeDtypeStruct((1, D_MODEL), F32)],
        scratch_shapes=[_COL_BLOCKS], compiler_params=_cparams(("arbitrary",)),
    )(xs, d_h1, *du_nat, du4.reshape(4, s // 4, D_MODEL), du16.reshape(16, s // 16, D_MODEL), g)


def _ret_tables():
    h = np.arange(RET_HEADS, dtype=np.float32)
    lg = np.log1p(-(np.float32(2.0) ** (-5.0 - h))).astype(np.float32)
    idx = np.arange(RET_CHUNK, dtype=np.float32)
    diff = idx[:, None] - idx[None, :]
    dm = np.where(diff[None] >= 0, np.exp(np.maximum(diff, 0.0)[None] * lg[:, None, None]), 0.0)
    qd = np.exp((idx + 1.0)[None, :, None] * lg[:, None, None])
    kd = np.exp((RET_CHUNK - 1.0 - idx)[None, :, None] * lg[:, None, None])
    cd = np.exp(RET_CHUNK * lg)[:, None, None]
    return [jnp.asarray(t, F32) for t in (dm, qd, kd, cd)]


def _rope_half(v, cos, sin):
    v1, v2 = v[:, :128], v[:, 128:]
    return jnp.concatenate([v1 * cos - v2 * sin, v2 * cos + v1 * sin], axis=1)


def _unrope_half(d, cos, sin):
    d1, d2 = d[:, :128], d[:, 128:]
    return jnp.concatenate([d1 * cos + d2 * sin, d2 * cos - d1 * sin], axis=1)


_RET_HEADS_FWD, _RET_HEADS_BWD = 1, 2


def _ret_specs(rb, rev_n, hp):
    def rowmap(w_blk):
        return lambda h, n: (rev_n(n), w_blk(h))
    tab = [pl.BlockSpec((hp, RET_CHUNK, RET_CHUNK), lambda h, n: (h, 0, 0)),
           pl.BlockSpec((hp, RET_CHUNK, 1), lambda h, n: (h, 0, 0)),
           pl.BlockSpec((hp, RET_CHUNK, 1), lambda h, n: (h, 0, 0)),
           pl.BlockSpec((hp, 1, 1), lambda h, n: (h, 0, 0))]
    proj = pl.BlockSpec((rb, hp * 1536), rowmap(lambda h: h))
    cs = pl.BlockSpec((rb, 128), rowmap(lambda h: 0))
    hv = pl.BlockSpec((rb, hp * RET_V), rowmap(lambda h: h))
    return proj, cs, hv, tab


def _ret_fwd(proj_ret, cos, sin, s, carry=None):
    rb = min(512, s)
    ch = rb // RET_CHUNK
    nb = s // rb
    hp = _RET_HEADS_FWD
    proj_spec, cs_spec, hv_spec, tab_specs = _ret_specs(rb, lambda n: n, hp)

    def body(p_ref, cos_ref, sin_ref, dm_ref, qd_ref, kd_ref, cd_ref, yr_ref, y_ref, rs_ref, r_acc):
        @pl.when(pl.program_id(1) == 0)
        def _():
            r_acc[...] = jnp.zeros_like(r_acc)

        for c, hh in [(c, hh) for c in range(ch) for hh in range(hp)]:
            rows = slice(c * RET_CHUNK, (c + 1) * RET_CHUNK)
            pc, hc = hh * 1536, hh * RET_V
            dm, qd, kd, cd = dm_ref[hh], qd_ref[hh], kd_ref[hh], cd_ref[hh]
            cosv, sinv = cos_ref[rows, :], sin_ref[rows, :]
            q = _rope_half(p_ref[rows, pc:pc + 256].astype(F32), cosv, sinv)
            kk = _rope_half(p_ref[rows, pc + 256:pc + 512].astype(F32), cosv, sinv) * (RET_QK ** -0.5)
            v = p_ref[rows, pc + 512:pc + 1024]
            g = p_ref[rows, pc + 1024:pc + 1536].astype(F32)
            rb16 = r_acc[hh].astype(BF16)
            rs_ref[hh, c] = rb16
            sc = _dot(q.astype(BF16), kk.astype(BF16), NT) * dm
            y = _dot(sc.astype(BF16), v, NN) + _dot((q * qd).astype(BF16), rb16, NN)
            r_acc[hh] = r_acc[hh] * cd + _dot((kk * kd).astype(BF16), v, TN)
            y_ref[rows, hc:hc + RET_V] = y.astype(BF16)
            yr_ref[rows, hc:hc + RET_V] = (_rms(y) * (g * _sigmoid(g))).astype(BF16)

    return _pallas(
        body, name="ret_fwd", grid=(RET_HEADS // hp, nb),
        in_specs=[proj_spec, cs_spec, cs_spec] + tab_specs,
        out_specs=[hv_spec, hv_spec, pl.BlockSpec((hp, ch, RET_QK, RET_V), lambda h, n: (h, n, 0, 0))],
        out_shape=[jax.ShapeDtypeStruct((s, RET_HEADS * RET_V), BF16), jax.ShapeDtypeStruct((s, RET_HEADS * RET_V), BF16),
                   jax.ShapeDtypeStruct((RET_HEADS, s // RET_CHUNK, RET_QK, RET_V), BF16)],
        scratch=[pltpu.VMEM((hp, RET_QK, RET_V), F32)], semantics=("parallel", "arbitrary"),
        args=(proj_ret, cos, sin, *_ret_tables()), carry=carry)


def _ret_bwd(proj_ret, cos, sin, y, d_yr, rs, s, carry=None):
    rb = min(512, s)
    ch = rb // RET_CHUNK
    nb = s // rb
    hp = _RET_HEADS_BWD
    proj_spec, cs_spec, hv_spec, tab_specs = _ret_specs(rb, lambda n: nb - 1 - n, hp)

    def body(p_ref, cos_ref, sin_ref, y_ref, dyr_ref, rs_ref, dm_ref, qd_ref, kd_ref, cd_ref, o_ref, dr_acc):
        @pl.when(pl.program_id(1) == 0)
        def _():
            dr_acc[...] = jnp.zeros_like(dr_acc)

        for c, hh in [(c, hh) for c in reversed(range(ch)) for hh in range(hp)]:
            rows = slice(c * RET_CHUNK, (c + 1) * RET_CHUNK)
            pc, hc = hh * 1536, hh * RET_V
            dm, qd, kd, cd = dm_ref[hh], qd_ref[hh], kd_ref[hh], cd_ref[hh]
            cosv, sinv = cos_ref[rows, :], sin_ref[rows, :]
            q = _rope_half(p_ref[rows, pc:pc + 256].astype(F32), cosv, sinv)
            kk = _rope_half(p_ref[rows, pc + 256:pc + 512].astype(F32), cosv, sinv) * (RET_QK ** -0.5)
            v = p_ref[rows, pc + 512:pc + 1024]
            g = p_ref[rows, pc + 1024:pc + 1536].astype(F32)
            yv = y_ref[rows, hc:hc + RET_V].astype(F32)
            dyr = dyr_ref[rows, hc:hc + RET_V].astype(F32)
            sg = _sigmoid(g)
            r = lax.rsqrt(jnp.mean(yv * yv, axis=-1, keepdims=True) + EPS)
            yn = yv * r
            dg = dyr * yn * (sg * (1.0 + g * (1.0 - sg)))
            dyn = dyr * (g * sg)
            dy = (r * (dyn - yn * jnp.mean(dyn * yn, axis=-1, keepdims=True))).astype(BF16)
            qb, kb = q.astype(BF16), kk.astype(BF16)
            rb16 = rs_ref[hh, c]
            drb = dr_acc[hh].astype(BF16)
            sd = _dot(qb, kb, NT) * dm
            ds = (_dot(dy, v, NT) * dm).astype(BF16)
            dq = _dot(ds, kb, NN) + qd * _dot(dy, rb16, NT)
            dk = _dot(ds, qb, TN) + kd * _dot(v, drb, NT)
            dv = _dot(sd.astype(BF16), dy, TN) + _dot((kk * kd).astype(BF16), drb, NN)
            dr_acc[hh] = dr_acc[hh] * cd + _dot((q * qd).astype(BF16), dy, TN)
            o_ref[rows, pc:pc + 256] = _unrope_half(dq, cosv, sinv).astype(BF16)
            o_ref[rows, pc + 256:pc + 512] = (_unrope_half(dk, cosv, sinv) * (RET_QK ** -0.5)).astype(BF16)
            o_ref[rows, pc + 512:pc + 1024] = dv.astype(BF16)
            o_ref[rows, pc + 1024:pc + 1536] = dg.astype(BF16)

    in_specs = [proj_spec, cs_spec, cs_spec, hv_spec, hv_spec,
                pl.BlockSpec((hp, ch, RET_QK, RET_V), lambda h, n: (h, nb - 1 - n, 0, 0))] + tab_specs
    return _pallas(
        body, name="ret_bwd", grid=(RET_HEADS // hp, nb), in_specs=in_specs, out_specs=[proj_spec],
        out_shape=[jax.ShapeDtypeStruct((s, RET_HEADS * 1536), BF16)], scratch=[pltpu.VMEM((hp, RET_QK, RET_V), F32)],
        semantics=("parallel", "arbitrary"), args=(proj_ret, cos, sin, y, d_yr, rs, *_ret_tables()), carry=carry)


def _rope_qk(acc, c, s1, s2):
    outs = []
    for cc in range(8):
        vv = acc[:, cc * 128:(cc + 1) * 128]
        outs.append(vv * c + pltpu.roll(vv, 120, 1) * s1 + pltpu.roll(vv, 8, 1) * s2)
    return jnp.concatenate(outs + [acc[:, 2 * DIL_W:]], axis=1)


def _pair_masks(keys_on_rows=False):
    ri = lax.broadcasted_iota(jnp.int32, (2 * QB, 2 * QB), 1 if keys_on_rows else 0)
    ci = lax.broadcasted_iota(jnp.int32, (2 * QB, 2 * QB), 0 if keys_on_rows else 1)
    e = ci - (ri & (QB - 1))
    lane_lo = lax.broadcasted_iota(jnp.int32, (2 * QB, 128), 1) < 64
    return ci, jnp.logical_and(e >= 0, e <= QB), lane_lo


def _stack_heads(v, lane_lo):
    z = jnp.zeros_like(v)
    return jnp.concatenate([jnp.where(lane_lo, v, z), jnp.where(lane_lo, z, v)], axis=0)


def _dil_fwd(qkv, dil, s, name):
    length = s // dil
    rb = min(512, length)
    nsub = rb // QB
    nbs = length // rb
    sub_per = rb // QB

    def body(q_ref, k_ref, v_ref, kp_ref, vp_ref, o_ref, l_ref):
        first = (pl.program_id(0) % nbs) == 0
        ci, band, lane_lo = _pair_masks()
        lo1 = lane_lo[0:QB]

        for i in range(nsub):
            rows = slice(i * QB, (i + 1) * QB)
            mask = jnp.logical_and(band, ci >= jnp.where(first, QB, 0)) if i == 0 else band
            for j in range(4):
                lanes = slice(j * 128, (j + 1) * 128)
                q2 = _stack_heads(q_ref[rows, lanes], lo1)
                if i == 0:
                    k2 = jnp.concatenate([kp_ref[:, lanes], k_ref[rows, lanes]], axis=0)
                    v2 = jnp.concatenate([vp_ref[:, lanes], v_ref[rows, lanes]], axis=0)
                else:
                    k2, v2 = k_ref[(i - 1) * QB:(i + 1) * QB, lanes], v_ref[(i - 1) * QB:(i + 1) * QB, lanes]
                v2 = _stack_heads(v2, lane_lo)
                sc = jnp.where(mask, _dot(q2, k2, NT) * 0.125, NEG)
                m = jnp.max(sc, axis=1, keepdims=True)
                p = jnp.exp(sc - m)
                den = jnp.sum(p, axis=1, keepdims=True)
                pb = p.astype(BF16)
                o = _dot(jnp.concatenate([pb[0:QB], pb[QB:]], axis=1), v2, NN)
                inv = 1.0 / den
                lse = m + jnp.log(den)
                o_ref[rows, lanes] = o * jnp.where(lo1, inv[0:QB], inv[QB:])
                l_ref[rows, lanes] = jnp.where(lo1, lse[0:QB], lse[QB:])

    prev = lambda n: jnp.maximum(n * sub_per - 1, 0)
    cur = lambda cb: (lambda n: (n, cb))
    return pl.pallas_call(
        body, name=name, grid=(s // rb,),
        in_specs=[pl.BlockSpec((rb, DIL_W), cur(0)), pl.BlockSpec((rb, DIL_W), cur(1)), pl.BlockSpec((rb, DIL_W), cur(2)),
                  pl.BlockSpec((QB, DIL_W), lambda n: (prev(n), 1)), pl.BlockSpec((QB, DIL_W), lambda n: (prev(n), 2))],
        out_specs=[pl.BlockSpec((rb, DIL_W), cur(0)), pl.BlockSpec((rb, DIL_W), cur(0))],
        out_shape=[jax.ShapeDtypeStruct((s, DIL_W), F32), jax.ShapeDtypeStruct((s, DIL_W), F32)],
        compiler_params=_cparams(("parallel",)),
    )(qkv, qkv, qkv, qkv, qkv)


def _dil_bwd(qkv, dya, lse, dlt, tc, ts1, ts2, dil, s, name):
    length = s // dil
    rb = min(512, length)
    nsub = rb // QB
    nbs = length // rb
    last_blk = s // QB - 1

    def body(q_ref, k_ref, v_ref, kp_ref, vp_ref, qn_ref, dy_ref, dyn_ref, l_ref, ln_ref, d_ref, dn_ref,
             c_ref, s1_ref, s2_ref, o_ref, dka, dva):
        nl = pl.program_id(0) % nbs
        first, last = nl == 0, nl == nbs - 1
        ci, band, lane_lo = _pair_masks(keys_on_rows=True)
        lo1 = lane_lo[0:QB]

        def unrope(d, rows):
            return d * c_ref[rows, :] + pltpu.roll(d * s1_ref[rows, :], 8, 1) + pltpu.roll(d * s2_ref[rows, :], 120, 1)

        for qi in range(nsub + 1):
            nxt = qi == nsub
            rows = slice((nsub - 1) * QB, nsub * QB) if nxt else slice(qi * QB, (qi + 1) * QB)
            prev_rows = slice((qi - 1) * QB, qi * QB)
            if qi == 0:
                mask = jnp.logical_and(band, ci >= jnp.where(first, QB, 0))
            elif nxt:
                mask = jnp.logical_and(band, ci <= jnp.where(last, -1, QB - 1))[0:QB, :]
            else:
                mask = band
            for j in range(4):
                lanes = slice(j * 128, (j + 1) * 128)
                if nxt:
                    q, do, lv, dl = qn_ref[:, lanes], dyn_ref[:, lanes], ln_ref[:, lanes], dn_ref[:, lanes]
                    k2, v2 = k_ref[prev_rows, lanes], v_ref[prev_rows, lanes]
                else:
                    q, do, lv, dl = q_ref[rows, lanes], dy_ref[rows, lanes], l_ref[rows, lanes], d_ref[rows, lanes]
                    if qi == 0:
                        k2 = jnp.concatenate([kp_ref[:, lanes], k_ref[rows, lanes]], axis=0)
                        v2 = jnp.concatenate([vp_ref[:, lanes], v_ref[rows, lanes]], axis=0)
                    else:
                        k2, v2 = k_ref[(qi - 1) * QB:(qi + 1) * QB, lanes], v_ref[(qi - 1) * QB:(qi + 1) * QB, lanes]
                q2, do2 = _stack_heads(q, lo1), _stack_heads(do, lo1)
                lt, dt = lv.T, dl.T
                lse2 = jnp.concatenate([lt[0:1], lt[64:65]], axis=1)
                dl2 = jnp.concatenate([dt[0:1], dt[64:65]], axis=1)
                sc = _dot(k2, q2, NT) * 0.125
                p = jnp.where(mask, jnp.exp(jnp.minimum(sc - lse2, 0.0)), 0.0)
                ds = (p * (_dot(v2, do2, NT) - dl2) * 0.125).astype(BF16)
                dk2 = _dot(ds, q2, NN)
                dv2 = _dot(p.astype(BF16), do2, NN)
                if qi >= 1:
                    dka[prev_rows, lanes] += dk2[0:QB]
                    dva[prev_rows, lanes] += dv2[0:QB]
                if not nxt:
                    dka[rows, lanes] = dk2[QB:]
                    dva[rows, lanes] = dv2[QB:]
                    dq = _dot(jnp.concatenate([ds[:, 0:QB], ds[:, QB:]], axis=0), _stack_heads(k2, lane_lo), TN)
                    o_ref[rows, lanes] = unrope(dq, rows).astype(BF16)

        for cc in range(4):
            lanes = slice(cc * 128, (cc + 1) * 128)
            o_ref[:, 512 + cc * 128:512 + (cc + 1) * 128] = unrope(dka[:, lanes], slice(None)).astype(BF16)
            o_ref[:, 1024 + cc * 128:1024 + (cc + 1) * 128] = dva[:, lanes].astype(BF16)

    prev = lambda n: jnp.maximum(n * nsub - 1, 0)
    nxt = lambda n: jnp.minimum(n * nsub + nsub, last_blk)
    cur = lambda cb: (lambda n: (n, cb))
    big = lambda cb: pl.BlockSpec((rb, DIL_W), cur(cb))
    small = lambda im: pl.BlockSpec((QB, DIL_W), im)
    tab = pl.BlockSpec((rb, 128), cur(0))
    return pl.pallas_call(
        body, name=name, grid=(s // rb,),
        in_specs=[big(0), big(1), big(2), small(lambda n: (prev(n), 1)), small(lambda n: (prev(n), 2)),
                  small(lambda n: (nxt(n), 0)), big(0), small(lambda n: (nxt(n), 0)), big(0), small(lambda n: (nxt(n), 0)),
                  big(0), small(lambda n: (nxt(n), 0)), tab, tab, tab],
        out_specs=pl.BlockSpec((rb, 3 * DIL_W), cur(0)),
        out_shape=jax.ShapeDtypeStruct((s, 3 * DIL_W), BF16),
        scratch_shapes=[pltpu.VMEM((rb, DIL_W), F32), pltpu.VMEM((rb, DIL_W), F32)],
        compiler_params=_cparams(("parallel",)),
    )(qkv, qkv, qkv, qkv, qkv, qkv, dya, dya, lse, lse, dlt, dlt, tc, ts1, ts2)


def _stream_specs(tr):
    nat = pl.BlockSpec((tr, 128), lambda i, j: (i, j))
    return [nat] + [pl.BlockSpec((dil, tr // dil, 128), lambda i, j: (0, i, j)) for dil in DIL_GROUPS[1:]]


def _dil_merge(o_g, l_g, s):
    tr = min(2048, s)
    nat, sp4, sp16 = _stream_specs(tr)

    def body(o0_ref, l0_ref, o1_ref, l1_ref, o2_ref, l2_ref, ya_ref, lse_ref, o1n, l1n, o2n, l2n):
        for src, dst, dil in ((o1_ref, o1n, 4), (l1_ref, l1n, 4), (o2_ref, o2n, 16), (l2_ref, l2n, 16)):
            for c in range(dil):
                dst[pl.ds(c, tr // dil, stride=dil), :] = src[c]
        l0, l1, l2 = l0_ref[...], l1n[...], l2n[...]
        m = jnp.maximum(jnp.maximum(l0, l1), l2)
        e0, e1, e2 = jnp.exp(l0 - m), jnp.exp(l1 - m), jnp.exp(l2 - m)
        den = e0 + e1 + e2
        ya_ref[...] = ((e0 * o0_ref[...] + e1 * o1n[...] + e2 * o2n[...]) / den).astype(BF16)
        lse_ref[...] = m + jnp.log(den)

    v3 = lambda a, dil: a.reshape(dil, s // dil, DIL_W)
    return pl.pallas_call(
        body, name="dil_merge", grid=(s // tr, 4),
        in_specs=[nat, nat, sp4, sp4, sp16, sp16], out_specs=[nat, nat],
        out_shape=[jax.ShapeDtypeStruct((s, DIL_W), BF16), jax.ShapeDtypeStruct((s, DIL_W), F32)],
        scratch_shapes=[pltpu.VMEM((tr, 128), F32)] * 4,
        compiler_params=_cparams(("parallel", "parallel")),
    )(o_g[0], l_g[0], v3(o_g[1], 4), v3(l_g[1], 4), v3(o_g[2], 16), v3(l_g[2], 16))


def _dil_bwd_prep(d_ya, ya, lse, s):
    tr = min(2048, s)
    nat, sp4, sp16 = _stream_specs(tr)

    def body(dya_ref, ya_ref, lse_ref, dy0, dl0, dy1, ls1, dl1, dy2, ls2, dl2, dlt):
        lane_lo = lax.broadcasted_iota(jnp.int32, (tr, 128), 1) < 64
        prod = dya_ref[...] * ya_ref[...].astype(F32)
        lo = jnp.where(lane_lo, prod, 0.0)
        dlt[...] = jnp.where(lane_lo, jnp.sum(lo, axis=1, keepdims=True), jnp.sum(prod - lo, axis=1, keepdims=True))
        dy0[...] = dya_ref[...].astype(BF16)
        dl0[...] = dlt[...]
        for dil, dy, ls, dl in ((4, dy1, ls1, dl1), (16, dy2, ls2, dl2)):
            for c in range(dil):
                rows = pl.ds(c, tr // dil, stride=dil)
                dy[c] = dya_ref[rows, :].astype(BF16)
                ls[c] = lse_ref[rows, :]
                dl[c] = dlt[rows, :]

    sh = lambda dil, dt: jax.ShapeDtypeStruct((dil, s // dil, DIL_W), dt)
    res = pl.pallas_call(
        body, name="dil_bwd_prep", grid=(s // tr, 4),
        in_specs=[nat, nat, nat], out_specs=[nat, nat, sp4, sp4, sp4, sp16, sp16, sp16],
        out_shape=[jax.ShapeDtypeStruct((s, DIL_W), BF16), jax.ShapeDtypeStruct((s, DIL_W), F32),
                   sh(4, BF16), sh(4, F32), sh(4, F32), sh(16, BF16), sh(16, F32), sh(16, F32)],
        scratch_shapes=[pltpu.VMEM((tr, 128), F32)],
        compiler_params=_cparams(("parallel", "parallel")),
    )(d_ya, ya, lse)
    dy0, dl0, dy1, ls1, dl1, dy2, ls2, dl2 = [r.reshape(s, DIL_W) for r in res]
    return [(dy0, lse, dl0), (dy1, ls1, dl1), (dy2, ls2, dl2)]


_RET_SEGS = ((0, 256), (1024, 256), (2048, 512), (4096, 512))


def _split_w_in(win):
    per_head = [win[a:a + RET_HEADS * n].reshape(RET_HEADS, n, D_MODEL) for a, n in _RET_SEGS]
    w_ret = jnp.concatenate(per_head, axis=1).reshape(RET_HEADS * 1536, D_MODEL)
    w_dil = [jnp.concatenate([win[a + DIL_W * g:a + DIL_W * (g + 1)] for a in (6144, 7680, 9216)], axis=0) for g in range(3)]
    return w_ret, win[10752:12800], w_dil


def _join_w_in(g_ret, g_gate, g_dil):
    g_ret = g_ret.reshape(RET_HEADS, 1536, D_MODEL)
    off = (0, 256, 512, 1024, 1536)
    parts = [g_ret[:, off[i]:off[i + 1]].reshape(-1, D_MODEL) for i in range(4)]
    dil = [g_dil[g][DIL_W * i:DIL_W * (i + 1)] for i in range(3) for g in range(3)]
    return jnp.concatenate(parts + dil + [g_gate], axis=0)


def _local_step(xs, pb, tgt, tabs, wts, vec, s, shards=None):
    tm = min(2048, s)
    tr = min(512, s)
    mm = functools.partial(_matmul, tm=tm)
    on_mesh = shards is not None
    wts, vec = dict(wts), dict(vec)
    blocks = lambda g: g.reshape(N_DEV, g.shape[0] // N_DEV, g.shape[1])

    late_shards = dict(shards) if on_mesh else {}
    first = _TwoLevelGather([late_shards.pop("w_in"), late_shards.pop("b_gate")]) if on_mesh else None
    u, gathered = _prenorm(xs, vec["g_pre_mix"], s, carry=first)
    if on_mesh:
        wts["w_in"] = gathered[0].reshape(N_DEV * gathered[0].shape[1], D_MODEL)
        bias = gathered[1].transpose(1, 0, 2).reshape(2, D_MODEL)
        vec.update(b0=bias[0:1], b1=bias[1:2])
    w_ret, w_gate, w_dil = _split_w_in(wts["w_in"])
    proj_ret = mm(u[0], w_ret, mode="nt", m=s, n=6144, k=1024, tn=1024, tk=1024, out_dtype=BF16, name="inproj_ret")
    proj_gate = mm(u[0], w_gate, mode="nt", m=s, n=2048, k=1024, tn=1024, tk=1024, out_dtype=BF16, name="inproj_gate")
    qkv = [_matmul(u[g], w_dil[g], mode="nt", m=s, n=1536, k=1024, tm=min(1024, s), tn=1536, tk=1024, out_dtype=BF16,
                   name="inproj_dil%d" % g, epi=tabs["dil"][g], epi_width=128, epi_fn=_rope_qk) for g in range(3)]

    names = list(late_shards) if on_mesh else []
    gather = _Exchange([late_shards[n] for n in names], [False] * len(names)) if on_mesh else None
    (yr, y_ret, rstate), gathered = _ret_fwd(proj_ret, tabs["cos_r"], tabs["sin_r"], s, carry=gather)
    wts.update({n: g.reshape(N_DEV * g.shape[1], g.shape[2]) for n, g in zip(names, gathered)})
    a_br = mm(yr, wts["w_ret_out"], mode="nn", m=s, n=1024, k=2048, tn=1024, tk=2048, out_dtype=BF16, name="ret_out")

    o_g, l_g = [], []
    for g, dil in enumerate(DIL_GROUPS):
        o, l = _dil_fwd(qkv[g], dil, s, "dil_fwd%d" % g)
        o_g.append(o)
        l_g.append(l)
    ya, lse = _dil_merge(o_g, l_g, s)
    b_br = mm(ya, wts["w_dil_out"], mode="nt", m=s, n=1024, k=512, tn=1024, tk=512, out_dtype=BF16, name="dil_out")

    def gate_mix(a, b, gr, ga, b0, b1):
        return [_sigmoid(gr.astype(F32) + b0) * a.astype(F32) + _sigmoid(ga.astype(F32) + b1) * b.astype(F32)], []

    (mixed,), _ = _rowwise("gate_mix", gate_mix, s, tr, [(a_br, 1024, 0), (b_br, 1024, 0), (proj_gate, 1024, 0), (proj_gate, 1024, 1)],
                           [vec["b0"], vec["b1"]], [(1024, BF16)])
    z = mm(mixed, wts["w_o"], mode="nn", m=s, n=1024, k=1024, tn=1024, tk=1024, out_dtype=BF16, name="w_o")

    def post_norm(h, f, g_post, g_pre):
        hn = h + _rms(f) * g_post
        return [hn, _rms(hn) * g_pre], []

    (h1, v2), _ = _rowwise("post_mix", post_norm, s, tr, [(xs, 1024, 0), (z, 1024, 0)], [vec["g_post_mix"], vec["g_pre_mlp"]],
                           [(1024, F32), (1024, BF16)])
    a_up = mm(v2, wts["w_up"], mode="nt", m=s, n=4096, k=1024, tn=1024, tk=1024, out_dtype=BF16, name="mlp_up")
    f_dn = mm(a_up, wts["w_down"], mode="nn", m=s, n=1024, k=4096, tn=1024, tk=1024, out_dtype=BF16, name="mlp_down", a_fn=_relu_sq)
    (h2, t_ple), _ = _rowwise("post_mlp", post_norm, s, tr, [(h1, 1024, 0), (f_dn, 1024, 0)], [vec["g_post_mlp"], vec["g_pre_ple"]],
                              [(1024, F32), (1024, BF16)])
    gl = mm(t_ple, wts["w_ple_gate"], mode="nn", m=s, n=1024, k=1024, tn=1024, tk=1024, out_dtype=BF16, name="ple_gate")
    e_ple = mm(pb, wts["w_ple_in"], mode="nt", m=s, n=1024, k=256, tn=1024, tk=256, out_dtype=BF16, name="ple_in")

    def ple_loss(h, glv, e, tg, b, g):
        gate = _sigmoid(glv + b)
        ge = gate * e
        diff = h + _rms(ge) * g - tg
        dy = diff * (1.0 / D_MODEL)
        d_ge, dg = _rms_bwd(ge, g, dy)
        d_gl = d_ge * e * gate * (1.0 - gate)
        loss = jnp.zeros((1, D_MODEL), F32) + 0.5 * jnp.sum(diff * diff) * (1.0 / D_MODEL)
        return [dy, d_gl, d_ge * gate], [_colsum(dg), _colsum(d_gl), loss]

    (dy, d_gl, d_e), (dg_post_ple, db_ple, loss) = _rowwise(
        "ple_loss", ple_loss, s, tr, [(h2, 1024, 0), (gl, 1024, 0), (e_ple, 1024, 0), (tgt, 1024, 0)],
        [vec["b_ple"], vec["g_post_ple"]], [(1024, F32), (1024, BF16), (1024, BF16)], [1024, 1024, 1024])

    ts, ts2 = min(1024, s), min(2048, s)
    wg = functools.partial(_matmul, mode="tn", k=s, tk=ts, out_dtype=BF16)
    grads = {}
    grads["w_ple_in"] = wg(d_e, pb, m=1024, n=256, tm=1024, tn=256, tk=ts2, name="g_ple_in")
    grads["w_ple_gate"] = wg(t_ple, d_gl, m=1024, n=1024, tm=1024, tn=1024, tk=ts2, name="g_ple_gate")
    d_t = mm(d_gl, wts["w_ple_gate"], mode="nt", m=s, n=1024, k=1024, tn=1024, tk=1024, out_dtype=BF16, name="d_t")

    def bwd_ple_mlp(h, dt, dyv, f, g_pre, g_post):
        dx, dg1 = _rms_bwd(h, g_pre, dt)
        dh = dyv + dx
        df, dg2 = _rms_bwd(f, g_post, dh)
        return [dh, df], [_colsum(dg1), _colsum(dg2)]

    (d_h2, d_f), (dg_pre_ple, dg_post_mlp) = _rowwise(
        "bwd_ple_mlp", bwd_ple_mlp, s, tr, [(h2, 1024, 0), (d_t, 1024, 0), (dy, 1024, 0), (f_dn, 1024, 0)],
        [vec["g_pre_ple"], vec["g_post_mlp"]], [(1024, F32), (1024, BF16)], [1024, 1024])
    d_a = mm(d_f, wts["w_down"], mode="nt", m=s, n=4096, k=1024, tn=1024, tk=1024, out_dtype=BF16, name="d_a",
             epi=(a_up,), epi_fn=lambda acc, av: acc * (2.0 * jnp.maximum(av.astype(F32), 0.0)))
    grads["w_down"] = wg(a_up, d_f, m=4096, n=1024, tm=2048, tn=1024, name="g_down", a_fn=_relu_sq)
    grads["w_up"] = wg(d_a, v2, m=4096, n=1024, tm=2048, tn=1024, tk=ts2, name="g_up")
    d_v2 = mm(d_a, wts["w_up"], mode="nn", m=s, n=1024, k=4096, tn=1024, tk=2048, out_dtype=BF16, name="d_v2")

    (d_h1, d_z), (dg_pre_mlp, dg_post_mix) = _rowwise(
        "bwd_mlp_mix", bwd_ple_mlp, s, tr, [(h1, 1024, 0), (d_v2, 1024, 0), (d_h2, 1024, 0), (z, 1024, 0)],
        [vec["g_pre_mlp"], vec["g_post_mix"]], [(1024, F32), (1024, BF16)], [1024, 1024])
    d_mixed = mm(d_z, wts["w_o"], mode="nt", m=s, n=1024, k=1024, tn=1024, tk=1024, out_dtype=BF16, name="d_mixed")
    grads["w_o"] = wg(mixed, d_z, m=1024, n=1024, tm=1024, tn=1024, tk=ts2, name="g_o")

    def bwd_gate(dm, a, b, gr, ga, b0, b1):
        sa, sb = _sigmoid(gr.astype(F32) + b0), _sigmoid(ga.astype(F32) + b1)
        dgr = dm * a.astype(F32) * sa * (1.0 - sa)
        dga = dm * b.astype(F32) * sb * (1.0 - sb)
        return [dm * sa, dm * sb, jnp.concatenate([dgr, dga], axis=1)], [_colsum(dgr), _colsum(dga)]

    (d_abr, d_bbr, dproj_gate), (db0, db1) = _rowwise(
        "bwd_gate", bwd_gate, s, tr, [(d_mixed, 1024, 0), (a_br, 1024, 0), (b_br, 1024, 0), (proj_gate, 1024, 0), (proj_gate, 1024, 1)],
        [vec["b0"], vec["b1"]], [(1024, BF16), (1024, BF16), (2048, BF16)], [1024, 1024])
    grads["w_ret_out"] = wg(yr, d_abr, m=2048, n=1024, tm=2048, tn=1024, tk=ts2, name="g_ret_out")
    d_yr = mm(d_abr, wts["w_ret_out"], mode="nt", m=s, n=2048, k=1024, tn=1024, tk=1024, out_dtype=BF16, name="d_yr")
    grads["w_dil_out"] = wg(d_bbr, ya, m=1024, n=512, tm=1024, tn=512, tk=ts2, name="g_dil_out")
    d_ya = mm(d_bbr, wts["w_dil_out"], mode="nn", m=s, n=512, k=1024, tn=512, tk=1024, out_dtype=F32, name="d_ya")

    slots = {}
    names = list(grads) if on_mesh else []
    shares = _Exchange([blocks(grads[n]) for n in names], [True] * len(names)) if on_mesh else None
    (dproj_ret,), got = _ret_bwd(proj_ret, tabs["cos_r"], tabs["sin_r"], y_ret, d_yr, rstate, s, carry=shares)
    slots.update(zip(names, got))
    upstream = _dil_bwd_prep(d_ya, ya, lse, s)
    dqkv = [_dil_bwd(qkv[g], *upstream[g], *tabs["dil"][g], dil, s, "dil_bwd%d" % g)
            for g, dil in enumerate(DIL_GROUPS)]

    g_ret = wg(dproj_ret, u[0], m=6144, n=1024, tm=2048, tn=1024, tk=ts2, name="g_in_ret")
    g_gate = wg(dproj_gate, u[0], m=2048, n=1024, tm=2048, tn=1024, tk=ts2, name="g_in_gate")
    g_dil = [wg(dqkv[g], u[g], m=1536, n=1024, tm=1536, tn=1024, tk=ts2, name="g_in_dil%d" % g) for g in range(3)]
    grads["w_in"] = _join_w_in(g_ret, g_gate, g_dil)

    du_ret = functools.partial(mm, dproj_ret, w_ret, mode="nn", m=s, n=1024, k=6144, tn=1024, tk=1024, out_dtype=BF16, name="du_ret")
    if on_mesh:
        du_ret, (slots["w_in"],) = du_ret(carry=_Exchange([blocks(grads["w_in"])], [True]))
    else:
        du_ret = du_ret()
    du_gate = mm(dproj_gate, w_gate, mode="nn", m=s, n=1024, k=2048, tn=1024, tk=2048, out_dtype=BF16, name="du_gate")
    du_dil = [mm(dqkv[g], w_dil[g], mode="nn", m=s, n=1024, k=1536, tn=1024, tk=1536, out_dtype=BF16, name="du_dil%d" % g)
              for g in range(3)]

    grad_x, dg_pre_mix = _grad_x(xs, d_h1, (du_ret, du_gate, du_dil[0]), du_dil[1], du_dil[2], vec["g_pre_mix"], s)

    zero = jnp.zeros((1, D_MODEL), F32)
    packet = jnp.concatenate([dg_pre_mix, dg_post_mix, dg_pre_mlp, dg_post_mlp, dg_pre_ple, db_ple, dg_post_ple, loss,
                              db0, db1] + [zero] * 6, axis=0)
    return grad_x, (slots if on_mesh else grads), packet


def _mesh_pos():
    return lax.axis_index("x"), lax.axis_index("y"), lax.axis_index("c")


class _Exchange:
    def __init__(self, arrays, scatter):
        self.arrays, self.scatter, self.n = list(arrays), list(scatter), len(arrays)
        self.out_shape = [jax.ShapeDtypeStruct(a.shape if sc else (N_DEV,) + a.shape, a.dtype)
                          for a, sc in zip(self.arrays, self.scatter)]
        self.scratch = [pltpu.SemaphoreType.DMA((self.n * 7,)), pltpu.SemaphoreType.DMA((self.n * 7,)),
                        pltpu.SemaphoreType.DMA((self.n,))]
        self.specs = [pl.BlockSpec(memory_space=pl.ANY)] * self.n

    def _copies(self, srcs, dsts, sems):
        send_sems, recv_sems, local_sems = sems
        x, y, c = _mesh_pos()
        my = 4 * x + 2 * y + c
        src_of = lambda w, idx: srcs[w].at[idx] if self.scatter[w] else srcs[w]
        local = [pltpu.make_async_copy(src_of(w, my), dsts[w].at[my], local_sems.at[w]) for w in range(self.n)]
        sends, recvs = [], []
        for w in range(self.n):
            for r in range(1, N_DEV):
                px = 1 - x if r & 4 else x
                py = 1 - y if r & 2 else y
                pc = 1 - c if r & 1 else c
                pidx = 4 * px + 2 * py + pc
                kw = dict(send_sem=send_sems.at[w * 7 + r - 1], recv_sem=recv_sems.at[w * 7 + r - 1],
                          device_id=(px, py, pc), device_id_type=MESH)
                sends.append(pltpu.make_async_remote_copy(src_ref=src_of(w, pidx), dst_ref=dsts[w].at[my], **kw))
                recvs.append(pltpu.make_async_remote_copy(src_ref=src_of(w, pidx), dst_ref=dsts[w].at[pidx], **kw))
        return local, sends, recvs

    def start(self, srcs, dsts, sems):
        local, sends, _ = self._copies(srcs, dsts, sems)
        for cp in local + sends:
            cp.start()

    def wait(self, srcs, dsts, sems):
        local, sends, recvs = self._copies(srcs, dsts, sems)
        for cp in recvs:
            cp.wait_recv()
        for cp in sends:
            cp.wait_send()
        for cp in local:
            cp.wait()

    def split(self, refs, n_in, n_out):
        srcs = refs[n_in:n_in + self.n]
        dsts = refs[n_in + self.n + n_out:n_in + 2 * self.n + n_out]
        return srcs, dsts, refs[len(refs) - 3:]


class _TwoLevelGather(_Exchange):
    def __init__(self, arrays):
        super().__init__(arrays, [False] * len(arrays))

    def _plan(self, srcs, dsts, sems):
        send_sems, recv_sems, local_sems = sems
        x, y, c = _mesh_pos()
        me, sibling = (x, y, c), (x, y, 1 - c)
        chips = [(1 - x, y), (x, 1 - y), (1 - x, 1 - y)]
        region = lambda w, dev: dsts[w].at[4 * dev[0] + 2 * dev[1] + dev[2]]

        def copy(w, kk, block, to, src=None):
            return pltpu.make_async_remote_copy(
                src_ref=region(w, block) if src is None else src, dst_ref=region(w, block),
                send_sem=send_sems.at[w * 7 + kk], recv_sem=recv_sems.at[w * 7 + kk], device_id=to, device_id_type=MESH)

        mine = [pltpu.make_async_copy(srcs[w], region(w, me), local_sems.at[w]) for w in range(self.n)]
        first = []
        for w in range(self.n):
            first.append(copy(w, 0, me, sibling, src=srcs[w]))
            first += [copy(w, 1 + j, me, (*chip, c), src=srcs[w]) for j, chip in enumerate(chips)]
        return me, sibling, chips, c, copy, mine, first

    def start(self, srcs, dsts, sems):
        *_, mine, first = self._plan(srcs, dsts, sems)
        for cp in mine + first:
            cp.start()

    def wait(self, srcs, dsts, sems):
        me, sibling, chips, c, copy, mine, first = self._plan(srcs, dsts, sems)
        passed = []
        for j, chip in enumerate(chips):
            for w in range(self.n):
                copy(w, 1 + j, (*chip, c), me).wait_recv()
                cp = copy(w, 4 + j, (*chip, c), sibling)
                cp.start()
                passed.append(cp)
        for w in range(self.n):
            copy(w, 0, sibling, me).wait_recv()
            for j, chip in enumerate(chips):
                copy(w, 4 + j, (*chip, 1 - c), me).wait_recv()
        for cp in first + passed:
            cp.wait_send()
        for cp in mine:
            cp.wait()


def _run_exchange(ex, name):
    def body(*refs):
        parts = ex.split(refs, 0, 0)
        ex.start(*parts)
        ex.wait(*parts)

    return pl.pallas_call(body, name=name, in_specs=ex.specs, out_specs=ex.specs, out_shape=ex.out_shape,
                          scratch_shapes=ex.scratch)(*ex.arrays)


def _pick_rows(r, c, target_bytes):
    t = r
    while (t // 2) % 16 == 0 and t // 2 >= 16 and t * c * 4 > target_bytes:
        t //= 2
    return t


def _sum_slots(slots, name):
    ns, r, c = slots.shape
    tr = _pick_rows(r, c, 256 * 1024)

    def body(s_ref, o_ref):
        acc = s_ref[0].astype(F32)
        for kk in range(1, ns):
            acc = acc + s_ref[kk].astype(F32)
        o_ref[...] = acc

    return pl.pallas_call(
        body, name=name, grid=(r // tr,),
        in_specs=[pl.BlockSpec((ns, tr, c), lambda i: (0, i, 0))], out_specs=pl.BlockSpec((tr, c), lambda i: (i, 0)),
        out_shape=jax.ShapeDtypeStruct((r, c), F32), compiler_params=_cparams(("parallel",)),
    )(slots)


def _adamw(slots, w, m, v, name):
    ns, r, c = slots.shape
    tr = _pick_rows(r, c, 256 * 1024)

    def body(s_ref, w_ref, m_ref, v_ref, g_out, d_out, m_out, v_out):
        g = s_ref[0].astype(F32)
        for kk in range(1, ns):
            g = g + s_ref[kk].astype(F32)
        mn = ADAM_B1 * m_ref[...] + (1.0 - ADAM_B1) * g
        vn = ADAM_B2 * v_ref[...] + (1.0 - ADAM_B2) * (g * g)
        m_hat = mn / (1.0 - ADAM_B1 ** ADAM_STEP)
        v_hat = vn / (1.0 - ADAM_B2 ** ADAM_STEP)
        g_out[...] = g
        d_out[...] = -ADAM_LR * (m_hat / (jnp.sqrt(v_hat) + ADAM_EPS) + ADAM_WD * w_ref[...])
        m_out[...] = mn
        v_out[...] = vn

    blk = pl.BlockSpec((tr, c), lambda i: (i, 0))
    return pl.pallas_call(
        body, name=name, grid=(r // tr,),
        in_specs=[pl.BlockSpec((ns, tr, c), lambda i: (0, i, 0)), blk, blk, blk], out_specs=[blk] * 4,
        out_shape=[jax.ShapeDtypeStruct((r, c), F32)] * 4, compiler_params=_cparams(("parallel",)),
    )(slots, w, m, v)


def _rotary_tables(pos, s):
    posf = pos.astype(F32)
    inv_freq = 1.0 / (10000.0 ** jnp.linspace(0.0, 1.0, RET_QK // 2, dtype=F32))
    ang = posf[:, None] * inv_freq
    tabs = {"cos_r": jnp.cos(ang), "sin_r": jnp.sin(ang), "dil": []}
    freqs = 500000.0 ** (-jnp.arange(0, 16, 2, dtype=F32) / 16)
    spread = np.zeros((16, 384), np.float32)
    bias = np.zeros((1, 384), np.float32)
    for head in range(2):
        for i in range(8):
            spread[i, 64 * head + i] = spread[i, 64 * head + 8 + i] = 1.0
            spread[8 + i, 128 + 64 * head + i] = -1.0
            spread[8 + i, 256 + 64 * head + 8 + i] = 1.0
        bias[0, 64 * head + 16:64 * head + 64] = 1.0

    def expand(t, e, b):
        hi = t.astype(BF16)
        lo = (t - hi.astype(F32)).astype(BF16)
        out = _dot(hi, e, NN) + _dot(lo, e, NN) + b
        return [out[:, 0:128], out[:, 128:256], out[:, 256:384]], []

    for g, dil in enumerate(DIL_GROUPS):
        ang = posf.reshape(s // dil, dil).T.reshape(s, 1) * freqs
        cs = jnp.concatenate([jnp.cos(ang), jnp.sin(ang)], axis=1)
        t3, _ = _rowwise("rot_tables%d" % g, expand, s, min(1024, s), [(cs, 16, 0)],
                         [jnp.asarray(spread, BF16), jnp.asarray(bias)], [(128, F32)] * 3)
        tabs["dil"].append(tuple(t3))
    return tabs


_TRANSPOSED = ("w_in", "w_dil_out", "w_up", "w_ple_in")
_MATS = ("w_in", "w_ret_out", "w_dil_out", "w_o", "w_up", "w_down", "w_ple_gate", "w_ple_in")
_VECS = ("g_pre_mix", "g_post_mix", "g_pre_mlp", "g_post_mlp", "g_pre_ple", "b_ple_gate", "g_post_ple")
_ORDER = ("w_in", "b_gate", "w_ret_out", "w_dil_out", "w_o", "g_pre_mix", "g_post_mix", "g_pre_mlp", "g_post_mlp", "w_up",
          "w_down", "g_pre_ple", "w_ple_gate", "b_ple_gate", "w_ple_in", "g_post_ple")


def kernel(x, p, positions, w_in, b_gate, w_ret_out, w_dil_out, w_o, g_pre_mix, g_post_mix, g_pre_mlp, g_post_mlp, w_up, w_down, g_pre_ple, w_ple_gate, b_ple_gate, w_ple_in, g_post_ple, loss_target, m_w_in, m_b_gate, m_w_ret_out, m_w_dil_out, m_w_o, m_g_pre_mix, m_g_post_mix, m_g_pre_mlp, m_g_post_mlp, m_w_up, m_w_down, m_g_pre_ple, m_w_ple_gate, m_b_ple_gate, m_w_ple_in, m_g_post_ple, v_w_in, v_b_gate, v_w_ret_out, v_w_dil_out, v_w_o, v_g_pre_mix, v_g_post_mix, v_g_pre_mlp, v_g_post_mlp, v_w_up, v_w_down, v_g_pre_ple, v_w_ple_gate, v_b_ple_gate, v_w_ple_in, v_g_post_ple):
    s = x.shape[1]
    wd = dict(w_in=w_in, b_gate=b_gate, w_ret_out=w_ret_out, w_dil_out=w_dil_out, w_o=w_o, g_pre_mix=g_pre_mix,
              g_post_mix=g_post_mix, g_pre_mlp=g_pre_mlp, g_post_mlp=g_post_mlp, w_up=w_up, w_down=w_down,
              g_pre_ple=g_pre_ple, w_ple_gate=w_ple_gate, b_ple_gate=b_ple_gate, w_ple_in=w_ple_in, g_post_ple=g_post_ple)
    md = dict(w_in=m_w_in, b_gate=m_b_gate, w_ret_out=m_w_ret_out, w_dil_out=m_w_dil_out, w_o=m_w_o, g_pre_mix=m_g_pre_mix,
              g_post_mix=m_g_post_mix, g_pre_mlp=m_g_pre_mlp, g_post_mlp=m_g_post_mlp, w_up=m_w_up, w_down=m_w_down,
              g_pre_ple=m_g_pre_ple, w_ple_gate=m_w_ple_gate, b_ple_gate=m_b_ple_gate, w_ple_in=m_w_ple_in, g_post_ple=m_g_post_ple)
    vd = dict(w_in=v_w_in, b_gate=v_b_gate, w_ret_out=v_w_ret_out, w_dil_out=v_w_dil_out, w_o=v_w_o, g_pre_mix=v_g_pre_mix,
              g_post_mix=v_g_post_mix, g_pre_mlp=v_g_pre_mlp, g_post_mlp=v_g_post_mlp, w_up=v_w_up, w_down=v_w_down,
              g_pre_ple=v_g_pre_ple, w_ple_gate=v_w_ple_gate, b_ple_gate=v_b_ple_gate, w_ple_in=v_w_ple_in, g_post_ple=v_g_post_ple)

    shards = {n: (wd[n][0].T if n in _TRANSPOSED else wd[n][0]).astype(BF16) for n in _MATS}
    shards["b_gate"] = b_gate[0]
    vec = {n: wd[n] for n in _VECS}
    vec["b_ple"] = b_ple_gate

    tabs = _rotary_tables(positions[0], s)
    grad_x, slots, packet = _local_step(x[0], p[0, 0].astype(BF16), loss_target[0], tabs, {}, vec, s, shards=shards)

    (packets,) = _run_exchange(_Exchange([packet], [False]), "exchange_vectors")
    out = {}
    for n in _MATS:
        sl = slots[n]
        if n in _TRANSPOSED:
            sl = _sum_slots(sl, "sum_" + n).T[None]
        out[n] = _adamw(sl, wd[n][0], md[n][0], vd[n][0], "adamw_" + n)
    zero_rows = jnp.zeros((16 - len(_VECS), D_MODEL), F32)
    pack = lambda d: jnp.concatenate([d[n] for n in _VECS] + [zero_rows], axis=0)
    small = _adamw(packets, pack(wd), pack(md), pack(vd), "adamw_vectors")
    for i, n in enumerate(_VECS):
        out[n] = tuple(t[i:i + 1] for t in small)
    my = 4 * lax.axis_index("x") + 2 * lax.axis_index("y") + lax.axis_index("c")
    g_bias = lax.dynamic_slice(small[0], (8, my * 128), (2, 128))
    out["b_gate"] = _adamw(g_bias[None], b_gate[0], m_b_gate[0], v_b_gate[0], "adamw_b_gate")
    loss = small[0][7, 0]

    res = [loss, grad_x[None]]
    for kk in range(4):
        res += [out[n][kk][None] if out[n][kk].ndim == 2 and wd[n].ndim == 3 else out[n][kk] for n in _ORDER]
    return tuple(res)
```

```python
import functools
import math

import numpy as np
import jax
import jax.numpy as jnp
from jax import lax
from jax.experimental import pallas as pl
from jax.experimental.pallas import tpu as pltpu

F32, BF16 = jnp.float32, jnp.bfloat16
D_MODEL = 1024
EPS = 1e-6
N_DEV = 8
RET_HEADS, RET_QK, RET_V, RET_CHUNK = 4, 256, 512, 128
DIL_GROUPS = (1, 4, 16)
DIL_W = 512
QB = 128
NEG = -1e30
ADAM_LR, ADAM_B1, ADAM_B2, ADAM_EPS, ADAM_WD, ADAM_STEP = 0.001, 0.9, 0.999, 1e-08, 0.01, 10
VMEM_LIMIT_BYTES = 56 * 1024 * 1024
MESH = pl.DeviceIdType.MESH

NN = ((1,), (0,))
NT = ((1,), (1,))
TN = ((0,), (0,))


def _dot(a, b, dn):
    return lax.dot_general(a, b, (dn, ((), ())), preferred_element_type=F32)


def _cparams(sem):
    return pltpu.CompilerParams(dimension_semantics=sem, vmem_limit_bytes=VMEM_LIMIT_BYTES)


def _rms(x):
    return x * lax.rsqrt(jnp.mean(x * x, axis=-1, keepdims=True) + EPS)


def _rms_bwd(x, g, dy):
    r = lax.rsqrt(jnp.mean(x * x, axis=-1, keepdims=True) + EPS)
    xh = x * r
    t = dy * g
    dx = r * (t - xh * jnp.mean(t * xh, axis=-1, keepdims=True))
    return dx, dy * xh


def _colsum(v):
    return jnp.sum(v, axis=0, keepdims=True)


def _sigmoid(v):
    return 1.0 / (1.0 + jnp.exp(-v))


def _pallas(compute, *, name, grid, in_specs, out_specs, out_shape, scratch, semantics, args, carry=None):
    n_in, n_out = len(in_specs), len(out_specs)
    if carry is None:
        res = pl.pallas_call(compute, name=name, grid=grid, in_specs=in_specs, out_specs=out_specs, out_shape=out_shape,
                             scratch_shapes=scratch, compiler_params=_cparams(semantics))(*args)
        return res, []
    n_steps = math.prod(grid)

    def body(*refs):
        step = 0
        for axis, size in enumerate(grid):
            step = step * size + pl.program_id(axis)
        parts = carry.split(refs, n_in, n_out)
        pl.when(step == 0)(lambda: carry.start(*parts))
        compute(*refs[:n_in], *refs[n_in + carry.n:n_in + carry.n + n_out], *refs[n_in + 2 * carry.n + n_out:len(refs) - 3])
        pl.when(step == n_steps - 1)(lambda: carry.wait(*parts))

    res = pl.pallas_call(
        body, name=name, grid=grid, in_specs=list(in_specs) + carry.specs, out_specs=list(out_specs) + carry.specs,
        out_shape=list(out_shape) + carry.out_shape, scratch_shapes=list(scratch) + carry.scratch,
        compiler_params=_cparams(("arbitrary",) * len(grid)))(*args, *carry.arrays)
    return res[:n_out], res[n_out:]


def _matmul(a, b, *, mode, m, n, k, tm, tn, tk, out_dtype, name, a_fn=None, epi=(), epi_width=None, epi_fn=None, carry=None):
    nk = k // tk
    grid = (m // tm, n // tn, nk)
    if mode == "nn":
        a_blk, a_im, b_blk, b_im, dn = (tm, tk), (lambda i, j, kk: (i, kk)), (tk, tn), (lambda i, j, kk: (kk, j)), NN
    elif mode == "nt":
        a_blk, a_im, b_blk, b_im, dn = (tm, tk), (lambda i, j, kk: (i, kk)), (tn, tk), (lambda i, j, kk: (j, kk)), NT
    else:
        a_blk, a_im, b_blk, b_im, dn = (tk, tm), (lambda i, j, kk: (kk, i)), (tk, tn), (lambda i, j, kk: (kk, j)), TN
    o_im = lambda i, j, kk: (i, j)
    n_in = 2 + len(epi)

    def body(*refs):
        a_ref, b_ref = refs[0], refs[1]
        o_ref = refs[n_in]
        acc_ref = refs[n_in + 1] if nk > 1 else None

        def finish(acc):
            if epi:
                acc = epi_fn(acc, *[r[...] for r in refs[2:n_in]])
            o_ref[...] = acc.astype(o_ref.dtype)

        av = a_ref[...]
        if a_fn is not None:
            av = a_fn(av)
        part = _dot(av, b_ref[...], dn)
        if nk == 1:
            finish(part)
        else:
            kk = pl.program_id(2)

            @pl.when(kk == 0)
            def _():
                acc_ref[...] = part

            @pl.when(kk > 0)
            def _():
                acc_ref[...] += part

            @pl.when(kk == nk - 1)
            def _():
                finish(acc_ref[...])

    epi_spec = pl.BlockSpec((tm, tn), o_im) if epi_width is None else pl.BlockSpec((tm, epi_width), lambda i, j, kk: (i, 0))
    in_specs = [pl.BlockSpec(a_blk, a_im), pl.BlockSpec(b_blk, b_im)] + [epi_spec] * len(epi)
    args = [a, b, *epi]
    (out,), got = _pallas(
        body, name=name, grid=grid, in_specs=in_specs, out_specs=[pl.BlockSpec((tm, tn), o_im)],
        out_shape=[jax.ShapeDtypeStruct((m, n), out_dtype)], scratch=[pltpu.VMEM((tm, tn), F32)] if nk > 1 else [],
        semantics=("parallel", "parallel", "arbitrary"), args=args, carry=carry)
    return out if carry is None else (out, got)


def _relu_sq(v):
    r = jnp.maximum(v.astype(F32), 0.0)
    return (r * r).astype(BF16)


def _rowwise(name, fn, s, tr, rows, vecs, outs, accs=()):
    n_r, n_v, n_o, n_a = len(rows), len(vecs), len(outs), len(accs)

    def body(*refs):
        vals = [refs[i][...].astype(F32) for i in range(n_r)] + [refs[n_r + i][...] for i in range(n_v)]
        o_refs = refs[n_r + n_v:n_r + n_v + n_o]
        a_refs = refs[n_r + n_v + n_o:]
        o_vals, a_vals = fn(*vals)
        for ref, val in zip(o_refs, o_vals):
            ref[...] = val.astype(ref.dtype)
        if n_a:
            @pl.when(pl.program_id(0) == 0)
            def _():
                for ref in a_refs:
                    ref[...] = jnp.zeros_like(ref)

            for ref, val in zip(a_refs, a_vals):
                ref[...] += val

    in_specs = [pl.BlockSpec((tr, w), functools.partial(lambda i, cb: (i, cb), cb=cb)) for _, w, cb in rows]
    in_specs += [pl.BlockSpec(v.shape, lambda i: (0, 0)) for v in vecs]
    out_specs = [pl.BlockSpec((tr, w), lambda i: (i, 0)) for w, _ in outs]
    out_specs += [pl.BlockSpec((1, w), lambda i: (0, 0)) for w in accs]
    out_shape = [jax.ShapeDtypeStruct((s, w), dt) for w, dt in outs]
    out_shape += [jax.ShapeDtypeStruct((1, w), F32) for w in accs]
    res = pl.pallas_call(
        body, name=name, grid=(s // tr,), in_specs=in_specs, out_specs=out_specs, out_shape=out_shape,
        compiler_params=_cparams(("arbitrary",)),
    )(*[r[0] for r in rows], *vecs)
    return res[:n_o], res[n_o:]


_ROW_TILE = 512
_STREAM_SPECS = [pl.BlockSpec((dil, _ROW_TILE // dil, D_MODEL), lambda i: (0, i, 0)) for dil in DIL_GROUPS[1:]]
_NAT_SPEC = pl.BlockSpec((_ROW_TILE, D_MODEL), lambda i: (i, 0))
_VEC_SPEC = pl.BlockSpec((1, D_MODEL), lambda i: (0, 0))
_COL_BLOCKS = pltpu.VMEM((D_MODEL // 128, _ROW_TILE, 128), F32)


def _prenorm(xs, g, s, carry=None):
    tr = _ROW_TILE

    def body(x_ref, g_ref, u_ref, u4_ref, u16_ref, buf):
        xn = _rms(x_ref[...]) * g_ref[...]
        u_ref[...] = xn.astype(BF16)
        for cb in range(8):
            buf[cb] = xn[:, cb * 128:(cb + 1) * 128]
        for dil, out in ((4, u4_ref), (16, u16_ref)):
            for c in range(dil):
                rows = pl.ds(c, tr // dil, stride=dil)
                out[c] = jnp.concatenate([buf.at[cb][rows, :] for cb in range(8)], axis=1).astype(BF16)

    res, got = _pallas(
        body, name="prenorm", grid=(s // tr,), in_specs=[_NAT_SPEC, _VEC_SPEC], out_specs=[_NAT_SPEC] + _STREAM_SPECS,
        out_shape=[jax.ShapeDtypeStruct((s, D_MODEL), BF16)]
        + [jax.ShapeDtypeStruct((dil, s // dil, D_MODEL), BF16) for dil in DIL_GROUPS[1:]],
        scratch=[_COL_BLOCKS], semantics=("parallel",), args=(xs, g), carry=carry)
    return [r.reshape(s, D_MODEL) for r in res], got


def _grad_x(xs, d_h1, du_nat, du4, du16, g, s):
    tr = _ROW_TILE

    def body(x_ref, dh_ref, a_ref, b_ref, c_ref, u4_ref, u16_ref, g_ref, dx_ref, dg_ref, buf):
        du = a_ref[...].astype(F32) + b_ref[...].astype(F32) + c_ref[...].astype(F32)
        for dil, src in ((4, u4_ref), (16, u16_ref)):
            for c in range(dil):
                part = src[c].astype(F32)
                for cb in range(8):
                    buf.at[cb][pl.ds(c, tr // dil, stride=dil), :] = part[:, cb * 128:(cb + 1) * 128]
            du = du + jnp.concatenate([buf[cb] for cb in range(8)], axis=1)
        dx, dgr = _rms_bwd(x_ref[...], g_ref[...], du)
        dx_ref[...] = dh_ref[...] + dx

        @pl.when(pl.program_id(0) == 0)
        def _():
            dg_ref[...] = jnp.zeros_like(dg_ref)

        dg_ref[...] += _colsum(dgr)

    return pl.pallas_call(
        body, name="grad_x", grid=(s // tr,), in_specs=[_NAT_SPEC] * 5 + _STREAM_SPECS + [_VEC_SPEC],
        out_specs=[_NAT_SPEC, _VEC_SPEC],
        out_shape=[jax.ShapeDtypeStruct((s, D_MODEL), F32), jax.ShapeDtypeStruct((1, D_MODEL), F32)],
        scratch_shapes=[_COL_BLOCKS], compiler_params=_cparams(("arbitrary",)),
    )(xs, d_h1, *du_nat, du4.reshape(4, s // 4, D_MODEL), du16.reshape(16, s // 16, D_MODEL), g)


def _ret_tables():
    h = np.arange(RET_HEADS, dtype=np.float32)
    lg = np.log1p(-(np.float32(2.0) ** (-5.0 - h))).astype(np.float32)
    idx = np.arange(RET_CHUNK, dtype=np.float32)
    diff = idx[:, None] - idx[None, :]
    dm = np.where(diff[None] >= 0, np.exp(np.maximum(diff, 0.0)[None] * lg[:, None, None]), 0.0)
    qd = np.exp((idx + 1.0)[None, :, None] * lg[:, None, None])
    kd = np.exp((RET_CHUNK - 1.0 - idx)[None, :, None] * lg[:, None, None])
    cd = np.exp(RET_CHUNK * lg)[:, None, None]
    return [jnp.asarray(t, F32) for t in (dm, qd, kd, cd)]


def _rope_half(v, cos, sin):
    v1, v2 = v[:, :128], v[:, 128:]
    return jnp.concatenate([v1 * cos - v2 * sin, v2 * cos + v1 * sin], axis=1)


def _unrope_half(d, cos, sin):
    d1, d2 = d[:, :128], d[:, 128:]
    return jnp.concatenate([d1 * cos + d2 * sin, d2 * cos - d1 * sin], axis=1)


_RET_HEADS_FWD, _RET_HEADS_BWD = 1, 2


def _ret_specs(rb, rev_n, hp):
    def rowmap(w_blk):
        return lambda h, n: (rev_n(n), w_blk(h))
    tab = [pl.BlockSpec((hp, RET_CHUNK, RET_CHUNK), lambda h, n: (h, 0, 0)),
           pl.BlockSpec((hp, RET_CHUNK, 1), lambda h, n: (h, 0, 0)),
           pl.BlockSpec((hp, RET_CHUNK, 1), lambda h, n: (h, 0, 0)),
           pl.BlockSpec((hp, 1, 1), lambda h, n: (h, 0, 0))]
    proj = pl.BlockSpec((rb, hp * 1536), rowmap(lambda h: h))
    cs = pl.BlockSpec((rb, 128), rowmap(lambda h: 0))
    hv = pl.BlockSpec((rb, hp * RET_V), rowmap(lambda h: h))
    return proj, cs, hv, tab


def _ret_fwd(proj_ret, cos, sin, s, carry=None):
    rb = min(512, s)
    ch = rb // RET_CHUNK
    nb = s // rb
    hp = _RET_HEADS_FWD
    proj_spec, cs_spec, hv_spec, tab_specs = _ret_specs(rb, lambda n: n, hp)

    def body(p_ref, cos_ref, sin_ref, dm_ref, qd_ref, kd_ref, cd_ref, yr_ref, y_ref, rs_ref, r_acc):
        @pl.when(pl.program_id(1) == 0)
        def _():
            r_acc[...] = jnp.zeros_like(r_acc)

        for c, hh in [(c, hh) for c in range(ch) for hh in range(hp)]:
            rows = slice(c * RET_CHUNK, (c + 1) * RET_CHUNK)
            pc, hc = hh * 1536, hh * RET_V
            dm, qd, kd, cd = dm_ref[hh], qd_ref[hh], kd_ref[hh], cd_ref[hh]
            cosv, sinv = cos_ref[rows, :], sin_ref[rows, :]
            q = _rope_half(p_ref[rows, pc:pc + 256].astype(F32), cosv, sinv)
            kk = _rope_half(p_ref[rows, pc + 256:pc + 512].astype(F32), cosv, sinv) * (RET_QK ** -0.5)
            v = p_ref[rows, pc + 512:pc + 1024]
            g = p_ref[rows, pc + 1024:pc + 1536].astype(F32)
            rb16 = r_acc[hh].astype(BF16)
            rs_ref[hh, c] = rb16
            sc = _dot(q.astype(BF16), kk.astype(BF16), NT) * dm
            y = _dot(sc.astype(BF16), v, NN) + _dot((q * qd).astype(BF16), rb16, NN)
            r_acc[hh] = r_acc[hh] * cd + _dot((kk * kd).astype(BF16), v, TN)
            y_ref[rows, hc:hc + RET_V] = y.astype(BF16)
            yr_ref[rows, hc:hc + RET_V] = (_rms(y) * (g * _sigmoid(g))).astype(BF16)

    return _pallas(
        body, name="ret_fwd", grid=(RET_HEADS // hp, nb),
        in_specs=[proj_spec, cs_spec, cs_spec] + tab_specs,
        out_specs=[hv_spec, hv_spec, pl.BlockSpec((hp, ch, RET_QK, RET_V), lambda h, n: (h, n, 0, 0))],
        out_shape=[jax.ShapeDtypeStruct((s, RET_HEADS * RET_V), BF16), jax.ShapeDtypeStruct((s, RET_HEADS * RET_V), BF16),
                   jax.ShapeDtypeStruct((RET_HEADS, s // RET_CHUNK, RET_QK, RET_V), BF16)],
        scratch=[pltpu.VMEM((hp, RET_QK, RET_V), F32)], semantics=("parallel", "arbitrary"),
        args=(proj_ret, cos, sin, *_ret_tables()), carry=carry)


def _ret_bwd(proj_ret, cos, sin, y, d_yr, rs, s, carry=None):
    rb = min(512, s)
    ch = rb // RET_CHUNK
    nb = s // rb
    hp = _RET_HEADS_BWD
    proj_spec, cs_spec, hv_spec, tab_specs = _ret_specs(rb, lambda n: nb - 1 - n, hp)

    def body(p_ref, cos_ref, sin_ref, y_ref, dyr_ref, rs_ref, dm_ref, qd_ref, kd_ref, cd_ref, o_ref, dr_acc):
        @pl.when(pl.program_id(1) == 0)
        def _():
            dr_acc[...] = jnp.zeros_like(dr_acc)

        for c, hh in [(c, hh) for c in reversed(range(ch)) for hh in range(hp)]:
            rows = slice(c * RET_CHUNK, (c + 1) * RET_CHUNK)
            pc, hc = hh * 1536, hh * RET_V
            dm, qd, kd, cd = dm_ref[hh], qd_ref[hh], kd_ref[hh], cd_ref[hh]
            cosv, sinv = cos_ref[rows, :], sin_ref[rows, :]
            q = _rope_half(p_ref[rows, pc:pc + 256].astype(F32), cosv, sinv)
            kk = _rope_half(p_ref[rows, pc + 256:pc + 512].astype(F32), cosv, sinv) * (RET_QK ** -0.5)
            v = p_ref[rows, pc + 512:pc + 1024]
            g = p_ref[rows, pc + 1024:pc + 1536].astype(F32)
            yv = y_ref[rows, hc:hc + RET_V].astype(F32)
            dyr = dyr_ref[rows, hc:hc + RET_V].astype(F32)
            sg = _sigmoid(g)
            r = lax.rsqrt(jnp.mean(yv * yv, axis=-1, keepdims=True) + EPS)
            yn = yv * r
            dg = dyr * yn * (sg * (1.0 + g * (1.0 - sg)))
            dyn = dyr * (g * sg)
            dy = (r * (dyn - yn * jnp.mean(dyn * yn, axis=-1, keepdims=True))).astype(BF16)
            qb, kb = q.astype(BF16), kk.astype(BF16)
            rb16 = rs_ref[hh, c]
            drb = dr_acc[hh].astype(BF16)
            sd = _dot(qb, kb, NT) * dm
            ds = (_dot(dy, v, NT) * dm).astype(BF16)
            dq = _dot(ds, kb, NN) + qd * _dot(dy, rb16, NT)
            dk = _dot(ds, qb, TN) + kd * _dot(v, drb, NT)
            dv = _dot(sd.astype(BF16), dy, TN) + _dot((kk * kd).astype(BF16), drb, NN)
            dr_acc[hh] = dr_acc[hh] * cd + _dot((q * qd).astype(BF16), dy, TN)
            o_ref[rows, pc:pc + 256] = _unrope_half(dq, cosv, sinv).astype(BF16)
            o_ref[rows, pc + 256:pc + 512] = (_unrope_half(dk, cosv, sinv) * (RET_QK ** -0.5)).astype(BF16)
            o_ref[rows, pc + 512:pc + 1024] = dv.astype(BF16)
            o_ref[rows, pc + 1024:pc + 1536] = dg.astype(BF16)

    in_specs = [proj_spec, cs_spec, cs_spec, hv_spec, hv_spec,
                pl.BlockSpec((hp, ch, RET_QK, RET_V), lambda h, n: (h, nb - 1 - n, 0, 0))] + tab_specs
    return _pallas(
        body, name="ret_bwd", grid=(RET_HEADS // hp, nb), in_specs=in_specs, out_specs=[proj_spec],
        out_shape=[jax.ShapeDtypeStruct((s, RET_HEADS * 1536), BF16)], scratch=[pltpu.VMEM((hp, RET_QK, RET_V), F32)],
        semantics=("parallel", "arbitrary"), args=(proj_ret, cos, sin, y, d_yr, rs, *_ret_tables()), carry=carry)


def _rope_qk(acc, c, s1, s2):
    outs = []
    for cc in range(8):
        vv = acc[:, cc * 128:(cc + 1) * 128]
        outs.append(vv * c + pltpu.roll(vv, 120, 1) * s1 + pltpu.roll(vv, 8, 1) * s2)
    return jnp.concatenate(outs + [acc[:, 2 * DIL_W:]], axis=1)


def _pair_masks(keys_on_rows=False):
    ri = lax.broadcasted_iota(jnp.int32, (2 * QB, 2 * QB), 1 if keys_on_rows else 0)
    ci = lax.broadcasted_iota(jnp.int32, (2 * QB, 2 * QB), 0 if keys_on_rows else 1)
    e = ci - (ri & (QB - 1))
    lane_lo = lax.broadcasted_iota(jnp.int32, (2 * QB, 128), 1) < 64
    return ci, jnp.logical_and(e >= 0, e <= QB), lane_lo


def _stack_heads(v, lane_lo):
    z = jnp.zeros_like(v)
    return jnp.concatenate([jnp.where(lane_lo, v, z), jnp.where(lane_lo, z, v)], axis=0)


def _dil_fwd(qkv, dil, s, name):
    length = s // dil
    rb = min(512, length)
    nsub = rb // QB
    nbs = length // rb
    sub_per = rb // QB

    def body(q_ref, k_ref, v_ref, kp_ref, vp_ref, o_ref, l_ref):
        first = (pl.program_id(0) % nbs) == 0
        ci, band, lane_lo = _pair_masks()
        lo1 = lane_lo[0:QB]

        for i in range(nsub):
            rows = slice(i * QB, (i + 1) * QB)
            mask = jnp.logical_and(band, ci >= jnp.where(first, QB, 0)) if i == 0 else band
            for j in range(4):
                lanes = slice(j * 128, (j + 1) * 128)
                q2 = _stack_heads(q_ref[rows, lanes], lo1)
                if i == 0:
                    k2 = jnp.concatenate([kp_ref[:, lanes], k_ref[rows, lanes]], axis=0)
                    v2 = jnp.concatenate([vp_ref[:, lanes], v_ref[rows, lanes]], axis=0)
                else:
                    k2, v2 = k_ref[(i - 1) * QB:(i + 1) * QB, lanes], v_ref[(i - 1) * QB:(i + 1) * QB, lanes]
                v2 = _stack_heads(v2, lane_lo)
                sc = jnp.where(mask, _dot(q2, k2, NT) * 0.125, NEG)
                m = jnp.max(sc, axis=1, keepdims=True)
                p = jnp.exp(sc - m)
                den = jnp.sum(p, axis=1, keepdims=True)
                pb = p.astype(BF16)
                o = _dot(jnp.concatenate([pb[0:QB], pb[QB:]], axis=1), v2, NN)
                inv = 1.0 / den
                lse = m + jnp.log(den)
                o_ref[rows, lanes] = o * jnp.where(lo1, inv[0:QB], inv[QB:])
                l_ref[rows, lanes] = jnp.where(lo1, lse[0:QB], lse[QB:])

    prev = lambda n: jnp.maximum(n * sub_per - 1, 0)
    cur = lambda cb: (lambda n: (n, cb))
    return pl.pallas_call(
        body, name=name, grid=(s // rb,),
        in_specs=[pl.BlockSpec((rb, DIL_W), cur(0)), pl.BlockSpec((rb, DIL_W), cur(1)), pl.BlockSpec((rb, DIL_W), cur(2)),
                  pl.BlockSpec((QB, DIL_W), lambda n: (prev(n), 1)), pl.BlockSpec((QB, DIL_W), lambda n: (prev(n), 2))],
        out_specs=[pl.BlockSpec((rb, DIL_W), cur(0)), pl.BlockSpec((rb, DIL_W), cur(0))],
        out_shape=[jax.ShapeDtypeStruct((s, DIL_W), F32), jax.ShapeDtypeStruct((s, DIL_W), F32)],
        compiler_params=_cparams(("parallel",)),
    )(qkv, qkv, qkv, qkv, qkv)


def _dil_bwd(qkv, dya, lse, dlt, tc, ts1, ts2, dil, s, name):
    length = s // dil
    rb = min(512, length)
    nsub = rb // QB
    nbs = length // rb
    last_blk = s // QB - 1

    def body(q_ref, k_ref, v_ref, kp_ref, vp_ref, qn_ref, dy_ref, dyn_ref, l_ref, ln_ref, d_ref, dn_ref,
             c_ref, s1_ref, s2_ref, o_ref, dka, dva):
        nl = pl.program_id(0) % nbs
        first, last = nl == 0, nl == nbs - 1
        ci, band, lane_lo = _pair_masks(keys_on_rows=True)
        lo1 = lane_lo[0:QB]

        def unrope(d, rows):
            return d * c_ref[rows, :] + pltpu.roll(d * s1_ref[rows, :], 8, 1) + pltpu.roll(d * s2_ref[rows, :], 120, 1)

        for qi in range(nsub + 1):
            nxt = qi == nsub
            rows = slice((nsub - 1) * QB, nsub * QB) if nxt else slice(qi * QB, (qi + 1) * QB)
            prev_rows = slice((qi - 1) * QB, qi * QB)
            if qi == 0:
                mask = jnp.logical_and(band, ci >= jnp.where(first, QB, 0))
            elif nxt:
                mask = jnp.logical_and(band, ci <= jnp.where(last, -1, QB - 1))[0:QB, :]
            else:
                mask = band
            for j in range(4):
                lanes = slice(j * 128, (j + 1) * 128)
                if nxt:
                    q, do, lv, dl = qn_ref[:, lanes], dyn_ref[:, lanes], ln_ref[:, lanes], dn_ref[:, lanes]
                    k2, v2 = k_ref[prev_rows, lanes], v_ref[prev_rows, lanes]
                else:
                    q, do, lv, dl = q_ref[rows, lanes], dy_ref[rows, lanes], l_ref[rows, lanes], d_ref[rows, lanes]
                    if qi == 0:
                        k2 = jnp.concatenate([kp_ref[:, lanes], k_ref[rows, lanes]], axis=0)
                        v2 = jnp.concatenate([vp_ref[:, lanes], v_ref[rows, lanes]], axis=0)
                    else:
                        k2, v2 = k_ref[(qi - 1) * QB:(qi + 1) * QB, lanes], v_ref[(qi - 1) * QB:(qi + 1) * QB, lanes]
                q2, do2 = _stack_heads(q, lo1), _stack_heads(do, lo1)
                lt, dt = lv.T, dl.T
                lse2 = jnp.concatenate([lt[0:1], lt[64:65]], axis=1)
                dl2 = jnp.concatenate([dt[0:1], dt[64:65]], axis=1)
                sc = _dot(k2, q2, NT) * 0.125
                p = jnp.where(mask, jnp.exp(jnp.minimum(sc - lse2, 0.0)), 0.0)
                ds = (p * (_dot(v2, do2, NT) - dl2) * 0.125).astype(BF16)
                dk2 = _dot(ds, q2, NN)
                dv2 = _dot(p.astype(BF16), do2, NN)
                if qi >= 1:
                    dka[prev_rows, lanes] += dk2[0:QB]
                    dva[prev_rows, lanes] += dv2[0:QB]
                if not nxt:
                    dka[rows, lanes] = dk2[QB:]
                    dva[rows, lanes] = dv2[QB:]
                    dq = _dot(jnp.concatenate([ds[:, 0:QB], ds[:, QB:]], axis=0), _stack_heads(k2, lane_lo), TN)
                    o_ref[rows, lanes] = unrope(dq, rows).astype(BF16)

        for cc in range(4):
            lanes = slice(cc * 128, (cc + 1) * 128)
            o_ref[:, 512 + cc * 128:512 + (cc + 1) * 128] = unrope(dka[:, lanes], slice(None)).astype(BF16)
            o_ref[:, 1024 + cc * 128:1024 + (cc + 1) * 128] = dva[:, lanes].astype(BF16)

    prev = lambda n: jnp.maximum(n * nsub - 1, 0)
    nxt = lambda n: jnp.minimum(n * nsub + nsub, last_blk)
    cur = lambda cb: (lambda n: (n, cb))
    big = lambda cb: pl.BlockSpec((rb, DIL_W), cur(cb))
    small = lambda im: pl.BlockSpec((QB, DIL_W), im)
    tab = pl.BlockSpec((rb, 128), cur(0))
    return pl.pallas_call(
        body, name=name, grid=(s // rb,),
        in_specs=[big(0), big(1), big(2), small(lambda n: (prev(n), 1)), small(lambda n: (prev(n), 2)),
                  small(lambda n: (nxt(n), 0)), big(0), small(lambda n: (nxt(n), 0)), big(0), small(lambda n: (nxt(n), 0)),
                  big(0), small(lambda n: (nxt(n), 0)), tab, tab, tab],
        out_specs=pl.BlockSpec((rb, 3 * DIL_W), cur(0)),
        out_shape=jax.ShapeDtypeStruct((s, 3 * DIL_W), BF16),
        scratch_shapes=[pltpu.VMEM((rb, DIL_W), F32), pltpu.VMEM((rb, DIL_W), F32)],
        compiler_params=_cparams(("parallel",)),
    )(qkv, qkv, qkv, qkv, qkv, qkv, dya, dya, lse, lse, dlt, dlt, tc, ts1, ts2)


def _stream_specs(tr):
    nat = pl.BlockSpec((tr, 128), lambda i, j: (i, j))
    return [nat] + [pl.BlockSpec((dil, tr // dil, 128), lambda i, j: (0, i, j)) for dil in DIL_GROUPS[1:]]


def _dil_merge(o_g, l_g, s):
    tr = min(2048, s)
    nat, sp4, sp16 = _stream_specs(tr)

    def body(o0_ref, l0_ref, o1_ref, l1_ref, o2_ref, l2_ref, ya_ref, lse_ref, o1n, l1n, o2n, l2n):
        for src, dst, dil in ((o1_ref, o1n, 4), (l1_ref, l1n, 4), (o2_ref, o2n, 16), (l2_ref, l2n, 16)):
            for c in range(dil):
                dst[pl.ds(c, tr // dil, stride=dil), :] = src[c]
        l0, l1, l2 = l0_ref[...], l1n[...], l2n[...]
        m = jnp.maximum(jnp.maximum(l0, l1), l2)
        e0, e1, e2 = jnp.exp(l0 - m), jnp.exp(l1 - m), jnp.exp(l2 - m)
        den = e0 + e1 + e2
        ya_ref[...] = ((e0 * o0_ref[...] + e1 * o1n[...] + e2 * o2n[...]) / den).astype(BF16)
        lse_ref[...] = m + jnp.log(den)

    v3 = lambda a, dil: a.reshape(dil, s // dil, DIL_W)
    return pl.pallas_call(
        body, name="dil_merge", grid=(s // tr, 4),
        in_specs=[nat, nat, sp4, sp4, sp16, sp16], out_specs=[nat, nat],
        out_shape=[jax.ShapeDtypeStruct((s, DIL_W), BF16), jax.ShapeDtypeStruct((s, DIL_W), F32)],
        scratch_shapes=[pltpu.VMEM((tr, 128), F32)] * 4,
        compiler_params=_cparams(("parallel", "parallel")),
    )(o_g[0], l_g[0], v3(o_g[1], 4), v3(l_g[1], 4), v3(o_g[2], 16), v3(l_g[2], 16))


def _dil_bwd_prep(d_ya, ya, lse, s):
    tr = min(2048, s)
    nat, sp4, sp16 = _stream_specs(tr)

    def body(dya_ref, ya_ref, lse_ref, dy0, dl0, dy1, ls1, dl1, dy2, ls2, dl2, dlt):
        lane_lo = lax.broadcasted_iota(jnp.int32, (tr, 128), 1) < 64
        prod = dya_ref[...] * ya_ref[...].astype(F32)
        lo = jnp.where(lane_lo, prod, 0.0)
        dlt[...] = jnp.where(lane_lo, jnp.sum(lo, axis=1, keepdims=True), jnp.sum(prod - lo, axis=1, keepdims=True))
        dy0[...] = dya_ref[...].astype(BF16)
        dl0[...] = dlt[...]
        for dil, dy, ls, dl in ((4, dy1, ls1, dl1), (16, dy2, ls2, dl2)):
            for c in range(dil):
                rows = pl.ds(c, tr // dil, stride=dil)
                dy[c] = dya_ref[rows, :].astype(BF16)
                ls[c] = lse_ref[rows, :]
                dl[c] = dlt[rows, :]

    sh = lambda dil, dt: jax.ShapeDtypeStruct((dil, s // dil, DIL_W), dt)
    res = pl.pallas_call(
        body, name="dil_bwd_prep", grid=(s // tr, 4),
        in_specs=[nat, nat, nat], out_specs=[nat, nat, sp4, sp4, sp4, sp16, sp16, sp16],
        out_shape=[jax.ShapeDtypeStruct((s, DIL_W), BF16), jax.ShapeDtypeStruct((s, DIL_W), F32),
                   sh(4, BF16), sh(4, F32), sh(4, F32), sh(16, BF16), sh(16, F32), sh(16, F32)],
        scratch_shapes=[pltpu.VMEM((tr, 128), F32)],
        compiler_params=_cparams(("parallel", "parallel")),
    )(d_ya, ya, lse)
    dy0, dl0, dy1, ls1, dl1, dy2, ls2, dl2 = [r.reshape(s, DIL_W) for r in res]
    return [(dy0, lse, dl0), (dy1, ls1, dl1), (dy2, ls2, dl2)]


_RET_SEGS = ((0, 256), (1024, 256), (2048, 512), (4096, 512))


def _split_w_in(win):
    per_head = [win[a:a + RET_HEADS * n].reshape(RET_HEADS, n, D_MODEL) for a, n in _RET_SEGS]
    w_ret = jnp.concatenate(per_head, axis=1).reshape(RET_HEADS * 1536, D_MODEL)
    w_dil = [jnp.concatenate([win[a + DIL_W * g:a + DIL_W * (g + 1)] for a in (6144, 7680, 9216)], axis=0) for g in range(3)]
    return w_ret, win[10752:12800], w_dil


def _join_w_in(g_ret, g_gate, g_dil):
    g_ret = g_ret.reshape(RET_HEADS, 1536, D_MODEL)
    off = (0, 256, 512, 1024, 1536)
    parts = [g_ret[:, off[i]:off[i + 1]].reshape(-1, D_MODEL) for i in range(4)]
    dil = [g_dil[g][DIL_W * i:DIL_W * (i + 1)] for i in range(3) for g in range(3)]
    return jnp.concatenate(parts + dil + [g_gate], axis=0)


def _local_step(xs, pb, tgt, tabs, wts, vec, s, shards=None):
    tm = min(2048, s)
    tr = min(512, s)
    mm = functools.partial(_matmul, tm=tm)
    on_mesh = shards is not None
    wts, vec = dict(wts), dict(vec)
    blocks = lambda g: g.reshape(N_DEV, g.shape[0] // N_DEV, g.shape[1])

    late_shards = dict(shards) if on_mesh else {}
    first = _TwoLevelGather([late_shards.pop("w_in"), late_shards.pop("b_gate")]) if on_mesh else None
    u, gathered = _prenorm(xs, vec["g_pre_mix"], s, carry=first)
    if on_mesh:
        wts["w_in"] = gathered[0].reshape(N_DEV * gathered[0].shape[1], D_MODEL)
        bias = gathered[1].transpose(1, 0, 2).reshape(2, D_MODEL)
        vec.update(b0=bias[0:1], b1=bias[1:2])
    w_ret, w_gate, w_dil = _split_w_in(wts["w_in"])
    proj_ret = mm(u[0], w_ret, mode="nt", m=s, n=6144, k=1024, tn=1024, tk=1024, out_dtype=BF16, name="inproj_ret")
    proj_gate = mm(u[0], w_gate, mode="nt", m=s, n=2048, k=1024, tn=1024, tk=1024, out_dtype=BF16, name="inproj_gate")
    qkv = [_matmul(u[g], w_dil[g], mode="nt", m=s, n=1536, k=1024, tm=min(1024, s), tn=1536, tk=1024, out_dtype=BF16,
                   name="inproj_dil%d" % g, epi=tabs["dil"][g], epi_width=128, epi_fn=_rope_qk) for g in range(3)]

    names = list(late_shards) if on_mesh else []
    gather = _TwoLevelGather([late_shards[n] for n in names]) if on_mesh else None
    (yr, y_ret, rstate), gathered = _ret_fwd(proj_ret, tabs["cos_r"], tabs["sin_r"], s, carry=gather)
    wts.update({n: g.reshape(N_DEV * g.shape[1], g.shape[2]) for n, g in zip(names, gathered)})
    a_br = mm(yr, wts["w_ret_out"], mode="nn", m=s, n=1024, k=2048, tn=1024, tk=2048, out_dtype=BF16, name="ret_out")

    o_g, l_g = [], []
    for g, dil in enumerate(DIL_GROUPS):
        o, l = _dil_fwd(qkv[g], dil, s, "dil_fwd%d" % g)
        o_g.append(o)
        l_g.append(l)
    ya, lse = _dil_merge(o_g, l_g, s)
    b_br = mm(ya, wts["w_dil_out"], mode="nt", m=s, n=1024, k=512, tn=1024, tk=512, out_dtype=BF16, name="dil_out")

    def gate_mix(a, b, gr, ga, b0, b1):
        return [_sigmoid(gr.astype(F32) + b0) * a.astype(F32) + _sigmoid(ga.astype(F32) + b1) * b.astype(F32)], []

    (mixed,), _ = _rowwise("gate_mix", gate_mix, s, tr, [(a_br, 1024, 0), (b_br, 1024, 0), (proj_gate, 1024, 0), (proj_gate, 1024, 1)],
                           [vec["b0"], vec["b1"]], [(1024, BF16)])
    z = mm(mixed, wts["w_o"], mode="nn", m=s, n=1024, k=1024, tn=1024, tk=1024, out_dtype=BF16, name="w_o")

    def post_norm(h, f, g_post, g_pre):
        hn = h + _rms(f) * g_post
        return [hn, _rms(hn) * g_pre], []

    (h1, v2), _ = _rowwise("post_mix", post_norm, s, tr, [(xs, 1024, 0), (z, 1024, 0)], [vec["g_post_mix"], vec["g_pre_mlp"]],
                           [(1024, F32), (1024, BF16)])
    a_up = mm(v2, wts["w_up"], mode="nt", m=s, n=4096, k=1024, tn=1024, tk=1024, out_dtype=BF16, name="mlp_up")
    f_dn = mm(a_up, wts["w_down"], mode="nn", m=s, n=1024, k=4096, tn=1024, tk=1024, out_dtype=BF16, name="mlp_down", a_fn=_relu_sq)
    (h2, t_ple), _ = _rowwise("post_mlp", post_norm, s, tr, [(h1, 1024, 0), (f_dn, 1024, 0)], [vec["g_post_mlp"], vec["g_pre_ple"]],
                              [(1024, F32), (1024, BF16)])
    gl = mm(t_ple, wts["w_ple_gate"], mode="nn", m=s, n=1024, k=1024, tn=1024, tk=1024, out_dtype=BF16, name="ple_gate")
    e_ple = mm(pb, wts["w_ple_in"], mode="nt", m=s, n=1024, k=256, tn=1024, tk=256, out_dtype=BF16, name="ple_in")

    def ple_loss(h, glv, e, tg, b, g):
        gate = _sigmoid(glv + b)
        ge = gate * e
        diff = h + _rms(ge) * g - tg
        dy = diff * (1.0 / D_MODEL)
        d_ge, dg = _rms_bwd(ge, g, dy)
        d_gl = d_ge * e * gate * (1.0 - gate)
        loss = jnp.zeros((1, D_MODEL), F32) + 0.5 * jnp.sum(diff * diff) * (1.0 / D_MODEL)
        return [dy, d_gl, d_ge * gate], [_colsum(dg), _colsum(d_gl), loss]

    (dy, d_gl, d_e), (dg_post_ple, db_ple, loss) = _rowwise(
        "ple_loss", ple_loss, s, tr, [(h2, 1024, 0), (gl, 1024, 0), (e_ple, 1024, 0), (tgt, 1024, 0)],
        [vec["b_ple"], vec["g_post_ple"]], [(1024, F32), (1024, BF16), (1024, BF16)], [1024, 1024, 1024])

    ts, ts2 = min(1024, s), min(2048, s)
    wg = functools.partial(_matmul, mode="tn", k=s, tk=ts, out_dtype=BF16)
    grads = {}
    grads["w_ple_in"] = wg(d_e, pb, m=1024, n=256, tm=1024, tn=256, tk=ts2, name="g_ple_in")
    grads["w_ple_gate"] = wg(t_ple, d_gl, m=1024, n=1024, tm=1024, tn=1024, tk=ts2, name="g_ple_gate")
    d_t = mm(d_gl, wts["w_ple_gate"], mode="nt", m=s, n=1024, k=1024, tn=1024, tk=1024, out_dtype=BF16, name="d_t")

    def bwd_ple_mlp(h, dt, dyv, f, g_pre, g_post):
        dx, dg1 = _rms_bwd(h, g_pre, dt)
        dh = dyv + dx
        df, dg2 = _rms_bwd(f, g_post, dh)
        return [dh, df], [_colsum(dg1), _colsum(dg2)]

    (d_h2, d_f), (dg_pre_ple, dg_post_mlp) = _rowwise(
        "bwd_ple_mlp", bwd_ple_mlp, s, tr, [(h2, 1024, 0), (d_t, 1024, 0), (dy, 1024, 0), (f_dn, 1024, 0)],
        [vec["g_pre_ple"], vec["g_post_mlp"]], [(1024, F32), (1024, BF16)], [1024, 1024])
    d_a = mm(d_f, wts["w_down"], mode="nt", m=s, n=4096, k=1024, tn=1024, tk=1024, out_dtype=BF16, name="d_a",
             epi=(a_up,), epi_fn=lambda acc, av: acc * (2.0 * jnp.maximum(av.astype(F32), 0.0)))
    grads["w_down"] = wg(a_up, d_f, m=4096, n=1024, tm=2048, tn=1024, name="g_down", a_fn=_relu_sq)
    grads["w_up"] = wg(d_a, v2, m=4096, n=1024, tm=2048, tn=1024, tk=ts2, name="g_up")
    d_v2 = mm(d_a, wts["w_up"], mode="nn", m=s, n=1024, k=4096, tn=1024, tk=2048, out_dtype=BF16, name="d_v2")

    (d_h1, d_z), (dg_pre_mlp, dg_post_mix) = _rowwise(
        "bwd_mlp_mix", bwd_ple_mlp, s, tr, [(h1, 1024, 0), (d_v2, 1024, 0), (d_h2, 1024, 0), (z, 1024, 0)],
        [vec["g_pre_mlp"], vec["g_post_mix"]], [(1024, F32), (1024, BF16)], [1024, 1024])
    d_mixed = mm(d_z, wts["w_o"], mode="nt", m=s, n=1024, k=1024, tn=1024, tk=1024, out_dtype=BF16, name="d_mixed")
    grads["w_o"] = wg(mixed, d_z, m=1024, n=1024, tm=1024, tn=1024, tk=ts2, name="g_o")

    def bwd_gate(dm, a, b, gr, ga, b0, b1):
        sa, sb = _sigmoid(gr.astype(F32) + b0), _sigmoid(ga.astype(F32) + b1)
        dgr = dm * a.astype(F32) * sa * (1.0 - sa)
        dga = dm * b.astype(F32) * sb * (1.0 - sb)
        return [dm * sa, dm * sb, jnp.concatenate([dgr, dga], axis=1)], [_colsum(dgr), _colsum(dga)]

    (d_abr, d_bbr, dproj_gate), (db0, db1) = _rowwise(
        "bwd_gate", bwd_gate, s, tr, [(d_mixed, 1024, 0), (a_br, 1024, 0), (b_br, 1024, 0), (proj_gate, 1024, 0), (proj_gate, 1024, 1)],
        [vec["b0"], vec["b1"]], [(1024, BF16), (1024, BF16), (2048, BF16)], [1024, 1024])
    grads["w_ret_out"] = wg(yr, d_abr, m=2048, n=1024, tm=2048, tn=1024, tk=ts2, name="g_ret_out")
    d_yr = mm(d_abr, wts["w_ret_out"], mode="nt", m=s, n=2048, k=1024, tn=1024, tk=1024, out_dtype=BF16, name="d_yr")
    grads["w_dil_out"] = wg(d_bbr, ya, m=1024, n=512, tm=1024, tn=512, tk=ts2, name="g_dil_out")
    d_ya = mm(d_bbr, wts["w_dil_out"], mode="nn", m=s, n=512, k=1024, tn=512, tk=1024, out_dtype=F32, name="d_ya")

    slots = {}
    names = list(grads) if on_mesh else []
    shares = _Exchange([blocks(grads[n]) for n in names], [True] * len(names)) if on_mesh else None
    (dproj_ret,), got = _ret_bwd(proj_ret, tabs["cos_r"], tabs["sin_r"], y_ret, d_yr, rstate, s, carry=shares)
    slots.update(zip(names, got))
    upstream = _dil_bwd_prep(d_ya, ya, lse, s)
    dqkv = [_dil_bwd(qkv[g], *upstream[g], *tabs["dil"][g], dil, s, "dil_bwd%d" % g)
            for g, dil in enumerate(DIL_GROUPS)]

    g_ret = wg(dproj_ret, u[0], m=6144, n=1024, tm=2048, tn=1024, tk=ts2, name="g_in_ret")
    g_gate = wg(dproj_gate, u[0], m=2048, n=1024, tm=2048, tn=1024, tk=ts2, name="g_in_gate")
    g_dil = [wg(dqkv[g], u[g], m=1536, n=1024, tm=1536, tn=1024, tk=ts2, name="g_in_dil%d" % g) for g in range(3)]
    grads["w_in"] = _join_w_in(g_ret, g_gate, g_dil)

    du_ret = functools.partial(mm, dproj_ret, w_ret, mode="nn", m=s, n=1024, k=6144, tn=1024, tk=1024, out_dtype=BF16, name="du_ret")
    if on_mesh:
        du_ret, (slots["w_in"],) = du_ret(carry=_Exchange([blocks(grads["w_in"])], [True]))
    else:
        du_ret = du_ret()
    du_gate = mm(dproj_gate, w_gate, mode="nn", m=s, n=1024, k=2048, tn=1024, tk=2048, out_dtype=BF16, name="du_gate")
    du_dil = [mm(dqkv[g], w_dil[g], mode="nn", m=s, n=1024, k=1536, tn=1024, tk=1536, out_dtype=BF16, name="du_dil%d" % g)
              for g in range(3)]

    grad_x, dg_pre_mix = _grad_x(xs, d_h1, (du_ret, du_gate, du_dil[0]), du_dil[1], du_dil[2], vec["g_pre_mix"], s)

    zero = jnp.zeros((1, D_MODEL), F32)
    packet = jnp.concatenate([dg_pre_mix, dg_post_mix, dg_pre_mlp, dg_post_mlp, dg_pre_ple, db_ple, dg_post_ple, loss,
                              db0, db1] + [zero] * 6, axis=0)
    return grad_x, (slots if on_mesh else grads), packet


def _mesh_pos():
    return lax.axis_index("x"), lax.axis_index("y"), lax.axis_index("c")


class _Exchange:
    def __init__(self, arrays, scatter):
        self.arrays, self.scatter, self.n = list(arrays), list(scatter), len(arrays)
        self.out_shape = [jax.ShapeDtypeStruct(a.shape if sc else (N_DEV,) + a.shape, a.dtype)
                          for a, sc in zip(self.arrays, self.scatter)]
        self.scratch = [pltpu.SemaphoreType.DMA((self.n * 7,)), pltpu.SemaphoreType.DMA((self.n * 7,)),
                        pltpu.SemaphoreType.DMA((self.n,))]
        self.specs = [pl.BlockSpec(memory_space=pl.ANY)] * self.n

    def _copies(self, srcs, dsts, sems):
        send_sems, recv_sems, local_sems = sems
        x, y, c = _mesh_pos()
        my = 4 * x + 2 * y + c
        src_of = lambda w, idx: srcs[w].at[idx] if self.scatter[w] else srcs[w]
        local = [pltpu.make_async_copy(src_of(w, my), dsts[w].at[my], local_sems.at[w]) for w in range(self.n)]
        sends, recvs = [], []
        for w in range(self.n):
            for r in range(1, N_DEV):
                px = 1 - x if r & 4 else x
                py = 1 - y if r & 2 else y
                pc = 1 - c if r & 1 else c
                pidx = 4 * px + 2 * py + pc
                kw = dict(send_sem=send_sems.at[w * 7 + r - 1], recv_sem=recv_sems.at[w * 7 + r - 1],
                          device_id=(px, py, pc), device_id_type=MESH)
                sends.append(pltpu.make_async_remote_copy(src_ref=src_of(w, pidx), dst_ref=dsts[w].at[my], **kw))
                recvs.append(pltpu.make_async_remote_copy(src_ref=src_of(w, pidx), dst_ref=dsts[w].at[pidx], **kw))
        return local, sends, recvs

    def start(self, srcs, dsts, sems):
        local, sends, _ = self._copies(srcs, dsts, sems)
        for cp in local + sends:
            cp.start()

    def wait(self, srcs, dsts, sems):
        local, sends, recvs = self._copies(srcs, dsts, sems)
        for cp in recvs:
            cp.wait_recv()
        for cp in sends:
            cp.wait_send()
        for cp in local:
            cp.wait()

    def split(self, refs, n_in, n_out):
        srcs = refs[n_in:n_in + self.n]
        dsts = refs[n_in + self.n + n_out:n_in + 2 * self.n + n_out]
        return srcs, dsts, refs[len(refs) - 3:]


class _TwoLevelGather(_Exchange):
    def __init__(self, arrays):
        super().__init__(arrays, [False] * len(arrays))

    def _plan(self, srcs, dsts, sems):
        send_sems, recv_sems, local_sems = sems
        x, y, c = _mesh_pos()
        me, sibling = (x, y, c), (x, y, 1 - c)
        chips = [(1 - x, y), (x, 1 - y), (1 - x, 1 - y)]
        region = lambda w, dev: dsts[w].at[4 * dev[0] + 2 * dev[1] + dev[2]]

        def copy(w, kk, block, to, src=None):
            return pltpu.make_async_remote_copy(
                src_ref=region(w, block) if src is None else src, dst_ref=region(w, block),
                send_sem=send_sems.at[w * 7 + kk], recv_sem=recv_sems.at[w * 7 + kk], device_id=to, device_id_type=MESH)

        mine = [pltpu.make_async_copy(srcs[w], region(w, me), local_sems.at[w]) for w in range(self.n)]
        first = []
        for w in range(self.n):
            first.append(copy(w, 0, me, sibling, src=srcs[w]))
            first += [copy(w, 1 + j, me, (*chip, c), src=srcs[w]) for j, chip in enumerate(chips)]
        return me, sibling, chips, c, copy, mine, first

    def start(self, srcs, dsts, sems):
        *_, mine, first = self._plan(srcs, dsts, sems)
        for cp in mine + first:
            cp.start()

    def wait(self, srcs, dsts, sems):
        me, sibling, chips, c, copy, mine, first = self._plan(srcs, dsts, sems)
        passed = []
        for j, chip in enumerate(chips):
            for w in range(self.n):
                copy(w, 1 + j, (*chip, c), me).wait_recv()
                cp = copy(w, 4 + j, (*chip, c), sibling)
                cp.start()
                passed.append(cp)
        for w in range(self.n):
            copy(w, 0, sibling, me).wait_recv()
            for j, chip in enumerate(chips):
                copy(w, 4 + j, (*chip, 1 - c), me).wait_recv()
        for cp in first + passed:
            cp.wait_send()
        for cp in mine:
            cp.wait()


def _run_exchange(ex, name):
    def body(*refs):
        parts = ex.split(refs, 0, 0)
        ex.start(*parts)
        ex.wait(*parts)

    return pl.pallas_call(body, name=name, in_specs=ex.specs, out_specs=ex.specs, out_shape=ex.out_shape,
                          scratch_shapes=ex.scratch)(*ex.arrays)


def _pick_rows(r, c, target_bytes):
    t = r
    while (t // 2) % 16 == 0 and t // 2 >= 16 and t * c * 4 > target_bytes:
        t //= 2
    return t


def _sum_slots(slots, name):
    ns, r, c = slots.shape
    tr = _pick_rows(r, c, 256 * 1024)

    def body(s_ref, o_ref):
        acc = s_ref[0].astype(F32)
        for kk in range(1, ns):
            acc = acc + s_ref[kk].astype(F32)
        o_ref[...] = acc

    return pl.pallas_call(
        body, name=name, grid=(r // tr,),
        in_specs=[pl.BlockSpec((ns, tr, c), lambda i: (0, i, 0))], out_specs=pl.BlockSpec((tr, c), lambda i: (i, 0)),
        out_shape=jax.ShapeDtypeStruct((r, c), F32), compiler_params=_cparams(("parallel",)),
    )(slots)


def _adamw(slots, w, m, v, name):
    ns, r, c = slots.shape
    tr = _pick_rows(r, c, 256 * 1024)

    def body(s_ref, w_ref, m_ref, v_ref, g_out, d_out, m_out, v_out):
        g = s_ref[0].astype(F32)
        for kk in range(1, ns):
            g = g + s_ref[kk].astype(F32)
        mn = ADAM_B1 * m_ref[...] + (1.0 - ADAM_B1) * g
        vn = ADAM_B2 * v_ref[...] + (1.0 - ADAM_B2) * (g * g)
        m_hat = mn / (1.0 - ADAM_B1 ** ADAM_STEP)
        v_hat = vn / (1.0 - ADAM_B2 ** ADAM_STEP)
        g_out[...] = g
        d_out[...] = -ADAM_LR * (m_hat / (jnp.sqrt(v_hat) + ADAM_EPS) + ADAM_WD * w_ref[...])
        m_out[...] = mn
        v_out[...] = vn

    blk = pl.BlockSpec((tr, c), lambda i: (i, 0))
    return pl.pallas_call(
        body, name=name, grid=(r // tr,),
        in_specs=[pl.BlockSpec((ns, tr, c), lambda i: (0, i, 0)), blk, blk, blk], out_specs=[blk] * 4,
        out_shape=[jax.ShapeDtypeStruct((r, c), F32)] * 4, compiler_params=_cparams(("parallel",)),
    )(slots, w, m, v)


def _rotary_tables(pos, s):
    posf = pos.astype(F32)
    inv_freq = 1.0 / (10000.0 ** jnp.linspace(0.0, 1.0, RET_QK // 2, dtype=F32))
    ang = posf[:, None] * inv_freq
    tabs = {"cos_r": jnp.cos(ang), "sin_r": jnp.sin(ang), "dil": []}
    freqs = 500000.0 ** (-jnp.arange(0, 16, 2, dtype=F32) / 16)
    spread = np.zeros((16, 384), np.float32)
    bias = np.zeros((1, 384), np.float32)
    for head in range(2):
        for i in range(8):
            spread[i, 64 * head + i] = spread[i, 64 * head + 8 + i] = 1.0
            spread[8 + i, 128 + 64 * head + i] = -1.0
            spread[8 + i, 256 + 64 * head + 8 + i] = 1.0
        bias[0, 64 * head + 16:64 * head + 64] = 1.0

    def expand(t, e, b):
        hi = t.astype(BF16)
        lo = (t - hi.astype(F32)).astype(BF16)
        out = _dot(hi, e, NN) + _dot(lo, e, NN) + b
        return [out[:, 0:128], out[:, 128:256], out[:, 256:384]], []

    for g, dil in enumerate(DIL_GROUPS):
        ang = posf.reshape(s // dil, dil).T.reshape(s, 1) * freqs
        cs = jnp.concatenate([jnp.cos(ang), jnp.sin(ang)], axis=1)
        t3, _ = _rowwise("rot_tables%d" % g, expand, s, min(1024, s), [(cs, 16, 0)],
                         [jnp.asarray(spread, BF16), jnp.asarray(bias)], [(128, F32)] * 3)
        tabs["dil"].append(tuple(t3))
    return tabs


_TRANSPOSED = ("w_in", "w_dil_out", "w_up", "w_ple_in")
_MATS = ("w_in", "w_ret_out", "w_dil_out", "w_o", "w_up", "w_down", "w_ple_gate", "w_ple_in")
_VECS = ("g_pre_mix", "g_post_mix", "g_pre_mlp", "g_post_mlp", "g_pre_ple", "b_ple_gate", "g_post_ple")
_ORDER = ("w_in", "b_gate", "w_ret_out", "w_dil_out", "w_o", "g_pre_mix", "g_post_mix", "g_pre_mlp", "g_post_mlp", "w_up",
          "w_down", "g_pre_ple", "w_ple_gate", "b_ple_gate", "w_ple_in", "g_post_ple")


def kernel(x, p, positions, w_in, b_gate, w_ret_out, w_dil_out, w_o, g_pre_mix, g_post_mix, g_pre_mlp, g_post_mlp, w_up, w_down, g_pre_ple, w_ple_gate, b_ple_gate, w_ple_in, g_post_ple, loss_target, m_w_in, m_b_gate, m_w_ret_out, m_w_dil_out, m_w_o, m_g_pre_mix, m_g_post_mix, m_g_pre_mlp, m_g_post_mlp, m_w_up, m_w_down, m_g_pre_ple, m_w_ple_gate, m_b_ple_gate, m_w_ple_in, m_g_post_ple, v_w_in, v_b_gate, v_w_ret_out, v_w_dil_out, v_w_o, v_g_pre_mix, v_g_post_mix, v_g_pre_mlp, v_g_post_mlp, v_w_up, v_w_down, v_g_pre_ple, v_w_ple_gate, v_b_ple_gate, v_w_ple_in, v_g_post_ple):
    s = x.shape[1]
    wd = dict(w_in=w_in, b_gate=b_gate, w_ret_out=w_ret_out, w_dil_out=w_dil_out, w_o=w_o, g_pre_mix=g_pre_mix,
              g_post_mix=g_post_mix, g_pre_mlp=g_pre_mlp, g_post_mlp=g_post_mlp, w_up=w_up, w_down=w_down,
              g_pre_ple=g_pre_ple, w_ple_gate=w_ple_gate, b_ple_gate=b_ple_gate, w_ple_in=w_ple_in, g_post_ple=g_post_ple)
    md = dict(w_in=m_w_in, b_gate=m_b_gate, w_ret_out=m_w_ret_out, w_dil_out=m_w_dil_out, w_o=m_w_o, g_pre_mix=m_g_pre_mix,
              g_post_mix=m_g_post_mix, g_pre_mlp=m_g_pre_mlp, g_post_mlp=m_g_post_mlp, w_up=m_w_up, w_down=m_w_down,
              g_pre_ple=m_g_pre_ple, w_ple_gate=m_w_ple_gate, b_ple_gate=m_b_ple_gate, w_ple_in=m_w_ple_in, g_post_ple=m_g_post_ple)
    vd = dict(w_in=v_w_in, b_gate=v_b_gate, w_ret_out=v_w_ret_out, w_dil_out=v_w_dil_out, w_o=v_w_o, g_pre_mix=v_g_pre_mix,
              g_post_mix=v_g_post_mix, g_pre_mlp=v_g_pre_mlp, g_post_mlp=v_g_post_mlp, w_up=v_w_up, w_down=v_w_down,
              g_pre_ple=v_g_pre_ple, w_ple_gate=v_w_ple_gate, b_ple_gate=v_b_ple_gate, w_ple_in=v_w_ple_in, g_post_ple=v_g_post_ple)

    shards = {n: (wd[n][0].T if n in _TRANSPOSED else wd[n][0]).astype(BF16) for n in _MATS}
    shards["b_gate"] = b_gate[0]
    vec = {n: wd[n] for n in _VECS}
    vec["b_ple"] = b_ple_gate

    tabs = _rotary_tables(positions[0], s)
    grad_x, slots, packet = _local_step(x[0], p[0, 0].astype(BF16), loss_target[0], tabs, {}, vec, s, shards=shards)

    (packets,) = _run_exchange(_Exchange([packet], [False]), "exchange_vectors")
    out = {}
    for n in _MATS:
        sl = slots[n]
        if n in _TRANSPOSED:
            sl = _sum_slots(sl, "sum_" + n).T[None]
        out[n] = _adamw(sl, wd[n][0], md[n][0], vd[n][0], "adamw_" + n)
    zero_rows = jnp.zeros((16 - len(_VECS), D_MODEL), F32)
    pack = lambda d: jnp.concatenate([d[n] for n in _VECS] + [zero_rows], axis=0)
    small = _adamw(packets, pack(wd), pack(md), pack(vd), "adamw_vectors")
    for i, n in enumerate(_VECS):
        out[n] = tuple(t[i:i + 1] for t in small)
    my = 4 * lax.axis_index("x") + 2 * lax.axis_index("y") + lax.axis_index("c")
    g_bias = lax.dynamic_slice(small[0], (8, my * 128), (2, 128))
    out["b_gate"] = _adamw(g_bias[None], b_gate[0], m_b_gate[0], v_b_gate[0], "adamw_b_gate")
    loss = small[0][7, 0]

    res = [loss, grad_x[None]]
    for kk in range(4):
        res += [out[n][kk][None] if out[n][kk].ndim == 2 and wd[n].ndim == 3 else out[n][kk] for n in _ORDER]
    return tuple(res)
```

```python
import functools
import math

import numpy as np
import jax
import jax.numpy as jnp
from jax import lax
from jax.experimental import pallas as pl
from jax.experimental.pallas import tpu as pltpu

F32, BF16 = jnp.float32, jnp.bfloat16
D_MODEL = 1024
EPS = 1e-6
N_DEV = 8
RET_HEADS, RET_QK, RET_V, RET_CHUNK = 4, 256, 512, 128
DIL_GROUPS = (1, 4, 16)
DIL_W = 512
QB = 128
NEG = -1e30
ADAM_LR, ADAM_B1, ADAM_B2, ADAM_EPS, ADAM_WD, ADAM_STEP = 0.001, 0.9, 0.999, 1e-08, 0.01, 10
VMEM_LIMIT_BYTES = 56 * 1024 * 1024
MESH = pl.DeviceIdType.MESH

NN = ((1,), (0,))
NT = ((1,), (1,))
TN = ((0,), (0,))


def _dot(a, b, dn):
    return lax.dot_general(a, b, (dn, ((), ())), preferred_element_type=F32)


def _cparams(sem):
    return pltpu.CompilerParams(dimension_semantics=sem, vmem_limit_bytes=VMEM_LIMIT_BYTES)


def _rms(x):
    return x * lax.rsqrt(jnp.mean(x * x, axis=-1, keepdims=True) + EPS)


def _rms_bwd(x, g, dy):
    r = lax.rsqrt(jnp.mean(x * x, axis=-1, keepdims=True) + EPS)
    xh = x * r
    t = dy * g
    dx = r * (t - xh * jnp.mean(t * xh, axis=-1, keepdims=True))
    return dx, dy * xh


def _colsum(v):
    return jnp.sum(v, axis=0, keepdims=True)


def _sigmoid(v):
    return 1.0 / (1.0 + jnp.exp(-v))


def _pallas(compute, *, name, grid, in_specs, out_specs, out_shape, scratch, semantics, args, carry=None):
    n_in, n_out = len(in_specs), len(out_specs)
    if carry is None:
        res = pl.pallas_call(compute, name=name, grid=grid, in_specs=in_specs, out_specs=out_specs, out_shape=out_shape,
                             scratch_shapes=scratch, compiler_params=_cparams(semantics))(*args)
        return res, []
    n_steps = math.prod(grid)

    def body(*refs):
        step = 0
        for axis, size in enumerate(grid):
            step = step * size + pl.program_id(axis)
        parts = carry.split(refs, n_in, n_out)
        pl.when(step == 0)(lambda: carry.start(*parts))
        compute(*refs[:n_in], *refs[n_in + carry.n:n_in + carry.n + n_out], *refs[n_in + 2 * carry.n + n_out:len(refs) - 3])
        pl.when(step == n_steps - 1)(lambda: carry.wait(*parts))

    res = pl.pallas_call(
        body, name=name, grid=grid, in_specs=list(in_specs) + carry.specs, out_specs=list(out_specs) + carry.specs,
        out_shape=list(out_shape) + carry.out_shape, scratch_shapes=list(scratch) + carry.scratch,
        compiler_params=_cparams(("arbitrary",) * len(grid)))(*args, *carry.arrays)
    return res[:n_out], res[n_out:]


def _matmul(a, b, *, mode, m, n, k, tm, tn, tk, out_dtype, name, a_fn=None, epi=(), epi_width=None, epi_fn=None, carry=None):
    nk = k // tk
    grid = (m // tm, n // tn, nk)
    if mode == "nn":
        a_blk, a_im, b_blk, b_im, dn = (tm, tk), (lambda i, j, kk: (i, kk)), (tk, tn), (lambda i, j, kk: (kk, j)), NN
    elif mode == "nt":
        a_blk, a_im, b_blk, b_im, dn = (tm, tk), (lambda i, j, kk: (i, kk)), (tn, tk), (lambda i, j, kk: (j, kk)), NT
    else:
        a_blk, a_im, b_blk, b_im, dn = (tk, tm), (lambda i, j, kk: (kk, i)), (tk, tn), (lambda i, j, kk: (kk, j)), TN
    o_im = lambda i, j, kk: (i, j)
    n_in = 2 + len(epi)

    def body(*refs):
        a_ref, b_ref = refs[0], refs[1]
        o_ref = refs[n_in]
        acc_ref = refs[n_in + 1] if nk > 1 else None

        def finish(acc):
            if epi:
                acc = epi_fn(acc, *[r[...] for r in refs[2:n_in]])
            o_ref[...] = acc.astype(o_ref.dtype)

        av = a_ref[...]
        if a_fn is not None:
            av = a_fn(av)
        part = _dot(av, b_ref[...], dn)
        if nk == 1:
            finish(part)
        else:
            kk = pl.program_id(2)

            @pl.when(kk == 0)
            def _():
                acc_ref[...] = part

            @pl.when(kk > 0)
            def _():
                acc_ref[...] += part

            @pl.when(kk == nk - 1)
            def _():
                finish(acc_ref[...])

    epi_spec = pl.BlockSpec((tm, tn), o_im) if epi_width is None else pl.BlockSpec((tm, epi_width), lambda i, j, kk: (i, 0))
    in_specs = [pl.BlockSpec(a_blk, a_im), pl.BlockSpec(b_blk, b_im)] + [epi_spec] * len(epi)
    args = [a, b, *epi]
    (out,), got = _pallas(
        body, name=name, grid=grid, in_specs=in_specs, out_specs=[pl.BlockSpec((tm, tn), o_im)],
        out_shape=[jax.ShapeDtypeStruct((m, n), out_dtype)], scratch=[pltpu.VMEM((tm, tn), F32)] if nk > 1 else [],
        semantics=("parallel", "parallel", "arbitrary"), args=args, carry=carry)
    return out if carry is None else (out, got)


def _relu_sq(v):
    r = jnp.maximum(v, jnp.zeros_like(v))
    return r * r


def _rowwise(name, fn, s, tr, rows, vecs, outs, accs=()):
    n_r, n_v, n_o, n_a = len(rows), len(vecs), len(outs), len(accs)

    def body(*refs):
        vals = [refs[i][...].astype(F32) for i in range(n_r)] + [refs[n_r + i][...] for i in range(n_v)]
        o_refs = refs[n_r + n_v:n_r + n_v + n_o]
        a_refs = refs[n_r + n_v + n_o:]
        o_vals, a_vals = fn(*vals)
        for ref, val in zip(o_refs, o_vals):
            ref[...] = val.astype(ref.dtype)
        if n_a:
            @pl.when(pl.program_id(0) == 0)
            def _():
                for ref in a_refs:
                    ref[...] = jnp.zeros_like(ref)

            for ref, val in zip(a_refs, a_vals):
                ref[...] += val

    in_specs = [pl.BlockSpec((tr, w), functools.partial(lambda i, cb: (i, cb), cb=cb)) for _, w, cb in rows]
    in_specs += [pl.BlockSpec(v.shape, lambda i: (0, 0)) for v in vecs]
    out_specs = [pl.BlockSpec((tr, w), lambda i: (i, 0)) for w, _ in outs]
    out_specs += [pl.BlockSpec((1, w), lambda i: (0, 0)) for w in accs]
    out_shape = [jax.ShapeDtypeStruct((s, w), dt) for w, dt in outs]
    out_shape += [jax.ShapeDtypeStruct((1, w), F32) for w in accs]
    res = pl.pallas_call(
        body, name=name, grid=(s // tr,), in_specs=in_specs, out_specs=out_specs, out_shape=out_shape,
        compiler_params=_cparams(("arbitrary",)),
    )(*[r[0] for r in rows], *vecs)
    return res[:n_o], res[n_o:]


_ROW_TILE = 512
_STREAM_SPECS = [pl.BlockSpec((dil, _ROW_TILE // dil, D_MODEL), lambda i: (0, i, 0)) for dil in DIL_GROUPS[1:]]
_NAT_SPEC = pl.BlockSpec((_ROW_TILE, D_MODEL), lambda i: (i, 0))
_VEC_SPEC = pl.BlockSpec((1, D_MODEL), lambda i: (0, 0))
_COL_BLOCKS = pltpu.VMEM((D_MODEL // 128, _ROW_TILE, 128), F32)


def _prenorm(xs, g, s, carry=None):
    tr = _ROW_TILE

    def body(x_ref, g_ref, u_ref, u4_ref, u16_ref, buf):
        xn = _rms(x_ref[...]) * g_ref[...]
        u_ref[...] = xn.astype(BF16)
        for cb in range(8):
            buf[cb] = xn[:, cb * 128:(cb + 1) * 128]
        for dil, out in ((4, u4_ref), (16, u16_ref)):
            for c in range(dil):
                rows = pl.ds(c, tr // dil, stride=dil)
                out[c] = jnp.concatenate([buf.at[cb][rows, :] for cb in range(8)], axis=1).astype(BF16)

    res, got = _pallas(
        body, name="prenorm", grid=(s // tr,), in_specs=[_NAT_SPEC, _VEC_SPEC], out_specs=[_NAT_SPEC] + _STREAM_SPECS,
        out_shape=[jax.ShapeDtypeStruct((s, D_MODEL), BF16)]
        + [jax.ShapeDtypeStruct((dil, s // dil, D_MODEL), BF16) for dil in DIL_GROUPS[1:]],
        scratch=[_COL_BLOCKS], semantics=("parallel",), args=(xs, g), carry=carry)
    return [r.reshape(s, D_MODEL) for r in res], got


def _grad_x(xs, d_h1, du_nat, du4, du16, g, s):
    tr = _ROW_TILE

    def body(x_ref, dh_ref, a_ref, b_ref, c_ref, u4_ref, u16_ref, g_ref, dx_ref, dg_ref, buf):
        du = a_ref[...].astype(F32) + b_ref[...].astype(F32) + c_ref[...].astype(F32)
        for dil, src in ((4, u4_ref), (16, u16_ref)):
            for c in range(dil):
                part = src[c].astype(F32)
                for cb in range(8):
                    buf.at[cb][pl.ds(c, tr // dil, stride=dil), :] = part[:, cb * 128:(cb + 1) * 128]
            du = du + jnp.concatenate([buf[cb] for cb in range(8)], axis=1)
        dx, dgr = _rms_bwd(x_ref[...], g_ref[...], du)
        dx_ref[...] = dh_ref[...] + dx

        @pl.when(pl.program_id(0) == 0)
        def _():
            dg_ref[...] = jnp.zeros_like(dg_ref)

        dg_ref[...] += _colsum(dgr)

    return pl.pallas_call(
        body, name="grad_x", grid=(s // tr,), in_specs=[_NAT_SPEC] * 5 + _STREAM_SPECS + [_VEC_SPEC],
        out_specs=[_NAT_SPEC, _VEC_SPEC],
        out_shape=[jax.ShapeDtypeStruct((s, D_MODEL), F32), jax.ShapeDtypeStruct((1, D_MODEL), F32)],
        scratch_shapes=[_COL_BLOCKS], compiler_params=_cparams(("arbitrary",)),
    )(xs, d_h1, *du_nat, du4.reshape(4, s // 4, D_MODEL), du16.reshape(16, s // 16, D_MODEL), g)


def _ret_tables():
    h = np.arange(RET_HEADS, dtype=np.float32)
    lg = np.log1p(-(np.float32(2.0) ** (-5.0 - h))).astype(np.float32)
    idx = np.arange(RET_CHUNK, dtype=np.float32)
    diff = idx[:, None] - idx[None, :]
    dm = np.where(diff[None] >= 0, np.exp(np.maximum(diff, 0.0)[None] * lg[:, None, None]), 0.0)
    qd = np.exp((idx + 1.0)[None, :, None] * lg[:, None, None])
    kd = np.exp((RET_CHUNK - 1.0 - idx)[None, :, None] * lg[:, None, None])
    cd = np.exp(RET_CHUNK * lg)[:, None, None]
    return [jnp.asarray(t, F32) for t in (dm, qd, kd, cd)]


def _rope_half(v, cos, sin):
    v1, v2 = v[:, :128], v[:, 128:]
    return jnp.concatenate([v1 * cos - v2 * sin, v2 * cos + v1 * sin], axis=1)


def _unrope_half(d, cos, sin):
    d1, d2 = d[:, :128], d[:, 128:]
    return jnp.concatenate([d1 * cos + d2 * sin, d2 * cos - d1 * sin], axis=1)


_RET_HEADS_FWD, _RET_HEADS_BWD = 1, 2


def _ret_specs(rb, rev_n, hp):
    def rowmap(w_blk):
        return lambda h, n: (rev_n(n), w_blk(h))
    tab = [pl.BlockSpec((hp, RET_CHUNK, RET_CHUNK), lambda h, n: (h, 0, 0)),
           pl.BlockSpec((hp, RET_CHUNK, 1), lambda h, n: (h, 0, 0)),
           pl.BlockSpec((hp, RET_CHUNK, 1), lambda h, n: (h, 0, 0)),
           pl.BlockSpec((hp, 1, 1), lambda h, n: (h, 0, 0))]
    proj = pl.BlockSpec((rb, hp * 1536), rowmap(lambda h: h))
    cs = pl.BlockSpec((rb, 128), rowmap(lambda h: 0))
    hv = pl.BlockSpec((rb, hp * RET_V), rowmap(lambda h: h))
    return proj, cs, hv, tab


def _ret_fwd(proj_ret, cos, sin, s, carry=None):
    rb = min(512, s)
    ch = rb // RET_CHUNK
    nb = s // rb
    hp = _RET_HEADS_FWD
    proj_spec, cs_spec, hv_spec, tab_specs = _ret_specs(rb, lambda n: n, hp)

    def body(p_ref, cos_ref, sin_ref, dm_ref, qd_ref, kd_ref, cd_ref, yr_ref, y_ref, rs_ref, r_acc):
        @pl.when(pl.program_id(1) == 0)
        def _():
            r_acc[...] = jnp.zeros_like(r_acc)

        for c, hh in [(c, hh) for c in range(ch) for hh in range(hp)]:
            rows = slice(c * RET_CHUNK, (c + 1) * RET_CHUNK)
            pc, hc = hh * 1536, hh * RET_V
            dm, qd, kd, cd = dm_ref[hh], qd_ref[hh], kd_ref[hh], cd_ref[hh]
            cosv, sinv = cos_ref[rows, :], sin_ref[rows, :]
            q = _rope_half(p_ref[rows, pc:pc + 256].astype(F32), cosv, sinv)
            kk = _rope_half(p_ref[rows, pc + 256:pc + 512].astype(F32), cosv, sinv) * (RET_QK ** -0.5)
            v = p_ref[rows, pc + 512:pc + 1024]
            g = p_ref[rows, pc + 1024:pc + 1536].astype(F32)
            rb16 = r_acc[hh].astype(BF16)
            rs_ref[hh, c] = rb16
            sc = _dot(q.astype(BF16), kk.astype(BF16), NT) * dm
            y = _dot(sc.astype(BF16), v, NN) + _dot((q * qd).astype(BF16), rb16, NN)
            r_acc[hh] = r_acc[hh] * cd + _dot((kk * kd).astype(BF16), v, TN)
            y_ref[rows, hc:hc + RET_V] = y.astype(BF16)
            yr_ref[rows, hc:hc + RET_V] = (_rms(y) * (g * _sigmoid(g))).astype(BF16)

    return _pallas(
        body, name="ret_fwd", grid=(RET_HEADS // hp, nb),
        in_specs=[proj_spec, cs_spec, cs_spec] + tab_specs,
        out_specs=[hv_spec, hv_spec, pl.BlockSpec((hp, ch, RET_QK, RET_V), lambda h, n: (h, n, 0, 0))],
        out_shape=[jax.ShapeDtypeStruct((s, RET_HEADS * RET_V), BF16), jax.ShapeDtypeStruct((s, RET_HEADS * RET_V), BF16),
                   jax.ShapeDtypeStruct((RET_HEADS, s // RET_CHUNK, RET_QK, RET_V), BF16)],
        scratch=[pltpu.VMEM((hp, RET_QK, RET_V), F32)], semantics=("parallel", "arbitrary"),
        args=(proj_ret, cos, sin, *_ret_tables()), carry=carry)


def _ret_bwd(proj_ret, cos, sin, y, d_yr, rs, s, carry=None):
    rb = min(512, s)
    ch = rb // RET_CHUNK
    nb = s // rb
    hp = _RET_HEADS_BWD
    proj_spec, cs_spec, hv_spec, tab_specs = _ret_specs(rb, lambda n: nb - 1 - n, hp)

    def body(p_ref, cos_ref, sin_ref, y_ref, dyr_ref, rs_ref, dm_ref, qd_ref, kd_ref, cd_ref, o_ref, dr_acc):
        @pl.when(pl.program_id(1) == 0)
        def _():
            dr_acc[...] = jnp.zeros_like(dr_acc)

        for c, hh in [(c, hh) for c in reversed(range(ch)) for hh in range(hp)]:
            rows = slice(c * RET_CHUNK, (c + 1) * RET_CHUNK)
            pc, hc = hh * 1536, hh * RET_V
            dm, qd, kd, cd = dm_ref[hh], qd_ref[hh], kd_ref[hh], cd_ref[hh]
            cosv, sinv = cos_ref[rows, :], sin_ref[rows, :]
            q = _rope_half(p_ref[rows, pc:pc + 256].astype(F32), cosv, sinv)
            kk = _rope_half(p_ref[rows, pc + 256:pc + 512].astype(F32), cosv, sinv) * (RET_QK ** -0.5)
            v = p_ref[rows, pc + 512:pc + 1024]
            g = p_ref[rows, pc + 1024:pc + 1536].astype(F32)
            yv = y_ref[rows, hc:hc + RET_V].astype(F32)
            dyr = dyr_ref[rows, hc:hc + RET_V].astype(F32)
            sg = _sigmoid(g)
            r = lax.rsqrt(jnp.mean(yv * yv, axis=-1, keepdims=True) + EPS)
            yn = yv * r
            dg = dyr * yn * (sg * (1.0 + g * (1.0 - sg)))
            dyn = dyr * (g * sg)
            dy = (r * (dyn - yn * jnp.mean(dyn * yn, axis=-1, keepdims=True))).astype(BF16)
            qb, kb = q.astype(BF16), kk.astype(BF16)
            rb16 = rs_ref[hh, c]
            drb = dr_acc[hh].astype(BF16)
            sd = _dot(qb, kb, NT) * dm
            ds = (_dot(dy, v, NT) * dm).astype(BF16)
            dq = _dot(ds, kb, NN) + qd * _dot(dy, rb16, NT)
            dk = _dot(ds, qb, TN) + kd * _dot(v, drb, NT)
            dv = _dot(sd.astype(BF16), dy, TN) + _dot((kk * kd).astype(BF16), drb, NN)
            dr_acc[hh] = dr_acc[hh] * cd + _dot((q * qd).astype(BF16), dy, TN)
            o_ref[rows, pc:pc + 256] = _unrope_half(dq, cosv, sinv).astype(BF16)
            o_ref[rows, pc + 256:pc + 512] = (_unrope_half(dk, cosv, sinv) * (RET_QK ** -0.5)).astype(BF16)
            o_ref[rows, pc + 512:pc + 1024] = dv.astype(BF16)
            o_ref[rows, pc + 1024:pc + 1536] = dg.astype(BF16)

    in_specs = [proj_spec, cs_spec, cs_spec, hv_spec, hv_spec,
                pl.BlockSpec((hp, ch, RET_QK, RET_V), lambda h, n: (h, nb - 1 - n, 0, 0))] + tab_specs
    return _pallas(
        body, name="ret_bwd", grid=(RET_HEADS // hp, nb), in_specs=in_specs, out_specs=[proj_spec],
        out_shape=[jax.ShapeDtypeStruct((s, RET_HEADS * 1536), BF16)], scratch=[pltpu.VMEM((hp, RET_QK, RET_V), F32)],
        semantics=("parallel", "arbitrary"), args=(proj_ret, cos, sin, y, d_yr, rs, *_ret_tables()), carry=carry)


def _rope_qk(acc, c, s1, s2):
    outs = []
    for cc in range(8):
        vv = acc[:, cc * 128:(cc + 1) * 128]
        outs.append(vv * c + pltpu.roll(vv, 120, 1) * s1 + pltpu.roll(vv, 8, 1) * s2)
    return jnp.concatenate(outs + [acc[:, 2 * DIL_W:]], axis=1)


def _pair_masks(keys_on_rows=False):
    ri = lax.broadcasted_iota(jnp.int32, (2 * QB, 2 * QB), 1 if keys_on_rows else 0)
    ci = lax.broadcasted_iota(jnp.int32, (2 * QB, 2 * QB), 0 if keys_on_rows else 1)
    e = ci - (ri & (QB - 1))
    lane_lo = lax.broadcasted_iota(jnp.int32, (2 * QB, 128), 1) < 64
    return ci, jnp.logical_and(e >= 0, e <= QB), lane_lo


def _stack_heads(v, lane_lo):
    z = jnp.zeros_like(v)
    return jnp.concatenate([jnp.where(lane_lo, v, z), jnp.where(lane_lo, z, v)], axis=0)


def _dil_fwd(qkv, dil, s, name):
    length = s // dil
    rb = min(512, length)
    nsub = rb // QB
    nbs = length // rb
    sub_per = rb // QB

    def body(q_ref, k_ref, v_ref, kp_ref, vp_ref, o_ref, l_ref):
        first = (pl.program_id(0) % nbs) == 0
        ci, band, lane_lo = _pair_masks()
        lo1 = lane_lo[0:QB]

        for i in range(nsub):
            rows = slice(i * QB, (i + 1) * QB)
            mask = jnp.logical_and(band, ci >= jnp.where(first, QB, 0)) if i == 0 else band
            for j in range(4):
                lanes = slice(j * 128, (j + 1) * 128)
                q2 = _stack_heads(q_ref[rows, lanes], lo1)
                if i == 0:
                    k2 = jnp.concatenate([kp_ref[:, lanes], k_ref[rows, lanes]], axis=0)
                    v2 = jnp.concatenate([vp_ref[:, lanes], v_ref[rows, lanes]], axis=0)
                else:
                    k2, v2 = k_ref[(i - 1) * QB:(i + 1) * QB, lanes], v_ref[(i - 1) * QB:(i + 1) * QB, lanes]
                v2 = _stack_heads(v2, lane_lo)
                sc = jnp.where(mask, _dot(q2, k2, NT) * 0.125, NEG)
                m = jnp.max(sc, axis=1, keepdims=True)
                p = jnp.exp(sc - m)
                den = jnp.sum(p, axis=1, keepdims=True)
                pb = p.astype(BF16)
                o = _dot(jnp.concatenate([pb[0:QB], pb[QB:]], axis=1), v2, NN)
                inv = 1.0 / den
                lse = m + jnp.log(den)
                o_ref[rows, lanes] = o * jnp.where(lo1, inv[0:QB], inv[QB:])
                l_ref[rows, lanes] = jnp.where(lo1, lse[0:QB], lse[QB:])

    prev = lambda n: jnp.maximum(n * sub_per - 1, 0)
    cur = lambda cb: (lambda n: (n, cb))
    return pl.pallas_call(
        body, name=name, grid=(s // rb,),
        in_specs=[pl.BlockSpec((rb, DIL_W), cur(0)), pl.BlockSpec((rb, DIL_W), cur(1)), pl.BlockSpec((rb, DIL_W), cur(2)),
                  pl.BlockSpec((QB, DIL_W), lambda n: (prev(n), 1)), pl.BlockSpec((QB, DIL_W), lambda n: (prev(n), 2))],
        out_specs=[pl.BlockSpec((rb, DIL_W), cur(0)), pl.BlockSpec((rb, DIL_W), cur(0))],
        out_shape=[jax.ShapeDtypeStruct((s, DIL_W), F32), jax.ShapeDtypeStruct((s, DIL_W), F32)],
        compiler_params=_cparams(("parallel",)),
    )(qkv, qkv, qkv, qkv, qkv)


def _dil_bwd(qkv, dya, lse, dlt, tc, ts1, ts2, dil, s, name):
    length = s // dil
    rb = min(512, length)
    nsub = rb // QB
    nbs = length // rb
    last_blk = s // QB - 1

    def body(q_ref, k_ref, v_ref, kp_ref, vp_ref, qn_ref, dy_ref, dyn_ref, l_ref, ln_ref, d_ref, dn_ref,
             c_ref, s1_ref, s2_ref, o_ref, dka, dva):
        nl = pl.program_id(0) % nbs
        first, last = nl == 0, nl == nbs - 1
        ci, band, lane_lo = _pair_masks(keys_on_rows=True)
        lo1 = lane_lo[0:QB]

        def unrope(d, rows):
            return d * c_ref[rows, :] + pltpu.roll(d * s1_ref[rows, :], 8, 1) + pltpu.roll(d * s2_ref[rows, :], 120, 1)

        for qi in range(nsub + 1):
            nxt = qi == nsub
            rows = slice((nsub - 1) * QB, nsub * QB) if nxt else slice(qi * QB, (qi + 1) * QB)
            prev_rows = slice((qi - 1) * QB, qi * QB)
            if qi == 0:
                mask = jnp.logical_and(band, ci >= jnp.where(first, QB, 0))
            elif nxt:
                mask = jnp.logical_and(band, ci <= jnp.where(last, -1, QB - 1))[0:QB, :]
            else:
                mask = band
            for j in range(4):
                lanes = slice(j * 128, (j + 1) * 128)
                if nxt:
                    q, do, lv, dl = qn_ref[:, lanes], dyn_ref[:, lanes], ln_ref[:, lanes], dn_ref[:, lanes]
                    k2, v2 = k_ref[prev_rows, lanes], v_ref[prev_rows, lanes]
                else:
                    q, do, lv, dl = q_ref[rows, lanes], dy_ref[rows, lanes], l_ref[rows, lanes], d_ref[rows, lanes]
                    if qi == 0:
                        k2 = jnp.concatenate([kp_ref[:, lanes], k_ref[rows, lanes]], axis=0)
                        v2 = jnp.concatenate([vp_ref[:, lanes], v_ref[rows, lanes]], axis=0)
                    else:
                        k2, v2 = k_ref[(qi - 1) * QB:(qi + 1) * QB, lanes], v_ref[(qi - 1) * QB:(qi + 1) * QB, lanes]
                q2, do2 = _stack_heads(q, lo1), _stack_heads(do, lo1)
                lt, dt = lv.T, dl.T
                lse2 = jnp.concatenate([lt[0:1], lt[64:65]], axis=1)
                dl2 = jnp.concatenate([dt[0:1], dt[64:65]], axis=1)
                sc = _dot(k2, q2, NT) * 0.125
                p = jnp.where(mask, jnp.exp(jnp.minimum(sc - lse2, 0.0)), 0.0)
                ds = (p * (_dot(v2, do2, NT) - dl2) * 0.125).astype(BF16)
                dk2 = _dot(ds, q2, NN)
                dv2 = _dot(p.astype(BF16), do2, NN)
                if qi >= 1:
                    dka[prev_rows, lanes] += dk2[0:QB]
                    dva[prev_rows, lanes] += dv2[0:QB]
                if not nxt:
                    dka[rows, lanes] = dk2[QB:]
                    dva[rows, lanes] = dv2[QB:]
                    dq = _dot(jnp.concatenate([ds[:, 0:QB], ds[:, QB:]], axis=0), _stack_heads(k2, lane_lo), TN)
                    o_ref[rows, lanes] = unrope(dq, rows).astype(BF16)

        for cc in range(4):
            lanes = slice(cc * 128, (cc + 1) * 128)
            o_ref[:, 512 + cc * 128:512 + (cc + 1) * 128] = unrope(dka[:, lanes], slice(None)).astype(BF16)
            o_ref[:, 1024 + cc * 128:1024 + (cc + 1) * 128] = dva[:, lanes].astype(BF16)

    prev = lambda n: jnp.maximum(n * nsub - 1, 0)
    nxt = lambda n: jnp.minimum(n * nsub + nsub, last_blk)
    cur = lambda cb: (lambda n: (n, cb))
    big = lambda cb: pl.BlockSpec((rb, DIL_W), cur(cb))
    small = lambda im: pl.BlockSpec((QB, DIL_W), im)
    tab = pl.BlockSpec((rb, 128), cur(0))
    return pl.pallas_call(
        body, name=name, grid=(s // rb,),
        in_specs=[big(0), big(1), big(2), small(lambda n: (prev(n), 1)), small(lambda n: (prev(n), 2)),
                  small(lambda n: (nxt(n), 0)), big(0), small(lambda n: (nxt(n), 0)), big(0), small(lambda n: (nxt(n), 0)),
                  big(0), small(lambda n: (nxt(n), 0)), tab, tab, tab],
        out_specs=pl.BlockSpec((rb, 3 * DIL_W), cur(0)),
        out_shape=jax.ShapeDtypeStruct((s, 3 * DIL_W), BF16),
        scratch_shapes=[pltpu.VMEM((rb, DIL_W), F32), pltpu.VMEM((rb, DIL_W), F32)],
        compiler_params=_cparams(("parallel",)),
    )(qkv, qkv, qkv, qkv, qkv, qkv, dya, dya, lse, lse, dlt, dlt, tc, ts1, ts2)


def _stream_specs(tr):
    nat = pl.BlockSpec((tr, 128), lambda i, j: (i, j))
    return [nat] + [pl.BlockSpec((dil, tr // dil, 128), lambda i, j: (0, i, j)) for dil in DIL_GROUPS[1:]]


def _dil_merge(o_g, l_g, s):
    tr = min(2048, s)
    nat, sp4, sp16 = _stream_specs(tr)

    def body(o0_ref, l0_ref, o1_ref, l1_ref, o2_ref, l2_ref, ya_ref, lse_ref, o1n, l1n, o2n, l2n):
        for src, dst, dil in ((o1_ref, o1n, 4), (l1_ref, l1n, 4), (o2_ref, o2n, 16), (l2_ref, l2n, 16)):
            for c in range(dil):
                dst[pl.ds(c, tr // dil, stride=dil), :] = src[c]
        l0, l1, l2 = l0_ref[...], l1n[...], l2n[...]
        m = jnp.maximum(jnp.maximum(l0, l1), l2)
        e0, e1, e2 = jnp.exp(l0 - m), jnp.exp(l1 - m), jnp.exp(l2 - m)
        den = e0 + e1 + e2
        ya_ref[...] = ((e0 * o0_ref[...] + e1 * o1n[...] + e2 * o2n[...]) / den).astype(BF16)
        lse_ref[...] = m + jnp.log(den)

    v3 = lambda a, dil: a.reshape(dil, s // dil, DIL_W)
    return pl.pallas_call(
        body, name="dil_merge", grid=(s // tr, 4),
        in_specs=[nat, nat, sp4, sp4, sp16, sp16], out_specs=[nat, nat],
        out_shape=[jax.ShapeDtypeStruct((s, DIL_W), BF16), jax.ShapeDtypeStruct((s, DIL_W), F32)],
        scratch_shapes=[pltpu.VMEM((tr, 128), F32)] * 4,
        compiler_params=_cparams(("parallel", "parallel")),
    )(o_g[0], l_g[0], v3(o_g[1], 4), v3(l_g[1], 4), v3(o_g[2], 16), v3(l_g[2], 16))


def _dil_bwd_prep(d_ya, ya, lse, s):
    tr = min(2048, s)
    nat, sp4, sp16 = _stream_specs(tr)

    def body(dya_ref, ya_ref, lse_ref, dy0, dl0, dy1, ls1, dl1, dy2, ls2, dl2, dlt):
        lane_lo = lax.broadcasted_iota(jnp.int32, (tr, 128), 1) < 64
        prod = dya_ref[...] * ya_ref[...].astype(F32)
        lo = jnp.where(lane_lo, prod, 0.0)
        dlt[...] = jnp.where(lane_lo, jnp.sum(lo, axis=1, keepdims=True), jnp.sum(prod - lo, axis=1, keepdims=True))
        dy0[...] = dya_ref[...].astype(BF16)
        dl0[...] = dlt[...]
        for dil, dy, ls, dl in ((4, dy1, ls1, dl1), (16, dy2, ls2, dl2)):
            for c in range(dil):
                rows = pl.ds(c, tr // dil, stride=dil)
                dy[c] = dya_ref[rows, :].astype(BF16)
                ls[c] = lse_ref[rows, :]
                dl[c] = dlt[rows, :]

    sh = lambda dil, dt: jax.ShapeDtypeStruct((dil, s // dil, DIL_W), dt)
    res = pl.pallas_call(
        body, name="dil_bwd_prep", grid=(s // tr, 4),
        in_specs=[nat, nat, nat], out_specs=[nat, nat, sp4, sp4, sp4, sp16, sp16, sp16],
        out_shape=[jax.ShapeDtypeStruct((s, DIL_W), BF16), jax.ShapeDtypeStruct((s, DIL_W), F32),
                   sh(4, BF16), sh(4, F32), sh(4, F32), sh(16, BF16), sh(16, F32), sh(16, F32)],
        scratch_shapes=[pltpu.VMEM((tr, 128), F32)],
        compiler_params=_cparams(("parallel", "parallel")),
    )(d_ya, ya, lse)
    dy0, dl0, dy1, ls1, dl1, dy2, ls2, dl2 = [r.reshape(s, DIL_W) for r in res]
    return [(dy0, lse, dl0), (dy1, ls1, dl1), (dy2, ls2, dl2)]


_RET_SEGS = ((0, 256), (1024, 256), (2048, 512), (4096, 512))


def _split_w_in(win):
    per_head = [win[a:a + RET_HEADS * n].reshape(RET_HEADS, n, D_MODEL) for a, n in _RET_SEGS]
    w_ret = jnp.concatenate(per_head, axis=1).reshape(RET_HEADS * 1536, D_MODEL)
    w_dil = [jnp.concatenate([win[a + DIL_W * g:a + DIL_W * (g + 1)] for a in (6144, 7680, 9216)], axis=0) for g in range(3)]
    return w_ret, win[10752:12800], w_dil


def _join_w_in(g_ret, g_gate, g_dil):
    g_ret = g_ret.reshape(RET_HEADS, 1536, D_MODEL)
    off = (0, 256, 512, 1024, 1536)
    parts = [g_ret[:, off[i]:off[i + 1]].reshape(-1, D_MODEL) for i in range(4)]
    dil = [g_dil[g][DIL_W * i:DIL_W * (i + 1)] for i in range(3) for g in range(3)]
    return jnp.concatenate(parts + dil + [g_gate], axis=0)


def _local_step(xs, pb, tgt, tabs, wts, vec, s, shards=None):
    tm = min(2048, s)
    tr = min(512, s)
    mm = functools.partial(_matmul, tm=tm)
    on_mesh = shards is not None
    wts, vec = dict(wts), dict(vec)
    blocks = lambda g: g.reshape(N_DEV, g.shape[0] // N_DEV, g.shape[1])

    late_shards = dict(shards) if on_mesh else {}
    first = _TwoLevelGather([late_shards.pop("w_in"), late_shards.pop("b_gate")]) if on_mesh else None
    u, gathered = _prenorm(xs, vec["g_pre_mix"], s, carry=first)
    if on_mesh:
        wts["w_in"] = gathered[0].reshape(N_DEV * gathered[0].shape[1], D_MODEL)
        bias = gathered[1].transpose(1, 0, 2).reshape(2, D_MODEL)
        vec.update(b0=bias[0:1], b1=bias[1:2])
    w_ret, w_gate, w_dil = _split_w_in(wts["w_in"])
    proj_ret = mm(u[0], w_ret, mode="nt", m=s, n=6144, k=1024, tn=1024, tk=1024, out_dtype=BF16, name="inproj_ret")
    proj_gate = mm(u[0], w_gate, mode="nt", m=s, n=2048, k=1024, tn=1024, tk=1024, out_dtype=BF16, name="inproj_gate")
    qkv = [_matmul(u[g], w_dil[g], mode="nt", m=s, n=1536, k=1024, tm=min(1024, s), tn=1536, tk=1024, out_dtype=BF16,
                   name="inproj_dil%d" % g, epi=tabs["dil"][g], epi_width=128, epi_fn=_rope_qk) for g in range(3)]

    names = list(late_shards) if on_mesh else []
    gather = _TwoLevelGather([late_shards[n] for n in names]) if on_mesh else None
    (yr, y_ret, rstate), gathered = _ret_fwd(proj_ret, tabs["cos_r"], tabs["sin_r"], s, carry=gather)
    wts.update({n: g.reshape(N_DEV * g.shape[1], g.shape[2]) for n, g in zip(names, gathered)})
    a_br = mm(yr, wts["w_ret_out"], mode="nn", m=s, n=1024, k=2048, tn=1024, tk=2048, out_dtype=BF16, name="ret_out")

    o_g, l_g = [], []
    for g, dil in enumerate(DIL_GROUPS):
        o, l = _dil_fwd(qkv[g], dil, s, "dil_fwd%d" % g)
        o_g.append(o)
        l_g.append(l)
    ya, lse = _dil_merge(o_g, l_g, s)
    b_br = mm(ya, wts["w_dil_out"], mode="nt", m=s, n=1024, k=512, tn=1024, tk=512, out_dtype=BF16, name="dil_out")

    def gate_mix(a, b, gr, ga, b0, b1):
        return [_sigmoid(gr.astype(F32) + b0) * a.astype(F32) + _sigmoid(ga.astype(F32) + b1) * b.astype(F32)], []

    (mixed,), _ = _rowwise("gate_mix", gate_mix, s, tr, [(a_br, 1024, 0), (b_br, 1024, 0), (proj_gate, 1024, 0), (proj_gate, 1024, 1)],
                           [vec["b0"], vec["b1"]], [(1024, BF16)])
    z = mm(mixed, wts["w_o"], mode="nn", m=s, n=1024, k=1024, tn=1024, tk=1024, out_dtype=BF16, name="w_o")

    def post_norm(h, f, g_post, g_pre):
        hn = h + _rms(f) * g_post
        return [hn, _rms(hn) * g_pre], []

    (h1, v2), _ = _rowwise("post_mix", post_norm, s, tr, [(xs, 1024, 0), (z, 1024, 0)], [vec["g_post_mix"], vec["g_pre_mlp"]],
                           [(1024, F32), (1024, BF16)])
    a_up = mm(v2, wts["w_up"], mode="nt", m=s, n=4096, k=1024, tn=1024, tk=1024, out_dtype=BF16, name="mlp_up")
    f_dn = mm(a_up, wts["w_down"], mode="nn", m=s, n=1024, k=4096, tn=1024, tk=2048, out_dtype=BF16, name="mlp_down", a_fn=_relu_sq)
    (h2, t_ple), _ = _rowwise("post_mlp", post_norm, s, tr, [(h1, 1024, 0), (f_dn, 1024, 0)], [vec["g_post_mlp"], vec["g_pre_ple"]],
                              [(1024, F32), (1024, BF16)])
    gl = mm(t_ple, wts["w_ple_gate"], mode="nn", m=s, n=1024, k=1024, tn=1024, tk=1024, out_dtype=BF16, name="ple_gate")
    e_ple = mm(pb, wts["w_ple_in"], mode="nt", m=s, n=1024, k=256, tn=1024, tk=256, out_dtype=BF16, name="ple_in")

    def ple_loss(h, glv, e, tg, b, g):
        gate = _sigmoid(glv + b)
        ge = gate * e
        diff = h + _rms(ge) * g - tg
        dy = diff * (1.0 / D_MODEL)
        d_ge, dg = _rms_bwd(ge, g, dy)
        d_gl = d_ge * e * gate * (1.0 - gate)
        loss = jnp.zeros((1, D_MODEL), F32) + 0.5 * jnp.sum(diff * diff) * (1.0 / D_MODEL)
        return [dy, d_gl, d_ge * gate], [_colsum(dg), _colsum(d_gl), loss]

    (dy, d_gl, d_e), (dg_post_ple, db_ple, loss) = _rowwise(
        "ple_loss", ple_loss, s, tr, [(h2, 1024, 0), (gl, 1024, 0), (e_ple, 1024, 0), (tgt, 1024, 0)],
        [vec["b_ple"], vec["g_post_ple"]], [(1024, F32), (1024, BF16), (1024, BF16)], [1024, 1024, 1024])

    ts, ts2 = min(1024, s), min(2048, s)
    wg = functools.partial(_matmul, mode="tn", k=s, tk=ts, out_dtype=BF16)
    grads = {}
    grads["w_ple_in"] = wg(d_e, pb, m=1024, n=256, tm=1024, tn=256, tk=ts2, name="g_ple_in")
    grads["w_ple_gate"] = wg(t_ple, d_gl, m=1024, n=1024, tm=1024, tn=1024, tk=ts2, name="g_ple_gate")
    d_t = mm(d_gl, wts["w_ple_gate"], mode="nt", m=s, n=1024, k=1024, tn=1024, tk=1024, out_dtype=BF16, name="d_t")

    def bwd_ple_mlp(h, dt, dyv, f, g_pre, g_post):
        dx, dg1 = _rms_bwd(h, g_pre, dt)
        dh = dyv + dx
        df, dg2 = _rms_bwd(f, g_post, dh)
        return [dh, df], [_colsum(dg1), _colsum(dg2)]

    (d_h2, d_f), (dg_pre_ple, dg_post_mlp) = _rowwise(
        "bwd_ple_mlp", bwd_ple_mlp, s, tr, [(h2, 1024, 0), (d_t, 1024, 0), (dy, 1024, 0), (f_dn, 1024, 0)],
        [vec["g_pre_ple"], vec["g_post_mlp"]], [(1024, F32), (1024, BF16)], [1024, 1024])
    d_a = mm(d_f, wts["w_down"], mode="nt", m=s, n=4096, k=1024, tn=1024, tk=1024, out_dtype=BF16, name="d_a",
             epi=(a_up,), epi_fn=lambda acc, av: acc * (2.0 * jnp.maximum(av.astype(F32), 0.0)))
    grads["w_down"] = wg(a_up, d_f, m=4096, n=1024, tm=2048, tn=1024, tk=ts2, name="g_down", a_fn=_relu_sq)
    grads["w_up"] = wg(d_a, v2, m=4096, n=1024, tm=2048, tn=1024, tk=ts2, name="g_up")
    d_v2 = mm(d_a, wts["w_up"], mode="nn", m=s, n=1024, k=4096, tn=1024, tk=2048, out_dtype=BF16, name="d_v2")

    (d_h1, d_z), (dg_pre_mlp, dg_post_mix) = _rowwise(
        "bwd_mlp_mix", bwd_ple_mlp, s, tr, [(h1, 1024, 0), (d_v2, 1024, 0), (d_h2, 1024, 0), (z, 1024, 0)],
        [vec["g_pre_mlp"], vec["g_post_mix"]], [(1024, F32), (1024, BF16)], [1024, 1024])
    d_mixed = mm(d_z, wts["w_o"], mode="nt", m=s, n=1024, k=1024, tn=1024, tk=1024, out_dtype=BF16, name="d_mixed")
    grads["w_o"] = wg(mixed, d_z, m=1024, n=1024, tm=1024, tn=1024, tk=ts2, name="g_o")

    def bwd_gate(dm, a, b, gr, ga, b0, b1):
        sa, sb = _sigmoid(gr.astype(F32) + b0), _sigmoid(ga.astype(F32) + b1)
        dgr = dm * a.astype(F32) * sa * (1.0 - sa)
        dga = dm * b.astype(F32) * sb * (1.0 - sb)
        return [dm * sa, dm * sb, jnp.concatenate([dgr, dga], axis=1)], [_colsum(dgr), _colsum(dga)]

    (d_abr, d_bbr, dproj_gate), (db0, db1) = _rowwise(
        "bwd_gate", bwd_gate, s, tr, [(d_mixed, 1024, 0), (a_br, 1024, 0), (b_br, 1024, 0), (proj_gate, 1024, 0), (proj_gate, 1024, 1)],
        [vec["b0"], vec["b1"]], [(1024, BF16), (1024, BF16), (2048, BF16)], [1024, 1024])
    grads["w_ret_out"] = wg(yr, d_abr, m=2048, n=1024, tm=2048, tn=1024, tk=ts2, name="g_ret_out")
    d_yr = mm(d_abr, wts["w_ret_out"], mode="nt", m=s, n=2048, k=1024, tn=1024, tk=1024, out_dtype=BF16, name="d_yr")
    grads["w_dil_out"] = wg(d_bbr, ya, m=1024, n=512, tm=1024, tn=512, tk=ts2, name="g_dil_out")
    d_ya = mm(d_bbr, wts["w_dil_out"], mode="nn", m=s, n=512, k=1024, tn=512, tk=1024, out_dtype=F32, name="d_ya")

    slots = {}
    names = list(grads) if on_mesh else []
    shares = _Exchange([blocks(grads[n]) for n in names], [True] * len(names)) if on_mesh else None
    (dproj_ret,), got = _ret_bwd(proj_ret, tabs["cos_r"], tabs["sin_r"], y_ret, d_yr, rstate, s, carry=shares)
    slots.update(zip(names, got))
    upstream = _dil_bwd_prep(d_ya, ya, lse, s)
    dqkv = [_dil_bwd(qkv[g], *upstream[g], *tabs["dil"][g], dil, s, "dil_bwd%d" % g)
            for g, dil in enumerate(DIL_GROUPS)]

    g_ret = wg(dproj_ret, u[0], m=6144, n=1024, tm=2048, tn=1024, tk=ts2, name="g_in_ret")
    g_gate = wg(dproj_gate, u[0], m=2048, n=1024, tm=2048, tn=1024, tk=ts2, name="g_in_gate")
    g_dil = [wg(dqkv[g], u[g], m=1536, n=1024, tm=1536, tn=1024, tk=ts2, name="g_in_dil%d" % g) for g in range(3)]
    grads["w_in"] = _join_w_in(g_ret, g_gate, g_dil)

    du_ret = functools.partial(mm, dproj_ret, w_ret, mode="nn", m=s, n=1024, k=6144, tn=1024, tk=1024, out_dtype=BF16, name="du_ret")
    if on_mesh:
        du_ret, (slots["w_in"],) = du_ret(carry=_Exchange([blocks(grads["w_in"])], [True]))
    else:
        du_ret = du_ret()
    du_gate = mm(dproj_gate, w_gate, mode="nn", m=s, n=1024, k=2048, tn=1024, tk=2048, out_dtype=BF16, name="du_gate")
    du_dil = [mm(dqkv[g], w_dil[g], mode="nn", m=s, n=1024, k=1536, tn=1024, tk=1536, out_dtype=BF16, name="du_dil%d" % g)
              for g in range(3)]

    grad_x, dg_pre_mix = _grad_x(xs, d_h1, (du_ret, du_gate, du_dil[0]), du_dil[1], du_dil[2], vec["g_pre_mix"], s)

    zero = jnp.zeros((1, D_MODEL), F32)
    packet = jnp.concatenate([dg_pre_mix, dg_post_mix, dg_pre_mlp, dg_post_mlp, dg_pre_ple, db_ple, dg_post_ple, loss,
                              db0, db1] + [zero] * 6, axis=0)
    return grad_x, (slots if on_mesh else grads), packet


def _mesh_pos():
    return lax.axis_index("x"), lax.axis_index("y"), lax.axis_index("c")


class _Exchange:
    def __init__(self, arrays, scatter):
        self.arrays, self.scatter, self.n = list(arrays), list(scatter), len(arrays)
        self.out_shape = [jax.ShapeDtypeStruct(a.shape if sc else (N_DEV,) + a.shape, a.dtype)
                          for a, sc in zip(self.arrays, self.scatter)]
        self.scratch = [pltpu.SemaphoreType.DMA((self.n * 7,)), pltpu.SemaphoreType.DMA((self.n * 7,)),
                        pltpu.SemaphoreType.DMA((self.n,))]
        self.specs = [pl.BlockSpec(memory_space=pl.ANY)] * self.n

    def _copies(self, srcs, dsts, sems):
        send_sems, recv_sems, local_sems = sems
        x, y, c = _mesh_pos()
        my = 4 * x + 2 * y + c
        src_of = lambda w, idx: srcs[w].at[idx] if self.scatter[w] else srcs[w]
        local = [pltpu.make_async_copy(src_of(w, my), dsts[w].at[my], local_sems.at[w]) for w in range(self.n)]
        sends, recvs = [], []
        for w in range(self.n):
            for r in range(1, N_DEV):
                px = 1 - x if r & 4 else x
                py = 1 - y if r & 2 else y
                pc = 1 - c if r & 1 else c
                pidx = 4 * px + 2 * py + pc
                kw = dict(send_sem=send_sems.at[w * 7 + r - 1], recv_sem=recv_sems.at[w * 7 + r - 1],
                          device_id=(px, py, pc), device_id_type=MESH)
                sends.append(pltpu.make_async_remote_copy(src_ref=src_of(w, pidx), dst_ref=dsts[w].at[my], **kw))
                recvs.append(pltpu.make_async_remote_copy(src_ref=src_of(w, pidx), dst_ref=dsts[w].at[pidx], **kw))
        return local, sends, recvs

    def start(self, srcs, dsts, sems):
        local, sends, _ = self._copies(srcs, dsts, sems)
        for cp in local + sends:
            cp.start()

    def wait(self, srcs, dsts, sems):
        local, sends, recvs = self._copies(srcs, dsts, sems)
        for cp in recvs:
            cp.wait_recv()
        for cp in sends:
            cp.wait_send()
        for cp in local:
            cp.wait()

    def split(self, refs, n_in, n_out):
        srcs = refs[n_in:n_in + self.n]
        dsts = refs[n_in + self.n + n_out:n_in + 2 * self.n + n_out]
        return srcs, dsts, refs[len(refs) - 3:]


class _TwoLevelGather(_Exchange):
    def __init__(self, arrays):
        super().__init__(arrays, [False] * len(arrays))

    def _plan(self, srcs, dsts, sems):
        send_sems, recv_sems, local_sems = sems
        x, y, c = _mesh_pos()
        me, sibling = (x, y, c), (x, y, 1 - c)
        chips = [(1 - x, y), (x, 1 - y), (1 - x, 1 - y)]
        region = lambda w, dev: dsts[w].at[4 * dev[0] + 2 * dev[1] + dev[2]]

        def copy(w, kk, block, to, src=None):
            return pltpu.make_async_remote_copy(
                src_ref=region(w, block) if src is None else src, dst_ref=region(w, block),
                send_sem=send_sems.at[w * 7 + kk], recv_sem=recv_sems.at[w * 7 + kk], device_id=to, device_id_type=MESH)

        mine = [pltpu.make_async_copy(srcs[w], region(w, me), local_sems.at[w]) for w in range(self.n)]
        first = []
        for w in range(self.n):
            first.append(copy(w, 0, me, sibling, src=srcs[w]))
            first += [copy(w, 1 + j, me, (*chip, c), src=srcs[w]) for j, chip in enumerate(chips)]
        return me, sibling, chips, c, copy, mine, first

    def start(self, srcs, dsts, sems):
        *_, mine, first = self._plan(srcs, dsts, sems)
        for cp in mine + first:
            cp.start()

    def wait(self, srcs, dsts, sems):
        me, sibling, chips, c, copy, mine, first = self._plan(srcs, dsts, sems)
        passed = []
        for j, chip in enumerate(chips):
            for w in range(self.n):
                copy(w, 1 + j, (*chip, c), me).wait_recv()
                cp = copy(w, 4 + j, (*chip, c), sibling)
                cp.start()
                passed.append(cp)
        for w in range(self.n):
            copy(w, 0, sibling, me).wait_recv()
            for j, chip in enumerate(chips):
                copy(w, 4 + j, (*chip, 1 - c), me).wait_recv()
        for cp in first + passed:
            cp.wait_send()
        for cp in mine:
            cp.wait()


def _run_exchange(ex, name):
    def body(*refs):
        parts = ex.split(refs, 0, 0)
        ex.start(*parts)
        ex.wait(*parts)

    return pl.pallas_call(body, name=name, in_specs=ex.specs, out_specs=ex.specs, out_shape=ex.out_shape,
                          scratch_shapes=ex.scratch)(*ex.arrays)


def _pick_rows(r, c, target_bytes):
    t = r
    while (t // 2) % 16 == 0 and t // 2 >= 16 and t * c * 4 > target_bytes:
        t //= 2
    return t


def _sum_slots(slots, name):
    ns, r, c = slots.shape
    tr = _pick_rows(r, c, 256 * 1024)

    def body(s_ref, o_ref):
        acc = s_ref[0].astype(F32)
        for kk in range(1, ns):
            acc = acc + s_ref[kk].astype(F32)
        o_ref[...] = acc

    return pl.pallas_call(
        body, name=name, grid=(r // tr,),
        in_specs=[pl.BlockSpec((ns, tr, c), lambda i: (0, i, 0))], out_specs=pl.BlockSpec((tr, c), lambda i: (i, 0)),
        out_shape=jax.ShapeDtypeStruct((r, c), F32), compiler_params=_cparams(("parallel",)),
    )(slots)


def _adamw(slots, w, m, v, name):
    ns, r, c = slots.shape
    tr = _pick_rows(r, c, 256 * 1024)

    def body(s_ref, w_ref, m_ref, v_ref, g_out, d_out, m_out, v_out):
        g = s_ref[0].astype(F32)
        for kk in range(1, ns):
            g = g + s_ref[kk].astype(F32)
        mn = ADAM_B1 * m_ref[...] + (1.0 - ADAM_B1) * g
        vn = ADAM_B2 * v_ref[...] + (1.0 - ADAM_B2) * (g * g)
        m_hat = mn / (1.0 - ADAM_B1 ** ADAM_STEP)
        v_hat = vn / (1.0 - ADAM_B2 ** ADAM_STEP)
        g_out[...] = g
        d_out[...] = -ADAM_LR * (m_hat / (jnp.sqrt(v_hat) + ADAM_EPS) + ADAM_WD * w_ref[...])
        m_out[...] = mn
        v_out[...] = vn

    blk = pl.BlockSpec((tr, c), lambda i: (i, 0))
    return pl.pallas_call(
        body, name=name, grid=(r // tr,),
        in_specs=[pl.BlockSpec((ns, tr, c), lambda i: (0, i, 0)), blk, blk, blk], out_specs=[blk] * 4,
        out_shape=[jax.ShapeDtypeStruct((r, c), F32)] * 4, compiler_params=_cparams(("parallel",)),
    )(slots, w, m, v)


def _rotary_tables(pos, s):
    posf = pos.astype(F32)
    inv_freq = 1.0 / (10000.0 ** jnp.linspace(0.0, 1.0, RET_QK // 2, dtype=F32))
    ang = posf[:, None] * inv_freq
    tabs = {"cos_r": jnp.cos(ang), "sin_r": jnp.sin(ang), "dil": []}
    freqs = 500000.0 ** (-jnp.arange(0, 16, 2, dtype=F32) / 16)
    spread = np.zeros((16, 384), np.float32)
    bias = np.zeros((1, 384), np.float32)
    for head in range(2):
        for i in range(8):
            spread[i, 64 * head + i] = spread[i, 64 * head + 8 + i] = 1.0
            spread[8 + i, 128 + 64 * head + i] = -1.0
            spread[8 + i, 256 + 64 * head + 8 + i] = 1.0
        bias[0, 64 * head + 16:64 * head + 64] = 1.0

    def expand(t, e, b):
        hi = t.astype(BF16)
        lo = (t - hi.astype(F32)).astype(BF16)
        out = _dot(hi, e, NN) + _dot(lo, e, NN) + b
        return [out[:, 0:128], out[:, 128:256], out[:, 256:384]], []

    for g, dil in enumerate(DIL_GROUPS):
        ang = posf.reshape(s // dil, dil).T.reshape(s, 1) * freqs
        cs = jnp.concatenate([jnp.cos(ang), jnp.sin(ang)], axis=1)
        t3, _ = _rowwise("rot_tables%d" % g, expand, s, min(1024, s), [(cs, 16, 0)],
                         [jnp.asarray(spread, BF16), jnp.asarray(bias)], [(128, F32)] * 3)
        tabs["dil"].append(tuple(t3))
    return tabs


_TRANSPOSED = ("w_in", "w_dil_out", "w_up", "w_ple_in")
_MATS = ("w_in", "w_ret_out", "w_dil_out", "w_o", "w_up", "w_down", "w_ple_gate", "w_ple_in")
_VECS = ("g_pre_mix", "g_post_mix", "g_pre_mlp", "g_post_mlp", "g_pre_ple", "b_ple_gate", "g_post_ple")
_ORDER = ("w_in", "b_gate", "w_ret_out", "w_dil_out", "w_o", "g_pre_mix", "g_post_mix", "g_pre_mlp", "g_post_mlp", "w_up",
          "w_down", "g_pre_ple", "w_ple_gate", "b_ple_gate", "w_ple_in", "g_post_ple")


def kernel(x, p, positions, w_in, b_gate, w_ret_out, w_dil_out, w_o, g_pre_mix, g_post_mix, g_pre_mlp, g_post_mlp, w_up, w_down, g_pre_ple, w_ple_gate, b_ple_gate, w_ple_in, g_post_ple, loss_target, m_w_in, m_b_gate, m_w_ret_out, m_w_dil_out, m_w_o, m_g_pre_mix, m_g_post_mix, m_g_pre_mlp, m_g_post_mlp, m_w_up, m_w_down, m_g_pre_ple, m_w_ple_gate, m_b_ple_gate, m_w_ple_in, m_g_post_ple, v_w_in, v_b_gate, v_w_ret_out, v_w_dil_out, v_w_o, v_g_pre_mix, v_g_post_mix, v_g_pre_mlp, v_g_post_mlp, v_w_up, v_w_down, v_g_pre_ple, v_w_ple_gate, v_b_ple_gate, v_w_ple_in, v_g_post_ple):
    s = x.shape[1]
    wd = dict(w_in=w_in, b_gate=b_gate, w_ret_out=w_ret_out, w_dil_out=w_dil_out, w_o=w_o, g_pre_mix=g_pre_mix,
              g_post_mix=g_post_mix, g_pre_mlp=g_pre_mlp, g_post_mlp=g_post_mlp, w_up=w_up, w_down=w_down,
              g_pre_ple=g_pre_ple, w_ple_gate=w_ple_gate, b_ple_gate=b_ple_gate, w_ple_in=w_ple_in, g_post_ple=g_post_ple)
    md = dict(w_in=m_w_in, b_gate=m_b_gate, w_ret_out=m_w_ret_out, w_dil_out=m_w_dil_out, w_o=m_w_o, g_pre_mix=m_g_pre_mix,
              g_post_mix=m_g_post_mix, g_pre_mlp=m_g_pre_mlp, g_post_mlp=m_g_post_mlp, w_up=m_w_up, w_down=m_w_down,
              g_pre_ple=m_g_pre_ple, w_ple_gate=m_w_ple_gate, b_ple_gate=m_b_ple_gate, w_ple_in=m_w_ple_in, g_post_ple=m_g_post_ple)
    vd = dict(w_in=v_w_in, b_gate=v_b_gate, w_ret_out=v_w_ret_out, w_dil_out=v_w_dil_out, w_o=v_w_o, g_pre_mix=v_g_pre_mix,
              g_post_mix=v_g_post_mix, g_pre_mlp=v_g_pre_mlp, g_post_mlp=v_g_post_mlp, w_up=v_w_up, w_down=v_w_down,
              g_pre_ple=v_g_pre_ple, w_ple_gate=v_w_ple_gate, b_ple_gate=v_b_ple_gate, w_ple_in=v_w_ple_in, g_post_ple=v_g_post_ple)

    shards = {n: (wd[n][0].T if n in _TRANSPOSED else wd[n][0]).astype(BF16) for n in _MATS}
    shards["b_gate"] = b_gate[0]
    vec = {n: wd[n] for n in _VECS}
    vec["b_ple"] = b_ple_gate

    tabs = _rotary_tables(positions[0], s)
    grad_x, slots, packet = _local_step(x[0], p[0, 0].astype(BF16), loss_target[0], tabs, {}, vec, s, shards=shards)

    (packets,) = _run_exchange(_Exchange([packet], [False]), "exchange_vectors")
    out = {}
    for n in _MATS:
        sl = slots[n]
        if n in _TRANSPOSED:
            sl = _sum_slots(sl, "sum_" + n).T[None]
        out[n] = _adamw(sl, wd[n][0], md[n][0], vd[n][0], "adamw_" + n)
    zero_rows = jnp.zeros((16 - len(_VECS), D_MODEL), F32)
    pack = lambda d: jnp.concatenate([d[n] for n in _VECS] + [zero_rows], axis=0)
    small = _adamw(packets, pack(wd), pack(md), pack(vd), "adamw_vectors")
    for i, n in enumerate(_VECS):
        out[n] = tuple(t[i:i + 1] for t in small)
    my = 4 * lax.axis_index("x") + 2 * lax.axis_index("y") + lax.axis_index("c")
    g_bias = lax.dynamic_slice(small[0], (8, my * 128), (2, 128))
    out["b_gate"] = _adamw(g_bias[None], b_gate[0], m_b_gate[0], v_b_gate[0], "adamw_b_gate")
    loss = small[0][7, 0]

    res = [loss, grad_x[None]]
    for kk in range(4):
        res += [out[n][kk][None] if out[n][kk].ndim == 2 and wd[n].ndim == 3 else out[n][kk] for n in _ORDER]
    return tuple(res)
```

```python
import functools
import math

import numpy as np
import jax
import jax.numpy as jnp
from jax import lax
from jax.experimental import pallas as pl
from jax.experimental.pallas import tpu as pltpu

F32, BF16 = jnp.float32, jnp.bfloat16
D_MODEL = 1024
EPS = 1e-6
N_DEV = 8
RET_HEADS, RET_QK, RET_V, RET_CHUNK = 4, 256, 512, 128
DIL_GROUPS = (1, 4, 16)
DIL_W = 512
QB = 128
NEG = -1e30
ADAM_LR, ADAM_B1, ADAM_B2, ADAM_EPS, ADAM_WD, ADAM_STEP = 0.001, 0.9, 0.999, 1e-08, 0.01, 10
VMEM_LIMIT_BYTES = 56 * 1024 * 1024
MESH = pl.DeviceIdType.MESH

NN = ((1,), (0,))
NT = ((1,), (1,))
TN = ((0,), (0,))


def _dot(a, b, dn):
    return lax.dot_general(a, b, (dn, ((), ())), preferred_element_type=F32)


def _cparams(sem):
    return pltpu.CompilerParams(dimension_semantics=sem, vmem_limit_bytes=VMEM_LIMIT_BYTES)


def _rms(x):
    return x * lax.rsqrt(jnp.mean(x * x, axis=-1, keepdims=True) + EPS)


def _rms_bwd(x, g, dy):
    r = lax.rsqrt(jnp.mean(x * x, axis=-1, keepdims=True) + EPS)
    xh = x * r
    t = dy * g
    dx = r * (t - xh * jnp.mean(t * xh, axis=-1, keepdims=True))
    return dx, dy * xh


def _colsum(v):
    return jnp.sum(v, axis=0, keepdims=True)


def _sigmoid(v):
    return 1.0 / (1.0 + jnp.exp(-v))


def _pallas(compute, *, name, grid, in_specs, out_specs, out_shape, scratch, semantics, args, carry=None):
    n_in, n_out = len(in_specs), len(out_specs)
    if carry is None:
        res = pl.pallas_call(compute, name=name, grid=grid, in_specs=in_specs, out_specs=out_specs, out_shape=out_shape,
                             scratch_shapes=scratch, compiler_params=_cparams(semantics))(*args)
        return res, []
    n_steps = math.prod(grid)

    def body(*refs):
        step = 0
        for axis, size in enumerate(grid):
            step = step * size + pl.program_id(axis)
        parts = carry.split(refs, n_in, n_out)
        pl.when(step == 0)(lambda: carry.start(*parts))
        compute(*refs[:n_in], *refs[n_in + carry.n:n_in + carry.n + n_out], *refs[n_in + 2 * carry.n + n_out:len(refs) - 3])
        pl.when(step == n_steps - 1)(lambda: carry.wait(*parts))

    res = pl.pallas_call(
        body, name=name, grid=grid, in_specs=list(in_specs) + carry.specs, out_specs=list(out_specs) + carry.specs,
        out_shape=list(out_shape) + carry.out_shape, scratch_shapes=list(scratch) + carry.scratch,
        compiler_params=_cparams(("arbitrary",) * len(grid)))(*args, *carry.arrays)
    return res[:n_out], res[n_out:]


def _matmul(a, b, *, mode, m, n, k, tm, tn, tk, out_dtype, name, a_fn=None, epi=(), epi_width=None, epi_fn=None, carry=None):
    nk = k // tk
    grid = (m // tm, n // tn, nk)
    if mode == "nn":
        a_blk, a_im, b_blk, b_im, dn = (tm, tk), (lambda i, j, kk: (i, kk)), (tk, tn), (lambda i, j, kk: (kk, j)), NN
    elif mode == "nt":
        a_blk, a_im, b_blk, b_im, dn = (tm, tk), (lambda i, j, kk: (i, kk)), (tn, tk), (lambda i, j, kk: (j, kk)), NT
    else:
        a_blk, a_im, b_blk, b_im, dn = (tk, tm), (lambda i, j, kk: (kk, i)), (tk, tn), (lambda i, j, kk: (kk, j)), TN
    o_im = lambda i, j, kk: (i, j)
    n_in = 2 + len(epi)

    def body(*refs):
        a_ref, b_ref = refs[0], refs[1]
        o_ref = refs[n_in]
        acc_ref = refs[n_in + 1] if nk > 1 else None

        def finish(acc):
            if epi:
                acc = epi_fn(acc, *[r[...] for r in refs[2:n_in]])
            o_ref[...] = acc.astype(o_ref.dtype)

        av = a_ref[...]
        if a_fn is not None:
            av = a_fn(av)
        part = _dot(av, b_ref[...], dn)
        if nk == 1:
            finish(part)
        else:
            kk = pl.program_id(2)

            @pl.when(kk == 0)
            def _():
                acc_ref[...] = part

            @pl.when(kk > 0)
            def _():
                acc_ref[...] += part

            @pl.when(kk == nk - 1)
            def _():
                finish(acc_ref[...])

    epi_spec = pl.BlockSpec((tm, tn), o_im) if epi_width is None else pl.BlockSpec((tm, epi_width), lambda i, j, kk: (i, 0))
    in_specs = [pl.BlockSpec(a_blk, a_im), pl.BlockSpec(b_blk, b_im)] + [epi_spec] * len(epi)
    args = [a, b, *epi]
    (out,), got = _pallas(
        body, name=name, grid=grid, in_specs=in_specs, out_specs=[pl.BlockSpec((tm, tn), o_im)],
        out_shape=[jax.ShapeDtypeStruct((m, n), out_dtype)], scratch=[pltpu.VMEM((tm, tn), F32)] if nk > 1 else [],
        semantics=("parallel", "parallel", "arbitrary"), args=args, carry=carry)
    return out if carry is None else (out, got)


def _relu_sq(v):
    r = jnp.maximum(v, jnp.zeros_like(v))
    return r * r


def _rowwise(name, fn, s, tr, rows, vecs, outs, accs=()):
    n_r, n_v, n_o, n_a = len(rows), len(vecs), len(outs), len(accs)

    def body(*refs):
        vals = [refs[i][...].astype(F32) for i in range(n_r)] + [refs[n_r + i][...] for i in range(n_v)]
        o_refs = refs[n_r + n_v:n_r + n_v + n_o]
        a_refs = refs[n_r + n_v + n_o:]
        o_vals, a_vals = fn(*vals)
        for ref, val in zip(o_refs, o_vals):
            ref[...] = val.astype(ref.dtype)
        if n_a:
            @pl.when(pl.program_id(0) == 0)
            def _():
                for ref in a_refs:
                    ref[...] = jnp.zeros_like(ref)

            for ref, val in zip(a_refs, a_vals):
                ref[...] += val

    in_specs = [pl.BlockSpec((tr, w), functools.partial(lambda i, cb: (i, cb), cb=cb)) for _, w, cb in rows]
    in_specs += [pl.BlockSpec(v.shape, lambda i: (0, 0)) for v in vecs]
    out_specs = [pl.BlockSpec((tr, w), lambda i: (i, 0)) for w, _ in outs]
    out_specs += [pl.BlockSpec((1, w), lambda i: (0, 0)) for w in accs]
    out_shape = [jax.ShapeDtypeStruct((s, w), dt) for w, dt in outs]
    out_shape += [jax.ShapeDtypeStruct((1, w), F32) for w in accs]
    res = pl.pallas_call(
        body, name=name, grid=(s // tr,), in_specs=in_specs, out_specs=out_specs, out_shape=out_shape,
        compiler_params=_cparams(("arbitrary",)),
    )(*[r[0] for r in rows], *vecs)
    return res[:n_o], res[n_o:]


_ROW_TILE = 512
_STREAM_SPECS = [pl.BlockSpec((dil, _ROW_TILE // dil, D_MODEL), lambda i: (0, i, 0)) for dil in DIL_GROUPS[1:]]
_NAT_SPEC = pl.BlockSpec((_ROW_TILE, D_MODEL), lambda i: (i, 0))
_VEC_SPEC = pl.BlockSpec((1, D_MODEL), lambda i: (0, 0))
_COL_BLOCKS = pltpu.VMEM((D_MODEL // 128, _ROW_TILE, 128), F32)


def _prenorm(xs, g, tabs, s, carry=None):
    tr = _ROW_TILE
    n_g = len(DIL_GROUPS)

    def body(x_ref, g_ref, *rest):
        cs_refs, (e_ref, b_ref), (u_ref, u4_ref, u16_ref) = rest[:n_g], rest[n_g:n_g + 2], rest[n_g + 2:n_g + 5]
        tab_refs, buf = rest[n_g + 5:n_g + 5 + 3 * n_g], rest[-1]
        xn = _rms(x_ref[...]) * g_ref[...]
        u_ref[...] = xn.astype(BF16)
        for cb in range(8):
            buf[cb] = xn[:, cb * 128:(cb + 1) * 128]
        for dil, out in ((4, u4_ref), (16, u16_ref)):
            for c in range(dil):
                rows = pl.ds(c, tr // dil, stride=dil)
                out[c] = jnp.concatenate([buf.at[cb][rows, :] for cb in range(8)], axis=1).astype(BF16)
        for gi in range(n_g):
            for ref, val in zip(tab_refs[3 * gi:3 * gi + 3], _spread_rotary(cs_refs[gi][...], e_ref[...], b_ref[...])):
                ref[...] = val

    row = lambda w: pl.BlockSpec((tr, w), lambda i: (i, 0))
    whole = lambda a: pl.BlockSpec(a.shape, lambda i: (0, 0))
    res, got = _pallas(
        body, name="prenorm", grid=(s // tr,),
        in_specs=[_NAT_SPEC, _VEC_SPEC] + [row(16)] * n_g + [whole(tabs["spread"]), whole(tabs["bias"])],
        out_specs=[_NAT_SPEC] + _STREAM_SPECS + [row(128)] * (3 * n_g),
        out_shape=[jax.ShapeDtypeStruct((s, D_MODEL), BF16)]
        + [jax.ShapeDtypeStruct((dil, s // dil, D_MODEL), BF16) for dil in DIL_GROUPS[1:]]
        + [jax.ShapeDtypeStruct((s, 128), F32)] * (3 * n_g),
        scratch=[_COL_BLOCKS], semantics=("parallel",), args=(xs, g, *tabs["dil_cs"], tabs["spread"], tabs["bias"]), carry=carry)
    return [r.reshape(s, D_MODEL) for r in res[:3]], [tuple(res[3 + 3 * gi:6 + 3 * gi]) for gi in range(n_g)], got


def _grad_x(xs, d_h1, du_nat, du4, du16, g, s):
    tr = _ROW_TILE

    def body(x_ref, dh_ref, a_ref, b_ref, c_ref, u4_ref, u16_ref, g_ref, dx_ref, dg_ref, buf):
        du = a_ref[...].astype(F32) + b_ref[...].astype(F32) + c_ref[...].astype(F32)
        for dil, src in ((4, u4_ref), (16, u16_ref)):
            for c in range(dil):
                part = src[c].astype(F32)
                for cb in range(8):
                    buf.at[cb][pl.ds(c, tr // dil, stride=dil), :] = part[:, cb * 128:(cb + 1) * 128]
            du = du + jnp.concatenate([buf[cb] for cb in range(8)], axis=1)
        dx, dgr = _rms_bwd(x_ref[...], g_ref[...], du)
        dx_ref[...] = dh_ref[...] + dx

        @pl.when(pl.program_id(0) == 0)
        def _():
            dg_ref[...] = jnp.zeros_like(dg_ref)

        dg_ref[...] += _colsum(dgr)

    return pl.pallas_call(
        body, name="grad_x", grid=(s // tr,), in_specs=[_NAT_SPEC] * 5 + _STREAM_SPECS + [_VEC_SPEC],
        out_specs=[_NAT_SPEC, _VEC_SPEC],
        out_shape=[jax.ShapeDtypeStruct((s, D_MODEL), F32), jax.ShapeDtypeStruct((1, D_MODEL), F32)],
        scratch_shapes=[_COL_BLOCKS], compiler_params=_cparams(("arbitrary",)),
    )(xs, d_h1, *du_nat, du4.reshape(4, s // 4, D_MODEL), du16.reshape(16, s // 16, D_MODEL), g)


def _ret_tables():
    h = np.arange(RET_HEADS, dtype=np.float32)
    lg = np.log1p(-(np.float32(2.0) ** (-5.0 - h))).astype(np.float32)
    idx = np.arange(RET_CHUNK, dtype=np.float32)
    diff = idx[:, None] - idx[None, :]
    dm = np.where(diff[None] >= 0, np.exp(np.maximum(diff, 0.0)[None] * lg[:, None, None]), 0.0)
    qd = np.exp((idx + 1.0)[None, :, None] * lg[:, None, None])
    kd = np.exp((RET_CHUNK - 1.0 - idx)[None, :, None] * lg[:, None, None])
    cd = np.exp(RET_CHUNK * lg)[:, None, None]
    return [jnp.asarray(t, F32) for t in (dm, qd, kd, cd)]


def _rope_half(v, cos, sin):
    v1, v2 = v[:, :128], v[:, 128:]
    return jnp.concatenate([v1 * cos - v2 * sin, v2 * cos + v1 * sin], axis=1)


def _unrope_half(d, cos, sin):
    d1, d2 = d[:, :128], d[:, 128:]
    return jnp.concatenate([d1 * cos + d2 * sin, d2 * cos - d1 * sin], axis=1)


_RET_HEADS_FWD, _RET_HEADS_BWD = 1, 2


def _ret_specs(rb, rev_n, hp):
    def rowmap(w_blk):
        return lambda h, n: (rev_n(n), w_blk(h))
    tab = [pl.BlockSpec((hp, RET_CHUNK, RET_CHUNK), lambda h, n: (h, 0, 0)),
           pl.BlockSpec((hp, RET_CHUNK, 1), lambda h, n: (h, 0, 0)),
           pl.BlockSpec((hp, RET_CHUNK, 1), lambda h, n: (h, 0, 0)),
           pl.BlockSpec((hp, 1, 1), lambda h, n: (h, 0, 0))]
    proj = pl.BlockSpec((rb, hp * 1536), rowmap(lambda h: h))
    cs = pl.BlockSpec((rb, 128), rowmap(lambda h: 0))
    hv = pl.BlockSpec((rb, hp * RET_V), rowmap(lambda h: h))
    return proj, cs, hv, tab


def _ret_fwd(proj_ret, cos, sin, s, carry=None):
    rb = min(512, s)
    ch = rb // RET_CHUNK
    nb = s // rb
    hp = _RET_HEADS_FWD
    proj_spec, cs_spec, hv_spec, tab_specs = _ret_specs(rb, lambda n: n, hp)

    def body(p_ref, cos_ref, sin_ref, dm_ref, qd_ref, kd_ref, cd_ref, yr_ref, y_ref, rs_ref, r_acc):
        @pl.when(pl.program_id(1) == 0)
        def _():
            r_acc[...] = jnp.zeros_like(r_acc)

        for c, hh in [(c, hh) for c in range(ch) for hh in range(hp)]:
            rows = slice(c * RET_CHUNK, (c + 1) * RET_CHUNK)
            pc, hc = hh * 1536, hh * RET_V
            dm, qd, kd, cd = dm_ref[hh], qd_ref[hh], kd_ref[hh], cd_ref[hh]
            cosv, sinv = cos_ref[rows, :], sin_ref[rows, :]
            q = _rope_half(p_ref[rows, pc:pc + 256].astype(F32), cosv, sinv)
            kk = _rope_half(p_ref[rows, pc + 256:pc + 512].astype(F32), cosv, sinv) * (RET_QK ** -0.5)
            v = p_ref[rows, pc + 512:pc + 1024]
            g = p_ref[rows, pc + 1024:pc + 1536].astype(F32)
            rb16 = r_acc[hh].astype(BF16)
            rs_ref[hh, c] = rb16
            sc = _dot(q.astype(BF16), kk.astype(BF16), NT) * dm
            y = _dot(sc.astype(BF16), v, NN) + _dot((q * qd).astype(BF16), rb16, NN)
            r_acc[hh] = r_acc[hh] * cd + _dot((kk * kd).astype(BF16), v, TN)
            y_ref[rows, hc:hc + RET_V] = y.astype(BF16)
            yr_ref[rows, hc:hc + RET_V] = (_rms(y) * (g * _sigmoid(g))).astype(BF16)

    return _pallas(
        body, name="ret_fwd", grid=(RET_HEADS // hp, nb),
        in_specs=[proj_spec, cs_spec, cs_spec] + tab_specs,
        out_specs=[hv_spec, hv_spec, pl.BlockSpec((hp, ch, RET_QK, RET_V), lambda h, n: (h, n, 0, 0))],
        out_shape=[jax.ShapeDtypeStruct((s, RET_HEADS * RET_V), BF16), jax.ShapeDtypeStruct((s, RET_HEADS * RET_V), BF16),
                   jax.ShapeDtypeStruct((RET_HEADS, s // RET_CHUNK, RET_QK, RET_V), BF16)],
        scratch=[pltpu.VMEM((hp, RET_QK, RET_V), F32)], semantics=("parallel", "arbitrary"),
        args=(proj_ret, cos, sin, *_ret_tables()), carry=carry)


def _ret_bwd(proj_ret, cos, sin, y, d_yr, rs, s, carry=None):
    rb = min(512, s)
    ch = rb // RET_CHUNK
    nb = s // rb
    hp = _RET_HEADS_BWD
    proj_spec, cs_spec, hv_spec, tab_specs = _ret_specs(rb, lambda n: nb - 1 - n, hp)

    def body(p_ref, cos_ref, sin_ref, y_ref, dyr_ref, rs_ref, dm_ref, qd_ref, kd_ref, cd_ref, o_ref, dr_acc):
        @pl.when(pl.program_id(1) == 0)
        def _():
            dr_acc[...] = jnp.zeros_like(dr_acc)

        for c, hh in [(c, hh) for c in reversed(range(ch)) for hh in range(hp)]:
            rows = slice(c * RET_CHUNK, (c + 1) * RET_CHUNK)
            pc, hc = hh * 1536, hh * RET_V
            dm, qd, kd, cd = dm_ref[hh], qd_ref[hh], kd_ref[hh], cd_ref[hh]
            cosv, sinv = cos_ref[rows, :], sin_ref[rows, :]
            q = _rope_half(p_ref[rows, pc:pc + 256].astype(F32), cosv, sinv)
            kk = _rope_half(p_ref[rows, pc + 256:pc + 512].astype(F32), cosv, sinv) * (RET_QK ** -0.5)
            v = p_ref[rows, pc + 512:pc + 1024]
            g = p_ref[rows, pc + 1024:pc + 1536].astype(F32)
            yv = y_ref[rows, hc:hc + RET_V].astype(F32)
            dyr = dyr_ref[rows, hc:hc + RET_V].astype(F32)
            sg = _sigmoid(g)
            r = lax.rsqrt(jnp.mean(yv * yv, axis=-1, keepdims=True) + EPS)
            yn = yv * r
            dg = dyr * yn * (sg * (1.0 + g * (1.0 - sg)))
            dyn = dyr * (g * sg)
            dy = (r * (dyn - yn * jnp.mean(dyn * yn, axis=-1, keepdims=True))).astype(BF16)
            qb, kb = q.astype(BF16), kk.astype(BF16)
            rb16 = rs_ref[hh, c]
            drb = dr_acc[hh].astype(BF16)
            sd = _dot(qb, kb, NT) * dm
            ds = (_dot(dy, v, NT) * dm).astype(BF16)
            dq = _dot(ds, kb, NN) + qd * _dot(dy, rb16, NT)
            dk = _dot(ds, qb, TN) + kd * _dot(v, drb, NT)
            dv = _dot(sd.astype(BF16), dy, TN) + _dot((kk * kd).astype(BF16), drb, NN)
            dr_acc[hh] = dr_acc[hh] * cd + _dot((q * qd).astype(BF16), dy, TN)
            o_ref[rows, pc:pc + 256] = _unrope_half(dq, cosv, sinv).astype(BF16)
            o_ref[rows, pc + 256:pc + 512] = (_unrope_half(dk, cosv, sinv) * (RET_QK ** -0.5)).astype(BF16)
            o_ref[rows, pc + 512:pc + 1024] = dv.astype(BF16)
            o_ref[rows, pc + 1024:pc + 1536] = dg.astype(BF16)

    in_specs = [proj_spec, cs_spec, cs_spec, hv_spec, hv_spec,
                pl.BlockSpec((hp, ch, RET_QK, RET_V), lambda h, n: (h, nb - 1 - n, 0, 0))] + tab_specs
    return _pallas(
        body, name="ret_bwd", grid=(RET_HEADS // hp, nb), in_specs=in_specs, out_specs=[proj_spec],
        out_shape=[jax.ShapeDtypeStruct((s, RET_HEADS * 1536), BF16)], scratch=[pltpu.VMEM((hp, RET_QK, RET_V), F32)],
        semantics=("parallel", "arbitrary"), args=(proj_ret, cos, sin, y, d_yr, rs, *_ret_tables()), carry=carry)


def _rope_qk(acc, c, s1, s2):
    outs = []
    for cc in range(8):
        vv = acc[:, cc * 128:(cc + 1) * 128]
        outs.append(vv * c + pltpu.roll(vv, 120, 1) * s1 + pltpu.roll(vv, 8, 1) * s2)
    return jnp.concatenate(outs + [acc[:, 2 * DIL_W:]], axis=1)


def _pair_masks(keys_on_rows=False):
    ri = lax.broadcasted_iota(jnp.int32, (2 * QB, 2 * QB), 1 if keys_on_rows else 0)
    ci = lax.broadcasted_iota(jnp.int32, (2 * QB, 2 * QB), 0 if keys_on_rows else 1)
    e = ci - (ri & (QB - 1))
    lane_lo = lax.broadcasted_iota(jnp.int32, (2 * QB, 128), 1) < 64
    return ci, jnp.logical_and(e >= 0, e <= QB), lane_lo


def _stack_heads(v, lane_lo):
    z = jnp.zeros_like(v)
    return jnp.concatenate([jnp.where(lane_lo, v, z), jnp.where(lane_lo, z, v)], axis=0)


def _dil_fwd(qkv, dil, s, name):
    length = s // dil
    rb = min(512, length)
    nsub = rb // QB
    nbs = length // rb
    sub_per = rb // QB

    def body(q_ref, k_ref, v_ref, kp_ref, vp_ref, o_ref, l_ref):
        first = (pl.program_id(0) % nbs) == 0
        ci, band, lane_lo = _pair_masks()
        lo1 = lane_lo[0:QB]

        for i in range(nsub):
            rows = slice(i * QB, (i + 1) * QB)
            mask = jnp.logical_and(band, ci >= jnp.where(first, QB, 0)) if i == 0 else band
            for j in range(4):
                lanes = slice(j * 128, (j + 1) * 128)
                q2 = _stack_heads(q_ref[rows, lanes], lo1)
                if i == 0:
                    k2 = jnp.concatenate([kp_ref[:, lanes], k_ref[rows, lanes]], axis=0)
                    v2 = jnp.concatenate([vp_ref[:, lanes], v_ref[rows, lanes]], axis=0)
                else:
                    k2, v2 = k_ref[(i - 1) * QB:(i + 1) * QB, lanes], v_ref[(i - 1) * QB:(i + 1) * QB, lanes]
                v2 = _stack_heads(v2, lane_lo)
                sc = jnp.where(mask, _dot(q2, k2, NT) * 0.125, NEG)
                m = jnp.max(sc, axis=1, keepdims=True)
                p = jnp.exp(sc - m)
                den = jnp.sum(p, axis=1, keepdims=True)
                pb = p.astype(BF16)
                o = _dot(jnp.concatenate([pb[0:QB], pb[QB:]], axis=1), v2, NN)
                inv = 1.0 / den
                lse = m + jnp.log(den)
                o_ref[rows, lanes] = o * jnp.where(lo1, inv[0:QB], inv[QB:])
                l_ref[rows, lanes] = jnp.where(lo1, lse[0:QB], lse[QB:])

    prev = lambda n: jnp.maximum(n * sub_per - 1, 0)
    cur = lambda cb: (lambda n: (n, cb))
    return pl.pallas_call(
        body, name=name, grid=(s // rb,),
        in_specs=[pl.BlockSpec((rb, DIL_W), cur(0)), pl.BlockSpec((rb, DIL_W), cur(1)), pl.BlockSpec((rb, DIL_W), cur(2)),
                  pl.BlockSpec((QB, DIL_W), lambda n: (prev(n), 1)), pl.BlockSpec((QB, DIL_W), lambda n: (prev(n), 2))],
        out_specs=[pl.BlockSpec((rb, DIL_W), cur(0)), pl.BlockSpec((rb, DIL_W), cur(0))],
        out_shape=[jax.ShapeDtypeStruct((s, DIL_W), F32), jax.ShapeDtypeStruct((s, DIL_W), F32)],
        compiler_params=_cparams(("parallel",)),
    )(qkv, qkv, qkv, qkv, qkv)


def _dil_bwd(qkv, dya, lse, dlt, tc, ts1, ts2, dil, s, name):
    length = s // dil
    rb = min(512, length)
    nsub = rb // QB
    nbs = length // rb
    last_blk = s // QB - 1

    def body(q_ref, k_ref, v_ref, kp_ref, vp_ref, qn_ref, dy_ref, dyn_ref, l_ref, ln_ref, d_ref, dn_ref,
             c_ref, s1_ref, s2_ref, o_ref, dka, dva):
        nl = pl.program_id(0) % nbs
        first, last = nl == 0, nl == nbs - 1
        ci, band, lane_lo = _pair_masks(keys_on_rows=True)
        lo1 = lane_lo[0:QB]

        def unrope(d, rows):
            return d * c_ref[rows, :] + pltpu.roll(d * s1_ref[rows, :], 8, 1) + pltpu.roll(d * s2_ref[rows, :], 120, 1)

        for qi in range(nsub + 1):
            nxt = qi == nsub
            rows = slice((nsub - 1) * QB, nsub * QB) if nxt else slice(qi * QB, (qi + 1) * QB)
            prev_rows = slice((qi - 1) * QB, qi * QB)
            if qi == 0:
                mask = jnp.logical_and(band, ci >= jnp.where(first, QB, 0))
            elif nxt:
                mask = jnp.logical_and(band, ci <= jnp.where(last, -1, QB - 1))[0:QB, :]
            else:
                mask = band
            for j in range(4):
                lanes = slice(j * 128, (j + 1) * 128)
                if nxt:
                    q, do, lv, dl = qn_ref[:, lanes], dyn_ref[:, lanes], ln_ref[:, lanes], dn_ref[:, lanes]
                    k2, v2 = k_ref[prev_rows, lanes], v_ref[prev_rows, lanes]
                else:
                    q, do, lv, dl = q_ref[rows, lanes], dy_ref[rows, lanes], l_ref[rows, lanes], d_ref[rows, lanes]
                    if qi == 0:
                        k2 = jnp.concatenate([kp_ref[:, lanes], k_ref[rows, lanes]], axis=0)
                        v2 = jnp.concatenate([vp_ref[:, lanes], v_ref[rows, lanes]], axis=0)
                    else:
                        k2, v2 = k_ref[(qi - 1) * QB:(qi + 1) * QB, lanes], v_ref[(qi - 1) * QB:(qi + 1) * QB, lanes]
                q2, do2 = _stack_heads(q, lo1), _stack_heads(do, lo1)
                lt, dt = lv.T, dl.T
                lse2 = jnp.concatenate([lt[0:1], lt[64:65]], axis=1)
                dl2 = jnp.concatenate([dt[0:1], dt[64:65]], axis=1)
                sc = _dot(k2, q2, NT) * 0.125
                p = jnp.where(mask, jnp.exp(jnp.minimum(sc - lse2, 0.0)), 0.0)
                ds = (p * (_dot(v2, do2, NT) - dl2) * 0.125).astype(BF16)
                dk2 = _dot(ds, q2, NN)
                dv2 = _dot(p.astype(BF16), do2, NN)
                if qi >= 1:
                    dka[prev_rows, lanes] += dk2[0:QB]
                    dva[prev_rows, lanes] += dv2[0:QB]
                if not nxt:
                    dka[rows, lanes] = dk2[QB:]
                    dva[rows, lanes] = dv2[QB:]
                    dq = _dot(jnp.concatenate([ds[:, 0:QB], ds[:, QB:]], axis=0), _stack_heads(k2, lane_lo), TN)
                    o_ref[rows, lanes] = unrope(dq, rows).astype(BF16)

        for cc in range(4):
            lanes = slice(cc * 128, (cc + 1) * 128)
            o_ref[:, 512 + cc * 128:512 + (cc + 1) * 128] = unrope(dka[:, lanes], slice(None)).astype(BF16)
            o_ref[:, 1024 + cc * 128:1024 + (cc + 1) * 128] = dva[:, lanes].astype(BF16)

    prev = lambda n: jnp.maximum(n * nsub - 1, 0)
    nxt = lambda n: jnp.minimum(n * nsub + nsub, last_blk)
    cur = lambda cb: (lambda n: (n, cb))
    big = lambda cb: pl.BlockSpec((rb, DIL_W), cur(cb))
    small = lambda im: pl.BlockSpec((QB, DIL_W), im)
    tab = pl.BlockSpec((rb, 128), cur(0))
    return pl.pallas_call(
        body, name=name, grid=(s // rb,),
        in_specs=[big(0), big(1), big(2), small(lambda n: (prev(n), 1)), small(lambda n: (prev(n), 2)),
                  small(lambda n: (nxt(n), 0)), big(0), small(lambda n: (nxt(n), 0)), big(0), small(lambda n: (nxt(n), 0)),
                  big(0), small(lambda n: (nxt(n), 0)), tab, tab, tab],
        out_specs=pl.BlockSpec((rb, 3 * DIL_W), cur(0)),
        out_shape=jax.ShapeDtypeStruct((s, 3 * DIL_W), BF16),
        scratch_shapes=[pltpu.VMEM((rb, DIL_W), F32), pltpu.VMEM((rb, DIL_W), F32)],
        compiler_params=_cparams(("parallel",)),
    )(qkv, qkv, qkv, qkv, qkv, qkv, dya, dya, lse, lse, dlt, dlt, tc, ts1, ts2)


def _stream_specs(tr):
    nat = pl.BlockSpec((tr, 128), lambda i, j: (i, j))
    return [nat] + [pl.BlockSpec((dil, tr // dil, 128), lambda i, j: (0, i, j)) for dil in DIL_GROUPS[1:]]


def _dil_merge(o_g, l_g, s):
    tr = min(2048, s)
    nat, sp4, sp16 = _stream_specs(tr)

    def body(o0_ref, l0_ref, o1_ref, l1_ref, o2_ref, l2_ref, ya_ref, lse_ref, o1n, l1n, o2n, l2n):
        for src, dst, dil in ((o1_ref, o1n, 4), (l1_ref, l1n, 4), (o2_ref, o2n, 16), (l2_ref, l2n, 16)):
            for c in range(dil):
                dst[pl.ds(c, tr // dil, stride=dil), :] = src[c]
        l0, l1, l2 = l0_ref[...], l1n[...], l2n[...]
        m = jnp.maximum(jnp.maximum(l0, l1), l2)
        e0, e1, e2 = jnp.exp(l0 - m), jnp.exp(l1 - m), jnp.exp(l2 - m)
        den = e0 + e1 + e2
        ya_ref[...] = ((e0 * o0_ref[...] + e1 * o1n[...] + e2 * o2n[...]) / den).astype(BF16)
        lse_ref[...] = m + jnp.log(den)

    v3 = lambda a, dil: a.reshape(dil, s // dil, DIL_W)
    return pl.pallas_call(
        body, name="dil_merge", grid=(s // tr, 4),
        in_specs=[nat, nat, sp4, sp4, sp16, sp16], out_specs=[nat, nat],
        out_shape=[jax.ShapeDtypeStruct((s, DIL_W), BF16), jax.ShapeDtypeStruct((s, DIL_W), F32)],
        scratch_shapes=[pltpu.VMEM((tr, 128), F32)] * 4,
        compiler_params=_cparams(("parallel", "parallel")),
    )(o_g[0], l_g[0], v3(o_g[1], 4), v3(l_g[1], 4), v3(o_g[2], 16), v3(l_g[2], 16))


def _dil_bwd_prep(d_ya, ya, lse, s):
    tr = min(2048, s)
    nat, sp4, sp16 = _stream_specs(tr)

    def body(dya_ref, ya_ref, lse_ref, dy0, dl0, dy1, ls1, dl1, dy2, ls2, dl2, dlt):
        lane_lo = lax.broadcasted_iota(jnp.int32, (tr, 128), 1) < 64
        prod = dya_ref[...] * ya_ref[...].astype(F32)
        lo = jnp.where(lane_lo, prod, 0.0)
        dlt[...] = jnp.where(lane_lo, jnp.sum(lo, axis=1, keepdims=True), jnp.sum(prod - lo, axis=1, keepdims=True))
        dy0[...] = dya_ref[...].astype(BF16)
        dl0[...] = dlt[...]
        for dil, dy, ls, dl in ((4, dy1, ls1, dl1), (16, dy2, ls2, dl2)):
            for c in range(dil):
                rows = pl.ds(c, tr // dil, stride=dil)
                dy[c] = dya_ref[rows, :].astype(BF16)
                ls[c] = lse_ref[rows, :]
                dl[c] = dlt[rows, :]

    sh = lambda dil, dt: jax.ShapeDtypeStruct((dil, s // dil, DIL_W), dt)
    res = pl.pallas_call(
        body, name="dil_bwd_prep", grid=(s // tr, 4),
        in_specs=[nat, nat, nat], out_specs=[nat, nat, sp4, sp4, sp4, sp16, sp16, sp16],
        out_shape=[jax.ShapeDtypeStruct((s, DIL_W), BF16), jax.ShapeDtypeStruct((s, DIL_W), F32),
                   sh(4, BF16), sh(4, F32), sh(4, F32), sh(16, BF16), sh(16, F32), sh(16, F32)],
        scratch_shapes=[pltpu.VMEM((tr, 128), F32)],
        compiler_params=_cparams(("parallel", "parallel")),
    )(d_ya, ya, lse)
    dy0, dl0, dy1, ls1, dl1, dy2, ls2, dl2 = [r.reshape(s, DIL_W) for r in res]
    return [(dy0, lse, dl0), (dy1, ls1, dl1), (dy2, ls2, dl2)]


_RET_SEGS = ((0, 256), (1024, 256), (2048, 512), (4096, 512))


def _split_w_in(win):
    per_head = [win[a:a + RET_HEADS * n].reshape(RET_HEADS, n, D_MODEL) for a, n in _RET_SEGS]
    w_ret = jnp.concatenate(per_head, axis=1).reshape(RET_HEADS * 1536, D_MODEL)
    w_dil = [jnp.concatenate([win[a + DIL_W * g:a + DIL_W * (g + 1)] for a in (6144, 7680, 9216)], axis=0) for g in range(3)]
    return w_ret, win[10752:12800], w_dil


def _join_w_in(g_ret, g_gate, g_dil):
    g_ret = g_ret.reshape(RET_HEADS, 1536, D_MODEL)
    off = (0, 256, 512, 1024, 1536)
    parts = [g_ret[:, off[i]:off[i + 1]].reshape(-1, D_MODEL) for i in range(4)]
    dil = [g_dil[g][DIL_W * i:DIL_W * (i + 1)] for i in range(3) for g in range(3)]
    return jnp.concatenate(parts + dil + [g_gate], axis=0)


def _local_step(xs, pb, tgt, tabs, wts, vec, s, shards=None):
    tm = min(2048, s)
    tr = min(512, s)
    mm = functools.partial(_matmul, tm=tm)
    on_mesh = shards is not None
    wts, vec = dict(wts), dict(vec)
    blocks = lambda g: g.reshape(N_DEV, g.shape[0] // N_DEV, g.shape[1])

    late_shards = dict(shards) if on_mesh else {}
    first = _TwoLevelGather([late_shards.pop("w_in"), late_shards.pop("b_gate")]) if on_mesh else None
    u, rot, gathered = _prenorm(xs, vec["g_pre_mix"], tabs, s, carry=first)
    if on_mesh:
        wts["w_in"] = gathered[0].reshape(N_DEV * gathered[0].shape[1], D_MODEL)
        bias = gathered[1].transpose(1, 0, 2).reshape(2, D_MODEL)
        vec.update(b0=bias[0:1], b1=bias[1:2])
    w_ret, w_gate, w_dil = _split_w_in(wts["w_in"])
    proj_ret = mm(u[0], w_ret, mode="nt", m=s, n=6144, k=1024, tn=1024, tk=1024, out_dtype=BF16, name="inproj_ret")
    proj_gate = mm(u[0], w_gate, mode="nt", m=s, n=2048, k=1024, tn=1024, tk=1024, out_dtype=BF16, name="inproj_gate")
    qkv = [_matmul(u[g], w_dil[g], mode="nt", m=s, n=1536, k=1024, tm=min(1024, s), tn=1536, tk=1024, out_dtype=BF16,
                   name="inproj_dil%d" % g, epi=rot[g], epi_width=128, epi_fn=_rope_qk) for g in range(3)]

    names = list(late_shards) if on_mesh else []
    gather = _TwoLevelGather([late_shards[n] for n in names]) if on_mesh else None
    (yr, y_ret, rstate), gathered = _ret_fwd(proj_ret, tabs["cos_r"], tabs["sin_r"], s, carry=gather)
    wts.update({n: g.reshape(N_DEV * g.shape[1], g.shape[2]) for n, g in zip(names, gathered)})
    a_br = mm(yr, wts["w_ret_out"], mode="nn", m=s, n=1024, k=2048, tn=1024, tk=2048, out_dtype=BF16, name="ret_out")

    o_g, l_g = [], []
    for g, dil in enumerate(DIL_GROUPS):
        o, l = _dil_fwd(qkv[g], dil, s, "dil_fwd%d" % g)
        o_g.append(o)
        l_g.append(l)
    ya, lse = _dil_merge(o_g, l_g, s)
    b_br = mm(ya, wts["w_dil_out"], mode="nt", m=s, n=1024, k=512, tn=1024, tk=512, out_dtype=BF16, name="dil_out")

    def gate_mix(a, b, gr, ga, b0, b1):
        return [_sigmoid(gr.astype(F32) + b0) * a.astype(F32) + _sigmoid(ga.astype(F32) + b1) * b.astype(F32)], []

    (mixed,), _ = _rowwise("gate_mix", gate_mix, s, tr, [(a_br, 1024, 0), (b_br, 1024, 0), (proj_gate, 1024, 0), (proj_gate, 1024, 1)],
                           [vec["b0"], vec["b1"]], [(1024, BF16)])
    z = mm(mixed, wts["w_o"], mode="nn", m=s, n=1024, k=1024, tn=1024, tk=1024, out_dtype=BF16, name="w_o")

    def post_norm(h, f, g_post, g_pre):
        hn = h + _rms(f) * g_post
        return [hn, _rms(hn) * g_pre], []

    (h1, v2), _ = _rowwise("post_mix", post_norm, s, tr, [(xs, 1024, 0), (z, 1024, 0)], [vec["g_post_mix"], vec["g_pre_mlp"]],
                           [(1024, F32), (1024, BF16)])
    a_up = mm(v2, wts["w_up"], mode="nt", m=s, n=4096, k=1024, tn=1024, tk=1024, out_dtype=BF16, name="mlp_up")
    f_dn = mm(a_up, wts["w_down"], mode="nn", m=s, n=1024, k=4096, tn=1024, tk=2048, out_dtype=BF16, name="mlp_down", a_fn=_relu_sq)
    (h2, t_ple), _ = _rowwise("post_mlp", post_norm, s, tr, [(h1, 1024, 0), (f_dn, 1024, 0)], [vec["g_post_mlp"], vec["g_pre_ple"]],
                              [(1024, F32), (1024, BF16)])
    gl = mm(t_ple, wts["w_ple_gate"], mode="nn", m=s, n=1024, k=1024, tn=1024, tk=1024, out_dtype=BF16, name="ple_gate")
    e_ple = mm(pb, wts["w_ple_in"], mode="nt", m=s, n=1024, k=256, tn=1024, tk=256, out_dtype=BF16, name="ple_in")

    def ple_loss(h, glv, e, tg, b, g):
        gate = _sigmoid(glv + b)
        ge = gate * e
        diff = h + _rms(ge) * g - tg
        dy = diff * (1.0 / D_MODEL)
        d_ge, dg = _rms_bwd(ge, g, dy)
        d_gl = d_ge * e * gate * (1.0 - gate)
        loss = jnp.zeros((1, D_MODEL), F32) + 0.5 * jnp.sum(diff * diff) * (1.0 / D_MODEL)
        return [dy, d_gl, d_ge * gate], [_colsum(dg), _colsum(d_gl), loss]

    (dy, d_gl, d_e), (dg_post_ple, db_ple, loss) = _rowwise(
        "ple_loss", ple_loss, s, tr, [(h2, 1024, 0), (gl, 1024, 0), (e_ple, 1024, 0), (tgt, 1024, 0)],
        [vec["b_ple"], vec["g_post_ple"]], [(1024, F32), (1024, BF16), (1024, BF16)], [1024, 1024, 1024])

    ts, ts2 = min(1024, s), min(2048, s)
    wg = functools.partial(_matmul, mode="tn", k=s, tk=ts, out_dtype=BF16)
    grads = {}
    grads["w_ple_in"] = wg(d_e, pb, m=1024, n=256, tm=1024, tn=256, tk=ts2, name="g_ple_in")
    grads["w_ple_gate"] = wg(t_ple, d_gl, m=1024, n=1024, tm=1024, tn=1024, tk=ts2, name="g_ple_gate")
    d_t = mm(d_gl, wts["w_ple_gate"], mode="nt", m=s, n=1024, k=1024, tn=1024, tk=1024, out_dtype=BF16, name="d_t")

    def bwd_ple_mlp(h, dt, dyv, f, g_pre, g_post):
        dx, dg1 = _rms_bwd(h, g_pre, dt)
        dh = dyv + dx
        df, dg2 = _rms_bwd(f, g_post, dh)
        return [dh, df], [_colsum(dg1), _colsum(dg2)]

    (d_h2, d_f), (dg_pre_ple, dg_post_mlp) = _rowwise(
        "bwd_ple_mlp", bwd_ple_mlp, s, tr, [(h2, 1024, 0), (d_t, 1024, 0), (dy, 1024, 0), (f_dn, 1024, 0)],
        [vec["g_pre_ple"], vec["g_post_mlp"]], [(1024, F32), (1024, BF16)], [1024, 1024])
    d_a = mm(d_f, wts["w_down"], mode="nt", m=s, n=4096, k=1024, tn=1024, tk=1024, out_dtype=BF16, name="d_a",
             epi=(a_up,), epi_fn=lambda acc, av: acc * (2.0 * jnp.maximum(av.astype(F32), 0.0)))
    grads["w_down"] = wg(a_up, d_f, m=4096, n=1024, tm=2048, tn=1024, tk=ts2, name="g_down", a_fn=_relu_sq)
    grads["w_up"] = wg(d_a, v2, m=4096, n=1024, tm=2048, tn=1024, tk=ts2, name="g_up")
    d_v2 = mm(d_a, wts["w_up"], mode="nn", m=s, n=1024, k=4096, tn=1024, tk=2048, out_dtype=BF16, name="d_v2")

    (d_h1, d_z), (dg_pre_mlp, dg_post_mix) = _rowwise(
        "bwd_mlp_mix", bwd_ple_mlp, s, tr, [(h1, 1024, 0), (d_v2, 1024, 0), (d_h2, 1024, 0), (z, 1024, 0)],
        [vec["g_pre_mlp"], vec["g_post_mix"]], [(1024, F32), (1024, BF16)], [1024, 1024])
    d_mixed = mm(d_z, wts["w_o"], mode="nt", m=s, n=1024, k=1024, tn=1024, tk=1024, out_dtype=BF16, name="d_mixed")
    grads["w_o"] = wg(mixed, d_z, m=1024, n=1024, tm=1024, tn=1024, tk=ts2, name="g_o")

    def bwd_gate(dm, a, b, gr, ga, b0, b1):
        sa, sb = _sigmoid(gr.astype(F32) + b0), _sigmoid(ga.astype(F32) + b1)
        dgr = dm * a.astype(F32) * sa * (1.0 - sa)
        dga = dm * b.astype(F32) * sb * (1.0 - sb)
        return [dm * sa, dm * sb, jnp.concatenate([dgr, dga], axis=1)], [_colsum(dgr), _colsum(dga)]

    (d_abr, d_bbr, dproj_gate), (db0, db1) = _rowwise(
        "bwd_gate", bwd_gate, s, tr, [(d_mixed, 1024, 0), (a_br, 1024, 0), (b_br, 1024, 0), (proj_gate, 1024, 0), (proj_gate, 1024, 1)],
        [vec["b0"], vec["b1"]], [(1024, BF16), (1024, BF16), (2048, BF16)], [1024, 1024])
    grads["w_ret_out"] = wg(yr, d_abr, m=2048, n=1024, tm=2048, tn=1024, tk=ts2, name="g_ret_out")
    d_yr = mm(d_abr, wts["w_ret_out"], mode="nt", m=s, n=2048, k=1024, tn=1024, tk=1024, out_dtype=BF16, name="d_yr")
    grads["w_dil_out"] = wg(d_bbr, ya, m=1024, n=512, tm=1024, tn=512, tk=ts2, name="g_dil_out")
    d_ya = mm(d_bbr, wts["w_dil_out"], mode="nn", m=s, n=512, k=1024, tn=512, tk=1024, out_dtype=F32, name="d_ya")

    slots = {}
    names = list(grads) if on_mesh else []
    shares = _Exchange([blocks(grads[n]) for n in names], [True] * len(names)) if on_mesh else None
    (dproj_ret,), got = _ret_bwd(proj_ret, tabs["cos_r"], tabs["sin_r"], y_ret, d_yr, rstate, s, carry=shares)
    slots.update(zip(names, got))
    upstream = _dil_bwd_prep(d_ya, ya, lse, s)
    dqkv = [_dil_bwd(qkv[g], *upstream[g], *rot[g], dil, s, "dil_bwd%d" % g)
            for g, dil in enumerate(DIL_GROUPS)]

    g_ret = wg(dproj_ret, u[0], m=6144, n=1024, tm=2048, tn=1024, tk=ts2, name="g_in_ret")
    g_gate = wg(dproj_gate, u[0], m=2048, n=1024, tm=2048, tn=1024, tk=ts2, name="g_in_gate")
    g_dil = [wg(dqkv[g], u[g], m=1536, n=1024, tm=1536, tn=1024, tk=ts2, name="g_in_dil%d" % g) for g in range(3)]
    grads["w_in"] = _join_w_in(g_ret, g_gate, g_dil)

    du_ret = functools.partial(mm, dproj_ret, w_ret, mode="nn", m=s, n=1024, k=6144, tn=1024, tk=1024, out_dtype=BF16, name="du_ret")
    if on_mesh:
        du_ret, (slots["w_in"],) = du_ret(carry=_Exchange([blocks(grads["w_in"])], [True]))
    else:
        du_ret = du_ret()
    du_gate = mm(dproj_gate, w_gate, mode="nn", m=s, n=1024, k=2048, tn=1024, tk=2048, out_dtype=BF16, name="du_gate")
    du_dil = [mm(dqkv[g], w_dil[g], mode="nn", m=s, n=1024, k=1536, tn=1024, tk=1536, out_dtype=BF16, name="du_dil%d" % g)
              for g in range(3)]

    grad_x, dg_pre_mix = _grad_x(xs, d_h1, (du_ret, du_gate, du_dil[0]), du_dil[1], du_dil[2], vec["g_pre_mix"], s)

    zero = jnp.zeros((1, D_MODEL), F32)
    packet = jnp.concatenate([dg_pre_mix, dg_post_mix, dg_pre_mlp, dg_post_mlp, dg_pre_ple, db_ple, dg_post_ple, loss,
                              db0, db1] + [zero] * 6, axis=0)
    return grad_x, (slots if on_mesh else grads), packet


def _mesh_pos():
    return lax.axis_index("x"), lax.axis_index("y"), lax.axis_index("c")


class _Exchange:
    def __init__(self, arrays, scatter):
        self.arrays, self.scatter, self.n = list(arrays), list(scatter), len(arrays)
        self.out_shape = [jax.ShapeDtypeStruct(a.shape if sc else (N_DEV,) + a.shape, a.dtype)
                          for a, sc in zip(self.arrays, self.scatter)]
        self.scratch = [pltpu.SemaphoreType.DMA((self.n * 7,)), pltpu.SemaphoreType.DMA((self.n * 7,)),
                        pltpu.SemaphoreType.DMA((self.n,))]
        self.specs = [pl.BlockSpec(memory_space=pl.ANY)] * self.n

    def _copies(self, srcs, dsts, sems):
        send_sems, recv_sems, local_sems = sems
        x, y, c = _mesh_pos()
        my = 4 * x + 2 * y + c
        src_of = lambda w, idx: srcs[w].at[idx] if self.scatter[w] else srcs[w]
        local = [pltpu.make_async_copy(src_of(w, my), dsts[w].at[my], local_sems.at[w]) for w in range(self.n)]
        sends, recvs = [], []
        for w in range(self.n):
            for r in range(1, N_DEV):
                px = 1 - x if r & 4 else x
                py = 1 - y if r & 2 else y
                pc = 1 - c if r & 1 else c
                pidx = 4 * px + 2 * py + pc
                kw = dict(send_sem=send_sems.at[w * 7 + r - 1], recv_sem=recv_sems.at[w * 7 + r - 1],
                          device_id=(px, py, pc), device_id_type=MESH)
                sends.append(pltpu.make_async_remote_copy(src_ref=src_of(w, pidx), dst_ref=dsts[w].at[my], **kw))
                recvs.append(pltpu.make_async_remote_copy(src_ref=src_of(w, pidx), dst_ref=dsts[w].at[pidx], **kw))
        return local, sends, recvs

    def start(self, srcs, dsts, sems):
        local, sends, _ = self._copies(srcs, dsts, sems)
        for cp in local + sends:
            cp.start()

    def wait(self, srcs, dsts, sems):
        local, sends, recvs = self._copies(srcs, dsts, sems)
        for cp in recvs:
            cp.wait_recv()
        for cp in sends:
            cp.wait_send()
        for cp in local:
            cp.wait()

    def split(self, refs, n_in, n_out):
        srcs = refs[n_in:n_in + self.n]
        dsts = refs[n_in + self.n + n_out:n_in + 2 * self.n + n_out]
        return srcs, dsts, refs[len(refs) - 3:]


class _TwoLevelGather(_Exchange):
    def __init__(self, arrays):
        super().__init__(arrays, [False] * len(arrays))

    def _plan(self, srcs, dsts, sems):
        send_sems, recv_sems, local_sems = sems
        x, y, c = _mesh_pos()
        me, sibling = (x, y, c), (x, y, 1 - c)
        chips = [(1 - x, y), (x, 1 - y), (1 - x, 1 - y)]
        region = lambda w, dev: dsts[w].at[4 * dev[0] + 2 * dev[1] + dev[2]]

        def copy(w, kk, block, to, src=None):
            return pltpu.make_async_remote_copy(
                src_ref=region(w, block) if src is None else src, dst_ref=region(w, block),
                send_sem=send_sems.at[w * 7 + kk], recv_sem=recv_sems.at[w * 7 + kk], device_id=to, device_id_type=MESH)

        mine = [pltpu.make_async_copy(srcs[w], region(w, me), local_sems.at[w]) for w in range(self.n)]
        first = []
        for w in range(self.n):
            first.append(copy(w, 0, me, sibling, src=srcs[w]))
            first += [copy(w, 1 + j, me, (*chip, c), src=srcs[w]) for j, chip in enumerate(chips)]
        return me, sibling, chips, c, copy, mine, first

    def start(self, srcs, dsts, sems):
        *_, mine, first = self._plan(srcs, dsts, sems)
        for cp in mine + first:
            cp.start()

    def wait(self, srcs, dsts, sems):
        me, sibling, chips, c, copy, mine, first = self._plan(srcs, dsts, sems)
        passed = []
        for j, chip in enumerate(chips):
            for w in range(self.n):
                copy(w, 1 + j, (*chip, c), me).wait_recv()
                cp = copy(w, 4 + j, (*chip, c), sibling)
                cp.start()
                passed.append(cp)
        for w in range(self.n):
            copy(w, 0, sibling, me).wait_recv()
            for j, chip in enumerate(chips):
                copy(w, 4 + j, (*chip, 1 - c), me).wait_recv()
        for cp in first + passed:
            cp.wait_send()
        for cp in mine:
            cp.wait()


def _run_exchange(ex, name):
    def body(*refs):
        parts = ex.split(refs, 0, 0)
        ex.start(*parts)
        ex.wait(*parts)

    return pl.pallas_call(body, name=name, in_specs=ex.specs, out_specs=ex.specs, out_shape=ex.out_shape,
                          scratch_shapes=ex.scratch)(*ex.arrays)


def _pick_rows(r, c, target_bytes):
    t = r
    while (t // 2) % 16 == 0 and t // 2 >= 16 and t * c * 4 > target_bytes:
        t //= 2
    return t


def _sum_slots(slots, name):
    ns, r, c = slots.shape
    tr = _pick_rows(r, c, 256 * 1024)

    def body(s_ref, o_ref):
        acc = s_ref[0].astype(F32)
        for kk in range(1, ns):
            acc = acc + s_ref[kk].astype(F32)
        o_ref[...] = acc

    return pl.pallas_call(
        body, name=name, grid=(r // tr,),
        in_specs=[pl.BlockSpec((ns, tr, c), lambda i: (0, i, 0))], out_specs=pl.BlockSpec((tr, c), lambda i: (i, 0)),
        out_shape=jax.ShapeDtypeStruct((r, c), F32), compiler_params=_cparams(("parallel",)),
    )(slots)


def _adamw(slots, w, m, v, name):
    ns, r, c = slots.shape
    tr = _pick_rows(r, c, 256 * 1024)

    def body(s_ref, w_ref, m_ref, v_ref, g_out, d_out, m_out, v_out):
        g = s_ref[0].astype(F32)
        for kk in range(1, ns):
            g = g + s_ref[kk].astype(F32)
        mn = ADAM_B1 * m_ref[...] + (1.0 - ADAM_B1) * g
        vn = ADAM_B2 * v_ref[...] + (1.0 - ADAM_B2) * (g * g)
        m_hat = mn / (1.0 - ADAM_B1 ** ADAM_STEP)
        v_hat = vn / (1.0 - ADAM_B2 ** ADAM_STEP)
        g_out[...] = g
        d_out[...] = -ADAM_LR * (m_hat / (jnp.sqrt(v_hat) + ADAM_EPS) + ADAM_WD * w_ref[...])
        m_out[...] = mn
        v_out[...] = vn

    blk = pl.BlockSpec((tr, c), lambda i: (i, 0))
    return pl.pallas_call(
        body, name=name, grid=(r // tr,),
        in_specs=[pl.BlockSpec((ns, tr, c), lambda i: (0, i, 0)), blk, blk, blk], out_specs=[blk] * 4,
        out_shape=[jax.ShapeDtypeStruct((r, c), F32)] * 4, compiler_params=_cparams(("parallel",)),
    )(slots, w, m, v)


def _rotary_tables(pos, s):
    posf = pos.astype(F32)
    inv_freq = 1.0 / (10000.0 ** jnp.linspace(0.0, 1.0, RET_QK // 2, dtype=F32))
    ang = posf[:, None] * inv_freq
    tabs = {"cos_r": jnp.cos(ang), "sin_r": jnp.sin(ang), "dil_cs": []}
    freqs = 500000.0 ** (-jnp.arange(0, 16, 2, dtype=F32) / 16)
    spread = np.zeros((16, 384), np.float32)
    bias = np.zeros((1, 384), np.float32)
    for head in range(2):
        for i in range(8):
            spread[i, 64 * head + i] = spread[i, 64 * head + 8 + i] = 1.0
            spread[8 + i, 128 + 64 * head + i] = -1.0
            spread[8 + i, 256 + 64 * head + 8 + i] = 1.0
        bias[0, 64 * head + 16:64 * head + 64] = 1.0

    for dil in DIL_GROUPS:
        ang = posf.reshape(s // dil, dil).T.reshape(s, 1) * freqs
        tabs["dil_cs"].append(jnp.concatenate([jnp.cos(ang), jnp.sin(ang)], axis=1))
    tabs["spread"], tabs["bias"] = jnp.asarray(spread, BF16), jnp.asarray(bias)
    return tabs


def _spread_rotary(t, e, b):
    hi = t.astype(BF16)
    lo = (t - hi.astype(F32)).astype(BF16)
    out = _dot(hi, e, NN) + _dot(lo, e, NN) + b
    return out[:, 0:128], out[:, 128:256], out[:, 256:384]


_TRANSPOSED = ("w_in", "w_dil_out", "w_up", "w_ple_in")
_MATS = ("w_in", "w_ret_out", "w_dil_out", "w_o", "w_up", "w_down", "w_ple_gate", "w_ple_in")
_VECS = ("g_pre_mix", "g_post_mix", "g_pre_mlp", "g_post_mlp", "g_pre_ple", "b_ple_gate", "g_post_ple")
_ORDER = ("w_in", "b_gate", "w_ret_out", "w_dil_out", "w_o", "g_pre_mix", "g_post_mix", "g_pre_mlp", "g_post_mlp", "w_up",
          "w_down", "g_pre_ple", "w_ple_gate", "b_ple_gate", "w_ple_in", "g_post_ple")


def kernel(x, p, positions, w_in, b_gate, w_ret_out, w_dil_out, w_o, g_pre_mix, g_post_mix, g_pre_mlp, g_post_mlp, w_up, w_down, g_pre_ple, w_ple_gate, b_ple_gate, w_ple_in, g_post_ple, loss_target, m_w_in, m_b_gate, m_w_ret_out, m_w_dil_out, m_w_o, m_g_pre_mix, m_g_post_mix, m_g_pre_mlp, m_g_post_mlp, m_w_up, m_w_down, m_g_pre_ple, m_w_ple_gate, m_b_ple_gate, m_w_ple_in, m_g_post_ple, v_w_in, v_b_gate, v_w_ret_out, v_w_dil_out, v_w_o, v_g_pre_mix, v_g_post_mix, v_g_pre_mlp, v_g_post_mlp, v_w_up, v_w_down, v_g_pre_ple, v_w_ple_gate, v_b_ple_gate, v_w_ple_in, v_g_post_ple):
    s = x.shape[1]
    wd = dict(w_in=w_in, b_gate=b_gate, w_ret_out=w_ret_out, w_dil_out=w_dil_out, w_o=w_o, g_pre_mix=g_pre_mix,
              g_post_mix=g_post_mix, g_pre_mlp=g_pre_mlp, g_post_mlp=g_post_mlp, w_up=w_up, w_down=w_down,
              g_pre_ple=g_pre_ple, w_ple_gate=w_ple_gate, b_ple_gate=b_ple_gate, w_ple_in=w_ple_in, g_post_ple=g_post_ple)
    md = dict(w_in=m_w_in, b_gate=m_b_gate, w_ret_out=m_w_ret_out, w_dil_out=m_w_dil_out, w_o=m_w_o, g_pre_mix=m_g_pre_mix,
              g_post_mix=m_g_post_mix, g_pre_mlp=m_g_pre_mlp, g_post_mlp=m_g_post_mlp, w_up=m_w_up, w_down=m_w_down,
              g_pre_ple=m_g_pre_ple, w_ple_gate=m_w_ple_gate, b_ple_gate=m_b_ple_gate, w_ple_in=m_w_ple_in, g_post_ple=m_g_post_ple)
    vd = dict(w_in=v_w_in, b_gate=v_b_gate, w_ret_out=v_w_ret_out, w_dil_out=v_w_dil_out, w_o=v_w_o, g_pre_mix=v_g_pre_mix,
              g_post_mix=v_g_post_mix, g_pre_mlp=v_g_pre_mlp, g_post_mlp=v_g_post_mlp, w_up=v_w_up, w_down=v_w_down,
              g_pre_ple=v_g_pre_ple, w_ple_gate=v_w_ple_gate, b_ple_gate=v_b_ple_gate, w_ple_in=v_w_ple_in, g_post_ple=v_g_post_ple)

    shards = {n: (wd[n][0].T if n in _TRANSPOSED else wd[n][0]).astype(BF16) for n in _MATS}
    shards["b_gate"] = b_gate[0]
    vec = {n: wd[n] for n in _VECS}
    vec["b_ple"] = b_ple_gate

    tabs = _rotary_tables(positions[0], s)
    grad_x, slots, packet = _local_step(x[0], p[0, 0].astype(BF16), loss_target[0], tabs, {}, vec, s, shards=shards)

    (packets,) = _run_exchange(_Exchange([packet], [False]), "exchange_vectors")
    out = {}
    for n in _MATS:
        sl = slots[n]
        if n in _TRANSPOSED:
            sl = _sum_slots(sl, "sum_" + n).T[None]
        out[n] = _adamw(sl, wd[n][0], md[n][0], vd[n][0], "adamw_" + n)
    zero_rows = jnp.zeros((16 - len(_VECS), D_MODEL), F32)
    pack = lambda d: jnp.concatenate([d[n] for n in _VECS] + [zero_rows], axis=0)
    small = _adamw(packets, pack(wd), pack(md), pack(vd), "adamw_vectors")
    for i, n in enumerate(_VECS):
        out[n] = tuple(t[i:i + 1] for t in small)
    my = 4 * lax.axis_index("x") + 2 * lax.axis_index("y") + lax.axis_index("c")
    g_bias = lax.dynamic_slice(small[0], (8, my * 128), (2, 128))
    out["b_gate"] = _adamw(g_bias[None], b_gate[0], m_b_gate[0], v_b_gate[0], "adamw_b_gate")
    loss = small[0][7, 0]

    res = [loss, grad_x[None]]
    for kk in range(4):
        res += [out[n][kk][None] if out[n][kk].ndim == 2 and wd[n].ndim == 3 else out[n][kk] for n in _ORDER]
    return tuple(res)
```

```python
import functools
import math

import numpy as np
import jax
import jax.numpy as jnp
from jax import lax
from jax.experimental import pallas as pl
from jax.experimental.pallas import tpu as pltpu

F32, BF16 = jnp.float32, jnp.bfloat16
D_MODEL = 1024
EPS = 1e-6
N_DEV = 8
RET_HEADS, RET_QK, RET_V, RET_CHUNK = 4, 256, 512, 128
DIL_GROUPS = (1, 4, 16)
DIL_W = 512
QB = 128
NEG = -1e30
ADAM_LR, ADAM_B1, ADAM_B2, ADAM_EPS, ADAM_WD, ADAM_STEP = 0.001, 0.9, 0.999, 1e-08, 0.01, 10
VMEM_LIMIT_BYTES = 56 * 1024 * 1024
MESH = pl.DeviceIdType.MESH

NN = ((1,), (0,))
NT = ((1,), (1,))
TN = ((0,), (0,))


def _dot(a, b, dn):
    return lax.dot_general(a, b, (dn, ((), ())), preferred_element_type=F32)


def _cparams(sem):
    return pltpu.CompilerParams(dimension_semantics=sem, vmem_limit_bytes=VMEM_LIMIT_BYTES)


def _rms(x):
    return x * lax.rsqrt(jnp.mean(x * x, axis=-1, keepdims=True) + EPS)


def _rms_bwd(x, g, dy):
    r = lax.rsqrt(jnp.mean(x * x, axis=-1, keepdims=True) + EPS)
    xh = x * r
    t = dy * g
    dx = r * (t - xh * jnp.mean(t * xh, axis=-1, keepdims=True))
    return dx, dy * xh


def _colsum(v):
    return jnp.sum(v, axis=0, keepdims=True)


def _sigmoid(v):
    return 1.0 / (1.0 + jnp.exp(-v))


def _pallas(compute, *, name, grid, in_specs, out_specs, out_shape, scratch, semantics, args, carry=None):
    n_in, n_out = len(in_specs), len(out_specs)
    if carry is None:
        res = pl.pallas_call(compute, name=name, grid=grid, in_specs=in_specs, out_specs=out_specs, out_shape=out_shape,
                             scratch_shapes=scratch, compiler_params=_cparams(semantics))(*args)
        return res, []
    n_steps = math.prod(grid)

    def body(*refs):
        step = 0
        for axis, size in enumerate(grid):
            step = step * size + pl.program_id(axis)
        parts = carry.split(refs, n_in, n_out)
        pl.when(step == 0)(lambda: carry.start(*parts))
        compute(*refs[:n_in], *refs[n_in + carry.n:n_in + carry.n + n_out], *refs[n_in + 2 * carry.n + n_out:len(refs) - 3])
        pl.when(step == n_steps - 1)(lambda: carry.wait(*parts))

    res = pl.pallas_call(
        body, name=name, grid=grid, in_specs=list(in_specs) + carry.specs, out_specs=list(out_specs) + carry.specs,
        out_shape=list(out_shape) + carry.out_shape, scratch_shapes=list(scratch) + carry.scratch,
        compiler_params=_cparams(("arbitrary",) * len(grid)))(*args, *carry.arrays)
    return res[:n_out], res[n_out:]


def _matmul(a, b, *, mode, m, n, k, tm, tn, tk, out_dtype, name, a_fn=None, epi=(), epi_width=None, epi_fn=None, carry=None):
    nk = k // tk
    grid = (m // tm, n // tn, nk)
    if mode == "nn":
        a_blk, a_im, b_blk, b_im, dn = (tm, tk), (lambda i, j, kk: (i, kk)), (tk, tn), (lambda i, j, kk: (kk, j)), NN
    elif mode == "nt":
        a_blk, a_im, b_blk, b_im, dn = (tm, tk), (lambda i, j, kk: (i, kk)), (tn, tk), (lambda i, j, kk: (j, kk)), NT
    else:
        a_blk, a_im, b_blk, b_im, dn = (tk, tm), (lambda i, j, kk: (kk, i)), (tk, tn), (lambda i, j, kk: (kk, j)), TN
    o_im = lambda i, j, kk: (i, j)
    n_in = 2 + len(epi)

    def body(*refs):
        a_ref, b_ref = refs[0], refs[1]
        o_ref = refs[n_in]
        acc_ref = refs[n_in + 1] if nk > 1 else None

        def finish(acc):
            if epi:
                acc = epi_fn(acc, *[r[...] for r in refs[2:n_in]])
            o_ref[...] = acc.astype(o_ref.dtype)

        av = a_ref[...]
        if a_fn is not None:
            av = a_fn(av)
        part = _dot(av, b_ref[...], dn)
        if nk == 1:
            finish(part)
        else:
            kk = pl.program_id(2)

            @pl.when(kk == 0)
            def _():
                acc_ref[...] = part

            @pl.when(kk > 0)
            def _():
                acc_ref[...] += part

            @pl.when(kk == nk - 1)
            def _():
                finish(acc_ref[...])

    epi_spec = pl.BlockSpec((tm, tn), o_im) if epi_width is None else pl.BlockSpec((tm, epi_width), lambda i, j, kk: (i, 0))
    in_specs = [pl.BlockSpec(a_blk, a_im), pl.BlockSpec(b_blk, b_im)] + [epi_spec] * len(epi)
    args = [a, b, *epi]
    (out,), got = _pallas(
        body, name=name, grid=grid, in_specs=in_specs, out_specs=[pl.BlockSpec((tm, tn), o_im)],
        out_shape=[jax.ShapeDtypeStruct((m, n), out_dtype)], scratch=[pltpu.VMEM((tm, tn), F32)] if nk > 1 else [],
        semantics=("parallel", "parallel", "arbitrary"), args=args, carry=carry)
    return out if carry is None else (out, got)


def _relu_sq(v):
    r = jnp.maximum(v, jnp.zeros_like(v))
    return r * r


def _rowwise(name, fn, s, tr, rows, vecs, outs, accs=()):
    n_r, n_v, n_o, n_a = len(rows), len(vecs), len(outs), len(accs)

    def body(*refs):
        vals = [refs[i][...].astype(F32) for i in range(n_r)] + [refs[n_r + i][...] for i in range(n_v)]
        o_refs = refs[n_r + n_v:n_r + n_v + n_o]
        a_refs = refs[n_r + n_v + n_o:]
        o_vals, a_vals = fn(*vals)
        for ref, val in zip(o_refs, o_vals):
            ref[...] = val.astype(ref.dtype)
        if n_a:
            @pl.when(pl.program_id(0) == 0)
            def _():
                for ref in a_refs:
                    ref[...] = jnp.zeros_like(ref)

            for ref, val in zip(a_refs, a_vals):
                ref[...] += val

    in_specs = [pl.BlockSpec((tr, w), functools.partial(lambda i, cb: (i, cb), cb=cb)) for _, w, cb in rows]
    in_specs += [pl.BlockSpec(v.shape, lambda i: (0, 0)) for v in vecs]
    out_specs = [pl.BlockSpec((tr, w), lambda i: (i, 0)) for w, _ in outs]
    out_specs += [pl.BlockSpec((1, w), lambda i: (0, 0)) for w in accs]
    out_shape = [jax.ShapeDtypeStruct((s, w), dt) for w, dt in outs]
    out_shape += [jax.ShapeDtypeStruct((1, w), F32) for w in accs]
    res = pl.pallas_call(
        body, name=name, grid=(s // tr,), in_specs=in_specs, out_specs=out_specs, out_shape=out_shape,
        compiler_params=_cparams(("arbitrary",)),
    )(*[r[0] for r in rows], *vecs)
    return res[:n_o], res[n_o:]


_ROW_TILE = 512
_STREAM_SPECS = [pl.BlockSpec((dil, _ROW_TILE // dil, D_MODEL), lambda i: (0, i, 0)) for dil in DIL_GROUPS[1:]]
_NAT_SPEC = pl.BlockSpec((_ROW_TILE, D_MODEL), lambda i: (i, 0))
_VEC_SPEC = pl.BlockSpec((1, D_MODEL), lambda i: (0, 0))
_COL_BLOCKS = pltpu.VMEM((D_MODEL // 128, _ROW_TILE, 128), F32)


def _prenorm(xs, g, tabs, s, carry=None):
    tr = _ROW_TILE
    n_g = len(DIL_GROUPS)

    def body(x_ref, g_ref, *rest):
        cs_refs, (e_ref, b_ref), (u_ref, u4_ref, u16_ref) = rest[:n_g], rest[n_g:n_g + 2], rest[n_g + 2:n_g + 5]
        tab_refs, buf = rest[n_g + 5:n_g + 5 + 3 * n_g], rest[-1]
        xn = _rms(x_ref[...]) * g_ref[...]
        u_ref[...] = xn.astype(BF16)
        for cb in range(8):
            buf[cb] = xn[:, cb * 128:(cb + 1) * 128]
        for dil, out in ((4, u4_ref), (16, u16_ref)):
            for c in range(dil):
                rows = pl.ds(c, tr // dil, stride=dil)
                out[c] = jnp.concatenate([buf.at[cb][rows, :] for cb in range(8)], axis=1).astype(BF16)
        for gi in range(n_g):
            for ref, val in zip(tab_refs[3 * gi:3 * gi + 3], _spread_rotary(cs_refs[gi][...], e_ref[...], b_ref[...])):
                ref[...] = val

    row = lambda w: pl.BlockSpec((tr, w), lambda i: (i, 0))
    whole = lambda a: pl.BlockSpec(a.shape, lambda i: (0, 0))
    res, got = _pallas(
        body, name="prenorm", grid=(s // tr,),
        in_specs=[_NAT_SPEC, _VEC_SPEC] + [row(16)] * n_g + [whole(tabs["spread"]), whole(tabs["bias"])],
        out_specs=[_NAT_SPEC] + _STREAM_SPECS + [row(128)] * (3 * n_g),
        out_shape=[jax.ShapeDtypeStruct((s, D_MODEL), BF16)]
        + [jax.ShapeDtypeStruct((dil, s // dil, D_MODEL), BF16) for dil in DIL_GROUPS[1:]]
        + [jax.ShapeDtypeStruct((s, 128), F32)] * (3 * n_g),
        scratch=[_COL_BLOCKS], semantics=("parallel",), args=(xs, g, *tabs["dil_cs"], tabs["spread"], tabs["bias"]), carry=carry)
    return [r.reshape(s, D_MODEL) for r in res[:3]], [tuple(res[3 + 3 * gi:6 + 3 * gi]) for gi in range(n_g)], got


def _grad_x(xs, d_h1, du_nat, du4, du16, g, s):
    tr = _ROW_TILE

    def body(x_ref, dh_ref, a_ref, b_ref, c_ref, u4_ref, u16_ref, g_ref, dx_ref, dg_ref, buf):
        du = a_ref[...].astype(F32) + b_ref[...].astype(F32) + c_ref[...].astype(F32)
        for dil, src in ((4, u4_ref), (16, u16_ref)):
            for c in range(dil):
                part = src[c].astype(F32)
                for cb in range(8):
                    buf.at[cb][pl.ds(c, tr // dil, stride=dil), :] = part[:, cb * 128:(cb + 1) * 128]
            du = du + jnp.concatenate([buf[cb] for cb in range(8)], axis=1)
        dx, dgr = _rms_bwd(x_ref[...], g_ref[...], du)
        dx_ref[...] = dh_ref[...] + dx

        @pl.when(pl.program_id(0) == 0)
        def _():
            dg_ref[...] = jnp.zeros_like(dg_ref)

        dg_ref[...] += _colsum(dgr)

    return pl.pallas_call(
        body, name="grad_x", grid=(s // tr,), in_specs=[_NAT_SPEC] * 5 + _STREAM_SPECS + [_VEC_SPEC],
        out_specs=[_NAT_SPEC, _VEC_SPEC],
        out_shape=[jax.ShapeDtypeStruct((s, D_MODEL), F32), jax.ShapeDtypeStruct((1, D_MODEL), F32)],
        scratch_shapes=[_COL_BLOCKS], compiler_params=_cparams(("arbitrary",)),
    )(xs, d_h1, *du_nat, du4.reshape(4, s // 4, D_MODEL), du16.reshape(16, s // 16, D_MODEL), g)


def _ret_tables():
    h = np.arange(RET_HEADS, dtype=np.float32)
    lg = np.log1p(-(np.float32(2.0) ** (-5.0 - h))).astype(np.float32)
    idx = np.arange(RET_CHUNK, dtype=np.float32)
    diff = idx[:, None] - idx[None, :]
    dm = np.where(diff[None] >= 0, np.exp(np.maximum(diff, 0.0)[None] * lg[:, None, None]), 0.0)
    qd = np.exp((idx + 1.0)[None, :, None] * lg[:, None, None])
    kd = np.exp((RET_CHUNK - 1.0 - idx)[None, :, None] * lg[:, None, None])
    cd = np.exp(RET_CHUNK * lg)[:, None, None]
    return [jnp.asarray(t, F32) for t in (dm, qd, kd, cd)]


def _rope_half(v, cos, sin):
    v1, v2 = v[:, :128], v[:, 128:]
    return jnp.concatenate([v1 * cos - v2 * sin, v2 * cos + v1 * sin], axis=1)


def _unrope_half(d, cos, sin):
    d1, d2 = d[:, :128], d[:, 128:]
    return jnp.concatenate([d1 * cos + d2 * sin, d2 * cos - d1 * sin], axis=1)


_RET_HEADS_FWD, _RET_HEADS_BWD = 1, 2


def _ret_specs(rb, rev_n, hp):
    def rowmap(w_blk):
        return lambda h, n: (rev_n(n), w_blk(h))
    tab = [pl.BlockSpec((hp, RET_CHUNK, RET_CHUNK), lambda h, n: (h, 0, 0)),
           pl.BlockSpec((hp, RET_CHUNK, 1), lambda h, n: (h, 0, 0)),
           pl.BlockSpec((hp, RET_CHUNK, 1), lambda h, n: (h, 0, 0)),
           pl.BlockSpec((hp, 1, 1), lambda h, n: (h, 0, 0))]
    proj = pl.BlockSpec((rb, hp * 1536), rowmap(lambda h: h))
    cs = pl.BlockSpec((rb, 128), rowmap(lambda h: 0))
    hv = pl.BlockSpec((rb, hp * RET_V), rowmap(lambda h: h))
    return proj, cs, hv, tab


def _ret_fwd(proj_ret, cos, sin, s, carry=None):
    rb = min(512, s)
    ch = rb // RET_CHUNK
    nb = s // rb
    hp = _RET_HEADS_FWD
    proj_spec, cs_spec, hv_spec, tab_specs = _ret_specs(rb, lambda n: n, hp)

    def body(p_ref, cos_ref, sin_ref, dm_ref, qd_ref, kd_ref, cd_ref, yr_ref, y_ref, rs_ref, r_acc):
        @pl.when(pl.program_id(1) == 0)
        def _():
            r_acc[...] = jnp.zeros_like(r_acc)

        for c, hh in [(c, hh) for c in range(ch) for hh in range(hp)]:
            rows = slice(c * RET_CHUNK, (c + 1) * RET_CHUNK)
            pc, hc = hh * 1536, hh * RET_V
            dm, qd, kd, cd = dm_ref[hh], qd_ref[hh], kd_ref[hh], cd_ref[hh]
            cosv, sinv = cos_ref[rows, :], sin_ref[rows, :]
            q = _rope_half(p_ref[rows, pc:pc + 256].astype(F32), cosv, sinv)
            kk = _rope_half(p_ref[rows, pc + 256:pc + 512].astype(F32), cosv, sinv) * (RET_QK ** -0.5)
            v = p_ref[rows, pc + 512:pc + 1024]
            g = p_ref[rows, pc + 1024:pc + 1536].astype(F32)
            rb16 = r_acc[hh].astype(BF16)
            rs_ref[hh, c] = rb16
            sc = _dot(q.astype(BF16), kk.astype(BF16), NT) * dm
            y = _dot(sc.astype(BF16), v, NN) + _dot((q * qd).astype(BF16), rb16, NN)
            r_acc[hh] = r_acc[hh] * cd + _dot((kk * kd).astype(BF16), v, TN)
            y_ref[rows, hc:hc + RET_V] = y.astype(BF16)
            yr_ref[rows, hc:hc + RET_V] = (_rms(y) * (g * _sigmoid(g))).astype(BF16)

    return _pallas(
        body, name="ret_fwd", grid=(RET_HEADS // hp, nb),
        in_specs=[proj_spec, cs_spec, cs_spec] + tab_specs,
        out_specs=[hv_spec, hv_spec, pl.BlockSpec((hp, ch, RET_QK, RET_V), lambda h, n: (h, n, 0, 0))],
        out_shape=[jax.ShapeDtypeStruct((s, RET_HEADS * RET_V), BF16), jax.ShapeDtypeStruct((s, RET_HEADS * RET_V), BF16),
                   jax.ShapeDtypeStruct((RET_HEADS, s // RET_CHUNK, RET_QK, RET_V), BF16)],
        scratch=[pltpu.VMEM((hp, RET_QK, RET_V), F32)], semantics=("parallel", "arbitrary"),
        args=(proj_ret, cos, sin, *_ret_tables()), carry=carry)


def _ret_bwd(proj_ret, cos, sin, y, d_yr, rs, s, carry=None):
    rb = min(512, s)
    ch = rb // RET_CHUNK
    nb = s // rb
    hp = _RET_HEADS_BWD
    proj_spec, cs_spec, hv_spec, tab_specs = _ret_specs(rb, lambda n: nb - 1 - n, hp)

    def body(p_ref, cos_ref, sin_ref, y_ref, dyr_ref, rs_ref, dm_ref, qd_ref, kd_ref, cd_ref, o_ref, dr_acc):
        @pl.when(pl.program_id(1) == 0)
        def _():
            dr_acc[...] = jnp.zeros_like(dr_acc)

        for c, hh in [(c, hh) for c in reversed(range(ch)) for hh in range(hp)]:
            rows = slice(c * RET_CHUNK, (c + 1) * RET_CHUNK)
            pc, hc = hh * 1536, hh * RET_V
            dm, qd, kd, cd = dm_ref[hh], qd_ref[hh], kd_ref[hh], cd_ref[hh]
            cosv, sinv = cos_ref[rows, :], sin_ref[rows, :]
            q = _rope_half(p_ref[rows, pc:pc + 256].astype(F32), cosv, sinv)
            kk = _rope_half(p_ref[rows, pc + 256:pc + 512].astype(F32), cosv, sinv) * (RET_QK ** -0.5)
            v = p_ref[rows, pc + 512:pc + 1024]
            g = p_ref[rows, pc + 1024:pc + 1536].astype(F32)
            yv = y_ref[rows, hc:hc + RET_V].astype(F32)
            dyr = dyr_ref[rows, hc:hc + RET_V].astype(F32)
            sg = _sigmoid(g)
            r = lax.rsqrt(jnp.mean(yv * yv, axis=-1, keepdims=True) + EPS)
            yn = yv * r
            dg = dyr * yn * (sg * (1.0 + g * (1.0 - sg)))
            dyn = dyr * (g * sg)
            dy = (r * (dyn - yn * jnp.mean(dyn * yn, axis=-1, keepdims=True))).astype(BF16)
            qb, kb = q.astype(BF16), kk.astype(BF16)
            rb16 = rs_ref[hh, c]
            drb = dr_acc[hh].astype(BF16)
            sd = _dot(qb, kb, NT) * dm
            ds = (_dot(dy, v, NT) * dm).astype(BF16)
            dq = _dot(ds, kb, NN) + qd * _dot(dy, rb16, NT)
            dk = _dot(ds, qb, TN) + kd * _dot(v, drb, NT)
            dv = _dot(sd.astype(BF16), dy, TN) + _dot((kk * kd).astype(BF16), drb, NN)
            dr_acc[hh] = dr_acc[hh] * cd + _dot((q * qd).astype(BF16), dy, TN)
            o_ref[rows, pc:pc + 256] = _unrope_half(dq, cosv, sinv).astype(BF16)
            o_ref[rows, pc + 256:pc + 512] = (_unrope_half(dk, cosv, sinv) * (RET_QK ** -0.5)).astype(BF16)
            o_ref[rows, pc + 512:pc + 1024] = dv.astype(BF16)
            o_ref[rows, pc + 1024:pc + 1536] = dg.astype(BF16)

    in_specs = [proj_spec, cs_spec, cs_spec, hv_spec, hv_spec,
                pl.BlockSpec((hp, ch, RET_QK, RET_V), lambda h, n: (h, nb - 1 - n, 0, 0))] + tab_specs
    return _pallas(
        body, name="ret_bwd", grid=(RET_HEADS // hp, nb), in_specs=in_specs, out_specs=[proj_spec],
        out_shape=[jax.ShapeDtypeStruct((s, RET_HEADS * 1536), BF16)], scratch=[pltpu.VMEM((hp, RET_QK, RET_V), F32)],
        semantics=("parallel", "arbitrary"), args=(proj_ret, cos, sin, y, d_yr, rs, *_ret_tables()), carry=carry)


def _rope_qk(acc, c, s1, s2):
    outs = []
    for cc in range(8):
        vv = acc[:, cc * 128:(cc + 1) * 128]
        outs.append(vv * c + pltpu.roll(vv, 120, 1) * s1 + pltpu.roll(vv, 8, 1) * s2)
    return jnp.concatenate(outs + [acc[:, 2 * DIL_W:]], axis=1)


def _pair_masks(keys_on_rows=False):
    ri = lax.broadcasted_iota(jnp.int32, (2 * QB, 2 * QB), 1 if keys_on_rows else 0)
    ci = lax.broadcasted_iota(jnp.int32, (2 * QB, 2 * QB), 0 if keys_on_rows else 1)
    e = ci - (ri & (QB - 1))
    lane_lo = lax.broadcasted_iota(jnp.int32, (2 * QB, 128), 1) < 64
    return ci, jnp.logical_and(e >= 0, e <= QB), lane_lo


def _stack_heads(v, lane_lo):
    z = jnp.zeros_like(v)
    return jnp.concatenate([jnp.where(lane_lo, v, z), jnp.where(lane_lo, z, v)], axis=0)


def _dil_fwd(qkv, dil, s, name):
    length = s // dil
    rb = min(512, length)
    nsub = rb // QB
    nbs = length // rb
    sub_per = rb // QB

    def body(q_ref, k_ref, v_ref, kp_ref, vp_ref, o_ref, l_ref):
        first = (pl.program_id(0) % nbs) == 0
        ci, band, lane_lo = _pair_masks()
        lo1 = lane_lo[0:QB]

        for i in range(nsub):
            rows = slice(i * QB, (i + 1) * QB)
            mask = jnp.logical_and(band, ci >= jnp.where(first, QB, 0)) if i == 0 else band
            for j in range(4):
                lanes = slice(j * 128, (j + 1) * 128)
                q2 = _stack_heads(q_ref[rows, lanes], lo1)
                if i == 0:
                    k2 = jnp.concatenate([kp_ref[:, lanes], k_ref[rows, lanes]], axis=0)
                    v2 = jnp.concatenate([vp_ref[:, lanes], v_ref[rows, lanes]], axis=0)
                else:
                    k2, v2 = k_ref[(i - 1) * QB:(i + 1) * QB, lanes], v_ref[(i - 1) * QB:(i + 1) * QB, lanes]
                v2 = _stack_heads(v2, lane_lo)
                sc = jnp.where(mask, _dot(q2, k2, NT) * 0.125, NEG)
                m = jnp.max(sc, axis=1, keepdims=True)
                p = jnp.exp(sc - m)
                den = jnp.sum(p, axis=1, keepdims=True)
                pb = p.astype(BF16)
                o = _dot(jnp.concatenate([pb[0:QB], pb[QB:]], axis=1), v2, NN)
                inv = 1.0 / den
                lse = m + jnp.log(den)
                o_ref[rows, lanes] = o * jnp.where(lo1, inv[0:QB], inv[QB:])
                l_ref[rows, lanes] = jnp.where(lo1, lse[0:QB], lse[QB:])

    prev = lambda n: jnp.maximum(n * sub_per - 1, 0)
    cur = lambda cb: (lambda n: (n, cb))
    return pl.pallas_call(
        body, name=name, grid=(s // rb,),
        in_specs=[pl.BlockSpec((rb, DIL_W), cur(0)), pl.BlockSpec((rb, DIL_W), cur(1)), pl.BlockSpec((rb, DIL_W), cur(2)),
                  pl.BlockSpec((QB, DIL_W), lambda n: (prev(n), 1)), pl.BlockSpec((QB, DIL_W), lambda n: (prev(n), 2))],
        out_specs=[pl.BlockSpec((rb, DIL_W), cur(0)), pl.BlockSpec((rb, DIL_W), cur(0))],
        out_shape=[jax.ShapeDtypeStruct((s, DIL_W), F32), jax.ShapeDtypeStruct((s, DIL_W), F32)],
        compiler_params=_cparams(("parallel",)),
    )(qkv, qkv, qkv, qkv, qkv)


def _dil_bwd(qkv, dya, lse, dlt, tc, ts1, ts2, dil, s, name):
    length = s // dil
    rb = min(512, length)
    nsub = rb // QB
    nbs = length // rb
    last_blk = s // QB - 1

    def body(q_ref, k_ref, v_ref, kp_ref, vp_ref, qn_ref, dy_ref, dyn_ref, l_ref, ln_ref, d_ref, dn_ref,
             c_ref, s1_ref, s2_ref, o_ref, dka, dva):
        nl = pl.program_id(0) % nbs
        first, last = nl == 0, nl == nbs - 1
        ci, band, lane_lo = _pair_masks(keys_on_rows=True)
        lo1 = lane_lo[0:QB]

        def unrope(d, rows):
            return d * c_ref[rows, :] + pltpu.roll(d * s1_ref[rows, :], 8, 1) + pltpu.roll(d * s2_ref[rows, :], 120, 1)

        for qi in range(nsub + 1):
            nxt = qi == nsub
            rows = slice((nsub - 1) * QB, nsub * QB) if nxt else slice(qi * QB, (qi + 1) * QB)
            prev_rows = slice((qi - 1) * QB, qi * QB)
            if qi == 0:
                mask = jnp.logical_and(band, ci >= jnp.where(first, QB, 0))
            elif nxt:
                mask = jnp.logical_and(band, ci <= jnp.where(last, -1, QB - 1))[0:QB, :]
            else:
                mask = band
            for j in range(4):
                lanes = slice(j * 128, (j + 1) * 128)
                if nxt:
                    q, do, lv, dl = qn_ref[:, lanes], dyn_ref[:, lanes], ln_ref[:, lanes], dn_ref[:, lanes]
                    k2, v2 = k_ref[prev_rows, lanes], v_ref[prev_rows, lanes]
                else:
                    q, do, lv, dl = q_ref[rows, lanes], dy_ref[rows, lanes], l_ref[rows, lanes], d_ref[rows, lanes]
                    if qi == 0:
                        k2 = jnp.concatenate([kp_ref[:, lanes], k_ref[rows, lanes]], axis=0)
                        v2 = jnp.concatenate([vp_ref[:, lanes], v_ref[rows, lanes]], axis=0)
                    else:
                        k2, v2 = k_ref[(qi - 1) * QB:(qi + 1) * QB, lanes], v_ref[(qi - 1) * QB:(qi + 1) * QB, lanes]
                q2, do2 = _stack_heads(q, lo1), _stack_heads(do, lo1)
                lt, dt = lv.T, dl.T
                lse2 = jnp.concatenate([lt[0:1], lt[64:65]], axis=1)
                dl2 = jnp.concatenate([dt[0:1], dt[64:65]], axis=1)
                sc = _dot(k2, q2, NT) * 0.125
                p = jnp.where(mask, jnp.exp(jnp.minimum(sc - lse2, 0.0)), 0.0)
                ds = (p * (_dot(v2, do2, NT) - dl2) * 0.125).astype(BF16)
                dk2 = _dot(ds, q2, NN)
                dv2 = _dot(p.astype(BF16), do2, NN)
                if qi >= 1:
                    dka[prev_rows, lanes] += dk2[0:QB]
                    dva[prev_rows, lanes] += dv2[0:QB]
                if not nxt:
                    dka[rows, lanes] = dk2[QB:]
                    dva[rows, lanes] = dv2[QB:]
                    dq = _dot(jnp.concatenate([ds[:, 0:QB], ds[:, QB:]], axis=0), _stack_heads(k2, lane_lo), TN)
                    o_ref[rows, lanes] = unrope(dq, rows).astype(BF16)

        for cc in range(4):
            lanes = slice(cc * 128, (cc + 1) * 128)
            o_ref[:, 512 + cc * 128:512 + (cc + 1) * 128] = unrope(dka[:, lanes], slice(None)).astype(BF16)
            o_ref[:, 1024 + cc * 128:1024 + (cc + 1) * 128] = dva[:, lanes].astype(BF16)

    prev = lambda n: jnp.maximum(n * nsub - 1, 0)
    nxt = lambda n: jnp.minimum(n * nsub + nsub, last_blk)
    cur = lambda cb: (lambda n: (n, cb))
    big = lambda cb: pl.BlockSpec((rb, DIL_W), cur(cb))
    small = lambda im: pl.BlockSpec((QB, DIL_W), im)
    tab = pl.BlockSpec((rb, 128), cur(0))
    return pl.pallas_call(
        body, name=name, grid=(s // rb,),
        in_specs=[big(0), big(1), big(2), small(lambda n: (prev(n), 1)), small(lambda n: (prev(n), 2)),
                  small(lambda n: (nxt(n), 0)), big(0), small(lambda n: (nxt(n), 0)), big(0), small(lambda n: (nxt(n), 0)),
                  big(0), small(lambda n: (nxt(n), 0)), tab, tab, tab],
        out_specs=pl.BlockSpec((rb, 3 * DIL_W), cur(0)),
        out_shape=jax.ShapeDtypeStruct((s, 3 * DIL_W), BF16),
        scratch_shapes=[pltpu.VMEM((rb, DIL_W), F32), pltpu.VMEM((rb, DIL_W), F32)],
        compiler_params=_cparams(("parallel",)),
    )(qkv, qkv, qkv, qkv, qkv, qkv, dya, dya, lse, lse, dlt, dlt, tc, ts1, ts2)


def _stream_specs(tr):
    nat = pl.BlockSpec((tr, 128), lambda i, j: (i, j))
    return [nat] + [pl.BlockSpec((dil, tr // dil, 128), lambda i, j: (0, i, j)) for dil in DIL_GROUPS[1:]]


def _dil_merge(o_g, l_g, s):
    tr = min(2048, s)
    nat, sp4, sp16 = _stream_specs(tr)

    def body(o0_ref, l0_ref, o1_ref, l1_ref, o2_ref, l2_ref, ya_ref, lse_ref, o1n, l1n, o2n, l2n):
        for src, dst, dil in ((o1_ref, o1n, 4), (l1_ref, l1n, 4), (o2_ref, o2n, 16), (l2_ref, l2n, 16)):
            for c in range(dil):
                dst[pl.ds(c, tr // dil, stride=dil), :] = src[c]
        l0, l1, l2 = l0_ref[...], l1n[...], l2n[...]
        m = jnp.maximum(jnp.maximum(l0, l1), l2)
        e0, e1, e2 = jnp.exp(l0 - m), jnp.exp(l1 - m), jnp.exp(l2 - m)
        den = e0 + e1 + e2
        ya_ref[...] = ((e0 * o0_ref[...] + e1 * o1n[...] + e2 * o2n[...]) / den).astype(BF16)
        lse_ref[...] = m + jnp.log(den)

    v3 = lambda a, dil: a.reshape(dil, s // dil, DIL_W)
    return pl.pallas_call(
        body, name="dil_merge", grid=(s // tr, 4),
        in_specs=[nat, nat, sp4, sp4, sp16, sp16], out_specs=[nat, nat],
        out_shape=[jax.ShapeDtypeStruct((s, DIL_W), BF16), jax.ShapeDtypeStruct((s, DIL_W), F32)],
        scratch_shapes=[pltpu.VMEM((tr, 128), F32)] * 4,
        compiler_params=_cparams(("parallel", "parallel")),
    )(o_g[0], l_g[0], v3(o_g[1], 4), v3(l_g[1], 4), v3(o_g[2], 16), v3(l_g[2], 16))


def _dil_bwd_prep(d_ya, ya, lse, s):
    tr = min(2048, s)
    nat, sp4, sp16 = _stream_specs(tr)

    def body(dya_ref, ya_ref, lse_ref, dy0, dl0, dy1, ls1, dl1, dy2, ls2, dl2, dlt):
        lane_lo = lax.broadcasted_iota(jnp.int32, (tr, 128), 1) < 64
        prod = dya_ref[...] * ya_ref[...].astype(F32)
        lo = jnp.where(lane_lo, prod, 0.0)
        dlt[...] = jnp.where(lane_lo, jnp.sum(lo, axis=1, keepdims=True), jnp.sum(prod - lo, axis=1, keepdims=True))
        dy0[...] = dya_ref[...].astype(BF16)
        dl0[...] = dlt[...]
        for dil, dy, ls, dl in ((4, dy1, ls1, dl1), (16, dy2, ls2, dl2)):
            for c in range(dil):
                rows = pl.ds(c, tr // dil, stride=dil)
                dy[c] = dya_ref[rows, :].astype(BF16)
                ls[c] = lse_ref[rows, :]
                dl[c] = dlt[rows, :]

    sh = lambda dil, dt: jax.ShapeDtypeStruct((dil, s // dil, DIL_W), dt)
    res = pl.pallas_call(
        body, name="dil_bwd_prep", grid=(s // tr, 4),
        in_specs=[nat, nat, nat], out_specs=[nat, nat, sp4, sp4, sp4, sp16, sp16, sp16],
        out_shape=[jax.ShapeDtypeStruct((s, DIL_W), BF16), jax.ShapeDtypeStruct((s, DIL_W), F32),
                   sh(4, BF16), sh(4, F32), sh(4, F32), sh(16, BF16), sh(16, F32), sh(16, F32)],
        scratch_shapes=[pltpu.VMEM((tr, 128), F32)],
        compiler_params=_cparams(("parallel", "parallel")),
    )(d_ya, ya, lse)
    dy0, dl0, dy1, ls1, dl1, dy2, ls2, dl2 = [r.reshape(s, DIL_W) for r in res]
    return [(dy0, lse, dl0), (dy1, ls1, dl1), (dy2, ls2, dl2)]


_RET_SEGS = ((0, 256), (1024, 256), (2048, 512), (4096, 512))


def _split_w_in(win):
    per_head = [win[a:a + RET_HEADS * n].reshape(RET_HEADS, n, D_MODEL) for a, n in _RET_SEGS]
    w_ret = jnp.concatenate(per_head, axis=1).reshape(RET_HEADS * 1536, D_MODEL)
    w_dil = [jnp.concatenate([win[a + DIL_W * g:a + DIL_W * (g + 1)] for a in (6144, 7680, 9216)], axis=0) for g in range(3)]
    return w_ret, win[10752:12800], w_dil


def _join_w_in(g_ret, g_gate, g_dil):
    g_ret = g_ret.reshape(RET_HEADS, 1536, D_MODEL)
    off = (0, 256, 512, 1024, 1536)
    parts = [g_ret[:, off[i]:off[i + 1]].reshape(-1, D_MODEL) for i in range(4)]
    dil = [g_dil[g][DIL_W * i:DIL_W * (i + 1)] for i in range(3) for g in range(3)]
    return jnp.concatenate(parts + dil + [g_gate], axis=0)


def _local_step(xs, pb, tgt, tabs, wts, vec, s, shards=None):
    tm = min(2048, s)
    tr = min(512, s)
    mm = functools.partial(_matmul, tm=tm)
    on_mesh = shards is not None
    wts, vec = dict(wts), dict(vec)
    blocks = lambda g: g.reshape(N_DEV, g.shape[0] // N_DEV, g.shape[1])

    late_shards = dict(shards) if on_mesh else {}
    first = _TwoLevelGather([late_shards.pop("w_in"), late_shards.pop("b_gate")]) if on_mesh else None
    u, rot, gathered = _prenorm(xs, vec["g_pre_mix"], tabs, s, carry=first)
    if on_mesh:
        wts["w_in"] = gathered[0].reshape(N_DEV * gathered[0].shape[1], D_MODEL)
        bias = gathered[1].transpose(1, 0, 2).reshape(2, D_MODEL)
        vec.update(b0=bias[0:1], b1=bias[1:2])
    w_ret, w_gate, w_dil = _split_w_in(wts["w_in"])
    proj_ret = mm(u[0], w_ret, mode="nt", m=s, n=6144, k=1024, tn=2048, tk=1024, out_dtype=BF16, name="inproj_ret")
    proj_gate = mm(u[0], w_gate, mode="nt", m=s, n=2048, k=1024, tn=2048, tk=1024, out_dtype=BF16, name="inproj_gate")
    qkv = [_matmul(u[g], w_dil[g], mode="nt", m=s, n=1536, k=1024, tm=min(1024, s), tn=1536, tk=1024, out_dtype=BF16,
                   name="inproj_dil%d" % g, epi=rot[g], epi_width=128, epi_fn=_rope_qk) for g in range(3)]

    names = list(late_shards) if on_mesh else []
    gather = _TwoLevelGather([late_shards[n] for n in names]) if on_mesh else None
    (yr, y_ret, rstate), gathered = _ret_fwd(proj_ret, tabs["cos_r"], tabs["sin_r"], s, carry=gather)
    wts.update({n: g.reshape(N_DEV * g.shape[1], g.shape[2]) for n, g in zip(names, gathered)})
    a_br = mm(yr, wts["w_ret_out"], mode="nn", m=s, n=1024, k=2048, tn=1024, tk=2048, out_dtype=BF16, name="ret_out")

    o_g, l_g = [], []
    for g, dil in enumerate(DIL_GROUPS):
        o, l = _dil_fwd(qkv[g], dil, s, "dil_fwd%d" % g)
        o_g.append(o)
        l_g.append(l)
    ya, lse = _dil_merge(o_g, l_g, s)
    b_br = mm(ya, wts["w_dil_out"], mode="nt", m=s, n=1024, k=512, tn=1024, tk=512, out_dtype=BF16, name="dil_out")

    def gate_mix(a, b, gr, ga, b0, b1):
        return [_sigmoid(gr.astype(F32) + b0) * a.astype(F32) + _sigmoid(ga.astype(F32) + b1) * b.astype(F32)], []

    (mixed,), _ = _rowwise("gate_mix", gate_mix, s, tr, [(a_br, 1024, 0), (b_br, 1024, 0), (proj_gate, 1024, 0), (proj_gate, 1024, 1)],
                           [vec["b0"], vec["b1"]], [(1024, BF16)])
    z = mm(mixed, wts["w_o"], mode="nn", m=s, n=1024, k=1024, tn=1024, tk=1024, out_dtype=BF16, name="w_o")

    def post_norm(h, f, g_post, g_pre):
        hn = h + _rms(f) * g_post
        return [hn, _rms(hn) * g_pre], []

    (h1, v2), _ = _rowwise("post_mix", post_norm, s, tr, [(xs, 1024, 0), (z, 1024, 0)], [vec["g_post_mix"], vec["g_pre_mlp"]],
                           [(1024, F32), (1024, BF16)])
    a_up = mm(v2, wts["w_up"], mode="nt", m=s, n=4096, k=1024, tn=2048, tk=1024, out_dtype=BF16, name="mlp_up")
    f_dn = mm(a_up, wts["w_down"], mode="nn", m=s, n=1024, k=4096, tn=1024, tk=2048, out_dtype=BF16, name="mlp_down", a_fn=_relu_sq)
    (h2, t_ple), _ = _rowwise("post_mlp", post_norm, s, tr, [(h1, 1024, 0), (f_dn, 1024, 0)], [vec["g_post_mlp"], vec["g_pre_ple"]],
                              [(1024, F32), (1024, BF16)])
    gl = mm(t_ple, wts["w_ple_gate"], mode="nn", m=s, n=1024, k=1024, tn=1024, tk=1024, out_dtype=BF16, name="ple_gate")
    e_ple = mm(pb, wts["w_ple_in"], mode="nt", m=s, n=1024, k=256, tn=1024, tk=256, out_dtype=BF16, name="ple_in")

    def ple_loss(h, glv, e, tg, b, g):
        gate = _sigmoid(glv + b)
        ge = gate * e
        diff = h + _rms(ge) * g - tg
        dy = diff * (1.0 / D_MODEL)
        d_ge, dg = _rms_bwd(ge, g, dy)
        d_gl = d_ge * e * gate * (1.0 - gate)
        loss = jnp.zeros((1, D_MODEL), F32) + 0.5 * jnp.sum(diff * diff) * (1.0 / D_MODEL)
        return [dy, d_gl, d_ge * gate], [_colsum(dg), _colsum(d_gl), loss]

    (dy, d_gl, d_e), (dg_post_ple, db_ple, loss) = _rowwise(
        "ple_loss", ple_loss, s, tr, [(h2, 1024, 0), (gl, 1024, 0), (e_ple, 1024, 0), (tgt, 1024, 0)],
        [vec["b_ple"], vec["g_post_ple"]], [(1024, F32), (1024, BF16), (1024, BF16)], [1024, 1024, 1024])

    ts, ts2 = min(1024, s), min(2048, s)
    wg = functools.partial(_matmul, mode="tn", k=s, tk=ts, out_dtype=BF16)
    grads = {}
    grads["w_ple_in"] = wg(d_e, pb, m=1024, n=256, tm=1024, tn=256, tk=ts2, name="g_ple_in")
    grads["w_ple_gate"] = wg(t_ple, d_gl, m=1024, n=1024, tm=1024, tn=1024, tk=ts2, name="g_ple_gate")
    d_t = mm(d_gl, wts["w_ple_gate"], mode="nt", m=s, n=1024, k=1024, tn=1024, tk=1024, out_dtype=BF16, name="d_t")

    def bwd_ple_mlp(h, dt, dyv, f, g_pre, g_post):
        dx, dg1 = _rms_bwd(h, g_pre, dt)
        dh = dyv + dx
        df, dg2 = _rms_bwd(f, g_post, dh)
        return [dh, df], [_colsum(dg1), _colsum(dg2)]

    (d_h2, d_f), (dg_pre_ple, dg_post_mlp) = _rowwise(
        "bwd_ple_mlp", bwd_ple_mlp, s, tr, [(h2, 1024, 0), (d_t, 1024, 0), (dy, 1024, 0), (f_dn, 1024, 0)],
        [vec["g_pre_ple"], vec["g_post_mlp"]], [(1024, F32), (1024, BF16)], [1024, 1024])
    d_a = mm(d_f, wts["w_down"], mode="nt", m=s, n=4096, k=1024, tn=1024, tk=1024, out_dtype=BF16, name="d_a",
             epi=(a_up,), epi_fn=lambda acc, av: acc * (2.0 * jnp.maximum(av.astype(F32), 0.0)))
    grads["w_down"] = wg(a_up, d_f, m=4096, n=1024, tm=2048, tn=1024, tk=ts2, name="g_down", a_fn=_relu_sq)
    grads["w_up"] = wg(d_a, v2, m=4096, n=1024, tm=2048, tn=1024, tk=ts2, name="g_up")
    d_v2 = mm(d_a, wts["w_up"], mode="nn", m=s, n=1024, k=4096, tn=1024, tk=2048, out_dtype=BF16, name="d_v2")

    (d_h1, d_z), (dg_pre_mlp, dg_post_mix) = _rowwise(
        "bwd_mlp_mix", bwd_ple_mlp, s, tr, [(h1, 1024, 0), (d_v2, 1024, 0), (d_h2, 1024, 0), (z, 1024, 0)],
        [vec["g_pre_mlp"], vec["g_post_mix"]], [(1024, F32), (1024, BF16)], [1024, 1024])
    d_mixed = mm(d_z, wts["w_o"], mode="nt", m=s, n=1024, k=1024, tn=1024, tk=1024, out_dtype=BF16, name="d_mixed")
    grads["w_o"] = wg(mixed, d_z, m=1024, n=1024, tm=1024, tn=1024, tk=ts2, name="g_o")

    def bwd_gate(dm, a, b, gr, ga, b0, b1):
        sa, sb = _sigmoid(gr.astype(F32) + b0), _sigmoid(ga.astype(F32) + b1)
        dgr = dm * a.astype(F32) * sa * (1.0 - sa)
        dga = dm * b.astype(F32) * sb * (1.0 - sb)
        return [dm * sa, dm * sb, jnp.concatenate([dgr, dga], axis=1)], [_colsum(dgr), _colsum(dga)]

    (d_abr, d_bbr, dproj_gate), (db0, db1) = _rowwise(
        "bwd_gate", bwd_gate, s, tr, [(d_mixed, 1024, 0), (a_br, 1024, 0), (b_br, 1024, 0), (proj_gate, 1024, 0), (proj_gate, 1024, 1)],
        [vec["b0"], vec["b1"]], [(1024, BF16), (1024, BF16), (2048, BF16)], [1024, 1024])
    grads["w_ret_out"] = wg(yr, d_abr, m=2048, n=1024, tm=2048, tn=1024, tk=ts2, name="g_ret_out")
    d_yr = mm(d_abr, wts["w_ret_out"], mode="nt", m=s, n=2048, k=1024, tn=2048, tk=1024, out_dtype=BF16, name="d_yr")
    grads["w_dil_out"] = wg(d_bbr, ya, m=1024, n=512, tm=1024, tn=512, tk=ts2, name="g_dil_out")
    d_ya = mm(d_bbr, wts["w_dil_out"], mode="nn", m=s, n=512, k=1024, tn=512, tk=1024, out_dtype=F32, name="d_ya")

    slots = {}
    names = list(grads) if on_mesh else []
    shares = _Exchange([blocks(grads[n]) for n in names], [True] * len(names)) if on_mesh else None
    (dproj_ret,), got = _ret_bwd(proj_ret, tabs["cos_r"], tabs["sin_r"], y_ret, d_yr, rstate, s, carry=shares)
    slots.update(zip(names, got))
    upstream = _dil_bwd_prep(d_ya, ya, lse, s)
    dqkv = [_dil_bwd(qkv[g], *upstream[g], *rot[g], dil, s, "dil_bwd%d" % g)
            for g, dil in enumerate(DIL_GROUPS)]

    g_ret = wg(dproj_ret, u[0], m=6144, n=1024, tm=2048, tn=1024, tk=ts2, name="g_in_ret")
    g_gate = wg(dproj_gate, u[0], m=2048, n=1024, tm=2048, tn=1024, tk=ts2, name="g_in_gate")
    g_dil = [wg(dqkv[g], u[g], m=1536, n=1024, tm=1536, tn=1024, tk=ts2, name="g_in_dil%d" % g) for g in range(3)]
    grads["w_in"] = _join_w_in(g_ret, g_gate, g_dil)

    du_ret = functools.partial(mm, dproj_ret, w_ret, mode="nn", m=s, n=1024, k=6144, tn=1024, tk=1024, out_dtype=BF16, name="du_ret")
    if on_mesh:
        du_ret, (slots["w_in"],) = du_ret(carry=_Exchange([blocks(grads["w_in"])], [True]))
    else:
        du_ret = du_ret()
    du_gate = mm(dproj_gate, w_gate, mode="nn", m=s, n=1024, k=2048, tn=1024, tk=2048, out_dtype=BF16, name="du_gate")
    du_dil = [mm(dqkv[g], w_dil[g], mode="nn", m=s, n=1024, k=1536, tn=1024, tk=1536, out_dtype=BF16, name="du_dil%d" % g)
              for g in range(3)]

    grad_x, dg_pre_mix = _grad_x(xs, d_h1, (du_ret, du_gate, du_dil[0]), du_dil[1], du_dil[2], vec["g_pre_mix"], s)

    zero = jnp.zeros((1, D_MODEL), F32)
    packet = jnp.concatenate([dg_pre_mix, dg_post_mix, dg_pre_mlp, dg_post_mlp, dg_pre_ple, db_ple, dg_post_ple, loss,
                              db0, db1] + [zero] * 6, axis=0)
    return grad_x, (slots if on_mesh else grads), packet


def _mesh_pos():
    return lax.axis_index("x"), lax.axis_index("y"), lax.axis_index("c")


class _Exchange:
    def __init__(self, arrays, scatter):
        self.arrays, self.scatter, self.n = list(arrays), list(scatter), len(arrays)
        self.out_shape = [jax.ShapeDtypeStruct(a.shape if sc else (N_DEV,) + a.shape, a.dtype)
                          for a, sc in zip(self.arrays, self.scatter)]
        self.scratch = [pltpu.SemaphoreType.DMA((self.n * 7,)), pltpu.SemaphoreType.DMA((self.n * 7,)),
                        pltpu.SemaphoreType.DMA((self.n,))]
        self.specs = [pl.BlockSpec(memory_space=pl.ANY)] * self.n

    def _copies(self, srcs, dsts, sems):
        send_sems, recv_sems, local_sems = sems
        x, y, c = _mesh_pos()
        my = 4 * x + 2 * y + c
        src_of = lambda w, idx: srcs[w].at[idx] if self.scatter[w] else srcs[w]
        local = [pltpu.make_async_copy(src_of(w, my), dsts[w].at[my], local_sems.at[w]) for w in range(self.n)]
        sends, recvs = [], []
        for w in range(self.n):
            for r in range(1, N_DEV):
                px = 1 - x if r & 4 else x
                py = 1 - y if r & 2 else y
                pc = 1 - c if r & 1 else c
                pidx = 4 * px + 2 * py + pc
                kw = dict(send_sem=send_sems.at[w * 7 + r - 1], recv_sem=recv_sems.at[w * 7 + r - 1],
                          device_id=(px, py, pc), device_id_type=MESH)
                sends.append(pltpu.make_async_remote_copy(src_ref=src_of(w, pidx), dst_ref=dsts[w].at[my], **kw))
                recvs.append(pltpu.make_async_remote_copy(src_ref=src_of(w, pidx), dst_ref=dsts[w].at[pidx], **kw))
        return local, sends, recvs

    def start(self, srcs, dsts, sems):
        local, sends, _ = self._copies(srcs, dsts, sems)
        for cp in local + sends:
            cp.start()

    def wait(self, srcs, dsts, sems):
        local, sends, recvs = self._copies(srcs, dsts, sems)
        for cp in recvs:
            cp.wait_recv()
        for cp in sends:
            cp.wait_send()
        for cp in local:
            cp.wait()

    def split(self, refs, n_in, n_out):
        srcs = refs[n_in:n_in + self.n]
        dsts = refs[n_in + self.n + n_out:n_in + 2 * self.n + n_out]
        return srcs, dsts, refs[len(refs) - 3:]


class _TwoLevelGather(_Exchange):
    def __init__(self, arrays):
        super().__init__(arrays, [False] * len(arrays))

    def _plan(self, srcs, dsts, sems):
        send_sems, recv_sems, local_sems = sems
        x, y, c = _mesh_pos()
        me, sibling = (x, y, c), (x, y, 1 - c)
        chips = [(1 - x, y), (x, 1 - y), (1 - x, 1 - y)]
        region = lambda w, dev: dsts[w].at[4 * dev[0] + 2 * dev[1] + dev[2]]

        def copy(w, kk, block, to, src=None):
            return pltpu.make_async_remote_copy(
                src_ref=region(w, block) if src is None else src, dst_ref=region(w, block),
                send_sem=send_sems.at[w * 7 + kk], recv_sem=recv_sems.at[w * 7 + kk], device_id=to, device_id_type=MESH)

        mine = [pltpu.make_async_copy(srcs[w], region(w, me), local_sems.at[w]) for w in range(self.n)]
        first = []
        for w in range(self.n):
            first.append(copy(w, 0, me, sibling, src=srcs[w]))
            first += [copy(w, 1 + j, me, (*chip, c), src=srcs[w]) for j, chip in enumerate(chips)]
        return me, sibling, chips, c, copy, mine, first

    def start(self, srcs, dsts, sems):
        *_, mine, first = self._plan(srcs, dsts, sems)
        for cp in mine + first:
            cp.start()

    def wait(self, srcs, dsts, sems):
        me, sibling, chips, c, copy, mine, first = self._plan(srcs, dsts, sems)
        passed = []
        for j, chip in enumerate(chips):
            for w in range(self.n):
                copy(w, 1 + j, (*chip, c), me).wait_recv()
                cp = copy(w, 4 + j, (*chip, c), sibling)
                cp.start()
                passed.append(cp)
        for w in range(self.n):
            copy(w, 0, sibling, me).wait_recv()
            for j, chip in enumerate(chips):
                copy(w, 4 + j, (*chip, 1 - c), me).wait_recv()
        for cp in first + passed:
            cp.wait_send()
        for cp in mine:
            cp.wait()


def _run_exchange(ex, name):
    def body(*refs):
        parts = ex.split(refs, 0, 0)
        ex.start(*parts)
        ex.wait(*parts)

    return pl.pallas_call(body, name=name, in_specs=ex.specs, out_specs=ex.specs, out_shape=ex.out_shape,
                          scratch_shapes=ex.scratch)(*ex.arrays)


def _pick_rows(r, c, target_bytes):
    t = r
    while (t // 2) % 16 == 0 and t // 2 >= 16 and t * c * 4 > target_bytes:
        t //= 2
    return t


def _sum_slots(slots, name):
    ns, r, c = slots.shape
    tr = _pick_rows(r, c, 256 * 1024)

    def body(s_ref, o_ref):
        acc = s_ref[0].astype(F32)
        for kk in range(1, ns):
            acc = acc + s_ref[kk].astype(F32)
        o_ref[...] = acc

    return pl.pallas_call(
        body, name=name, grid=(r // tr,),
        in_specs=[pl.BlockSpec((ns, tr, c), lambda i: (0, i, 0))], out_specs=pl.BlockSpec((tr, c), lambda i: (i, 0)),
        out_shape=jax.ShapeDtypeStruct((r, c), F32), compiler_params=_cparams(("parallel",)),
    )(slots)


def _adamw(slots, w, m, v, name):
    ns, r, c = slots.shape
    tr = _pick_rows(r, c, 256 * 1024)

    def body(s_ref, w_ref, m_ref, v_ref, g_out, d_out, m_out, v_out):
        g = s_ref[0].astype(F32)
        for kk in range(1, ns):
            g = g + s_ref[kk].astype(F32)
        mn = ADAM_B1 * m_ref[...] + (1.0 - ADAM_B1) * g
        vn = ADAM_B2 * v_ref[...] + (1.0 - ADAM_B2) * (g * g)
        m_hat = mn / (1.0 - ADAM_B1 ** ADAM_STEP)
        v_hat = vn / (1.0 - ADAM_B2 ** ADAM_STEP)
        g_out[...] = g
        d_out[...] = -ADAM_LR * (m_hat / (jnp.sqrt(v_hat) + ADAM_EPS) + ADAM_WD * w_ref[...])
        m_out[...] = mn
        v_out[...] = vn

    blk = pl.BlockSpec((tr, c), lambda i: (i, 0))
    return pl.pallas_call(
        body, name=name, grid=(r // tr,),
        in_specs=[pl.BlockSpec((ns, tr, c), lambda i: (0, i, 0)), blk, blk, blk], out_specs=[blk] * 4,
        out_shape=[jax.ShapeDtypeStruct((r, c), F32)] * 4, compiler_params=_cparams(("parallel",)),
    )(slots, w, m, v)


def _rotary_tables(pos, s):
    posf = pos.astype(F32)
    inv_freq = 1.0 / (10000.0 ** jnp.linspace(0.0, 1.0, RET_QK // 2, dtype=F32))
    ang = posf[:, None] * inv_freq
    tabs = {"cos_r": jnp.cos(ang), "sin_r": jnp.sin(ang), "dil_cs": []}
    freqs = 500000.0 ** (-jnp.arange(0, 16, 2, dtype=F32) / 16)
    spread = np.zeros((16, 384), np.float32)
    bias = np.zeros((1, 384), np.float32)
    for head in range(2):
        for i in range(8):
            spread[i, 64 * head + i] = spread[i, 64 * head + 8 + i] = 1.0
            spread[8 + i, 128 + 64 * head + i] = -1.0
            spread[8 + i, 256 + 64 * head + 8 + i] = 1.0
        bias[0, 64 * head + 16:64 * head + 64] = 1.0

    for dil in DIL_GROUPS:
        ang = posf.reshape(s // dil, dil).T.reshape(s, 1) * freqs
        tabs["dil_cs"].append(jnp.concatenate([jnp.cos(ang), jnp.sin(ang)], axis=1))
    tabs["spread"], tabs["bias"] = jnp.asarray(spread, BF16), jnp.asarray(bias)
    return tabs


def _spread_rotary(t, e, b):
    hi = t.astype(BF16)
    lo = (t - hi.astype(F32)).astype(BF16)
    out = _dot(hi, e, NN) + _dot(lo, e, NN) + b
    return out[:, 0:128], out[:, 128:256], out[:, 256:384]


_TRANSPOSED = ("w_in", "w_dil_out", "w_up", "w_ple_in")
_MATS = ("w_in", "w_ret_out", "w_dil_out", "w_o", "w_up", "w_down", "w_ple_gate", "w_ple_in")
_VECS = ("g_pre_mix", "g_post_mix", "g_pre_mlp", "g_post_mlp", "g_pre_ple", "b_ple_gate", "g_post_ple")
_ORDER = ("w_in", "b_gate", "w_ret_out", "w_dil_out", "w_o", "g_pre_mix", "g_post_mix", "g_pre_mlp", "g_post_mlp", "w_up",
          "w_down", "g_pre_ple", "w_ple_gate", "b_ple_gate", "w_ple_in", "g_post_ple")


def kernel(x, p, positions, w_in, b_gate, w_ret_out, w_dil_out, w_o, g_pre_mix, g_post_mix, g_pre_mlp, g_post_mlp, w_up, w_down, g_pre_ple, w_ple_gate, b_ple_gate, w_ple_in, g_post_ple, loss_target, m_w_in, m_b_gate, m_w_ret_out, m_w_dil_out, m_w_o, m_g_pre_mix, m_g_post_mix, m_g_pre_mlp, m_g_post_mlp, m_w_up, m_w_down, m_g_pre_ple, m_w_ple_gate, m_b_ple_gate, m_w_ple_in, m_g_post_ple, v_w_in, v_b_gate, v_w_ret_out, v_w_dil_out, v_w_o, v_g_pre_mix, v_g_post_mix, v_g_pre_mlp, v_g_post_mlp, v_w_up, v_w_down, v_g_pre_ple, v_w_ple_gate, v_b_ple_gate, v_w_ple_in, v_g_post_ple):
    s = x.shape[1]
    wd = dict(w_in=w_in, b_gate=b_gate, w_ret_out=w_ret_out, w_dil_out=w_dil_out, w_o=w_o, g_pre_mix=g_pre_mix,
              g_post_mix=g_post_mix, g_pre_mlp=g_pre_mlp, g_post_mlp=g_post_mlp, w_up=w_up, w_down=w_down,
              g_pre_ple=g_pre_ple, w_ple_gate=w_ple_gate, b_ple_gate=b_ple_gate, w_ple_in=w_ple_in, g_post_ple=g_post_ple)
    md = dict(w_in=m_w_in, b_gate=m_b_gate, w_ret_out=m_w_ret_out, w_dil_out=m_w_dil_out, w_o=m_w_o, g_pre_mix=m_g_pre_mix,
              g_post_mix=m_g_post_mix, g_pre_mlp=m_g_pre_mlp, g_post_mlp=m_g_post_mlp, w_up=m_w_up, w_down=m_w_down,
              g_pre_ple=m_g_pre_ple, w_ple_gate=m_w_ple_gate, b_ple_gate=m_b_ple_gate, w_ple_in=m_w_ple_in, g_post_ple=m_g_post_ple)
    vd = dict(w_in=v_w_in, b_gate=v_b_gate, w_ret_out=v_w_ret_out, w_dil_out=v_w_dil_out, w_o=v_w_o, g_pre_mix=v_g_pre_mix,
              g_post_mix=v_g_post_mix, g_pre_mlp=v_g_pre_mlp, g_post_mlp=v_g_post_mlp, w_up=v_w_up, w_down=v_w_down,
              g_pre_ple=v_g_pre_ple, w_ple_gate=v_w_ple_gate, b_ple_gate=v_b_ple_gate, w_ple_in=v_w_ple_in, g_post_ple=v_g_post_ple)

    shards = {n: (wd[n][0].T if n in _TRANSPOSED else wd[n][0]).astype(BF16) for n in _MATS}
    shards["b_gate"] = b_gate[0]
    vec = {n: wd[n] for n in _VECS}
    vec["b_ple"] = b_ple_gate

    tabs = _rotary_tables(positions[0], s)
    grad_x, slots, packet = _local_step(x[0], p[0, 0].astype(BF16), loss_target[0], tabs, {}, vec, s, shards=shards)

    (packets,) = _run_exchange(_Exchange([packet], [False]), "exchange_vectors")
    out = {}
    for n in _MATS:
        sl = slots[n]
        if n in _TRANSPOSED:
            sl = _sum_slots(sl, "sum_" + n).T[None]
        out[n] = _adamw(sl, wd[n][0], md[n][0], vd[n][0], "adamw_" + n)
    zero_rows = jnp.zeros((16 - len(_VECS), D_MODEL), F32)
    pack = lambda d: jnp.concatenate([d[n] for n in _VECS] + [zero_rows], axis=0)
    small = _adamw(packets, pack(wd), pack(md), pack(vd), "adamw_vectors")
    for i, n in enumerate(_VECS):
        out[n] = tuple(t[i:i + 1] for t in small)
    my = 4 * lax.axis_index("x") + 2 * lax.axis_index("y") + lax.axis_index("c")
    g_bias = lax.dynamic_slice(small[0], (8, my * 128), (2, 128))
    out["b_gate"] = _adamw(g_bias[None], b_gate[0], m_b_gate[0], v_b_gate[0], "adamw_b_gate")
    loss = small[0][7, 0]

    res = [loss, grad_x[None]]
    for kk in range(4):
        res += [out[n][kk][None] if out[n][kk].ndim == 2 and wd[n].ndim == 3 else out[n][kk] for n in _ORDER]
    return tuple(res)
```

```python
import functools
import math

import numpy as np
import jax
import jax.numpy as jnp
from jax import lax
from jax.experimental import pallas as pl
from jax.experimental.pallas import tpu as pltpu

F32, BF16 = jnp.float32, jnp.bfloat16
D_MODEL = 1024
EPS = 1e-6
N_DEV = 8
RET_HEADS, RET_QK, RET_V, RET_CHUNK = 4, 256, 512, 128
DIL_GROUPS = (1, 4, 16)
DIL_W = 512
QB = 128
NEG = -1e30
ADAM_LR, ADAM_B1, ADAM_B2, ADAM_EPS, ADAM_WD, ADAM_STEP = 0.001, 0.9, 0.999, 1e-08, 0.01, 10
VMEM_LIMIT_BYTES = 56 * 1024 * 1024
MESH = pl.DeviceIdType.MESH

NN = ((1,), (0,))
NT = ((1,), (1,))
TN = ((0,), (0,))


def _dot(a, b, dn):
    return lax.dot_general(a, b, (dn, ((), ())), preferred_element_type=F32)


def _cparams(sem):
    return pltpu.CompilerParams(dimension_semantics=sem, vmem_limit_bytes=VMEM_LIMIT_BYTES)


def _rms(x):
    return x * lax.rsqrt(jnp.mean(x * x, axis=-1, keepdims=True) + EPS)


def _rms_bwd(x, g, dy):
    r = lax.rsqrt(jnp.mean(x * x, axis=-1, keepdims=True) + EPS)
    xh = x * r
    t = dy * g
    dx = r * (t - xh * jnp.mean(t * xh, axis=-1, keepdims=True))
    return dx, dy * xh


def _colsum(v):
    return jnp.sum(v, axis=0, keepdims=True)


def _sigmoid(v):
    return 1.0 / (1.0 + jnp.exp(-v))


def _pallas(compute, *, name, grid, in_specs, out_specs, out_shape, scratch, semantics, args, carry=None):
    n_in, n_out = len(in_specs), len(out_specs)
    if carry is None:
        res = pl.pallas_call(compute, name=name, grid=grid, in_specs=in_specs, out_specs=out_specs, out_shape=out_shape,
                             scratch_shapes=scratch, compiler_params=_cparams(semantics))(*args)
        return res, []
    n_steps = math.prod(grid)

    def body(*refs):
        step = 0
        for axis, size in enumerate(grid):
            step = step * size + pl.program_id(axis)
        parts = carry.split(refs, n_in, n_out)
        pl.when(step == 0)(lambda: carry.start(*parts))
        compute(*refs[:n_in], *refs[n_in + carry.n:n_in + carry.n + n_out], *refs[n_in + 2 * carry.n + n_out:len(refs) - 3])
        pl.when(step == n_steps - 1)(lambda: carry.wait(*parts))

    res = pl.pallas_call(
        body, name=name, grid=grid, in_specs=list(in_specs) + carry.specs, out_specs=list(out_specs) + carry.specs,
        out_shape=list(out_shape) + carry.out_shape, scratch_shapes=list(scratch) + carry.scratch,
        compiler_params=_cparams(("arbitrary",) * len(grid)))(*args, *carry.arrays)
    return res[:n_out], res[n_out:]


def _matmul(a, b, *, mode, m, n, k, tm, tn, tk, out_dtype, name, a_fn=None, epi=(), epi_width=None, epi_fn=None, carry=None):
    nk = k // tk
    grid = (m // tm, n // tn, nk)
    if mode == "nn":
        a_blk, a_im, b_blk, b_im, dn = (tm, tk), (lambda i, j, kk: (i, kk)), (tk, tn), (lambda i, j, kk: (kk, j)), NN
    elif mode == "nt":
        a_blk, a_im, b_blk, b_im, dn = (tm, tk), (lambda i, j, kk: (i, kk)), (tn, tk), (lambda i, j, kk: (j, kk)), NT
    else:
        a_blk, a_im, b_blk, b_im, dn = (tk, tm), (lambda i, j, kk: (kk, i)), (tk, tn), (lambda i, j, kk: (kk, j)), TN
    o_im = lambda i, j, kk: (i, j)
    n_in = 2 + len(epi)

    def body(*refs):
        a_ref, b_ref = refs[0], refs[1]
        o_ref = refs[n_in]
        acc_ref = refs[n_in + 1] if nk > 1 else None

        def finish(acc):
            if epi:
                acc = epi_fn(acc, *[r[...] for r in refs[2:n_in]])
            o_ref[...] = acc.astype(o_ref.dtype)

        av = a_ref[...]
        if a_fn is not None:
            av = a_fn(av)
        part = _dot(av, b_ref[...], dn)
        if nk == 1:
            finish(part)
        else:
            kk = pl.program_id(2)

            @pl.when(kk == 0)
            def _():
                acc_ref[...] = part

            @pl.when(kk > 0)
            def _():
                acc_ref[...] += part

            @pl.when(kk == nk - 1)
            def _():
                finish(acc_ref[...])

    epi_spec = pl.BlockSpec((tm, tn), o_im) if epi_width is None else pl.BlockSpec((tm, epi_width), lambda i, j, kk: (i, 0))
    in_specs = [pl.BlockSpec(a_blk, a_im), pl.BlockSpec(b_blk, b_im)] + [epi_spec] * len(epi)
    args = [a, b, *epi]
    (out,), got = _pallas(
        body, name=name, grid=grid, in_specs=in_specs, out_specs=[pl.BlockSpec((tm, tn), o_im)],
        out_shape=[jax.ShapeDtypeStruct((m, n), out_dtype)], scratch=[pltpu.VMEM((tm, tn), F32)] if nk > 1 else [],
        semantics=("parallel", "parallel", "arbitrary"), args=args, carry=carry)
    return out if carry is None else (out, got)


def _relu_sq(v):
    r = jnp.maximum(v, jnp.zeros_like(v))
    return r * r


def _rowwise(name, fn, s, tr, rows, vecs, outs, accs=()):
    n_r, n_v, n_o, n_a = len(rows), len(vecs), len(outs), len(accs)

    def body(*refs):
        vals = [refs[i][...].astype(F32) for i in range(n_r)] + [refs[n_r + i][...] for i in range(n_v)]
        o_refs = refs[n_r + n_v:n_r + n_v + n_o]
        a_refs = refs[n_r + n_v + n_o:]
        o_vals, a_vals = fn(*vals)
        for ref, val in zip(o_refs, o_vals):
            ref[...] = val.astype(ref.dtype)
        if n_a:
            @pl.when(pl.program_id(0) == 0)
            def _():
                for ref in a_refs:
                    ref[...] = jnp.zeros_like(ref)

            for ref, val in zip(a_refs, a_vals):
                ref[...] += val

    in_specs = [pl.BlockSpec((tr, w), functools.partial(lambda i, cb: (i, cb), cb=cb)) for _, w, cb in rows]
    in_specs += [pl.BlockSpec(v.shape, lambda i: (0, 0)) for v in vecs]
    out_specs = [pl.BlockSpec((tr, w), lambda i: (i, 0)) for w, _ in outs]
    out_specs += [pl.BlockSpec((1, w), lambda i: (0, 0)) for w in accs]
    out_shape = [jax.ShapeDtypeStruct((s, w), dt) for w, dt in outs]
    out_shape += [jax.ShapeDtypeStruct((1, w), F32) for w in accs]
    res = pl.pallas_call(
        body, name=name, grid=(s // tr,), in_specs=in_specs, out_specs=out_specs, out_shape=out_shape,
        compiler_params=_cparams(("arbitrary",)),
    )(*[r[0] for r in rows], *vecs)
    return res[:n_o], res[n_o:]


_ROW_TILE = 512
_STREAM_SPECS = [pl.BlockSpec((dil, _ROW_TILE // dil, D_MODEL), lambda i: (0, i, 0)) for dil in DIL_GROUPS[1:]]
_NAT_SPEC = pl.BlockSpec((_ROW_TILE, D_MODEL), lambda i: (i, 0))
_VEC_SPEC = pl.BlockSpec((1, D_MODEL), lambda i: (0, 0))
_COL_BLOCKS = pltpu.VMEM((D_MODEL // 128, _ROW_TILE, 128), F32)


def _prenorm(xs, g, tabs, s, carry=None):
    tr = _ROW_TILE
    n_g = len(DIL_GROUPS)

    def body(x_ref, g_ref, *rest):
        cs_refs, (e_ref, b_ref), (u_ref, u4_ref, u16_ref) = rest[:n_g], rest[n_g:n_g + 2], rest[n_g + 2:n_g + 5]
        tab_refs, buf = rest[n_g + 5:n_g + 5 + 3 * n_g], rest[-1]
        xn = _rms(x_ref[...]) * g_ref[...]
        u_ref[...] = xn.astype(BF16)
        for cb in range(8):
            buf[cb] = xn[:, cb * 128:(cb + 1) * 128]
        for dil, out in ((4, u4_ref), (16, u16_ref)):
            for c in range(dil):
                rows = pl.ds(c, tr // dil, stride=dil)
                out[c] = jnp.concatenate([buf.at[cb][rows, :] for cb in range(8)], axis=1).astype(BF16)
        for gi in range(n_g):
            for ref, val in zip(tab_refs[3 * gi:3 * gi + 3], _spread_rotary(cs_refs[gi][...], e_ref[...], b_ref[...])):
                ref[...] = val

    row = lambda w: pl.BlockSpec((tr, w), lambda i: (i, 0))
    whole = lambda a: pl.BlockSpec(a.shape, lambda i: (0, 0))
    res, got = _pallas(
        body, name="prenorm", grid=(s // tr,),
        in_specs=[_NAT_SPEC, _VEC_SPEC] + [row(16)] * n_g + [whole(tabs["spread"]), whole(tabs["bias"])],
        out_specs=[_NAT_SPEC] + _STREAM_SPECS + [row(128)] * (3 * n_g),
        out_shape=[jax.ShapeDtypeStruct((s, D_MODEL), BF16)]
        + [jax.ShapeDtypeStruct((dil, s // dil, D_MODEL), BF16) for dil in DIL_GROUPS[1:]]
        + [jax.ShapeDtypeStruct((s, 128), F32)] * (3 * n_g),
        scratch=[_COL_BLOCKS], semantics=("parallel",), args=(xs, g, *tabs["dil_cs"], tabs["spread"], tabs["bias"]), carry=carry)
    return [r.reshape(s, D_MODEL) for r in res[:3]], [tuple(res[3 + 3 * gi:6 + 3 * gi]) for gi in range(n_g)], got


def _grad_x(xs, d_h1, du_nat, du4, du16, g, s):
    tr = _ROW_TILE

    def body(x_ref, dh_ref, a_ref, b_ref, c_ref, u4_ref, u16_ref, g_ref, dx_ref, dg_ref, buf):
        du = a_ref[...].astype(F32) + b_ref[...].astype(F32) + c_ref[...].astype(F32)
        for dil, src in ((4, u4_ref), (16, u16_ref)):
            for c in range(dil):
                part = src[c].astype(F32)
                for cb in range(8):
                    buf.at[cb][pl.ds(c, tr // dil, stride=dil), :] = part[:, cb * 128:(cb + 1) * 128]
            du = du + jnp.concatenate([buf[cb] for cb in range(8)], axis=1)
        dx, dgr = _rms_bwd(x_ref[...], g_ref[...], du)
        dx_ref[...] = dh_ref[...] + dx

        @pl.when(pl.program_id(0) == 0)
        def _():
            dg_ref[...] = jnp.zeros_like(dg_ref)

        dg_ref[...] += _colsum(dgr)

    return pl.pallas_call(
        body, name="grad_x", grid=(s // tr,), in_specs=[_NAT_SPEC] * 5 + _STREAM_SPECS + [_VEC_SPEC],
        out_specs=[_NAT_SPEC, _VEC_SPEC],
        out_shape=[jax.ShapeDtypeStruct((s, D_MODEL), F32), jax.ShapeDtypeStruct((1, D_MODEL), F32)],
        scratch_shapes=[_COL_BLOCKS], compiler_params=_cparams(("arbitrary",)),
    )(xs, d_h1, *du_nat, du4.reshape(4, s // 4, D_MODEL), du16.reshape(16, s // 16, D_MODEL), g)


def _ret_tables():
    h = np.arange(RET_HEADS, dtype=np.float32)
    lg = np.log1p(-(np.float32(2.0) ** (-5.0 - h))).astype(np.float32)
    idx = np.arange(RET_CHUNK, dtype=np.float32)
    diff = idx[:, None] - idx[None, :]
    dm = np.where(diff[None] >= 0, np.exp(np.maximum(diff, 0.0)[None] * lg[:, None, None]), 0.0)
    qd = np.exp((idx + 1.0)[None, :, None] * lg[:, None, None])
    kd = np.exp((RET_CHUNK - 1.0 - idx)[None, :, None] * lg[:, None, None])
    cd = np.exp(RET_CHUNK * lg)[:, None, None]
    return [jnp.asarray(t, F32) for t in (dm, qd, kd, cd)]


def _rope_half(v, cos, sin):
    v1, v2 = v[:, :128], v[:, 128:]
    return jnp.concatenate([v1 * cos - v2 * sin, v2 * cos + v1 * sin], axis=1)


def _unrope_half(d, cos, sin):
    d1, d2 = d[:, :128], d[:, 128:]
    return jnp.concatenate([d1 * cos + d2 * sin, d2 * cos - d1 * sin], axis=1)


_RET_HEADS_FWD, _RET_HEADS_BWD = 1, 2


def _ret_specs(rb, rev_n, hp):
    def rowmap(w_blk):
        return lambda h, n: (rev_n(n), w_blk(h))
    tab = [pl.BlockSpec((hp, RET_CHUNK, RET_CHUNK), lambda h, n: (h, 0, 0)),
           pl.BlockSpec((hp, RET_CHUNK, 1), lambda h, n: (h, 0, 0)),
           pl.BlockSpec((hp, RET_CHUNK, 1), lambda h, n: (h, 0, 0)),
           pl.BlockSpec((hp, 1, 1), lambda h, n: (h, 0, 0))]
    proj = pl.BlockSpec((rb, hp * 1536), rowmap(lambda h: h))
    cs = pl.BlockSpec((rb, 128), rowmap(lambda h: 0))
    hv = pl.BlockSpec((rb, hp * RET_V), rowmap(lambda h: h))
    return proj, cs, hv, tab


def _ret_fwd(proj_ret, cos, sin, s, carry=None):
    rb = min(512, s)
    ch = rb // RET_CHUNK
    nb = s // rb
    hp = _RET_HEADS_FWD
    proj_spec, cs_spec, hv_spec, tab_specs = _ret_specs(rb, lambda n: n, hp)

    def body(p_ref, cos_ref, sin_ref, dm_ref, qd_ref, kd_ref, cd_ref, yr_ref, y_ref, rs_ref, r_acc):
        @pl.when(pl.program_id(1) == 0)
        def _():
            r_acc[...] = jnp.zeros_like(r_acc)

        for c, hh in [(c, hh) for c in range(ch) for hh in range(hp)]:
            rows = slice(c * RET_CHUNK, (c + 1) * RET_CHUNK)
            pc, hc = hh * 1536, hh * RET_V
            dm, qd, kd, cd = dm_ref[hh], qd_ref[hh], kd_ref[hh], cd_ref[hh]
            cosv, sinv = cos_ref[rows, :], sin_ref[rows, :]
            q = _rope_half(p_ref[rows, pc:pc + 256].astype(F32), cosv, sinv)
            kk = _rope_half(p_ref[rows, pc + 256:pc + 512].astype(F32), cosv, sinv) * (RET_QK ** -0.5)
            v = p_ref[rows, pc + 512:pc + 1024]
            g = p_ref[rows, pc + 1024:pc + 1536].astype(F32)
            rb16 = r_acc[hh].astype(BF16)
            rs_ref[hh, c] = rb16
            sc = _dot(q.astype(BF16), kk.astype(BF16), NT) * dm
            y = _dot(sc.astype(BF16), v, NN) + _dot((q * qd).astype(BF16), rb16, NN)
            r_acc[hh] = r_acc[hh] * cd + _dot((kk * kd).astype(BF16), v, TN)
            y_ref[rows, hc:hc + RET_V] = y.astype(BF16)
            yr_ref[rows, hc:hc + RET_V] = (_rms(y) * (g * _sigmoid(g))).astype(BF16)

    return _pallas(
        body, name="ret_fwd", grid=(RET_HEADS // hp, nb),
        in_specs=[proj_spec, cs_spec, cs_spec] + tab_specs,
        out_specs=[hv_spec, hv_spec, pl.BlockSpec((hp, ch, RET_QK, RET_V), lambda h, n: (h, n, 0, 0))],
        out_shape=[jax.ShapeDtypeStruct((s, RET_HEADS * RET_V), BF16), jax.ShapeDtypeStruct((s, RET_HEADS * RET_V), BF16),
                   jax.ShapeDtypeStruct((RET_HEADS, s // RET_CHUNK, RET_QK, RET_V), BF16)],
        scratch=[pltpu.VMEM((hp, RET_QK, RET_V), F32)], semantics=("parallel", "arbitrary"),
        args=(proj_ret, cos, sin, *_ret_tables()), carry=carry)


def _ret_bwd(proj_ret, cos, sin, y, d_yr, rs, s, carry=None):
    rb = min(512, s)
    ch = rb // RET_CHUNK
    nb = s // rb
    hp = _RET_HEADS_BWD
    proj_spec, cs_spec, hv_spec, tab_specs = _ret_specs(rb, lambda n: nb - 1 - n, hp)

    def body(p_ref, cos_ref, sin_ref, y_ref, dyr_ref, rs_ref, dm_ref, qd_ref, kd_ref, cd_ref, o_ref, dr_acc):
        @pl.when(pl.program_id(1) == 0)
        def _():
            dr_acc[...] = jnp.zeros_like(dr_acc)

        for c, hh in [(c, hh) for c in reversed(range(ch)) for hh in range(hp)]:
            rows = slice(c * RET_CHUNK, (c + 1) * RET_CHUNK)
            pc, hc = hh * 1536, hh * RET_V
            dm, qd, kd, cd = dm_ref[hh], qd_ref[hh], kd_ref[hh], cd_ref[hh]
            cosv, sinv = cos_ref[rows, :], sin_ref[rows, :]
            q = _rope_half(p_ref[rows, pc:pc + 256].astype(F32), cosv, sinv)
            kk = _rope_half(p_ref[rows, pc + 256:pc + 512].astype(F32), cosv, sinv) * (RET_QK ** -0.5)
            v = p_ref[rows, pc + 512:pc + 1024]
            g = p_ref[rows, pc + 1024:pc + 1536].astype(F32)
            yv = y_ref[rows, hc:hc + RET_V].astype(F32)
            dyr = dyr_ref[rows, hc:hc + RET_V].astype(F32)
            sg = _sigmoid(g)
            r = lax.rsqrt(jnp.mean(yv * yv, axis=-1, keepdims=True) + EPS)
            yn = yv * r
            dg = dyr * yn * (sg * (1.0 + g * (1.0 - sg)))
            dyn = dyr * (g * sg)
            dy = (r * (dyn - yn * jnp.mean(dyn * yn, axis=-1, keepdims=True))).astype(BF16)
            qb, kb = q.astype(BF16), kk.astype(BF16)
            rb16 = rs_ref[hh, c]
            drb = dr_acc[hh].astype(BF16)
            sd = _dot(qb, kb, NT) * dm
            ds = (_dot(dy, v, NT) * dm).astype(BF16)
            dq = _dot(ds, kb, NN) + qd * _dot(dy, rb16, NT)
            dk = _dot(ds, qb, TN) + kd * _dot(v, drb, NT)
            dv = _dot(sd.astype(BF16), dy, TN) + _dot((kk * kd).astype(BF16), drb, NN)
            dr_acc[hh] = dr_acc[hh] * cd + _dot((q * qd).astype(BF16), dy, TN)
            o_ref[rows, pc:pc + 256] = _unrope_half(dq, cosv, sinv).astype(BF16)
            o_ref[rows, pc + 256:pc + 512] = (_unrope_half(dk, cosv, sinv) * (RET_QK ** -0.5)).astype(BF16)
            o_ref[rows, pc + 512:pc + 1024] = dv.astype(BF16)
            o_ref[rows, pc + 1024:pc + 1536] = dg.astype(BF16)

    in_specs = [proj_spec, cs_spec, cs_spec, hv_spec, hv_spec,
                pl.BlockSpec((hp, ch, RET_QK, RET_V), lambda h, n: (h, nb - 1 - n, 0, 0))] + tab_specs
    return _pallas(
        body, name="ret_bwd", grid=(RET_HEADS // hp, nb), in_specs=in_specs, out_specs=[proj_spec],
        out_shape=[jax.ShapeDtypeStruct((s, RET_HEADS * 1536), BF16)], scratch=[pltpu.VMEM((hp, RET_QK, RET_V), F32)],
        semantics=("parallel", "arbitrary"), args=(proj_ret, cos, sin, y, d_yr, rs, *_ret_tables()), carry=carry)


def _rope_qk(acc, c, s1, s2):
    outs = []
    for cc in range(8):
        vv = acc[:, cc * 128:(cc + 1) * 128]
        outs.append(vv * c + pltpu.roll(vv, 120, 1) * s1 + pltpu.roll(vv, 8, 1) * s2)
    return jnp.concatenate(outs + [acc[:, 2 * DIL_W:]], axis=1)


def _pair_masks(keys_on_rows=False):
    ri = lax.broadcasted_iota(jnp.int32, (2 * QB, 2 * QB), 1 if keys_on_rows else 0)
    ci = lax.broadcasted_iota(jnp.int32, (2 * QB, 2 * QB), 0 if keys_on_rows else 1)
    e = ci - (ri & (QB - 1))
    lane_lo = lax.broadcasted_iota(jnp.int32, (2 * QB, 128), 1) < 64
    return ci, jnp.logical_and(e >= 0, e <= QB), lane_lo


def _stack_heads(v, lane_lo):
    z = jnp.zeros_like(v)
    return jnp.concatenate([jnp.where(lane_lo, v, z), jnp.where(lane_lo, z, v)], axis=0)


def _dil_fwd(qkv, dil, s, name):
    length = s // dil
    rb = min(512, length)
    nsub = rb // QB
    nbs = length // rb
    sub_per = rb // QB

    def body(q_ref, k_ref, v_ref, kp_ref, vp_ref, o_ref, l_ref):
        first = (pl.program_id(0) % nbs) == 0
        ci, band, lane_lo = _pair_masks()
        lo1 = lane_lo[0:QB]

        for i in range(nsub):
            rows = slice(i * QB, (i + 1) * QB)
            mask = jnp.logical_and(band, ci >= jnp.where(first, QB, 0)) if i == 0 else band
            for j in range(4):
                lanes = slice(j * 128, (j + 1) * 128)
                q2 = _stack_heads(q_ref[rows, lanes], lo1)
                if i == 0:
                    k2 = jnp.concatenate([kp_ref[:, lanes], k_ref[rows, lanes]], axis=0)
                    v2 = jnp.concatenate([vp_ref[:, lanes], v_ref[rows, lanes]], axis=0)
                else:
                    k2, v2 = k_ref[(i - 1) * QB:(i + 1) * QB, lanes], v_ref[(i - 1) * QB:(i + 1) * QB, lanes]
                v2 = _stack_heads(v2, lane_lo)
                sc = jnp.where(mask, _dot(q2, k2, NT) * 0.125, NEG)
                m = jnp.max(sc, axis=1, keepdims=True)
                p = jnp.exp(sc - m)
                den = jnp.sum(p, axis=1, keepdims=True)
                pb = p.astype(BF16)
                o = _dot(jnp.concatenate([pb[0:QB], pb[QB:]], axis=1), v2, NN)
                inv = 1.0 / den
                lse = m + jnp.log(den)
                o_ref[rows, lanes] = (o * jnp.where(lo1, inv[0:QB], inv[QB:])).astype(BF16)
                l_ref[rows, lanes] = jnp.where(lo1, lse[0:QB], lse[QB:])

    prev = lambda n: jnp.maximum(n * sub_per - 1, 0)
    cur = lambda cb: (lambda n: (n, cb))
    return pl.pallas_call(
        body, name=name, grid=(s // rb,),
        in_specs=[pl.BlockSpec((rb, DIL_W), cur(0)), pl.BlockSpec((rb, DIL_W), cur(1)), pl.BlockSpec((rb, DIL_W), cur(2)),
                  pl.BlockSpec((QB, DIL_W), lambda n: (prev(n), 1)), pl.BlockSpec((QB, DIL_W), lambda n: (prev(n), 2))],
        out_specs=[pl.BlockSpec((rb, DIL_W), cur(0)), pl.BlockSpec((rb, DIL_W), cur(0))],
        out_shape=[jax.ShapeDtypeStruct((s, DIL_W), BF16), jax.ShapeDtypeStruct((s, DIL_W), F32)],
        compiler_params=_cparams(("parallel",)),
    )(qkv, qkv, qkv, qkv, qkv)


def _dil_bwd(qkv, dya, lse, dlt, tc, ts1, ts2, dil, s, name):
    length = s // dil
    rb = min(512, length)
    nsub = rb // QB
    nbs = length // rb
    last_blk = s // QB - 1

    def body(q_ref, k_ref, v_ref, kp_ref, vp_ref, qn_ref, dy_ref, dyn_ref, l_ref, ln_ref, d_ref, dn_ref,
             c_ref, s1_ref, s2_ref, o_ref, dka, dva):
        nl = pl.program_id(0) % nbs
        first, last = nl == 0, nl == nbs - 1
        ci, band, lane_lo = _pair_masks(keys_on_rows=True)
        lo1 = lane_lo[0:QB]

        def unrope(d, rows):
            return d * c_ref[rows, :] + pltpu.roll(d * s1_ref[rows, :], 8, 1) + pltpu.roll(d * s2_ref[rows, :], 120, 1)

        for qi in range(nsub + 1):
            nxt = qi == nsub
            rows = slice((nsub - 1) * QB, nsub * QB) if nxt else slice(qi * QB, (qi + 1) * QB)
            prev_rows = slice((qi - 1) * QB, qi * QB)
            if qi == 0:
                mask = jnp.logical_and(band, ci >= jnp.where(first, QB, 0))
            elif nxt:
                mask = jnp.logical_and(band, ci <= jnp.where(last, -1, QB - 1))[0:QB, :]
            else:
                mask = band
            for j in range(4):
                lanes = slice(j * 128, (j + 1) * 128)
                if nxt:
                    q, do, lv, dl = qn_ref[:, lanes], dyn_ref[:, lanes], ln_ref[:, lanes], dn_ref[:, lanes]
                    k2, v2 = k_ref[prev_rows, lanes], v_ref[prev_rows, lanes]
                else:
                    q, do, lv, dl = q_ref[rows, lanes], dy_ref[rows, lanes], l_ref[rows, lanes], d_ref[rows, lanes]
                    if qi == 0:
                        k2 = jnp.concatenate([kp_ref[:, lanes], k_ref[rows, lanes]], axis=0)
                        v2 = jnp.concatenate([vp_ref[:, lanes], v_ref[rows, lanes]], axis=0)
                    else:
                        k2, v2 = k_ref[(qi - 1) * QB:(qi + 1) * QB, lanes], v_ref[(qi - 1) * QB:(qi + 1) * QB, lanes]
                q2, do2 = _stack_heads(q, lo1), _stack_heads(do, lo1)
                lt, dt = lv.T, dl.T
                lse2 = jnp.concatenate([lt[0:1], lt[64:65]], axis=1)
                dl2 = jnp.concatenate([dt[0:1], dt[64:65]], axis=1)
                sc = _dot(k2, q2, NT) * 0.125
                p = jnp.where(mask, jnp.exp(jnp.minimum(sc - lse2, 0.0)), 0.0)
                ds = (p * (_dot(v2, do2, NT) - dl2) * 0.125).astype(BF16)
                dk2 = _dot(ds, q2, NN)
                dv2 = _dot(p.astype(BF16), do2, NN)
                if qi >= 1:
                    dka[prev_rows, lanes] += dk2[0:QB]
                    dva[prev_rows, lanes] += dv2[0:QB]
                if not nxt:
                    dka[rows, lanes] = dk2[QB:]
                    dva[rows, lanes] = dv2[QB:]
                    dq = _dot(jnp.concatenate([ds[:, 0:QB], ds[:, QB:]], axis=0), _stack_heads(k2, lane_lo), TN)
                    o_ref[rows, lanes] = unrope(dq, rows).astype(BF16)

        for cc in range(4):
            lanes = slice(cc * 128, (cc + 1) * 128)
            o_ref[:, 512 + cc * 128:512 + (cc + 1) * 128] = unrope(dka[:, lanes], slice(None)).astype(BF16)
            o_ref[:, 1024 + cc * 128:1024 + (cc + 1) * 128] = dva[:, lanes].astype(BF16)

    prev = lambda n: jnp.maximum(n * nsub - 1, 0)
    nxt = lambda n: jnp.minimum(n * nsub + nsub, last_blk)
    cur = lambda cb: (lambda n: (n, cb))
    big = lambda cb: pl.BlockSpec((rb, DIL_W), cur(cb))
    small = lambda im: pl.BlockSpec((QB, DIL_W), im)
    tab = pl.BlockSpec((rb, 128), cur(0))
    return pl.pallas_call(
        body, name=name, grid=(s // rb,),
        in_specs=[big(0), big(1), big(2), small(lambda n: (prev(n), 1)), small(lambda n: (prev(n), 2)),
                  small(lambda n: (nxt(n), 0)), big(0), small(lambda n: (nxt(n), 0)), big(0), small(lambda n: (nxt(n), 0)),
                  big(0), small(lambda n: (nxt(n), 0)), tab, tab, tab],
        out_specs=pl.BlockSpec((rb, 3 * DIL_W), cur(0)),
        out_shape=jax.ShapeDtypeStruct((s, 3 * DIL_W), BF16),
        scratch_shapes=[pltpu.VMEM((rb, DIL_W), F32), pltpu.VMEM((rb, DIL_W), F32)],
        compiler_params=_cparams(("parallel",)),
    )(qkv, qkv, qkv, qkv, qkv, qkv, dya, dya, lse, lse, dlt, dlt, tc, ts1, ts2)


def _stream_specs(tr):
    nat = pl.BlockSpec((tr, 128), lambda i, j: (i, j))
    return [nat] + [pl.BlockSpec((dil, tr // dil, 128), lambda i, j: (0, i, j)) for dil in DIL_GROUPS[1:]]


def _dil_merge(o_g, l_g, s):
    tr = min(2048, s)
    nat, sp4, sp16 = _stream_specs(tr)

    def body(o0_ref, l0_ref, o1_ref, l1_ref, o2_ref, l2_ref, ya_ref, lse_ref, o1n, l1n, o2n, l2n):
        for src, dst, dil in ((o1_ref, o1n, 4), (l1_ref, l1n, 4), (o2_ref, o2n, 16), (l2_ref, l2n, 16)):
            for c in range(dil):
                dst[pl.ds(c, tr // dil, stride=dil), :] = src[c].astype(F32)
        l0, l1, l2 = l0_ref[...], l1n[...], l2n[...]
        m = jnp.maximum(jnp.maximum(l0, l1), l2)
        e0, e1, e2 = jnp.exp(l0 - m), jnp.exp(l1 - m), jnp.exp(l2 - m)
        den = e0 + e1 + e2
        ya_ref[...] = ((e0 * o0_ref[...].astype(F32) + e1 * o1n[...] + e2 * o2n[...]) / den).astype(BF16)
        lse_ref[...] = m + jnp.log(den)

    v3 = lambda a, dil: a.reshape(dil, s // dil, DIL_W)
    return pl.pallas_call(
        body, name="dil_merge", grid=(s // tr, 4),
        in_specs=[nat, nat, sp4, sp4, sp16, sp16], out_specs=[nat, nat],
        out_shape=[jax.ShapeDtypeStruct((s, DIL_W), BF16), jax.ShapeDtypeStruct((s, DIL_W), F32)],
        scratch_shapes=[pltpu.VMEM((tr, 128), F32)] * 4,
        compiler_params=_cparams(("parallel", "parallel")),
    )(o_g[0], l_g[0], v3(o_g[1], 4), v3(l_g[1], 4), v3(o_g[2], 16), v3(l_g[2], 16))


def _dil_bwd_prep(d_ya, ya, lse, s):
    tr = min(2048, s)
    nat, sp4, sp16 = _stream_specs(tr)

    def body(dya_ref, ya_ref, lse_ref, dy0, dl0, dy1, ls1, dl1, dy2, ls2, dl2, dlt):
        lane_lo = lax.broadcasted_iota(jnp.int32, (tr, 128), 1) < 64
        prod = dya_ref[...] * ya_ref[...].astype(F32)
        lo = jnp.where(lane_lo, prod, 0.0)
        dlt[...] = jnp.where(lane_lo, jnp.sum(lo, axis=1, keepdims=True), jnp.sum(prod - lo, axis=1, keepdims=True))
        dy0[...] = dya_ref[...].astype(BF16)
        dl0[...] = dlt[...]
        for dil, dy, ls, dl in ((4, dy1, ls1, dl1), (16, dy2, ls2, dl2)):
            for c in range(dil):
                rows = pl.ds(c, tr // dil, stride=dil)
                dy[c] = dya_ref[rows, :].astype(BF16)
                ls[c] = lse_ref[rows, :]
                dl[c] = dlt[rows, :]

    sh = lambda dil, dt: jax.ShapeDtypeStruct((dil, s // dil, DIL_W), dt)
    res = pl.pallas_call(
        body, name="dil_bwd_prep", grid=(s // tr, 4),
        in_specs=[nat, nat, nat], out_specs=[nat, nat, sp4, sp4, sp4, sp16, sp16, sp16],
        out_shape=[jax.ShapeDtypeStruct((s, DIL_W), BF16), jax.ShapeDtypeStruct((s, DIL_W), F32),
                   sh(4, BF16), sh(4, F32), sh(4, F32), sh(16, BF16), sh(16, F32), sh(16, F32)],
        scratch_shapes=[pltpu.VMEM((tr, 128), F32)],
        compiler_params=_cparams(("parallel", "parallel")),
    )(d_ya, ya, lse)
    dy0, dl0, dy1, ls1, dl1, dy2, ls2, dl2 = [r.reshape(s, DIL_W) for r in res]
    return [(dy0, lse, dl0), (dy1, ls1, dl1), (dy2, ls2, dl2)]


_RET_SEGS = ((0, 256), (1024, 256), (2048, 512), (4096, 512))


def _split_w_in(win):
    per_head = [win[a:a + RET_HEADS * n].reshape(RET_HEADS, n, D_MODEL) for a, n in _RET_SEGS]
    w_ret = jnp.concatenate(per_head, axis=1).reshape(RET_HEADS * 1536, D_MODEL)
    w_dil = [jnp.concatenate([win[a + DIL_W * g:a + DIL_W * (g + 1)] for a in (6144, 7680, 9216)], axis=0) for g in range(3)]
    return w_ret, win[10752:12800], w_dil


def _join_w_in(g_ret, g_gate, g_dil):
    g_ret = g_ret.reshape(RET_HEADS, 1536, D_MODEL)
    off = (0, 256, 512, 1024, 1536)
    parts = [g_ret[:, off[i]:off[i + 1]].reshape(-1, D_MODEL) for i in range(4)]
    dil = [g_dil[g][DIL_W * i:DIL_W * (i + 1)] for i in range(3) for g in range(3)]
    return jnp.concatenate(parts + dil + [g_gate], axis=0)


def _local_step(xs, pb, tgt, tabs, wts, vec, s, shards=None):
    tm = min(2048, s)
    tr = min(512, s)
    mm = functools.partial(_matmul, tm=tm)
    on_mesh = shards is not None
    wts, vec = dict(wts), dict(vec)
    blocks = lambda g: g.reshape(N_DEV, g.shape[0] // N_DEV, g.shape[1])

    late_shards = dict(shards) if on_mesh else {}
    first = _TwoLevelGather([late_shards.pop("w_in"), late_shards.pop("b_gate")]) if on_mesh else None
    u, rot, gathered = _prenorm(xs, vec["g_pre_mix"], tabs, s, carry=first)
    if on_mesh:
        wts["w_in"] = gathered[0].reshape(N_DEV * gathered[0].shape[1], D_MODEL)
        bias = gathered[1].transpose(1, 0, 2).reshape(2, D_MODEL)
        vec.update(b0=bias[0:1], b1=bias[1:2])
    w_ret, w_gate, w_dil = _split_w_in(wts["w_in"])
    proj_ret = mm(u[0], w_ret, mode="nt", m=s, n=6144, k=1024, tn=2048, tk=1024, out_dtype=BF16, name="inproj_ret")
    proj_gate = mm(u[0], w_gate, mode="nt", m=s, n=2048, k=1024, tn=2048, tk=1024, out_dtype=BF16, name="inproj_gate")
    qkv = [_matmul(u[g], w_dil[g], mode="nt", m=s, n=1536, k=1024, tm=min(1024, s), tn=1536, tk=1024, out_dtype=BF16,
                   name="inproj_dil%d" % g, epi=rot[g], epi_width=128, epi_fn=_rope_qk) for g in range(3)]

    names = list(late_shards) if on_mesh else []
    gather = _TwoLevelGather([late_shards[n] for n in names]) if on_mesh else None
    (yr, y_ret, rstate), gathered = _ret_fwd(proj_ret, tabs["cos_r"], tabs["sin_r"], s, carry=gather)
    wts.update({n: g.reshape(N_DEV * g.shape[1], g.shape[2]) for n, g in zip(names, gathered)})
    a_br = mm(yr, wts["w_ret_out"], mode="nn", m=s, n=1024, k=2048, tn=1024, tk=2048, out_dtype=BF16, name="ret_out")

    o_g, l_g = [], []
    for g, dil in enumerate(DIL_GROUPS):
        o, l = _dil_fwd(qkv[g], dil, s, "dil_fwd%d" % g)
        o_g.append(o)
        l_g.append(l)
    ya, lse = _dil_merge(o_g, l_g, s)
    b_br = mm(ya, wts["w_dil_out"], mode="nt", m=s, n=1024, k=512, tn=1024, tk=512, out_dtype=BF16, name="dil_out")

    def gate_mix(a, b, gr, ga, b0, b1):
        return [_sigmoid(gr.astype(F32) + b0) * a.astype(F32) + _sigmoid(ga.astype(F32) + b1) * b.astype(F32)], []

    (mixed,), _ = _rowwise("gate_mix", gate_mix, s, tr, [(a_br, 1024, 0), (b_br, 1024, 0), (proj_gate, 1024, 0), (proj_gate, 1024, 1)],
                           [vec["b0"], vec["b1"]], [(1024, BF16)])
    z = mm(mixed, wts["w_o"], mode="nn", m=s, n=1024, k=1024, tn=1024, tk=1024, out_dtype=BF16, name="w_o")

    def post_norm(h, f, g_post, g_pre):
        hn = h + _rms(f) * g_post
        return [hn, _rms(hn) * g_pre], []

    (h1, v2), _ = _rowwise("post_mix", post_norm, s, tr, [(xs, 1024, 0), (z, 1024, 0)], [vec["g_post_mix"], vec["g_pre_mlp"]],
                           [(1024, F32), (1024, BF16)])
    a_up = mm(v2, wts["w_up"], mode="nt", m=s, n=4096, k=1024, tn=2048, tk=1024, out_dtype=BF16, name="mlp_up")
    f_dn = mm(a_up, wts["w_down"], mode="nn", m=s, n=1024, k=4096, tn=1024, tk=2048, out_dtype=BF16, name="mlp_down", a_fn=_relu_sq)
    (h2, t_ple), _ = _rowwise("post_mlp", post_norm, s, tr, [(h1, 1024, 0), (f_dn, 1024, 0)], [vec["g_post_mlp"], vec["g_pre_ple"]],
                              [(1024, F32), (1024, BF16)])
    gl = mm(t_ple, wts["w_ple_gate"], mode="nn", m=s, n=1024, k=1024, tn=1024, tk=1024, out_dtype=BF16, name="ple_gate")
    e_ple = mm(pb, wts["w_ple_in"], mode="nt", m=s, n=1024, k=256, tn=1024, tk=256, out_dtype=BF16, name="ple_in")

    def ple_loss(h, glv, e, tg, b, g):
        gate = _sigmoid(glv + b)
        ge = gate * e
        diff = h + _rms(ge) * g - tg
        dy = diff * (1.0 / D_MODEL)
        d_ge, dg = _rms_bwd(ge, g, dy)
        d_gl = d_ge * e * gate * (1.0 - gate)
        loss = jnp.zeros((1, D_MODEL), F32) + 0.5 * jnp.sum(diff * diff) * (1.0 / D_MODEL)
        return [dy, d_gl, d_ge * gate], [_colsum(dg), _colsum(d_gl), loss]

    (dy, d_gl, d_e), (dg_post_ple, db_ple, loss) = _rowwise(
        "ple_loss", ple_loss, s, tr, [(h2, 1024, 0), (gl, 1024, 0), (e_ple, 1024, 0), (tgt, 1024, 0)],
        [vec["b_ple"], vec["g_post_ple"]], [(1024, F32), (1024, BF16), (1024, BF16)], [1024, 1024, 1024])

    ts, ts2 = min(1024, s), min(2048, s)
    wg = functools.partial(_matmul, mode="tn", k=s, tk=ts, out_dtype=BF16)
    grads = {}
    grads["w_ple_in"] = wg(d_e, pb, m=1024, n=256, tm=1024, tn=256, tk=ts2, name="g_ple_in")
    grads["w_ple_gate"] = wg(t_ple, d_gl, m=1024, n=1024, tm=1024, tn=1024, tk=ts2, name="g_ple_gate")
    d_t = mm(d_gl, wts["w_ple_gate"], mode="nt", m=s, n=1024, k=1024, tn=1024, tk=1024, out_dtype=BF16, name="d_t")

    def bwd_ple_mlp(h, dt, dyv, f, g_pre, g_post):
        dx, dg1 = _rms_bwd(h, g_pre, dt)
        dh = dyv + dx
        df, dg2 = _rms_bwd(f, g_post, dh)
        return [dh, df], [_colsum(dg1), _colsum(dg2)]

    (d_h2, d_f), (dg_pre_ple, dg_post_mlp) = _rowwise(
        "bwd_ple_mlp", bwd_ple_mlp, s, tr, [(h2, 1024, 0), (d_t, 1024, 0), (dy, 1024, 0), (f_dn, 1024, 0)],
        [vec["g_pre_ple"], vec["g_post_mlp"]], [(1024, F32), (1024, BF16)], [1024, 1024])
    d_a = mm(d_f, wts["w_down"], mode="nt", m=s, n=4096, k=1024, tn=1024, tk=1024, out_dtype=BF16, name="d_a",
             epi=(a_up,), epi_fn=lambda acc, av: acc * (2.0 * jnp.maximum(av.astype(F32), 0.0)))
    grads["w_down"] = wg(a_up, d_f, m=4096, n=1024, tm=2048, tn=1024, tk=ts2, name="g_down", a_fn=_relu_sq)
    grads["w_up"] = wg(d_a, v2, m=4096, n=1024, tm=2048, tn=1024, tk=ts2, name="g_up")
    d_v2 = mm(d_a, wts["w_up"], mode="nn", m=s, n=1024, k=4096, tn=1024, tk=2048, out_dtype=BF16, name="d_v2")

    (d_h1, d_z), (dg_pre_mlp, dg_post_mix) = _rowwise(
        "bwd_mlp_mix", bwd_ple_mlp, s, tr, [(h1, 1024, 0), (d_v2, 1024, 0), (d_h2, 1024, 0), (z, 1024, 0)],
        [vec["g_pre_mlp"], vec["g_post_mix"]], [(1024, F32), (1024, BF16)], [1024, 1024])
    d_mixed = mm(d_z, wts["w_o"], mode="nt", m=s, n=1024, k=1024, tn=1024, tk=1024, out_dtype=BF16, name="d_mixed")
    grads["w_o"] = wg(mixed, d_z, m=1024, n=1024, tm=1024, tn=1024, tk=ts2, name="g_o")

    def bwd_gate(dm, a, b, gr, ga, b0, b1):
        sa, sb = _sigmoid(gr.astype(F32) + b0), _sigmoid(ga.astype(F32) + b1)
        dgr = dm * a.astype(F32) * sa * (1.0 - sa)
        dga = dm * b.astype(F32) * sb * (1.0 - sb)
        return [dm * sa, dm * sb, jnp.concatenate([dgr, dga], axis=1)], [_colsum(dgr), _colsum(dga)]

    (d_abr, d_bbr, dproj_gate), (db0, db1) = _rowwise(
        "bwd_gate", bwd_gate, s, tr, [(d_mixed, 1024, 0), (a_br, 1024, 0), (b_br, 1024, 0), (proj_gate, 1024, 0), (proj_gate, 1024, 1)],
        [vec["b0"], vec["b1"]], [(1024, BF16), (1024, BF16), (2048, BF16)], [1024, 1024])
    grads["w_ret_out"] = wg(yr, d_abr, m=2048, n=1024, tm=2048, tn=1024, tk=ts2, name="g_ret_out")
    d_yr = mm(d_abr, wts["w_ret_out"], mode="nt", m=s, n=2048, k=1024, tn=2048, tk=1024, out_dtype=BF16, name="d_yr")
    grads["w_dil_out"] = wg(d_bbr, ya, m=1024, n=512, tm=1024, tn=512, tk=ts2, name="g_dil_out")
    d_ya = mm(d_bbr, wts["w_dil_out"], mode="nn", m=s, n=512, k=1024, tn=512, tk=1024, out_dtype=F32, name="d_ya")

    slots = {}
    names = list(grads) if on_mesh else []
    shares = _Exchange([blocks(grads[n]) for n in names], [True] * len(names)) if on_mesh else None
    (dproj_ret,), got = _ret_bwd(proj_ret, tabs["cos_r"], tabs["sin_r"], y_ret, d_yr, rstate, s, carry=shares)
    slots.update(zip(names, got))
    upstream = _dil_bwd_prep(d_ya, ya, lse, s)
    dqkv = [_dil_bwd(qkv[g], *upstream[g], *rot[g], dil, s, "dil_bwd%d" % g)
            for g, dil in enumerate(DIL_GROUPS)]

    g_ret = wg(dproj_ret, u[0], m=6144, n=1024, tm=2048, tn=1024, tk=ts2, name="g_in_ret")
    g_gate = wg(dproj_gate, u[0], m=2048, n=1024, tm=2048, tn=1024, tk=ts2, name="g_in_gate")
    g_dil = [wg(dqkv[g], u[g], m=1536, n=1024, tm=1536, tn=1024, tk=ts2, name="g_in_dil%d" % g) for g in range(3)]
    grads["w_in"] = _join_w_in(g_ret, g_gate, g_dil)

    du_ret = functools.partial(mm, dproj_ret, w_ret, mode="nn", m=s, n=1024, k=6144, tn=1024, tk=1024, out_dtype=BF16, name="du_ret")
    if on_mesh:
        du_ret, (slots["w_in"],) = du_ret(carry=_Exchange([blocks(grads["w_in"])], [True]))
    else:
        du_ret = du_ret()
    du_gate = mm(dproj_gate, w_gate, mode="nn", m=s, n=1024, k=2048, tn=1024, tk=2048, out_dtype=BF16, name="du_gate")
    du_dil = [mm(dqkv[g], w_dil[g], mode="nn", m=s, n=1024, k=1536, tn=1024, tk=1536, out_dtype=BF16, name="du_dil%d" % g)
              for g in range(3)]

    grad_x, dg_pre_mix = _grad_x(xs, d_h1, (du_ret, du_gate, du_dil[0]), du_dil[1], du_dil[2], vec["g_pre_mix"], s)

    zero = jnp.zeros((1, D_MODEL), F32)
    packet = jnp.concatenate([dg_pre_mix, dg_post_mix, dg_pre_mlp, dg_post_mlp, dg_pre_ple, db_ple, dg_post_ple, loss,
                              db0, db1] + [zero] * 6, axis=0)
    return grad_x, (slots if on_mesh else grads), packet


def _mesh_pos():
    return lax.axis_index("x"), lax.axis_index("y"), lax.axis_index("c")


class _Exchange:
    def __init__(self, arrays, scatter):
        self.arrays, self.scatter, self.n = list(arrays), list(scatter), len(arrays)
        self.out_shape = [jax.ShapeDtypeStruct(a.shape if sc else (N_DEV,) + a.shape, a.dtype)
                          for a, sc in zip(self.arrays, self.scatter)]
        self.scratch = [pltpu.SemaphoreType.DMA((self.n * 7,)), pltpu.SemaphoreType.DMA((self.n * 7,)),
                        pltpu.SemaphoreType.DMA((self.n,))]
        self.specs = [pl.BlockSpec(memory_space=pl.ANY)] * self.n

    def _copies(self, srcs, dsts, sems):
        send_sems, recv_sems, local_sems = sems
        x, y, c = _mesh_pos()
        my = 4 * x + 2 * y + c
        src_of = lambda w, idx: srcs[w].at[idx] if self.scatter[w] else srcs[w]
        local = [pltpu.make_async_copy(src_of(w, my), dsts[w].at[my], local_sems.at[w]) for w in range(self.n)]
        sends, recvs = [], []
        for w in range(self.n):
            for r in range(1, N_DEV):
                px = 1 - x if r & 4 else x
                py = 1 - y if r & 2 else y
                pc = 1 - c if r & 1 else c
                pidx = 4 * px + 2 * py + pc
                kw = dict(send_sem=send_sems.at[w * 7 + r - 1], recv_sem=recv_sems.at[w * 7 + r - 1],
                          device_id=(px, py, pc), device_id_type=MESH)
                sends.append(pltpu.make_async_remote_copy(src_ref=src_of(w, pidx), dst_ref=dsts[w].at[my], **kw))
                recvs.append(pltpu.make_async_remote_copy(src_ref=src_of(w, pidx), dst_ref=dsts[w].at[pidx], **kw))
        return local, sends, recvs

    def start(self, srcs, dsts, sems):
        local, sends, _ = self._copies(srcs, dsts, sems)
        for cp in local + sends:
            cp.start()

    def wait(self, srcs, dsts, sems):
        local, sends, recvs = self._copies(srcs, dsts, sems)
        for cp in recvs:
            cp.wait_recv()
        for cp in sends:
            cp.wait_send()
        for cp in local:
            cp.wait()

    def split(self, refs, n_in, n_out):
        srcs = refs[n_in:n_in + self.n]
        dsts = refs[n_in + self.n + n_out:n_in + 2 * self.n + n_out]
        return srcs, dsts, refs[len(refs) - 3:]


class _TwoLevelGather(_Exchange):
    def __init__(self, arrays):
        super().__init__(arrays, [False] * len(arrays))

    def _plan(self, srcs, dsts, sems):
        send_sems, recv_sems, local_sems = sems
        x, y, c = _mesh_pos()
        me, sibling = (x, y, c), (x, y, 1 - c)
        chips = [(1 - x, y), (x, 1 - y), (1 - x, 1 - y)]
        region = lambda w, dev: dsts[w].at[4 * dev[0] + 2 * dev[1] + dev[2]]

        def copy(w, kk, block, to, src=None):
            return pltpu.make_async_remote_copy(
                src_ref=region(w, block) if src is None else src, dst_ref=region(w, block),
                send_sem=send_sems.at[w * 7 + kk], recv_sem=recv_sems.at[w * 7 + kk], device_id=to, device_id_type=MESH)

        mine = [pltpu.make_async_copy(srcs[w], region(w, me), local_sems.at[w]) for w in range(self.n)]
        first = []
        for w in range(self.n):
            first.append(copy(w, 0, me, sibling, src=srcs[w]))
            first += [copy(w, 1 + j, me, (*chip, c), src=srcs[w]) for j, chip in enumerate(chips)]
        return me, sibling, chips, c, copy, mine, first

    def start(self, srcs, dsts, sems):
        *_, mine, first = self._plan(srcs, dsts, sems)
        for cp in mine + first:
            cp.start()

    def wait(self, srcs, dsts, sems):
        me, sibling, chips, c, copy, mine, first = self._plan(srcs, dsts, sems)
        passed = []
        for j, chip in enumerate(chips):
            for w in range(self.n):
                copy(w, 1 + j, (*chip, c), me).wait_recv()
                cp = copy(w, 4 + j, (*chip, c), sibling)
                cp.start()
                passed.append(cp)
        for w in range(self.n):
            copy(w, 0, sibling, me).wait_recv()
            for j, chip in enumerate(chips):
                copy(w, 4 + j, (*chip, 1 - c), me).wait_recv()
        for cp in first + passed:
            cp.wait_send()
        for cp in mine:
            cp.wait()


def _run_exchange(ex, name):
    def body(*refs):
        parts = ex.split(refs, 0, 0)
        ex.start(*parts)
        ex.wait(*parts)

    return pl.pallas_call(body, name=name, in_specs=ex.specs, out_specs=ex.specs, out_shape=ex.out_shape,
                          scratch_shapes=ex.scratch)(*ex.arrays)


def _pick_rows(r, c, target_bytes):
    t = r
    while (t // 2) % 16 == 0 and t // 2 >= 16 and t * c * 4 > target_bytes:
        t //= 2
    return t


def _sum_slots(slots, name):
    ns, r, c = slots.shape
    tr = _pick_rows(r, c, 256 * 1024)

    def body(s_ref, o_ref):
        acc = s_ref[0].astype(F32)
        for kk in range(1, ns):
            acc = acc + s_ref[kk].astype(F32)
        o_ref[...] = acc

    return pl.pallas_call(
        body, name=name, grid=(r // tr,),
        in_specs=[pl.BlockSpec((ns, tr, c), lambda i: (0, i, 0))], out_specs=pl.BlockSpec((tr, c), lambda i: (i, 0)),
        out_shape=jax.ShapeDtypeStruct((r, c), F32), compiler_params=_cparams(("parallel",)),
    )(slots)


def _adamw(slots, w, m, v, name):
    ns, r, c = slots.shape
    tr = _pick_rows(r, c, 256 * 1024)

    def body(s_ref, w_ref, m_ref, v_ref, g_out, d_out, m_out, v_out):
        g = s_ref[0].astype(F32)
        for kk in range(1, ns):
            g = g + s_ref[kk].astype(F32)
        mn = ADAM_B1 * m_ref[...] + (1.0 - ADAM_B1) * g
        vn = ADAM_B2 * v_ref[...] + (1.0 - ADAM_B2) * (g * g)
        m_hat = mn / (1.0 - ADAM_B1 ** ADAM_STEP)
        v_hat = vn / (1.0 - ADAM_B2 ** ADAM_STEP)
        g_out[...] = g
        d_out[...] = -ADAM_LR * (m_hat / (jnp.sqrt(v_hat) + ADAM_EPS) + ADAM_WD * w_ref[...])
        m_out[...] = mn
        v_out[...] = vn

    blk = pl.BlockSpec((tr, c), lambda i: (i, 0))
    return pl.pallas_call(
        body, name=name, grid=(r // tr,),
        in_specs=[pl.BlockSpec((ns, tr, c), lambda i: (0, i, 0)), blk, blk, blk], out_specs=[blk] * 4,
        out_shape=[jax.ShapeDtypeStruct((r, c), F32)] * 4, compiler_params=_cparams(("parallel",)),
    )(slots, w, m, v)


def _rotary_tables(pos, s):
    posf = pos.astype(F32)
    inv_freq = 1.0 / (10000.0 ** jnp.linspace(0.0, 1.0, RET_QK // 2, dtype=F32))
    ang = posf[:, None] * inv_freq
    tabs = {"cos_r": jnp.cos(ang), "sin_r": jnp.sin(ang), "dil_cs": []}
    freqs = 500000.0 ** (-jnp.arange(0, 16, 2, dtype=F32) / 16)
    spread = np.zeros((16, 384), np.float32)
    bias = np.zeros((1, 384), np.float32)
    for head in range(2):
        for i in range(8):
            spread[i, 64 * head + i] = spread[i, 64 * head + 8 + i] = 1.0
            spread[8 + i, 128 + 64 * head + i] = -1.0
            spread[8 + i, 256 + 64 * head + 8 + i] = 1.0
        bias[0, 64 * head + 16:64 * head + 64] = 1.0

    for dil in DIL_GROUPS:
        ang = posf.reshape(s // dil, dil).T.reshape(s, 1) * freqs
        tabs["dil_cs"].append(jnp.concatenate([jnp.cos(ang), jnp.sin(ang)], axis=1))
    tabs["spread"], tabs["bias"] = jnp.asarray(spread, BF16), jnp.asarray(bias)
    return tabs


def _spread_rotary(t, e, b):
    hi = t.astype(BF16)
    lo = (t - hi.astype(F32)).astype(BF16)
    out = _dot(hi, e, NN) + _dot(lo, e, NN) + b
    return out[:, 0:128], out[:, 128:256], out[:, 256:384]


_TRANSPOSED = ("w_in", "w_dil_out", "w_up", "w_ple_in")
_MATS = ("w_in", "w_ret_out", "w_dil_out", "w_o", "w_up", "w_down", "w_ple_gate", "w_ple_in")
_VECS = ("g_pre_mix", "g_post_mix", "g_pre_mlp", "g_post_mlp", "g_pre_ple", "b_ple_gate", "g_post_ple")
_ORDER = ("w_in", "b_gate", "w_ret_out", "w_dil_out", "w_o", "g_pre_mix", "g_post_mix", "g_pre_mlp", "g_post_mlp", "w_up",
          "w_down", "g_pre_ple", "w_ple_gate", "b_ple_gate", "w_ple_in", "g_post_ple")


def kernel(x, p, positions, w_in, b_gate, w_ret_out, w_dil_out, w_o, g_pre_mix, g_post_mix, g_pre_mlp, g_post_mlp, w_up, w_down, g_pre_ple, w_ple_gate, b_ple_gate, w_ple_in, g_post_ple, loss_target, m_w_in, m_b_gate, m_w_ret_out, m_w_dil_out, m_w_o, m_g_pre_mix, m_g_post_mix, m_g_pre_mlp, m_g_post_mlp, m_w_up, m_w_down, m_g_pre_ple, m_w_ple_gate, m_b_ple_gate, m_w_ple_in, m_g_post_ple, v_w_in, v_b_gate, v_w_ret_out, v_w_dil_out, v_w_o, v_g_pre_mix, v_g_post_mix, v_g_pre_mlp, v_g_post_mlp, v_w_up, v_w_down, v_g_pre_ple, v_w_ple_gate, v_b_ple_gate, v_w_ple_in, v_g_post_ple):
    s = x.shape[1]
    wd = dict(w_in=w_in, b_gate=b_gate, w_ret_out=w_ret_out, w_dil_out=w_dil_out, w_o=w_o, g_pre_mix=g_pre_mix,
              g_post_mix=g_post_mix, g_pre_mlp=g_pre_mlp, g_post_mlp=g_post_mlp, w_up=w_up, w_down=w_down,
              g_pre_ple=g_pre_ple, w_ple_gate=w_ple_gate, b_ple_gate=b_ple_gate, w_ple_in=w_ple_in, g_post_ple=g_post_ple)
    md = dict(w_in=m_w_in, b_gate=m_b_gate, w_ret_out=m_w_ret_out, w_dil_out=m_w_dil_out, w_o=m_w_o, g_pre_mix=m_g_pre_mix,
              g_post_mix=m_g_post_mix, g_pre_mlp=m_g_pre_mlp, g_post_mlp=m_g_post_mlp, w_up=m_w_up, w_down=m_w_down,
              g_pre_ple=m_g_pre_ple, w_ple_gate=m_w_ple_gate, b_ple_gate=m_b_ple_gate, w_ple_in=m_w_ple_in, g_post_ple=m_g_post_ple)
    vd = dict(w_in=v_w_in, b_gate=v_b_gate, w_ret_out=v_w_ret_out, w_dil_out=v_w_dil_out, w_o=v_w_o, g_pre_mix=v_g_pre_mix,
              g_post_mix=v_g_post_mix, g_pre_mlp=v_g_pre_mlp, g_post_mlp=v_g_post_mlp, w_up=v_w_up, w_down=v_w_down,
              g_pre_ple=v_g_pre_ple, w_ple_gate=v_w_ple_gate, b_ple_gate=v_b_ple_gate, w_ple_in=v_w_ple_in, g_post_ple=v_g_post_ple)

    shards = {n: (wd[n][0].T if n in _TRANSPOSED else wd[n][0]).astype(BF16) for n in _MATS}
    shards["b_gate"] = b_gate[0]
    vec = {n: wd[n] for n in _VECS}
    vec["b_ple"] = b_ple_gate

    tabs = _rotary_tables(positions[0], s)
    grad_x, slots, packet = _local_step(x[0], p[0, 0].astype(BF16), loss_target[0], tabs, {}, vec, s, shards=shards)

    (packets,) = _run_exchange(_Exchange([packet], [False]), "exchange_vectors")
    out = {}
    for n in _MATS:
        sl = slots[n]
        if n in _TRANSPOSED:
            sl = _sum_slots(sl, "sum_" + n).T[None]
        out[n] = _adamw(sl, wd[n][0], md[n][0], vd[n][0], "adamw_" + n)
    zero_rows = jnp.zeros((16 - len(_VECS), D_MODEL), F32)
    pack = lambda d: jnp.concatenate([d[n] for n in _VECS] + [zero_rows], axis=0)
    small = _adamw(packets, pack(wd), pack(md), pack(vd), "adamw_vectors")
    for i, n in enumerate(_VECS):
        out[n] = tuple(t[i:i + 1] for t in small)
    my = 4 * lax.axis_index("x") + 2 * lax.axis_index("y") + lax.axis_index("c")
    g_bias = lax.dynamic_slice(small[0], (8, my * 128), (2, 128))
    out["b_gate"] = _adamw(g_bias[None], b_gate[0], m_b_gate[0], v_b_gate[0], "adamw_b_gate")
    loss = small[0][7, 0]

    res = [loss, grad_x[None]]
    for kk in range(4):
        res += [out[n][kk][None] if out[n][kk].ndim == 2 and wd[n].ndim == 3 else out[n][kk] for n in _ORDER]
    return tuple(res)
```

```python
import functools
import math

import numpy as np
import jax
import jax.numpy as jnp
from jax import lax
from jax.experimental import pallas as pl
from jax.experimental.pallas import tpu as pltpu

F32, BF16 = jnp.float32, jnp.bfloat16
D_MODEL = 1024
EPS = 1e-6
N_DEV = 8
RET_HEADS, RET_QK, RET_V, RET_CHUNK = 4, 256, 512, 128
DIL_GROUPS = (1, 4, 16)
DIL_W = 512
QB = 128
NEG = -1e30
ADAM_LR, ADAM_B1, ADAM_B2, ADAM_EPS, ADAM_WD, ADAM_STEP = 0.001, 0.9, 0.999, 1e-08, 0.01, 10
VMEM_LIMIT_BYTES = 56 * 1024 * 1024
MESH = pl.DeviceIdType.MESH

NN = ((1,), (0,))
NT = ((1,), (1,))
TN = ((0,), (0,))


def _dot(a, b, dn):
    return lax.dot_general(a, b, (dn, ((), ())), preferred_element_type=F32)


def _cparams(sem):
    return pltpu.CompilerParams(dimension_semantics=sem, vmem_limit_bytes=VMEM_LIMIT_BYTES)


def _rms(x):
    return x * lax.rsqrt(jnp.mean(x * x, axis=-1, keepdims=True) + EPS)


def _rms_bwd(x, g, dy):
    r = lax.rsqrt(jnp.mean(x * x, axis=-1, keepdims=True) + EPS)
    xh = x * r
    t = dy * g
    dx = r * (t - xh * jnp.mean(t * xh, axis=-1, keepdims=True))
    return dx, dy * xh


def _colsum(v):
    return jnp.sum(v, axis=0, keepdims=True)


def _sigmoid(v):
    return 1.0 / (1.0 + jnp.exp(-v))


def _pallas(compute, *, name, grid, in_specs, out_specs, out_shape, scratch, semantics, args, carry=None):
    n_in, n_out = len(in_specs), len(out_specs)
    if carry is None:
        res = pl.pallas_call(compute, name=name, grid=grid, in_specs=in_specs, out_specs=out_specs, out_shape=out_shape,
                             scratch_shapes=scratch, compiler_params=_cparams(semantics))(*args)
        return res, []
    n_steps = math.prod(grid)

    def body(*refs):
        step = 0
        for axis, size in enumerate(grid):
            step = step * size + pl.program_id(axis)
        parts = carry.split(refs, n_in, n_out)
        pl.when(step == 0)(lambda: carry.start(*parts))
        compute(*refs[:n_in], *refs[n_in + carry.n:n_in + carry.n + n_out], *refs[n_in + 2 * carry.n + n_out:len(refs) - 3])
        pl.when(step == n_steps - 1)(lambda: carry.wait(*parts))

    res = pl.pallas_call(
        body, name=name, grid=grid, in_specs=list(in_specs) + carry.specs, out_specs=list(out_specs) + carry.specs,
        out_shape=list(out_shape) + carry.out_shape, scratch_shapes=list(scratch) + carry.scratch,
        compiler_params=_cparams(("arbitrary",) * len(grid)))(*args, *carry.arrays)
    return res[:n_out], res[n_out:]


def _matmul(a, b, *, mode, m, n, k, tm, tn, tk, out_dtype, name, a_fn=None, epi=(), epi_width=None, epi_fn=None, carry=None):
    nk = k // tk
    grid = (m // tm, n // tn, nk)
    if mode == "nn":
        a_blk, a_im, b_blk, b_im, dn = (tm, tk), (lambda i, j, kk: (i, kk)), (tk, tn), (lambda i, j, kk: (kk, j)), NN
    elif mode == "nt":
        a_blk, a_im, b_blk, b_im, dn = (tm, tk), (lambda i, j, kk: (i, kk)), (tn, tk), (lambda i, j, kk: (j, kk)), NT
    else:
        a_blk, a_im, b_blk, b_im, dn = (tk, tm), (lambda i, j, kk: (kk, i)), (tk, tn), (lambda i, j, kk: (kk, j)), TN
    o_im = lambda i, j, kk: (i, j)
    n_in = 2 + len(epi)

    def body(*refs):
        a_ref, b_ref = refs[0], refs[1]
        o_ref = refs[n_in]
        acc_ref = refs[n_in + 1] if nk > 1 else None

        def finish(acc):
            if epi:
                acc = epi_fn(acc, *[r[...] for r in refs[2:n_in]])
            o_ref[...] = acc.astype(o_ref.dtype)

        av = a_ref[...]
        if a_fn is not None:
            av = a_fn(av)
        part = _dot(av, b_ref[...], dn)
        if nk == 1:
            finish(part)
        else:
            kk = pl.program_id(2)

            @pl.when(kk == 0)
            def _():
                acc_ref[...] = part

            @pl.when(kk > 0)
            def _():
                acc_ref[...] += part

            @pl.when(kk == nk - 1)
            def _():
                finish(acc_ref[...])

    epi_spec = pl.BlockSpec((tm, tn), o_im) if epi_width is None else pl.BlockSpec((tm, epi_width), lambda i, j, kk: (i, 0))
    in_specs = [pl.BlockSpec(a_blk, a_im), pl.BlockSpec(b_blk, b_im)] + [epi_spec] * len(epi)
    args = [a, b, *epi]
    (out,), got = _pallas(
        body, name=name, grid=grid, in_specs=in_specs, out_specs=[pl.BlockSpec((tm, tn), o_im)],
        out_shape=[jax.ShapeDtypeStruct((m, n), out_dtype)], scratch=[pltpu.VMEM((tm, tn), F32)] if nk > 1 else [],
        semantics=("parallel", "parallel", "arbitrary"), args=args, carry=carry)
    return out if carry is None else (out, got)


def _relu_sq(v):
    r = jnp.maximum(v, jnp.zeros_like(v))
    return r * r


def _rowwise(name, fn, s, tr, rows, vecs, outs, accs=()):
    n_r, n_v, n_o, n_a = len(rows), len(vecs), len(outs), len(accs)

    def body(*refs):
        vals = [refs[i][...].astype(F32) for i in range(n_r)] + [refs[n_r + i][...] for i in range(n_v)]
        o_refs = refs[n_r + n_v:n_r + n_v + n_o]
        a_refs = refs[n_r + n_v + n_o:]
        o_vals, a_vals = fn(*vals)
        for ref, val in zip(o_refs, o_vals):
            ref[...] = val.astype(ref.dtype)
        if n_a:
            @pl.when(pl.program_id(0) == 0)
            def _():
                for ref in a_refs:
                    ref[...] = jnp.zeros_like(ref)

            for ref, val in zip(a_refs, a_vals):
                ref[...] += val

    in_specs = [pl.BlockSpec((tr, w), functools.partial(lambda i, cb: (i, cb), cb=cb)) for _, w, cb in rows]
    in_specs += [pl.BlockSpec(v.shape, lambda i: (0, 0)) for v in vecs]
    out_specs = [pl.BlockSpec((tr, w), lambda i: (i, 0)) for w, _ in outs]
    out_specs += [pl.BlockSpec((1, w), lambda i: (0, 0)) for w in accs]
    out_shape = [jax.ShapeDtypeStruct((s, w), dt) for w, dt in outs]
    out_shape += [jax.ShapeDtypeStruct((1, w), F32) for w in accs]
    res = pl.pallas_call(
        body, name=name, grid=(s // tr,), in_specs=in_specs, out_specs=out_specs, out_shape=out_shape,
        compiler_params=_cparams(("arbitrary",)),
    )(*[r[0] for r in rows], *vecs)
    return res[:n_o], res[n_o:]


_ROW_TILE = 512
_STREAM_SPECS = [pl.BlockSpec((dil, _ROW_TILE // dil, D_MODEL), lambda i: (0, i, 0)) for dil in DIL_GROUPS[1:]]
_NAT_SPEC = pl.BlockSpec((_ROW_TILE, D_MODEL), lambda i: (i, 0))
_VEC_SPEC = pl.BlockSpec((1, D_MODEL), lambda i: (0, 0))
_COL_BLOCKS = pltpu.VMEM((D_MODEL // 128, _ROW_TILE, 128), F32)


def _prenorm(xs, g, tabs, s, carry=None):
    tr = _ROW_TILE
    n_g = len(DIL_GROUPS)

    def body(x_ref, g_ref, *rest):
        cs_refs, (e_ref, b_ref), (u_ref, u4_ref, u16_ref) = rest[:n_g], rest[n_g:n_g + 2], rest[n_g + 2:n_g + 5]
        tab_refs, buf = rest[n_g + 5:n_g + 5 + 3 * n_g], rest[-1]
        xn = _rms(x_ref[...]) * g_ref[...]
        u_ref[...] = xn.astype(BF16)
        for cb in range(8):
            buf[cb] = xn[:, cb * 128:(cb + 1) * 128]
        for dil, out in ((4, u4_ref), (16, u16_ref)):
            for c in range(dil):
                rows = pl.ds(c, tr // dil, stride=dil)
                out[c] = jnp.concatenate([buf.at[cb][rows, :] for cb in range(8)], axis=1).astype(BF16)
        for gi in range(n_g):
            for ref, val in zip(tab_refs[3 * gi:3 * gi + 3], _spread_rotary(cs_refs[gi][...], e_ref[...], b_ref[...])):
                ref[...] = val

    row = lambda w: pl.BlockSpec((tr, w), lambda i: (i, 0))
    whole = lambda a: pl.BlockSpec(a.shape, lambda i: (0, 0))
    res, got = _pallas(
        body, name="prenorm", grid=(s // tr,),
        in_specs=[_NAT_SPEC, _VEC_SPEC] + [row(16)] * n_g + [whole(tabs["spread"]), whole(tabs["bias"])],
        out_specs=[_NAT_SPEC] + _STREAM_SPECS + [row(128)] * (3 * n_g),
        out_shape=[jax.ShapeDtypeStruct((s, D_MODEL), BF16)]
        + [jax.ShapeDtypeStruct((dil, s // dil, D_MODEL), BF16) for dil in DIL_GROUPS[1:]]
        + [jax.ShapeDtypeStruct((s, 128), F32)] * (3 * n_g),
        scratch=[_COL_BLOCKS], semantics=("parallel",), args=(xs, g, *tabs["dil_cs"], tabs["spread"], tabs["bias"]), carry=carry)
    return [r.reshape(s, D_MODEL) for r in res[:3]], [tuple(res[3 + 3 * gi:6 + 3 * gi]) for gi in range(n_g)], got


def _grad_x(xs, d_h1, du_nat, du4, du16, g, s):
    tr = _ROW_TILE

    def body(x_ref, dh_ref, a_ref, b_ref, c_ref, u4_ref, u16_ref, g_ref, dx_ref, dg_ref, buf):
        du = a_ref[...].astype(F32) + b_ref[...].astype(F32) + c_ref[...].astype(F32)
        for dil, src in ((4, u4_ref), (16, u16_ref)):
            for c in range(dil):
                part = src[c].astype(F32)
                for cb in range(8):
                    buf.at[cb][pl.ds(c, tr // dil, stride=dil), :] = part[:, cb * 128:(cb + 1) * 128]
            du = du + jnp.concatenate([buf[cb] for cb in range(8)], axis=1)
        dx, dgr = _rms_bwd(x_ref[...], g_ref[...], du)
        dx_ref[...] = dh_ref[...] + dx

        @pl.when(pl.program_id(0) == 0)
        def _():
            dg_ref[...] = jnp.zeros_like(dg_ref)

        dg_ref[...] += _colsum(dgr)

    return pl.pallas_call(
        body, name="grad_x", grid=(s // tr,), in_specs=[_NAT_SPEC] * 5 + _STREAM_SPECS + [_VEC_SPEC],
        out_specs=[_NAT_SPEC, _VEC_SPEC],
        out_shape=[jax.ShapeDtypeStruct((s, D_MODEL), F32), jax.ShapeDtypeStruct((1, D_MODEL), F32)],
        scratch_shapes=[_COL_BLOCKS], compiler_params=_cparams(("arbitrary",)),
    )(xs, d_h1, *du_nat, du4.reshape(4, s // 4, D_MODEL), du16.reshape(16, s // 16, D_MODEL), g)


def _ret_tables():
    h = np.arange(RET_HEADS, dtype=np.float32)
    lg = np.log1p(-(np.float32(2.0) ** (-5.0 - h))).astype(np.float32)
    idx = np.arange(RET_CHUNK, dtype=np.float32)
    diff = idx[:, None] - idx[None, :]
    dm = np.where(diff[None] >= 0, np.exp(np.maximum(diff, 0.0)[None] * lg[:, None, None]), 0.0)
    qd = np.exp((idx + 1.0)[None, :, None] * lg[:, None, None])
    kd = np.exp((RET_CHUNK - 1.0 - idx)[None, :, None] * lg[:, None, None])
    cd = np.exp(RET_CHUNK * lg)[:, None, None]
    return [jnp.asarray(t, F32) for t in (dm, qd, kd, cd)]


def _rope_half(v, cos, sin):
    v1, v2 = v[:, :128], v[:, 128:]
    return jnp.concatenate([v1 * cos - v2 * sin, v2 * cos + v1 * sin], axis=1)


def _unrope_half(d, cos, sin):
    d1, d2 = d[:, :128], d[:, 128:]
    return jnp.concatenate([d1 * cos + d2 * sin, d2 * cos - d1 * sin], axis=1)


_RET_HEADS_FWD, _RET_HEADS_BWD = 1, 2


def _ret_specs(rb, rev_n, hp):
    def rowmap(w_blk):
        return lambda h, n: (rev_n(n), w_blk(h))
    tab = [pl.BlockSpec((hp, RET_CHUNK, RET_CHUNK), lambda h, n: (h, 0, 0)),
           pl.BlockSpec((hp, RET_CHUNK, 1), lambda h, n: (h, 0, 0)),
           pl.BlockSpec((hp, RET_CHUNK, 1), lambda h, n: (h, 0, 0)),
           pl.BlockSpec((hp, 1, 1), lambda h, n: (h, 0, 0))]
    proj = pl.BlockSpec((rb, hp * 1536), rowmap(lambda h: h))
    cs = pl.BlockSpec((rb, 128), rowmap(lambda h: 0))
    hv = pl.BlockSpec((rb, hp * RET_V), rowmap(lambda h: h))
    return proj, cs, hv, tab


def _ret_fwd(proj_ret, cos, sin, s, carry=None):
    rb = min(512, s)
    ch = rb // RET_CHUNK
    nb = s // rb
    hp = _RET_HEADS_FWD
    proj_spec, cs_spec, hv_spec, tab_specs = _ret_specs(rb, lambda n: n, hp)

    def body(p_ref, cos_ref, sin_ref, dm_ref, qd_ref, kd_ref, cd_ref, yr_ref, y_ref, rs_ref, r_acc):
        @pl.when(pl.program_id(1) == 0)
        def _():
            r_acc[...] = jnp.zeros_like(r_acc)

        for c, hh in [(c, hh) for c in range(ch) for hh in range(hp)]:
            rows = slice(c * RET_CHUNK, (c + 1) * RET_CHUNK)
            pc, hc = hh * 1536, hh * RET_V
            dm, qd, kd, cd = dm_ref[hh], qd_ref[hh], kd_ref[hh], cd_ref[hh]
            cosv, sinv = cos_ref[rows, :], sin_ref[rows, :]
            q = _rope_half(p_ref[rows, pc:pc + 256].astype(F32), cosv, sinv)
            kk = _rope_half(p_ref[rows, pc + 256:pc + 512].astype(F32), cosv, sinv) * (RET_QK ** -0.5)
            v = p_ref[rows, pc + 512:pc + 1024]
            g = p_ref[rows, pc + 1024:pc + 1536].astype(F32)
            rb16 = r_acc[hh].astype(BF16)
            rs_ref[hh, c] = rb16
            sc = _dot(q.astype(BF16), kk.astype(BF16), NT) * dm
            y = _dot(sc.astype(BF16), v, NN) + _dot((q * qd).astype(BF16), rb16, NN)
            r_acc[hh] = r_acc[hh] * cd + _dot((kk * kd).astype(BF16), v, TN)
            y_ref[rows, hc:hc + RET_V] = y.astype(BF16)
            yr_ref[rows, hc:hc + RET_V] = (_rms(y) * (g * _sigmoid(g))).astype(BF16)

    return _pallas(
        body, name="ret_fwd", grid=(RET_HEADS // hp, nb),
        in_specs=[proj_spec, cs_spec, cs_spec] + tab_specs,
        out_specs=[hv_spec, hv_spec, pl.BlockSpec((hp, ch, RET_QK, RET_V), lambda h, n: (h, n, 0, 0))],
        out_shape=[jax.ShapeDtypeStruct((s, RET_HEADS * RET_V), BF16), jax.ShapeDtypeStruct((s, RET_HEADS * RET_V), BF16),
                   jax.ShapeDtypeStruct((RET_HEADS, s // RET_CHUNK, RET_QK, RET_V), BF16)],
        scratch=[pltpu.VMEM((hp, RET_QK, RET_V), F32)], semantics=("parallel", "arbitrary"),
        args=(proj_ret, cos, sin, *_ret_tables()), carry=carry)


def _ret_bwd(proj_ret, cos, sin, y, d_yr, rs, s, carry=None):
    rb = min(512, s)
    ch = rb // RET_CHUNK
    nb = s // rb
    hp = _RET_HEADS_BWD
    proj_spec, cs_spec, hv_spec, tab_specs = _ret_specs(rb, lambda n: nb - 1 - n, hp)

    def body(p_ref, cos_ref, sin_ref, y_ref, dyr_ref, rs_ref, dm_ref, qd_ref, kd_ref, cd_ref, o_ref, dr_acc):
        @pl.when(pl.program_id(1) == 0)
        def _():
            dr_acc[...] = jnp.zeros_like(dr_acc)

        for c, hh in [(c, hh) for c in reversed(range(ch)) for hh in range(hp)]:
            rows = slice(c * RET_CHUNK, (c + 1) * RET_CHUNK)
            pc, hc = hh * 1536, hh * RET_V
            dm, qd, kd, cd = dm_ref[hh], qd_ref[hh], kd_ref[hh], cd_ref[hh]
            cosv, sinv = cos_ref[rows, :], sin_ref[rows, :]
            q = _rope_half(p_ref[rows, pc:pc + 256].astype(F32), cosv, sinv)
            kk = _rope_half(p_ref[rows, pc + 256:pc + 512].astype(F32), cosv, sinv) * (RET_QK ** -0.5)
            v = p_ref[rows, pc + 512:pc + 1024]
            g = p_ref[rows, pc + 1024:pc + 1536].astype(F32)
            yv = y_ref[rows, hc:hc + RET_V].astype(F32)
            dyr = dyr_ref[rows, hc:hc + RET_V].astype(F32)
            sg = _sigmoid(g)
            r = lax.rsqrt(jnp.mean(yv * yv, axis=-1, keepdims=True) + EPS)
            yn = yv * r
            dg = dyr * yn * (sg * (1.0 + g * (1.0 - sg)))
            dyn = dyr * (g * sg)
            dy = (r * (dyn - yn * jnp.mean(dyn * yn, axis=-1, keepdims=True))).astype(BF16)
            qb, kb = q.astype(BF16), kk.astype(BF16)
            rb16 = rs_ref[hh, c]
            drb = dr_acc[hh].astype(BF16)
            sd = _dot(qb, kb, NT) * dm
            ds = (_dot(dy, v, NT) * dm).astype(BF16)
            dq = _dot(ds, kb, NN) + qd * _dot(dy, rb16, NT)
            dk = _dot(ds, qb, TN) + kd * _dot(v, drb, NT)
            dv = _dot(sd.astype(BF16), dy, TN) + _dot((kk * kd).astype(BF16), drb, NN)
            dr_acc[hh] = dr_acc[hh] * cd + _dot((q * qd).astype(BF16), dy, TN)
            o_ref[rows, pc:pc + 256] = _unrope_half(dq, cosv, sinv).astype(BF16)
            o_ref[rows, pc + 256:pc + 512] = (_unrope_half(dk, cosv, sinv) * (RET_QK ** -0.5)).astype(BF16)
            o_ref[rows, pc + 512:pc + 1024] = dv.astype(BF16)
            o_ref[rows, pc + 1024:pc + 1536] = dg.astype(BF16)

    in_specs = [proj_spec, cs_spec, cs_spec, hv_spec, hv_spec,
                pl.BlockSpec((hp, ch, RET_QK, RET_V), lambda h, n: (h, nb - 1 - n, 0, 0))] + tab_specs
    return _pallas(
        body, name="ret_bwd", grid=(RET_HEADS // hp, nb), in_specs=in_specs, out_specs=[proj_spec],
        out_shape=[jax.ShapeDtypeStruct((s, RET_HEADS * 1536), BF16)], scratch=[pltpu.VMEM((hp, RET_QK, RET_V), F32)],
        semantics=("parallel", "arbitrary"), args=(proj_ret, cos, sin, y, d_yr, rs, *_ret_tables()), carry=carry)


def _rope_qk(acc, c, s1, s2):
    outs = []
    for cc in range(8):
        vv = acc[:, cc * 128:(cc + 1) * 128]
        outs.append(vv * c + pltpu.roll(vv, 120, 1) * s1 + pltpu.roll(vv, 8, 1) * s2)
    return jnp.concatenate(outs + [acc[:, 2 * DIL_W:]], axis=1)


def _pair_masks(keys_on_rows=False):
    ri = lax.broadcasted_iota(jnp.int32, (2 * QB, 2 * QB), 1 if keys_on_rows else 0)
    ci = lax.broadcasted_iota(jnp.int32, (2 * QB, 2 * QB), 0 if keys_on_rows else 1)
    e = ci - (ri & (QB - 1))
    lane_lo = lax.broadcasted_iota(jnp.int32, (2 * QB, 128), 1) < 64
    return ci, jnp.logical_and(e >= 0, e <= QB), lane_lo


def _stack_heads(v, lane_lo):
    z = jnp.zeros_like(v)
    return jnp.concatenate([jnp.where(lane_lo, v, z), jnp.where(lane_lo, z, v)], axis=0)


def _dil_fwd(qkv, dil, s, name):
    length = s // dil
    rb = min(512, length)
    nsub = rb // QB
    nbs = length // rb
    sub_per = rb // QB

    def body(q_ref, k_ref, v_ref, kp_ref, vp_ref, o_ref, l_ref):
        first = (pl.program_id(0) % nbs) == 0
        ci, band, lane_lo = _pair_masks()
        lo1 = lane_lo[0:QB]

        for i in range(nsub):
            rows = slice(i * QB, (i + 1) * QB)
            mask = jnp.logical_and(band, ci >= jnp.where(first, QB, 0)) if i == 0 else band
            for j in range(4):
                lanes = slice(j * 128, (j + 1) * 128)
                q2 = _stack_heads(q_ref[rows, lanes], lo1)
                if i == 0:
                    k2 = jnp.concatenate([kp_ref[:, lanes], k_ref[rows, lanes]], axis=0)
                    v2 = jnp.concatenate([vp_ref[:, lanes], v_ref[rows, lanes]], axis=0)
                else:
                    k2, v2 = k_ref[(i - 1) * QB:(i + 1) * QB, lanes], v_ref[(i - 1) * QB:(i + 1) * QB, lanes]
                v2 = _stack_heads(v2, lane_lo)
                sc = jnp.where(mask, _dot(q2, k2, NT) * 0.125, NEG)
                m = jnp.max(sc, axis=1, keepdims=True)
                p = jnp.exp(sc - m)
                den = jnp.sum(p, axis=1, keepdims=True)
                pb = p.astype(BF16)
                o = _dot(jnp.concatenate([pb[0:QB], pb[QB:]], axis=1), v2, NN)
                inv = 1.0 / den
                lse = m + jnp.log(den)
                o_ref[rows, lanes] = (o * jnp.where(lo1, inv[0:QB], inv[QB:])).astype(BF16)
                l_ref[rows, lanes] = jnp.where(lo1, lse[0:QB], lse[QB:])

    prev = lambda n: jnp.maximum(n * sub_per - 1, 0)
    cur = lambda cb: (lambda n: (n, cb))
    return pl.pallas_call(
        body, name=name, grid=(s // rb,),
        in_specs=[pl.BlockSpec((rb, DIL_W), cur(0)), pl.BlockSpec((rb, DIL_W), cur(1)), pl.BlockSpec((rb, DIL_W), cur(2)),
                  pl.BlockSpec((QB, DIL_W), lambda n: (prev(n), 1)), pl.BlockSpec((QB, DIL_W), lambda n: (prev(n), 2))],
        out_specs=[pl.BlockSpec((rb, DIL_W), cur(0)), pl.BlockSpec((rb, DIL_W), cur(0))],
        out_shape=[jax.ShapeDtypeStruct((s, DIL_W), BF16), jax.ShapeDtypeStruct((s, DIL_W), F32)],
        compiler_params=_cparams(("parallel",)),
    )(qkv, qkv, qkv, qkv, qkv)


def _dil_bwd(qkv, dya, lse, dlt, tc, ts1, ts2, dil, s, name):
    length = s // dil
    rb = min(512, length)
    nsub = rb // QB
    nbs = length // rb
    last_blk = s // QB - 1

    def body(q_ref, k_ref, v_ref, kp_ref, vp_ref, qn_ref, dy_ref, dyn_ref, l_ref, ln_ref, d_ref, dn_ref,
             c_ref, s1_ref, s2_ref, o_ref, dka, dva):
        nl = pl.program_id(0) % nbs
        first, last = nl == 0, nl == nbs - 1
        ci, band, lane_lo = _pair_masks(keys_on_rows=True)
        lo1 = lane_lo[0:QB]

        def unrope(d, rows):
            return d * c_ref[rows, :] + pltpu.roll(d * s1_ref[rows, :], 8, 1) + pltpu.roll(d * s2_ref[rows, :], 120, 1)

        for qi in range(nsub + 1):
            nxt = qi == nsub
            rows = slice((nsub - 1) * QB, nsub * QB) if nxt else slice(qi * QB, (qi + 1) * QB)
            prev_rows = slice((qi - 1) * QB, qi * QB)
            if qi == 0:
                mask = jnp.logical_and(band, ci >= jnp.where(first, QB, 0))
            elif nxt:
                mask = jnp.logical_and(band, ci <= jnp.where(last, -1, QB - 1))[0:QB, :]
            else:
                mask = band
            for j in range(4):
                lanes = slice(j * 128, (j + 1) * 128)
                if nxt:
                    q, do, lv, dl = qn_ref[:, lanes], dyn_ref[:, lanes], ln_ref[:, lanes], dn_ref[:, lanes]
                    k2, v2 = k_ref[prev_rows, lanes], v_ref[prev_rows, lanes]
                else:
                    q, do, lv, dl = q_ref[rows, lanes], dy_ref[rows, lanes], l_ref[rows, lanes], d_ref[rows, lanes]
                    if qi == 0:
                        k2 = jnp.concatenate([kp_ref[:, lanes], k_ref[rows, lanes]], axis=0)
                        v2 = jnp.concatenate([vp_ref[:, lanes], v_ref[rows, lanes]], axis=0)
                    else:
                        k2, v2 = k_ref[(qi - 1) * QB:(qi + 1) * QB, lanes], v_ref[(qi - 1) * QB:(qi + 1) * QB, lanes]
                q2, do2 = _stack_heads(q, lo1), _stack_heads(do, lo1)
                lt, dt = lv.T, dl.T
                lse2 = jnp.concatenate([lt[0:1], lt[64:65]], axis=1)
                dl2 = jnp.concatenate([dt[0:1], dt[64:65]], axis=1)
                sc = _dot(k2, q2, NT) * 0.125
                p = jnp.where(mask, jnp.exp(jnp.minimum(sc - lse2, 0.0)), 0.0)
                ds = (p * (_dot(v2, do2, NT) - dl2) * 0.125).astype(BF16)
                dk2 = _dot(ds, q2, NN)
                dv2 = _dot(p.astype(BF16), do2, NN)
                if qi >= 1:
                    dka[prev_rows, lanes] += dk2[0:QB]
                    dva[prev_rows, lanes] += dv2[0:QB]
                if not nxt:
                    dka[rows, lanes] = dk2[QB:]
                    dva[rows, lanes] = dv2[QB:]
                    dq = _dot(jnp.concatenate([ds[:, 0:QB], ds[:, QB:]], axis=0), _stack_heads(k2, lane_lo), TN)
                    o_ref[rows, lanes] = unrope(dq, rows).astype(BF16)

        for cc in range(4):
            lanes = slice(cc * 128, (cc + 1) * 128)
            o_ref[:, 512 + cc * 128:512 + (cc + 1) * 128] = unrope(dka[:, lanes], slice(None)).astype(BF16)
            o_ref[:, 1024 + cc * 128:1024 + (cc + 1) * 128] = dva[:, lanes].astype(BF16)

    prev = lambda n: jnp.maximum(n * nsub - 1, 0)
    nxt = lambda n: jnp.minimum(n * nsub + nsub, last_blk)
    cur = lambda cb: (lambda n: (n, cb))
    big = lambda cb: pl.BlockSpec((rb, DIL_W), cur(cb))
    small = lambda im: pl.BlockSpec((QB, DIL_W), im)
    tab = pl.BlockSpec((rb, 128), cur(0))
    return pl.pallas_call(
        body, name=name, grid=(s // rb,),
        in_specs=[big(0), big(1), big(2), small(lambda n: (prev(n), 1)), small(lambda n: (prev(n), 2)),
                  small(lambda n: (nxt(n), 0)), big(0), small(lambda n: (nxt(n), 0)), big(0), small(lambda n: (nxt(n), 0)),
                  big(0), small(lambda n: (nxt(n), 0)), tab, tab, tab],
        out_specs=pl.BlockSpec((rb, 3 * DIL_W), cur(0)),
        out_shape=jax.ShapeDtypeStruct((s, 3 * DIL_W), BF16),
        scratch_shapes=[pltpu.VMEM((rb, DIL_W), F32), pltpu.VMEM((rb, DIL_W), F32)],
        compiler_params=_cparams(("parallel",)),
    )(qkv, qkv, qkv, qkv, qkv, qkv, dya, dya, lse, lse, dlt, dlt, tc, ts1, ts2)


def _stream_specs(tr):
    nat = pl.BlockSpec((tr, 128), lambda i, j: (i, j))
    return [nat] + [pl.BlockSpec((dil, tr // dil, 128), lambda i, j: (0, i, j)) for dil in DIL_GROUPS[1:]]


def _dil_merge(o_g, l_g, s):
    tr = min(2048, s)
    nat, sp4, sp16 = _stream_specs(tr)

    def body(o0_ref, l0_ref, o1_ref, l1_ref, o2_ref, l2_ref, ya_ref, lse_ref, o1n, l1n, o2n, l2n):
        for src, dst, dil in ((o1_ref, o1n, 4), (l1_ref, l1n, 4), (o2_ref, o2n, 16), (l2_ref, l2n, 16)):
            for c in range(dil):
                dst[pl.ds(c, tr // dil, stride=dil), :] = src[c].astype(F32)
        l0, l1, l2 = l0_ref[...], l1n[...], l2n[...]
        m = jnp.maximum(jnp.maximum(l0, l1), l2)
        e0, e1, e2 = jnp.exp(l0 - m), jnp.exp(l1 - m), jnp.exp(l2 - m)
        den = e0 + e1 + e2
        ya_ref[...] = ((e0 * o0_ref[...].astype(F32) + e1 * o1n[...] + e2 * o2n[...]) / den).astype(BF16)
        lse_ref[...] = m + jnp.log(den)

    v3 = lambda a, dil: a.reshape(dil, s // dil, DIL_W)
    return pl.pallas_call(
        body, name="dil_merge", grid=(s // tr, 4),
        in_specs=[nat, nat, sp4, sp4, sp16, sp16], out_specs=[nat, nat],
        out_shape=[jax.ShapeDtypeStruct((s, DIL_W), BF16), jax.ShapeDtypeStruct((s, DIL_W), F32)],
        scratch_shapes=[pltpu.VMEM((tr, 128), F32)] * 4,
        compiler_params=_cparams(("parallel", "parallel")),
    )(o_g[0], l_g[0], v3(o_g[1], 4), v3(l_g[1], 4), v3(o_g[2], 16), v3(l_g[2], 16))


def _dil_bwd_prep(d_ya, ya, lse, s):
    tr = min(2048, s)
    nat, sp4, sp16 = _stream_specs(tr)

    def body(dya_ref, ya_ref, lse_ref, dy0, dl0, dy1, ls1, dl1, dy2, ls2, dl2, dlt):
        lane_lo = lax.broadcasted_iota(jnp.int32, (tr, 128), 1) < 64
        prod = dya_ref[...] * ya_ref[...].astype(F32)
        lo = jnp.where(lane_lo, prod, 0.0)
        dlt[...] = jnp.where(lane_lo, jnp.sum(lo, axis=1, keepdims=True), jnp.sum(prod - lo, axis=1, keepdims=True))
        dy0[...] = dya_ref[...].astype(BF16)
        dl0[...] = dlt[...]
        for dil, dy, ls, dl in ((4, dy1, ls1, dl1), (16, dy2, ls2, dl2)):
            for c in range(dil):
                rows = pl.ds(c, tr // dil, stride=dil)
                dy[c] = dya_ref[rows, :].astype(BF16)
                ls[c] = lse_ref[rows, :]
                dl[c] = dlt[rows, :]

    sh = lambda dil, dt: jax.ShapeDtypeStruct((dil, s // dil, DIL_W), dt)
    res = pl.pallas_call(
        body, name="dil_bwd_prep", grid=(s // tr, 4),
        in_specs=[nat, nat, nat], out_specs=[nat, nat, sp4, sp4, sp4, sp16, sp16, sp16],
        out_shape=[jax.ShapeDtypeStruct((s, DIL_W), BF16), jax.ShapeDtypeStruct((s, DIL_W), F32),
                   sh(4, BF16), sh(4, F32), sh(4, F32), sh(16, BF16), sh(16, F32), sh(16, F32)],
        scratch_shapes=[pltpu.VMEM((tr, 128), F32)],
        compiler_params=_cparams(("parallel", "parallel")),
    )(d_ya, ya, lse)
    dy0, dl0, dy1, ls1, dl1, dy2, ls2, dl2 = [r.reshape(s, DIL_W) for r in res]
    return [(dy0, lse, dl0), (dy1, ls1, dl1), (dy2, ls2, dl2)]


_RET_SEGS = ((0, 256), (1024, 256), (2048, 512), (4096, 512))


def _split_w_in(win):
    per_head = [win[a:a + RET_HEADS * n].reshape(RET_HEADS, n, D_MODEL) for a, n in _RET_SEGS]
    w_ret = jnp.concatenate(per_head, axis=1).reshape(RET_HEADS * 1536, D_MODEL)
    w_dil = [jnp.concatenate([win[a + DIL_W * g:a + DIL_W * (g + 1)] for a in (6144, 7680, 9216)], axis=0) for g in range(3)]
    return w_ret, win[10752:12800], w_dil


def _join_w_in(g_ret, g_gate, g_dil):
    g_ret = g_ret.reshape(RET_HEADS, 1536, D_MODEL)
    off = (0, 256, 512, 1024, 1536)
    parts = [g_ret[:, off[i]:off[i + 1]].reshape(-1, D_MODEL) for i in range(4)]
    dil = [g_dil[g][DIL_W * i:DIL_W * (i + 1)] for i in range(3) for g in range(3)]
    return jnp.concatenate(parts + dil + [g_gate], axis=0)


def _local_step(xs, pb, tgt, tabs, wts, vec, s, shards=None):
    tm = min(2048, s)
    tr = min(512, s)
    mm = functools.partial(_matmul, tm=tm)
    on_mesh = shards is not None
    wts, vec = dict(wts), dict(vec)
    blocks = lambda g: g.reshape(N_DEV, g.shape[0] // N_DEV, g.shape[1])

    late_shards = dict(shards) if on_mesh else {}
    first = _TwoLevelGather([late_shards.pop("w_in"), late_shards.pop("b_gate")]) if on_mesh else None
    u, rot, gathered = _prenorm(xs, vec["g_pre_mix"], tabs, s, carry=first)
    if on_mesh:
        wts["w_in"] = gathered[0].reshape(N_DEV * gathered[0].shape[1], D_MODEL)
        bias = gathered[1].transpose(1, 0, 2).reshape(2, D_MODEL)
        vec.update(b0=bias[0:1], b1=bias[1:2])
    w_ret, w_gate, w_dil = _split_w_in(wts["w_in"])
    proj_ret = mm(u[0], w_ret, mode="nt", m=s, n=6144, k=1024, tn=2048, tk=1024, out_dtype=BF16, name="inproj_ret")
    proj_gate = mm(u[0], w_gate, mode="nt", m=s, n=2048, k=1024, tn=2048, tk=1024, out_dtype=BF16, name="inproj_gate")
    qkv = [_matmul(u[g], w_dil[g], mode="nt", m=s, n=1536, k=1024, tm=min(1024, s), tn=1536, tk=1024, out_dtype=BF16,
                   name="inproj_dil%d" % g, epi=rot[g], epi_width=128, epi_fn=_rope_qk) for g in range(3)]

    names = list(late_shards) if on_mesh else []
    gather = _TwoLevelGather([late_shards[n] for n in names]) if on_mesh else None
    (yr, y_ret, rstate), gathered = _ret_fwd(proj_ret, tabs["cos_r"], tabs["sin_r"], s, carry=gather)
    wts.update({n: g.reshape(N_DEV * g.shape[1], g.shape[2]) for n, g in zip(names, gathered)})
    a_br = mm(yr, wts["w_ret_out"], mode="nn", m=s, n=1024, k=2048, tn=1024, tk=2048, out_dtype=BF16, name="ret_out")

    o_g, l_g = [], []
    for g, dil in enumerate(DIL_GROUPS):
        o, l = _dil_fwd(qkv[g], dil, s, "dil_fwd%d" % g)
        o_g.append(o)
        l_g.append(l)
    ya, lse = _dil_merge(o_g, l_g, s)
    b_br = mm(ya, wts["w_dil_out"], mode="nt", m=s, n=1024, k=512, tn=1024, tk=512, out_dtype=BF16, name="dil_out")

    def gate_mix(a, b, gr, ga, b0, b1):
        return [_sigmoid(gr.astype(F32) + b0) * a.astype(F32) + _sigmoid(ga.astype(F32) + b1) * b.astype(F32)], []

    (mixed,), _ = _rowwise("gate_mix", gate_mix, s, tr, [(a_br, 1024, 0), (b_br, 1024, 0), (proj_gate, 1024, 0), (proj_gate, 1024, 1)],
                           [vec["b0"], vec["b1"]], [(1024, BF16)])
    z = mm(mixed, wts["w_o"], mode="nn", m=s, n=1024, k=1024, tn=1024, tk=1024, out_dtype=BF16, name="w_o")

    def post_norm(h, f, g_post, g_pre):
        hn = h + _rms(f) * g_post
        return [hn, _rms(hn) * g_pre], []

    (h1, v2), _ = _rowwise("post_mix", post_norm, s, tr, [(xs, 1024, 0), (z, 1024, 0)], [vec["g_post_mix"], vec["g_pre_mlp"]],
                           [(1024, F32), (1024, BF16)])
    a_up = mm(v2, wts["w_up"], mode="nt", m=s, n=4096, k=1024, tn=2048, tk=1024, out_dtype=BF16, name="mlp_up")
    f_dn = mm(a_up, wts["w_down"], mode="nn", m=s, n=1024, k=4096, tn=1024, tk=2048, out_dtype=BF16, name="mlp_down", a_fn=_relu_sq)
    (h2, t_ple), _ = _rowwise("post_mlp", post_norm, s, tr, [(h1, 1024, 0), (f_dn, 1024, 0)], [vec["g_post_mlp"], vec["g_pre_ple"]],
                              [(1024, F32), (1024, BF16)])
    gl = mm(t_ple, wts["w_ple_gate"], mode="nn", m=s, n=1024, k=1024, tn=1024, tk=1024, out_dtype=BF16, name="ple_gate")
    e_ple = mm(pb, wts["w_ple_in"], mode="nt", m=s, n=1024, k=256, tn=1024, tk=256, out_dtype=BF16, name="ple_in")

    def ple_loss(h, glv, e, tg, b, g):
        gate = _sigmoid(glv + b)
        ge = gate * e
        diff = h + _rms(ge) * g - tg
        dy = diff * (1.0 / D_MODEL)
        d_ge, dg = _rms_bwd(ge, g, dy)
        d_gl = d_ge * e * gate * (1.0 - gate)
        loss = jnp.zeros((1, D_MODEL), F32) + 0.5 * jnp.sum(diff * diff) * (1.0 / D_MODEL)
        return [dy, d_gl, d_ge * gate], [_colsum(dg), _colsum(d_gl), loss]

    (dy, d_gl, d_e), (dg_post_ple, db_ple, loss) = _rowwise(
        "ple_loss", ple_loss, s, tr, [(h2, 1024, 0), (gl, 1024, 0), (e_ple, 1024, 0), (tgt, 1024, 0)],
        [vec["b_ple"], vec["g_post_ple"]], [(1024, F32), (1024, BF16), (1024, BF16)], [1024, 1024, 1024])

    ts, ts2 = min(1024, s), min(2048, s)
    wg = functools.partial(_matmul, mode="tn", k=s, tk=ts, out_dtype=BF16)
    grads = {}
    grads["w_ple_in"] = wg(d_e, pb, m=1024, n=256, tm=1024, tn=256, tk=ts2, name="g_ple_in")
    grads["w_ple_gate"] = wg(t_ple, d_gl, m=1024, n=1024, tm=1024, tn=1024, tk=ts2, name="g_ple_gate")
    d_t = mm(d_gl, wts["w_ple_gate"], mode="nt", m=s, n=1024, k=1024, tn=1024, tk=1024, out_dtype=BF16, name="d_t")

    def bwd_ple_mlp(h, dt, dyv, f, g_pre, g_post):
        dx, dg1 = _rms_bwd(h, g_pre, dt)
        dh = dyv + dx
        df, dg2 = _rms_bwd(f, g_post, dh)
        return [dh, df], [_colsum(dg1), _colsum(dg2)]

    (d_h2, d_f), (dg_pre_ple, dg_post_mlp) = _rowwise(
        "bwd_ple_mlp", bwd_ple_mlp, s, tr, [(h2, 1024, 0), (d_t, 1024, 0), (dy, 1024, 0), (f_dn, 1024, 0)],
        [vec["g_pre_ple"], vec["g_post_mlp"]], [(1024, F32), (1024, BF16)], [1024, 1024])
    d_a = mm(d_f, wts["w_down"], mode="nt", m=s, n=4096, k=1024, tn=1024, tk=1024, out_dtype=BF16, name="d_a",
             epi=(a_up,), epi_fn=lambda acc, av: acc * (2.0 * jnp.maximum(av.astype(F32), 0.0)))
    grads["w_down"] = wg(a_up, d_f, m=4096, n=1024, tm=2048, tn=1024, tk=ts2, name="g_down", a_fn=_relu_sq)
    grads["w_up"] = wg(d_a, v2, m=4096, n=1024, tm=2048, tn=1024, tk=ts2, name="g_up")
    d_v2 = mm(d_a, wts["w_up"], mode="nn", m=s, n=1024, k=4096, tn=1024, tk=2048, out_dtype=BF16, name="d_v2")

    (d_h1, d_z), (dg_pre_mlp, dg_post_mix) = _rowwise(
        "bwd_mlp_mix", bwd_ple_mlp, s, tr, [(h1, 1024, 0), (d_v2, 1024, 0), (d_h2, 1024, 0), (z, 1024, 0)],
        [vec["g_pre_mlp"], vec["g_post_mix"]], [(1024, F32), (1024, BF16)], [1024, 1024])
    d_mixed = mm(d_z, wts["w_o"], mode="nt", m=s, n=1024, k=1024, tn=1024, tk=1024, out_dtype=BF16, name="d_mixed")
    grads["w_o"] = wg(mixed, d_z, m=1024, n=1024, tm=1024, tn=1024, tk=ts2, name="g_o")

    def bwd_gate(dm, a, b, gr, ga, b0, b1):
        sa, sb = _sigmoid(gr.astype(F32) + b0), _sigmoid(ga.astype(F32) + b1)
        dgr = dm * a.astype(F32) * sa * (1.0 - sa)
        dga = dm * b.astype(F32) * sb * (1.0 - sb)
        return [dm * sa, dm * sb, jnp.concatenate([dgr, dga], axis=1)], [_colsum(dgr), _colsum(dga)]

    (d_abr, d_bbr, dproj_gate), (db0, db1) = _rowwise(
        "bwd_gate", bwd_gate, s, tr, [(d_mixed, 1024, 0), (a_br, 1024, 0), (b_br, 1024, 0), (proj_gate, 1024, 0), (proj_gate, 1024, 1)],
        [vec["b0"], vec["b1"]], [(1024, BF16), (1024, BF16), (2048, BF16)], [1024, 1024])
    grads["w_ret_out"] = wg(yr, d_abr, m=2048, n=1024, tm=2048, tn=1024, tk=ts2, name="g_ret_out")
    d_yr = mm(d_abr, wts["w_ret_out"], mode="nt", m=s, n=2048, k=1024, tn=2048, tk=1024, out_dtype=BF16, name="d_yr")
    grads["w_dil_out"] = wg(d_bbr, ya, m=1024, n=512, tm=1024, tn=512, tk=ts2, name="g_dil_out")
    d_ya = mm(d_bbr, wts["w_dil_out"], mode="nn", m=s, n=512, k=1024, tn=512, tk=1024, out_dtype=F32, name="d_ya")

    slots = {}
    names = list(grads) if on_mesh else []
    shares = _Exchange([blocks(grads[n]) for n in names], [True] * len(names)) if on_mesh else None
    (dproj_ret,), got = _ret_bwd(proj_ret, tabs["cos_r"], tabs["sin_r"], y_ret, d_yr, rstate, s, carry=shares)
    slots.update(zip(names, got))
    upstream = _dil_bwd_prep(d_ya, ya, lse, s)
    dqkv = [_dil_bwd(qkv[g], *upstream[g], *rot[g], dil, s, "dil_bwd%d" % g)
            for g, dil in enumerate(DIL_GROUPS)]

    g_ret = wg(dproj_ret, u[0], m=6144, n=1024, tm=2048, tn=1024, tk=ts2, name="g_in_ret")
    g_gate = wg(dproj_gate, u[0], m=2048, n=1024, tm=2048, tn=1024, tk=ts2, name="g_in_gate")
    g_dil = [wg(dqkv[g], u[g], m=1536, n=1024, tm=1536, tn=1024, tk=ts2, name="g_in_dil%d" % g) for g in range(3)]
    grads["w_in"] = _join_w_in(g_ret, g_gate, g_dil)

    du_ret = functools.partial(mm, dproj_ret, w_ret, mode="nn", m=s, n=1024, k=6144, tm=min(512, s), tn=1024, tk=6144,
                               out_dtype=BF16, name="du_ret")
    if on_mesh:
        du_ret, (slots["w_in"],) = du_ret(carry=_Exchange([blocks(grads["w_in"])], [True]))
    else:
        du_ret = du_ret()
    du_gate = mm(dproj_gate, w_gate, mode="nn", m=s, n=1024, k=2048, tn=1024, tk=2048, out_dtype=BF16, name="du_gate")
    du_dil = [mm(dqkv[g], w_dil[g], mode="nn", m=s, n=1024, k=1536, tn=1024, tk=1536, out_dtype=BF16, name="du_dil%d" % g)
              for g in range(3)]

    grad_x, dg_pre_mix = _grad_x(xs, d_h1, (du_ret, du_gate, du_dil[0]), du_dil[1], du_dil[2], vec["g_pre_mix"], s)

    zero = jnp.zeros((1, D_MODEL), F32)
    packet = jnp.concatenate([dg_pre_mix, dg_post_mix, dg_pre_mlp, dg_post_mlp, dg_pre_ple, db_ple, dg_post_ple, loss,
                              db0, db1] + [zero] * 6, axis=0)
    return grad_x, (slots if on_mesh else grads), packet


def _mesh_pos():
    return lax.axis_index("x"), lax.axis_index("y"), lax.axis_index("c")


class _Exchange:
    def __init__(self, arrays, scatter):
        self.arrays, self.scatter, self.n = list(arrays), list(scatter), len(arrays)
        self.out_shape = [jax.ShapeDtypeStruct(a.shape if sc else (N_DEV,) + a.shape, a.dtype)
                          for a, sc in zip(self.arrays, self.scatter)]
        self.scratch = [pltpu.SemaphoreType.DMA((self.n * 7,)), pltpu.SemaphoreType.DMA((self.n * 7,)),
                        pltpu.SemaphoreType.DMA((self.n,))]
        self.specs = [pl.BlockSpec(memory_space=pl.ANY)] * self.n

    def _copies(self, srcs, dsts, sems):
        send_sems, recv_sems, local_sems = sems
        x, y, c = _mesh_pos()
        my = 4 * x + 2 * y + c
        src_of = lambda w, idx: srcs[w].at[idx] if self.scatter[w] else srcs[w]
        local = [pltpu.make_async_copy(src_of(w, my), dsts[w].at[my], local_sems.at[w]) for w in range(self.n)]
        sends, recvs = [], []
        for w in range(self.n):
            for r in range(1, N_DEV):
                px = 1 - x if r & 4 else x
                py = 1 - y if r & 2 else y
                pc = 1 - c if r & 1 else c
                pidx = 4 * px + 2 * py + pc
                kw = dict(send_sem=send_sems.at[w * 7 + r - 1], recv_sem=recv_sems.at[w * 7 + r - 1],
                          device_id=(px, py, pc), device_id_type=MESH)
                sends.append(pltpu.make_async_remote_copy(src_ref=src_of(w, pidx), dst_ref=dsts[w].at[my], **kw))
                recvs.append(pltpu.make_async_remote_copy(src_ref=src_of(w, pidx), dst_ref=dsts[w].at[pidx], **kw))
        return local, sends, recvs

    def start(self, srcs, dsts, sems):
        local, sends, _ = self._copies(srcs, dsts, sems)
        for cp in local + sends:
            cp.start()

    def wait(self, srcs, dsts, sems):
        local, sends, recvs = self._copies(srcs, dsts, sems)
        for cp in recvs:
            cp.wait_recv()
        for cp in sends:
            cp.wait_send()
        for cp in local:
            cp.wait()

    def split(self, refs, n_in, n_out):
        srcs = refs[n_in:n_in + self.n]
        dsts = refs[n_in + self.n + n_out:n_in + 2 * self.n + n_out]
        return srcs, dsts, refs[len(refs) - 3:]


class _TwoLevelGather(_Exchange):
    def __init__(self, arrays):
        super().__init__(arrays, [False] * len(arrays))

    def _plan(self, srcs, dsts, sems):
        send_sems, recv_sems, local_sems = sems
        x, y, c = _mesh_pos()
        me, sibling = (x, y, c), (x, y, 1 - c)
        chips = [(1 - x, y), (x, 1 - y), (1 - x, 1 - y)]
        region = lambda w, dev: dsts[w].at[4 * dev[0] + 2 * dev[1] + dev[2]]

        def copy(w, kk, block, to, src=None):
            return pltpu.make_async_remote_copy(
                src_ref=region(w, block) if src is None else src, dst_ref=region(w, block),
                send_sem=send_sems.at[w * 7 + kk], recv_sem=recv_sems.at[w * 7 + kk], device_id=to, device_id_type=MESH)

        mine = [pltpu.make_async_copy(srcs[w], region(w, me), local_sems.at[w]) for w in range(self.n)]
        first = []
        for w in range(self.n):
            first.append(copy(w, 0, me, sibling, src=srcs[w]))
            first += [copy(w, 1 + j, me, (*chip, c), src=srcs[w]) for j, chip in enumerate(chips)]
        return me, sibling, chips, c, copy, mine, first

    def start(self, srcs, dsts, sems):
        *_, mine, first = self._plan(srcs, dsts, sems)
        for cp in mine + first:
            cp.start()

    def wait(self, srcs, dsts, sems):
        me, sibling, chips, c, copy, mine, first = self._plan(srcs, dsts, sems)
        passed = []
        for j, chip in enumerate(chips):
            for w in range(self.n):
                copy(w, 1 + j, (*chip, c), me).wait_recv()
                cp = copy(w, 4 + j, (*chip, c), sibling)
                cp.start()
                passed.append(cp)
        for w in range(self.n):
            copy(w, 0, sibling, me).wait_recv()
            for j, chip in enumerate(chips):
                copy(w, 4 + j, (*chip, 1 - c), me).wait_recv()
        for cp in first + passed:
            cp.wait_send()
        for cp in mine:
            cp.wait()


def _run_exchange(ex, name):
    def body(*refs):
        parts = ex.split(refs, 0, 0)
        ex.start(*parts)
        ex.wait(*parts)

    return pl.pallas_call(body, name=name, in_specs=ex.specs, out_specs=ex.specs, out_shape=ex.out_shape,
                          scratch_shapes=ex.scratch)(*ex.arrays)


def _pick_rows(r, c, target_bytes):
    t = r
    while (t // 2) % 16 == 0 and t // 2 >= 16 and t * c * 4 > target_bytes:
        t //= 2
    return t


def _sum_slots(slots, name):
    ns, r, c = slots.shape
    tr = _pick_rows(r, c, 256 * 1024)

    def body(s_ref, o_ref):
        acc = s_ref[0].astype(F32)
        for kk in range(1, ns):
            acc = acc + s_ref[kk].astype(F32)
        o_ref[...] = acc

    return pl.pallas_call(
        body, name=name, grid=(r // tr,),
        in_specs=[pl.BlockSpec((ns, tr, c), lambda i: (0, i, 0))], out_specs=pl.BlockSpec((tr, c), lambda i: (i, 0)),
        out_shape=jax.ShapeDtypeStruct((r, c), F32), compiler_params=_cparams(("parallel",)),
    )(slots)


def _adamw(slots, w, m, v, name):
    ns, r, c = slots.shape
    tr = _pick_rows(r, c, 256 * 1024)

    def body(s_ref, w_ref, m_ref, v_ref, g_out, d_out, m_out, v_out):
        g = s_ref[0].astype(F32)
        for kk in range(1, ns):
            g = g + s_ref[kk].astype(F32)
        mn = ADAM_B1 * m_ref[...] + (1.0 - ADAM_B1) * g
        vn = ADAM_B2 * v_ref[...] + (1.0 - ADAM_B2) * (g * g)
        m_hat = mn / (1.0 - ADAM_B1 ** ADAM_STEP)
        v_hat = vn / (1.0 - ADAM_B2 ** ADAM_STEP)
        g_out[...] = g
        d_out[...] = -ADAM_LR * (m_hat / (jnp.sqrt(v_hat) + ADAM_EPS) + ADAM_WD * w_ref[...])
        m_out[...] = mn
        v_out[...] = vn

    blk = pl.BlockSpec((tr, c), lambda i: (i, 0))
    return pl.pallas_call(
        body, name=name, grid=(r // tr,),
        in_specs=[pl.BlockSpec((ns, tr, c), lambda i: (0, i, 0)), blk, blk, blk], out_specs=[blk] * 4,
        out_shape=[jax.ShapeDtypeStruct((r, c), F32)] * 4, compiler_params=_cparams(("parallel",)),
    )(slots, w, m, v)


def _rotary_tables(pos, s):
    posf = pos.astype(F32)
    inv_freq = 1.0 / (10000.0 ** jnp.linspace(0.0, 1.0, RET_QK // 2, dtype=F32))
    ang = posf[:, None] * inv_freq
    tabs = {"cos_r": jnp.cos(ang), "sin_r": jnp.sin(ang), "dil_cs": []}
    freqs = 500000.0 ** (-jnp.arange(0, 16, 2, dtype=F32) / 16)
    spread = np.zeros((16, 384), np.float32)
    bias = np.zeros((1, 384), np.float32)
    for head in range(2):
        for i in range(8):
            spread[i, 64 * head + i] = spread[i, 64 * head + 8 + i] = 1.0
            spread[8 + i, 128 + 64 * head + i] = -1.0
            spread[8 + i, 256 + 64 * head + 8 + i] = 1.0
        bias[0, 64 * head + 16:64 * head + 64] = 1.0

    for dil in DIL_GROUPS:
        ang = posf.reshape(s // dil, dil).T.reshape(s, 1) * freqs
        tabs["dil_cs"].append(jnp.concatenate([jnp.cos(ang), jnp.sin(ang)], axis=1))
    tabs["spread"], tabs["bias"] = jnp.asarray(spread, BF16), jnp.asarray(bias)
    return tabs


def _spread_rotary(t, e, b):
    hi = t.astype(BF16)
    lo = (t - hi.astype(F32)).astype(BF16)
    out = _dot(hi, e, NN) + _dot(lo, e, NN) + b
    return out[:, 0:128], out[:, 128:256], out[:, 256:384]


_TRANSPOSED = ("w_in", "w_dil_out", "w_up", "w_ple_in")
_MATS = ("w_in", "w_ret_out", "w_dil_out", "w_o", "w_up", "w_down", "w_ple_gate", "w_ple_in")
_VECS = ("g_pre_mix", "g_post_mix", "g_pre_mlp", "g_post_mlp", "g_pre_ple", "b_ple_gate", "g_post_ple")
_ORDER = ("w_in", "b_gate", "w_ret_out", "w_dil_out", "w_o", "g_pre_mix", "g_post_mix", "g_pre_mlp", "g_post_mlp", "w_up",
          "w_down", "g_pre_ple", "w_ple_gate", "b_ple_gate", "w_ple_in", "g_post_ple")


def kernel(x, p, positions, w_in, b_gate, w_ret_out, w_dil_out, w_o, g_pre_mix, g_post_mix, g_pre_mlp, g_post_mlp, w_up, w_down, g_pre_ple, w_ple_gate, b_ple_gate, w_ple_in, g_post_ple, loss_target, m_w_in, m_b_gate, m_w_ret_out, m_w_dil_out, m_w_o, m_g_pre_mix, m_g_post_mix, m_g_pre_mlp, m_g_post_mlp, m_w_up, m_w_down, m_g_pre_ple, m_w_ple_gate, m_b_ple_gate, m_w_ple_in, m_g_post_ple, v_w_in, v_b_gate, v_w_ret_out, v_w_dil_out, v_w_o, v_g_pre_mix, v_g_post_mix, v_g_pre_mlp, v_g_post_mlp, v_w_up, v_w_down, v_g_pre_ple, v_w_ple_gate, v_b_ple_gate, v_w_ple_in, v_g_post_ple):
    s = x.shape[1]
    wd = dict(w_in=w_in, b_gate=b_gate, w_ret_out=w_ret_out, w_dil_out=w_dil_out, w_o=w_o, g_pre_mix=g_pre_mix,
              g_post_mix=g_post_mix, g_pre_mlp=g_pre_mlp, g_post_mlp=g_post_mlp, w_up=w_up, w_down=w_down,
              g_pre_ple=g_pre_ple, w_ple_gate=w_ple_gate, b_ple_gate=b_ple_gate, w_ple_in=w_ple_in, g_post_ple=g_post_ple)
    md = dict(w_in=m_w_in, b_gate=m_b_gate, w_ret_out=m_w_ret_out, w_dil_out=m_w_dil_out, w_o=m_w_o, g_pre_mix=m_g_pre_mix,
              g_post_mix=m_g_post_mix, g_pre_mlp=m_g_pre_mlp, g_post_mlp=m_g_post_mlp, w_up=m_w_up, w_down=m_w_down,
              g_pre_ple=m_g_pre_ple, w_ple_gate=m_w_ple_gate, b_ple_gate=m_b_ple_gate, w_ple_in=m_w_ple_in, g_post_ple=m_g_post_ple)
    vd = dict(w_in=v_w_in, b_gate=v_b_gate, w_ret_out=v_w_ret_out, w_dil_out=v_w_dil_out, w_o=v_w_o, g_pre_mix=v_g_pre_mix,
              g_post_mix=v_g_post_mix, g_pre_mlp=v_g_pre_mlp, g_post_mlp=v_g_post_mlp, w_up=v_w_up, w_down=v_w_down,
              g_pre_ple=v_g_pre_ple, w_ple_gate=v_w_ple_gate, b_ple_gate=v_b_ple_gate, w_ple_in=v_w_ple_in, g_post_ple=v_g_post_ple)

    shards = {n: (wd[n][0].T if n in _TRANSPOSED else wd[n][0]).astype(BF16) for n in _MATS}
    shards["b_gate"] = b_gate[0]
    vec = {n: wd[n] for n in _VECS}
    vec["b_ple"] = b_ple_gate

    tabs = _rotary_tables(positions[0], s)
    grad_x, slots, packet = _local_step(x[0], p[0, 0].astype(BF16), loss_target[0], tabs, {}, vec, s, shards=shards)

    (packets,) = _run_exchange(_Exchange([packet], [False]), "exchange_vectors")
    out = {}
    for n in _MATS:
        sl = slots[n]
        if n in _TRANSPOSED:
            sl = _sum_slots(sl, "sum_" + n).T[None]
        out[n] = _adamw(sl, wd[n][0], md[n][0], vd[n][0], "adamw_" + n)
    zero_rows = jnp.zeros((16 - len(_VECS), D_MODEL), F32)
    pack = lambda d: jnp.concatenate([d[n] for n in _VECS] + [zero_rows], axis=0)
    small = _adamw(packets, pack(wd), pack(md), pack(vd), "adamw_vectors")
    for i, n in enumerate(_VECS):
        out[n] = tuple(t[i:i + 1] for t in small)
    my = 4 * lax.axis_index("x") + 2 * lax.axis_index("y") + lax.axis_index("c")
    g_bias = lax.dynamic_slice(small[0], (8, my * 128), (2, 128))
    out["b_gate"] = _adamw(g_bias[None], b_gate[0], m_b_gate[0], v_b_gate[0], "adamw_b_gate")
    loss = small[0][7, 0]

    res = [loss, grad_x[None]]
    for kk in range(4):
        res += [out[n][kk][None] if out[n][kk].ndim == 2 and wd[n].ndim == 3 else out[n][kk] for n in _ORDER]
    return tuple(res)
```
